```python
import math
import jax, jax.numpy as jnp
from jax import lax
import numpy as np

D_MODEL = 1024
BATCH = 8
SEQ = 4096
DEPTH = 1

CONV_WIDTH = D_MODEL // 2
CONV_KERNEL = 31
SSM_WIDTH = D_MODEL // 2
SSM_GROUP = 16
SSM_GROUPS = SSM_WIDTH // SSM_GROUP
SSM_STATE = 64
FFN_HIDDEN = ((8 * D_MODEL + 3 * 256 - 1) // (3 * 256)) * 256
IN_COLS = 2 * CONV_WIDTH + SSM_WIDTH + 2 * D_MODEL
N_MOD = 6
EPS = 1e-6
DT_MIN = 1e-3
DT_MAX = 1e-1

kernel_name = "hybrid_conv_s5_gated_block"


def rmsnorm(x, g):
    x32 = x.astype(jnp.float32)
    y = x32 * lax.rsqrt(jnp.mean(x32 * x32, axis=-1, keepdims=True) + EPS)
    return (y * g.astype(jnp.float32)).astype(x.dtype)


def layernorm(x, g, b):
    x32 = x.astype(jnp.float32)
    mu = jnp.mean(x32, axis=-1, keepdims=True)
    var = jnp.mean(jnp.square(x32 - mu), axis=-1, keepdims=True)
    y = (x32 - mu) * lax.rsqrt(var + EPS)
    return (y * g.astype(jnp.float32) + b.astype(jnp.float32)).astype(x.dtype)


def conformer_conv(u_glu, conv_w, conv_b, ln_g, ln_b, w_proj):
    a, gate = jnp.split(u_glu, 2, axis=-1)
    u = a * jax.nn.sigmoid(gate)
    y = lax.conv_general_dilated(
        u, conv_w[:, None, :].astype(u.dtype), window_strides=(1,),
        padding=[(CONV_KERNEL - 1, 0)],
        dimension_numbers=("NWC", "WIO", "NWC"),
        feature_group_count=CONV_WIDTH) + conv_b
    y = jax.nn.silu(layernorm(y, ln_g, ln_b))
    return y @ w_proj


def _scan_combine(e1, e2):
    a1r, a1i, b1r, b1i = e1
    a2r, a2i, b2r, b2i = e2
    ar = a2r * a1r - a2i * a1i
    ai = a2r * a1i + a2i * a1r
    br = a2r * b1r - a2i * b1i + b2r
    bi = a2r * b1i + a2i * b1r + b2i
    return (ar, ai, br, bi)


def s5_mixer(u, a_re, a_im, b_re, b_im, c_re, c_im, d, log_dt, w_glu):
    bsz, seq, _ = u.shape
    f32 = jnp.float32
    u32 = u.astype(f32).reshape(bsz, seq, SSM_GROUPS, SSM_GROUP)
    ar, ai = a_re.astype(f32), a_im.astype(f32)
    dt = jnp.exp(log_dt.astype(f32))[:, None]
    mag = jnp.exp(dt * ar)
    e_re, e_im = mag * jnp.cos(dt * ai), mag * jnp.sin(dt * ai)
    n_re, n_im = e_re - 1.0, e_im
    den = ar * ar + ai * ai
    q_re = (n_re * ar + n_im * ai) / den
    q_im = (n_im * ar - n_re * ai) / den
    br32, bi32 = b_re.astype(f32), b_im.astype(f32)
    bb_re = q_re[..., None] * br32 - q_im[..., None] * bi32
    bb_im = q_re[..., None] * bi32 + q_im[..., None] * br32
    bu_re = jnp.einsum("bsgh,gph->bsgp", u32, bb_re)
    bu_im = jnp.einsum("bsgh,gph->bsgp", u32, bb_im)
    abar_re = jnp.broadcast_to(e_re, bu_re.shape)
    abar_im = jnp.broadcast_to(e_im, bu_re.shape)
    _, _, x_re, x_im = lax.associative_scan(
        _scan_combine, (abar_re, abar_im, bu_re, bu_im), axis=1)
    y = (jnp.einsum("bsgp,ghp->bsgh", x_re, c_re.astype(f32))
         - jnp.einsum("bsgp,ghp->bsgh", x_im, c_im.astype(f32)))
    y = y.reshape(bsz, seq, SSM_WIDTH) + d.astype(f32) * u.astype(f32)
    y = jax.nn.gelu(y).astype(u.dtype)
    za, zb = jnp.split(y @ w_glu, 2, axis=-1)
    return za * jax.nn.sigmoid(zb)


def _fwd_setup_inputs(seed: int = 0) -> dict:
    key = jax.random.key(seed)
    ks = jax.random.split(key, 32)
    f32 = jnp.float32
    L, D, G, P, H = DEPTH, D_MODEL, SSM_GROUPS, SSM_STATE, SSM_GROUP

    def nrm(k, shape, fan_in):
        return jax.random.normal(k, shape, f32) * fan_in ** -0.5

    def gain(k, shape):
        return 1.0 + 0.05 * jax.random.normal(k, shape, f32)

    n_idx = jnp.arange(P, dtype=f32)
    a_re = -0.5 + 0.01 * jax.random.normal(ks[8], (L, G, P), f32)
    a_im = math.pi * n_idx[None, None, :] + 0.01 * jax.random.normal(ks[9], (L, G, P), f32)
    log_dt = jax.random.uniform(ks[10], (L, G), f32, math.log(DT_MIN), math.log(DT_MAX))
    return {
        "x": jax.random.normal(ks[0], (BATCH, SEQ, D), f32),
        "c": jax.random.normal(ks[1], (BATCH, D), f32),
        "w_ada": nrm(ks[2], (L, D, N_MOD * D), D) * 0.5,
        "b_ada": 0.02 * jax.random.normal(ks[3], (L, N_MOD * D), f32),
        "norm1_g": gain(ks[4], (L, D)),
        "w_in": nrm(ks[5], (L, D, IN_COLS), D),
        "conv_w": nrm(ks[6], (L, CONV_KERNEL, CONV_WIDTH), CONV_KERNEL),
        "conv_b": 0.02 * jax.random.normal(ks[7], (L, CONV_WIDTH), f32),
        "conv_ln_g": gain(ks[11], (L, CONV_WIDTH)),
        "conv_ln_b": 0.02 * jax.random.normal(ks[12], (L, CONV_WIDTH), f32),
        "conv_proj": nrm(ks[13], (L, CONV_WIDTH, D), CONV_WIDTH),
        "ssm_a_re": a_re,
        "ssm_a_im": a_im,
        "ssm_b_re": nrm(ks[14], (L, G, P, H), 2 * H),
        "ssm_b_im": nrm(ks[15], (L, G, P, H), 2 * H),
        "ssm_c_re": nrm(ks[16], (L, G, H, P), P),
        "ssm_c_im": nrm(ks[17], (L, G, H, P), P),
        "ssm_d": jax.random.normal(ks[18], (L, SSM_WIDTH), f32),
        "ssm_log_dt": log_dt,
        "ssm_glu": nrm(ks[19], (L, SSM_WIDTH, 2 * D), SSM_WIDTH),
        "w_out": nrm(ks[20], (L, D, D), D),
        "norm2_g": gain(ks[21], (L, D)),
        "w_ffn_in": nrm(ks[22], (L, D, 2 * FFN_HIDDEN), D),
        "w_ffn_out": nrm(ks[23], (L, FFN_HIDDEN, D), FFN_HIDDEN),
        "final_g": gain(ks[24], (D,)),
    }


def _fwd_reference(x, c, w_ada, b_ada, norm1_g, w_in, conv_w, conv_b, conv_ln_g, conv_ln_b,
              conv_proj, ssm_a_re, ssm_a_im, ssm_b_re, ssm_b_im, ssm_c_re, ssm_c_im,
              ssm_d, ssm_log_dt, ssm_glu, w_out, norm2_g, w_ffn_in, w_ffn_out, final_g):
    c_act = jax.nn.silu(c)
    split_pts = [2 * CONV_WIDTH, 2 * CONV_WIDTH + SSM_WIDTH, 2 * CONV_WIDTH + SSM_WIDTH + D_MODEL]
    for l in range(DEPTH):
        mod = (c_act @ w_ada[l] + b_ada[l])[:, None, :]
        sh1, sc1, g1, sh2, sc2, g2 = jnp.split(mod, N_MOD, axis=-1)

        h = rmsnorm(x, norm1_g[l]) * (1.0 + sc1) + sh1
        z = h @ w_in[l]
        u_conv, u_ssm, gl_conv, gl_ssm = jnp.split(z, split_pts, axis=-1)
        y_conv = conformer_conv(u_conv, conv_w[l], conv_b[l], conv_ln_g[l],
                                conv_ln_b[l], conv_proj[l])
        y_ssm = s5_mixer(u_ssm, ssm_a_re[l], ssm_a_im[l], ssm_b_re[l], ssm_b_im[l],
                         ssm_c_re[l], ssm_c_im[l], ssm_d[l], ssm_log_dt[l], ssm_glu[l])
        merged = jax.nn.sigmoid(gl_conv) * y_conv + jax.nn.sigmoid(gl_ssm) * y_ssm
        x = x + g1 * (merged @ w_out[l])

        h = rmsnorm(x, norm2_g[l]) * (1.0 + sc2) + sh2
        f_gate, f_up = jnp.split(h @ w_ffn_in[l], 2, axis=-1)
        x = x + g2 * ((jax.nn.silu(f_gate) * f_up) @ w_ffn_out[l])
    return rmsnorm(x, final_g)


import jax as _jax
import jax.numpy as _jnp

TWIN_FORMAT = 'train_step'
FWD_PARAMS = ['x', 'c', 'w_ada', 'b_ada', 'norm1_g', 'w_in', 'conv_w', 'conv_b', 'conv_ln_g', 'conv_ln_b', 'conv_proj', 'ssm_a_re', 'ssm_a_im', 'ssm_b_re', 'ssm_b_im', 'ssm_c_re', 'ssm_c_im', 'ssm_d', 'ssm_log_dt', 'ssm_glu', 'w_out', 'norm2_g', 'w_ffn_in', 'w_ffn_out', 'final_g']
TWIN_WEIGHTS = ['w_ada', 'b_ada', 'norm1_g', 'w_in', 'conv_w', 'conv_b', 'conv_ln_g', 'conv_ln_b', 'conv_proj', 'ssm_a_re', 'ssm_a_im', 'ssm_b_re', 'ssm_b_im', 'ssm_c_re', 'ssm_c_im', 'ssm_d', 'ssm_log_dt', 'ssm_glu', 'w_out', 'norm2_g', 'w_ffn_in', 'w_ffn_out', 'final_g']
TWIN_DIFF_INPUT = 'x'
TWIN_INPUTS = ['x', 'c', 'w_ada', 'b_ada', 'norm1_g', 'w_in', 'conv_w', 'conv_b', 'conv_ln_g', 'conv_ln_b', 'conv_proj', 'ssm_a_re', 'ssm_a_im', 'ssm_b_re', 'ssm_b_im', 'ssm_c_re', 'ssm_c_im', 'ssm_d', 'ssm_log_dt', 'ssm_glu', 'w_out', 'norm2_g', 'w_ffn_in', 'w_ffn_out', 'final_g', 'loss_target', 'm_w_ada', 'm_b_ada', 'm_norm1_g', 'm_w_in', 'm_conv_w', 'm_conv_b', 'm_conv_ln_g', 'm_conv_ln_b', 'm_conv_proj', 'm_ssm_a_re', 'm_ssm_a_im', 'm_ssm_b_re', 'm_ssm_b_im', 'm_ssm_c_re', 'm_ssm_c_im', 'm_ssm_d', 'm_ssm_log_dt', 'm_ssm_glu', 'm_w_out', 'm_norm2_g', 'm_w_ffn_in', 'm_w_ffn_out', 'm_final_g', 'v_w_ada', 'v_b_ada', 'v_norm1_g', 'v_w_in', 'v_conv_w', 'v_conv_b', 'v_conv_ln_g', 'v_conv_ln_b', 'v_conv_proj', 'v_ssm_a_re', 'v_ssm_a_im', 'v_ssm_b_re', 'v_ssm_b_im', 'v_ssm_c_re', 'v_ssm_c_im', 'v_ssm_d', 'v_ssm_log_dt', 'v_ssm_glu', 'v_w_out', 'v_norm2_g', 'v_w_ffn_in', 'v_w_ffn_out', 'v_final_g']
TWIN_OUTPUTS = ['loss', 'grad_x', 'grad_w_ada', 'grad_b_ada', 'grad_norm1_g', 'grad_w_in', 'grad_conv_w', 'grad_conv_b', 'grad_conv_ln_g', 'grad_conv_ln_b', 'grad_conv_proj', 'grad_ssm_a_re', 'grad_ssm_a_im', 'grad_ssm_b_re', 'grad_ssm_b_im', 'grad_ssm_c_re', 'grad_ssm_c_im', 'grad_ssm_d', 'grad_ssm_log_dt', 'grad_ssm_glu', 'grad_w_out', 'grad_norm2_g', 'grad_w_ffn_in', 'grad_w_ffn_out', 'grad_final_g', 'delta_w_ada', 'delta_b_ada', 'delta_norm1_g', 'delta_w_in', 'delta_conv_w', 'delta_conv_b', 'delta_conv_ln_g', 'delta_conv_ln_b', 'delta_conv_proj', 'delta_ssm_a_re', 'delta_ssm_a_im', 'delta_ssm_b_re', 'delta_ssm_b_im', 'delta_ssm_c_re', 'delta_ssm_c_im', 'delta_ssm_d', 'delta_ssm_log_dt', 'delta_ssm_glu', 'delta_w_out', 'delta_norm2_g', 'delta_w_ffn_in', 'delta_w_ffn_out', 'delta_final_g', 'new_m_w_ada', 'new_m_b_ada', 'new_m_norm1_g', 'new_m_w_in', 'new_m_conv_w', 'new_m_conv_b', 'new_m_conv_ln_g', 'new_m_conv_ln_b', 'new_m_conv_proj', 'new_m_ssm_a_re', 'new_m_ssm_a_im', 'new_m_ssm_b_re', 'new_m_ssm_b_im', 'new_m_ssm_c_re', 'new_m_ssm_c_im', 'new_m_ssm_d', 'new_m_ssm_log_dt', 'new_m_ssm_glu', 'new_m_w_out', 'new_m_norm2_g', 'new_m_w_ffn_in', 'new_m_w_ffn_out', 'new_m_final_g', 'new_v_w_ada', 'new_v_b_ada', 'new_v_norm1_g', 'new_v_w_in', 'new_v_conv_w', 'new_v_conv_b', 'new_v_conv_ln_g', 'new_v_conv_ln_b', 'new_v_conv_proj', 'new_v_ssm_a_re', 'new_v_ssm_a_im', 'new_v_ssm_b_re', 'new_v_ssm_b_im', 'new_v_ssm_c_re', 'new_v_ssm_c_im', 'new_v_ssm_d', 'new_v_ssm_log_dt', 'new_v_ssm_glu', 'new_v_w_out', 'new_v_norm2_g', 'new_v_w_ffn_in', 'new_v_w_ffn_out', 'new_v_final_g']
TWIN_LEAF_KINDS = {'loss': 'loss', 'grad_x': 'grad_x', 'grad_w_ada': 'grad_w', 'grad_b_ada': 'grad_w', 'grad_norm1_g': 'grad_w', 'grad_w_in': 'grad_w', 'grad_conv_w': 'grad_w', 'grad_conv_b': 'grad_w', 'grad_conv_ln_g': 'grad_w', 'grad_conv_ln_b': 'grad_w', 'grad_conv_proj': 'grad_w', 'grad_ssm_a_re': 'grad_w', 'grad_ssm_a_im': 'grad_w', 'grad_ssm_b_re': 'grad_w', 'grad_ssm_b_im': 'grad_w', 'grad_ssm_c_re': 'grad_w', 'grad_ssm_c_im': 'grad_w', 'grad_ssm_d': 'grad_w', 'grad_ssm_log_dt': 'grad_w', 'grad_ssm_glu': 'grad_w', 'grad_w_out': 'grad_w', 'grad_norm2_g': 'grad_w', 'grad_w_ffn_in': 'grad_w', 'grad_w_ffn_out': 'grad_w', 'grad_final_g': 'grad_w', 'delta_w_ada': 'delta_w', 'delta_b_ada': 'delta_w', 'delta_norm1_g': 'delta_w', 'delta_w_in': 'delta_w', 'delta_conv_w': 'delta_w', 'delta_conv_b': 'delta_w', 'delta_conv_ln_g': 'delta_w', 'delta_conv_ln_b': 'delta_w', 'delta_conv_proj': 'delta_w', 'delta_ssm_a_re': 'delta_w', 'delta_ssm_a_im': 'delta_w', 'delta_ssm_b_re': 'delta_w', 'delta_ssm_b_im': 'delta_w', 'delta_ssm_c_re': 'delta_w', 'delta_ssm_c_im': 'delta_w', 'delta_ssm_d': 'delta_w', 'delta_ssm_log_dt': 'delta_w', 'delta_ssm_glu': 'delta_w', 'delta_w_out': 'delta_w', 'delta_norm2_g': 'delta_w', 'delta_w_ffn_in': 'delta_w', 'delta_w_ffn_out': 'delta_w', 'delta_final_g': 'delta_w', 'new_m_w_ada': 'new_m', 'new_m_b_ada': 'new_m', 'new_m_norm1_g': 'new_m', 'new_m_w_in': 'new_m', 'new_m_conv_w': 'new_m', 'new_m_conv_b': 'new_m', 'new_m_conv_ln_g': 'new_m', 'new_m_conv_ln_b': 'new_m', 'new_m_conv_proj': 'new_m', 'new_m_ssm_a_re': 'new_m', 'new_m_ssm_a_im': 'new_m', 'new_m_ssm_b_re': 'new_m', 'new_m_ssm_b_im': 'new_m', 'new_m_ssm_c_re': 'new_m', 'new_m_ssm_c_im': 'new_m', 'new_m_ssm_d': 'new_m', 'new_m_ssm_log_dt': 'new_m', 'new_m_ssm_glu': 'new_m', 'new_m_w_out': 'new_m', 'new_m_norm2_g': 'new_m', 'new_m_w_ffn_in': 'new_m', 'new_m_w_ffn_out': 'new_m', 'new_m_final_g': 'new_m', 'new_v_w_ada': 'new_v', 'new_v_b_ada': 'new_v', 'new_v_norm1_g': 'new_v', 'new_v_w_in': 'new_v', 'new_v_conv_w': 'new_v', 'new_v_conv_b': 'new_v', 'new_v_conv_ln_g': 'new_v', 'new_v_conv_ln_b': 'new_v', 'new_v_conv_proj': 'new_v', 'new_v_ssm_a_re': 'new_v', 'new_v_ssm_a_im': 'new_v', 'new_v_ssm_b_re': 'new_v', 'new_v_ssm_b_im': 'new_v', 'new_v_ssm_c_re': 'new_v', 'new_v_ssm_c_im': 'new_v', 'new_v_ssm_d': 'new_v', 'new_v_ssm_log_dt': 'new_v', 'new_v_ssm_glu': 'new_v', 'new_v_w_out': 'new_v', 'new_v_norm2_g': 'new_v', 'new_v_w_ffn_in': 'new_v', 'new_v_w_ffn_out': 'new_v', 'new_v_final_g': 'new_v'}


def _forward(args):
    return _fwd_reference(*[args[k] for k in FWD_PARAMS])


def _output_shape():
    def fwd():
        inp = _fwd_setup_inputs(0)
        return _fwd_reference(*[inp[k] for k in FWD_PARAMS])
    out = _jax.eval_shape(fwd)
    return out.shape, out.dtype

N_MICROBATCH = 1
ADAM_LR = 0.001
ADAM_B1 = 0.9
ADAM_B2 = 0.999
ADAM_EPS = 1e-08
ADAM_WD = 0.01
ADAM_STEP = 10
PER_EXAMPLE_BATCH_AXIS = {'x': 0, 'c': 0, 'loss_target': 0}
SHARED_INPUTS = []
_WEIGHT_DTYPES = {'w_ada': _jnp.float32, 'b_ada': _jnp.float32, 'norm1_g': _jnp.float32, 'w_in': _jnp.float32, 'conv_w': _jnp.float32, 'conv_b': _jnp.float32, 'conv_ln_g': _jnp.float32, 'conv_ln_b': _jnp.float32, 'conv_proj': _jnp.float32, 'ssm_a_re': _jnp.float32, 'ssm_a_im': _jnp.float32, 'ssm_b_re': _jnp.float32, 'ssm_b_im': _jnp.float32, 'ssm_c_re': _jnp.float32, 'ssm_c_im': _jnp.float32, 'ssm_d': _jnp.float32, 'ssm_log_dt': _jnp.float32, 'ssm_glu': _jnp.float32, 'w_out': _jnp.float32, 'norm2_g': _jnp.float32, 'w_ffn_in': _jnp.float32, 'w_ffn_out': _jnp.float32, 'final_g': _jnp.float32}
MOMENT_SCALE = {'w_ada': 4.985348e-02, 'b_ada': 9.277654e-02, 'norm1_g': 2.419449e-02, 'w_in': 1.355699e-02, 'conv_w': 2.690208e-02, 'conv_b': 5.422635e-02, 'conv_ln_g': 3.194768e-02, 'conv_ln_b': 2.772617e-02, 'conv_proj': 1.835121e-02, 'ssm_a_re': 1.818708e-03, 'ssm_a_im': 1.165223e-03, 'ssm_b_re': 8.627369e-04, 'ssm_b_im': 8.836765e-04, 'ssm_c_re': 1.275196e-03, 'ssm_c_im': 1.216754e-03, 'ssm_d': 1.760183e-02, 'ssm_log_dt': 1.089528e+00, 'ssm_glu': 8.560366e-03, 'w_out': 2.176170e-02, 'norm2_g': 5.438863e-02, 'w_ffn_in': 2.349459e-02, 'w_ffn_out': 3.823195e-02, 'final_g': 3.208262e+01}


def _to_microbatches(a, axis):
    t = _jnp.moveaxis(a, axis, 0)
    t = t.reshape((N_MICROBATCH, t.shape[0] // N_MICROBATCH) + t.shape[1:])
    return _jnp.moveaxis(t, 1, axis + 1)


def setup_inputs(seed: int = 0) -> dict:
    inp = _fwd_setup_inputs(seed)
    key = _jax.random.fold_in(_jax.random.key(seed), 7919)
    shape, _ = _output_shape()
    out = dict(inp)
    out["loss_target"] = _jax.random.normal(_jax.random.fold_in(key, 0), shape, _jnp.float32)
    for i, name in enumerate(TWIN_WEIGHTS):
        w = inp[name].astype(_jnp.float32)
        if MOMENT_SCALE is None:
            s = _jnp.sqrt(_jnp.mean(_jnp.square(w)) + 1e-30)
        else:
            s = MOMENT_SCALE[name]
        km, kv = _jax.random.split(_jax.random.fold_in(key, i + 1))
        out[name] = w
        out["m_" + name] = s * _jax.random.normal(km, w.shape, _jnp.float32)
        out["v_" + name] = (s * s) * _jax.random.uniform(kv, w.shape, _jnp.float32, 0.5, 1.5)
    if N_MICROBATCH > 1:
        for name, axis in PER_EXAMPLE_BATCH_AXIS.items():
            out[name] = _to_microbatches(out[name], axis)
    return {'x': out['x'], 'c': out['c'], 'w_ada': out['w_ada'], 'b_ada': out['b_ada'], 'norm1_g': out['norm1_g'], 'w_in': out['w_in'], 'conv_w': out['conv_w'], 'conv_b': out['conv_b'], 'conv_ln_g': out['conv_ln_g'], 'conv_ln_b': out['conv_ln_b'], 'conv_proj': out['conv_proj'], 'ssm_a_re': out['ssm_a_re'], 'ssm_a_im': out['ssm_a_im'], 'ssm_b_re': out['ssm_b_re'], 'ssm_b_im': out['ssm_b_im'], 'ssm_c_re': out['ssm_c_re'], 'ssm_c_im': out['ssm_c_im'], 'ssm_d': out['ssm_d'], 'ssm_log_dt': out['ssm_log_dt'], 'ssm_glu': out['ssm_glu'], 'w_out': out['w_out'], 'norm2_g': out['norm2_g'], 'w_ffn_in': out['w_ffn_in'], 'w_ffn_out': out['w_ffn_out'], 'final_g': out['final_g'], 'loss_target': out['loss_target'], 'm_w_ada': out['m_w_ada'], 'm_b_ada': out['m_b_ada'], 'm_norm1_g': out['m_norm1_g'], 'm_w_in': out['m_w_in'], 'm_conv_w': out['m_conv_w'], 'm_conv_b': out['m_conv_b'], 'm_conv_ln_g': out['m_conv_ln_g'], 'm_conv_ln_b': out['m_conv_ln_b'], 'm_conv_proj': out['m_conv_proj'], 'm_ssm_a_re': out['m_ssm_a_re'], 'm_ssm_a_im': out['m_ssm_a_im'], 'm_ssm_b_re': out['m_ssm_b_re'], 'm_ssm_b_im': out['m_ssm_b_im'], 'm_ssm_c_re': out['m_ssm_c_re'], 'm_ssm_c_im': out['m_ssm_c_im'], 'm_ssm_d': out['m_ssm_d'], 'm_ssm_log_dt': out['m_ssm_log_dt'], 'm_ssm_glu': out['m_ssm_glu'], 'm_w_out': out['m_w_out'], 'm_norm2_g': out['m_norm2_g'], 'm_w_ffn_in': out['m_w_ffn_in'], 'm_w_ffn_out': out['m_w_ffn_out'], 'm_final_g': out['m_final_g'], 'v_w_ada': out['v_w_ada'], 'v_b_ada': out['v_b_ada'], 'v_norm1_g': out['v_norm1_g'], 'v_w_in': out['v_w_in'], 'v_conv_w': out['v_conv_w'], 'v_conv_b': out['v_conv_b'], 'v_conv_ln_g': out['v_conv_ln_g'], 'v_conv_ln_b': out['v_conv_ln_b'], 'v_conv_proj': out['v_conv_proj'], 'v_ssm_a_re': out['v_ssm_a_re'], 'v_ssm_a_im': out['v_ssm_a_im'], 'v_ssm_b_re': out['v_ssm_b_re'], 'v_ssm_b_im': out['v_ssm_b_im'], 'v_ssm_c_re': out['v_ssm_c_re'], 'v_ssm_c_im': out['v_ssm_c_im'], 'v_ssm_d': out['v_ssm_d'], 'v_ssm_log_dt': out['v_ssm_log_dt'], 'v_ssm_glu': out['v_ssm_glu'], 'v_w_out': out['v_w_out'], 'v_norm2_g': out['v_norm2_g'], 'v_w_ffn_in': out['v_w_ffn_in'], 'v_w_ffn_out': out['v_w_ffn_out'], 'v_final_g': out['v_final_g']}


def _loss(weights, diff, rest, loss_target):
    with _jax.named_scope("forward"):
        args = {**rest, TWIN_DIFF_INPUT: diff, **{k: w.astype(_WEIGHT_DTYPES[k]) for k, w in weights.items()}}
        y = _forward(args)
    with _jax.named_scope("loss_head"):
        err = _jnp.square(y.astype(_jnp.float32) - loss_target)
        return 0.5 * _jnp.sum(_jnp.mean(err, axis=-1)) if err.ndim else 0.5 * err


def _adamw(w, g, m, v):
    m = ADAM_B1 * m + (1.0 - ADAM_B1) * g
    v = ADAM_B2 * v + (1.0 - ADAM_B2) * _jnp.square(g)
    m_hat = m / (1.0 - ADAM_B1 ** ADAM_STEP)
    v_hat = v / (1.0 - ADAM_B2 ** ADAM_STEP)
    delta = -ADAM_LR * (m_hat / (_jnp.sqrt(v_hat) + ADAM_EPS) + ADAM_WD * w)
    return delta, m, v


def reference(x, c, w_ada, b_ada, norm1_g, w_in, conv_w, conv_b, conv_ln_g, conv_ln_b, conv_proj, ssm_a_re, ssm_a_im, ssm_b_re, ssm_b_im, ssm_c_re, ssm_c_im, ssm_d, ssm_log_dt, ssm_glu, w_out, norm2_g, w_ffn_in, w_ffn_out, final_g, loss_target, m_w_ada, m_b_ada, m_norm1_g, m_w_in, m_conv_w, m_conv_b, m_conv_ln_g, m_conv_ln_b, m_conv_proj, m_ssm_a_re, m_ssm_a_im, m_ssm_b_re, m_ssm_b_im, m_ssm_c_re, m_ssm_c_im, m_ssm_d, m_ssm_log_dt, m_ssm_glu, m_w_out, m_norm2_g, m_w_ffn_in, m_w_ffn_out, m_final_g, v_w_ada, v_b_ada, v_norm1_g, v_w_in, v_conv_w, v_conv_b, v_conv_ln_g, v_conv_ln_b, v_conv_proj, v_ssm_a_re, v_ssm_a_im, v_ssm_b_re, v_ssm_b_im, v_ssm_c_re, v_ssm_c_im, v_ssm_d, v_ssm_log_dt, v_ssm_glu, v_w_out, v_norm2_g, v_w_ffn_in, v_w_ffn_out, v_final_g):
    given = dict(x=x, c=c, w_ada=w_ada, b_ada=b_ada, norm1_g=norm1_g, w_in=w_in, conv_w=conv_w, conv_b=conv_b, conv_ln_g=conv_ln_g, conv_ln_b=conv_ln_b, conv_proj=conv_proj, ssm_a_re=ssm_a_re, ssm_a_im=ssm_a_im, ssm_b_re=ssm_b_re, ssm_b_im=ssm_b_im, ssm_c_re=ssm_c_re, ssm_c_im=ssm_c_im, ssm_d=ssm_d, ssm_log_dt=ssm_log_dt, ssm_glu=ssm_glu, w_out=w_out, norm2_g=norm2_g, w_ffn_in=w_ffn_in, w_ffn_out=w_ffn_out, final_g=final_g, loss_target=loss_target, m_w_ada=m_w_ada, m_b_ada=m_b_ada, m_norm1_g=m_norm1_g, m_w_in=m_w_in, m_conv_w=m_conv_w, m_conv_b=m_conv_b, m_conv_ln_g=m_conv_ln_g, m_conv_ln_b=m_conv_ln_b, m_conv_proj=m_conv_proj, m_ssm_a_re=m_ssm_a_re, m_ssm_a_im=m_ssm_a_im, m_ssm_b_re=m_ssm_b_re, m_ssm_b_im=m_ssm_b_im, m_ssm_c_re=m_ssm_c_re, m_ssm_c_im=m_ssm_c_im, m_ssm_d=m_ssm_d, m_ssm_log_dt=m_ssm_log_dt, m_ssm_glu=m_ssm_glu, m_w_out=m_w_out, m_norm2_g=m_norm2_g, m_w_ffn_in=m_w_ffn_in, m_w_ffn_out=m_w_ffn_out, m_final_g=m_final_g, v_w_ada=v_w_ada, v_b_ada=v_b_ada, v_norm1_g=v_norm1_g, v_w_in=v_w_in, v_conv_w=v_conv_w, v_conv_b=v_conv_b, v_conv_ln_g=v_conv_ln_g, v_conv_ln_b=v_conv_ln_b, v_conv_proj=v_conv_proj, v_ssm_a_re=v_ssm_a_re, v_ssm_a_im=v_ssm_a_im, v_ssm_b_re=v_ssm_b_re, v_ssm_b_im=v_ssm_b_im, v_ssm_c_re=v_ssm_c_re, v_ssm_c_im=v_ssm_c_im, v_ssm_d=v_ssm_d, v_ssm_log_dt=v_ssm_log_dt, v_ssm_glu=v_ssm_glu, v_w_out=v_w_out, v_norm2_g=v_norm2_g, v_w_ffn_in=v_w_ffn_in, v_w_ffn_out=v_w_ffn_out, v_final_g=v_final_g)
    weights = {n: given[n] for n in TWIN_WEIGHTS}
    shared = {n: given[n] for n in SHARED_INPUTS}
    per_example = {n: given[n] for n in ['x', 'c']}
    grad_fn = _jax.value_and_grad(_loss, argnums=(0, 1))

    def one_microbatch(ex, loss_target):
        ex = dict(ex)
        diff = ex.pop(TWIN_DIFF_INPUT)
        return grad_fn(weights, diff, {**shared, **ex}, loss_target)

    if N_MICROBATCH == 1:
        loss, (grad_w, grad_x) = one_microbatch(per_example, given["loss_target"])
    else:
        def body(carry, xs):
            loss_sum, grad_sum = carry
            l_k, (gw_k, gx_k) = one_microbatch(xs[0], xs[1])
            with _jax.named_scope("update"):
                return (loss_sum + l_k, _jax.tree.map(_jnp.add, grad_sum, gw_k)), gx_k

        init = (_jnp.zeros((), _jnp.float32), _jax.tree.map(_jnp.zeros_like, weights))
        (loss, grad_w), grad_x = _jax.lax.scan(body, init, (per_example, given["loss_target"]))
    with _jax.named_scope("update"):
        delta_w, new_m, new_v = {}, {}, {}
        for n in TWIN_WEIGHTS:
            delta_w[n], new_m[n], new_v[n] = _adamw(weights[n], grad_w[n], given["m_" + n], given["v_" + n])
    return (loss, grad_x, *[grad_w[n] for n in TWIN_WEIGHTS], *[delta_w[n] for n in TWIN_WEIGHTS],
            *[new_m[n] for n in TWIN_WEIGHTS], *[new_v[n] for n in TWIN_WEIGHTS])
```

```python
import functools
import math

import jax
import jax.numpy as jnp
from jax import lax
from jax.experimental import pallas as pl
from jax.experimental.pallas import tpu as pltpu

F32 = jnp.float32
BF16 = jnp.bfloat16

D = 1024
CW = 512
KC = 31
SW = 512
NG = 32
GH = 16
NP = 64
NST = NG * NP
FH = 2816
NMOD = 6
NDEV = 8
EPS = 1e-6
CB = 128
SB = 512
NBLK = SW // CB
HALO = 32

ADAM_LR = 0.001
ADAM_B1 = 0.9
ADAM_B2 = 0.999
ADAM_EPS = 1e-08
ADAM_WD = 0.01
ADAM_STEP = 10

V7X_VMEM_BYTES = 64 * 1024 * 1024
VMEM_LIMIT = V7X_VMEM_BYTES - 8 * 1024 * 1024
LANE = 128
MESH = pl.DeviceIdType.MESH


def _params(sem=None, **kw):
    if sem is not None:
        kw["dimension_semantics"] = sem
    return pltpu.CompilerParams(vmem_limit_bytes=VMEM_LIMIT, **kw)


def _tile(n, most):
    best = None
    for t in range(LANE, most + 1, LANE):
        if n % t == 0:
            best = t
    if best is None:
        raise ValueError(f"no tile for {n}")
    return best


def _sig(x):
    return jax.nn.sigmoid(x)


def mm(name, a, b, mode, out_dtype=F32):
    if mode == "nn":
        (m, k), (k2, n) = a.shape, b.shape
    elif mode == "nt":
        (m, k), (n, k2) = a.shape, b.shape
    else:
        (k, m), (k2, n) = a.shape, b.shape
    assert k == k2, (name, a.shape, b.shape)
    bm, bn, bk = _tile(m, 1024), _tile(n, 1408), _tile(k, 1408 if k % 1408 == 0 else 1024)
    nk = k // bk
    if mode == "nn":
        a_spec = pl.BlockSpec((bm, bk), lambda i, j, kk: (i, kk))
        b_spec = pl.BlockSpec((bk, bn), lambda i, j, kk: (kk, j))
        dims = (((1,), (0,)), ((), ()))
    elif mode == "nt":
        a_spec = pl.BlockSpec((bm, bk), lambda i, j, kk: (i, kk))
        b_spec = pl.BlockSpec((bn, bk), lambda i, j, kk: (j, kk))
        dims = (((1,), (1,)), ((), ()))
    else:
        a_spec = pl.BlockSpec((bk, bm), lambda i, j, kk: (kk, i))
        b_spec = pl.BlockSpec((bk, bn), lambda i, j, kk: (kk, j))
        dims = (((0,), (0,)), ((), ()))

    def body(a_ref, b_ref, o_ref, acc_ref):
        kk = pl.program_id(2)

        @pl.when(kk == 0)
        def _():
            acc_ref[...] = jnp.zeros_like(acc_ref)

        acc_ref[...] += lax.dot_general(a_ref[...], b_ref[...], dims, preferred_element_type=F32)

        @pl.when(kk == nk - 1)
        def _():
            o_ref[...] = acc_ref[...].astype(o_ref.dtype)

    return pl.pallas_call(
        body, name=name,
        grid=(m // bm, n // bn, nk),
        in_specs=[a_spec, b_spec],
        out_specs=pl.BlockSpec((bm, bn), lambda i, j, kk: (i, j)),
        out_shape=jax.ShapeDtypeStruct((m, n), out_dtype),
        scratch_shapes=[pltpu.VMEM((bm, bn), F32)],
        compiler_params=_params(("parallel", "parallel", "arbitrary")),
    )(a, b)


def rowwise(name, fn, rows, consts, out_rows, out_sums, ts):
    rows = [r if isinstance(r, tuple) else (r, r.shape[1], 0) for r in rows]
    s = rows[0][0].shape[0]
    nt = s // ts
    nr, nc, no, ns = len(rows), len(consts), len(out_rows), len(out_sums)
    in_specs = [pl.BlockSpec((ts, w), functools.partial(lambda i, cb: (i, cb), cb=cb)) for (_, w, cb) in rows]
    in_specs += [pl.BlockSpec(c.shape, lambda i: (0, 0)) for c in consts]
    out_shape = [jax.ShapeDtypeStruct((s, w), dt) for (w, dt) in out_rows]
    out_shape += [jax.ShapeDtypeStruct((1, w), F32) for w in out_sums]
    out_specs = [pl.BlockSpec((ts, w), lambda i: (i, 0)) for (w, _) in out_rows]
    out_specs += [pl.BlockSpec((1, w), lambda i: (0, 0)) for w in out_sums]

    def body(*refs):
        ins, outs = refs[:nr + nc], refs[nr + nc:]
        i = pl.program_id(0)
        ro, so = fn(*[r[...] for r in ins])
        for q in range(no):
            outs[q][...] = ro[q].astype(outs[q].dtype)
        if ns:
            @pl.when(i == 0)
            def _():
                for q in range(ns):
                    outs[no + q][...] = jnp.zeros_like(outs[no + q])

            for q in range(ns):
                outs[no + q][...] += so[q]

    return pl.pallas_call(
        body, name=name, grid=(nt,),
        in_specs=in_specs, out_specs=out_specs, out_shape=out_shape,
        compiler_params=_params(("arbitrary",) if ns else ("parallel",)),
    )(*[r[0] for r in rows], *consts)


def _colsum(v):
    return jnp.sum(v, axis=0, keepdims=True)


def _rms_stats(xv):
    r = lax.rsqrt(jnp.mean(xv * xv, axis=-1, keepdims=True) + EPS)
    return r, xv * r


def _rms_bwd(dxhat, xhat, r):
    return r * (dxhat - xhat * jnp.mean(dxhat * xhat, axis=-1, keepdims=True))


def _gelu(v):
    k = math.sqrt(2.0 / math.pi)
    t = jnp.tanh(k * (v + 0.044715 * v * v * v))
    return 0.5 * v * (1.0 + t), t


def _gelu_grad(v, t):
    k = math.sqrt(2.0 / math.pi)
    return 0.5 * (1.0 + t) + 0.5 * v * (1.0 - t * t) * k * (1.0 + 3.0 * 0.044715 * v * v)


CONV_TS = 256
CONV_CH = 64


def _ln_fwd(yc, g, b):
    mu = jnp.mean(yc, axis=-1, keepdims=True)
    xc = yc - mu
    rstd = lax.rsqrt(jnp.mean(xc * xc, axis=-1, keepdims=True) + EPS)
    nhat = xc * rstd
    return nhat, rstd, nhat * g + b


def conv_fwd(z, w32, cb, lg, lb):
    s = z.shape[0]
    ts = CONV_TS
    nt = s // ts
    hb = ts // HALO

    def body(a_ref, g_ref, ah_ref, gh_ref, w_ref, cb_ref, lg_ref, lb_ref, yc_ref, s_ref, ubuf):
        i = pl.program_id(0)
        first = (i > 0).astype(F32)
        ubuf[0:HALO, :] = ah_ref[...] * _sig(gh_ref[...]) * first
        ubuf[HALO:HALO + ts, :] = a_ref[...] * _sig(g_ref[...])
        for c0 in range(0, ts, CONV_CH):
            acc = jnp.zeros((CONV_CH, CW), F32)
            for k in range(KC):
                acc = acc + w_ref[k:k + 1, :] * ubuf[pl.ds(c0 + k + 2, CONV_CH), :]
            yc = acc + cb_ref[...]
            yc_ref[c0:c0 + CONV_CH, :] = yc
            _, _, ln = _ln_fwd(yc, lg_ref[...], lb_ref[...])
            s_ref[c0:c0 + CONV_CH, :] = (ln * _sig(ln)).astype(s_ref.dtype)

    cur = lambda cbk: pl.BlockSpec((ts, CW), functools.partial(lambda i, q: (i, q), q=cbk))
    prev = lambda cbk: pl.BlockSpec((HALO, CW), functools.partial(lambda i, q: (jnp.maximum(i * hb - 1, 0), q), q=cbk))
    const = lambda a: pl.BlockSpec(a.shape, lambda i: (0, 0))
    return pl.pallas_call(
        body, name="conv_fwd", grid=(nt,),
        in_specs=[cur(0), cur(1), prev(0), prev(1), const(w32), const(cb), const(lg), const(lb)],
        out_specs=[pl.BlockSpec((ts, CW), lambda i: (i, 0)), pl.BlockSpec((ts, CW), lambda i: (i, 0))],
        out_shape=[jax.ShapeDtypeStruct((s, CW), F32), jax.ShapeDtypeStruct((s, CW), BF16)],
        scratch_shapes=[pltpu.VMEM((HALO + ts, CW), F32)],
        compiler_params=_params(("parallel",)),
    )(z, z, z, z, w32, cb, lg, lb)


def conv_bwd(ds, yc, z, w32, lg, lb):
    s = z.shape[0]
    ts = CONV_TS
    nt = s // ts
    hb = ts // HALO
    last_hb = s // HALO - 1

    def ln_bwd(dsv, ycv, g, b):
        nhat, rstd, ln = _ln_fwd(ycv, g, b)
        sg = _sig(ln)
        dln = dsv * (sg * (1.0 + ln * (1.0 - sg)))
        dnh = dln * g
        dyc = rstd * (dnh - jnp.mean(dnh, axis=-1, keepdims=True)
                      - nhat * jnp.mean(dnh * nhat, axis=-1, keepdims=True))
        return dyc, dln, nhat

    def body(ds_ref, yc_ref, dsn_ref, ycn_ref, a_ref, g_ref, ah_ref, gh_ref, w_ref, lg_ref, lb_ref,
             dz_ref, dlg_ref, dlb_ref, dcb_ref, dw_ref, dbuf, ubuf):
        i = pl.program_id(0)

        @pl.when(i == 0)
        def _():
            dlg_ref[...] = jnp.zeros_like(dlg_ref)
            dlb_ref[...] = jnp.zeros_like(dlb_ref)
            dcb_ref[...] = jnp.zeros_like(dcb_ref)
            dw_ref[...] = jnp.zeros_like(dw_ref)

        lg, lb = lg_ref[...], lb_ref[...]
        dyc, dln, nhat = ln_bwd(ds_ref[...], yc_ref[...], lg, lb)
        dlg_ref[...] += _colsum(dln * nhat)
        dlb_ref[...] += _colsum(dln)
        dcb_ref[...] += _colsum(dyc)
        dbuf[0:ts, :] = dyc
        nxt = (i < nt - 1).astype(F32)
        dbuf[ts:ts + HALO, :] = ln_bwd(dsn_ref[...], ycn_ref[...], lg, lb)[0] * nxt
        first = (i > 0).astype(F32)
        ubuf[0:HALO, :] = ah_ref[...] * _sig(gh_ref[...]) * first
        ubuf[HALO:HALO + ts, :] = a_ref[...] * _sig(g_ref[...])
        for c0 in range(0, ts, CONV_CH):
            du = jnp.zeros((CONV_CH, CW), F32)
            dyc_c = dbuf[c0:c0 + CONV_CH, :]
            for k in range(KC):
                du = du + w_ref[k:k + 1, :] * dbuf[pl.ds(c0 + KC - 1 - k, CONV_CH), :]
                dw_ref[k:k + 1, :] += _colsum(dyc_c * ubuf[pl.ds(c0 + k + 2, CONV_CH), :])
            av = a_ref[c0:c0 + CONV_CH, :]
            sg = _sig(g_ref[c0:c0 + CONV_CH, :])
            dz_ref[c0:c0 + CONV_CH, 0:CW] = (du * sg).astype(dz_ref.dtype)
            dz_ref[c0:c0 + CONV_CH, CW:2 * CW] = (du * av * sg * (1.0 - sg)).astype(dz_ref.dtype)

    cur = lambda w, cbk: pl.BlockSpec((ts, w), functools.partial(lambda i, q: (i, q), q=cbk))
    prev = lambda cbk: pl.BlockSpec((HALO, CW), functools.partial(lambda i, q: (jnp.maximum(i * hb - 1, 0), q), q=cbk))
    nxt_spec = pl.BlockSpec((HALO, CW), lambda i: (jnp.minimum((i + 1) * hb, last_hb), 0))
    const = lambda a: pl.BlockSpec(a.shape, lambda i: (0, 0))
    acc = lambda r: pl.BlockSpec((r, CW), lambda i: (0, 0))
    return pl.pallas_call(
        body, name="conv_bwd", grid=(nt,),
        in_specs=[cur(CW, 0), cur(CW, 0), nxt_spec, nxt_spec, cur(CW, 0), cur(CW, 1), prev(0), prev(1),
                  const(w32), const(lg), const(lb)],
        out_specs=[pl.BlockSpec((ts, 2 * CW), lambda i: (i, 0)), acc(1), acc(1), acc(1), acc(HALO)],
        out_shape=[jax.ShapeDtypeStruct((s, 2 * CW), BF16), jax.ShapeDtypeStruct((1, CW), F32),
                   jax.ShapeDtypeStruct((1, CW), F32), jax.ShapeDtypeStruct((1, CW), F32),
                   jax.ShapeDtypeStruct((HALO, CW), F32)],
        scratch_shapes=[pltpu.VMEM((ts + HALO, CW), F32), pltpu.VMEM((HALO + ts, CW), F32)],
        compiler_params=_params(("arbitrary",)),
    )(ds, yc, ds, yc, z, z, z, z, w32, lg, lb)


SSM_TS = 512
GRP = 8


def _cmul(ar, ai, br, bi):
    return ar * br - ai * bi, ar * bi + ai * br


def _scan_tables(ar, ai, reverse):
    n = ar.shape[1]
    row = lax.broadcasted_iota(jnp.int32, (GRP, n), 0)
    dist = (GRP - 1 - row) if reverse else row
    one_r = jnp.broadcast_to(ar, (GRP, n))
    one_i = jnp.broadcast_to(ai, (GRP, n))
    p2r, p2i = _cmul(one_r, one_i, one_r, one_i)
    p4r, p4i = _cmul(p2r, p2i, p2r, p2i)
    steps = []
    for sft, (pr, pi) in ((1, (one_r, one_i)), (2, (p2r, p2i)), (4, (p4r, p4i))):
        keep = dist >= sft
        steps.append((jnp.where(keep, pr, 0.0), jnp.where(keep, pi, 0.0)))
    cr, ci = one_r, one_i
    accr, acci = one_r, one_i
    for e in range(1, GRP):
        cr, ci = _cmul(cr, ci, one_r, one_i)
        accr = jnp.where(dist == e, cr, accr)
        acci = jnp.where(dist == e, ci, acci)
    return steps, (accr, acci)


def _scan_group(xr, xi, steps, carry_tab, cr, ci, reverse):
    for sft, (tr, ti) in zip((1, 2, 4), steps):
        amt = (GRP - sft) if reverse else sft
        sr = pltpu.roll(xr, amt, 0)
        si = pltpu.roll(xi, amt, 0)
        xr, xi = xr + tr * sr - ti * si, xi + tr * si + ti * sr
    pr, pi = carry_tab
    xr = xr + pr * cr - pi * ci
    xi = xi + pr * ci + pi * cr
    return xr, xi


def ssm_fwd(z, wb_re, wb_im, wc, e_re, e_im, dvec):
    s = z.shape[0]
    ts = SSM_TS
    nt = s // ts
    ucol0 = 2 * CW // CB

    def body(u_ref, wbr_ref, wbi_ref, wc_ref, er_ref, ei_ref, d_ref, xr_ref, xi_ref, y_ref, gl_ref, car_r, car_i):
        i = pl.program_id(1)

        @pl.when(i == 0)
        def _():
            car_r[...] = jnp.zeros_like(car_r)
            car_i[...] = jnp.zeros_like(car_i)

        u = u_ref[...]
        ub = u.astype(BF16)
        xr_ref[...] = jnp.dot(ub, wbr_ref[0], preferred_element_type=F32)
        xi_ref[...] = jnp.dot(ub, wbi_ref[0], preferred_element_type=F32)
        steps, ctab = _scan_tables(er_ref[0], ei_ref[0], False)

        def grp(r, carry):
            cr, ci = carry
            r0 = pl.multiple_of(r * GRP, GRP)
            xr, xi = _scan_group(xr_ref[pl.ds(r0, GRP), :], xi_ref[pl.ds(r0, GRP), :], steps, ctab, cr, ci, False)
            xr_ref[pl.ds(r0, GRP), :] = xr
            xi_ref[pl.ds(r0, GRP), :] = xi
            return (jnp.broadcast_to(xr[GRP - 1:GRP, :], (GRP, SB)), jnp.broadcast_to(xi[GRP - 1:GRP, :], (GRP, SB)))

        cr, ci = lax.fori_loop(0, ts // GRP, grp, (car_r[...], car_i[...]))
        car_r[...] = cr
        car_i[...] = ci
        y = (jnp.dot(xr_ref[...].astype(BF16), wc_ref[0, 0:SB, :], preferred_element_type=F32)
             + jnp.dot(xi_ref[...].astype(BF16), wc_ref[0, SB:2 * SB, :], preferred_element_type=F32)
             + d_ref[0] * u)
        y_ref[...] = y
        gl_ref[...] = _gelu(y)[0].astype(gl_ref.dtype)

    blk3 = lambda a: pl.BlockSpec((1,) + a.shape[1:], lambda j, i: (j, 0, 0))
    return pl.pallas_call(
        body, name="ssm_fwd", grid=(NBLK, nt),
        in_specs=[pl.BlockSpec((ts, CB), lambda j, i: (i, ucol0 + j)),
                  blk3(wb_re), blk3(wb_im), blk3(wc), blk3(e_re), blk3(e_im), blk3(dvec)],
        out_specs=[pl.BlockSpec((ts, SB), lambda j, i: (i, j)), pl.BlockSpec((ts, SB), lambda j, i: (i, j)),
                   pl.BlockSpec((ts, CB), lambda j, i: (i, j)), pl.BlockSpec((ts, CB), lambda j, i: (i, j))],
        out_shape=[jax.ShapeDtypeStruct((s, NST), F32), jax.ShapeDtypeStruct((s, NST), F32),
                   jax.ShapeDtypeStruct((s, SW), F32), jax.ShapeDtypeStruct((s, SW), BF16)],
        scratch_shapes=[pltpu.VMEM((GRP, SB), F32), pltpu.VMEM((GRP, SB), F32)],
        compiler_params=_params(("parallel", "arbitrary")),
    )(z, wb_re, wb_im, wc, e_re, e_im, dvec)


def ssm_bwd(dgl, ypre, z, xs_re, xs_im, wbt_re, wbt_im, wct, e_re, e_im, dvec):
    s = z.shape[0]
    ts = SSM_TS
    nt = s // ts
    ucol0 = 2 * CW // CB
    tn_dims = (((0,), (0,)), ((), ()))

    def body(dgl_ref, y_ref, u_ref, xr_ref, xi_ref, wbtr_ref, wbti_ref, wct_ref, er_ref, ei_ref, d_ref,
             du_ref, dd_ref, dar_ref, dai_ref, dwbr_ref, dwbi_ref, dwc_ref,
             lr_ref, li_ref, car_r, car_i, acc_r, acc_i):
        i = pl.program_id(1)

        @pl.when(i == 0)
        def _():
            for ref in (car_r, car_i, acc_r, acc_i, dd_ref, dwbr_ref, dwbi_ref, dwc_ref):
                ref[...] = jnp.zeros_like(ref)

        u = u_ref[...]
        y = y_ref[...]
        dy = dgl_ref[...] * _gelu_grad(y, _gelu(y)[1])
        dd_ref[0] += _colsum(dy * u)
        dyb = dy.astype(BF16)
        dxo = jnp.dot(dyb, wct_ref[0], preferred_element_type=F32)
        lr_ref[...] = dxo[:, 0:SB]
        li_ref[...] = dxo[:, SB:2 * SB]
        steps, ctab = _scan_tables(er_ref[0], -ei_ref[0], True)
        row = lax.broadcasted_iota(jnp.int32, (GRP, SB), 0)

        def grp(q, carry):
            cr, ci, ar, ai = carry
            r0 = pl.multiple_of((ts // GRP - 1 - q) * GRP, GRP)
            lr, li = _scan_group(lr_ref[pl.ds(r0, GRP), :], li_ref[pl.ds(r0, GRP), :], steps, ctab, cr, ci, True)
            lr_ref[pl.ds(r0, GRP), :] = lr
            li_ref[pl.ds(r0, GRP), :] = li
            nr = jnp.where(row == GRP - 1, cr, pltpu.roll(lr, GRP - 1, 0))
            ni = jnp.where(row == GRP - 1, ci, pltpu.roll(li, GRP - 1, 0))
            xr = xr_ref[pl.ds(r0, GRP), :]
            xi = xi_ref[pl.ds(r0, GRP), :]
            ar = ar + nr * xr + ni * xi
            ai = ai + ni * xr - nr * xi
            return (jnp.broadcast_to(lr[0:1, :], (GRP, SB)), jnp.broadcast_to(li[0:1, :], (GRP, SB)), ar, ai)

        cr, ci, ar, ai = lax.fori_loop(0, ts // GRP, grp, (car_r[...], car_i[...], acc_r[...], acc_i[...]))
        car_r[...] = cr
        car_i[...] = ci
        acc_r[...] = ar
        acc_i[...] = ai

        @pl.when(i == nt - 1)
        def _():
            dar_ref[0] = _colsum(ar)
            dai_ref[0] = _colsum(ai)

        lrb = lr_ref[...].astype(BF16)
        lib = li_ref[...].astype(BF16)
        du = (jnp.dot(lrb, wbtr_ref[0], preferred_element_type=F32)
              + jnp.dot(lib, wbti_ref[0], preferred_element_type=F32) + d_ref[0] * dy)
        du_ref[...] = du.astype(du_ref.dtype)
        ub = u.astype(BF16)
        dwbr_ref[0] += lax.dot_general(ub, lrb, tn_dims, preferred_element_type=F32)
        dwbi_ref[0] += lax.dot_general(ub, lib, tn_dims, preferred_element_type=F32)
        dwc_ref[0, 0:SB, :] += lax.dot_general(xr_ref[...].astype(BF16), dyb, tn_dims, preferred_element_type=F32)
        dwc_ref[0, SB:2 * SB, :] += lax.dot_general(xi_ref[...].astype(BF16), dyb, tn_dims, preferred_element_type=F32)

    rev = lambda i: nt - 1 - i
    blk3 = lambda a: pl.BlockSpec((1,) + a.shape[1:], lambda j, i: (j, 0, 0))
    acc3 = lambda r, c: pl.BlockSpec((1, r, c), lambda j, i: (j, 0, 0))
    return pl.pallas_call(
        body, name="ssm_bwd", grid=(NBLK, nt),
        in_specs=[pl.BlockSpec((ts, CB), lambda j, i: (rev(i), j)), pl.BlockSpec((ts, CB), lambda j, i: (rev(i), j)),
                  pl.BlockSpec((ts, CB), lambda j, i: (rev(i), ucol0 + j)),
                  pl.BlockSpec((ts, SB), lambda j, i: (rev(i), j)), pl.BlockSpec((ts, SB), lambda j, i: (rev(i), j)),
                  blk3(wbt_re), blk3(wbt_im), blk3(wct), blk3(e_re), blk3(e_im), blk3(dvec)],
        out_specs=[pl.BlockSpec((ts, CB), lambda j, i: (rev(i), j)),
                   acc3(1, CB), acc3(1, SB), acc3(1, SB), acc3(CB, SB), acc3(CB, SB), acc3(2 * SB, CB)],
        out_shape=[jax.ShapeDtypeStruct((s, SW), BF16),
                   jax.ShapeDtypeStruct((NBLK, 1, CB), F32),
                   jax.ShapeDtypeStruct((NBLK, 1, SB), F32), jax.ShapeDtypeStruct((NBLK, 1, SB), F32),
                   jax.ShapeDtypeStruct((NBLK, CB, SB), F32), jax.ShapeDtypeStruct((NBLK, CB, SB), F32),
                   jax.ShapeDtypeStruct((NBLK, 2 * SB, CB), F32)],
        scratch_shapes=[pltpu.VMEM((ts, SB), F32), pltpu.VMEM((ts, SB), F32)] + [pltpu.VMEM((GRP, SB), F32)] * 4,
        compiler_params=_params(("parallel", "arbitrary")),
    )(dgl, ypre, z, xs_re, xs_im, wbt_re, wbt_im, wct, e_re, e_im, dvec)


def _disc(a_re, a_im, log_dt, b_re, b_im, expand):
    dt = jnp.dot(expand, jnp.exp(log_dt), preferred_element_type=F32, precision=lax.Precision.HIGHEST)
    mag = jnp.exp(dt * a_re)
    e_re, e_im = mag * jnp.cos(dt * a_im), mag * jnp.sin(dt * a_im)
    n_re, n_im = e_re - 1.0, e_im
    den = a_re * a_re + a_im * a_im
    q_re = (n_re * a_re + n_im * a_im) / den
    q_im = (n_im * a_re - n_re * a_im) / den
    return e_re, e_im, q_re * b_re - q_im * b_im, q_re * b_im + q_im * b_re


def _whole(a):
    return pl.BlockSpec(a.shape, functools.partial(lambda n: (0,) * n, n=a.ndim))


def disc_fwd(a_re, a_im, log_dt, b_re, b_im, expand):
    def body(ar, ai, ld, br, bi, ex, er_o, ei_o, bbr_o, bbi_o):
        er, ei, bbr, bbi = _disc(ar[...], ai[...], ld[...], br[...], bi[...], ex[...])
        er_o[...] = er
        ei_o[...] = ei
        bbr_o[...] = bbr
        bbi_o[...] = bbi

    ins = (a_re, a_im, log_dt, b_re, b_im, expand)
    outs = [jax.ShapeDtypeStruct(a_re.shape, F32)] * 2 + [jax.ShapeDtypeStruct(b_re.shape, F32)] * 2
    return pl.pallas_call(body, name="disc_fwd", in_specs=[_whole(a) for a in ins],
                          out_specs=[_whole(o) for o in outs], out_shape=outs, compiler_params=_params())(*ins)


def disc_bwd(a_re, a_im, log_dt, b_re, b_im, expand, de_re, de_im, dbb_re, dbb_im):
    def body(ar, ai, ld, br, bi, ex, der, dei, dbr, dbi, o_ar, o_ai, o_ld, o_br, o_bi):
        exv = ex[...]
        _, vjp = jax.vjp(lambda *p: _disc(*p, exv), ar[...], ai[...], ld[...], br[...], bi[...])
        g = vjp((der[...], dei[...], dbr[...], dbi[...]))
        for o, v in zip((o_ar, o_ai, o_ld, o_br, o_bi), g):
            o[...] = v

    ins = (a_re, a_im, log_dt, b_re, b_im, expand, de_re, de_im, dbb_re, dbb_im)
    outs = [jax.ShapeDtypeStruct(a.shape, F32) for a in (a_re, a_im, log_dt, b_re, b_im)]
    return pl.pallas_call(body, name="disc_bwd", in_specs=[_whole(a) for a in ins],
                          out_specs=[_whole(o) for o in outs], out_shape=outs, compiler_params=_params())(*ins)


def mod_fwd(c_all, w_ada, b_cols):
    def body(c_ref, w_ref, b_ref, act_ref, mod_ref):
        cv = c_ref[...]
        act = cv * _sig(cv)
        act_ref[...] = act
        mod_ref[...] = jnp.dot(act, w_ref[...], preferred_element_type=F32, precision=lax.Precision.HIGHEST) + b_ref[...]

    ins = (c_all, w_ada, b_cols)
    outs = [jax.ShapeDtypeStruct(c_all.shape, F32), jax.ShapeDtypeStruct((NDEV, w_ada.shape[1]), F32)]
    return pl.pallas_call(body, name="mod_fwd", in_specs=[_whole(a) for a in ins],
                          out_specs=[_whole(o) for o in outs], out_shape=outs, compiler_params=_params())(*ins)


def ada_grad(act_all, dmod_cols):
    def body(a_ref, d_ref, o_ref):
        o_ref[...] = lax.dot_general(a_ref[...], d_ref[...], (((0,), (0,)), ((), ())),
                                     preferred_element_type=F32, precision=lax.Precision.HIGHEST)

    out = jax.ShapeDtypeStruct((act_all.shape[1], dmod_cols.shape[1]), F32)
    return pl.pallas_call(body, name="ada_grad", in_specs=[_whole(act_all), _whole(dmod_cols)],
                          out_specs=_whole(out), out_shape=out, compiler_params=_params())(act_all, dmod_cols)


def _adam_math(w, g, m, v):
    m2 = ADAM_B1 * m + (1.0 - ADAM_B1) * g
    v2 = ADAM_B2 * v + (1.0 - ADAM_B2) * (g * g)
    m_hat = m2 / (1.0 - ADAM_B1 ** ADAM_STEP)
    v_hat = v2 / (1.0 - ADAM_B2 ** ADAM_STEP)
    delta = -ADAM_LR * (m_hat / (jnp.sqrt(v_hat) + ADAM_EPS) + ADAM_WD * w)
    return delta, m2, v2


def adam(name, w, g, m, v):
    r, c = w.shape
    tr = r
    for cand in (256, 128, 64, 32, 16, 8):
        if r % cand == 0 and r > cand:
            tr = cand
            break

    def body(w_ref, g_ref, m_ref, v_ref, d_o, m_o, v_o):
        d, m2, v2 = _adam_math(w_ref[...], g_ref[...], m_ref[...], v_ref[...])
        d_o[...] = d
        m_o[...] = m2
        v_o[...] = v2

    spec = pl.BlockSpec((tr, c), lambda i: (i, 0))
    out = jax.ShapeDtypeStruct((r, c), F32)
    return pl.pallas_call(body, name=name, grid=(r // tr,), in_specs=[spec] * 4, out_specs=[spec] * 3,
                          out_shape=[out] * 3, compiler_params=_params(("parallel",)))(w, g, m, v)


def sum_slots(name, slots):
    n, r, c = slots.shape
    tr = r
    for cand in (256, 128, 64, 32, 16, 8):
        if r % cand == 0 and r > cand:
            tr = cand
            break

    def body(s_ref, o_ref):
        acc = s_ref[0]
        for q in range(1, n):
            acc = acc + s_ref[q]
        o_ref[...] = acc

    return pl.pallas_call(body, name=name, grid=(r // tr,),
                          in_specs=[pl.BlockSpec((n, tr, c), lambda i: (0, i, 0))],
                          out_specs=pl.BlockSpec((tr, c), lambda i: (i, 0)),
                          out_shape=jax.ShapeDtypeStruct((r, c), F32), compiler_params=_params(("parallel",)))(slots)


HBM_SPEC = pl.BlockSpec(memory_space=pltpu.HBM)


def _coords():
    return lax.axis_index("x"), lax.axis_index("y"), lax.axis_index("c")


def _linear(x, y, c):
    return 4 * x + 2 * y + c


def all_gather(name, shard):
    r, c = shard.shape

    def body(x_ref, out_ref, send_sems, recv_sems, local_sem):
        x, y, cc = _coords()
        me, sibling = (x, y, cc), (x, y, 1 - cc)
        chips = [(1 - x, y), (x, 1 - y), (1 - x, 1 - y)]

        def slot(px, py, pc):
            return out_ref.at[_linear(px, py, pc)]

        def copy(k, block, to, src=None):
            return pltpu.make_async_remote_copy(
                src_ref=slot(*block) if src is None else src, dst_ref=slot(*block),
                send_sem=send_sems.at[k], recv_sem=recv_sems.at[k], device_id=to, device_id_type=MESH)

        mine = pltpu.make_async_copy(x_ref, slot(*me), local_sem)
        mine.start()
        first = [copy(0, me, sibling, src=x_ref)]
        first += [copy(1 + j, me, (*chip, cc), src=x_ref) for j, chip in enumerate(chips)]
        for cp in first:
            cp.start()
        passed = [copy(4 + j, (*chip, cc), sibling) for j, chip in enumerate(chips)]
        for j, chip in enumerate(chips):
            copy(1 + j, (*chip, cc), me).wait_recv()
            passed[j].start()
        copy(0, sibling, me).wait_recv()
        for j, chip in enumerate(chips):
            copy(4 + j, (*chip, 1 - cc), me).wait_recv()
        for cp in first + passed:
            cp.wait_send()
        mine.wait()

    return pl.pallas_call(
        body, name=name, in_specs=[HBM_SPEC], out_specs=HBM_SPEC,
        out_shape=jax.ShapeDtypeStruct((NDEV, r, c), shard.dtype),
        scratch_shapes=[pltpu.SemaphoreType.DMA((7,)), pltpu.SemaphoreType.DMA((7,)), pltpu.SemaphoreType.DMA],
    )(shard)


def scatter_blocks(name, blocks):
    n, r, c = blocks.shape

    def body(in_ref, out_ref, send_sems, recv_sems, local_sem):
        x, y, cc = _coords()
        me = _linear(x, y, cc)
        mine = pltpu.make_async_copy(in_ref.at[me], out_ref.at[me], local_sem)
        mine.start()
        copies = []
        for k in range(1, NDEV):
            fx, fy, fc = (k >> 2) & 1, (k >> 1) & 1, k & 1
            px = x + fx - 2 * fx * x
            py = y + fy - 2 * fy * y
            pc = cc + fc - 2 * fc * cc
            cp = pltpu.make_async_remote_copy(
                src_ref=in_ref.at[_linear(px, py, pc)], dst_ref=out_ref.at[me],
                send_sem=send_sems.at[k - 1], recv_sem=recv_sems.at[k - 1], device_id=(px, py, pc), device_id_type=MESH)
            cp.start()
            copies.append(cp)
        for cp in copies:
            cp.wait_recv()
        for cp in copies:
            cp.wait_send()
        mine.wait()

    return pl.pallas_call(
        body, name=name, in_specs=[HBM_SPEC], out_specs=HBM_SPEC,
        out_shape=jax.ShapeDtypeStruct((NDEV, r, c), blocks.dtype),
        scratch_shapes=[pltpu.SemaphoreType.DMA((7,)), pltpu.SemaphoreType.DMA((7,)), pltpu.SemaphoreType.DMA],
    )(blocks)


def _block_diag(w, rows_per, cols_per):
    w = w.reshape(NBLK, 8, rows_per, cols_per)
    eye = jnp.eye(8, dtype=w.dtype)
    out = w[:, :, :, None, :] * eye[None, :, None, :, None]
    return out.reshape(NBLK, 8 * rows_per, 8 * cols_per)


def _diag_blocks(wd, rows_per, cols_per):
    wd = wd.reshape(NBLK, 8, rows_per, 8, cols_per)
    idx = jnp.arange(8)
    return wd[:, idx, :, idx, :].transpose(1, 0, 2, 3).reshape(NG, rows_per, cols_per)


def _pad_rows(v, mult):
    n = v.shape[0]
    return jnp.pad(v, (0, (-n) % mult))


def kernel(x, c, w_ada, b_ada, norm1_g, w_in, conv_w, conv_b, conv_ln_g, conv_ln_b, conv_proj, ssm_a_re, ssm_a_im, ssm_b_re, ssm_b_im, ssm_c_re, ssm_c_im, ssm_d, ssm_log_dt, ssm_glu, w_out, norm2_g, w_ffn_in, w_ffn_out, final_g, loss_target, m_w_ada, m_b_ada, m_norm1_g, m_w_in, m_conv_w, m_conv_b, m_conv_ln_g, m_conv_ln_b, m_conv_proj, m_ssm_a_re, m_ssm_a_im, m_ssm_b_re, m_ssm_b_im, m_ssm_c_re, m_ssm_c_im, m_ssm_d, m_ssm_log_dt, m_ssm_glu, m_w_out, m_norm2_g, m_w_ffn_in, m_w_ffn_out, m_final_g, v_w_ada, v_b_ada, v_norm1_g, v_w_in, v_conv_w, v_conv_b, v_conv_ln_g, v_conv_ln_b, v_conv_proj, v_ssm_a_re, v_ssm_a_im, v_ssm_b_re, v_ssm_b_im, v_ssm_c_re, v_ssm_c_im, v_ssm_d, v_ssm_log_dt, v_ssm_glu, v_w_out, v_norm2_g, v_w_ffn_in, v_w_ffn_out, v_final_g):
    me = _linear(*_coords())
    xs = x[0]
    tgt = loss_target[0]
    seq = xs.shape[0]

    parts = [w_in[0].T, conv_proj[0].T.reshape(-1, D), ssm_glu[0].T.reshape(-1, D), w_out[0],
             w_ffn_in[0].T, w_ffn_out[0]]
    part_rows = [p.shape[0] for p in parts]
    offs = [0]
    for r in part_rows:
        offs.append(offs[-1] + r)
    gathered = all_gather("gather_weights", jnp.concatenate(parts, axis=0).astype(BF16))

    def full(q, cols):
        return gathered[:, offs[q]:offs[q + 1], :].reshape(-1, cols)

    w_in_t = full(0, D)
    conv_proj_t = full(1, CW)
    ssm_glu_t = full(2, SW)
    w_out_f = full(3, D)
    w_ffn_in_t = full(4, D)
    w_ffn_out_f = full(5, D)

    ncol = w_ada.shape[2]
    c_all = all_gather("gather_c", c).reshape(NDEV, D)
    b_cols = lax.dynamic_slice_in_dim(b_ada, me * ncol, ncol, axis=1)
    act_all, mod_cols = mod_fwd(c_all, w_ada[0], b_cols)
    mod_all = all_gather("gather_mod", mod_cols)
    mod = lax.dynamic_index_in_dim(mod_all, me, axis=1, keepdims=False).reshape(NMOD, D)
    sh1, sc1, g1, sh2, sc2, g2 = [mod[q:q + 1] for q in range(NMOD)]

    expand = jnp.repeat(jnp.eye(NG, dtype=F32), NP, axis=0)
    a_re_c, a_im_c = ssm_a_re.reshape(NST, 1), ssm_a_im.reshape(NST, 1)
    ldt_c = ssm_log_dt.reshape(NG, 1)
    b_re_r, b_im_r = ssm_b_re.reshape(NST, GH), ssm_b_im.reshape(NST, GH)
    e_re, e_im, bb_re, bb_im = disc_fwd(a_re_c, a_im_c, ldt_c, b_re_r, b_im_r, expand)
    e_re_b, e_im_b = e_re.reshape(NBLK, 1, SB), e_im.reshape(NBLK, 1, SB)
    bb_re_g, bb_im_g = bb_re.reshape(NG, NP, GH), bb_im.reshape(NG, NP, GH)
    wbt_re = _block_diag(bb_re_g, NP, GH)
    wbt_im = _block_diag(bb_im_g, NP, GH)
    wb_re, wb_im = wbt_re.transpose(0, 2, 1), wbt_im.transpose(0, 2, 1)
    wct = jnp.concatenate([_block_diag(ssm_c_re[0], GH, NP), -_block_diag(ssm_c_im[0], GH, NP)], axis=2)
    wc = wct.transpose(0, 2, 1)
    to_b = lambda a: a.astype(BF16)
    dvec = ssm_d.reshape(NBLK, 1, CB)

    n1g = norm1_g

    def f_norm1(xv, g, sc, sh):
        _, xh = _rms_stats(xv)
        return [xh * g * (1.0 + sc) + sh], []

    (h1,) = rowwise("norm1", f_norm1, [xs], [n1g, sc1, sh1], [(D, BF16)], [], 512)
    z = mm("mm_in", h1, w_in_t, "nt")

    cw_g = all_gather("gather_conv_w", conv_w[0])
    conv_w_full = cw_g.transpose(1, 0, 2).reshape(KC, CW)
    w32 = jnp.pad(conv_w_full, ((0, HALO - KC), (0, 0)))
    yc, s_act = conv_fwd(z, w32, conv_b, conv_ln_g, conv_ln_b)
    y_conv = mm("mm_conv_proj", s_act, conv_proj_t, "nt")

    xs_re, xs_im, ypre, gl = ssm_fwd(z, to_b(wb_re), to_b(wb_im), to_b(wc), e_re_b, e_im_b, dvec)
    z2 = mm("mm_ssm_glu", gl, ssm_glu_t, "nt")

    def f_merge(yc_v, za, zb, glc0, glc1, gls0, gls1):
        glc = jnp.concatenate([glc0, glc1], axis=1)
        gls = jnp.concatenate([gls0, gls1], axis=1)
        return [_sig(glc) * yc_v + _sig(gls) * (za * _sig(zb))], []

    (merged,) = rowwise("merge", f_merge,
                        [y_conv, (z2, D, 0), (z2, D, 1), (z, CW, 3), (z, CW, 4), (z, CW, 5), (z, CW, 6)],
                        [], [(D, BF16)], [], 512)
    o1 = mm("mm_out", merged, w_out_f, "nn")

    def f_norm2(xv, o1v, g1v, g, sc, sh):
        x1v = xv + g1v * o1v
        _, xh = _rms_stats(x1v)
        return [x1v, xh * g * (1.0 + sc) + sh], []

    x1, h2 = rowwise("norm2", f_norm2, [xs, o1], [g1, norm2_g, sc2, sh2], [(D, F32), (D, BF16)], [], 512)
    f = mm("mm_ffn_in", h2, w_ffn_in_t, "nt")

    def f_swiglu(fg, fu):
        return [fg * _sig(fg) * fu], []

    (act,) = rowwise("swiglu", f_swiglu, [(f, FH, 0), (f, FH, 1)], [], [(FH, BF16)], [], 256)
    o2 = mm("mm_ffn_out", act, w_ffn_out_f, "nn")

    fg_row = final_g.reshape(1, D)

    def f_final(x1v, o2v, tv, g2v, fg):
        x2v = x1v + g2v * o2v
        r, xh = _rms_stats(x2v)
        yv = xh * fg
        err = yv - tv
        loss = jnp.sum(_colsum(err * err), axis=1, keepdims=True) * (0.5 / D)
        dy = err * (1.0 / D)
        dx2 = _rms_bwd(dy * fg, xh, r)
        return ([dx2, g2v * dx2],
                [jnp.broadcast_to(loss, (1, LANE)), _colsum(dy * xh), _colsum(dx2 * o2v)])

    dx2, do2, loss_l, d_final_g, d_g2 = rowwise(
        "final", f_final, [x1, o2, tgt], [g2, fg_row], [(D, F32), (D, BF16)], [LANE, D, D], 256)

    dact = mm("mm_dact", do2, w_ffn_out_f, "nt")
    g_ffn_out = mm("mm_g_ffn_out", act, do2, "tn")

    def f_dswiglu(fg, fu, da):
        sg = _sig(fg)
        return [jnp.concatenate([da * fu * (sg * (1.0 + fg * (1.0 - sg))), da * (fg * sg)], axis=1)], []

    (df,) = rowwise("dswiglu", f_dswiglu, [(f, FH, 0), (f, FH, 1), dact], [], [(2 * FH, BF16)], [], 256)
    dh2 = mm("mm_dh2", df, w_ffn_in_t, "nn")
    g_ffn_in_t = mm("mm_g_ffn_in", df, h2, "tn")

    def f_dnorm2(dh, x1v, dx2v, o1v, g, sc, g1v):
        r, xh = _rms_stats(x1v)
        dxh = dh * (1.0 + sc) * g
        dx1 = dx2v + _rms_bwd(dxh, xh, r)
        return ([dx1, g1v * dx1],
                [_colsum(dh * xh * g), _colsum(dh), _colsum(dh * (1.0 + sc) * xh), _colsum(dx1 * o1v)])

    dx1, do1, d_sc2, d_sh2, d_n2g, d_g1 = rowwise(
        "dnorm2", f_dnorm2, [dh2, x1, dx2, o1], [norm2_g, sc2, g1], [(D, F32), (D, BF16)], [D, D, D, D], 256)

    dmerged = mm("mm_dmerged", do1, w_out_f, "nt")
    g_out = mm("mm_g_out", merged, do1, "tn")

    def f_dmerge(dm, yc_v, za, zb, glc0, glc1, gls0, gls1):
        sc_ = _sig(jnp.concatenate([glc0, glc1], axis=1))
        ss_ = _sig(jnp.concatenate([gls0, gls1], axis=1))
        sb_ = _sig(zb)
        dys = dm * ss_
        dz2 = jnp.concatenate([dys * sb_, dys * za * sb_ * (1.0 - sb_)], axis=1)
        dgl = jnp.concatenate([dm * yc_v * sc_ * (1.0 - sc_), dm * (za * sb_) * ss_ * (1.0 - ss_)], axis=1)
        return [dm * sc_, dz2, dgl], []

    dyconv, dz2, dz_gates = rowwise(
        "dmerge", f_dmerge,
        [dmerged, y_conv, (z2, D, 0), (z2, D, 1), (z, CW, 3), (z, CW, 4), (z, CW, 5), (z, CW, 6)],
        [], [(D, BF16), (2 * D, BF16), (2 * D, BF16)], [], 256)

    ds = mm("mm_ds", dyconv, conv_proj_t, "nn")
    g_conv_proj_t = mm("mm_g_conv_proj", dyconv, s_act, "tn")
    dz_conv, d_lng, d_lnb, d_cb, d_cw32 = conv_bwd(ds, yc, z, w32, conv_ln_g, conv_ln_b)

    dgl = mm("mm_dgl", dz2, ssm_glu_t, "nn")
    g_ssm_glu_t = mm("mm_g_ssm_glu", dz2, gl, "tn")
    dz_ssm, d_d, d_ar, d_ai, d_wb_re, d_wb_im, d_wc = ssm_bwd(
        dgl, ypre, z, xs_re, xs_im, to_b(wbt_re), to_b(wbt_im), to_b(wct), e_re_b, e_im_b, dvec)

    dz = jnp.concatenate([dz_conv, dz_ssm, dz_gates], axis=1)
    dh1 = mm("mm_dh1", dz, w_in_t, "nn")
    g_in_t = mm("mm_g_in", dz, h1, "tn")

    def f_dnorm1(dh, xv, dx1v, g, sc):
        r, xh = _rms_stats(xv)
        dxh = dh * (1.0 + sc) * g
        return ([dx1v + _rms_bwd(dxh, xh, r)],
                [_colsum(dh * xh * g), _colsum(dh), _colsum(dh * (1.0 + sc) * xh)])

    grad_x, d_sc1, d_sh1, d_n1g = rowwise(
        "dnorm1", f_dnorm1, [dh1, xs, dx1], [n1g, sc1], [(D, F32)], [D, D, D], 256)

    d_bb_re = _diag_blocks(d_wb_re.transpose(0, 2, 1), NP, GH).reshape(NST, GH)
    d_bb_im = _diag_blocks(d_wb_im.transpose(0, 2, 1), NP, GH).reshape(NST, GH)
    d_wct = d_wc.transpose(0, 2, 1)
    d_c_re = _diag_blocks(d_wct[:, :, 0:SB], GH, NP)
    d_c_im = -_diag_blocks(d_wct[:, :, SB:2 * SB], GH, NP)
    d_a_re, d_a_im, d_ldt, d_b_re, d_b_im = disc_bwd(
        a_re_c, a_im_c, ldt_c, b_re_r, b_im_r, expand, d_ar.reshape(NST, 1), d_ai.reshape(NST, 1), d_bb_re, d_bb_im)

    dmod = jnp.concatenate([d_sh1, d_sc1, d_g1, d_sh2, d_sc2, d_g2], axis=1)
    small_local = [dmod.reshape(-1), d_n1g.reshape(-1), d_cw32[0:KC].reshape(-1), d_cb.reshape(-1), d_lng.reshape(-1),
                   d_lnb.reshape(-1), d_a_re.reshape(-1), d_a_im.reshape(-1), d_b_re.reshape(-1), d_b_im.reshape(-1),
                   d_c_re.reshape(-1), d_c_im.reshape(-1), d_d.reshape(-1), d_ldt.reshape(-1), d_n2g.reshape(-1),
                   d_final_g.reshape(-1)]
    small_sizes = [v.shape[0] for v in small_local]
    packed = _pad_rows(jnp.concatenate(small_local), 256 * LANE).reshape(-1, LANE)
    small_all = all_gather("gather_small", packed)
    small_sum = sum_slots("sum_small", small_all).reshape(-1)
    pieces, pos = [], 0
    for n in small_sizes:
        pieces.append(small_sum[pos:pos + n])
        pos += n
    (g_b_ada, g_n1g, g_cw_full, g_cb, g_lng, g_lnb, g_a_re, g_a_im, g_b_re, g_b_im, g_c_re, g_c_im, g_d, g_ldt,
     g_n2g, g_fg) = pieces
    dmod_all = small_all.reshape(NDEV, -1)[:, 0:NMOD * D]
    g_w_ada = ada_grad(act_all, lax.dynamic_slice_in_dim(dmod_all, me * ncol, ncol, axis=1))
    ccol = conv_w.shape[2]
    g_conv_w = lax.dynamic_slice_in_dim(g_cw_full.reshape(KC, CW), me * ccol, ccol, axis=1)

    big_t = [g_in_t, g_conv_proj_t.reshape(-1, D), g_ssm_glu_t.reshape(-1, D), g_out, g_ffn_in_t, g_ffn_out]
    blocks = jnp.concatenate([g.reshape(NDEV, -1, D) for g in big_t], axis=1)
    recv = scatter_blocks("scatter_grads", blocks)
    gsum = sum_slots("sum_grads", recv)

    def mine(q, cols):
        return gsum[offs[q]:offs[q + 1]].reshape(-1, cols)

    g_w_in = mine(0, D).T
    g_conv_proj = mine(1, CW).T
    g_ssm_glu = mine(2, SW).T
    g_w_out = mine(3, D)
    g_w_ffn_in = mine(4, D).T
    g_w_ffn_out = mine(5, D)

    loss = lax.psum(loss_l[0, 0], ("x", "y", "c"))

    grads = {
        "w_ada": g_w_ada[None], "b_ada": g_b_ada.reshape(b_ada.shape), "norm1_g": g_n1g.reshape(norm1_g.shape),
        "w_in": g_w_in[None], "conv_w": g_conv_w[None], "conv_b": g_cb.reshape(conv_b.shape),
        "conv_ln_g": g_lng.reshape(conv_ln_g.shape), "conv_ln_b": g_lnb.reshape(conv_ln_b.shape),
        "conv_proj": g_conv_proj[None], "ssm_a_re": g_a_re.reshape(ssm_a_re.shape),
        "ssm_a_im": g_a_im.reshape(ssm_a_im.shape), "ssm_b_re": g_b_re.reshape(ssm_b_re.shape),
        "ssm_b_im": g_b_im.reshape(ssm_b_im.shape), "ssm_c_re": g_c_re.reshape(ssm_c_re.shape),
        "ssm_c_im": g_c_im.reshape(ssm_c_im.shape), "ssm_d": g_d.reshape(ssm_d.shape),
        "ssm_log_dt": g_ldt.reshape(ssm_log_dt.shape), "ssm_glu": g_ssm_glu[None], "w_out": g_w_out[None],
        "norm2_g": g_n2g.reshape(norm2_g.shape), "w_ffn_in": g_w_ffn_in[None], "w_ffn_out": g_w_ffn_out[None],
        "final_g": g_fg.reshape(final_g.shape),
    }
    weights = {
        "w_ada": (w_ada, m_w_ada, v_w_ada), "b_ada": (b_ada, m_b_ada, v_b_ada), "norm1_g": (norm1_g, m_norm1_g, v_norm1_g),
        "w_in": (w_in, m_w_in, v_w_in), "conv_w": (conv_w, m_conv_w, v_conv_w), "conv_b": (conv_b, m_conv_b, v_conv_b),
        "conv_ln_g": (conv_ln_g, m_conv_ln_g, v_conv_ln_g), "conv_ln_b": (conv_ln_b, m_conv_ln_b, v_conv_ln_b),
        "conv_proj": (conv_proj, m_conv_proj, v_conv_proj), "ssm_a_re": (ssm_a_re, m_ssm_a_re, v_ssm_a_re),
        "ssm_a_im": (ssm_a_im, m_ssm_a_im, v_ssm_a_im), "ssm_b_re": (ssm_b_re, m_ssm_b_re, v_ssm_b_re),
        "ssm_b_im": (ssm_b_im, m_ssm_b_im, v_ssm_b_im), "ssm_c_re": (ssm_c_re, m_ssm_c_re, v_ssm_c_re),
        "ssm_c_im": (ssm_c_im, m_ssm_c_im, v_ssm_c_im), "ssm_d": (ssm_d, m_ssm_d, v_ssm_d),
        "ssm_log_dt": (ssm_log_dt, m_ssm_log_dt, v_ssm_log_dt), "ssm_glu": (ssm_glu, m_ssm_glu, v_ssm_glu),
        "w_out": (w_out, m_w_out, v_w_out), "norm2_g": (norm2_g, m_norm2_g, v_norm2_g),
        "w_ffn_in": (w_ffn_in, m_w_ffn_in, v_w_ffn_in), "w_ffn_out": (w_ffn_out, m_w_ffn_out, v_w_ffn_out),
        "final_g": (final_g, m_final_g, v_final_g),
    }
    order = list(weights)
    big = ("w_ada", "w_in", "conv_proj", "ssm_glu", "w_out", "w_ffn_in", "w_ffn_out")
    delta, new_m, new_v = {}, {}, {}
    for n in big:
        wv, mv, vv = weights[n]
        shp = wv.shape
        d_, m_, v_ = adam("adam_" + n, wv.reshape(shp[-2:]), grads[n].reshape(shp[-2:]), mv.reshape(shp[-2:]),
                          vv.reshape(shp[-2:]))
        delta[n], new_m[n], new_v[n] = d_.reshape(shp), m_.reshape(shp), v_.reshape(shp)
    small = [n for n in order if n not in big]
    sizes = [weights[n][0].size for n in small]

    def pack(idx_or_grad):
        vs = [(grads[n] if idx_or_grad is None else weights[n][idx_or_grad]).reshape(-1) for n in small]
        return _pad_rows(jnp.concatenate(vs), 256 * LANE).reshape(-1, LANE)

    d_p, m_p, v_p = adam("adam_small", pack(0), pack(None), pack(1), pack(2))
    d_p, m_p, v_p = d_p.reshape(-1), m_p.reshape(-1), v_p.reshape(-1)
    pos = 0
    for n, sz in zip(small, sizes):
        shp = weights[n][0].shape
        delta[n] = d_p[pos:pos + sz].reshape(shp)
        new_m[n] = m_p[pos:pos + sz].reshape(shp)
        new_v[n] = v_p[pos:pos + sz].reshape(shp)
        pos += sz

    return (loss, grad_x[None], *[grads[n] for n in order], *[delta[n] for n in order],
            *[new_m[n] for n in order], *[new_v[n] for n in order])
```

```python
import functools
import math

import jax
import jax.numpy as jnp
from jax import lax
from jax.experimental import pallas as pl
from jax.experimental.pallas import tpu as pltpu

F32 = jnp.float32
BF16 = jnp.bfloat16

D = 1024
CW = 512
KC = 31
SW = 512
NG = 32
GH = 16
NP = 64
NST = NG * NP
FH = 2816
NMOD = 6
NDEV = 8
EPS = 1e-6
CB = 128
SB = 512
NBLK = SW // CB
HALO = 32
ZW = 2 * CW + SW + 2 * D
Z_ROT = (ZW // CW, 3)
ZB_A, ZB_G, ZB_U = 4, 5, 6

ADAM_LR = 0.001
ADAM_B1 = 0.9
ADAM_B2 = 0.999
ADAM_EPS = 1e-08
ADAM_WD = 0.01
ADAM_STEP = 10

V7X_VMEM_BYTES = 64 * 1024 * 1024
VMEM_LIMIT = V7X_VMEM_BYTES - 8 * 1024 * 1024
LANE = 128
MESH = pl.DeviceIdType.MESH


def _params(sem=None, **kw):
    if sem is not None:
        kw["dimension_semantics"] = sem
    return pltpu.CompilerParams(vmem_limit_bytes=VMEM_LIMIT, **kw)


def _tile(n, most):
    best = None
    for t in range(LANE, most + 1, LANE):
        if n % t == 0:
            best = t
    if best is None:
        raise ValueError(f"no tile for {n}")
    return best


def _sig(x):
    return jax.nn.sigmoid(x)


def mm(name, a, b, mode, out_dtype=F32, tiles=None, b_rot=None, o_rot=None):
    if mode == "nn":
        (m, k), (k2, n) = a.shape, b.shape
    elif mode == "nt":
        (m, k), (n, k2) = a.shape, b.shape
    else:
        (k, m), (k2, n) = a.shape, b.shape
    assert k == k2, (name, a.shape, b.shape)
    bm, bn, bk = tiles or (_tile(m, 1024), _tile(n, 1408), _tile(k, 1408 if k % 1408 == 0 else 1024))
    bm, bn, bk = min(bm, m), min(bn, n), min(bk, k)
    assert m % bm == 0 and n % bn == 0 and k % bk == 0, (name, m, n, k, bm, bn, bk)
    nk = k // bk
    rot = lambda idx, r: idx if r is None else (idx + r[1]) % r[0]
    if mode == "nn":
        a_spec = pl.BlockSpec((bm, bk), lambda i, j, kk: (i, kk))
        b_spec = pl.BlockSpec((bk, bn), lambda i, j, kk: (rot(kk, b_rot), j))
        dims = (((1,), (0,)), ((), ()))
    elif mode == "nt":
        a_spec = pl.BlockSpec((bm, bk), lambda i, j, kk: (i, kk))
        b_spec = pl.BlockSpec((bn, bk), lambda i, j, kk: (rot(j, b_rot), kk))
        dims = (((1,), (1,)), ((), ()))
    else:
        assert b_rot is None
        a_spec = pl.BlockSpec((bk, bm), lambda i, j, kk: (kk, i))
        b_spec = pl.BlockSpec((bk, bn), lambda i, j, kk: (kk, j))
        dims = (((0,), (0,)), ((), ()))

    def body(a_ref, b_ref, o_ref, acc_ref):
        kk = pl.program_id(2)

        @pl.when(kk == 0)
        def _():
            acc_ref[...] = jnp.zeros_like(acc_ref)

        acc_ref[...] += lax.dot_general(a_ref[...], b_ref[...], dims, preferred_element_type=F32)

        @pl.when(kk == nk - 1)
        def _():
            o_ref[...] = acc_ref[...].astype(o_ref.dtype)

    return pl.pallas_call(
        body, name=name,
        grid=(m // bm, n // bn, nk),
        in_specs=[a_spec, b_spec],
        out_specs=pl.BlockSpec((bm, bn), lambda i, j, kk: (rot(i, o_rot), j)),
        out_shape=jax.ShapeDtypeStruct((m, n), out_dtype),
        scratch_shapes=[pltpu.VMEM((bm, bn), F32)],
        compiler_params=_params(("parallel", "parallel", "arbitrary")),
    )(a, b)


def rowwise(name, fn, rows, consts, out_rows, out_sums, ts, alias=None):
    rows = [r if isinstance(r, tuple) else (r, r.shape[1], 0) for r in rows]
    out_rows = [o if len(o) == 4 else (o[0], o[1], o[0], 0) for o in out_rows]
    s = rows[0][0].shape[0]
    nt = s // ts
    nr, nc, no, ns = len(rows), len(consts), len(out_rows), len(out_sums)
    in_specs = [pl.BlockSpec((ts, w), functools.partial(lambda i, cb: (i, cb), cb=cb)) for (_, w, cb) in rows]
    in_specs += [pl.BlockSpec(c.shape, lambda i: (0, 0)) for c in consts]
    operands = [r[0] for r in rows] + list(consts)
    aliases = {}
    if alias is not None:
        in_specs.append(pl.BlockSpec(memory_space=pl.ANY))
        operands.append(alias[0])
        aliases = {nr + nc: alias[1]}
    out_shape = [jax.ShapeDtypeStruct((s, tw), dt) for (_, dt, tw, _) in out_rows]
    out_shape += [jax.ShapeDtypeStruct((1, w), F32) for w in out_sums]
    out_specs = [pl.BlockSpec((ts, w), functools.partial(lambda i, cb: (i, cb), cb=cb)) for (w, _, _, cb) in out_rows]
    out_specs += [pl.BlockSpec((1, w), lambda i: (0, 0)) for w in out_sums]
    n_in = len(operands)

    def body(*refs):
        ins, outs = refs[:nr + nc], refs[n_in:]
        i = pl.program_id(0)
        ro, so = fn(*[r[...] for r in ins])
        for q in range(no):
            outs[q][...] = ro[q].astype(outs[q].dtype)
        if ns:
            @pl.when(i == 0)
            def _():
                for q in range(ns):
                    outs[no + q][...] = jnp.zeros_like(outs[no + q])

            for q in range(ns):
                outs[no + q][...] += so[q]

    return pl.pallas_call(
        body, name=name, grid=(nt,),
        in_specs=in_specs, out_specs=out_specs, out_shape=out_shape, input_output_aliases=aliases,
        compiler_params=_params(("arbitrary",) if ns else ("parallel",)),
    )(*operands)


def _colsum(v):
    return jnp.sum(v, axis=0, keepdims=True)


def _rms_stats(xv):
    r = lax.rsqrt(jnp.mean(xv * xv, axis=-1, keepdims=True) + EPS)
    return r, xv * r


def _rms_bwd(dxhat, xhat, r):
    return r * (dxhat - xhat * jnp.mean(dxhat * xhat, axis=-1, keepdims=True))


def _gelu(v):
    k = math.sqrt(2.0 / math.pi)
    t = jnp.tanh(k * (v + 0.044715 * v * v * v))
    return 0.5 * v * (1.0 + t), t


def _gelu_grad(v, t):
    k = math.sqrt(2.0 / math.pi)
    return 0.5 * (1.0 + t) + 0.5 * v * (1.0 - t * t) * k * (1.0 + 3.0 * 0.044715 * v * v)


CONV_TS = 256
CONV_CH = 64


def _ln_fwd(yc, g, b):
    mu = jnp.mean(yc, axis=-1, keepdims=True)
    xc = yc - mu
    rstd = lax.rsqrt(jnp.mean(xc * xc, axis=-1, keepdims=True) + EPS)
    nhat = xc * rstd
    return nhat, rstd, nhat * g + b


def conv_fwd(z, w32, cb, lg, lb):
    s = z.shape[0]
    ts = CONV_TS
    nt = s // ts
    hb = ts // HALO

    def body(a_ref, g_ref, ah_ref, gh_ref, w_ref, cb_ref, lg_ref, lb_ref, yc_ref, s_ref, ubuf):
        i = pl.program_id(0)
        first = (i > 0).astype(F32)
        ubuf[0:HALO, :] = ah_ref[...] * _sig(gh_ref[...]) * first
        ubuf[HALO:HALO + ts, :] = a_ref[...] * _sig(g_ref[...])
        for c0 in range(0, ts, CONV_CH):
            acc = jnp.zeros((CONV_CH, CW), F32)
            for k in range(KC):
                acc = acc + w_ref[k:k + 1, :] * ubuf[pl.ds(c0 + k + 2, CONV_CH), :]
            yc = acc + cb_ref[...]
            yc_ref[c0:c0 + CONV_CH, :] = yc
            _, _, ln = _ln_fwd(yc, lg_ref[...], lb_ref[...])
            s_ref[c0:c0 + CONV_CH, :] = (ln * _sig(ln)).astype(s_ref.dtype)

    cur = lambda cbk: pl.BlockSpec((ts, CW), functools.partial(lambda i, q: (i, q), q=cbk))
    prev = lambda cbk: pl.BlockSpec((HALO, CW), functools.partial(lambda i, q: (jnp.maximum(i * hb - 1, 0), q), q=cbk))
    const = lambda a: pl.BlockSpec(a.shape, lambda i: (0, 0))
    return pl.pallas_call(
        body, name="conv_fwd", grid=(nt,),
        in_specs=[cur(ZB_A), cur(ZB_G), prev(ZB_A), prev(ZB_G), const(w32), const(cb), const(lg), const(lb)],
        out_specs=[pl.BlockSpec((ts, CW), lambda i: (i, 0)), pl.BlockSpec((ts, CW), lambda i: (i, 0))],
        out_shape=[jax.ShapeDtypeStruct((s, CW), F32), jax.ShapeDtypeStruct((s, CW), BF16)],
        scratch_shapes=[pltpu.VMEM((HALO + ts, CW), F32)],
        compiler_params=_params(("parallel",)),
    )(z, z, z, z, w32, cb, lg, lb)


def conv_bwd(ds, yc, z, w32, lg, lb, dz):
    s = z.shape[0]
    ts = CONV_TS
    nt = s // ts
    hb = ts // HALO
    last_hb = s // HALO - 1

    def ln_bwd(dsv, ycv, g, b):
        nhat, rstd, ln = _ln_fwd(ycv, g, b)
        sg = _sig(ln)
        dln = dsv * (sg * (1.0 + ln * (1.0 - sg)))
        dnh = dln * g
        dyc = rstd * (dnh - jnp.mean(dnh, axis=-1, keepdims=True)
                      - nhat * jnp.mean(dnh * nhat, axis=-1, keepdims=True))
        return dyc, dln, nhat

    def body(ds_ref, yc_ref, dsn_ref, ycn_ref, a_ref, g_ref, ah_ref, gh_ref, w_ref, lg_ref, lb_ref, dz_in,
             dz_ref, dlg_ref, dlb_ref, dcb_ref, dw_ref, dbuf, ubuf):
        i = pl.program_id(0)

        @pl.when(i == 0)
        def _():
            dlg_ref[...] = jnp.zeros_like(dlg_ref)
            dlb_ref[...] = jnp.zeros_like(dlb_ref)
            dcb_ref[...] = jnp.zeros_like(dcb_ref)
            dw_ref[...] = jnp.zeros_like(dw_ref)

        lg, lb = lg_ref[...], lb_ref[...]
        dyc, dln, nhat = ln_bwd(ds_ref[...], yc_ref[...], lg, lb)
        dlg_ref[...] += _colsum(dln * nhat)
        dlb_ref[...] += _colsum(dln)
        dcb_ref[...] += _colsum(dyc)
        dbuf[0:ts, :] = dyc
        nxt = (i < nt - 1).astype(F32)
        dbuf[ts:ts + HALO, :] = ln_bwd(dsn_ref[...], ycn_ref[...], lg, lb)[0] * nxt
        first = (i > 0).astype(F32)
        ubuf[0:HALO, :] = ah_ref[...] * _sig(gh_ref[...]) * first
        ubuf[HALO:HALO + ts, :] = a_ref[...] * _sig(g_ref[...])
        for c0 in range(0, ts, CONV_CH):
            du = jnp.zeros((CONV_CH, CW), F32)
            dyc_c = dbuf[c0:c0 + CONV_CH, :]
            for k in range(KC):
                du = du + w_ref[k:k + 1, :] * dbuf[pl.ds(c0 + KC - 1 - k, CONV_CH), :]
                dw_ref[k:k + 1, :] += _colsum(dyc_c * ubuf[pl.ds(c0 + k + 2, CONV_CH), :])
            av = a_ref[c0:c0 + CONV_CH, :]
            sg = _sig(g_ref[c0:c0 + CONV_CH, :])
            dz_ref[c0:c0 + CONV_CH, 0:CW] = (du * sg).astype(dz_ref.dtype)
            dz_ref[c0:c0 + CONV_CH, CW:2 * CW] = (du * av * sg * (1.0 - sg)).astype(dz_ref.dtype)

    cur = lambda w, cbk: pl.BlockSpec((ts, w), functools.partial(lambda i, q: (i, q), q=cbk))
    prev = lambda cbk: pl.BlockSpec((HALO, CW), functools.partial(lambda i, q: (jnp.maximum(i * hb - 1, 0), q), q=cbk))
    nxt_spec = pl.BlockSpec((HALO, CW), lambda i: (jnp.minimum((i + 1) * hb, last_hb), 0))
    const = lambda a: pl.BlockSpec(a.shape, lambda i: (0, 0))
    acc = lambda r: pl.BlockSpec((r, CW), lambda i: (0, 0))
    return pl.pallas_call(
        body, name="conv_bwd", grid=(nt,),
        in_specs=[cur(CW, 0), cur(CW, 0), nxt_spec, nxt_spec, cur(CW, ZB_A), cur(CW, ZB_G), prev(ZB_A), prev(ZB_G),
                  const(w32), const(lg), const(lb), pl.BlockSpec(memory_space=pl.ANY)],
        out_specs=[pl.BlockSpec((ts, 2 * CW), lambda i: (i, ZB_A // 2)), acc(1), acc(1), acc(1), acc(HALO)],
        out_shape=[jax.ShapeDtypeStruct(dz.shape, dz.dtype), jax.ShapeDtypeStruct((1, CW), F32),
                   jax.ShapeDtypeStruct((1, CW), F32), jax.ShapeDtypeStruct((1, CW), F32),
                   jax.ShapeDtypeStruct((HALO, CW), F32)],
        scratch_shapes=[pltpu.VMEM((ts + HALO, CW), F32), pltpu.VMEM((HALO + ts, CW), F32)],
        input_output_aliases={11: 0},
        compiler_params=_params(("arbitrary",)),
    )(ds, yc, ds, yc, z, z, z, z, w32, lg, lb, dz)


SSM_TS = 512
GRP = 8


def _cmul(ar, ai, br, bi):
    return ar * br - ai * bi, ar * bi + ai * br


def _scan_tables(ar, ai, reverse):
    n = ar.shape[1]
    row = lax.broadcasted_iota(jnp.int32, (GRP, n), 0)
    dist = (GRP - 1 - row) if reverse else row
    one_r = jnp.broadcast_to(ar, (GRP, n))
    one_i = jnp.broadcast_to(ai, (GRP, n))
    p2r, p2i = _cmul(one_r, one_i, one_r, one_i)
    p4r, p4i = _cmul(p2r, p2i, p2r, p2i)
    steps = []
    for sft, (pr, pi) in ((1, (one_r, one_i)), (2, (p2r, p2i)), (4, (p4r, p4i))):
        keep = dist >= sft
        steps.append((jnp.where(keep, pr, 0.0), jnp.where(keep, pi, 0.0)))
    cr, ci = one_r, one_i
    accr, acci = one_r, one_i
    for e in range(1, GRP):
        cr, ci = _cmul(cr, ci, one_r, one_i)
        accr = jnp.where(dist == e, cr, accr)
        acci = jnp.where(dist == e, ci, acci)
    return steps, (accr, acci)


def _scan_group(xr, xi, steps, carry_tab, cr, ci, reverse):
    for sft, (tr, ti) in zip((1, 2, 4), steps):
        amt = (GRP - sft) if reverse else sft
        sr = pltpu.roll(xr, amt, 0)
        si = pltpu.roll(xi, amt, 0)
        xr, xi = xr + tr * sr - ti * si, xi + tr * si + ti * sr
    pr, pi = carry_tab
    xr = xr + pr * cr - pi * ci
    xi = xi + pr * ci + pi * cr
    return xr, xi


def ssm_fwd(z, wb_re, wb_im, wc, e_re, e_im, dvec):
    s = z.shape[0]
    ts = SSM_TS
    nt = s // ts
    ucol0 = ZB_U * CW // CB

    def body(u_ref, wbr_ref, wbi_ref, wc_ref, er_ref, ei_ref, d_ref, xr_ref, xi_ref, y_ref, gl_ref, car_r, car_i):
        i = pl.program_id(1)

        @pl.when(i == 0)
        def _():
            car_r[...] = jnp.zeros_like(car_r)
            car_i[...] = jnp.zeros_like(car_i)

        u = u_ref[...]
        ub = u.astype(BF16)
        xr_ref[...] = jnp.dot(ub, wbr_ref[0], preferred_element_type=F32)
        xi_ref[...] = jnp.dot(ub, wbi_ref[0], preferred_element_type=F32)
        steps, ctab = _scan_tables(er_ref[0], ei_ref[0], False)

        def grp(r, carry):
            cr, ci = carry
            r0 = pl.multiple_of(r * GRP, GRP)
            xr, xi = _scan_group(xr_ref[pl.ds(r0, GRP), :], xi_ref[pl.ds(r0, GRP), :], steps, ctab, cr, ci, False)
            xr_ref[pl.ds(r0, GRP), :] = xr
            xi_ref[pl.ds(r0, GRP), :] = xi
            return (jnp.broadcast_to(xr[GRP - 1:GRP, :], (GRP, SB)), jnp.broadcast_to(xi[GRP - 1:GRP, :], (GRP, SB)))

        cr, ci = lax.fori_loop(0, ts // GRP, grp, (car_r[...], car_i[...]))
        car_r[...] = cr
        car_i[...] = ci
        y = (jnp.dot(xr_ref[...].astype(BF16), wc_ref[0, 0:SB, :], preferred_element_type=F32)
             + jnp.dot(xi_ref[...].astype(BF16), wc_ref[0, SB:2 * SB, :], preferred_element_type=F32)
             + d_ref[0] * u)
        y_ref[...] = y
        gl_ref[...] = _gelu(y)[0].astype(gl_ref.dtype)

    blk3 = lambda a: pl.BlockSpec((1,) + a.shape[1:], lambda j, i: (j, 0, 0))
    return pl.pallas_call(
        body, name="ssm_fwd", grid=(NBLK, nt),
        in_specs=[pl.BlockSpec((ts, CB), lambda j, i: (i, ucol0 + j)),
                  blk3(wb_re), blk3(wb_im), blk3(wc), blk3(e_re), blk3(e_im), blk3(dvec)],
        out_specs=[pl.BlockSpec((ts, SB), lambda j, i: (i, j)), pl.BlockSpec((ts, SB), lambda j, i: (i, j)),
                   pl.BlockSpec((ts, CB), lambda j, i: (i, j)), pl.BlockSpec((ts, CB), lambda j, i: (i, j))],
        out_shape=[jax.ShapeDtypeStruct((s, NST), F32), jax.ShapeDtypeStruct((s, NST), F32),
                   jax.ShapeDtypeStruct((s, SW), F32), jax.ShapeDtypeStruct((s, SW), BF16)],
        scratch_shapes=[pltpu.VMEM((GRP, SB), F32), pltpu.VMEM((GRP, SB), F32)],
        compiler_params=_params(("parallel", "arbitrary")),
    )(z, wb_re, wb_im, wc, e_re, e_im, dvec)


def ssm_bwd(dgl, ypre, z, xs_re, xs_im, wbt_re, wbt_im, wct, e_re, e_im, dvec, dz):
    s = z.shape[0]
    ts = SSM_TS
    nt = s // ts
    ucol0 = ZB_U * CW // CB
    tn_dims = (((0,), (0,)), ((), ()))

    def body(dgl_ref, y_ref, u_ref, xr_ref, xi_ref, wbtr_ref, wbti_ref, wct_ref, er_ref, ei_ref, d_ref, dz_in,
             du_ref, dd_ref, dar_ref, dai_ref, dwbr_ref, dwbi_ref, dwc_ref,
             lr_ref, li_ref, car_r, car_i, acc_r, acc_i):
        i = pl.program_id(1)

        @pl.when(i == 0)
        def _():
            for ref in (car_r, car_i, acc_r, acc_i, dd_ref, dwbr_ref, dwbi_ref, dwc_ref):
                ref[...] = jnp.zeros_like(ref)

        u = u_ref[...]
        y = y_ref[...]
        dy = dgl_ref[...] * _gelu_grad(y, _gelu(y)[1])
        dd_ref[0] += _colsum(dy * u)
        dyb = dy.astype(BF16)
        dxo = jnp.dot(dyb, wct_ref[0], preferred_element_type=F32)
        lr_ref[...] = dxo[:, 0:SB]
        li_ref[...] = dxo[:, SB:2 * SB]
        steps, ctab = _scan_tables(er_ref[0], -ei_ref[0], True)
        row = lax.broadcasted_iota(jnp.int32, (GRP, SB), 0)

        def grp(q, carry):
            cr, ci, ar, ai = carry
            r0 = pl.multiple_of((ts // GRP - 1 - q) * GRP, GRP)
            lr, li = _scan_group(lr_ref[pl.ds(r0, GRP), :], li_ref[pl.ds(r0, GRP), :], steps, ctab, cr, ci, True)
            lr_ref[pl.ds(r0, GRP), :] = lr
            li_ref[pl.ds(r0, GRP), :] = li
            nr = jnp.where(row == GRP - 1, cr, pltpu.roll(lr, GRP - 1, 0))
            ni = jnp.where(row == GRP - 1, ci, pltpu.roll(li, GRP - 1, 0))
            xr = xr_ref[pl.ds(r0, GRP), :]
            xi = xi_ref[pl.ds(r0, GRP), :]
            ar = ar + nr * xr + ni * xi
            ai = ai + ni * xr - nr * xi
            return (jnp.broadcast_to(lr[0:1, :], (GRP, SB)), jnp.broadcast_to(li[0:1, :], (GRP, SB)), ar, ai)

        cr, ci, ar, ai = lax.fori_loop(0, ts // GRP, grp, (car_r[...], car_i[...], acc_r[...], acc_i[...]))
        car_r[...] = cr
        car_i[...] = ci
        acc_r[...] = ar
        acc_i[...] = ai

        @pl.when(i == nt - 1)
        def _():
            dar_ref[0] = _colsum(ar)
            dai_ref[0] = _colsum(ai)

        lrb = lr_ref[...].astype(BF16)
        lib = li_ref[...].astype(BF16)
        du = (jnp.dot(lrb, wbtr_ref[0], preferred_element_type=F32)
              + jnp.dot(lib, wbti_ref[0], preferred_element_type=F32) + d_ref[0] * dy)
        du_ref[...] = du.astype(du_ref.dtype)
        ub = u.astype(BF16)
        dwbr_ref[0] += lax.dot_general(ub, lrb, tn_dims, preferred_element_type=F32)
        dwbi_ref[0] += lax.dot_general(ub, lib, tn_dims, preferred_element_type=F32)
        dwc_ref[0, 0:SB, :] += lax.dot_general(xr_ref[...].astype(BF16), dyb, tn_dims, preferred_element_type=F32)
        dwc_ref[0, SB:2 * SB, :] += lax.dot_general(xi_ref[...].astype(BF16), dyb, tn_dims, preferred_element_type=F32)

    rev = lambda i: nt - 1 - i
    blk3 = lambda a: pl.BlockSpec((1,) + a.shape[1:], lambda j, i: (j, 0, 0))
    acc3 = lambda r, c: pl.BlockSpec((1, r, c), lambda j, i: (j, 0, 0))
    return pl.pallas_call(
        body, name="ssm_bwd", grid=(NBLK, nt),
        in_specs=[pl.BlockSpec((ts, CB), lambda j, i: (rev(i), j)), pl.BlockSpec((ts, CB), lambda j, i: (rev(i), j)),
                  pl.BlockSpec((ts, CB), lambda j, i: (rev(i), ucol0 + j)),
                  pl.BlockSpec((ts, SB), lambda j, i: (rev(i), j)), pl.BlockSpec((ts, SB), lambda j, i: (rev(i), j)),
                  blk3(wbt_re), blk3(wbt_im), blk3(wct), blk3(e_re), blk3(e_im), blk3(dvec),
                  pl.BlockSpec(memory_space=pl.ANY)],
        out_specs=[pl.BlockSpec((ts, CB), lambda j, i: (rev(i), ucol0 + j)),
                   acc3(1, CB), acc3(1, SB), acc3(1, SB), acc3(CB, SB), acc3(CB, SB), acc3(2 * SB, CB)],
        out_shape=[jax.ShapeDtypeStruct(dz.shape, dz.dtype),
                   jax.ShapeDtypeStruct((NBLK, 1, CB), F32),
                   jax.ShapeDtypeStruct((NBLK, 1, SB), F32), jax.ShapeDtypeStruct((NBLK, 1, SB), F32),
                   jax.ShapeDtypeStruct((NBLK, CB, SB), F32), jax.ShapeDtypeStruct((NBLK, CB, SB), F32),
                   jax.ShapeDtypeStruct((NBLK, 2 * SB, CB), F32)],
        scratch_shapes=[pltpu.VMEM((ts, SB), F32), pltpu.VMEM((ts, SB), F32)] + [pltpu.VMEM((GRP, SB), F32)] * 4,
        input_output_aliases={11: 0},
        compiler_params=_params(("parallel", "arbitrary")),
    )(dgl, ypre, z, xs_re, xs_im, wbt_re, wbt_im, wct, e_re, e_im, dvec, dz)


def _disc(a_re, a_im, log_dt, b_re, b_im, expand):
    dt = jnp.dot(expand, jnp.exp(log_dt), preferred_element_type=F32, precision=lax.Precision.HIGHEST)
    mag = jnp.exp(dt * a_re)
    e_re, e_im = mag * jnp.cos(dt * a_im), mag * jnp.sin(dt * a_im)
    n_re, n_im = e_re - 1.0, e_im
    den = a_re * a_re + a_im * a_im
    q_re = (n_re * a_re + n_im * a_im) / den
    q_im = (n_im * a_re - n_re * a_im) / den
    return e_re, e_im, q_re * b_re - q_im * b_im, q_re * b_im + q_im * b_re


def _whole(a):
    return pl.BlockSpec(a.shape, functools.partial(lambda n: (0,) * n, n=a.ndim))


def disc_fwd(a_re, a_im, log_dt, b_re, b_im, expand):
    def body(ar, ai, ld, br, bi, ex, er_o, ei_o, bbr_o, bbi_o):
        er, ei, bbr, bbi = _disc(ar[...], ai[...], ld[...], br[...], bi[...], ex[...])
        er_o[...] = er
        ei_o[...] = ei
        bbr_o[...] = bbr
        bbi_o[...] = bbi

    ins = (a_re, a_im, log_dt, b_re, b_im, expand)
    outs = [jax.ShapeDtypeStruct(a_re.shape, F32)] * 2 + [jax.ShapeDtypeStruct(b_re.shape, F32)] * 2
    return pl.pallas_call(body, name="disc_fwd", in_specs=[_whole(a) for a in ins],
                          out_specs=[_whole(o) for o in outs], out_shape=outs, compiler_params=_params())(*ins)


def disc_bwd(a_re, a_im, log_dt, b_re, b_im, expand, de_re, de_im, dbb_re, dbb_im):
    def body(ar, ai, ld, br, bi, ex, der, dei, dbr, dbi, o_ar, o_ai, o_ld, o_br, o_bi):
        exv = ex[...]
        _, vjp = jax.vjp(lambda *p: _disc(*p, exv), ar[...], ai[...], ld[...], br[...], bi[...])
        g = vjp((der[...], dei[...], dbr[...], dbi[...]))
        for o, v in zip((o_ar, o_ai, o_ld, o_br, o_bi), g):
            o[...] = v

    ins = (a_re, a_im, log_dt, b_re, b_im, expand, de_re, de_im, dbb_re, dbb_im)
    outs = [jax.ShapeDtypeStruct(a.shape, F32) for a in (a_re, a_im, log_dt, b_re, b_im)]
    return pl.pallas_call(body, name="disc_bwd", in_specs=[_whole(a) for a in ins],
                          out_specs=[_whole(o) for o in outs], out_shape=outs, compiler_params=_params())(*ins)


def mod_fwd(c_all, w_ada, b_cols):
    def body(c_ref, w_ref, b_ref, act_ref, mod_ref):
        cv = c_ref[...]
        act = cv * _sig(cv)
        act_ref[...] = act
        mod_ref[...] = jnp.dot(act, w_ref[...], preferred_element_type=F32, precision=lax.Precision.HIGHEST) + b_ref[...]

    ins = (c_all, w_ada, b_cols)
    outs = [jax.ShapeDtypeStruct(c_all.shape, F32), jax.ShapeDtypeStruct((NDEV, w_ada.shape[1]), F32)]
    return pl.pallas_call(body, name="mod_fwd", in_specs=[_whole(a) for a in ins],
                          out_specs=[_whole(o) for o in outs], out_shape=outs, compiler_params=_params())(*ins)


def ada_grad(act_all, dmod_cols):
    def body(a_ref, d_ref, o_ref):
        o_ref[...] = lax.dot_general(a_ref[...], d_ref[...], (((0,), (0,)), ((), ())),
                                     preferred_element_type=F32, precision=lax.Precision.HIGHEST)

    out = jax.ShapeDtypeStruct((act_all.shape[1], dmod_cols.shape[1]), F32)
    return pl.pallas_call(body, name="ada_grad", in_specs=[_whole(act_all), _whole(dmod_cols)],
                          out_specs=_whole(out), out_shape=out, compiler_params=_params())(act_all, dmod_cols)


def _adam_math(w, g, m, v):
    m2 = ADAM_B1 * m + (1.0 - ADAM_B1) * g
    v2 = ADAM_B2 * v + (1.0 - ADAM_B2) * (g * g)
    m_hat = m2 / (1.0 - ADAM_B1 ** ADAM_STEP)
    v_hat = v2 / (1.0 - ADAM_B2 ** ADAM_STEP)
    delta = -ADAM_LR * (m_hat / (jnp.sqrt(v_hat) + ADAM_EPS) + ADAM_WD * w)
    return delta, m2, v2


def adam(name, w, g, m, v):
    r, c = w.shape
    tr = r
    for cand in (256, 128, 64, 32, 16, 8):
        if r % cand == 0 and r > cand:
            tr = cand
            break

    def body(w_ref, g_ref, m_ref, v_ref, d_o, m_o, v_o):
        d, m2, v2 = _adam_math(w_ref[...], g_ref[...], m_ref[...], v_ref[...])
        d_o[...] = d
        m_o[...] = m2
        v_o[...] = v2

    spec = pl.BlockSpec((tr, c), lambda i: (i, 0))
    out = jax.ShapeDtypeStruct((r, c), F32)
    return pl.pallas_call(body, name=name, grid=(r // tr,), in_specs=[spec] * 4, out_specs=[spec] * 3,
                          out_shape=[out] * 3, compiler_params=_params(("parallel",)))(w, g, m, v)


def _rows_tile(r, most):
    best = None
    for t in range(16, min(r, most) + 1, 16):
        if r % t == 0:
            best = t
    assert best is not None, r
    return best


def sum_slots(name, slots, out_dtype=F32):
    n, r, c = slots.shape
    tr = _rows_tile(r, max(16, (2 * 1024 * 1024) // (n * c)))

    def body(s_ref, o_ref):
        acc = s_ref[0].astype(F32)
        for q in range(1, n):
            acc = acc + s_ref[q].astype(F32)
        o_ref[...] = acc.astype(o_ref.dtype)

    return pl.pallas_call(body, name=name, grid=(r // tr,),
                          in_specs=[pl.BlockSpec((n, tr, c), lambda i: (0, i, 0))],
                          out_specs=pl.BlockSpec((tr, c), lambda i: (i, 0)),
                          out_shape=jax.ShapeDtypeStruct((r, c), out_dtype), compiler_params=_params(("parallel",)))(slots)


HBM_SPEC = pl.BlockSpec(memory_space=pltpu.HBM)


def _coords():
    return lax.axis_index("x"), lax.axis_index("y"), lax.axis_index("c")


def _linear(x, y, c):
    return 4 * x + 2 * y + c


def all_gather(name, shards):
    nq = len(shards)

    def body(*refs):
        xs, outs = refs[:nq], refs[nq:2 * nq]
        send_sems, recv_sems, local_sems = refs[2 * nq:]
        x, y, cc = _coords()
        me, sibling = (x, y, cc), (x, y, 1 - cc)
        chips = [(1 - x, y), (x, 1 - y), (1 - x, 1 - y)]

        def slot(q, px, py, pc):
            return outs[q].at[_linear(px, py, pc)]

        def copy(q, k, block, to, src=None):
            return pltpu.make_async_remote_copy(
                src_ref=slot(q, *block) if src is None else src, dst_ref=slot(q, *block),
                send_sem=send_sems.at[7 * q + k], recv_sem=recv_sems.at[7 * q + k], device_id=to, device_id_type=MESH)

        mine = [pltpu.make_async_copy(xs[q], slot(q, *me), local_sems.at[q]) for q in range(nq)]
        first = []
        for q in range(nq):
            first.append(copy(q, 0, me, sibling, src=xs[q]))
            first += [copy(q, 1 + j, me, (*chip, cc), src=xs[q]) for j, chip in enumerate(chips)]
        for cp in mine + first:
            cp.start()
        passed = []
        for q in range(nq):
            for j, chip in enumerate(chips):
                copy(q, 1 + j, (*chip, cc), me).wait_recv()
                passed.append(copy(q, 4 + j, (*chip, cc), sibling))
                passed[-1].start()
        for q in range(nq):
            copy(q, 0, sibling, me).wait_recv()
            for j, chip in enumerate(chips):
                copy(q, 4 + j, (*chip, 1 - cc), me).wait_recv()
        for cp in first + passed:
            cp.wait_send()
        for cp in mine:
            cp.wait()

    return pl.pallas_call(
        body, name=name, in_specs=[HBM_SPEC] * nq, out_specs=[HBM_SPEC] * nq,
        out_shape=[jax.ShapeDtypeStruct((NDEV,) + s.shape, s.dtype) for s in shards],
        scratch_shapes=[pltpu.SemaphoreType.DMA((7 * nq,)), pltpu.SemaphoreType.DMA((7 * nq,)),
                        pltpu.SemaphoreType.DMA((nq,))],
    )(*shards)


NCHIP = 4


def pair_exchange(name, grads, offs, rows):
    nq = len(grads)
    c, dt = grads[0].shape[2], grads[0].dtype

    def body(*refs):
        gs, out = refs[:nq], refs[nq]
        send_sems, recv_sems, local_sems = refs[nq + 1:]
        x, y, cc = _coords()
        loc, rem = [], []
        for q in range(nq):
            dst_rows = pl.ds(offs[q], gs[q].shape[1])
            for chip in range(NCHIP):
                n = NCHIP * q + chip
                loc.append(pltpu.make_async_copy(gs[q].at[2 * chip + cc], out.at[0, chip, dst_rows, :], local_sems.at[n]))
                rem.append(pltpu.make_async_remote_copy(
                    src_ref=gs[q].at[2 * chip + 1 - cc], dst_ref=out.at[1, chip, dst_rows, :],
                    send_sem=send_sems.at[n], recv_sem=recv_sems.at[n], device_id=(x, y, 1 - cc), device_id_type=MESH))
        for cp in rem + loc:
            cp.start()
        for cp in rem:
            cp.wait_recv()
        for cp in rem:
            cp.wait_send()
        for cp in loc:
            cp.wait()

    ncopy = NCHIP * nq
    return pl.pallas_call(
        body, name=name, in_specs=[HBM_SPEC] * nq, out_specs=HBM_SPEC,
        out_shape=jax.ShapeDtypeStruct((2, NCHIP, rows, c), dt),
        scratch_shapes=[pltpu.SemaphoreType.DMA((ncopy,))] * 3,
    )(*grads)


def chip_exchange(name, partial):
    _, rows, c = partial.shape

    def body(in_ref, out_ref, send_sems, recv_sems, local_sem):
        x, y, cc = _coords()
        chip = 2 * x + y
        mine = pltpu.make_async_copy(in_ref.at[chip], out_ref.at[chip], local_sem)
        mine.start()
        copies = []
        for k in range(1, NCHIP):
            fx, fy = (k >> 1) & 1, k & 1
            px = x + fx - 2 * fx * x
            py = y + fy - 2 * fy * y
            cp = pltpu.make_async_remote_copy(
                src_ref=in_ref.at[2 * px + py], dst_ref=out_ref.at[chip],
                send_sem=send_sems.at[k - 1], recv_sem=recv_sems.at[k - 1], device_id=(px, py, cc), device_id_type=MESH)
            cp.start()
            copies.append(cp)
        for cp in copies:
            cp.wait_recv()
        for cp in copies:
            cp.wait_send()
        mine.wait()

    return pl.pallas_call(
        body, name=name, in_specs=[HBM_SPEC], out_specs=HBM_SPEC,
        out_shape=jax.ShapeDtypeStruct((NCHIP, rows, c), partial.dtype),
        scratch_shapes=[pltpu.SemaphoreType.DMA((NCHIP - 1,)), pltpu.SemaphoreType.DMA((NCHIP - 1,)),
                        pltpu.SemaphoreType.DMA],
    )(partial)


def _block_diag(w, rows_per, cols_per):
    w = w.reshape(NBLK, 8, rows_per, cols_per)
    eye = jnp.eye(8, dtype=w.dtype)
    out = w[:, :, :, None, :] * eye[None, :, None, :, None]
    return out.reshape(NBLK, 8 * rows_per, 8 * cols_per)


def _diag_blocks(wd, rows_per, cols_per):
    wd = wd.reshape(NBLK, 8, rows_per, 8, cols_per)
    idx = jnp.arange(8)
    return wd[:, idx, :, idx, :].transpose(1, 0, 2, 3).reshape(NG, rows_per, cols_per)


def _pad_rows(v, mult):
    n = v.shape[0]
    return jnp.pad(v, (0, (-n) % mult))


def kernel(x, c, w_ada, b_ada, norm1_g, w_in, conv_w, conv_b, conv_ln_g, conv_ln_b, conv_proj, ssm_a_re, ssm_a_im, ssm_b_re, ssm_b_im, ssm_c_re, ssm_c_im, ssm_d, ssm_log_dt, ssm_glu, w_out, norm2_g, w_ffn_in, w_ffn_out, final_g, loss_target, m_w_ada, m_b_ada, m_norm1_g, m_w_in, m_conv_w, m_conv_b, m_conv_ln_g, m_conv_ln_b, m_conv_proj, m_ssm_a_re, m_ssm_a_im, m_ssm_b_re, m_ssm_b_im, m_ssm_c_re, m_ssm_c_im, m_ssm_d, m_ssm_log_dt, m_ssm_glu, m_w_out, m_norm2_g, m_w_ffn_in, m_w_ffn_out, m_final_g, v_w_ada, v_b_ada, v_norm1_g, v_w_in, v_conv_w, v_conv_b, v_conv_ln_g, v_conv_ln_b, v_conv_proj, v_ssm_a_re, v_ssm_a_im, v_ssm_b_re, v_ssm_b_im, v_ssm_c_re, v_ssm_c_im, v_ssm_d, v_ssm_log_dt, v_ssm_glu, v_w_out, v_norm2_g, v_w_ffn_in, v_w_ffn_out, v_final_g):
    me = _linear(*_coords())
    xs = x[0]
    tgt = loss_target[0]
    seq = xs.shape[0]

    c_all, cw_g = all_gather("gather_c_conv_w", [c, conv_w[0]])
    parts = [w_in[0].T, conv_proj[0].T, ssm_glu[0].T, w_out[0], w_ffn_in[0].T, w_ffn_out[0]]
    part_rows = [p.size // D for p in parts]
    offs = [0]
    for r in part_rows:
        offs.append(offs[-1] + r)
    gathered = all_gather("gather_weights", [p.astype(BF16) for p in parts])
    w_in_t, conv_proj_t, ssm_glu_t, w_out_f, w_ffn_in_t, w_ffn_out_f = [
        g.reshape(NDEV * g.shape[1], g.shape[2]) for g in gathered]

    ncol = w_ada.shape[2]
    c_all = c_all.reshape(NDEV, D)
    b_cols = lax.dynamic_slice_in_dim(b_ada, me * ncol, ncol, axis=1)
    act_all, mod_cols = mod_fwd(c_all, w_ada[0], b_cols)
    (mod_all,) = all_gather("gather_mod", [mod_cols])
    mod = lax.dynamic_index_in_dim(mod_all, me, axis=1, keepdims=False).reshape(NMOD, D)
    sh1, sc1, g1, sh2, sc2, g2 = [mod[q:q + 1] for q in range(NMOD)]

    expand = jnp.repeat(jnp.eye(NG, dtype=F32), NP, axis=0)
    a_re_c, a_im_c = ssm_a_re.reshape(NST, 1), ssm_a_im.reshape(NST, 1)
    ldt_c = ssm_log_dt.reshape(NG, 1)
    b_re_r, b_im_r = ssm_b_re.reshape(NST, GH), ssm_b_im.reshape(NST, GH)
    e_re, e_im, bb_re, bb_im = disc_fwd(a_re_c, a_im_c, ldt_c, b_re_r, b_im_r, expand)
    e_re_b, e_im_b = e_re.reshape(NBLK, 1, SB), e_im.reshape(NBLK, 1, SB)
    bb_re_g, bb_im_g = bb_re.reshape(NG, NP, GH), bb_im.reshape(NG, NP, GH)
    wbt_re = _block_diag(bb_re_g, NP, GH)
    wbt_im = _block_diag(bb_im_g, NP, GH)
    wb_re, wb_im = wbt_re.transpose(0, 2, 1), wbt_im.transpose(0, 2, 1)
    wct = jnp.concatenate([_block_diag(ssm_c_re[0], GH, NP), -_block_diag(ssm_c_im[0], GH, NP)], axis=2)
    wc = wct.transpose(0, 2, 1)
    to_b = lambda a: a.astype(BF16)
    dvec = ssm_d.reshape(NBLK, 1, CB)

    n1g = norm1_g

    def f_norm1(xv, g, sc, sh):
        _, xh = _rms_stats(xv)
        return [xh * g * (1.0 + sc) + sh], []

    (h1,) = rowwise("norm1", f_norm1, [xs], [n1g, sc1, sh1], [(D, BF16)], [], 512)
    z = mm("mm_in", h1, w_in_t, "nt", tiles=(1024, CW, 1024), b_rot=Z_ROT)

    conv_w_full = cw_g.transpose(1, 0, 2).reshape(KC, CW)
    w32 = jnp.pad(conv_w_full, ((0, HALO - KC), (0, 0)))
    yc, s_act = conv_fwd(z, w32, conv_b, conv_ln_g, conv_ln_b)
    y_conv = mm("mm_conv_proj", s_act, conv_proj_t, "nt")

    xs_re, xs_im, ypre, gl = ssm_fwd(z, to_b(wb_re), to_b(wb_im), to_b(wc), e_re_b, e_im_b, dvec)
    z2 = mm("mm_ssm_glu", gl, ssm_glu_t, "nt")

    def f_merge(yc_v, za, zb, glc, gls):
        return [_sig(glc) * yc_v + _sig(gls) * (za * _sig(zb))], []

    (merged,) = rowwise("merge", f_merge, [y_conv, (z2, D, 0), (z2, D, 1), (z, D, 0), (z, D, 1)],
                        [], [(D, BF16)], [], 512)
    o1 = mm("mm_out", merged, w_out_f, "nn")

    def f_norm2(xv, o1v, g1v, g, sc, sh):
        x1v = xv + g1v * o1v
        _, xh = _rms_stats(x1v)
        return [x1v, xh * g * (1.0 + sc) + sh], []

    x1, h2 = rowwise("norm2", f_norm2, [xs, o1], [g1, norm2_g, sc2, sh2], [(D, F32), (D, BF16)], [], 512)
    f = mm("mm_ffn_in", h2, w_ffn_in_t, "nt")

    def f_swiglu(fg, fu):
        return [fg * _sig(fg) * fu], []

    (act,) = rowwise("swiglu", f_swiglu, [(f, FH, 0), (f, FH, 1)], [], [(FH, BF16)], [], 256)
    o2 = mm("mm_ffn_out", act, w_ffn_out_f, "nn")

    fg_row = final_g.reshape(1, D)

    def f_final(x1v, o2v, tv, g2v, fg):
        x2v = x1v + g2v * o2v
        r, xh = _rms_stats(x2v)
        yv = xh * fg
        err = yv - tv
        loss = jnp.sum(_colsum(err * err), axis=1, keepdims=True) * (0.5 / D)
        dy = err * (1.0 / D)
        dx2 = _rms_bwd(dy * fg, xh, r)
        return ([dx2, g2v * dx2],
                [jnp.broadcast_to(loss, (1, LANE)), _colsum(dy * xh), _colsum(dx2 * o2v)])

    dx2, do2, loss_l, d_final_g, d_g2 = rowwise(
        "final", f_final, [x1, o2, tgt], [g2, fg_row], [(D, F32), (D, BF16)], [LANE, D, D], 256)

    dact = mm("mm_dact", do2, w_ffn_out_f, "nt")
    g_ffn_out = mm("mm_g_ffn_out", act, do2, "tn", BF16)

    def f_dswiglu(fg, fu, da):
        sg = _sig(fg)
        return [jnp.concatenate([da * fu * (sg * (1.0 + fg * (1.0 - sg))), da * (fg * sg)], axis=1)], []

    (df,) = rowwise("dswiglu", f_dswiglu, [(f, FH, 0), (f, FH, 1), dact], [], [(2 * FH, BF16)], [], 256)
    dh2 = mm("mm_dh2", df, w_ffn_in_t, "nn")
    g_ffn_in_t = mm("mm_g_ffn_in", df, h2, "tn", BF16)

    def f_dnorm2(dh, x1v, dx2v, o1v, g, sc, g1v):
        r, xh = _rms_stats(x1v)
        dxh = dh * (1.0 + sc) * g
        dx1 = dx2v + _rms_bwd(dxh, xh, r)
        return ([dx1, g1v * dx1],
                [_colsum(dh * xh * g), _colsum(dh), _colsum(dh * (1.0 + sc) * xh), _colsum(dx1 * o1v)])

    dx1, do1, d_sc2, d_sh2, d_n2g, d_g1 = rowwise(
        "dnorm2", f_dnorm2, [dh2, x1, dx2, o1], [norm2_g, sc2, g1], [(D, F32), (D, BF16)], [D, D, D, D], 256)

    dmerged = mm("mm_dmerged", do1, w_out_f, "nt")
    g_out = mm("mm_g_out", merged, do1, "tn", BF16)

    def f_dmerge(dm, yc_v, za, zb, glc, gls):
        sc_ = _sig(glc)
        ss_ = _sig(gls)
        sb_ = _sig(zb)
        dys = dm * ss_
        dz2 = jnp.concatenate([dys * sb_, dys * za * sb_ * (1.0 - sb_)], axis=1)
        dgl = jnp.concatenate([dm * yc_v * sc_ * (1.0 - sc_), dm * (za * sb_) * ss_ * (1.0 - ss_)], axis=1)
        return [dm * sc_, dz2, dgl], []

    dyconv, dz2, dz = rowwise(
        "dmerge", f_dmerge, [dmerged, y_conv, (z2, D, 0), (z2, D, 1), (z, D, 0), (z, D, 1)],
        [], [(D, BF16), (2 * D, BF16), (2 * D, BF16, ZW, 0)], [], 256)

    ds = mm("mm_ds", dyconv, conv_proj_t, "nn")
    g_conv_proj_t = mm("mm_g_conv_proj", dyconv, s_act, "tn", BF16)
    dz, d_lng, d_lnb, d_cb, d_cw32 = conv_bwd(ds, yc, z, w32, conv_ln_g, conv_ln_b, dz)

    dgl = mm("mm_dgl", dz2, ssm_glu_t, "nn")
    g_ssm_glu_t = mm("mm_g_ssm_glu", dz2, gl, "tn", BF16)
    dz, d_d, d_ar, d_ai, d_wb_re, d_wb_im, d_wc = ssm_bwd(
        dgl, ypre, z, xs_re, xs_im, to_b(wbt_re), to_b(wbt_im), to_b(wct), e_re_b, e_im_b, dvec, dz)

    dh1 = mm("mm_dh1", dz, w_in_t, "nn", tiles=(1024, 1024, CW), b_rot=Z_ROT)
    g_in_t = mm("mm_g_in", dz, h1, "tn", BF16, tiles=(CW, 1024, 1024), o_rot=Z_ROT)

    def f_dnorm1(dh, xv, dx1v, g, sc):
        r, xh = _rms_stats(xv)
        dxh = dh * (1.0 + sc) * g
        return ([dx1v + _rms_bwd(dxh, xh, r)],
                [_colsum(dh * xh * g), _colsum(dh), _colsum(dh * (1.0 + sc) * xh)])

    grad_x, d_sc1, d_sh1, d_n1g = rowwise(
        "dnorm1", f_dnorm1, [dh1, xs, dx1], [n1g, sc1], [(D, F32)], [D, D, D], 256)

    d_bb_re = _diag_blocks(d_wb_re.transpose(0, 2, 1), NP, GH).reshape(NST, GH)
    d_bb_im = _diag_blocks(d_wb_im.transpose(0, 2, 1), NP, GH).reshape(NST, GH)
    d_wct = d_wc.transpose(0, 2, 1)
    d_c_re = _diag_blocks(d_wct[:, :, 0:SB], GH, NP)
    d_c_im = -_diag_blocks(d_wct[:, :, SB:2 * SB], GH, NP)
    d_a_re, d_a_im, d_ldt, d_b_re, d_b_im = disc_bwd(
        a_re_c, a_im_c, ldt_c, b_re_r, b_im_r, expand, d_ar.reshape(NST, 1), d_ai.reshape(NST, 1), d_bb_re, d_bb_im)

    dmod = jnp.concatenate([d_sh1, d_sc1, d_g1, d_sh2, d_sc2, d_g2], axis=1)
    small_local = [dmod.reshape(-1), d_n1g.reshape(-1), d_cw32[0:KC].reshape(-1), d_cb.reshape(-1), d_lng.reshape(-1),
                   d_lnb.reshape(-1), d_a_re.reshape(-1), d_a_im.reshape(-1), d_b_re.reshape(-1), d_b_im.reshape(-1),
                   d_c_re.reshape(-1), d_c_im.reshape(-1), d_d.reshape(-1), d_ldt.reshape(-1), d_n2g.reshape(-1),
                   d_final_g.reshape(-1), loss_l[0, 0:1]]
    small_sizes = [v.shape[0] for v in small_local]
    packed = _pad_rows(jnp.concatenate(small_local), 256 * LANE).reshape(-1, LANE)
    (small_all,) = all_gather("gather_small", [packed])
    small_sum = sum_slots("sum_small", small_all).reshape(-1)
    pieces, pos = [], 0
    for n in small_sizes:
        pieces.append(small_sum[pos:pos + n])
        pos += n
    (g_b_ada, g_n1g, g_cw_full, g_cb, g_lng, g_lnb, g_a_re, g_a_im, g_b_re, g_b_im, g_c_re, g_c_im, g_d, g_ldt,
     g_n2g, g_fg, loss_sum) = pieces
    loss = loss_sum[0]
    dmod_all = small_all.reshape(NDEV, -1)[:, 0:NMOD * D]
    g_w_ada = ada_grad(act_all, lax.dynamic_slice_in_dim(dmod_all, me * ncol, ncol, axis=1))
    ccol = conv_w.shape[2]
    g_conv_w = lax.dynamic_slice_in_dim(g_cw_full.reshape(KC, CW), me * ccol, ccol, axis=1)

    big_t = [g_in_t, g_conv_proj_t, g_ssm_glu_t, g_out, g_ffn_in_t, g_ffn_out]
    pairs = pair_exchange("grads_pair", [g.reshape(NDEV, -1, D) for g in big_t], offs, offs[-1])
    partial = sum_slots("sum_pair", pairs.reshape(2, NCHIP * offs[-1], D), BF16).reshape(NCHIP, offs[-1], D)
    recv = chip_exchange("grads_chips", partial)
    gsum = sum_slots("sum_grads", recv)

    def mine(q, cols):
        return gsum[offs[q]:offs[q + 1]].reshape(-1, cols)

    g_w_in = mine(0, D).T
    g_conv_proj = mine(1, CW).T
    g_ssm_glu = mine(2, SW).T
    g_w_out = mine(3, D)
    g_w_ffn_in = mine(4, D).T
    g_w_ffn_out = mine(5, D)

    grads = {
        "w_ada": g_w_ada[None], "b_ada": g_b_ada.reshape(b_ada.shape), "norm1_g": g_n1g.reshape(norm1_g.shape),
        "w_in": g_w_in[None], "conv_w": g_conv_w[None], "conv_b": g_cb.reshape(conv_b.shape),
        "conv_ln_g": g_lng.reshape(conv_ln_g.shape), "conv_ln_b": g_lnb.reshape(conv_ln_b.shape),
        "conv_proj": g_conv_proj[None], "ssm_a_re": g_a_re.reshape(ssm_a_re.shape),
        "ssm_a_im": g_a_im.reshape(ssm_a_im.shape), "ssm_b_re": g_b_re.reshape(ssm_b_re.shape),
        "ssm_b_im": g_b_im.reshape(ssm_b_im.shape), "ssm_c_re": g_c_re.reshape(ssm_c_re.shape),
        "ssm_c_im": g_c_im.reshape(ssm_c_im.shape), "ssm_d": g_d.reshape(ssm_d.shape),
        "ssm_log_dt": g_ldt.reshape(ssm_log_dt.shape), "ssm_glu": g_ssm_glu[None], "w_out": g_w_out[None],
        "norm2_g": g_n2g.reshape(norm2_g.shape), "w_ffn_in": g_w_ffn_in[None], "w_ffn_out": g_w_ffn_out[None],
        "final_g": g_fg.reshape(final_g.shape),
    }
    weights = {
        "w_ada": (w_ada, m_w_ada, v_w_ada), "b_ada": (b_ada, m_b_ada, v_b_ada), "norm1_g": (norm1_g, m_norm1_g, v_norm1_g),
        "w_in": (w_in, m_w_in, v_w_in), "conv_w": (conv_w, m_conv_w, v_conv_w), "conv_b": (conv_b, m_conv_b, v_conv_b),
        "conv_ln_g": (conv_ln_g, m_conv_ln_g, v_conv_ln_g), "conv_ln_b": (conv_ln_b, m_conv_ln_b, v_conv_ln_b),
        "conv_proj": (conv_proj, m_conv_proj, v_conv_proj), "ssm_a_re": (ssm_a_re, m_ssm_a_re, v_ssm_a_re),
        "ssm_a_im": (ssm_a_im, m_ssm_a_im, v_ssm_a_im), "ssm_b_re": (ssm_b_re, m_ssm_b_re, v_ssm_b_re),
        "ssm_b_im": (ssm_b_im, m_ssm_b_im, v_ssm_b_im), "ssm_c_re": (ssm_c_re, m_ssm_c_re, v_ssm_c_re),
        "ssm_c_im": (ssm_c_im, m_ssm_c_im, v_ssm_c_im), "ssm_d": (ssm_d, m_ssm_d, v_ssm_d),
        "ssm_log_dt": (ssm_log_dt, m_ssm_log_dt, v_ssm_log_dt), "ssm_glu": (ssm_glu, m_ssm_glu, v_ssm_glu),
        "w_out": (w_out, m_w_out, v_w_out), "norm2_g": (norm2_g, m_norm2_g, v_norm2_g),
        "w_ffn_in": (w_ffn_in, m_w_ffn_in, v_w_ffn_in), "w_ffn_out": (w_ffn_out, m_w_ffn_out, v_w_ffn_out),
        "final_g": (final_g, m_final_g, v_final_g),
    }
    order = list(weights)
    big = ("w_ada", "w_in", "conv_proj", "ssm_glu", "w_out", "w_ffn_in", "w_ffn_out")
    delta, new_m, new_v = {}, {}, {}
    for n in big:
        wv, mv, vv = weights[n]
        shp = wv.shape
        d_, m_, v_ = adam("adam_" + n, wv.reshape(shp[-2:]), grads[n].reshape(shp[-2:]), mv.reshape(shp[-2:]),
                          vv.reshape(shp[-2:]))
        delta[n], new_m[n], new_v[n] = d_.reshape(shp), m_.reshape(shp), v_.reshape(shp)
    small = [n for n in order if n not in big]
    sizes = [weights[n][0].size for n in small]

    def pack(idx_or_grad):
        vs = [(grads[n] if idx_or_grad is None else weights[n][idx_or_grad]).reshape(-1) for n in small]
        return _pad_rows(jnp.concatenate(vs), 256 * LANE).reshape(-1, LANE)

    d_p, m_p, v_p = adam("adam_small", pack(0), pack(None), pack(1), pack(2))
    d_p, m_p, v_p = d_p.reshape(-1), m_p.reshape(-1), v_p.reshape(-1)
    pos = 0
    for n, sz in zip(small, sizes):
        shp = weights[n][0].shape
        delta[n] = d_p[pos:pos + sz].reshape(shp)
        new_m[n] = m_p[pos:pos + sz].reshape(shp)
        new_v[n] = v_p[pos:pos + sz].reshape(shp)
        pos += sz

    return (loss, grad_x[None], *[grads[n] for n in order], *[delta[n] for n in order],
            *[new_m[n] for n in order], *[new_v[n] for n in order])
```

```python
import functools
import math

import jax
import jax.numpy as jnp
from jax import lax
from jax.experimental import pallas as pl
from jax.experimental.pallas import tpu as pltpu

F32 = jnp.float32
BF16 = jnp.bfloat16

D = 1024
CW = 512
KC = 31
SW = 512
NG = 32
GH = 16
NP = 64
NST = NG * NP
FH = 2816
NMOD = 6
NDEV = 8
EPS = 1e-6
CB = 128
SB = 512
NBLK = SW // CB
HALO = 32
ZW = 2 * CW + SW + 2 * D
Z_ROT = (ZW // CW, 3)
ZB_A, ZB_G, ZB_U = 4, 5, 6

ADAM_LR = 0.001
ADAM_B1 = 0.9
ADAM_B2 = 0.999
ADAM_EPS = 1e-08
ADAM_WD = 0.01
ADAM_STEP = 10

V7X_VMEM_BYTES = 64 * 1024 * 1024
VMEM_LIMIT = V7X_VMEM_BYTES - 8 * 1024 * 1024
LANE = 128
MESH = pl.DeviceIdType.MESH


def _params(sem=None, **kw):
    if sem is not None:
        kw["dimension_semantics"] = sem
    return pltpu.CompilerParams(vmem_limit_bytes=VMEM_LIMIT, **kw)


def _tile(n, most):
    best = None
    for t in range(LANE, most + 1, LANE):
        if n % t == 0:
            best = t
    if best is None:
        raise ValueError(f"no tile for {n}")
    return best


def _sig(x):
    return jax.nn.sigmoid(x)


def mm(name, a, b, mode, out_dtype=F32, tiles=None, b_rot=None, o_rot=None):
    if mode == "nn":
        (m, k), (k2, n) = a.shape, b.shape
    elif mode == "nt":
        (m, k), (n, k2) = a.shape, b.shape
    else:
        (k, m), (k2, n) = a.shape, b.shape
    assert k == k2, (name, a.shape, b.shape)
    bm, bn, bk = tiles or (_tile(m, 1024), _tile(n, 1408), _tile(k, 1408 if k % 1408 == 0 else 1024))
    bm, bn, bk = min(bm, m), min(bn, n), min(bk, k)
    assert m % bm == 0 and n % bn == 0 and k % bk == 0, (name, m, n, k, bm, bn, bk)
    nk = k // bk
    rot = lambda idx, r: idx if r is None else (idx + r[1]) % r[0]
    if mode == "nn":
        a_spec = pl.BlockSpec((bm, bk), lambda i, j, kk: (i, kk))
        b_spec = pl.BlockSpec((bk, bn), lambda i, j, kk: (rot(kk, b_rot), j))
        dims = (((1,), (0,)), ((), ()))
    elif mode == "nt":
        a_spec = pl.BlockSpec((bm, bk), lambda i, j, kk: (i, kk))
        b_spec = pl.BlockSpec((bn, bk), lambda i, j, kk: (rot(j, b_rot), kk))
        dims = (((1,), (1,)), ((), ()))
    else:
        assert b_rot is None
        a_spec = pl.BlockSpec((bk, bm), lambda i, j, kk: (kk, i))
        b_spec = pl.BlockSpec((bk, bn), lambda i, j, kk: (kk, j))
        dims = (((0,), (0,)), ((), ()))

    def body(a_ref, b_ref, o_ref, acc_ref):
        kk = pl.program_id(2)

        @pl.when(kk == 0)
        def _():
            acc_ref[...] = jnp.zeros_like(acc_ref)

        acc_ref[...] += lax.dot_general(a_ref[...], b_ref[...], dims, preferred_element_type=F32)

        @pl.when(kk == nk - 1)
        def _():
            o_ref[...] = acc_ref[...].astype(o_ref.dtype)

    return pl.pallas_call(
        body, name=name,
        grid=(m // bm, n // bn, nk),
        in_specs=[a_spec, b_spec],
        out_specs=pl.BlockSpec((bm, bn), lambda i, j, kk: (rot(i, o_rot), j)),
        out_shape=jax.ShapeDtypeStruct((m, n), out_dtype),
        scratch_shapes=[pltpu.VMEM((bm, bn), F32)],
        compiler_params=_params(("parallel", "parallel", "arbitrary")),
    )(a, b)


def rowwise(name, fn, rows, consts, out_rows, out_sums, ts, alias=None):
    rows = [r if isinstance(r, tuple) else (r, r.shape[1], 0) for r in rows]
    out_rows = [o if len(o) == 4 else (o[0], o[1], o[0], 0) for o in out_rows]
    s = rows[0][0].shape[0]
    nt = s // ts
    nr, nc, no, ns = len(rows), len(consts), len(out_rows), len(out_sums)
    in_specs = [pl.BlockSpec((ts, w), functools.partial(lambda i, cb: (i, cb), cb=cb)) for (_, w, cb) in rows]
    in_specs += [pl.BlockSpec(c.shape, lambda i: (0, 0)) for c in consts]
    operands = [r[0] for r in rows] + list(consts)
    aliases = {}
    if alias is not None:
        in_specs.append(pl.BlockSpec(memory_space=pl.ANY))
        operands.append(alias[0])
        aliases = {nr + nc: alias[1]}
    out_shape = [jax.ShapeDtypeStruct((s, tw), dt) for (_, dt, tw, _) in out_rows]
    out_shape += [jax.ShapeDtypeStruct((1, w), F32) for w in out_sums]
    out_specs = [pl.BlockSpec((ts, w), functools.partial(lambda i, cb: (i, cb), cb=cb)) for (w, _, _, cb) in out_rows]
    out_specs += [pl.BlockSpec((1, w), lambda i: (0, 0)) for w in out_sums]
    n_in = len(operands)

    def body(*refs):
        ins, outs = refs[:nr + nc], refs[n_in:]
        i = pl.program_id(0)
        ro, so = fn(*[r[...] for r in ins])
        for q in range(no):
            outs[q][...] = ro[q].astype(outs[q].dtype)
        if ns:
            @pl.when(i == 0)
            def _():
                for q in range(ns):
                    outs[no + q][...] = jnp.zeros_like(outs[no + q])

            for q in range(ns):
                outs[no + q][...] += so[q]

    return pl.pallas_call(
        body, name=name, grid=(nt,),
        in_specs=in_specs, out_specs=out_specs, out_shape=out_shape, input_output_aliases=aliases,
        compiler_params=_params(("arbitrary",) if ns else ("parallel",)),
    )(*operands)


def _colsum(v):
    return jnp.sum(v, axis=0, keepdims=True)


def _rms_stats(xv):
    r = lax.rsqrt(jnp.mean(xv * xv, axis=-1, keepdims=True) + EPS)
    return r, xv * r


def _rms_bwd(dxhat, xhat, r):
    return r * (dxhat - xhat * jnp.mean(dxhat * xhat, axis=-1, keepdims=True))


def _gelu(v):
    k = math.sqrt(2.0 / math.pi)
    t = jnp.tanh(k * (v + 0.044715 * v * v * v))
    return 0.5 * v * (1.0 + t), t


def _gelu_grad(v, t):
    k = math.sqrt(2.0 / math.pi)
    return 0.5 * (1.0 + t) + 0.5 * v * (1.0 - t * t) * k * (1.0 + 3.0 * 0.044715 * v * v)


CONV_TS = 256
CONV_CH = 64


def _ln_fwd(yc, g, b):
    mu = jnp.mean(yc, axis=-1, keepdims=True)
    xc = yc - mu
    rstd = lax.rsqrt(jnp.mean(xc * xc, axis=-1, keepdims=True) + EPS)
    nhat = xc * rstd
    return nhat, rstd, nhat * g + b


def conv_fwd(z, w32, cb, lg, lb):
    s = z.shape[0]
    ts = CONV_TS
    nt = s // ts
    hb = ts // HALO

    def body(a_ref, g_ref, ah_ref, gh_ref, w_ref, cb_ref, lg_ref, lb_ref, yc_ref, s_ref, ubuf):
        i = pl.program_id(0)
        first = (i > 0).astype(F32)
        ubuf[0:HALO, :] = ah_ref[...] * _sig(gh_ref[...]) * first
        ubuf[HALO:HALO + ts, :] = a_ref[...] * _sig(g_ref[...])
        for c0 in range(0, ts, CONV_CH):
            acc = jnp.zeros((CONV_CH, CW), F32)
            for k in range(KC):
                acc = acc + w_ref[k:k + 1, :] * ubuf[pl.ds(c0 + k + 2, CONV_CH), :]
            yc = acc + cb_ref[...]
            yc_ref[c0:c0 + CONV_CH, :] = yc
            _, _, ln = _ln_fwd(yc, lg_ref[...], lb_ref[...])
            s_ref[c0:c0 + CONV_CH, :] = (ln * _sig(ln)).astype(s_ref.dtype)

    cur = lambda cbk: pl.BlockSpec((ts, CW), functools.partial(lambda i, q: (i, q), q=cbk))
    prev = lambda cbk: pl.BlockSpec((HALO, CW), functools.partial(lambda i, q: (jnp.maximum(i * hb - 1, 0), q), q=cbk))
    const = lambda a: pl.BlockSpec(a.shape, lambda i: (0, 0))
    return pl.pallas_call(
        body, name="conv_fwd", grid=(nt,),
        in_specs=[cur(ZB_A), cur(ZB_G), prev(ZB_A), prev(ZB_G), const(w32), const(cb), const(lg), const(lb)],
        out_specs=[pl.BlockSpec((ts, CW), lambda i: (i, 0)), pl.BlockSpec((ts, CW), lambda i: (i, 0))],
        out_shape=[jax.ShapeDtypeStruct((s, CW), F32), jax.ShapeDtypeStruct((s, CW), BF16)],
        scratch_shapes=[pltpu.VMEM((HALO + ts, CW), F32)],
        compiler_params=_params(("parallel",)),
    )(z, z, z, z, w32, cb, lg, lb)


def conv_bwd(ds, yc, z, w32, lg, lb, dz):
    s = z.shape[0]
    ts = CONV_TS
    nt = s // ts
    hb = ts // HALO
    last_hb = s // HALO - 1

    def ln_bwd(dsv, ycv, g, b):
        nhat, rstd, ln = _ln_fwd(ycv, g, b)
        sg = _sig(ln)
        dln = dsv * (sg * (1.0 + ln * (1.0 - sg)))
        dnh = dln * g
        dyc = rstd * (dnh - jnp.mean(dnh, axis=-1, keepdims=True)
                      - nhat * jnp.mean(dnh * nhat, axis=-1, keepdims=True))
        return dyc, dln, nhat

    def body(ds_ref, yc_ref, dsn_ref, ycn_ref, a_ref, g_ref, ah_ref, gh_ref, w_ref, lg_ref, lb_ref, dz_in,
             dz_ref, dlg_ref, dlb_ref, dcb_ref, dw_ref, dbuf, ubuf):
        i = pl.program_id(0)

        @pl.when(i == 0)
        def _():
            dlg_ref[...] = jnp.zeros_like(dlg_ref)
            dlb_ref[...] = jnp.zeros_like(dlb_ref)
            dcb_ref[...] = jnp.zeros_like(dcb_ref)
            dw_ref[...] = jnp.zeros_like(dw_ref)

        lg, lb = lg_ref[...], lb_ref[...]
        dyc, dln, nhat = ln_bwd(ds_ref[...], yc_ref[...], lg, lb)
        dlg_ref[...] += _colsum(dln * nhat)
        dlb_ref[...] += _colsum(dln)
        dcb_ref[...] += _colsum(dyc)
        dbuf[0:ts, :] = dyc
        nxt = (i < nt - 1).astype(F32)
        dbuf[ts:ts + HALO, :] = ln_bwd(dsn_ref[...], ycn_ref[...], lg, lb)[0] * nxt
        first = (i > 0).astype(F32)
        ubuf[0:HALO, :] = ah_ref[...] * _sig(gh_ref[...]) * first
        ubuf[HALO:HALO + ts, :] = a_ref[...] * _sig(g_ref[...])
        for c0 in range(0, ts, CONV_CH):
            du = jnp.zeros((CONV_CH, CW), F32)
            dyc_c = dbuf[c0:c0 + CONV_CH, :]
            for k in range(KC):
                du = du + w_ref[k:k + 1, :] * dbuf[pl.ds(c0 + KC - 1 - k, CONV_CH), :]
                dw_ref[k:k + 1, :] += _colsum(dyc_c * ubuf[pl.ds(c0 + k + 2, CONV_CH), :])
            av = a_ref[c0:c0 + CONV_CH, :]
            sg = _sig(g_ref[c0:c0 + CONV_CH, :])
            dz_ref[c0:c0 + CONV_CH, 0:CW] = (du * sg).astype(dz_ref.dtype)
            dz_ref[c0:c0 + CONV_CH, CW:2 * CW] = (du * av * sg * (1.0 - sg)).astype(dz_ref.dtype)

    cur = lambda w, cbk: pl.BlockSpec((ts, w), functools.partial(lambda i, q: (i, q), q=cbk))
    prev = lambda cbk: pl.BlockSpec((HALO, CW), functools.partial(lambda i, q: (jnp.maximum(i * hb - 1, 0), q), q=cbk))
    nxt_spec = pl.BlockSpec((HALO, CW), lambda i: (jnp.minimum((i + 1) * hb, last_hb), 0))
    const = lambda a: pl.BlockSpec(a.shape, lambda i: (0, 0))
    acc = lambda r: pl.BlockSpec((r, CW), lambda i: (0, 0))
    return pl.pallas_call(
        body, name="conv_bwd", grid=(nt,),
        in_specs=[cur(CW, 0), cur(CW, 0), nxt_spec, nxt_spec, cur(CW, ZB_A), cur(CW, ZB_G), prev(ZB_A), prev(ZB_G),
                  const(w32), const(lg), const(lb), pl.BlockSpec(memory_space=pl.ANY)],
        out_specs=[pl.BlockSpec((ts, 2 * CW), lambda i: (i, ZB_A // 2)), acc(1), acc(1), acc(1), acc(HALO)],
        out_shape=[jax.ShapeDtypeStruct(dz.shape, dz.dtype), jax.ShapeDtypeStruct((1, CW), F32),
                   jax.ShapeDtypeStruct((1, CW), F32), jax.ShapeDtypeStruct((1, CW), F32),
                   jax.ShapeDtypeStruct((HALO, CW), F32)],
        scratch_shapes=[pltpu.VMEM((ts + HALO, CW), F32), pltpu.VMEM((HALO + ts, CW), F32)],
        input_output_aliases={11: 0},
        compiler_params=_params(("arbitrary",)),
    )(ds, yc, ds, yc, z, z, z, z, w32, lg, lb, dz)


SSM_TS = 512
GRP = 8


def _cmul(ar, ai, br, bi):
    return ar * br - ai * bi, ar * bi + ai * br


def _scan_tables(ar, ai, reverse):
    n = ar.shape[1]
    row = lax.broadcasted_iota(jnp.int32, (GRP, n), 0)
    dist = (GRP - 1 - row) if reverse else row
    one_r = jnp.broadcast_to(ar, (GRP, n))
    one_i = jnp.broadcast_to(ai, (GRP, n))
    p2r, p2i = _cmul(one_r, one_i, one_r, one_i)
    p4r, p4i = _cmul(p2r, p2i, p2r, p2i)
    steps = []
    for sft, (pr, pi) in ((1, (one_r, one_i)), (2, (p2r, p2i)), (4, (p4r, p4i))):
        keep = dist >= sft
        steps.append((jnp.where(keep, pr, 0.0), jnp.where(keep, pi, 0.0)))
    cr, ci = one_r, one_i
    accr, acci = one_r, one_i
    for e in range(1, GRP):
        cr, ci = _cmul(cr, ci, one_r, one_i)
        accr = jnp.where(dist == e, cr, accr)
        acci = jnp.where(dist == e, ci, acci)
    return steps, (accr, acci)


def _scan_group(xr, xi, steps, carry_tab, cr, ci, reverse):
    for sft, (tr, ti) in zip((1, 2, 4), steps):
        amt = (GRP - sft) if reverse else sft
        sr = pltpu.roll(xr, amt, 0)
        si = pltpu.roll(xi, amt, 0)
        xr, xi = xr + tr * sr - ti * si, xi + tr * si + ti * sr
    pr, pi = carry_tab
    xr = xr + pr * cr - pi * ci
    xi = xi + pr * ci + pi * cr
    return xr, xi


def ssm_fwd(z, wb_re, wb_im, wc, e_re, e_im, dvec):
    s = z.shape[0]
    ts = SSM_TS
    nt = s // ts
    ucol0 = ZB_U * CW // CB

    def body(u_ref, wbr_ref, wbi_ref, wc_ref, er_ref, ei_ref, d_ref, xr_ref, xi_ref, y_ref, gl_ref, car_r, car_i):
        i = pl.program_id(1)

        @pl.when(i == 0)
        def _():
            car_r[...] = jnp.zeros_like(car_r)
            car_i[...] = jnp.zeros_like(car_i)

        u = u_ref[...]
        ub = u.astype(BF16)
        xr_ref[...] = jnp.dot(ub, wbr_ref[0], preferred_element_type=F32)
        xi_ref[...] = jnp.dot(ub, wbi_ref[0], preferred_element_type=F32)
        steps, ctab = _scan_tables(er_ref[0], ei_ref[0], False)

        def grp(r, carry):
            cr, ci = carry
            r0 = pl.multiple_of(r * GRP, GRP)
            xr, xi = _scan_group(xr_ref[pl.ds(r0, GRP), :], xi_ref[pl.ds(r0, GRP), :], steps, ctab, cr, ci, False)
            xr_ref[pl.ds(r0, GRP), :] = xr
            xi_ref[pl.ds(r0, GRP), :] = xi
            return (jnp.broadcast_to(xr[GRP - 1:GRP, :], (GRP, SB)), jnp.broadcast_to(xi[GRP - 1:GRP, :], (GRP, SB)))

        cr, ci = lax.fori_loop(0, ts // GRP, grp, (car_r[...], car_i[...]))
        car_r[...] = cr
        car_i[...] = ci
        y = (jnp.dot(xr_ref[...].astype(BF16), wc_ref[0, 0:SB, :], preferred_element_type=F32)
             + jnp.dot(xi_ref[...].astype(BF16), wc_ref[0, SB:2 * SB, :], preferred_element_type=F32)
             + d_ref[0] * u)
        y_ref[...] = y
        gl_ref[...] = _gelu(y)[0].astype(gl_ref.dtype)

    blk3 = lambda a: pl.BlockSpec((1,) + a.shape[1:], lambda j, i: (j, 0, 0))
    return pl.pallas_call(
        body, name="ssm_fwd", grid=(NBLK, nt),
        in_specs=[pl.BlockSpec((ts, CB), lambda j, i: (i, ucol0 + j)),
                  blk3(wb_re), blk3(wb_im), blk3(wc), blk3(e_re), blk3(e_im), blk3(dvec)],
        out_specs=[pl.BlockSpec((ts, SB), lambda j, i: (i, j)), pl.BlockSpec((ts, SB), lambda j, i: (i, j)),
                   pl.BlockSpec((ts, CB), lambda j, i: (i, j)), pl.BlockSpec((ts, CB), lambda j, i: (i, j))],
        out_shape=[jax.ShapeDtypeStruct((s, NST), F32), jax.ShapeDtypeStruct((s, NST), F32),
                   jax.ShapeDtypeStruct((s, SW), F32), jax.ShapeDtypeStruct((s, SW), BF16)],
        scratch_shapes=[pltpu.VMEM((GRP, SB), F32), pltpu.VMEM((GRP, SB), F32)],
        compiler_params=_params(("parallel", "arbitrary")),
    )(z, wb_re, wb_im, wc, e_re, e_im, dvec)


def ssm_bwd(dgl, ypre, z, xs_re, xs_im, wbt_re, wbt_im, wct, e_re, e_im, dvec, dz):
    s = z.shape[0]
    ts = SSM_TS
    nt = s // ts
    ucol0 = ZB_U * CW // CB
    tn_dims = (((0,), (0,)), ((), ()))

    def body(dgl_ref, y_ref, u_ref, xr_ref, xi_ref, wbtr_ref, wbti_ref, wct_ref, er_ref, ei_ref, d_ref, dz_in,
             du_ref, dd_ref, dar_ref, dai_ref, dwbr_ref, dwbi_ref, dwc_ref,
             lr_ref, li_ref, car_r, car_i, acc_r, acc_i):
        i = pl.program_id(1)

        @pl.when(i == 0)
        def _():
            for ref in (car_r, car_i, acc_r, acc_i, dd_ref, dwbr_ref, dwbi_ref, dwc_ref):
                ref[...] = jnp.zeros_like(ref)

        u = u_ref[...]
        y = y_ref[...]
        dy = dgl_ref[...] * _gelu_grad(y, _gelu(y)[1])
        dd_ref[0] += _colsum(dy * u)
        dyb = dy.astype(BF16)
        dxo = jnp.dot(dyb, wct_ref[0], preferred_element_type=F32)
        lr_ref[...] = dxo[:, 0:SB]
        li_ref[...] = dxo[:, SB:2 * SB]
        steps, ctab = _scan_tables(er_ref[0], -ei_ref[0], True)
        row = lax.broadcasted_iota(jnp.int32, (GRP, SB), 0)

        def grp(q, carry):
            cr, ci, ar, ai = carry
            r0 = pl.multiple_of((ts // GRP - 1 - q) * GRP, GRP)
            lr, li = _scan_group(lr_ref[pl.ds(r0, GRP), :], li_ref[pl.ds(r0, GRP), :], steps, ctab, cr, ci, True)
            lr_ref[pl.ds(r0, GRP), :] = lr
            li_ref[pl.ds(r0, GRP), :] = li
            nr = jnp.where(row == GRP - 1, cr, pltpu.roll(lr, GRP - 1, 0))
            ni = jnp.where(row == GRP - 1, ci, pltpu.roll(li, GRP - 1, 0))
            xr = xr_ref[pl.ds(r0, GRP), :]
            xi = xi_ref[pl.ds(r0, GRP), :]
            ar = ar + nr * xr + ni * xi
            ai = ai + ni * xr - nr * xi
            return (jnp.broadcast_to(lr[0:1, :], (GRP, SB)), jnp.broadcast_to(li[0:1, :], (GRP, SB)), ar, ai)

        cr, ci, ar, ai = lax.fori_loop(0, ts // GRP, grp, (car_r[...], car_i[...], acc_r[...], acc_i[...]))
        car_r[...] = cr
        car_i[...] = ci
        acc_r[...] = ar
        acc_i[...] = ai

        @pl.when(i == nt - 1)
        def _():
            dar_ref[0] = _colsum(ar)
            dai_ref[0] = _colsum(ai)

        lrb = lr_ref[...].astype(BF16)
        lib = li_ref[...].astype(BF16)
        du = (jnp.dot(lrb, wbtr_ref[0], preferred_element_type=F32)
              + jnp.dot(lib, wbti_ref[0], preferred_element_type=F32) + d_ref[0] * dy)
        du_ref[...] = du.astype(du_ref.dtype)
        ub = u.astype(BF16)
        dwbr_ref[0] += lax.dot_general(ub, lrb, tn_dims, preferred_element_type=F32)
        dwbi_ref[0] += lax.dot_general(ub, lib, tn_dims, preferred_element_type=F32)
        dwc_ref[0, 0:SB, :] += lax.dot_general(xr_ref[...].astype(BF16), dyb, tn_dims, preferred_element_type=F32)
        dwc_ref[0, SB:2 * SB, :] += lax.dot_general(xi_ref[...].astype(BF16), dyb, tn_dims, preferred_element_type=F32)

    rev = lambda i: nt - 1 - i
    blk3 = lambda a: pl.BlockSpec((1,) + a.shape[1:], lambda j, i: (j, 0, 0))
    acc3 = lambda r, c: pl.BlockSpec((1, r, c), lambda j, i: (j, 0, 0))
    return pl.pallas_call(
        body, name="ssm_bwd", grid=(NBLK, nt),
        in_specs=[pl.BlockSpec((ts, CB), lambda j, i: (rev(i), j)), pl.BlockSpec((ts, CB), lambda j, i: (rev(i), j)),
                  pl.BlockSpec((ts, CB), lambda j, i: (rev(i), ucol0 + j)),
                  pl.BlockSpec((ts, SB), lambda j, i: (rev(i), j)), pl.BlockSpec((ts, SB), lambda j, i: (rev(i), j)),
                  blk3(wbt_re), blk3(wbt_im), blk3(wct), blk3(e_re), blk3(e_im), blk3(dvec),
                  pl.BlockSpec(memory_space=pl.ANY)],
        out_specs=[pl.BlockSpec((ts, CB), lambda j, i: (rev(i), ucol0 + j)),
                   acc3(1, CB), acc3(1, SB), acc3(1, SB), acc3(CB, SB), acc3(CB, SB), acc3(2 * SB, CB)],
        out_shape=[jax.ShapeDtypeStruct(dz.shape, dz.dtype),
                   jax.ShapeDtypeStruct((NBLK, 1, CB), F32),
                   jax.ShapeDtypeStruct((NBLK, 1, SB), F32), jax.ShapeDtypeStruct((NBLK, 1, SB), F32),
                   jax.ShapeDtypeStruct((NBLK, CB, SB), F32), jax.ShapeDtypeStruct((NBLK, CB, SB), F32),
                   jax.ShapeDtypeStruct((NBLK, 2 * SB, CB), F32)],
        scratch_shapes=[pltpu.VMEM((ts, SB), F32), pltpu.VMEM((ts, SB), F32)] + [pltpu.VMEM((GRP, SB), F32)] * 4,
        input_output_aliases={11: 0},
        compiler_params=_params(("parallel", "arbitrary")),
    )(dgl, ypre, z, xs_re, xs_im, wbt_re, wbt_im, wct, e_re, e_im, dvec, dz)


def _disc(a_re, a_im, log_dt, b_re, b_im, expand):
    dt = jnp.dot(expand, jnp.exp(log_dt), preferred_element_type=F32, precision=lax.Precision.HIGHEST)
    mag = jnp.exp(dt * a_re)
    e_re, e_im = mag * jnp.cos(dt * a_im), mag * jnp.sin(dt * a_im)
    n_re, n_im = e_re - 1.0, e_im
    den = a_re * a_re + a_im * a_im
    q_re = (n_re * a_re + n_im * a_im) / den
    q_im = (n_im * a_re - n_re * a_im) / den
    return e_re, e_im, q_re * b_re - q_im * b_im, q_re * b_im + q_im * b_re


def _whole(a):
    return pl.BlockSpec(a.shape, functools.partial(lambda n: (0,) * n, n=a.ndim))


def disc_fwd(a_re, a_im, log_dt, b_re, b_im, expand):
    def body(ar, ai, ld, br, bi, ex, er_o, ei_o, bbr_o, bbi_o):
        er, ei, bbr, bbi = _disc(ar[...], ai[...], ld[...], br[...], bi[...], ex[...])
        er_o[...] = er
        ei_o[...] = ei
        bbr_o[...] = bbr
        bbi_o[...] = bbi

    ins = (a_re, a_im, log_dt, b_re, b_im, expand)
    outs = [jax.ShapeDtypeStruct(a_re.shape, F32)] * 2 + [jax.ShapeDtypeStruct(b_re.shape, F32)] * 2
    return pl.pallas_call(body, name="disc_fwd", in_specs=[_whole(a) for a in ins],
                          out_specs=[_whole(o) for o in outs], out_shape=outs, compiler_params=_params())(*ins)


def disc_bwd(a_re, a_im, log_dt, b_re, b_im, expand, de_re, de_im, dbb_re, dbb_im):
    def body(ar, ai, ld, br, bi, ex, der, dei, dbr, dbi, o_ar, o_ai, o_ld, o_br, o_bi):
        exv = ex[...]
        _, vjp = jax.vjp(lambda *p: _disc(*p, exv), ar[...], ai[...], ld[...], br[...], bi[...])
        g = vjp((der[...], dei[...], dbr[...], dbi[...]))
        for o, v in zip((o_ar, o_ai, o_ld, o_br, o_bi), g):
            o[...] = v

    ins = (a_re, a_im, log_dt, b_re, b_im, expand, de_re, de_im, dbb_re, dbb_im)
    outs = [jax.ShapeDtypeStruct(a.shape, F32) for a in (a_re, a_im, log_dt, b_re, b_im)]
    return pl.pallas_call(body, name="disc_bwd", in_specs=[_whole(a) for a in ins],
                          out_specs=[_whole(o) for o in outs], out_shape=outs, compiler_params=_params())(*ins)


def mod_fwd(c_all, w_ada, b_cols):
    def body(c_ref, w_ref, b_ref, act_ref, mod_ref):
        cv = c_ref[...]
        act = cv * _sig(cv)
        act_ref[...] = act
        mod_ref[...] = jnp.dot(act, w_ref[...], preferred_element_type=F32, precision=lax.Precision.HIGHEST) + b_ref[...]

    ins = (c_all, w_ada, b_cols)
    outs = [jax.ShapeDtypeStruct(c_all.shape, F32), jax.ShapeDtypeStruct((NDEV, w_ada.shape[1]), F32)]
    return pl.pallas_call(body, name="mod_fwd", in_specs=[_whole(a) for a in ins],
                          out_specs=[_whole(o) for o in outs], out_shape=outs, compiler_params=_params())(*ins)


def ada_grad(act_all, dmod_cols):
    def body(a_ref, d_ref, o_ref):
        o_ref[...] = lax.dot_general(a_ref[...], d_ref[...], (((0,), (0,)), ((), ())),
                                     preferred_element_type=F32, precision=lax.Precision.HIGHEST)

    out = jax.ShapeDtypeStruct((act_all.shape[1], dmod_cols.shape[1]), F32)
    return pl.pallas_call(body, name="ada_grad", in_specs=[_whole(act_all), _whole(dmod_cols)],
                          out_specs=_whole(out), out_shape=out, compiler_params=_params())(act_all, dmod_cols)


def _adam_math(w, g, m, v):
    m2 = ADAM_B1 * m + (1.0 - ADAM_B1) * g
    v2 = ADAM_B2 * v + (1.0 - ADAM_B2) * (g * g)
    m_hat = m2 / (1.0 - ADAM_B1 ** ADAM_STEP)
    v_hat = v2 / (1.0 - ADAM_B2 ** ADAM_STEP)
    delta = -ADAM_LR * (m_hat / (jnp.sqrt(v_hat) + ADAM_EPS) + ADAM_WD * w)
    return delta, m2, v2


def adam(name, w, g, m, v):
    r, c = w.shape
    tr = r
    for cand in (256, 128, 64, 32, 16, 8):
        if r % cand == 0 and r > cand:
            tr = cand
            break

    def body(w_ref, g_ref, m_ref, v_ref, d_o, m_o, v_o):
        d, m2, v2 = _adam_math(w_ref[...], g_ref[...], m_ref[...], v_ref[...])
        d_o[...] = d
        m_o[...] = m2
        v_o[...] = v2

    spec = pl.BlockSpec((tr, c), lambda i: (i, 0))
    out = jax.ShapeDtypeStruct((r, c), F32)
    return pl.pallas_call(body, name=name, grid=(r // tr,), in_specs=[spec] * 4, out_specs=[spec] * 3,
                          out_shape=[out] * 3, compiler_params=_params(("parallel",)))(w, g, m, v)


def _rows_tile(r, most):
    best = None
    for t in range(16, min(r, most) + 1, 16):
        if r % t == 0:
            best = t
    assert best is not None, r
    return best


def sum_slots(name, slots, out_dtype=F32):
    n, r, c = slots.shape
    tr = _rows_tile(r, max(16, (2 * 1024 * 1024) // (n * c)))

    def body(s_ref, o_ref):
        acc = s_ref[0].astype(F32)
        for q in range(1, n):
            acc = acc + s_ref[q].astype(F32)
        o_ref[...] = acc.astype(o_ref.dtype)

    return pl.pallas_call(body, name=name, grid=(r // tr,),
                          in_specs=[pl.BlockSpec((n, tr, c), lambda i: (0, i, 0))],
                          out_specs=pl.BlockSpec((tr, c), lambda i: (i, 0)),
                          out_shape=jax.ShapeDtypeStruct((r, c), out_dtype), compiler_params=_params(("parallel",)))(slots)


HBM_SPEC = pl.BlockSpec(memory_space=pltpu.HBM)


def _coords():
    return lax.axis_index("x"), lax.axis_index("y"), lax.axis_index("c")


def _linear(x, y, c):
    return 4 * x + 2 * y + c


def all_gather(name, shards):
    nq = len(shards)

    def body(*refs):
        xs, outs = refs[:nq], refs[nq:2 * nq]
        send_sems, recv_sems, local_sems = refs[2 * nq:2 * nq + 3]
        bufs = refs[2 * nq + 3:]
        x, y, cc = _coords()
        me, sibling = (x, y, cc), (x, y, 1 - cc)
        chips = [(1 - x, y), (x, 1 - y), (1 - x, 1 - y)]

        def slot(q, px, py, pc):
            return outs[q].at[_linear(px, py, pc)]

        def copy(q, k, block, to, src=None):
            return pltpu.make_async_remote_copy(
                src_ref=slot(q, *block) if src is None else src, dst_ref=slot(q, *block),
                send_sem=send_sems.at[7 * q + k], recv_sem=recv_sems.at[7 * q + k], device_id=to, device_id_type=MESH)

        loads = [pltpu.make_async_copy(xs[q], bufs[q], local_sems.at[q]) for q in range(nq)]
        for cp in loads:
            cp.start()
        for cp in loads:
            cp.wait()
        mine = [pltpu.make_async_copy(bufs[q], slot(q, *me), local_sems.at[q]) for q in range(nq)]
        first = []
        for q in range(nq):
            first.append(copy(q, 0, me, sibling, src=bufs[q]))
            first += [copy(q, 1 + j, me, (*chip, cc), src=bufs[q]) for j, chip in enumerate(chips)]
        for cp in mine + first:
            cp.start()
        passed = []
        for q in range(nq):
            for j, chip in enumerate(chips):
                copy(q, 1 + j, (*chip, cc), me).wait_recv()
                passed.append(copy(q, 4 + j, (*chip, cc), sibling))
                passed[-1].start()
        for q in range(nq):
            copy(q, 0, sibling, me).wait_recv()
            for j, chip in enumerate(chips):
                copy(q, 4 + j, (*chip, 1 - cc), me).wait_recv()
        for cp in first + passed:
            cp.wait_send()
        for cp in mine:
            cp.wait()

    return pl.pallas_call(
        body, name=name, in_specs=[HBM_SPEC] * nq, out_specs=[HBM_SPEC] * nq,
        out_shape=[jax.ShapeDtypeStruct((NDEV,) + s.shape, s.dtype) for s in shards],
        scratch_shapes=[pltpu.SemaphoreType.DMA((7 * nq,)), pltpu.SemaphoreType.DMA((7 * nq,)),
                        pltpu.SemaphoreType.DMA((nq,))] + [pltpu.VMEM(s.shape, s.dtype) for s in shards],
    )(*shards)


NCHIP = 4


def pair_exchange(name, grads):
    nq = len(grads)

    def body(*refs):
        gs, outs = refs[:nq], refs[nq:2 * nq]
        send_sems, recv_sems = refs[2 * nq:]
        x, y, cc = _coords()
        rem = []
        for q in range(nq):
            for chip in range(NCHIP):
                n = NCHIP * q + chip
                rem.append(pltpu.make_async_remote_copy(
                    src_ref=gs[q].at[2 * chip + 1 - cc], dst_ref=outs[q].at[chip],
                    send_sem=send_sems.at[n], recv_sem=recv_sems.at[n], device_id=(x, y, 1 - cc), device_id_type=MESH))
        for cp in rem:
            cp.start()
        for cp in rem:
            cp.wait_recv()
        for cp in rem:
            cp.wait_send()

    ncopy = NCHIP * nq
    return pl.pallas_call(
        body, name=name, in_specs=[HBM_SPEC] * nq, out_specs=[HBM_SPEC] * nq,
        out_shape=[jax.ShapeDtypeStruct((NCHIP,) + g.shape[1:], g.dtype) for g in grads],
        scratch_shapes=[pltpu.SemaphoreType.DMA((ncopy,))] * 2,
    )(*grads)


def pair_sum(name, g, recv):
    _, r, c = g.shape
    tr = _rows_tile(r, 512)

    def body(g_ref, r_ref, o_ref):
        own = jnp.where(lax.axis_index("c") == 0, g_ref[0, 0], g_ref[0, 1])
        o_ref[0] = (own.astype(F32) + r_ref[0].astype(F32)).astype(o_ref.dtype)

    return pl.pallas_call(
        body, name=name, grid=(NCHIP, r // tr),
        in_specs=[pl.BlockSpec((1, 2, tr, c), lambda k, i: (k, 0, i, 0)), pl.BlockSpec((1, tr, c), lambda k, i: (k, i, 0))],
        out_specs=pl.BlockSpec((1, tr, c), lambda k, i: (k, i, 0)),
        out_shape=jax.ShapeDtypeStruct((NCHIP, r, c), g.dtype), compiler_params=_params(("parallel", "parallel")),
    )(g.reshape(NCHIP, 2, r, c), recv)


def chip_exchange(name, partials):
    nq = len(partials)

    def body(*refs):
        ins, outs = refs[:nq], refs[nq:2 * nq]
        send_sems, recv_sems = refs[2 * nq:]
        x, y, cc = _coords()
        copies = []
        for q in range(nq):
            for k in range(1, NCHIP):
                fx, fy = (k >> 1) & 1, k & 1
                px = x + fx - 2 * fx * x
                py = y + fy - 2 * fy * y
                n = (NCHIP - 1) * q + k - 1
                copies.append(pltpu.make_async_remote_copy(
                    src_ref=ins[q].at[2 * px + py], dst_ref=outs[q].at[k - 1],
                    send_sem=send_sems.at[n], recv_sem=recv_sems.at[n], device_id=(px, py, cc), device_id_type=MESH))
        for cp in copies:
            cp.start()
        for cp in copies:
            cp.wait_recv()
        for cp in copies:
            cp.wait_send()

    ncopy = (NCHIP - 1) * nq
    return pl.pallas_call(
        body, name=name, in_specs=[HBM_SPEC] * nq, out_specs=[HBM_SPEC] * nq,
        out_shape=[jax.ShapeDtypeStruct((NCHIP - 1,) + p.shape[1:], p.dtype) for p in partials],
        scratch_shapes=[pltpu.SemaphoreType.DMA((ncopy,))] * 2,
    )(*partials)


def chip_sum(name, partial, recv):
    _, r, c = partial.shape
    tr = _rows_tile(r, 512)

    def body(p_ref, r_ref, o_ref):
        chip = 2 * lax.axis_index("x") + lax.axis_index("y")
        own = p_ref[0]
        for k in range(1, NCHIP):
            own = jnp.where(chip == k, p_ref[k], own)
        acc = own.astype(F32)
        for k in range(NCHIP - 1):
            acc = acc + r_ref[k].astype(F32)
        o_ref[...] = acc

    return pl.pallas_call(
        body, name=name, grid=(r // tr,),
        in_specs=[pl.BlockSpec((NCHIP, tr, c), lambda i: (0, i, 0)), pl.BlockSpec((NCHIP - 1, tr, c), lambda i: (0, i, 0))],
        out_specs=pl.BlockSpec((tr, c), lambda i: (i, 0)),
        out_shape=jax.ShapeDtypeStruct((r, c), F32), compiler_params=_params(("parallel",)),
    )(partial, recv)


def _block_diag(w, rows_per, cols_per):
    w = w.reshape(NBLK, 8, rows_per, cols_per)
    eye = jnp.eye(8, dtype=w.dtype)
    out = w[:, :, :, None, :] * eye[None, :, None, :, None]
    return out.reshape(NBLK, 8 * rows_per, 8 * cols_per)


def _diag_blocks(wd, rows_per, cols_per):
    wd = wd.reshape(NBLK, 8, rows_per, 8, cols_per)
    idx = jnp.arange(8)
    return wd[:, idx, :, idx, :].transpose(1, 0, 2, 3).reshape(NG, rows_per, cols_per)


def _pad_rows(v, mult):
    n = v.shape[0]
    return jnp.pad(v, (0, (-n) % mult))


def kernel(x, c, w_ada, b_ada, norm1_g, w_in, conv_w, conv_b, conv_ln_g, conv_ln_b, conv_proj, ssm_a_re, ssm_a_im, ssm_b_re, ssm_b_im, ssm_c_re, ssm_c_im, ssm_d, ssm_log_dt, ssm_glu, w_out, norm2_g, w_ffn_in, w_ffn_out, final_g, loss_target, m_w_ada, m_b_ada, m_norm1_g, m_w_in, m_conv_w, m_conv_b, m_conv_ln_g, m_conv_ln_b, m_conv_proj, m_ssm_a_re, m_ssm_a_im, m_ssm_b_re, m_ssm_b_im, m_ssm_c_re, m_ssm_c_im, m_ssm_d, m_ssm_log_dt, m_ssm_glu, m_w_out, m_norm2_g, m_w_ffn_in, m_w_ffn_out, m_final_g, v_w_ada, v_b_ada, v_norm1_g, v_w_in, v_conv_w, v_conv_b, v_conv_ln_g, v_conv_ln_b, v_conv_proj, v_ssm_a_re, v_ssm_a_im, v_ssm_b_re, v_ssm_b_im, v_ssm_c_re, v_ssm_c_im, v_ssm_d, v_ssm_log_dt, v_ssm_glu, v_w_out, v_norm2_g, v_w_ffn_in, v_w_ffn_out, v_final_g):
    me = _linear(*_coords())
    xs = x[0]
    tgt = loss_target[0]
    seq = xs.shape[0]

    c_all, cw_g = all_gather("gather_c_conv_w", [c, conv_w[0]])
    parts = [w_in[0].T, conv_proj[0].T, ssm_glu[0].T, w_out[0], w_ffn_in[0].T, w_ffn_out[0]]
    part_rows = [p.size // D for p in parts]
    offs = [0]
    for r in part_rows:
        offs.append(offs[-1] + r)
    gathered = all_gather("gather_weights", [p.astype(BF16) for p in parts])
    w_in_t, conv_proj_t, ssm_glu_t, w_out_f, w_ffn_in_t, w_ffn_out_f = [
        g.reshape(NDEV * g.shape[1], g.shape[2]) for g in gathered]

    ncol = w_ada.shape[2]
    c_all = c_all.reshape(NDEV, D)
    b_cols = lax.dynamic_slice_in_dim(b_ada, me * ncol, ncol, axis=1)
    act_all, mod_cols = mod_fwd(c_all, w_ada[0], b_cols)
    (mod_all,) = all_gather("gather_mod", [mod_cols])
    mod = lax.dynamic_index_in_dim(mod_all, me, axis=1, keepdims=False).reshape(NMOD, D)
    sh1, sc1, g1, sh2, sc2, g2 = [mod[q:q + 1] for q in range(NMOD)]

    expand = jnp.repeat(jnp.eye(NG, dtype=F32), NP, axis=0)
    a_re_c, a_im_c = ssm_a_re.reshape(NST, 1), ssm_a_im.reshape(NST, 1)
    ldt_c = ssm_log_dt.reshape(NG, 1)
    b_re_r, b_im_r = ssm_b_re.reshape(NST, GH), ssm_b_im.reshape(NST, GH)
    e_re, e_im, bb_re, bb_im = disc_fwd(a_re_c, a_im_c, ldt_c, b_re_r, b_im_r, expand)
    e_re_b, e_im_b = e_re.reshape(NBLK, 1, SB), e_im.reshape(NBLK, 1, SB)
    bb_re_g, bb_im_g = bb_re.reshape(NG, NP, GH), bb_im.reshape(NG, NP, GH)
    wbt_re = _block_diag(bb_re_g, NP, GH)
    wbt_im = _block_diag(bb_im_g, NP, GH)
    wb_re, wb_im = wbt_re.transpose(0, 2, 1), wbt_im.transpose(0, 2, 1)
    wct = jnp.concatenate([_block_diag(ssm_c_re[0], GH, NP), -_block_diag(ssm_c_im[0], GH, NP)], axis=2)
    wc = wct.transpose(0, 2, 1)
    to_b = lambda a: a.astype(BF16)
    dvec = ssm_d.reshape(NBLK, 1, CB)

    n1g = norm1_g

    def f_norm1(xv, g, sc, sh):
        _, xh = _rms_stats(xv)
        return [xh * g * (1.0 + sc) + sh], []

    (h1,) = rowwise("norm1", f_norm1, [xs], [n1g, sc1, sh1], [(D, BF16)], [], 512)
    z = mm("mm_in", h1, w_in_t, "nt", tiles=(1024, CW, 1024), b_rot=Z_ROT)

    conv_w_full = cw_g.transpose(1, 0, 2).reshape(KC, CW)
    w32 = jnp.pad(conv_w_full, ((0, HALO - KC), (0, 0)))
    yc, s_act = conv_fwd(z, w32, conv_b, conv_ln_g, conv_ln_b)
    y_conv = mm("mm_conv_proj", s_act, conv_proj_t, "nt")

    xs_re, xs_im, ypre, gl = ssm_fwd(z, to_b(wb_re), to_b(wb_im), to_b(wc), e_re_b, e_im_b, dvec)
    z2 = mm("mm_ssm_glu", gl, ssm_glu_t, "nt")

    def f_merge(yc_v, za, zb, glc, gls):
        return [_sig(glc) * yc_v + _sig(gls) * (za * _sig(zb))], []

    (merged,) = rowwise("merge", f_merge, [y_conv, (z2, D, 0), (z2, D, 1), (z, D, 0), (z, D, 1)],
                        [], [(D, BF16)], [], 512)
    o1 = mm("mm_out", merged, w_out_f, "nn")

    def f_norm2(xv, o1v, g1v, g, sc, sh):
        x1v = xv + g1v * o1v
        _, xh = _rms_stats(x1v)
        return [x1v, xh * g * (1.0 + sc) + sh], []

    x1, h2 = rowwise("norm2", f_norm2, [xs, o1], [g1, norm2_g, sc2, sh2], [(D, F32), (D, BF16)], [], 512)
    f = mm("mm_ffn_in", h2, w_ffn_in_t, "nt")

    def f_swiglu(fg, fu):
        return [fg * _sig(fg) * fu], []

    (act,) = rowwise("swiglu", f_swiglu, [(f, FH, 0), (f, FH, 1)], [], [(FH, BF16)], [], 256)
    o2 = mm("mm_ffn_out", act, w_ffn_out_f, "nn")

    fg_row = final_g.reshape(1, D)

    def f_final(x1v, o2v, tv, g2v, fg):
        x2v = x1v + g2v * o2v
        r, xh = _rms_stats(x2v)
        yv = xh * fg
        err = yv - tv
        loss = jnp.sum(_colsum(err * err), axis=1, keepdims=True) * (0.5 / D)
        dy = err * (1.0 / D)
        dx2 = _rms_bwd(dy * fg, xh, r)
        return ([dx2, g2v * dx2],
                [jnp.broadcast_to(loss, (1, LANE)), _colsum(dy * xh), _colsum(dx2 * o2v)])

    dx2, do2, loss_l, d_final_g, d_g2 = rowwise(
        "final", f_final, [x1, o2, tgt], [g2, fg_row], [(D, F32), (D, BF16)], [LANE, D, D], 256)

    dact = mm("mm_dact", do2, w_ffn_out_f, "nt")
    g_ffn_out = mm("mm_g_ffn_out", act, do2, "tn", BF16)

    def f_dswiglu(fg, fu, da):
        sg = _sig(fg)
        return [jnp.concatenate([da * fu * (sg * (1.0 + fg * (1.0 - sg))), da * (fg * sg)], axis=1)], []

    (df,) = rowwise("dswiglu", f_dswiglu, [(f, FH, 0), (f, FH, 1), dact], [], [(2 * FH, BF16)], [], 256)
    dh2 = mm("mm_dh2", df, w_ffn_in_t, "nn")
    g_ffn_in_t = mm("mm_g_ffn_in", df, h2, "tn", BF16)

    def f_dnorm2(dh, x1v, dx2v, o1v, g, sc, g1v):
        r, xh = _rms_stats(x1v)
        dxh = dh * (1.0 + sc) * g
        dx1 = dx2v + _rms_bwd(dxh, xh, r)
        return ([dx1, g1v * dx1],
                [_colsum(dh * xh * g), _colsum(dh), _colsum(dh * (1.0 + sc) * xh), _colsum(dx1 * o1v)])

    dx1, do1, d_sc2, d_sh2, d_n2g, d_g1 = rowwise(
        "dnorm2", f_dnorm2, [dh2, x1, dx2, o1], [norm2_g, sc2, g1], [(D, F32), (D, BF16)], [D, D, D, D], 256)

    dmerged = mm("mm_dmerged", do1, w_out_f, "nt")
    g_out = mm("mm_g_out", merged, do1, "tn", BF16)

    def f_dmerge(dm, yc_v, za, zb, glc, gls):
        sc_ = _sig(glc)
        ss_ = _sig(gls)
        sb_ = _sig(zb)
        dys = dm * ss_
        dz2 = jnp.concatenate([dys * sb_, dys * za * sb_ * (1.0 - sb_)], axis=1)
        dgl = jnp.concatenate([dm * yc_v * sc_ * (1.0 - sc_), dm * (za * sb_) * ss_ * (1.0 - ss_)], axis=1)
        return [dm * sc_, dz2, dgl], []

    dyconv, dz2, dz = rowwise(
        "dmerge", f_dmerge, [dmerged, y_conv, (z2, D, 0), (z2, D, 1), (z, D, 0), (z, D, 1)],
        [], [(D, BF16), (2 * D, BF16), (2 * D, BF16, ZW, 0)], [], 256)

    ds = mm("mm_ds", dyconv, conv_proj_t, "nn")
    g_conv_proj_t = mm("mm_g_conv_proj", dyconv, s_act, "tn", BF16)
    dz, d_lng, d_lnb, d_cb, d_cw32 = conv_bwd(ds, yc, z, w32, conv_ln_g, conv_ln_b, dz)

    dgl = mm("mm_dgl", dz2, ssm_glu_t, "nn")
    g_ssm_glu_t = mm("mm_g_ssm_glu", dz2, gl, "tn", BF16)
    dz, d_d, d_ar, d_ai, d_wb_re, d_wb_im, d_wc = ssm_bwd(
        dgl, ypre, z, xs_re, xs_im, to_b(wbt_re), to_b(wbt_im), to_b(wct), e_re_b, e_im_b, dvec, dz)

    dh1 = mm("mm_dh1", dz, w_in_t, "nn", tiles=(1024, 1024, CW), b_rot=Z_ROT)
    g_in_t = mm("mm_g_in", dz, h1, "tn", BF16, tiles=(CW, 1024, 1024), o_rot=Z_ROT)

    def f_dnorm1(dh, xv, dx1v, g, sc):
        r, xh = _rms_stats(xv)
        dxh = dh * (1.0 + sc) * g
        return ([dx1v + _rms_bwd(dxh, xh, r)],
                [_colsum(dh * xh * g), _colsum(dh), _colsum(dh * (1.0 + sc) * xh)])

    grad_x, d_sc1, d_sh1, d_n1g = rowwise(
        "dnorm1", f_dnorm1, [dh1, xs, dx1], [n1g, sc1], [(D, F32)], [D, D, D], 256)

    d_bb_re = _diag_blocks(d_wb_re.transpose(0, 2, 1), NP, GH).reshape(NST, GH)
    d_bb_im = _diag_blocks(d_wb_im.transpose(0, 2, 1), NP, GH).reshape(NST, GH)
    d_wct = d_wc.transpose(0, 2, 1)
    d_c_re = _diag_blocks(d_wct[:, :, 0:SB], GH, NP)
    d_c_im = -_diag_blocks(d_wct[:, :, SB:2 * SB], GH, NP)
    d_a_re, d_a_im, d_ldt, d_b_re, d_b_im = disc_bwd(
        a_re_c, a_im_c, ldt_c, b_re_r, b_im_r, expand, d_ar.reshape(NST, 1), d_ai.reshape(NST, 1), d_bb_re, d_bb_im)

    dmod = jnp.concatenate([d_sh1, d_sc1, d_g1, d_sh2, d_sc2, d_g2], axis=1)
    small_local = [dmod.reshape(-1), d_n1g.reshape(-1), d_cw32[0:KC].reshape(-1), d_cb.reshape(-1), d_lng.reshape(-1),
                   d_lnb.reshape(-1), d_a_re.reshape(-1), d_a_im.reshape(-1), d_b_re.reshape(-1), d_b_im.reshape(-1),
                   d_c_re.reshape(-1), d_c_im.reshape(-1), d_d.reshape(-1), d_ldt.reshape(-1), d_n2g.reshape(-1),
                   d_final_g.reshape(-1), loss_l[0, 0:1]]
    small_sizes = [v.shape[0] for v in small_local]
    packed = _pad_rows(jnp.concatenate(small_local), 256 * LANE).reshape(-1, LANE)
    (small_all,) = all_gather("gather_small", [packed])
    small_sum = sum_slots("sum_small", small_all).reshape(-1)
    pieces, pos = [], 0
    for n in small_sizes:
        pieces.append(small_sum[pos:pos + n])
        pos += n
    (g_b_ada, g_n1g, g_cw_full, g_cb, g_lng, g_lnb, g_a_re, g_a_im, g_b_re, g_b_im, g_c_re, g_c_im, g_d, g_ldt,
     g_n2g, g_fg, loss_sum) = pieces
    loss = loss_sum[0]
    dmod_all = small_all.reshape(NDEV, -1)[:, 0:NMOD * D]
    g_w_ada = ada_grad(act_all, lax.dynamic_slice_in_dim(dmod_all, me * ncol, ncol, axis=1))
    ccol = conv_w.shape[2]
    g_conv_w = lax.dynamic_slice_in_dim(g_cw_full.reshape(KC, CW), me * ccol, ccol, axis=1)

    big_t = [g_in_t, g_conv_proj_t, g_ssm_glu_t, g_out, g_ffn_in_t, g_ffn_out]
    names = ("w_in", "conv_proj", "ssm_glu", "w_out", "w_ffn_in", "w_ffn_out")
    by_dev = [g.reshape(NDEV, -1, D) for g in big_t]
    from_sibling = pair_exchange("grads_pair", by_dev)
    partials = [pair_sum("pair_sum_" + n, g, r) for n, g, r in zip(names, by_dev, from_sibling)]
    from_chips = chip_exchange("grads_chips", partials)
    gsums = [chip_sum("chip_sum_" + n, p, r) for n, p, r in zip(names, partials, from_chips)]

    def mine(q, cols):
        return gsums[q].reshape(-1, cols)

    g_w_in = mine(0, D).T
    g_conv_proj = mine(1, CW).T
    g_ssm_glu = mine(2, SW).T
    g_w_out = mine(3, D)
    g_w_ffn_in = mine(4, D).T
    g_w_ffn_out = mine(5, D)

    grads = {
        "w_ada": g_w_ada[None], "b_ada": g_b_ada.reshape(b_ada.shape), "norm1_g": g_n1g.reshape(norm1_g.shape),
        "w_in": g_w_in[None], "conv_w": g_conv_w[None], "conv_b": g_cb.reshape(conv_b.shape),
        "conv_ln_g": g_lng.reshape(conv_ln_g.shape), "conv_ln_b": g_lnb.reshape(conv_ln_b.shape),
        "conv_proj": g_conv_proj[None], "ssm_a_re": g_a_re.reshape(ssm_a_re.shape),
        "ssm_a_im": g_a_im.reshape(ssm_a_im.shape), "ssm_b_re": g_b_re.reshape(ssm_b_re.shape),
        "ssm_b_im": g_b_im.reshape(ssm_b_im.shape), "ssm_c_re": g_c_re.reshape(ssm_c_re.shape),
        "ssm_c_im": g_c_im.reshape(ssm_c_im.shape), "ssm_d": g_d.reshape(ssm_d.shape),
        "ssm_log_dt": g_ldt.reshape(ssm_log_dt.shape), "ssm_glu": g_ssm_glu[None], "w_out": g_w_out[None],
        "norm2_g": g_n2g.reshape(norm2_g.shape), "w_ffn_in": g_w_ffn_in[None], "w_ffn_out": g_w_ffn_out[None],
        "final_g": g_fg.reshape(final_g.shape),
    }
    weights = {
        "w_ada": (w_ada, m_w_ada, v_w_ada), "b_ada": (b_ada, m_b_ada, v_b_ada), "norm1_g": (norm1_g, m_norm1_g, v_norm1_g),
        "w_in": (w_in, m_w_in, v_w_in), "conv_w": (conv_w, m_conv_w, v_conv_w), "conv_b": (conv_b, m_conv_b, v_conv_b),
        "conv_ln_g": (conv_ln_g, m_conv_ln_g, v_conv_ln_g), "conv_ln_b": (conv_ln_b, m_conv_ln_b, v_conv_ln_b),
        "conv_proj": (conv_proj, m_conv_proj, v_conv_proj), "ssm_a_re": (ssm_a_re, m_ssm_a_re, v_ssm_a_re),
        "ssm_a_im": (ssm_a_im, m_ssm_a_im, v_ssm_a_im), "ssm_b_re": (ssm_b_re, m_ssm_b_re, v_ssm_b_re),
        "ssm_b_im": (ssm_b_im, m_ssm_b_im, v_ssm_b_im), "ssm_c_re": (ssm_c_re, m_ssm_c_re, v_ssm_c_re),
        "ssm_c_im": (ssm_c_im, m_ssm_c_im, v_ssm_c_im), "ssm_d": (ssm_d, m_ssm_d, v_ssm_d),
        "ssm_log_dt": (ssm_log_dt, m_ssm_log_dt, v_ssm_log_dt), "ssm_glu": (ssm_glu, m_ssm_glu, v_ssm_glu),
        "w_out": (w_out, m_w_out, v_w_out), "norm2_g": (norm2_g, m_norm2_g, v_norm2_g),
        "w_ffn_in": (w_ffn_in, m_w_ffn_in, v_w_ffn_in), "w_ffn_out": (w_ffn_out, m_w_ffn_out, v_w_ffn_out),
        "final_g": (final_g, m_final_g, v_final_g),
    }
    order = list(weights)
    big = ("w_ada", "w_in", "conv_proj", "ssm_glu", "w_out", "w_ffn_in", "w_ffn_out")
    delta, new_m, new_v = {}, {}, {}
    for n in big:
        wv, mv, vv = weights[n]
        shp = wv.shape
        d_, m_, v_ = adam("adam_" + n, wv.reshape(shp[-2:]), grads[n].reshape(shp[-2:]), mv.reshape(shp[-2:]),
                          vv.reshape(shp[-2:]))
        delta[n], new_m[n], new_v[n] = d_.reshape(shp), m_.reshape(shp), v_.reshape(shp)
    small = [n for n in order if n not in big]
    sizes = [weights[n][0].size for n in small]

    def pack(idx_or_grad):
        vs = [(grads[n] if idx_or_grad is None else weights[n][idx_or_grad]).reshape(-1) for n in small]
        return _pad_rows(jnp.concatenate(vs), 256 * LANE).reshape(-1, LANE)

    d_p, m_p, v_p = adam("adam_small", pack(0), pack(None), pack(1), pack(2))
    d_p, m_p, v_p = d_p.reshape(-1), m_p.reshape(-1), v_p.reshape(-1)
    pos = 0
    for n, sz in zip(small, sizes):
        shp = weights[n][0].shape
        delta[n] = d_p[pos:pos + sz].reshape(shp)
        new_m[n] = m_p[pos:pos + sz].reshape(shp)
        new_v[n] = v_p[pos:pos + sz].reshape(shp)
        pos += sz

    return (loss, grad_x[None], *[grads[n] for n in order], *[delta[n] for n in order],
            *[new_m[n] for n in order], *[new_v[n] for n in order])
```

```python
import functools
import math

import jax
import jax.numpy as jnp
from jax import lax
from jax.experimental import pallas as pl
from jax.experimental.pallas import tpu as pltpu

F32 = jnp.float32
BF16 = jnp.bfloat16

D = 1024
CW = 512
KC = 31
SW = 512
NG = 32
GH = 16
NP = 64
NST = NG * NP
FH = 2816
NMOD = 6
NDEV = 8
EPS = 1e-6
CB = 128
SB = 512
NBLK = SW // CB
HALO = 32
ZW = 2 * CW + SW + 2 * D
Z_ROT = (ZW // CW, 3)
ZB_A, ZB_G, ZB_U = 4, 5, 6

ADAM_LR = 0.001
ADAM_B1 = 0.9
ADAM_B2 = 0.999
ADAM_EPS = 1e-08
ADAM_WD = 0.01
ADAM_STEP = 10

V7X_VMEM_BYTES = 64 * 1024 * 1024
VMEM_LIMIT = V7X_VMEM_BYTES - 8 * 1024 * 1024
LANE = 128
MESH = pl.DeviceIdType.MESH
ANY_SPEC = pl.BlockSpec(memory_space=pl.ANY)


def _params(sem=None, **kw):
    if sem is not None:
        kw["dimension_semantics"] = sem
    return pltpu.CompilerParams(vmem_limit_bytes=VMEM_LIMIT, **kw)


def _tile(n, most):
    best = None
    for t in range(LANE, most + 1, LANE):
        if n % t == 0:
            best = t
    if best is None:
        raise ValueError(f"no tile for {n}")
    return best


def _sig(x):
    return jax.nn.sigmoid(x)


def mm(name, a, b, mode, out_dtype=F32, tiles=None, b_rot=None, o_rot=None, deps=()):
    if mode == "nn":
        (m, k), (k2, n) = a.shape, b.shape
    elif mode == "nt":
        (m, k), (n, k2) = a.shape, b.shape
    else:
        (k, m), (k2, n) = a.shape, b.shape
    assert k == k2, (name, a.shape, b.shape)
    bm, bn, bk = tiles or (_tile(m, 1024), _tile(n, 1408), _tile(k, 1408 if k % 1408 == 0 else 1024))
    bm, bn, bk = min(bm, m), min(bn, n), min(bk, k)
    assert m % bm == 0 and n % bn == 0 and k % bk == 0, (name, m, n, k, bm, bn, bk)
    nk = k // bk
    rot = lambda idx, r: idx if r is None else (idx + r[1]) % r[0]
    if mode == "nn":
        a_spec = pl.BlockSpec((bm, bk), lambda i, j, kk: (i, kk))
        b_spec = pl.BlockSpec((bk, bn), lambda i, j, kk: (rot(kk, b_rot), j))
        dims = (((1,), (0,)), ((), ()))
    elif mode == "nt":
        a_spec = pl.BlockSpec((bm, bk), lambda i, j, kk: (i, kk))
        b_spec = pl.BlockSpec((bn, bk), lambda i, j, kk: (rot(j, b_rot), kk))
        dims = (((1,), (1,)), ((), ()))
    else:
        assert b_rot is None
        a_spec = pl.BlockSpec((bk, bm), lambda i, j, kk: (kk, i))
        b_spec = pl.BlockSpec((bk, bn), lambda i, j, kk: (kk, j))
        dims = (((0,), (0,)), ((), ()))

    def body(a_ref, b_ref, *rest):
        o_ref, acc_ref = rest[-2:]
        kk = pl.program_id(2)

        @pl.when(kk == 0)
        def _():
            acc_ref[...] = jnp.zeros_like(acc_ref)

        acc_ref[...] += lax.dot_general(a_ref[...], b_ref[...], dims, preferred_element_type=F32)

        @pl.when(kk == nk - 1)
        def _():
            o_ref[...] = acc_ref[...].astype(o_ref.dtype)

    return pl.pallas_call(
        body, name=name,
        grid=(m // bm, n // bn, nk),
        in_specs=[a_spec, b_spec] + [ANY_SPEC] * len(deps),
        out_specs=pl.BlockSpec((bm, bn), lambda i, j, kk: (rot(i, o_rot), j)),
        out_shape=jax.ShapeDtypeStruct((m, n), out_dtype),
        scratch_shapes=[pltpu.VMEM((bm, bn), F32)],
        compiler_params=_params(("parallel", "parallel", "arbitrary")),
    )(a, b, *deps)


def rowwise(name, fn, rows, consts, out_rows, out_sums, ts, alias=None, deps=()):
    rows = [r if isinstance(r, tuple) else (r, r.shape[1], 0) for r in rows]
    out_rows = [o if len(o) == 4 else (o[0], o[1], o[0], 0) for o in out_rows]
    s = rows[0][0].shape[0]
    nt = s // ts
    nr, nc, no, ns = len(rows), len(consts), len(out_rows), len(out_sums)
    in_specs = [pl.BlockSpec((ts, w), functools.partial(lambda i, cb: (i, cb), cb=cb)) for (_, w, cb) in rows]
    in_specs += [pl.BlockSpec(c.shape, lambda i: (0, 0)) for c in consts]
    operands = [r[0] for r in rows] + list(consts)
    aliases = {}
    if alias is not None:
        in_specs.append(pl.BlockSpec(memory_space=pl.ANY))
        operands.append(alias[0])
        aliases = {nr + nc: alias[1]}
    in_specs += [ANY_SPEC] * len(deps)
    operands += list(deps)
    out_shape = [jax.ShapeDtypeStruct((s, tw), dt) for (_, dt, tw, _) in out_rows]
    out_shape += [jax.ShapeDtypeStruct((1, w), F32) for w in out_sums]
    out_specs = [pl.BlockSpec((ts, w), functools.partial(lambda i, cb: (i, cb), cb=cb)) for (w, _, _, cb) in out_rows]
    out_specs += [pl.BlockSpec((1, w), lambda i: (0, 0)) for w in out_sums]
    n_in = len(operands)

    def body(*refs):
        ins, outs = refs[:nr + nc], refs[n_in:]
        i = pl.program_id(0)
        ro, so = fn(*[r[...] for r in ins])
        for q in range(no):
            outs[q][...] = ro[q].astype(outs[q].dtype)
        if ns:
            @pl.when(i == 0)
            def _():
                for q in range(ns):
                    outs[no + q][...] = jnp.zeros_like(outs[no + q])

            for q in range(ns):
                outs[no + q][...] += so[q]

    return pl.pallas_call(
        body, name=name, grid=(nt,),
        in_specs=in_specs, out_specs=out_specs, out_shape=out_shape, input_output_aliases=aliases,
        compiler_params=_params(("arbitrary",) if ns else ("parallel",)),
    )(*operands)


def _colsum(v):
    return jnp.sum(v, axis=0, keepdims=True)


def _rms_stats(xv):
    r = lax.rsqrt(jnp.mean(xv * xv, axis=-1, keepdims=True) + EPS)
    return r, xv * r


def _rms_bwd(dxhat, xhat, r):
    return r * (dxhat - xhat * jnp.mean(dxhat * xhat, axis=-1, keepdims=True))


def _gelu(v):
    k = math.sqrt(2.0 / math.pi)
    t = jnp.tanh(k * (v + 0.044715 * v * v * v))
    return 0.5 * v * (1.0 + t), t


def _gelu_grad(v, t):
    k = math.sqrt(2.0 / math.pi)
    return 0.5 * (1.0 + t) + 0.5 * v * (1.0 - t * t) * k * (1.0 + 3.0 * 0.044715 * v * v)


CONV_TS = 256
CONV_CH = 64


def _ln_fwd(yc, g, b):
    mu = jnp.mean(yc, axis=-1, keepdims=True)
    xc = yc - mu
    rstd = lax.rsqrt(jnp.mean(xc * xc, axis=-1, keepdims=True) + EPS)
    nhat = xc * rstd
    return nhat, rstd, nhat * g + b


def conv_fwd(z, w32, cb, lg, lb):
    s = z.shape[0]
    ts = CONV_TS
    nt = s // ts
    hb = ts // HALO

    def body(a_ref, g_ref, ah_ref, gh_ref, w_ref, cb_ref, lg_ref, lb_ref, yc_ref, s_ref, ubuf):
        i = pl.program_id(0)
        first = (i > 0).astype(F32)
        ubuf[0:HALO, :] = ah_ref[...] * _sig(gh_ref[...]) * first
        ubuf[HALO:HALO + ts, :] = a_ref[...] * _sig(g_ref[...])
        for c0 in range(0, ts, CONV_CH):
            acc = jnp.zeros((CONV_CH, CW), F32)
            for k in range(KC):
                acc = acc + w_ref[k:k + 1, :] * ubuf[pl.ds(c0 + k + 2, CONV_CH), :]
            yc = acc + cb_ref[...]
            yc_ref[c0:c0 + CONV_CH, :] = yc
            _, _, ln = _ln_fwd(yc, lg_ref[...], lb_ref[...])
            s_ref[c0:c0 + CONV_CH, :] = (ln * _sig(ln)).astype(s_ref.dtype)

    cur = lambda cbk: pl.BlockSpec((ts, CW), functools.partial(lambda i, q: (i, q), q=cbk))
    prev = lambda cbk: pl.BlockSpec((HALO, CW), functools.partial(lambda i, q: (jnp.maximum(i * hb - 1, 0), q), q=cbk))
    const = lambda a: pl.BlockSpec(a.shape, lambda i: (0, 0))
    return pl.pallas_call(
        body, name="conv_fwd", grid=(nt,),
        in_specs=[cur(ZB_A), cur(ZB_G), prev(ZB_A), prev(ZB_G), const(w32), const(cb), const(lg), const(lb)],
        out_specs=[pl.BlockSpec((ts, CW), lambda i: (i, 0)), pl.BlockSpec((ts, CW), lambda i: (i, 0))],
        out_shape=[jax.ShapeDtypeStruct((s, CW), F32), jax.ShapeDtypeStruct((s, CW), BF16)],
        scratch_shapes=[pltpu.VMEM((HALO + ts, CW), F32)],
        compiler_params=_params(("parallel",)),
    )(z, z, z, z, w32, cb, lg, lb)


def conv_bwd(ds, yc, z, w32, lg, lb, dz):
    s = z.shape[0]
    ts = CONV_TS
    nt = s // ts
    hb = ts // HALO
    last_hb = s // HALO - 1

    def ln_bwd(dsv, ycv, g, b):
        nhat, rstd, ln = _ln_fwd(ycv, g, b)
        sg = _sig(ln)
        dln = dsv * (sg * (1.0 + ln * (1.0 - sg)))
        dnh = dln * g
        dyc = rstd * (dnh - jnp.mean(dnh, axis=-1, keepdims=True)
                      - nhat * jnp.mean(dnh * nhat, axis=-1, keepdims=True))
        return dyc, dln, nhat

    def body(ds_ref, yc_ref, dsn_ref, ycn_ref, a_ref, g_ref, ah_ref, gh_ref, w_ref, lg_ref, lb_ref, dz_in,
             dz_ref, dlg_ref, dlb_ref, dcb_ref, dw_ref, dbuf, ubuf):
        i = pl.program_id(0)

        @pl.when(i == 0)
        def _():
            dlg_ref[...] = jnp.zeros_like(dlg_ref)
            dlb_ref[...] = jnp.zeros_like(dlb_ref)
            dcb_ref[...] = jnp.zeros_like(dcb_ref)
            dw_ref[...] = jnp.zeros_like(dw_ref)

        lg, lb = lg_ref[...], lb_ref[...]
        dyc, dln, nhat = ln_bwd(ds_ref[...], yc_ref[...], lg, lb)
        dlg_ref[...] += _colsum(dln * nhat)
        dlb_ref[...] += _colsum(dln)
        dcb_ref[...] += _colsum(dyc)
        dbuf[0:ts, :] = dyc
        nxt = (i < nt - 1).astype(F32)
        dbuf[ts:ts + HALO, :] = ln_bwd(dsn_ref[...], ycn_ref[...], lg, lb)[0] * nxt
        first = (i > 0).astype(F32)
        ubuf[0:HALO, :] = ah_ref[...] * _sig(gh_ref[...]) * first
        ubuf[HALO:HALO + ts, :] = a_ref[...] * _sig(g_ref[...])
        for c0 in range(0, ts, CONV_CH):
            du = jnp.zeros((CONV_CH, CW), F32)
            dyc_c = dbuf[c0:c0 + CONV_CH, :]
            for k in range(KC):
                du = du + w_ref[k:k + 1, :] * dbuf[pl.ds(c0 + KC - 1 - k, CONV_CH), :]
                dw_ref[k:k + 1, :] += _colsum(dyc_c * ubuf[pl.ds(c0 + k + 2, CONV_CH), :])
            av = a_ref[c0:c0 + CONV_CH, :]
            sg = _sig(g_ref[c0:c0 + CONV_CH, :])
            dz_ref[c0:c0 + CONV_CH, 0:CW] = (du * sg).astype(dz_ref.dtype)
            dz_ref[c0:c0 + CONV_CH, CW:2 * CW] = (du * av * sg * (1.0 - sg)).astype(dz_ref.dtype)

    cur = lambda w, cbk: pl.BlockSpec((ts, w), functools.partial(lambda i, q: (i, q), q=cbk))
    prev = lambda cbk: pl.BlockSpec((HALO, CW), functools.partial(lambda i, q: (jnp.maximum(i * hb - 1, 0), q), q=cbk))
    nxt_spec = pl.BlockSpec((HALO, CW), lambda i: (jnp.minimum((i + 1) * hb, last_hb), 0))
    const = lambda a: pl.BlockSpec(a.shape, lambda i: (0, 0))
    acc = lambda r: pl.BlockSpec((r, CW), lambda i: (0, 0))
    return pl.pallas_call(
        body, name="conv_bwd", grid=(nt,),
        in_specs=[cur(CW, 0), cur(CW, 0), nxt_spec, nxt_spec, cur(CW, ZB_A), cur(CW, ZB_G), prev(ZB_A), prev(ZB_G),
                  const(w32), const(lg), const(lb), pl.BlockSpec(memory_space=pl.ANY)],
        out_specs=[pl.BlockSpec((ts, 2 * CW), lambda i: (i, ZB_A // 2)), acc(1), acc(1), acc(1), acc(HALO)],
        out_shape=[jax.ShapeDtypeStruct(dz.shape, dz.dtype), jax.ShapeDtypeStruct((1, CW), F32),
                   jax.ShapeDtypeStruct((1, CW), F32), jax.ShapeDtypeStruct((1, CW), F32),
                   jax.ShapeDtypeStruct((HALO, CW), F32)],
        scratch_shapes=[pltpu.VMEM((ts + HALO, CW), F32), pltpu.VMEM((HALO + ts, CW), F32)],
        input_output_aliases={11: 0},
        compiler_params=_params(("arbitrary",)),
    )(ds, yc, ds, yc, z, z, z, z, w32, lg, lb, dz)


SSM_TS = 512
GRP = 8


def _cmul(ar, ai, br, bi):
    return ar * br - ai * bi, ar * bi + ai * br


def _scan_tables(ar, ai, reverse):
    n = ar.shape[1]
    row = lax.broadcasted_iota(jnp.int32, (GRP, n), 0)
    dist = (GRP - 1 - row) if reverse else row
    one_r = jnp.broadcast_to(ar, (GRP, n))
    one_i = jnp.broadcast_to(ai, (GRP, n))
    p2r, p2i = _cmul(one_r, one_i, one_r, one_i)
    p4r, p4i = _cmul(p2r, p2i, p2r, p2i)
    steps = []
    for sft, (pr, pi) in ((1, (one_r, one_i)), (2, (p2r, p2i)), (4, (p4r, p4i))):
        keep = dist >= sft
        steps.append((jnp.where(keep, pr, 0.0), jnp.where(keep, pi, 0.0)))
    cr, ci = one_r, one_i
    accr, acci = one_r, one_i
    for e in range(1, GRP):
        cr, ci = _cmul(cr, ci, one_r, one_i)
        accr = jnp.where(dist == e, cr, accr)
        acci = jnp.where(dist == e, ci, acci)
    return steps, (accr, acci)


def _scan_group(xr, xi, steps, carry_tab, cr, ci, reverse):
    for sft, (tr, ti) in zip((1, 2, 4), steps):
        amt = (GRP - sft) if reverse else sft
        sr = pltpu.roll(xr, amt, 0)
        si = pltpu.roll(xi, amt, 0)
        xr, xi = xr + tr * sr - ti * si, xi + tr * si + ti * sr
    pr, pi = carry_tab
    xr = xr + pr * cr - pi * ci
    xi = xi + pr * ci + pi * cr
    return xr, xi


def ssm_fwd(z, wb_re, wb_im, wc, e_re, e_im, dvec):
    s = z.shape[0]
    ts = SSM_TS
    nt = s // ts
    ucol0 = ZB_U * CW // CB

    def body(u_ref, wbr_ref, wbi_ref, wc_ref, er_ref, ei_ref, d_ref, xr_ref, xi_ref, y_ref, gl_ref, car_r, car_i):
        i = pl.program_id(1)

        @pl.when(i == 0)
        def _():
            car_r[...] = jnp.zeros_like(car_r)
            car_i[...] = jnp.zeros_like(car_i)

        u = u_ref[...]
        ub = u.astype(BF16)
        xr_ref[...] = jnp.dot(ub, wbr_ref[0], preferred_element_type=F32)
        xi_ref[...] = jnp.dot(ub, wbi_ref[0], preferred_element_type=F32)
        steps, ctab = _scan_tables(er_ref[0], ei_ref[0], False)

        def grp(r, carry):
            cr, ci = carry
            r0 = pl.multiple_of(r * GRP, GRP)
            xr, xi = _scan_group(xr_ref[pl.ds(r0, GRP), :], xi_ref[pl.ds(r0, GRP), :], steps, ctab, cr, ci, False)
            xr_ref[pl.ds(r0, GRP), :] = xr
            xi_ref[pl.ds(r0, GRP), :] = xi
            return (jnp.broadcast_to(xr[GRP - 1:GRP, :], (GRP, SB)), jnp.broadcast_to(xi[GRP - 1:GRP, :], (GRP, SB)))

        cr, ci = lax.fori_loop(0, ts // GRP, grp, (car_r[...], car_i[...]))
        car_r[...] = cr
        car_i[...] = ci
        y = (jnp.dot(xr_ref[...].astype(BF16), wc_ref[0, 0:SB, :], preferred_element_type=F32)
             + jnp.dot(xi_ref[...].astype(BF16), wc_ref[0, SB:2 * SB, :], preferred_element_type=F32)
             + d_ref[0] * u)
        y_ref[...] = y
        gl_ref[...] = _gelu(y)[0].astype(gl_ref.dtype)

    blk3 = lambda a: pl.BlockSpec((1,) + a.shape[1:], lambda j, i: (j, 0, 0))
    return pl.pallas_call(
        body, name="ssm_fwd", grid=(NBLK, nt),
        in_specs=[pl.BlockSpec((ts, CB), lambda j, i: (i, ucol0 + j)),
                  blk3(wb_re), blk3(wb_im), blk3(wc), blk3(e_re), blk3(e_im), blk3(dvec)],
        out_specs=[pl.BlockSpec((ts, SB), lambda j, i: (i, j)), pl.BlockSpec((ts, SB), lambda j, i: (i, j)),
                   pl.BlockSpec((ts, CB), lambda j, i: (i, j)), pl.BlockSpec((ts, CB), lambda j, i: (i, j))],
        out_shape=[jax.ShapeDtypeStruct((s, NST), F32), jax.ShapeDtypeStruct((s, NST), F32),
                   jax.ShapeDtypeStruct((s, SW), F32), jax.ShapeDtypeStruct((s, SW), BF16)],
        scratch_shapes=[pltpu.VMEM((GRP, SB), F32), pltpu.VMEM((GRP, SB), F32)],
        compiler_params=_params(("parallel", "arbitrary")),
    )(z, wb_re, wb_im, wc, e_re, e_im, dvec)


def ssm_bwd(dgl, ypre, z, xs_re, xs_im, wbt_re, wbt_im, wct, e_re, e_im, dvec, dz):
    s = z.shape[0]
    ts = SSM_TS
    nt = s // ts
    ucol0 = ZB_U * CW // CB
    tn_dims = (((0,), (0,)), ((), ()))

    def body(dgl_ref, y_ref, u_ref, xr_ref, xi_ref, wbtr_ref, wbti_ref, wct_ref, er_ref, ei_ref, d_ref, dz_in,
             du_ref, dd_ref, dar_ref, dai_ref, dwbr_ref, dwbi_ref, dwc_ref,
             lr_ref, li_ref, car_r, car_i, acc_r, acc_i):
        i = pl.program_id(1)

        @pl.when(i == 0)
        def _():
            for ref in (car_r, car_i, acc_r, acc_i, dd_ref, dwbr_ref, dwbi_ref, dwc_ref):
                ref[...] = jnp.zeros_like(ref)

        u = u_ref[...]
        y = y_ref[...]
        dy = dgl_ref[...] * _gelu_grad(y, _gelu(y)[1])
        dd_ref[0] += _colsum(dy * u)
        dyb = dy.astype(BF16)
        dxo = jnp.dot(dyb, wct_ref[0], preferred_element_type=F32)
        lr_ref[...] = dxo[:, 0:SB]
        li_ref[...] = dxo[:, SB:2 * SB]
        steps, ctab = _scan_tables(er_ref[0], -ei_ref[0], True)
        row = lax.broadcasted_iota(jnp.int32, (GRP, SB), 0)

        def grp(q, carry):
            cr, ci, ar, ai = carry
            r0 = pl.multiple_of((ts // GRP - 1 - q) * GRP, GRP)
            lr, li = _scan_group(lr_ref[pl.ds(r0, GRP), :], li_ref[pl.ds(r0, GRP), :], steps, ctab, cr, ci, True)
            lr_ref[pl.ds(r0, GRP), :] = lr
            li_ref[pl.ds(r0, GRP), :] = li
            nr = jnp.where(row == GRP - 1, cr, pltpu.roll(lr, GRP - 1, 0))
            ni = jnp.where(row == GRP - 1, ci, pltpu.roll(li, GRP - 1, 0))
            xr = xr_ref[pl.ds(r0, GRP), :]
            xi = xi_ref[pl.ds(r0, GRP), :]
            ar = ar + nr * xr + ni * xi
            ai = ai + ni * xr - nr * xi
            return (jnp.broadcast_to(lr[0:1, :], (GRP, SB)), jnp.broadcast_to(li[0:1, :], (GRP, SB)), ar, ai)

        cr, ci, ar, ai = lax.fori_loop(0, ts // GRP, grp, (car_r[...], car_i[...], acc_r[...], acc_i[...]))
        car_r[...] = cr
        car_i[...] = ci
        acc_r[...] = ar
        acc_i[...] = ai

        @pl.when(i == nt - 1)
        def _():
            dar_ref[0] = _colsum(ar)
            dai_ref[0] = _colsum(ai)

        lrb = lr_ref[...].astype(BF16)
        lib = li_ref[...].astype(BF16)
        du = (jnp.dot(lrb, wbtr_ref[0], preferred_element_type=F32)
              + jnp.dot(lib, wbti_ref[0], preferred_element_type=F32) + d_ref[0] * dy)
        du_ref[...] = du.astype(du_ref.dtype)
        ub = u.astype(BF16)
        dwbr_ref[0] += lax.dot_general(ub, lrb, tn_dims, preferred_element_type=F32)
        dwbi_ref[0] += lax.dot_general(ub, lib, tn_dims, preferred_element_type=F32)
        dwc_ref[0, 0:SB, :] += lax.dot_general(xr_ref[...].astype(BF16), dyb, tn_dims, preferred_element_type=F32)
        dwc_ref[0, SB:2 * SB, :] += lax.dot_general(xi_ref[...].astype(BF16), dyb, tn_dims, preferred_element_type=F32)

    rev = lambda i: nt - 1 - i
    blk3 = lambda a: pl.BlockSpec((1,) + a.shape[1:], lambda j, i: (j, 0, 0))
    acc3 = lambda r, c: pl.BlockSpec((1, r, c), lambda j, i: (j, 0, 0))
    return pl.pallas_call(
        body, name="ssm_bwd", grid=(NBLK, nt),
        in_specs=[pl.BlockSpec((ts, CB), lambda j, i: (rev(i), j)), pl.BlockSpec((ts, CB), lambda j, i: (rev(i), j)),
                  pl.BlockSpec((ts, CB), lambda j, i: (rev(i), ucol0 + j)),
                  pl.BlockSpec((ts, SB), lambda j, i: (rev(i), j)), pl.BlockSpec((ts, SB), lambda j, i: (rev(i), j)),
                  blk3(wbt_re), blk3(wbt_im), blk3(wct), blk3(e_re), blk3(e_im), blk3(dvec),
                  pl.BlockSpec(memory_space=pl.ANY)],
        out_specs=[pl.BlockSpec((ts, CB), lambda j, i: (rev(i), ucol0 + j)),
                   acc3(1, CB), acc3(1, SB), acc3(1, SB), acc3(CB, SB), acc3(CB, SB), acc3(2 * SB, CB)],
        out_shape=[jax.ShapeDtypeStruct(dz.shape, dz.dtype),
                   jax.ShapeDtypeStruct((NBLK, 1, CB), F32),
                   jax.ShapeDtypeStruct((NBLK, 1, SB), F32), jax.ShapeDtypeStruct((NBLK, 1, SB), F32),
                   jax.ShapeDtypeStruct((NBLK, CB, SB), F32), jax.ShapeDtypeStruct((NBLK, CB, SB), F32),
                   jax.ShapeDtypeStruct((NBLK, 2 * SB, CB), F32)],
        scratch_shapes=[pltpu.VMEM((ts, SB), F32), pltpu.VMEM((ts, SB), F32)] + [pltpu.VMEM((GRP, SB), F32)] * 4,
        input_output_aliases={11: 0},
        compiler_params=_params(("parallel", "arbitrary")),
    )(dgl, ypre, z, xs_re, xs_im, wbt_re, wbt_im, wct, e_re, e_im, dvec, dz)


def _disc(a_re, a_im, log_dt, b_re, b_im, expand):
    dt = jnp.dot(expand, jnp.exp(log_dt), preferred_element_type=F32, precision=lax.Precision.HIGHEST)
    mag = jnp.exp(dt * a_re)
    e_re, e_im = mag * jnp.cos(dt * a_im), mag * jnp.sin(dt * a_im)
    n_re, n_im = e_re - 1.0, e_im
    den = a_re * a_re + a_im * a_im
    q_re = (n_re * a_re + n_im * a_im) / den
    q_im = (n_im * a_re - n_re * a_im) / den
    return e_re, e_im, q_re * b_re - q_im * b_im, q_re * b_im + q_im * b_re


def _whole(a):
    return pl.BlockSpec(a.shape, functools.partial(lambda n: (0,) * n, n=a.ndim))


def disc_fwd(a_re, a_im, log_dt, b_re, b_im, expand):
    def body(ar, ai, ld, br, bi, ex, er_o, ei_o, bbr_o, bbi_o):
        er, ei, bbr, bbi = _disc(ar[...], ai[...], ld[...], br[...], bi[...], ex[...])
        er_o[...] = er
        ei_o[...] = ei
        bbr_o[...] = bbr
        bbi_o[...] = bbi

    ins = (a_re, a_im, log_dt, b_re, b_im, expand)
    outs = [jax.ShapeDtypeStruct(a_re.shape, F32)] * 2 + [jax.ShapeDtypeStruct(b_re.shape, F32)] * 2
    return pl.pallas_call(body, name="disc_fwd", in_specs=[_whole(a) for a in ins],
                          out_specs=[_whole(o) for o in outs], out_shape=outs, compiler_params=_params())(*ins)


def disc_bwd(a_re, a_im, log_dt, b_re, b_im, expand, de_re, de_im, dbb_re, dbb_im):
    def body(ar, ai, ld, br, bi, ex, der, dei, dbr, dbi, o_ar, o_ai, o_ld, o_br, o_bi):
        exv = ex[...]
        _, vjp = jax.vjp(lambda *p: _disc(*p, exv), ar[...], ai[...], ld[...], br[...], bi[...])
        g = vjp((der[...], dei[...], dbr[...], dbi[...]))
        for o, v in zip((o_ar, o_ai, o_ld, o_br, o_bi), g):
            o[...] = v

    ins = (a_re, a_im, log_dt, b_re, b_im, expand, de_re, de_im, dbb_re, dbb_im)
    outs = [jax.ShapeDtypeStruct(a.shape, F32) for a in (a_re, a_im, log_dt, b_re, b_im)]
    return pl.pallas_call(body, name="disc_bwd", in_specs=[_whole(a) for a in ins],
                          out_specs=[_whole(o) for o in outs], out_shape=outs, compiler_params=_params())(*ins)


def mod_fwd(c_all, w_ada, b_cols):
    def body(c_ref, w_ref, b_ref, act_ref, mod_ref):
        cv = c_ref[...]
        act = cv * _sig(cv)
        act_ref[...] = act
        mod_ref[...] = jnp.dot(act, w_ref[...], preferred_element_type=F32, precision=lax.Precision.HIGHEST) + b_ref[...]

    ins = (c_all, w_ada, b_cols)
    outs = [jax.ShapeDtypeStruct(c_all.shape, F32), jax.ShapeDtypeStruct((NDEV, w_ada.shape[1]), F32)]
    return pl.pallas_call(body, name="mod_fwd", in_specs=[_whole(a) for a in ins],
                          out_specs=[_whole(o) for o in outs], out_shape=outs, compiler_params=_params())(*ins)


def ada_grad(act_all, dmod_cols):
    def body(a_ref, d_ref, o_ref):
        o_ref[...] = lax.dot_general(a_ref[...], d_ref[...], (((0,), (0,)), ((), ())),
                                     preferred_element_type=F32, precision=lax.Precision.HIGHEST)

    out = jax.ShapeDtypeStruct((act_all.shape[1], dmod_cols.shape[1]), F32)
    return pl.pallas_call(body, name="ada_grad", in_specs=[_whole(act_all), _whole(dmod_cols)],
                          out_specs=_whole(out), out_shape=out, compiler_params=_params())(act_all, dmod_cols)


def _adam_math(w, g, m, v):
    m2 = ADAM_B1 * m + (1.0 - ADAM_B1) * g
    v2 = ADAM_B2 * v + (1.0 - ADAM_B2) * (g * g)
    m_hat = m2 / (1.0 - ADAM_B1 ** ADAM_STEP)
    v_hat = v2 / (1.0 - ADAM_B2 ** ADAM_STEP)
    delta = -ADAM_LR * (m_hat / (jnp.sqrt(v_hat) + ADAM_EPS) + ADAM_WD * w)
    return delta, m2, v2


def adam(name, w, g, m, v):
    r, c = w.shape
    tr = r
    for cand in (256, 128, 64, 32, 16, 8):
        if r % cand == 0 and r > cand:
            tr = cand
            break

    def body(w_ref, g_ref, m_ref, v_ref, d_o, m_o, v_o):
        d, m2, v2 = _adam_math(w_ref[...], g_ref[...], m_ref[...], v_ref[...])
        d_o[...] = d
        m_o[...] = m2
        v_o[...] = v2

    spec = pl.BlockSpec((tr, c), lambda i: (i, 0))
    out = jax.ShapeDtypeStruct((r, c), F32)
    return pl.pallas_call(body, name=name, grid=(r // tr,), in_specs=[spec] * 4, out_specs=[spec] * 3,
                          out_shape=[out] * 3, compiler_params=_params(("parallel",)))(w, g, m, v)


def _rows_tile(r, most):
    best = None
    for t in range(16, min(r, most) + 1, 16):
        if r % t == 0:
            best = t
    assert best is not None, r
    return best


def sum_slots(name, slots, out_dtype=F32):
    n, r, c = slots.shape
    tr = _rows_tile(r, max(16, (2 * 1024 * 1024) // (n * c)))

    def body(s_ref, o_ref):
        acc = s_ref[0].astype(F32)
        for q in range(1, n):
            acc = acc + s_ref[q].astype(F32)
        o_ref[...] = acc.astype(o_ref.dtype)

    return pl.pallas_call(body, name=name, grid=(r // tr,),
                          in_specs=[pl.BlockSpec((n, tr, c), lambda i: (0, i, 0))],
                          out_specs=pl.BlockSpec((tr, c), lambda i: (i, 0)),
                          out_shape=jax.ShapeDtypeStruct((r, c), out_dtype), compiler_params=_params(("parallel",)))(slots)


HBM_SPEC = pl.BlockSpec(memory_space=pltpu.HBM)


def _coords():
    return lax.axis_index("x"), lax.axis_index("y"), lax.axis_index("c")


def _linear(x, y, c):
    return 4 * x + 2 * y + c


def all_gather(name, shards):
    nq = len(shards)

    def body(*refs):
        xs, outs = refs[:nq], refs[nq:2 * nq]
        send_sems, recv_sems, local_sems = refs[2 * nq:2 * nq + 3]
        bufs = refs[2 * nq + 3:]
        x, y, cc = _coords()
        me, sibling = (x, y, cc), (x, y, 1 - cc)
        chips = [(1 - x, y), (x, 1 - y), (1 - x, 1 - y)]

        def slot(q, px, py, pc):
            return outs[q].at[_linear(px, py, pc)]

        def copy(q, k, block, to, src=None):
            return pltpu.make_async_remote_copy(
                src_ref=slot(q, *block) if src is None else src, dst_ref=slot(q, *block),
                send_sem=send_sems.at[7 * q + k], recv_sem=recv_sems.at[7 * q + k], device_id=to, device_id_type=MESH)

        loads = [pltpu.make_async_copy(xs[q], bufs[q], local_sems.at[q]) for q in range(nq)]
        for cp in loads:
            cp.start()
        for cp in loads:
            cp.wait()
        mine = [pltpu.make_async_copy(bufs[q], slot(q, *me), local_sems.at[q]) for q in range(nq)]
        first = []
        for q in range(nq):
            first.append(copy(q, 0, me, sibling, src=bufs[q]))
            first += [copy(q, 1 + j, me, (*chip, cc), src=bufs[q]) for j, chip in enumerate(chips)]
        for cp in mine + first:
            cp.start()
        passed = []
        for q in range(nq):
            for j, chip in enumerate(chips):
                copy(q, 1 + j, (*chip, cc), me).wait_recv()
                passed.append(copy(q, 4 + j, (*chip, cc), sibling))
                passed[-1].start()
        for q in range(nq):
            copy(q, 0, sibling, me).wait_recv()
            for j, chip in enumerate(chips):
                copy(q, 4 + j, (*chip, 1 - cc), me).wait_recv()
        for cp in first + passed:
            cp.wait_send()
        for cp in mine:
            cp.wait()

    return pl.pallas_call(
        body, name=name, in_specs=[HBM_SPEC] * nq, out_specs=[HBM_SPEC] * nq,
        out_shape=[jax.ShapeDtypeStruct((NDEV,) + s.shape, s.dtype) for s in shards],
        scratch_shapes=[pltpu.SemaphoreType.DMA((7 * nq,)), pltpu.SemaphoreType.DMA((7 * nq,)),
                        pltpu.SemaphoreType.DMA((nq,))] + [pltpu.VMEM(s.shape, s.dtype) for s in shards],
    )(*shards)


NCHIP = 4


SEM_SPEC = pl.BlockSpec(memory_space=pltpu.SEMAPHORE)
EFFECT = pltpu.SideEffectType.DATAFLOW_SIDE_EFFECTING


def _peer(x, y, cc, k):
    fx, fy, fc = (k >> 2) & 1, (k >> 1) & 1, k & 1
    return (x + fx - 2 * fx * x, y + fy - 2 * fy * y, cc + fc - 2 * fc * cc)


def gather_plan(srcs, lands, coords):
    x, y, cc = coords
    me = _linear(x, y, cc)
    return [(s, l.at[me], _peer(x, y, cc, k)) for s, l in zip(srcs, lands) for k in range(1, NDEV)]


def pair_plan(srcs, lands, coords):
    x, y, cc = coords
    return [(s.at[2 * chip + 1 - cc], l.at[chip], (x, y, 1 - cc)) for s, l in zip(srcs, lands) for chip in range(NCHIP)]


def chip_plan(srcs, lands, coords):
    x, y, cc = coords
    out = []
    for s, l in zip(srcs, lands):
        for k in range(1, NCHIP):
            px, py, _ = _peer(x, y, cc, 2 * k)
            out.append((s.at[2 * px + py], l.at[k - 1], (px, py, cc)))
    return out


def _remote(copy, i, send_sems, recv_sems):
    src, dst, dev = copy
    return pltpu.make_async_remote_copy(src_ref=src, dst_ref=dst, send_sem=send_sems.at[i], recv_sem=recv_sems.at[i],
                                        device_id=dev, device_id_type=MESH)


def exchange_start(name, plan, ncopy, srcs, land_shapes, deps=()):
    ns, nl, nd = len(srcs), len(land_shapes), len(deps)

    def body(*refs):
        s, l = refs[:ns], refs[ns:ns + nl]
        send_sems, recv_sems = refs[ns + nl + nd], refs[ns + nl + nd + 1]
        token = refs[-1]
        for i, cp in enumerate(plan(s, l, _coords())):
            _remote(cp, i, send_sems, recv_sems).start()
        token[...] = jnp.zeros_like(token)

    hbm = lambda a: pltpu.with_memory_space_constraint(a, pltpu.HBM)
    lands = [lax.empty(shp, dt) for shp, dt in land_shapes]
    thru = [pltpu.HBM(a.shape, a.dtype) for a in list(srcs) + lands]
    outs = pl.pallas_call(
        body, name=name,
        in_specs=[HBM_SPEC] * (ns + nl) + [ANY_SPEC] * nd,
        out_specs=(SEM_SPEC, SEM_SPEC, *[HBM_SPEC] * (ns + nl), pl.BlockSpec(memory_space=pltpu.VMEM)),
        out_shape=(pltpu.SemaphoreType.DMA((ncopy,)), pltpu.SemaphoreType.DMA((ncopy,)), *thru,
                   jax.ShapeDtypeStruct((8, LANE), F32)),
        input_output_aliases={i: 2 + i for i in range(ns + nl)},
        compiler_params=pltpu.CompilerParams(has_side_effects=EFFECT),
    )(*[hbm(a) for a in srcs], *[hbm(a) for a in lands], *deps)
    return outs[0], outs[1], list(outs[2:2 + ns]), list(outs[2 + ns:2 + ns + nl]), outs[-1]


def exchange_wait(name, plan, started, after, place_own=False):
    send_sems, recv_sems, srcs, lands, _ = started
    ns, nl = len(srcs), len(lands)

    def body(*refs):
        s, l = refs[:ns], refs[ns:ns + nl]
        send_sems, recv_sems = refs[ns + nl], refs[ns + nl + 1]
        l_out = refs[2 * ns + nl + 3:2 * ns + 2 * nl + 3]
        scratch = refs[2 * ns + 2 * nl + 3:]
        copies = [_remote(cp, i, send_sems, recv_sems) for i, cp in enumerate(plan(s, l, _coords()))]
        if place_own:
            me = _linear(*_coords())
            local_sems, bufs = scratch[0], scratch[1:]
            loads = [pltpu.make_async_copy(s[q], bufs[q], local_sems.at[q]) for q in range(ns)]
            for cp in loads:
                cp.start()
            for cp in loads:
                cp.wait()
            stores = [pltpu.make_async_copy(bufs[q], l_out[q].at[me], local_sems.at[q]) for q in range(ns)]
            for cp in stores:
                cp.start()
        for cp in copies:
            cp.wait_recv()
        for cp in copies:
            cp.wait_send()
        if place_own:
            for cp in stores:
                cp.wait()

    scratch_shapes = []
    if place_own:
        scratch_shapes = [pltpu.SemaphoreType.DMA((ns,))] + [pltpu.VMEM(a.shape, a.dtype) for a in srcs]
    outs = pl.pallas_call(
        body, name=name,
        in_specs=[HBM_SPEC] * (ns + nl) + [SEM_SPEC, SEM_SPEC, ANY_SPEC],
        out_specs=[HBM_SPEC] * (ns + nl),
        out_shape=[pltpu.HBM(a.shape, a.dtype) for a in srcs + lands],
        input_output_aliases={i: i for i in range(ns + nl)},
        scratch_shapes=scratch_shapes,
        compiler_params=pltpu.CompilerParams(has_side_effects=EFFECT),
    )(*srcs, *lands, send_sems, recv_sems, after)
    return list(outs[:ns]), list(outs[ns:])


def pair_sum(name, g, recv):
    _, r, c = g.shape
    tr = _rows_tile(r, 512)

    def body(g_ref, r_ref, o_ref):
        own = jnp.where(lax.axis_index("c") == 0, g_ref[0, 0], g_ref[0, 1])
        o_ref[0] = (own.astype(F32) + r_ref[0].astype(F32)).astype(o_ref.dtype)

    return pl.pallas_call(
        body, name=name, grid=(NCHIP, r // tr),
        in_specs=[pl.BlockSpec((1, 2, tr, c), lambda k, i: (k, 0, i, 0)), pl.BlockSpec((1, tr, c), lambda k, i: (k, i, 0))],
        out_specs=pl.BlockSpec((1, tr, c), lambda k, i: (k, i, 0)),
        out_shape=jax.ShapeDtypeStruct((NCHIP, r, c), g.dtype), compiler_params=_params(("parallel", "parallel")),
    )(g.reshape(NCHIP, 2, r, c), recv)


def chip_sum(name, partial, recv):
    _, r, c = partial.shape
    tr = _rows_tile(r, 512)

    def body(p_ref, r_ref, o_ref):
        chip = 2 * lax.axis_index("x") + lax.axis_index("y")
        own = p_ref[0]
        for k in range(1, NCHIP):
            own = jnp.where(chip == k, p_ref[k], own)
        acc = own.astype(F32)
        for k in range(NCHIP - 1):
            acc = acc + r_ref[k].astype(F32)
        o_ref[...] = acc

    return pl.pallas_call(
        body, name=name, grid=(r // tr,),
        in_specs=[pl.BlockSpec((NCHIP, tr, c), lambda i: (0, i, 0)), pl.BlockSpec((NCHIP - 1, tr, c), lambda i: (0, i, 0))],
        out_specs=pl.BlockSpec((tr, c), lambda i: (i, 0)),
        out_shape=jax.ShapeDtypeStruct((r, c), F32), compiler_params=_params(("parallel",)),
    )(partial, recv)


def _block_diag(w, rows_per, cols_per):
    w = w.reshape(NBLK, 8, rows_per, cols_per)
    eye = jnp.eye(8, dtype=w.dtype)
    out = w[:, :, :, None, :] * eye[None, :, None, :, None]
    return out.reshape(NBLK, 8 * rows_per, 8 * cols_per)


def _diag_blocks(wd, rows_per, cols_per):
    wd = wd.reshape(NBLK, 8, rows_per, 8, cols_per)
    idx = jnp.arange(8)
    return wd[:, idx, :, idx, :].transpose(1, 0, 2, 3).reshape(NG, rows_per, cols_per)


def _pad_rows(v, mult):
    n = v.shape[0]
    return jnp.pad(v, (0, (-n) % mult))


def kernel(x, c, w_ada, b_ada, norm1_g, w_in, conv_w, conv_b, conv_ln_g, conv_ln_b, conv_proj, ssm_a_re, ssm_a_im, ssm_b_re, ssm_b_im, ssm_c_re, ssm_c_im, ssm_d, ssm_log_dt, ssm_glu, w_out, norm2_g, w_ffn_in, w_ffn_out, final_g, loss_target, m_w_ada, m_b_ada, m_norm1_g, m_w_in, m_conv_w, m_conv_b, m_conv_ln_g, m_conv_ln_b, m_conv_proj, m_ssm_a_re, m_ssm_a_im, m_ssm_b_re, m_ssm_b_im, m_ssm_c_re, m_ssm_c_im, m_ssm_d, m_ssm_log_dt, m_ssm_glu, m_w_out, m_norm2_g, m_w_ffn_in, m_w_ffn_out, m_final_g, v_w_ada, v_b_ada, v_norm1_g, v_w_in, v_conv_w, v_conv_b, v_conv_ln_g, v_conv_ln_b, v_conv_proj, v_ssm_a_re, v_ssm_a_im, v_ssm_b_re, v_ssm_b_im, v_ssm_c_re, v_ssm_c_im, v_ssm_d, v_ssm_log_dt, v_ssm_glu, v_w_out, v_norm2_g, v_w_ffn_in, v_w_ffn_out, v_final_g):
    me = _linear(*_coords())
    xs = x[0]
    tgt = loss_target[0]
    seq = xs.shape[0]

    flat = lambda g: g.reshape(NDEV * g.shape[1], g.shape[2])
    c_all, cw_g, w_in_g = all_gather("gather_first", [c, conv_w[0], w_in[0].T.astype(BF16)])
    w_in_t = flat(w_in_g)
    mids = [p.astype(BF16) for p in (conv_proj[0].T, ssm_glu[0].T, w_out[0])]
    ffns = [p.astype(BF16) for p in (w_ffn_in[0].T, w_ffn_out[0])]
    zone = lambda p: ((NDEV,) + p.shape, p.dtype)
    mids_go = exchange_start("gather_mid_start", gather_plan, 7 * len(mids), mids, [zone(p) for p in mids],
                             deps=[c_all])
    ffns_go = exchange_start("gather_ffn_start", gather_plan, 7 * len(ffns), ffns, [zone(p) for p in ffns],
                             deps=[mids_go[4]])

    ncol = w_ada.shape[2]
    c_all = c_all.reshape(NDEV, D)
    b_cols = lax.dynamic_slice_in_dim(b_ada, me * ncol, ncol, axis=1)
    act_all, mod_cols = mod_fwd(c_all, w_ada[0], b_cols)
    (mod_all,) = all_gather("gather_mod", [mod_cols])
    mod = lax.dynamic_index_in_dim(mod_all, me, axis=1, keepdims=False).reshape(NMOD, D)
    sh1, sc1, g1, sh2, sc2, g2 = [mod[q:q + 1] for q in range(NMOD)]

    expand = jnp.repeat(jnp.eye(NG, dtype=F32), NP, axis=0)
    a_re_c, a_im_c = ssm_a_re.reshape(NST, 1), ssm_a_im.reshape(NST, 1)
    ldt_c = ssm_log_dt.reshape(NG, 1)
    b_re_r, b_im_r = ssm_b_re.reshape(NST, GH), ssm_b_im.reshape(NST, GH)
    e_re, e_im, bb_re, bb_im = disc_fwd(a_re_c, a_im_c, ldt_c, b_re_r, b_im_r, expand)
    e_re_b, e_im_b = e_re.reshape(NBLK, 1, SB), e_im.reshape(NBLK, 1, SB)
    bb_re_g, bb_im_g = bb_re.reshape(NG, NP, GH), bb_im.reshape(NG, NP, GH)
    wbt_re = _block_diag(bb_re_g, NP, GH)
    wbt_im = _block_diag(bb_im_g, NP, GH)
    wb_re, wb_im = wbt_re.transpose(0, 2, 1), wbt_im.transpose(0, 2, 1)
    wct = jnp.concatenate([_block_diag(ssm_c_re[0], GH, NP), -_block_diag(ssm_c_im[0], GH, NP)], axis=2)
    wc = wct.transpose(0, 2, 1)
    to_b = lambda a: a.astype(BF16)
    dvec = ssm_d.reshape(NBLK, 1, CB)

    n1g = norm1_g

    def f_norm1(xv, g, sc, sh):
        _, xh = _rms_stats(xv)
        return [xh * g * (1.0 + sc) + sh], []

    (h1,) = rowwise("norm1", f_norm1, [xs], [n1g, sc1, sh1], [(D, BF16)], [], 512, deps=[ffns_go[4]])
    z = mm("mm_in", h1, w_in_t, "nt", tiles=(1024, CW, 1024), b_rot=Z_ROT)

    conv_w_full = cw_g.transpose(1, 0, 2).reshape(KC, CW)
    w32 = jnp.pad(conv_w_full, ((0, HALO - KC), (0, 0)))
    yc, s_act = conv_fwd(z, w32, conv_b, conv_ln_g, conv_ln_b)
    conv_proj_t, ssm_glu_t, w_out_f = [
        flat(g) for g in exchange_wait("gather_mid_wait", gather_plan, mids_go, s_act, place_own=True)[1]]
    y_conv = mm("mm_conv_proj", s_act, conv_proj_t, "nt")

    xs_re, xs_im, ypre, gl = ssm_fwd(z, to_b(wb_re), to_b(wb_im), to_b(wc), e_re_b, e_im_b, dvec)
    z2 = mm("mm_ssm_glu", gl, ssm_glu_t, "nt")

    def f_merge(yc_v, za, zb, glc, gls):
        return [_sig(glc) * yc_v + _sig(gls) * (za * _sig(zb))], []

    (merged,) = rowwise("merge", f_merge, [y_conv, (z2, D, 0), (z2, D, 1), (z, D, 0), (z, D, 1)],
                        [], [(D, BF16)], [], 512)
    o1 = mm("mm_out", merged, w_out_f, "nn")

    def f_norm2(xv, o1v, g1v, g, sc, sh):
        x1v = xv + g1v * o1v
        _, xh = _rms_stats(x1v)
        return [x1v, xh * g * (1.0 + sc) + sh], []

    x1, h2 = rowwise("norm2", f_norm2, [xs, o1], [g1, norm2_g, sc2, sh2], [(D, F32), (D, BF16)], [], 512)
    w_ffn_in_t, w_ffn_out_f = [
        flat(g) for g in exchange_wait("gather_ffn_wait", gather_plan, ffns_go, h2, place_own=True)[1]]
    f = mm("mm_ffn_in", h2, w_ffn_in_t, "nt")

    def f_swiglu(fg, fu):
        return [fg * _sig(fg) * fu], []

    (act,) = rowwise("swiglu", f_swiglu, [(f, FH, 0), (f, FH, 1)], [], [(FH, BF16)], [], 256)
    o2 = mm("mm_ffn_out", act, w_ffn_out_f, "nn")

    fg_row = final_g.reshape(1, D)

    def f_final(x1v, o2v, tv, g2v, fg):
        x2v = x1v + g2v * o2v
        r, xh = _rms_stats(x2v)
        yv = xh * fg
        err = yv - tv
        loss = jnp.sum(_colsum(err * err), axis=1, keepdims=True) * (0.5 / D)
        dy = err * (1.0 / D)
        dx2 = _rms_bwd(dy * fg, xh, r)
        return ([dx2, g2v * dx2],
                [jnp.broadcast_to(loss, (1, LANE)), _colsum(dy * xh), _colsum(dx2 * o2v)])

    dx2, do2, loss_l, d_final_g, d_g2 = rowwise(
        "final", f_final, [x1, o2, tgt], [g2, fg_row], [(D, F32), (D, BF16)], [LANE, D, D], 256)

    dact = mm("mm_dact", do2, w_ffn_out_f, "nt")
    g_ffn_out = mm("mm_g_ffn_out", act, do2, "tn", BF16)

    def f_dswiglu(fg, fu, da):
        sg = _sig(fg)
        return [jnp.concatenate([da * fu * (sg * (1.0 + fg * (1.0 - sg))), da * (fg * sg)], axis=1)], []

    (df,) = rowwise("dswiglu", f_dswiglu, [(f, FH, 0), (f, FH, 1), dact], [], [(2 * FH, BF16)], [], 256)
    dh2 = mm("mm_dh2", df, w_ffn_in_t, "nn")
    g_ffn_in_t = mm("mm_g_ffn_in", df, h2, "tn", BF16)

    def pair_go(tag, grads_t, deps=()):
        srcs = [g.reshape(NDEV, -1, D) for g in grads_t]
        return exchange_start("pair_" + tag + "_start", pair_plan, NCHIP * len(srcs), srcs,
                              [((NCHIP,) + s.shape[1:], s.dtype) for s in srcs], deps)

    def chip_go(tag, names, pair_started, after):
        own, from_sibling = exchange_wait("pair_" + tag + "_wait", pair_plan, pair_started, after)
        partials = [pair_sum("pair_sum_" + n, g, r) for n, g, r in zip(names, own, from_sibling)]
        return exchange_start("chip_" + tag + "_start", chip_plan, (NCHIP - 1) * len(partials), partials,
                              [((NCHIP - 1,) + p.shape[1:], p.dtype) for p in partials])

    def chip_done(tag, names, chip_started, after):
        partials, from_chips = exchange_wait("chip_" + tag + "_wait", chip_plan, chip_started, after)
        return [chip_sum("chip_sum_" + n, p, r) for n, p, r in zip(names, partials, from_chips)]

    pair_ffn = pair_go("ffn", [g_ffn_out, g_ffn_in_t])

    def f_dnorm2(dh, x1v, dx2v, o1v, g, sc, g1v):
        r, xh = _rms_stats(x1v)
        dxh = dh * (1.0 + sc) * g
        dx1 = dx2v + _rms_bwd(dxh, xh, r)
        return ([dx1, g1v * dx1],
                [_colsum(dh * xh * g), _colsum(dh), _colsum(dh * (1.0 + sc) * xh), _colsum(dx1 * o1v)])

    dx1, do1, d_sc2, d_sh2, d_n2g, d_g1 = rowwise(
        "dnorm2", f_dnorm2, [dh2, x1, dx2, o1], [norm2_g, sc2, g1], [(D, F32), (D, BF16)], [D, D, D, D], 256,
        deps=[pair_ffn[4]])

    dmerged = mm("mm_dmerged", do1, w_out_f, "nt")
    g_out = mm("mm_g_out", merged, do1, "tn", BF16)
    chip_ffn = chip_go("ffn", ("w_ffn_out", "w_ffn_in"), pair_ffn, g_out)

    def f_dmerge(dm, yc_v, za, zb, glc, gls):
        sc_ = _sig(glc)
        ss_ = _sig(gls)
        sb_ = _sig(zb)
        dys = dm * ss_
        dz2 = jnp.concatenate([dys * sb_, dys * za * sb_ * (1.0 - sb_)], axis=1)
        dgl = jnp.concatenate([dm * yc_v * sc_ * (1.0 - sc_), dm * (za * sb_) * ss_ * (1.0 - ss_)], axis=1)
        return [dm * sc_, dz2, dgl], []

    dyconv, dz2, dz = rowwise(
        "dmerge", f_dmerge, [dmerged, y_conv, (z2, D, 0), (z2, D, 1), (z, D, 0), (z, D, 1)],
        [], [(D, BF16), (2 * D, BF16), (2 * D, BF16, ZW, 0)], [], 256, deps=[chip_ffn[4]])

    g_conv_proj_t = mm("mm_g_conv_proj", dyconv, s_act, "tn", BF16)
    dgl = mm("mm_dgl", dz2, ssm_glu_t, "nn")
    g_ssm_glu_t = mm("mm_g_ssm_glu", dz2, gl, "tn", BF16)
    pair_mid = pair_go("mid", [g_out, g_conv_proj_t, g_ssm_glu_t])
    ds = mm("mm_ds", dyconv, conv_proj_t, "nn", deps=[pair_mid[4]])
    dz, d_lng, d_lnb, d_cb, d_cw32 = conv_bwd(ds, yc, z, w32, conv_ln_g, conv_ln_b, dz)
    dz, d_d, d_ar, d_ai, d_wb_re, d_wb_im, d_wc = ssm_bwd(
        dgl, ypre, z, xs_re, xs_im, to_b(wbt_re), to_b(wbt_im), to_b(wct), e_re_b, e_im_b, dvec, dz)
    chip_mid = chip_go("mid", ("w_out", "conv_proj", "ssm_glu"), pair_mid, dz)

    dh1 = mm("mm_dh1", dz, w_in_t, "nn", tiles=(1024, 1024, CW), b_rot=Z_ROT, deps=[chip_mid[4]])
    g_in_t = mm("mm_g_in", dz, h1, "tn", BF16, tiles=(CW, 1024, 1024), o_rot=Z_ROT)
    pair_in = pair_go("in", [g_in_t])

    def f_dnorm1(dh, xv, dx1v, g, sc):
        r, xh = _rms_stats(xv)
        dxh = dh * (1.0 + sc) * g
        return ([dx1v + _rms_bwd(dxh, xh, r)],
                [_colsum(dh * xh * g), _colsum(dh), _colsum(dh * (1.0 + sc) * xh)])

    grad_x, d_sc1, d_sh1, d_n1g = rowwise(
        "dnorm1", f_dnorm1, [dh1, xs, dx1], [n1g, sc1], [(D, F32)], [D, D, D], 256, deps=[pair_in[4]])
    chip_in = chip_go("in", ("w_in",), pair_in, grad_x)

    weights = {
        "w_ada": (w_ada, m_w_ada, v_w_ada), "b_ada": (b_ada, m_b_ada, v_b_ada), "norm1_g": (norm1_g, m_norm1_g, v_norm1_g),
        "w_in": (w_in, m_w_in, v_w_in), "conv_w": (conv_w, m_conv_w, v_conv_w), "conv_b": (conv_b, m_conv_b, v_conv_b),
        "conv_ln_g": (conv_ln_g, m_conv_ln_g, v_conv_ln_g), "conv_ln_b": (conv_ln_b, m_conv_ln_b, v_conv_ln_b),
        "conv_proj": (conv_proj, m_conv_proj, v_conv_proj), "ssm_a_re": (ssm_a_re, m_ssm_a_re, v_ssm_a_re),
        "ssm_a_im": (ssm_a_im, m_ssm_a_im, v_ssm_a_im), "ssm_b_re": (ssm_b_re, m_ssm_b_re, v_ssm_b_re),
        "ssm_b_im": (ssm_b_im, m_ssm_b_im, v_ssm_b_im), "ssm_c_re": (ssm_c_re, m_ssm_c_re, v_ssm_c_re),
        "ssm_c_im": (ssm_c_im, m_ssm_c_im, v_ssm_c_im), "ssm_d": (ssm_d, m_ssm_d, v_ssm_d),
        "ssm_log_dt": (ssm_log_dt, m_ssm_log_dt, v_ssm_log_dt), "ssm_glu": (ssm_glu, m_ssm_glu, v_ssm_glu),
        "w_out": (w_out, m_w_out, v_w_out), "norm2_g": (norm2_g, m_norm2_g, v_norm2_g),
        "w_ffn_in": (w_ffn_in, m_w_ffn_in, v_w_ffn_in), "w_ffn_out": (w_ffn_out, m_w_ffn_out, v_w_ffn_out),
        "final_g": (final_g, m_final_g, v_final_g),
    }
    order = list(weights)
    big = ("w_ada", "w_in", "conv_proj", "ssm_glu", "w_out", "w_ffn_in", "w_ffn_out")
    grads, delta, new_m, new_v = {}, {}, {}, {}

    def adam_big(n, g2d):
        wv, mv, vv = weights[n]
        shp = wv.shape
        d_, m_, v_ = adam("adam_" + n, wv.reshape(shp[-2:]), g2d, mv.reshape(shp[-2:]), vv.reshape(shp[-2:]))
        grads[n], delta[n], new_m[n], new_v[n] = g2d.reshape(shp), d_.reshape(shp), m_.reshape(shp), v_.reshape(shp)
        return d_

    gs_ffn_out, gs_ffn_in = chip_done("ffn", ("w_ffn_out", "w_ffn_in"), chip_ffn, chip_in[4])
    adam_big("w_ffn_out", gs_ffn_out)
    last = adam_big("w_ffn_in", gs_ffn_in.T)
    gs_out, gs_conv_proj, gs_ssm_glu = chip_done("mid", ("w_out", "conv_proj", "ssm_glu"), chip_mid, last)
    adam_big("w_out", gs_out)
    adam_big("conv_proj", gs_conv_proj.reshape(-1, CW).T)
    adam_big("ssm_glu", gs_ssm_glu.reshape(-1, SW).T)

    d_bb_re = _diag_blocks(d_wb_re.transpose(0, 2, 1), NP, GH).reshape(NST, GH)
    d_bb_im = _diag_blocks(d_wb_im.transpose(0, 2, 1), NP, GH).reshape(NST, GH)
    d_wct = d_wc.transpose(0, 2, 1)
    d_c_re = _diag_blocks(d_wct[:, :, 0:SB], GH, NP)
    d_c_im = -_diag_blocks(d_wct[:, :, SB:2 * SB], GH, NP)
    d_a_re, d_a_im, d_ldt, d_b_re, d_b_im = disc_bwd(
        a_re_c, a_im_c, ldt_c, b_re_r, b_im_r, expand, d_ar.reshape(NST, 1), d_ai.reshape(NST, 1), d_bb_re, d_bb_im)

    dmod = jnp.concatenate([d_sh1, d_sc1, d_g1, d_sh2, d_sc2, d_g2], axis=1)
    small_local = [dmod.reshape(-1), d_n1g.reshape(-1), d_cw32[0:KC].reshape(-1), d_cb.reshape(-1), d_lng.reshape(-1),
                   d_lnb.reshape(-1), d_a_re.reshape(-1), d_a_im.reshape(-1), d_b_re.reshape(-1), d_b_im.reshape(-1),
                   d_c_re.reshape(-1), d_c_im.reshape(-1), d_d.reshape(-1), d_ldt.reshape(-1), d_n2g.reshape(-1),
                   d_final_g.reshape(-1), loss_l[0, 0:1]]
    small_sizes = [v.shape[0] for v in small_local]
    packed = _pad_rows(jnp.concatenate(small_local), 256 * LANE).reshape(-1, LANE)
    (small_all,) = all_gather("gather_small", [packed])
    small_sum = sum_slots("sum_small", small_all).reshape(-1)
    pieces, pos = [], 0
    for n in small_sizes:
        pieces.append(small_sum[pos:pos + n])
        pos += n
    (g_b_ada, g_n1g, g_cw_full, g_cb, g_lng, g_lnb, g_a_re, g_a_im, g_b_re, g_b_im, g_c_re, g_c_im, g_d, g_ldt,
     g_n2g, g_fg, loss_sum) = pieces
    loss = loss_sum[0]
    dmod_all = small_all.reshape(NDEV, -1)[:, 0:NMOD * D]
    g_w_ada = ada_grad(act_all, lax.dynamic_slice_in_dim(dmod_all, me * ncol, ncol, axis=1))
    ccol = conv_w.shape[2]
    g_conv_w = lax.dynamic_slice_in_dim(g_cw_full.reshape(KC, CW), me * ccol, ccol, axis=1)

    adam_big("w_ada", g_w_ada)
    grads.update({
        "b_ada": g_b_ada.reshape(b_ada.shape), "norm1_g": g_n1g.reshape(norm1_g.shape),
        "conv_w": g_conv_w[None], "conv_b": g_cb.reshape(conv_b.shape),
        "conv_ln_g": g_lng.reshape(conv_ln_g.shape), "conv_ln_b": g_lnb.reshape(conv_ln_b.shape),
        "ssm_a_re": g_a_re.reshape(ssm_a_re.shape),
        "ssm_a_im": g_a_im.reshape(ssm_a_im.shape), "ssm_b_re": g_b_re.reshape(ssm_b_re.shape),
        "ssm_b_im": g_b_im.reshape(ssm_b_im.shape), "ssm_c_re": g_c_re.reshape(ssm_c_re.shape),
        "ssm_c_im": g_c_im.reshape(ssm_c_im.shape), "ssm_d": g_d.reshape(ssm_d.shape),
        "ssm_log_dt": g_ldt.reshape(ssm_log_dt.shape),
        "norm2_g": g_n2g.reshape(norm2_g.shape),
        "final_g": g_fg.reshape(final_g.shape),
    })
    small = [n for n in order if n not in big]
    sizes = [weights[n][0].size for n in small]

    def pack(idx_or_grad):
        vs = [(grads[n] if idx_or_grad is None else weights[n][idx_or_grad]).reshape(-1) for n in small]
        return _pad_rows(jnp.concatenate(vs), 256 * LANE).reshape(-1, LANE)

    d_p, m_p, v_p = adam("adam_small", pack(0), pack(None), pack(1), pack(2))
    d_p, m_p, v_p = d_p.reshape(-1), m_p.reshape(-1), v_p.reshape(-1)
    pos = 0
    for n, sz in zip(small, sizes):
        shp = weights[n][0].shape
        delta[n] = d_p[pos:pos + sz].reshape(shp)
        new_m[n] = m_p[pos:pos + sz].reshape(shp)
        new_v[n] = v_p[pos:pos + sz].reshape(shp)
        pos += sz

    (gs_in,) = chip_done("in", ("w_in",), chip_in, d_p)
    adam_big("w_in", gs_in.T)

    return (loss, grad_x[None], *[grads[n] for n in order], *[delta[n] for n in order],
            *[new_m[n] for n in order], *[new_v[n] for n in order])
```

```python
import functools
import math

import jax
import jax.numpy as jnp
from jax import lax
from jax.experimental import pallas as pl
from jax.experimental.pallas import tpu as pltpu

F32 = jnp.float32
BF16 = jnp.bfloat16

D = 1024
CW = 512
KC = 31
SW = 512
NG = 32
GH = 16
NP = 64
NST = NG * NP
FH = 2816
NMOD = 6
NDEV = 8
EPS = 1e-6
CB = 128
SB = 512
NBLK = SW // CB
HALO = 32
ZW = 2 * CW + SW + 2 * D
Z_ROT = (ZW // CW, 3)
ZB_A, ZB_G, ZB_U = 4, 5, 6

ADAM_LR = 0.001
ADAM_B1 = 0.9
ADAM_B2 = 0.999
ADAM_EPS = 1e-08
ADAM_WD = 0.01
ADAM_STEP = 10

V7X_VMEM_BYTES = 64 * 1024 * 1024
VMEM_LIMIT = V7X_VMEM_BYTES - 8 * 1024 * 1024
LANE = 128
MESH = pl.DeviceIdType.MESH
ANY_SPEC = pl.BlockSpec(memory_space=pl.ANY)


def _params(sem=None, **kw):
    if sem is not None:
        kw["dimension_semantics"] = sem
    return pltpu.CompilerParams(vmem_limit_bytes=VMEM_LIMIT, **kw)


def _tile(n, most):
    best = None
    for t in range(LANE, most + 1, LANE):
        if n % t == 0:
            best = t
    if best is None:
        raise ValueError(f"no tile for {n}")
    return best


def _sig(x):
    return jax.nn.sigmoid(x)


def mm(name, a, b, mode, out_dtype=F32, tiles=None, b_rot=None, o_rot=None, deps=()):
    if mode == "nn":
        (m, k), (k2, n) = a.shape, b.shape
    elif mode == "nt":
        (m, k), (n, k2) = a.shape, b.shape
    else:
        (k, m), (k2, n) = a.shape, b.shape
    assert k == k2, (name, a.shape, b.shape)
    bm, bn, bk = tiles or (_tile(m, 1024), _tile(n, 1408), _tile(k, 1408 if k % 1408 == 0 else 1024))
    bm, bn, bk = min(bm, m), min(bn, n), min(bk, k)
    assert m % bm == 0 and n % bn == 0 and k % bk == 0, (name, m, n, k, bm, bn, bk)
    nk = k // bk
    rot = lambda idx, r: idx if r is None else (idx + r[1]) % r[0]
    if mode == "nn":
        a_spec = pl.BlockSpec((bm, bk), lambda i, j, kk: (i, kk))
        b_spec = pl.BlockSpec((bk, bn), lambda i, j, kk: (rot(kk, b_rot), j))
        dims = (((1,), (0,)), ((), ()))
    elif mode == "nt":
        a_spec = pl.BlockSpec((bm, bk), lambda i, j, kk: (i, kk))
        b_spec = pl.BlockSpec((bn, bk), lambda i, j, kk: (rot(j, b_rot), kk))
        dims = (((1,), (1,)), ((), ()))
    else:
        assert b_rot is None
        a_spec = pl.BlockSpec((bk, bm), lambda i, j, kk: (kk, i))
        b_spec = pl.BlockSpec((bk, bn), lambda i, j, kk: (kk, j))
        dims = (((0,), (0,)), ((), ()))

    def body(a_ref, b_ref, *rest):
        o_ref, acc_ref = rest[-2:]
        kk = pl.program_id(2)

        @pl.when(kk == 0)
        def _():
            acc_ref[...] = jnp.zeros_like(acc_ref)

        acc_ref[...] += lax.dot_general(a_ref[...], b_ref[...], dims, preferred_element_type=F32)

        @pl.when(kk == nk - 1)
        def _():
            o_ref[...] = acc_ref[...].astype(o_ref.dtype)

    return pl.pallas_call(
        body, name=name,
        grid=(m // bm, n // bn, nk),
        in_specs=[a_spec, b_spec] + [ANY_SPEC] * len(deps),
        out_specs=pl.BlockSpec((bm, bn), lambda i, j, kk: (rot(i, o_rot), j)),
        out_shape=jax.ShapeDtypeStruct((m, n), out_dtype),
        scratch_shapes=[pltpu.VMEM((bm, bn), F32)],
        compiler_params=_params(("parallel", "parallel", "arbitrary")),
    )(a, b, *deps)


def rowwise(name, fn, rows, consts, out_rows, out_sums, ts, alias=None, deps=()):
    rows = [r if isinstance(r, tuple) else (r, r.shape[1], 0) for r in rows]
    out_rows = [o if len(o) == 4 else (o[0], o[1], o[0], 0) for o in out_rows]
    s = rows[0][0].shape[0]
    nt = s // ts
    nr, nc, no, ns = len(rows), len(consts), len(out_rows), len(out_sums)
    in_specs = [pl.BlockSpec((ts, w), functools.partial(lambda i, cb: (i, cb), cb=cb)) for (_, w, cb) in rows]
    in_specs += [pl.BlockSpec(c.shape, lambda i: (0, 0)) for c in consts]
    operands = [r[0] for r in rows] + list(consts)
    aliases = {}
    if alias is not None:
        in_specs.append(pl.BlockSpec(memory_space=pl.ANY))
        operands.append(alias[0])
        aliases = {nr + nc: alias[1]}
    in_specs += [ANY_SPEC] * len(deps)
    operands += list(deps)
    out_shape = [jax.ShapeDtypeStruct((s, tw), dt) for (_, dt, tw, _) in out_rows]
    out_shape += [jax.ShapeDtypeStruct((1, w), F32) for w in out_sums]
    out_specs = [pl.BlockSpec((ts, w), functools.partial(lambda i, cb: (i, cb), cb=cb)) for (w, _, _, cb) in out_rows]
    out_specs += [pl.BlockSpec((1, w), lambda i: (0, 0)) for w in out_sums]
    n_in = len(operands)

    def body(*refs):
        ins, outs = refs[:nr + nc], refs[n_in:]
        i = pl.program_id(0)
        ro, so = fn(*[r[...] for r in ins])
        for q in range(no):
            outs[q][...] = ro[q].astype(outs[q].dtype)
        if ns:
            @pl.when(i == 0)
            def _():
                for q in range(ns):
                    outs[no + q][...] = jnp.zeros_like(outs[no + q])

            for q in range(ns):
                outs[no + q][...] += so[q]

    return pl.pallas_call(
        body, name=name, grid=(nt,),
        in_specs=in_specs, out_specs=out_specs, out_shape=out_shape, input_output_aliases=aliases,
        compiler_params=_params(("arbitrary",) if ns else ("parallel",)),
    )(*operands)


def _colsum(v):
    return jnp.sum(v, axis=0, keepdims=True)


def _rms_stats(xv):
    r = lax.rsqrt(jnp.mean(xv * xv, axis=-1, keepdims=True) + EPS)
    return r, xv * r


def _rms_bwd(dxhat, xhat, r):
    return r * (dxhat - xhat * jnp.mean(dxhat * xhat, axis=-1, keepdims=True))


def _gelu(v):
    k = math.sqrt(2.0 / math.pi)
    t = jnp.tanh(k * (v + 0.044715 * v * v * v))
    return 0.5 * v * (1.0 + t), t


def _gelu_grad(v, t):
    k = math.sqrt(2.0 / math.pi)
    return 0.5 * (1.0 + t) + 0.5 * v * (1.0 - t * t) * k * (1.0 + 3.0 * 0.044715 * v * v)


CONV_TS = 256
CONV_CH = 64


def _ln_fwd(yc, g, b):
    mu = jnp.mean(yc, axis=-1, keepdims=True)
    xc = yc - mu
    rstd = lax.rsqrt(jnp.mean(xc * xc, axis=-1, keepdims=True) + EPS)
    nhat = xc * rstd
    return nhat, rstd, nhat * g + b


SUBL = 8


def _shifted_copies(buf, sh, ts):
    for b in range(1, SUBL):
        sh[b - 1] = buf[pl.ds(b, ts + HALO - SUBL), :]


def _shifted(buf, sh, start):
    b = start % SUBL
    if b == 0:
        return buf[pl.ds(start, CONV_CH), :]
    return sh[b - 1, pl.ds(start - b, CONV_CH), :]


def conv_fwd(z, w32, cb, lg, lb):
    s = z.shape[0]
    ts = CONV_TS
    nt = s // ts
    hb = ts // HALO

    def body(a_ref, g_ref, ah_ref, gh_ref, w_ref, cb_ref, lg_ref, lb_ref, yc_ref, s_ref, ubuf, ush):
        i = pl.program_id(0)
        first = (i > 0).astype(F32)
        ubuf[0:HALO, :] = ah_ref[...] * _sig(gh_ref[...]) * first
        ubuf[HALO:HALO + ts, :] = a_ref[...] * _sig(g_ref[...])
        _shifted_copies(ubuf, ush, ts)
        for c0 in range(0, ts, CONV_CH):
            acc = jnp.zeros((CONV_CH, CW), F32)
            for k in range(KC):
                acc = acc + w_ref[k:k + 1, :] * _shifted(ubuf, ush, c0 + k + 2)
            yc = acc + cb_ref[...]
            yc_ref[c0:c0 + CONV_CH, :] = yc
            _, _, ln = _ln_fwd(yc, lg_ref[...], lb_ref[...])
            s_ref[c0:c0 + CONV_CH, :] = (ln * _sig(ln)).astype(s_ref.dtype)

    cur = lambda cbk: pl.BlockSpec((ts, CW), functools.partial(lambda i, q: (i, q), q=cbk))
    prev = lambda cbk: pl.BlockSpec((HALO, CW), functools.partial(lambda i, q: (jnp.maximum(i * hb - 1, 0), q), q=cbk))
    const = lambda a: pl.BlockSpec(a.shape, lambda i: (0, 0))
    return pl.pallas_call(
        body, name="conv_fwd", grid=(nt,),
        in_specs=[cur(ZB_A), cur(ZB_G), prev(ZB_A), prev(ZB_G), const(w32), const(cb), const(lg), const(lb)],
        out_specs=[pl.BlockSpec((ts, CW), lambda i: (i, 0)), pl.BlockSpec((ts, CW), lambda i: (i, 0))],
        out_shape=[jax.ShapeDtypeStruct((s, CW), F32), jax.ShapeDtypeStruct((s, CW), BF16)],
        scratch_shapes=[pltpu.VMEM((HALO + ts, CW), F32), pltpu.VMEM((SUBL - 1, ts + HALO - SUBL, CW), F32)],
        compiler_params=_params(("parallel",)),
    )(z, z, z, z, w32, cb, lg, lb)


def conv_bwd(ds, yc, z, w32, lg, lb, dz):
    s = z.shape[0]
    ts = CONV_TS
    nt = s // ts
    hb = ts // HALO
    last_hb = s // HALO - 1

    def ln_bwd(dsv, ycv, g, b):
        nhat, rstd, ln = _ln_fwd(ycv, g, b)
        sg = _sig(ln)
        dln = dsv * (sg * (1.0 + ln * (1.0 - sg)))
        dnh = dln * g
        dyc = rstd * (dnh - jnp.mean(dnh, axis=-1, keepdims=True)
                      - nhat * jnp.mean(dnh * nhat, axis=-1, keepdims=True))
        return dyc, dln, nhat

    def body(ds_ref, yc_ref, dsn_ref, ycn_ref, a_ref, g_ref, ah_ref, gh_ref, w_ref, lg_ref, lb_ref, dz_in,
             dz_ref, dlg_ref, dlb_ref, dcb_ref, dw_ref, dbuf, ubuf, dsh, ush):
        i = pl.program_id(0)

        @pl.when(i == 0)
        def _():
            dlg_ref[...] = jnp.zeros_like(dlg_ref)
            dlb_ref[...] = jnp.zeros_like(dlb_ref)
            dcb_ref[...] = jnp.zeros_like(dcb_ref)
            dw_ref[...] = jnp.zeros_like(dw_ref)

        lg, lb = lg_ref[...], lb_ref[...]
        dyc, dln, nhat = ln_bwd(ds_ref[...], yc_ref[...], lg, lb)
        dlg_ref[...] += _colsum(dln * nhat)
        dlb_ref[...] += _colsum(dln)
        dcb_ref[...] += _colsum(dyc)
        dbuf[0:ts, :] = dyc
        nxt = (i < nt - 1).astype(F32)
        dbuf[ts:ts + HALO, :] = ln_bwd(dsn_ref[...], ycn_ref[...], lg, lb)[0] * nxt
        first = (i > 0).astype(F32)
        ubuf[0:HALO, :] = ah_ref[...] * _sig(gh_ref[...]) * first
        ubuf[HALO:HALO + ts, :] = a_ref[...] * _sig(g_ref[...])
        _shifted_copies(dbuf, dsh, ts)
        _shifted_copies(ubuf, ush, ts)
        for c0 in range(0, ts, CONV_CH):
            du = jnp.zeros((CONV_CH, CW), F32)
            dyc_c = dbuf[c0:c0 + CONV_CH, :]
            for k in range(KC):
                du = du + w_ref[k:k + 1, :] * _shifted(dbuf, dsh, c0 + KC - 1 - k)
                dw_ref[k:k + 1, :] += _colsum(dyc_c * _shifted(ubuf, ush, c0 + k + 2))
            av = a_ref[c0:c0 + CONV_CH, :]
            sg = _sig(g_ref[c0:c0 + CONV_CH, :])
            dz_ref[c0:c0 + CONV_CH, 0:CW] = (du * sg).astype(dz_ref.dtype)
            dz_ref[c0:c0 + CONV_CH, CW:2 * CW] = (du * av * sg * (1.0 - sg)).astype(dz_ref.dtype)

    cur = lambda w, cbk: pl.BlockSpec((ts, w), functools.partial(lambda i, q: (i, q), q=cbk))
    prev = lambda cbk: pl.BlockSpec((HALO, CW), functools.partial(lambda i, q: (jnp.maximum(i * hb - 1, 0), q), q=cbk))
    nxt_spec = pl.BlockSpec((HALO, CW), lambda i: (jnp.minimum((i + 1) * hb, last_hb), 0))
    const = lambda a: pl.BlockSpec(a.shape, lambda i: (0, 0))
    acc = lambda r: pl.BlockSpec((r, CW), lambda i: (0, 0))
    return pl.pallas_call(
        body, name="conv_bwd", grid=(nt,),
        in_specs=[cur(CW, 0), cur(CW, 0), nxt_spec, nxt_spec, cur(CW, ZB_A), cur(CW, ZB_G), prev(ZB_A), prev(ZB_G),
                  const(w32), const(lg), const(lb), pl.BlockSpec(memory_space=pl.ANY)],
        out_specs=[pl.BlockSpec((ts, 2 * CW), lambda i: (i, ZB_A // 2)), acc(1), acc(1), acc(1), acc(HALO)],
        out_shape=[jax.ShapeDtypeStruct(dz.shape, dz.dtype), jax.ShapeDtypeStruct((1, CW), F32),
                   jax.ShapeDtypeStruct((1, CW), F32), jax.ShapeDtypeStruct((1, CW), F32),
                   jax.ShapeDtypeStruct((HALO, CW), F32)],
        scratch_shapes=[pltpu.VMEM((ts + HALO, CW), F32), pltpu.VMEM((HALO + ts, CW), F32)]
        + [pltpu.VMEM((SUBL - 1, ts + HALO - SUBL, CW), F32)] * 2,
        input_output_aliases={11: 0},
        compiler_params=_params(("arbitrary",)),
    )(ds, yc, ds, yc, z, z, z, z, w32, lg, lb, dz)


SSM_TS = 512
GRP = 8


def _cmul(ar, ai, br, bi):
    return ar * br - ai * bi, ar * bi + ai * br


def _scan_tables(ar, ai, reverse):
    n = ar.shape[1]
    row = lax.broadcasted_iota(jnp.int32, (GRP, n), 0)
    dist = (GRP - 1 - row) if reverse else row
    one_r = jnp.broadcast_to(ar, (GRP, n))
    one_i = jnp.broadcast_to(ai, (GRP, n))
    p2r, p2i = _cmul(one_r, one_i, one_r, one_i)
    p4r, p4i = _cmul(p2r, p2i, p2r, p2i)
    steps = []
    for sft, (pr, pi) in ((1, (one_r, one_i)), (2, (p2r, p2i)), (4, (p4r, p4i))):
        keep = dist >= sft
        steps.append((jnp.where(keep, pr, 0.0), jnp.where(keep, pi, 0.0)))
    cr, ci = one_r, one_i
    accr, acci = one_r, one_i
    for e in range(1, GRP):
        cr, ci = _cmul(cr, ci, one_r, one_i)
        accr = jnp.where(dist == e, cr, accr)
        acci = jnp.where(dist == e, ci, acci)
    return steps, (accr, acci)


def _scan_group(xr, xi, steps, carry_tab, cr, ci, reverse):
    for sft, (tr, ti) in zip((1, 2, 4), steps):
        amt = (GRP - sft) if reverse else sft
        sr = pltpu.roll(xr, amt, 0)
        si = pltpu.roll(xi, amt, 0)
        xr, xi = xr + tr * sr - ti * si, xi + tr * si + ti * sr
    pr, pi = carry_tab
    xr = xr + pr * cr - pi * ci
    xi = xi + pr * ci + pi * cr
    return xr, xi


def ssm_fwd(z, wb_re, wb_im, wc, e_re, e_im, dvec):
    s = z.shape[0]
    ts = SSM_TS
    nt = s // ts
    ucol0 = ZB_U * CW // CB

    def body(u_ref, wbr_ref, wbi_ref, wc_ref, er_ref, ei_ref, d_ref, xr_ref, xi_ref, y_ref, gl_ref, car_r, car_i):
        i = pl.program_id(1)

        @pl.when(i == 0)
        def _():
            car_r[...] = jnp.zeros_like(car_r)
            car_i[...] = jnp.zeros_like(car_i)

        u = u_ref[...]
        ub = u.astype(BF16)
        xr_ref[...] = jnp.dot(ub, wbr_ref[0], preferred_element_type=F32)
        xi_ref[...] = jnp.dot(ub, wbi_ref[0], preferred_element_type=F32)
        steps, ctab = _scan_tables(er_ref[0], ei_ref[0], False)

        def grp(r, carry):
            cr, ci = carry
            r0 = pl.multiple_of(r * GRP, GRP)
            xr, xi = _scan_group(xr_ref[pl.ds(r0, GRP), :], xi_ref[pl.ds(r0, GRP), :], steps, ctab, cr, ci, False)
            xr_ref[pl.ds(r0, GRP), :] = xr
            xi_ref[pl.ds(r0, GRP), :] = xi
            return (jnp.broadcast_to(xr[GRP - 1:GRP, :], (GRP, SB)), jnp.broadcast_to(xi[GRP - 1:GRP, :], (GRP, SB)))

        cr, ci = lax.fori_loop(0, ts // GRP, grp, (car_r[...], car_i[...]))
        car_r[...] = cr
        car_i[...] = ci
        y = (jnp.dot(xr_ref[...].astype(BF16), wc_ref[0, 0:SB, :], preferred_element_type=F32)
             + jnp.dot(xi_ref[...].astype(BF16), wc_ref[0, SB:2 * SB, :], preferred_element_type=F32)
             + d_ref[0] * u)
        y_ref[...] = y
        gl_ref[...] = _gelu(y)[0].astype(gl_ref.dtype)

    blk3 = lambda a: pl.BlockSpec((1,) + a.shape[1:], lambda j, i: (j, 0, 0))
    return pl.pallas_call(
        body, name="ssm_fwd", grid=(NBLK, nt),
        in_specs=[pl.BlockSpec((ts, CB), lambda j, i: (i, ucol0 + j)),
                  blk3(wb_re), blk3(wb_im), blk3(wc), blk3(e_re), blk3(e_im), blk3(dvec)],
        out_specs=[pl.BlockSpec((ts, SB), lambda j, i: (i, j)), pl.BlockSpec((ts, SB), lambda j, i: (i, j)),
                   pl.BlockSpec((ts, CB), lambda j, i: (i, j)), pl.BlockSpec((ts, CB), lambda j, i: (i, j))],
        out_shape=[jax.ShapeDtypeStruct((s, NST), F32), jax.ShapeDtypeStruct((s, NST), F32),
                   jax.ShapeDtypeStruct((s, SW), F32), jax.ShapeDtypeStruct((s, SW), BF16)],
        scratch_shapes=[pltpu.VMEM((GRP, SB), F32), pltpu.VMEM((GRP, SB), F32)],
        compiler_params=_params(("parallel", "arbitrary")),
    )(z, wb_re, wb_im, wc, e_re, e_im, dvec)


def ssm_bwd(dgl, ypre, z, xs_re, xs_im, wbt_re, wbt_im, wct, e_re, e_im, dvec, dz):
    s = z.shape[0]
    ts = SSM_TS
    nt = s // ts
    ucol0 = ZB_U * CW // CB
    tn_dims = (((0,), (0,)), ((), ()))

    def body(dgl_ref, y_ref, u_ref, xr_ref, xi_ref, wbtr_ref, wbti_ref, wct_ref, er_ref, ei_ref, d_ref, dz_in,
             du_ref, dd_ref, dar_ref, dai_ref, dwbr_ref, dwbi_ref, dwc_ref,
             lr_ref, li_ref, car_r, car_i, acc_r, acc_i):
        i = pl.program_id(1)

        @pl.when(i == 0)
        def _():
            for ref in (car_r, car_i, acc_r, acc_i, dd_ref, dwbr_ref, dwbi_ref, dwc_ref):
                ref[...] = jnp.zeros_like(ref)

        u = u_ref[...]
        y = y_ref[...]
        dy = dgl_ref[...] * _gelu_grad(y, _gelu(y)[1])
        dd_ref[0] += _colsum(dy * u)
        dyb = dy.astype(BF16)
        dxo = jnp.dot(dyb, wct_ref[0], preferred_element_type=F32)
        lr_ref[...] = dxo[:, 0:SB]
        li_ref[...] = dxo[:, SB:2 * SB]
        steps, ctab = _scan_tables(er_ref[0], -ei_ref[0], True)
        row = lax.broadcasted_iota(jnp.int32, (GRP, SB), 0)

        def grp(q, carry):
            cr, ci, ar, ai = carry
            r0 = pl.multiple_of((ts // GRP - 1 - q) * GRP, GRP)
            lr, li = _scan_group(lr_ref[pl.ds(r0, GRP), :], li_ref[pl.ds(r0, GRP), :], steps, ctab, cr, ci, True)
            lr_ref[pl.ds(r0, GRP), :] = lr
            li_ref[pl.ds(r0, GRP), :] = li
            nr = jnp.where(row == GRP - 1, cr, pltpu.roll(lr, GRP - 1, 0))
            ni = jnp.where(row == GRP - 1, ci, pltpu.roll(li, GRP - 1, 0))
            xr = xr_ref[pl.ds(r0, GRP), :]
            xi = xi_ref[pl.ds(r0, GRP), :]
            ar = ar + nr * xr + ni * xi
            ai = ai + ni * xr - nr * xi
            return (jnp.broadcast_to(lr[0:1, :], (GRP, SB)), jnp.broadcast_to(li[0:1, :], (GRP, SB)), ar, ai)

        cr, ci, ar, ai = lax.fori_loop(0, ts // GRP, grp, (car_r[...], car_i[...], acc_r[...], acc_i[...]))
        car_r[...] = cr
        car_i[...] = ci
        acc_r[...] = ar
        acc_i[...] = ai

        @pl.when(i == nt - 1)
        def _():
            dar_ref[0] = _colsum(ar)
            dai_ref[0] = _colsum(ai)

        lrb = lr_ref[...].astype(BF16)
        lib = li_ref[...].astype(BF16)
        du = (jnp.dot(lrb, wbtr_ref[0], preferred_element_type=F32)
              + jnp.dot(lib, wbti_ref[0], preferred_element_type=F32) + d_ref[0] * dy)
        du_ref[...] = du.astype(du_ref.dtype)
        ub = u.astype(BF16)
        dwbr_ref[0] += lax.dot_general(ub, lrb, tn_dims, preferred_element_type=F32)
        dwbi_ref[0] += lax.dot_general(ub, lib, tn_dims, preferred_element_type=F32)
        dwc_ref[0, 0:SB, :] += lax.dot_general(xr_ref[...].astype(BF16), dyb, tn_dims, preferred_element_type=F32)
        dwc_ref[0, SB:2 * SB, :] += lax.dot_general(xi_ref[...].astype(BF16), dyb, tn_dims, preferred_element_type=F32)

    rev = lambda i: nt - 1 - i
    blk3 = lambda a: pl.BlockSpec((1,) + a.shape[1:], lambda j, i: (j, 0, 0))
    acc3 = lambda r, c: pl.BlockSpec((1, r, c), lambda j, i: (j, 0, 0))
    return pl.pallas_call(
        body, name="ssm_bwd", grid=(NBLK, nt),
        in_specs=[pl.BlockSpec((ts, CB), lambda j, i: (rev(i), j)), pl.BlockSpec((ts, CB), lambda j, i: (rev(i), j)),
                  pl.BlockSpec((ts, CB), lambda j, i: (rev(i), ucol0 + j)),
                  pl.BlockSpec((ts, SB), lambda j, i: (rev(i), j)), pl.BlockSpec((ts, SB), lambda j, i: (rev(i), j)),
                  blk3(wbt_re), blk3(wbt_im), blk3(wct), blk3(e_re), blk3(e_im), blk3(dvec),
                  pl.BlockSpec(memory_space=pl.ANY)],
        out_specs=[pl.BlockSpec((ts, CB), lambda j, i: (rev(i), ucol0 + j)),
                   acc3(1, CB), acc3(1, SB), acc3(1, SB), acc3(CB, SB), acc3(CB, SB), acc3(2 * SB, CB)],
        out_shape=[jax.ShapeDtypeStruct(dz.shape, dz.dtype),
                   jax.ShapeDtypeStruct((NBLK, 1, CB), F32),
                   jax.ShapeDtypeStruct((NBLK, 1, SB), F32), jax.ShapeDtypeStruct((NBLK, 1, SB), F32),
                   jax.ShapeDtypeStruct((NBLK, CB, SB), F32), jax.ShapeDtypeStruct((NBLK, CB, SB), F32),
                   jax.ShapeDtypeStruct((NBLK, 2 * SB, CB), F32)],
        scratch_shapes=[pltpu.VMEM((ts, SB), F32), pltpu.VMEM((ts, SB), F32)] + [pltpu.VMEM((GRP, SB), F32)] * 4,
        input_output_aliases={11: 0},
        compiler_params=_params(("parallel", "arbitrary")),
    )(dgl, ypre, z, xs_re, xs_im, wbt_re, wbt_im, wct, e_re, e_im, dvec, dz)


def _disc(a_re, a_im, log_dt, b_re, b_im, expand):
    dt = jnp.dot(expand, jnp.exp(log_dt), preferred_element_type=F32, precision=lax.Precision.HIGHEST)
    mag = jnp.exp(dt * a_re)
    e_re, e_im = mag * jnp.cos(dt * a_im), mag * jnp.sin(dt * a_im)
    n_re, n_im = e_re - 1.0, e_im
    den = a_re * a_re + a_im * a_im
    q_re = (n_re * a_re + n_im * a_im) / den
    q_im = (n_im * a_re - n_re * a_im) / den
    return e_re, e_im, q_re * b_re - q_im * b_im, q_re * b_im + q_im * b_re


def _whole(a):
    return pl.BlockSpec(a.shape, functools.partial(lambda n: (0,) * n, n=a.ndim))


def disc_fwd(a_re, a_im, log_dt, b_re, b_im, expand):
    def body(ar, ai, ld, br, bi, ex, er_o, ei_o, bbr_o, bbi_o):
        er, ei, bbr, bbi = _disc(ar[...], ai[...], ld[...], br[...], bi[...], ex[...])
        er_o[...] = er
        ei_o[...] = ei
        bbr_o[...] = bbr
        bbi_o[...] = bbi

    ins = (a_re, a_im, log_dt, b_re, b_im, expand)
    outs = [jax.ShapeDtypeStruct(a_re.shape, F32)] * 2 + [jax.ShapeDtypeStruct(b_re.shape, F32)] * 2
    return pl.pallas_call(body, name="disc_fwd", in_specs=[_whole(a) for a in ins],
                          out_specs=[_whole(o) for o in outs], out_shape=outs, compiler_params=_params())(*ins)


def disc_bwd(a_re, a_im, log_dt, b_re, b_im, expand, de_re, de_im, dbb_re, dbb_im):
    def body(ar, ai, ld, br, bi, ex, der, dei, dbr, dbi, o_ar, o_ai, o_ld, o_br, o_bi):
        exv = ex[...]
        _, vjp = jax.vjp(lambda *p: _disc(*p, exv), ar[...], ai[...], ld[...], br[...], bi[...])
        g = vjp((der[...], dei[...], dbr[...], dbi[...]))
        for o, v in zip((o_ar, o_ai, o_ld, o_br, o_bi), g):
            o[...] = v

    ins = (a_re, a_im, log_dt, b_re, b_im, expand, de_re, de_im, dbb_re, dbb_im)
    outs = [jax.ShapeDtypeStruct(a.shape, F32) for a in (a_re, a_im, log_dt, b_re, b_im)]
    return pl.pallas_call(body, name="disc_bwd", in_specs=[_whole(a) for a in ins],
                          out_specs=[_whole(o) for o in outs], out_shape=outs, compiler_params=_params())(*ins)


def mod_fwd(c_all, w_ada, b_cols):
    def body(c_ref, w_ref, b_ref, act_ref, mod_ref):
        cv = c_ref[...]
        act = cv * _sig(cv)
        act_ref[...] = act
        mod_ref[...] = jnp.dot(act, w_ref[...], preferred_element_type=F32, precision=lax.Precision.HIGHEST) + b_ref[...]

    ins = (c_all, w_ada, b_cols)
    outs = [jax.ShapeDtypeStruct(c_all.shape, F32), jax.ShapeDtypeStruct((NDEV, w_ada.shape[1]), F32)]
    return pl.pallas_call(body, name="mod_fwd", in_specs=[_whole(a) for a in ins],
                          out_specs=[_whole(o) for o in outs], out_shape=outs, compiler_params=_params())(*ins)


def ada_grad(act_all, dmod_cols):
    def body(a_ref, d_ref, o_ref):
        o_ref[...] = lax.dot_general(a_ref[...], d_ref[...], (((0,), (0,)), ((), ())),
                                     preferred_element_type=F32, precision=lax.Precision.HIGHEST)

    out = jax.ShapeDtypeStruct((act_all.shape[1], dmod_cols.shape[1]), F32)
    return pl.pallas_call(body, name="ada_grad", in_specs=[_whole(act_all), _whole(dmod_cols)],
                          out_specs=_whole(out), out_shape=out, compiler_params=_params())(act_all, dmod_cols)


def _adam_math(w, g, m, v):
    m2 = ADAM_B1 * m + (1.0 - ADAM_B1) * g
    v2 = ADAM_B2 * v + (1.0 - ADAM_B2) * (g * g)
    m_hat = m2 / (1.0 - ADAM_B1 ** ADAM_STEP)
    v_hat = v2 / (1.0 - ADAM_B2 ** ADAM_STEP)
    delta = -ADAM_LR * (m_hat / (jnp.sqrt(v_hat) + ADAM_EPS) + ADAM_WD * w)
    return delta, m2, v2


def adam(name, w, g, m, v):
    r, c = w.shape
    tr = r
    for cand in (256, 128, 64, 32, 16, 8):
        if r % cand == 0 and r > cand:
            tr = cand
            break

    def body(w_ref, g_ref, m_ref, v_ref, d_o, m_o, v_o):
        d, m2, v2 = _adam_math(w_ref[...], g_ref[...], m_ref[...], v_ref[...])
        d_o[...] = d
        m_o[...] = m2
        v_o[...] = v2

    spec = pl.BlockSpec((tr, c), lambda i: (i, 0))
    out = jax.ShapeDtypeStruct((r, c), F32)
    return pl.pallas_call(body, name=name, grid=(r // tr,), in_specs=[spec] * 4, out_specs=[spec] * 3,
                          out_shape=[out] * 3, compiler_params=_params(("parallel",)))(w, g, m, v)


def adam_many(name, ws, gs, ms, vs):
    n = len(ws)

    def body(*refs):
        ins, outs = refs[:4 * n], refs[4 * n:]
        for q in range(n):
            d, m2, v2 = _adam_math(ins[q][...], ins[n + q][...], ins[2 * n + q][...], ins[3 * n + q][...])
            outs[q][...] = d
            outs[n + q][...] = m2
            outs[2 * n + q][...] = v2

    operands = list(ws) + list(gs) + list(ms) + list(vs)
    outs = [jax.ShapeDtypeStruct(w.shape, F32) for w in ws] * 3
    return pl.pallas_call(body, name=name, in_specs=[_whole(a) for a in operands],
                          out_specs=[_whole(o) for o in outs], out_shape=outs, compiler_params=_params())(*operands)


def _rows_tile(r, most):
    best = None
    for t in range(16, min(r, most) + 1, 16):
        if r % t == 0:
            best = t
    assert best is not None, r
    return best


def sum_slots(name, slots, out_dtype=F32):
    n, r, c = slots.shape
    tr = _rows_tile(r, max(16, (2 * 1024 * 1024) // (n * c)))

    def body(s_ref, o_ref):
        acc = s_ref[0].astype(F32)
        for q in range(1, n):
            acc = acc + s_ref[q].astype(F32)
        o_ref[...] = acc.astype(o_ref.dtype)

    return pl.pallas_call(body, name=name, grid=(r // tr,),
                          in_specs=[pl.BlockSpec((n, tr, c), lambda i: (0, i, 0))],
                          out_specs=pl.BlockSpec((tr, c), lambda i: (i, 0)),
                          out_shape=jax.ShapeDtypeStruct((r, c), out_dtype), compiler_params=_params(("parallel",)))(slots)


HBM_SPEC = pl.BlockSpec(memory_space=pltpu.HBM)


def _coords():
    return lax.axis_index("x"), lax.axis_index("y"), lax.axis_index("c")


def _linear(x, y, c):
    return 4 * x + 2 * y + c


def all_gather(name, shards):
    nq = len(shards)

    def body(*refs):
        xs, outs = refs[:nq], refs[nq:2 * nq]
        send_sems, recv_sems, local_sems = refs[2 * nq:2 * nq + 3]
        bufs = refs[2 * nq + 3:]
        x, y, cc = _coords()
        me, sibling = (x, y, cc), (x, y, 1 - cc)
        chips = [(1 - x, y), (x, 1 - y), (1 - x, 1 - y)]

        def slot(q, px, py, pc):
            return outs[q].at[_linear(px, py, pc)]

        def copy(q, k, block, to, src=None):
            return pltpu.make_async_remote_copy(
                src_ref=slot(q, *block) if src is None else src, dst_ref=slot(q, *block),
                send_sem=send_sems.at[7 * q + k], recv_sem=recv_sems.at[7 * q + k], device_id=to, device_id_type=MESH)

        loads = [pltpu.make_async_copy(xs[q], bufs[q], local_sems.at[q]) for q in range(nq)]
        for cp in loads:
            cp.start()
        for cp in loads:
            cp.wait()
        mine = [pltpu.make_async_copy(bufs[q], slot(q, *me), local_sems.at[q]) for q in range(nq)]
        first = []
        for q in range(nq):
            first.append(copy(q, 0, me, sibling, src=bufs[q]))
            first += [copy(q, 1 + j, me, (*chip, cc), src=bufs[q]) for j, chip in enumerate(chips)]
        for cp in mine + first:
            cp.start()
        passed = []
        for q in range(nq):
            for j, chip in enumerate(chips):
                copy(q, 1 + j, (*chip, cc), me).wait_recv()
                passed.append(copy(q, 4 + j, (*chip, cc), sibling))
                passed[-1].start()
        for q in range(nq):
            copy(q, 0, sibling, me).wait_recv()
            for j, chip in enumerate(chips):
                copy(q, 4 + j, (*chip, 1 - cc), me).wait_recv()
        for cp in first + passed:
            cp.wait_send()
        for cp in mine:
            cp.wait()

    return pl.pallas_call(
        body, name=name, in_specs=[HBM_SPEC] * nq, out_specs=[HBM_SPEC] * nq,
        out_shape=[jax.ShapeDtypeStruct((NDEV,) + s.shape, s.dtype) for s in shards],
        scratch_shapes=[pltpu.SemaphoreType.DMA((7 * nq,)), pltpu.SemaphoreType.DMA((7 * nq,)),
                        pltpu.SemaphoreType.DMA((nq,))] + [pltpu.VMEM(s.shape, s.dtype) for s in shards],
    )(*shards)


NCHIP = 4


SEM_SPEC = pl.BlockSpec(memory_space=pltpu.SEMAPHORE)
EFFECT = pltpu.SideEffectType.DATAFLOW_SIDE_EFFECTING


def _peer(x, y, cc, k):
    fx, fy, fc = (k >> 2) & 1, (k >> 1) & 1, k & 1
    return (x + fx - 2 * fx * x, y + fy - 2 * fy * y, cc + fc - 2 * fc * cc)


def gather_plan(srcs, lands, coords):
    x, y, cc = coords
    me = _linear(x, y, cc)
    return [(s, l.at[me], _peer(x, y, cc, k)) for s, l in zip(srcs, lands) for k in range(1, NDEV)]


def pair_plan(srcs, lands, coords):
    x, y, cc = coords
    return [(s.at[2 * chip + 1 - cc], l.at[chip], (x, y, 1 - cc)) for s, l in zip(srcs, lands) for chip in range(NCHIP)]


def chip_plan(srcs, lands, coords):
    x, y, cc = coords
    out = []
    for s, l in zip(srcs, lands):
        for k in range(1, NCHIP):
            px, py, _ = _peer(x, y, cc, 2 * k)
            out.append((s.at[2 * px + py], l.at[k - 1], (px, py, cc)))
    return out


def _remote(copy, i, send_sems, recv_sems):
    src, dst, dev = copy
    return pltpu.make_async_remote_copy(src_ref=src, dst_ref=dst, send_sem=send_sems.at[i], recv_sem=recv_sems.at[i],
                                        device_id=dev, device_id_type=MESH)


def exchange_start(name, plan, ncopy, srcs, land_shapes, deps=()):
    ns, nl, nd = len(srcs), len(land_shapes), len(deps)

    def body(*refs):
        s, l = refs[:ns], refs[ns:ns + nl]
        send_sems, recv_sems = refs[ns + nl + nd], refs[ns + nl + nd + 1]
        token = refs[-1]
        for i, cp in enumerate(plan(s, l, _coords())):
            _remote(cp, i, send_sems, recv_sems).start()
        token[...] = jnp.zeros_like(token)

    hbm = lambda a: pltpu.with_memory_space_constraint(a, pltpu.HBM)
    lands = [lax.empty(shp, dt) for shp, dt in land_shapes]
    thru = [pltpu.HBM(a.shape, a.dtype) for a in list(srcs) + lands]
    outs = pl.pallas_call(
        body, name=name,
        in_specs=[HBM_SPEC] * (ns + nl) + [ANY_SPEC] * nd,
        out_specs=(SEM_SPEC, SEM_SPEC, *[HBM_SPEC] * (ns + nl), pl.BlockSpec(memory_space=pltpu.VMEM)),
        out_shape=(pltpu.SemaphoreType.DMA((ncopy,)), pltpu.SemaphoreType.DMA((ncopy,)), *thru,
                   jax.ShapeDtypeStruct((8, LANE), F32)),
        input_output_aliases={i: 2 + i for i in range(ns + nl)},
        compiler_params=pltpu.CompilerParams(has_side_effects=EFFECT),
    )(*[hbm(a) for a in srcs], *[hbm(a) for a in lands], *deps)
    return outs[0], outs[1], list(outs[2:2 + ns]), list(outs[2 + ns:2 + ns + nl]), outs[-1]


def exchange_wait(name, plan, started, after, place_own=False):
    send_sems, recv_sems, srcs, lands, _ = started
    ns, nl = len(srcs), len(lands)

    def body(*refs):
        s, l = refs[:ns], refs[ns:ns + nl]
        send_sems, recv_sems = refs[ns + nl], refs[ns + nl + 1]
        l_out = refs[2 * ns + nl + 3:2 * ns + 2 * nl + 3]
        scratch = refs[2 * ns + 2 * nl + 3:]
        copies = [_remote(cp, i, send_sems, recv_sems) for i, cp in enumerate(plan(s, l, _coords()))]
        if place_own:
            me = _linear(*_coords())
            local_sems, bufs = scratch[0], scratch[1:]
            loads = [pltpu.make_async_copy(s[q], bufs[q], local_sems.at[q]) for q in range(ns)]
            for cp in loads:
                cp.start()
            for cp in loads:
                cp.wait()
            stores = [pltpu.make_async_copy(bufs[q], l_out[q].at[me], local_sems.at[q]) for q in range(ns)]
            for cp in stores:
                cp.start()
        for cp in copies:
            cp.wait_recv()
        for cp in copies:
            cp.wait_send()
        if place_own:
            for cp in stores:
                cp.wait()

    scratch_shapes = []
    if place_own:
        scratch_shapes = [pltpu.SemaphoreType.DMA((ns,))] + [pltpu.VMEM(a.shape, a.dtype) for a in srcs]
    outs = pl.pallas_call(
        body, name=name,
        in_specs=[HBM_SPEC] * (ns + nl) + [SEM_SPEC, SEM_SPEC, ANY_SPEC],
        out_specs=[HBM_SPEC] * (ns + nl),
        out_shape=[pltpu.HBM(a.shape, a.dtype) for a in srcs + lands],
        input_output_aliases={i: i for i in range(ns + nl)},
        scratch_shapes=scratch_shapes,
        compiler_params=pltpu.CompilerParams(has_side_effects=EFFECT),
    )(*srcs, *lands, send_sems, recv_sems, after)
    return list(outs[:ns]), list(outs[ns:])


def pair_sum(name, g, recv):
    _, r, c = g.shape
    tr = _rows_tile(r, 512)

    def body(g_ref, r_ref, o_ref):
        own = jnp.where(lax.axis_index("c") == 0, g_ref[0, 0], g_ref[0, 1])
        o_ref[0] = (own.astype(F32) + r_ref[0].astype(F32)).astype(o_ref.dtype)

    return pl.pallas_call(
        body, name=name, grid=(NCHIP, r // tr),
        in_specs=[pl.BlockSpec((1, 2, tr, c), lambda k, i: (k, 0, i, 0)), pl.BlockSpec((1, tr, c), lambda k, i: (k, i, 0))],
        out_specs=pl.BlockSpec((1, tr, c), lambda k, i: (k, i, 0)),
        out_shape=jax.ShapeDtypeStruct((NCHIP, r, c), g.dtype), compiler_params=_params(("parallel", "parallel")),
    )(g.reshape(NCHIP, 2, r, c), recv)


def chip_sum(name, partial, recv):
    _, r, c = partial.shape
    tr = _rows_tile(r, 512)

    def body(p_ref, r_ref, o_ref):
        chip = 2 * lax.axis_index("x") + lax.axis_index("y")
        own = p_ref[0]
        for k in range(1, NCHIP):
            own = jnp.where(chip == k, p_ref[k], own)
        acc = own.astype(F32)
        for k in range(NCHIP - 1):
            acc = acc + r_ref[k].astype(F32)
        o_ref[...] = acc

    return pl.pallas_call(
        body, name=name, grid=(r // tr,),
        in_specs=[pl.BlockSpec((NCHIP, tr, c), lambda i: (0, i, 0)), pl.BlockSpec((NCHIP - 1, tr, c), lambda i: (0, i, 0))],
        out_specs=pl.BlockSpec((tr, c), lambda i: (i, 0)),
        out_shape=jax.ShapeDtypeStruct((r, c), F32), compiler_params=_params(("parallel",)),
    )(partial, recv)


def _block_diag(w, rows_per, cols_per):
    w = w.reshape(NBLK, 8, rows_per, cols_per)
    eye = jnp.eye(8, dtype=w.dtype)
    out = w[:, :, :, None, :] * eye[None, :, None, :, None]
    return out.reshape(NBLK, 8 * rows_per, 8 * cols_per)


def _diag_blocks(wd, rows_per, cols_per):
    wd = wd.reshape(NBLK, 8, rows_per, 8, cols_per)
    idx = jnp.arange(8)
    return wd[:, idx, :, idx, :].transpose(1, 0, 2, 3).reshape(NG, rows_per, cols_per)


def _pad_rows(v, mult):
    n = v.shape[0]
    return jnp.pad(v, (0, (-n) % mult))


def kernel(x, c, w_ada, b_ada, norm1_g, w_in, conv_w, conv_b, conv_ln_g, conv_ln_b, conv_proj, ssm_a_re, ssm_a_im, ssm_b_re, ssm_b_im, ssm_c_re, ssm_c_im, ssm_d, ssm_log_dt, ssm_glu, w_out, norm2_g, w_ffn_in, w_ffn_out, final_g, loss_target, m_w_ada, m_b_ada, m_norm1_g, m_w_in, m_conv_w, m_conv_b, m_conv_ln_g, m_conv_ln_b, m_conv_proj, m_ssm_a_re, m_ssm_a_im, m_ssm_b_re, m_ssm_b_im, m_ssm_c_re, m_ssm_c_im, m_ssm_d, m_ssm_log_dt, m_ssm_glu, m_w_out, m_norm2_g, m_w_ffn_in, m_w_ffn_out, m_final_g, v_w_ada, v_b_ada, v_norm1_g, v_w_in, v_conv_w, v_conv_b, v_conv_ln_g, v_conv_ln_b, v_conv_proj, v_ssm_a_re, v_ssm_a_im, v_ssm_b_re, v_ssm_b_im, v_ssm_c_re, v_ssm_c_im, v_ssm_d, v_ssm_log_dt, v_ssm_glu, v_w_out, v_norm2_g, v_w_ffn_in, v_w_ffn_out, v_final_g):
    me = _linear(*_coords())
    xs = x[0]
    tgt = loss_target[0]
    seq = xs.shape[0]

    flat = lambda g: g.reshape(NDEV * g.shape[1], g.shape[2])
    c_all, cw_g, w_in_g = all_gather("gather_first", [c, conv_w[0], w_in[0].T.astype(BF16)])
    w_in_t = flat(w_in_g)
    mids = [p.astype(BF16) for p in (conv_proj[0].T, ssm_glu[0].T, w_out[0])]
    ffns = [p.astype(BF16) for p in (w_ffn_in[0].T, w_ffn_out[0])]
    zone = lambda p: ((NDEV,) + p.shape, p.dtype)
    mids_go = exchange_start("gather_mid_start", gather_plan, 7 * len(mids), mids, [zone(p) for p in mids],
                             deps=[c_all])
    ffns_go = exchange_start("gather_ffn_start", gather_plan, 7 * len(ffns), ffns, [zone(p) for p in ffns],
                             deps=[mids_go[4]])

    ncol = w_ada.shape[2]
    c_all = c_all.reshape(NDEV, D)
    b_cols = lax.dynamic_slice_in_dim(b_ada, me * ncol, ncol, axis=1)
    act_all, mod_cols = mod_fwd(c_all, w_ada[0], b_cols)
    (mod_all,) = all_gather("gather_mod", [mod_cols])
    mod = lax.dynamic_index_in_dim(mod_all, me, axis=1, keepdims=False).reshape(NMOD, D)
    sh1, sc1, g1, sh2, sc2, g2 = [mod[q:q + 1] for q in range(NMOD)]

    expand = jnp.repeat(jnp.eye(NG, dtype=F32), NP, axis=0)
    a_re_c, a_im_c = ssm_a_re.reshape(NST, 1), ssm_a_im.reshape(NST, 1)
    ldt_c = ssm_log_dt.reshape(NG, 1)
    b_re_r, b_im_r = ssm_b_re.reshape(NST, GH), ssm_b_im.reshape(NST, GH)
    e_re, e_im, bb_re, bb_im = disc_fwd(a_re_c, a_im_c, ldt_c, b_re_r, b_im_r, expand)
    e_re_b, e_im_b = e_re.reshape(NBLK, 1, SB), e_im.reshape(NBLK, 1, SB)
    bb_re_g, bb_im_g = bb_re.reshape(NG, NP, GH), bb_im.reshape(NG, NP, GH)
    wbt_re = _block_diag(bb_re_g, NP, GH)
    wbt_im = _block_diag(bb_im_g, NP, GH)
    wb_re, wb_im = wbt_re.transpose(0, 2, 1), wbt_im.transpose(0, 2, 1)
    wct = jnp.concatenate([_block_diag(ssm_c_re[0], GH, NP), -_block_diag(ssm_c_im[0], GH, NP)], axis=2)
    wc = wct.transpose(0, 2, 1)
    to_b = lambda a: a.astype(BF16)
    dvec = ssm_d.reshape(NBLK, 1, CB)

    n1g = norm1_g

    def f_norm1(xv, g, sc, sh):
        _, xh = _rms_stats(xv)
        return [xh * g * (1.0 + sc) + sh], []

    (h1,) = rowwise("norm1", f_norm1, [xs], [n1g, sc1, sh1], [(D, BF16)], [], 512, deps=[ffns_go[4]])
    z = mm("mm_in", h1, w_in_t, "nt", tiles=(1024, CW, 1024), b_rot=Z_ROT)

    conv_w_full = cw_g.transpose(1, 0, 2).reshape(KC, CW)
    w32 = jnp.pad(conv_w_full, ((0, HALO - KC), (0, 0)))
    yc, s_act = conv_fwd(z, w32, conv_b, conv_ln_g, conv_ln_b)
    conv_proj_t, ssm_glu_t, w_out_f = [
        flat(g) for g in exchange_wait("gather_mid_wait", gather_plan, mids_go, s_act, place_own=True)[1]]
    y_conv = mm("mm_conv_proj", s_act, conv_proj_t, "nt")

    xs_re, xs_im, ypre, gl = ssm_fwd(z, to_b(wb_re), to_b(wb_im), to_b(wc), e_re_b, e_im_b, dvec)
    z2 = mm("mm_ssm_glu", gl, ssm_glu_t, "nt")

    def f_merge(yc_v, za, zb, glc, gls):
        return [_sig(glc) * yc_v + _sig(gls) * (za * _sig(zb))], []

    (merged,) = rowwise("merge", f_merge, [y_conv, (z2, D, 0), (z2, D, 1), (z, D, 0), (z, D, 1)],
                        [], [(D, BF16)], [], 512)
    o1 = mm("mm_out", merged, w_out_f, "nn")

    def f_norm2(xv, o1v, g1v, g, sc, sh):
        x1v = xv + g1v * o1v
        _, xh = _rms_stats(x1v)
        return [x1v, xh * g * (1.0 + sc) + sh], []

    x1, h2 = rowwise("norm2", f_norm2, [xs, o1], [g1, norm2_g, sc2, sh2], [(D, F32), (D, BF16)], [], 512)
    w_ffn_in_t, w_ffn_out_f = [
        flat(g) for g in exchange_wait("gather_ffn_wait", gather_plan, ffns_go, h2, place_own=True)[1]]
    f = mm("mm_ffn_in", h2, w_ffn_in_t, "nt")

    def f_swiglu(fg, fu):
        return [fg * _sig(fg) * fu], []

    (act,) = rowwise("swiglu", f_swiglu, [(f, FH, 0), (f, FH, 1)], [], [(FH, BF16)], [], 256)
    o2 = mm("mm_ffn_out", act, w_ffn_out_f, "nn")

    fg_row = final_g.reshape(1, D)

    def f_final(x1v, o2v, tv, g2v, fg):
        x2v = x1v + g2v * o2v
        r, xh = _rms_stats(x2v)
        yv = xh * fg
        err = yv - tv
        loss = jnp.sum(_colsum(err * err), axis=1, keepdims=True) * (0.5 / D)
        dy = err * (1.0 / D)
        dx2 = _rms_bwd(dy * fg, xh, r)
        return ([dx2, g2v * dx2],
                [jnp.broadcast_to(loss, (1, LANE)), _colsum(dy * xh), _colsum(dx2 * o2v)])

    dx2, do2, loss_l, d_final_g, d_g2 = rowwise(
        "final", f_final, [x1, o2, tgt], [g2, fg_row], [(D, F32), (D, BF16)], [LANE, D, D], 256)

    dact = mm("mm_dact", do2, w_ffn_out_f, "nt")
    g_ffn_out = mm("mm_g_ffn_out", act, do2, "tn", BF16)

    def f_dswiglu(fg, fu, da):
        sg = _sig(fg)
        return [jnp.concatenate([da * fu * (sg * (1.0 + fg * (1.0 - sg))), da * (fg * sg)], axis=1)], []

    (df,) = rowwise("dswiglu", f_dswiglu, [(f, FH, 0), (f, FH, 1), dact], [], [(2 * FH, BF16)], [], 256)
    dh2 = mm("mm_dh2", df, w_ffn_in_t, "nn")
    g_ffn_in_t = mm("mm_g_ffn_in", df, h2, "tn", BF16)

    def pair_go(tag, grads_t, deps=()):
        srcs = [g.reshape(NDEV, -1, D) for g in grads_t]
        return exchange_start("pair_" + tag + "_start", pair_plan, NCHIP * len(srcs), srcs,
                              [((NCHIP,) + s.shape[1:], s.dtype) for s in srcs], deps)

    def chip_go(tag, names, pair_started, after):
        own, from_sibling = exchange_wait("pair_" + tag + "_wait", pair_plan, pair_started, after)
        partials = [pair_sum("pair_sum_" + n, g, r) for n, g, r in zip(names, own, from_sibling)]
        return exchange_start("chip_" + tag + "_start", chip_plan, (NCHIP - 1) * len(partials), partials,
                              [((NCHIP - 1,) + p.shape[1:], p.dtype) for p in partials])

    def chip_done(tag, names, chip_started, after):
        partials, from_chips = exchange_wait("chip_" + tag + "_wait", chip_plan, chip_started, after)
        return [chip_sum("chip_sum_" + n, p, r) for n, p, r in zip(names, partials, from_chips)]

    pair_ffn = pair_go("ffn", [g_ffn_out, g_ffn_in_t])

    def f_dnorm2(dh, x1v, dx2v, o1v, g, sc, g1v):
        r, xh = _rms_stats(x1v)
        dxh = dh * (1.0 + sc) * g
        dx1 = dx2v + _rms_bwd(dxh, xh, r)
        return ([dx1, g1v * dx1],
                [_colsum(dh * xh * g), _colsum(dh), _colsum(dh * (1.0 + sc) * xh), _colsum(dx1 * o1v)])

    dx1, do1, d_sc2, d_sh2, d_n2g, d_g1 = rowwise(
        "dnorm2", f_dnorm2, [dh2, x1, dx2, o1], [norm2_g, sc2, g1], [(D, F32), (D, BF16)], [D, D, D, D], 256,
        deps=[pair_ffn[4]])

    dmerged = mm("mm_dmerged", do1, w_out_f, "nt")
    g_out = mm("mm_g_out", merged, do1, "tn", BF16)
    chip_ffn = chip_go("ffn", ("w_ffn_out", "w_ffn_in"), pair_ffn, g_out)

    def f_dmerge(dm, yc_v, za, zb, glc, gls):
        sc_ = _sig(glc)
        ss_ = _sig(gls)
        sb_ = _sig(zb)
        dys = dm * ss_
        dz2 = jnp.concatenate([dys * sb_, dys * za * sb_ * (1.0 - sb_)], axis=1)
        dgl = jnp.concatenate([dm * yc_v * sc_ * (1.0 - sc_), dm * (za * sb_) * ss_ * (1.0 - ss_)], axis=1)
        return [dm * sc_, dz2, dgl], []

    dyconv, dz2, dz = rowwise(
        "dmerge", f_dmerge, [dmerged, y_conv, (z2, D, 0), (z2, D, 1), (z, D, 0), (z, D, 1)],
        [], [(D, BF16), (2 * D, BF16), (2 * D, BF16, ZW, 0)], [], 256, deps=[chip_ffn[4]])

    g_conv_proj_t = mm("mm_g_conv_proj", dyconv, s_act, "tn", BF16)
    dgl = mm("mm_dgl", dz2, ssm_glu_t, "nn")
    g_ssm_glu_t = mm("mm_g_ssm_glu", dz2, gl, "tn", BF16)
    pair_mid = pair_go("mid", [g_out, g_conv_proj_t, g_ssm_glu_t])
    ds = mm("mm_ds", dyconv, conv_proj_t, "nn", deps=[pair_mid[4]])
    dz, d_lng, d_lnb, d_cb, d_cw32 = conv_bwd(ds, yc, z, w32, conv_ln_g, conv_ln_b, dz)
    dz, d_d, d_ar, d_ai, d_wb_re, d_wb_im, d_wc = ssm_bwd(
        dgl, ypre, z, xs_re, xs_im, to_b(wbt_re), to_b(wbt_im), to_b(wct), e_re_b, e_im_b, dvec, dz)
    chip_mid = chip_go("mid", ("w_out", "conv_proj", "ssm_glu"), pair_mid, dz)

    dh1 = mm("mm_dh1", dz, w_in_t, "nn", tiles=(1024, 1024, CW), b_rot=Z_ROT, deps=[chip_mid[4]])
    g_in_t = mm("mm_g_in", dz, h1, "tn", BF16, tiles=(CW, 1024, 1024), o_rot=Z_ROT)
    pair_in = pair_go("in", [g_in_t])

    def f_dnorm1(dh, xv, dx1v, g, sc):
        r, xh = _rms_stats(xv)
        dxh = dh * (1.0 + sc) * g
        return ([dx1v + _rms_bwd(dxh, xh, r)],
                [_colsum(dh * xh * g), _colsum(dh), _colsum(dh * (1.0 + sc) * xh)])

    grad_x, d_sc1, d_sh1, d_n1g = rowwise(
        "dnorm1", f_dnorm1, [dh1, xs, dx1], [n1g, sc1], [(D, F32)], [D, D, D], 256, deps=[pair_in[4]])
    chip_in = chip_go("in", ("w_in",), pair_in, grad_x)

    weights = {
        "w_ada": (w_ada, m_w_ada, v_w_ada), "b_ada": (b_ada, m_b_ada, v_b_ada), "norm1_g": (norm1_g, m_norm1_g, v_norm1_g),
        "w_in": (w_in, m_w_in, v_w_in), "conv_w": (conv_w, m_conv_w, v_conv_w), "conv_b": (conv_b, m_conv_b, v_conv_b),
        "conv_ln_g": (conv_ln_g, m_conv_ln_g, v_conv_ln_g), "conv_ln_b": (conv_ln_b, m_conv_ln_b, v_conv_ln_b),
        "conv_proj": (conv_proj, m_conv_proj, v_conv_proj), "ssm_a_re": (ssm_a_re, m_ssm_a_re, v_ssm_a_re),
        "ssm_a_im": (ssm_a_im, m_ssm_a_im, v_ssm_a_im), "ssm_b_re": (ssm_b_re, m_ssm_b_re, v_ssm_b_re),
        "ssm_b_im": (ssm_b_im, m_ssm_b_im, v_ssm_b_im), "ssm_c_re": (ssm_c_re, m_ssm_c_re, v_ssm_c_re),
        "ssm_c_im": (ssm_c_im, m_ssm_c_im, v_ssm_c_im), "ssm_d": (ssm_d, m_ssm_d, v_ssm_d),
        "ssm_log_dt": (ssm_log_dt, m_ssm_log_dt, v_ssm_log_dt), "ssm_glu": (ssm_glu, m_ssm_glu, v_ssm_glu),
        "w_out": (w_out, m_w_out, v_w_out), "norm2_g": (norm2_g, m_norm2_g, v_norm2_g),
        "w_ffn_in": (w_ffn_in, m_w_ffn_in, v_w_ffn_in), "w_ffn_out": (w_ffn_out, m_w_ffn_out, v_w_ffn_out),
        "final_g": (final_g, m_final_g, v_final_g),
    }
    order = list(weights)
    big = ("w_ada", "w_in", "conv_proj", "ssm_glu", "w_out", "w_ffn_in", "w_ffn_out")
    grads, delta, new_m, new_v = {}, {}, {}, {}

    def adam_big(n, g2d):
        wv, mv, vv = weights[n]
        shp = wv.shape
        d_, m_, v_ = adam("adam_" + n, wv.reshape(shp[-2:]), g2d, mv.reshape(shp[-2:]), vv.reshape(shp[-2:]))
        grads[n], delta[n], new_m[n], new_v[n] = g2d.reshape(shp), d_.reshape(shp), m_.reshape(shp), v_.reshape(shp)
        return d_

    gs_ffn_out, gs_ffn_in = chip_done("ffn", ("w_ffn_out", "w_ffn_in"), chip_ffn, chip_in[4])
    adam_big("w_ffn_out", gs_ffn_out)
    last = adam_big("w_ffn_in", gs_ffn_in.T)
    gs_out, gs_conv_proj, gs_ssm_glu = chip_done("mid", ("w_out", "conv_proj", "ssm_glu"), chip_mid, last)
    adam_big("w_out", gs_out)
    adam_big("conv_proj", gs_conv_proj.reshape(-1, CW).T)
    adam_big("ssm_glu", gs_ssm_glu.reshape(-1, SW).T)

    d_bb_re = _diag_blocks(d_wb_re.transpose(0, 2, 1), NP, GH).reshape(NST, GH)
    d_bb_im = _diag_blocks(d_wb_im.transpose(0, 2, 1), NP, GH).reshape(NST, GH)
    d_wct = d_wc.transpose(0, 2, 1)
    d_c_re = _diag_blocks(d_wct[:, :, 0:SB], GH, NP)
    d_c_im = -_diag_blocks(d_wct[:, :, SB:2 * SB], GH, NP)
    d_a_re, d_a_im, d_ldt, d_b_re, d_b_im = disc_bwd(
        a_re_c, a_im_c, ldt_c, b_re_r, b_im_r, expand, d_ar.reshape(NST, 1), d_ai.reshape(NST, 1), d_bb_re, d_bb_im)

    dmod = jnp.concatenate([d_sh1, d_sc1, d_g1, d_sh2, d_sc2, d_g2], axis=1)
    small_local = [dmod.reshape(-1), d_n1g.reshape(-1), d_cw32[0:KC].reshape(-1), d_cb.reshape(-1), d_lng.reshape(-1),
                   d_lnb.reshape(-1), d_a_re.reshape(-1), d_a_im.reshape(-1), d_b_re.reshape(-1), d_b_im.reshape(-1),
                   d_c_re.reshape(-1), d_c_im.reshape(-1), d_d.reshape(-1), d_ldt.reshape(-1), d_n2g.reshape(-1),
                   d_final_g.reshape(-1), loss_l[0, 0:1]]
    small_sizes = [v.shape[0] for v in small_local]
    packed = _pad_rows(jnp.concatenate(small_local), 256 * LANE).reshape(-1, LANE)
    (small_all,) = all_gather("gather_small", [packed])
    small_sum = sum_slots("sum_small", small_all).reshape(-1)
    pieces, pos = [], 0
    for n in small_sizes:
        pieces.append(small_sum[pos:pos + n])
        pos += n
    (g_b_ada, g_n1g, g_cw_full, g_cb, g_lng, g_lnb, g_a_re, g_a_im, g_b_re, g_b_im, g_c_re, g_c_im, g_d, g_ldt,
     g_n2g, g_fg, loss_sum) = pieces
    loss = loss_sum[0]
    dmod_all = small_all.reshape(NDEV, -1)[:, 0:NMOD * D]
    g_w_ada = ada_grad(act_all, lax.dynamic_slice_in_dim(dmod_all, me * ncol, ncol, axis=1))
    ccol = conv_w.shape[2]
    g_conv_w = lax.dynamic_slice_in_dim(g_cw_full.reshape(KC, CW), me * ccol, ccol, axis=1)

    adam_big("w_ada", g_w_ada)
    grads.update({
        "b_ada": g_b_ada.reshape(b_ada.shape), "norm1_g": g_n1g.reshape(norm1_g.shape),
        "conv_w": g_conv_w[None], "conv_b": g_cb.reshape(conv_b.shape),
        "conv_ln_g": g_lng.reshape(conv_ln_g.shape), "conv_ln_b": g_lnb.reshape(conv_ln_b.shape),
        "ssm_a_re": g_a_re.reshape(ssm_a_re.shape),
        "ssm_a_im": g_a_im.reshape(ssm_a_im.shape), "ssm_b_re": g_b_re.reshape(ssm_b_re.shape),
        "ssm_b_im": g_b_im.reshape(ssm_b_im.shape), "ssm_c_re": g_c_re.reshape(ssm_c_re.shape),
        "ssm_c_im": g_c_im.reshape(ssm_c_im.shape), "ssm_d": g_d.reshape(ssm_d.shape),
        "ssm_log_dt": g_ldt.reshape(ssm_log_dt.shape),
        "norm2_g": g_n2g.reshape(norm2_g.shape),
        "final_g": g_fg.reshape(final_g.shape),
    })
    small = [n for n in order if n not in big]
    rows = lambda a: a.reshape(1, -1) if a.ndim == 1 else a
    small_out = adam_many("adam_small", [rows(weights[n][0]) for n in small], [rows(grads[n]) for n in small],
                          [rows(weights[n][1]) for n in small], [rows(weights[n][2]) for n in small])
    for q, n in enumerate(small):
        shp = weights[n][0].shape
        delta[n], new_m[n], new_v[n] = [small_out[t * len(small) + q].reshape(shp) for t in range(3)]

    (gs_in,) = chip_done("in", ("w_in",), chip_in, small_out[0])
    adam_big("w_in", gs_in.T)

    return (loss, grad_x[None], *[grads[n] for n in order], *[delta[n] for n in order],
            *[new_m[n] for n in order], *[new_v[n] for n in order])
```

```python
import functools
import math

import jax
import jax.numpy as jnp
from jax import lax
from jax.experimental import pallas as pl
from jax.experimental.pallas import tpu as pltpu

F32 = jnp.float32
BF16 = jnp.bfloat16

D = 1024
CW = 512
KC = 31
SW = 512
NG = 32
GH = 16
NP = 64
NST = NG * NP
FH = 2816
FFN_BLK = 1408
MRG_BLK = 512
NMOD = 6
NDEV = 8
EPS = 1e-6
CB = 128
SB = 512
NBLK = SW // CB
HALO = 32
ZW = 2 * CW + SW + 2 * D
Z_ROT = lambda j: (j + 3) % (ZW // CW)
ZB_A, ZB_G, ZB_U = 4, 5, 6

ADAM_LR = 0.001
ADAM_B1 = 0.9
ADAM_B2 = 0.999
ADAM_EPS = 1e-08
ADAM_WD = 0.01
ADAM_STEP = 10

V7X_VMEM_BYTES = 64 * 1024 * 1024
VMEM_LIMIT = V7X_VMEM_BYTES - 8 * 1024 * 1024
LANE = 128
MESH = pl.DeviceIdType.MESH
ANY_SPEC = pl.BlockSpec(memory_space=pl.ANY)


def _params(sem=None, **kw):
    if sem is not None:
        kw["dimension_semantics"] = sem
    return pltpu.CompilerParams(vmem_limit_bytes=VMEM_LIMIT, **kw)


def _tile(n, most):
    best = None
    for t in range(LANE, most + 1, LANE):
        if n % t == 0:
            best = t
    if best is None:
        raise ValueError(f"no tile for {n}")
    return best


def _sig(x):
    return jax.nn.sigmoid(x)


def mm(name, a, b, mode, out_dtype=F32, tiles=None, b_rot=None, o_rot=None, deps=()):
    if mode == "nn":
        (m, k), (k2, n) = a.shape, b.shape
    elif mode == "nt":
        (m, k), (n, k2) = a.shape, b.shape
    else:
        (k, m), (k2, n) = a.shape, b.shape
    assert k == k2, (name, a.shape, b.shape)
    bm, bn, bk = tiles or (_tile(m, 1024), _tile(n, 1408), _tile(k, 1408 if k % 1408 == 0 else 1024))
    bm, bn, bk = min(bm, m), min(bn, n), min(bk, k)
    assert m % bm == 0 and n % bn == 0 and k % bk == 0, (name, m, n, k, bm, bn, bk)
    nk = k // bk
    rot = lambda idx, r: idx if r is None else r(idx)
    if mode == "nn":
        a_spec = pl.BlockSpec((bm, bk), lambda i, j, kk: (i, kk))
        b_spec = pl.BlockSpec((bk, bn), lambda i, j, kk: (rot(kk, b_rot), j))
        dims = (((1,), (0,)), ((), ()))
    elif mode == "nt":
        a_spec = pl.BlockSpec((bm, bk), lambda i, j, kk: (i, kk))
        b_spec = pl.BlockSpec((bn, bk), lambda i, j, kk: (rot(j, b_rot), kk))
        dims = (((1,), (1,)), ((), ()))
    else:
        assert b_rot is None
        a_spec = pl.BlockSpec((bk, bm), lambda i, j, kk: (kk, i))
        b_spec = pl.BlockSpec((bk, bn), lambda i, j, kk: (kk, j))
        dims = (((0,), (0,)), ((), ()))

    def body(a_ref, b_ref, *rest):
        o_ref, acc_ref = rest[-2:]
        kk = pl.program_id(2)

        @pl.when(kk == 0)
        def _():
            acc_ref[...] = jnp.zeros_like(acc_ref)

        acc_ref[...] += lax.dot_general(a_ref[...], b_ref[...], dims, preferred_element_type=F32)

        @pl.when(kk == nk - 1)
        def _():
            o_ref[...] = acc_ref[...].astype(o_ref.dtype)

    return pl.pallas_call(
        body, name=name,
        grid=(m // bm, n // bn, nk),
        in_specs=[a_spec, b_spec] + [ANY_SPEC] * len(deps),
        out_specs=pl.BlockSpec((bm, bn), lambda i, j, kk: (rot(i, o_rot), j)),
        out_shape=jax.ShapeDtypeStruct((m, n), out_dtype),
        scratch_shapes=[pltpu.VMEM((bm, bn), F32)],
        compiler_params=_params(("parallel", "parallel", "arbitrary")),
    )(a, b, *deps)


def mm_ep(name, a, b, n_acc, acc_block, epilogue, extras, outs, tiles, deps=()):
    m, k = a.shape
    bm, bn, bk = tiles
    nj = outs[0][0] // (outs[0][2] * bn)
    nk = k // bk
    assert m % bm == 0 and k % bk == 0 and b.shape[1] == k, (name, a.shape, b.shape, tiles)
    ne, no, nd = len(extras), len(outs), len(deps)
    dims = (((1,), (1,)), ((), ()))

    def body(*refs):
        a_ref, b_refs = refs[0], refs[1:1 + n_acc]
        e_refs = refs[1 + n_acc:1 + n_acc + ne]
        o_refs = refs[1 + n_acc + ne + nd:1 + n_acc + ne + nd + no]
        acc_refs = refs[1 + n_acc + ne + nd + no:]
        av = a_ref[...]
        prods = [lax.dot_general(av, b_ref[...], dims, preferred_element_type=F32) for b_ref in b_refs]

        def finish(accs):
            for o_ref, v in zip(o_refs, epilogue(accs, *[e[...] for e in e_refs])):
                o_ref[...] = v.astype(o_ref.dtype)

        if nk == 1:
            finish(prods)
        else:
            kk = pl.program_id(2)

            @pl.when(kk == 0)
            def _():
                for acc_ref in acc_refs:
                    acc_ref[...] = jnp.zeros_like(acc_ref)

            for acc_ref, p in zip(acc_refs, prods):
                acc_ref[...] += p

            @pl.when(kk == nk - 1)
            def _():
                finish([acc_ref[...] for acc_ref in acc_refs])

    in_specs = [pl.BlockSpec((bm, bk), lambda i, j, kk: (i, kk))]
    in_specs += [pl.BlockSpec((bn, bk), functools.partial(lambda i, j, kk, q: (acc_block(j, q), kk), q=q))
                 for q in range(n_acc)]
    in_specs += [pl.BlockSpec((bm, w * bn), functools.partial(lambda i, j, kk, off: (i, j + off), off=off))
                 for (_, w, off) in extras]
    in_specs += [ANY_SPEC] * nd
    return pl.pallas_call(
        body, name=name, grid=(m // bm, nj, nk),
        in_specs=in_specs,
        out_specs=[pl.BlockSpec((bm, w * bn), lambda i, j, kk: (i, j)) for (_, _, w) in outs],
        out_shape=[jax.ShapeDtypeStruct((m, cols), dt) for (cols, dt, _) in outs],
        scratch_shapes=[pltpu.VMEM((bm, bn), F32)] * (n_acc if nk > 1 else 0),
        compiler_params=_params(("parallel", "parallel", "arbitrary")),
    )(a, *[b] * n_acc, *[e[0] for e in extras], *deps)


def rowwise(name, fn, rows, consts, out_rows, out_sums, ts, alias=None, deps=()):
    rows = [r if isinstance(r, tuple) else (r, r.shape[1], 0) for r in rows]
    out_rows = [o if len(o) == 4 else (o[0], o[1], o[0], 0) for o in out_rows]
    s = rows[0][0].shape[0]
    nt = s // ts
    nr, nc, no, ns = len(rows), len(consts), len(out_rows), len(out_sums)
    in_specs = [pl.BlockSpec((ts, w), functools.partial(lambda i, cb: (i, cb), cb=cb)) for (_, w, cb) in rows]
    in_specs += [pl.BlockSpec(c.shape, lambda i: (0, 0)) for c in consts]
    operands = [r[0] for r in rows] + list(consts)
    aliases = {}
    if alias is not None:
        in_specs.append(pl.BlockSpec(memory_space=pl.ANY))
        operands.append(alias[0])
        aliases = {nr + nc: alias[1]}
    in_specs += [ANY_SPEC] * len(deps)
    operands += list(deps)
    out_shape = [jax.ShapeDtypeStruct((s, tw), dt) for (_, dt, tw, _) in out_rows]
    out_shape += [jax.ShapeDtypeStruct((1, w), F32) for w in out_sums]
    out_specs = [pl.BlockSpec((ts, w), functools.partial(lambda i, cb: (i, cb), cb=cb)) for (w, _, _, cb) in out_rows]
    out_specs += [pl.BlockSpec((1, w), lambda i: (0, 0)) for w in out_sums]
    n_in = len(operands)

    def body(*refs):
        ins, outs = refs[:nr + nc], refs[n_in:]
        i = pl.program_id(0)
        ro, so = fn(*[r[...] for r in ins])
        for q in range(no):
            outs[q][...] = ro[q].astype(outs[q].dtype)
        if ns:
            @pl.when(i == 0)
            def _():
                for q in range(ns):
                    outs[no + q][...] = jnp.zeros_like(outs[no + q])

            for q in range(ns):
                outs[no + q][...] += so[q]

    return pl.pallas_call(
        body, name=name, grid=(nt,),
        in_specs=in_specs, out_specs=out_specs, out_shape=out_shape, input_output_aliases=aliases,
        compiler_params=_params(("arbitrary",) if ns else ("parallel",)),
    )(*operands)


def _colsum(v):
    return jnp.sum(v, axis=0, keepdims=True)


def _rms_stats(xv):
    r = lax.rsqrt(jnp.mean(xv * xv, axis=-1, keepdims=True) + EPS)
    return r, xv * r


def _rms_bwd(dxhat, xhat, r):
    return r * (dxhat - xhat * jnp.mean(dxhat * xhat, axis=-1, keepdims=True))


def _gelu(v):
    k = math.sqrt(2.0 / math.pi)
    t = jnp.tanh(k * (v + 0.044715 * v * v * v))
    return 0.5 * v * (1.0 + t), t


def _gelu_grad(v, t):
    k = math.sqrt(2.0 / math.pi)
    return 0.5 * (1.0 + t) + 0.5 * v * (1.0 - t * t) * k * (1.0 + 3.0 * 0.044715 * v * v)


CONV_TS = 256
CONV_CH = 64


def _ln_fwd(yc, g, b):
    mu = jnp.mean(yc, axis=-1, keepdims=True)
    xc = yc - mu
    rstd = lax.rsqrt(jnp.mean(xc * xc, axis=-1, keepdims=True) + EPS)
    nhat = xc * rstd
    return nhat, rstd, nhat * g + b


SUBL = 8


def _shifted_copies(buf, sh, ts):
    for b in range(1, SUBL):
        sh[b - 1] = buf[pl.ds(b, ts + HALO - SUBL), :]


def _shifted(buf, sh, start):
    b = start % SUBL
    if b == 0:
        return buf[pl.ds(start, CONV_CH), :]
    return sh[b - 1, pl.ds(start - b, CONV_CH), :]


def conv_fwd(z, w32, cb, lg, lb):
    s = z.shape[0]
    ts = CONV_TS
    nt = s // ts
    hb = ts // HALO

    def body(a_ref, g_ref, ah_ref, gh_ref, w_ref, cb_ref, lg_ref, lb_ref, yc_ref, s_ref, ubuf, ush):
        i = pl.program_id(0)
        first = (i > 0).astype(F32)
        ubuf[0:HALO, :] = ah_ref[...] * _sig(gh_ref[...]) * first
        ubuf[HALO:HALO + ts, :] = a_ref[...] * _sig(g_ref[...])
        _shifted_copies(ubuf, ush, ts)
        for c0 in range(0, ts, CONV_CH):
            acc = jnp.zeros((CONV_CH, CW), F32)
            for k in range(KC):
                acc = acc + w_ref[k:k + 1, :] * _shifted(ubuf, ush, c0 + k + 2)
            yc = acc + cb_ref[...]
            yc_ref[c0:c0 + CONV_CH, :] = yc
            _, _, ln = _ln_fwd(yc, lg_ref[...], lb_ref[...])
            s_ref[c0:c0 + CONV_CH, :] = (ln * _sig(ln)).astype(s_ref.dtype)

    cur = lambda cbk: pl.BlockSpec((ts, CW), functools.partial(lambda i, q: (i, q), q=cbk))
    prev = lambda cbk: pl.BlockSpec((HALO, CW), functools.partial(lambda i, q: (jnp.maximum(i * hb - 1, 0), q), q=cbk))
    const = lambda a: pl.BlockSpec(a.shape, lambda i: (0, 0))
    return pl.pallas_call(
        body, name="conv_fwd", grid=(nt,),
        in_specs=[cur(ZB_A), cur(ZB_G), prev(ZB_A), prev(ZB_G), const(w32), const(cb), const(lg), const(lb)],
        out_specs=[pl.BlockSpec((ts, CW), lambda i: (i, 0)), pl.BlockSpec((ts, CW), lambda i: (i, 0))],
        out_shape=[jax.ShapeDtypeStruct((s, CW), F32), jax.ShapeDtypeStruct((s, CW), BF16)],
        scratch_shapes=[pltpu.VMEM((HALO + ts, CW), F32), pltpu.VMEM((SUBL - 1, ts + HALO - SUBL, CW), F32)],
        compiler_params=_params(("parallel",)),
    )(z, z, z, z, w32, cb, lg, lb)


def conv_bwd(ds, yc, z, w32, lg, lb, dz):
    s = z.shape[0]
    ts = CONV_TS
    nt = s // ts
    hb = ts // HALO
    last_hb = s // HALO - 1

    def ln_bwd(dsv, ycv, g, b):
        nhat, rstd, ln = _ln_fwd(ycv, g, b)
        sg = _sig(ln)
        dln = dsv * (sg * (1.0 + ln * (1.0 - sg)))
        dnh = dln * g
        dyc = rstd * (dnh - jnp.mean(dnh, axis=-1, keepdims=True)
                      - nhat * jnp.mean(dnh * nhat, axis=-1, keepdims=True))
        return dyc, dln, nhat

    def body(ds_ref, yc_ref, dsn_ref, ycn_ref, a_ref, g_ref, ah_ref, gh_ref, w_ref, lg_ref, lb_ref, dz_in,
             dz_ref, dlg_ref, dlb_ref, dcb_ref, dw_ref, dbuf, ubuf, dsh, ush):
        i = pl.program_id(0)

        @pl.when(i == 0)
        def _():
            dlg_ref[...] = jnp.zeros_like(dlg_ref)
            dlb_ref[...] = jnp.zeros_like(dlb_ref)
            dcb_ref[...] = jnp.zeros_like(dcb_ref)
            dw_ref[...] = jnp.zeros_like(dw_ref)

        lg, lb = lg_ref[...], lb_ref[...]
        dyc, dln, nhat = ln_bwd(ds_ref[...], yc_ref[...], lg, lb)
        dlg_ref[...] += _colsum(dln * nhat)
        dlb_ref[...] += _colsum(dln)
        dcb_ref[...] += _colsum(dyc)
        dbuf[0:ts, :] = dyc
        nxt = (i < nt - 1).astype(F32)
        dbuf[ts:ts + HALO, :] = ln_bwd(dsn_ref[...], ycn_ref[...], lg, lb)[0] * nxt
        first = (i > 0).astype(F32)
        ubuf[0:HALO, :] = ah_ref[...] * _sig(gh_ref[...]) * first
        ubuf[HALO:HALO + ts, :] = a_ref[...] * _sig(g_ref[...])
        _shifted_copies(dbuf, dsh, ts)
        _shifted_copies(ubuf, ush, ts)
        for c0 in range(0, ts, CONV_CH):
            du = jnp.zeros((CONV_CH, CW), F32)
            dyc_c = dbuf[c0:c0 + CONV_CH, :]
            for k in range(KC):
                du = du + w_ref[k:k + 1, :] * _shifted(dbuf, dsh, c0 + KC - 1 - k)
                dw_ref[k:k + 1, :] += _colsum(dyc_c * _shifted(ubuf, ush, c0 + k + 2))
            av = a_ref[c0:c0 + CONV_CH, :]
            sg = _sig(g_ref[c0:c0 + CONV_CH, :])
            dz_ref[c0:c0 + CONV_CH, 0:CW] = (du * sg).astype(dz_ref.dtype)
            dz_ref[c0:c0 + CONV_CH, CW:2 * CW] = (du * av * sg * (1.0 - sg)).astype(dz_ref.dtype)

    cur = lambda w, cbk: pl.BlockSpec((ts, w), functools.partial(lambda i, q: (i, q), q=cbk))
    prev = lambda cbk: pl.BlockSpec((HALO, CW), functools.partial(lambda i, q: (jnp.maximum(i * hb - 1, 0), q), q=cbk))
    nxt_spec = pl.BlockSpec((HALO, CW), lambda i: (jnp.minimum((i + 1) * hb, last_hb), 0))
    const = lambda a: pl.BlockSpec(a.shape, lambda i: (0, 0))
    acc = lambda r: pl.BlockSpec((r, CW), lambda i: (0, 0))
    return pl.pallas_call(
        body, name="conv_bwd", grid=(nt,),
        in_specs=[cur(CW, 0), cur(CW, 0), nxt_spec, nxt_spec, cur(CW, ZB_A), cur(CW, ZB_G), prev(ZB_A), prev(ZB_G),
                  const(w32), const(lg), const(lb), pl.BlockSpec(memory_space=pl.ANY)],
        out_specs=[pl.BlockSpec((ts, 2 * CW), lambda i: (i, ZB_A // 2)), acc(1), acc(1), acc(1), acc(HALO)],
        out_shape=[jax.ShapeDtypeStruct(dz.shape, dz.dtype), jax.ShapeDtypeStruct((1, CW), F32),
                   jax.ShapeDtypeStruct((1, CW), F32), jax.ShapeDtypeStruct((1, CW), F32),
                   jax.ShapeDtypeStruct((HALO, CW), F32)],
        scratch_shapes=[pltpu.VMEM((ts + HALO, CW), F32), pltpu.VMEM((HALO + ts, CW), F32)]
        + [pltpu.VMEM((SUBL - 1, ts + HALO - SUBL, CW), F32)] * 2,
        input_output_aliases={11: 0},
        compiler_params=_params(("arbitrary",)),
    )(ds, yc, ds, yc, z, z, z, z, w32, lg, lb, dz)


SSM_TS = 512
GRP = 8


def _cmul(ar, ai, br, bi):
    return ar * br - ai * bi, ar * bi + ai * br


def _scan_tables(ar, ai, reverse):
    n = ar.shape[1]
    row = lax.broadcasted_iota(jnp.int32, (GRP, n), 0)
    dist = (GRP - 1 - row) if reverse else row
    one_r = jnp.broadcast_to(ar, (GRP, n))
    one_i = jnp.broadcast_to(ai, (GRP, n))
    p2r, p2i = _cmul(one_r, one_i, one_r, one_i)
    p4r, p4i = _cmul(p2r, p2i, p2r, p2i)
    steps = []
    for sft, (pr, pi) in ((1, (one_r, one_i)), (2, (p2r, p2i)), (4, (p4r, p4i))):
        keep = dist >= sft
        steps.append((jnp.where(keep, pr, 0.0), jnp.where(keep, pi, 0.0)))
    cr, ci = one_r, one_i
    accr, acci = one_r, one_i
    for e in range(1, GRP):
        cr, ci = _cmul(cr, ci, one_r, one_i)
        accr = jnp.where(dist == e, cr, accr)
        acci = jnp.where(dist == e, ci, acci)
    return steps, (accr, acci)


def _scan_group(xr, xi, steps, carry_tab, cr, ci, reverse):
    for sft, (tr, ti) in zip((1, 2, 4), steps):
        amt = (GRP - sft) if reverse else sft
        sr = pltpu.roll(xr, amt, 0)
        si = pltpu.roll(xi, amt, 0)
        xr, xi = xr + tr * sr - ti * si, xi + tr * si + ti * sr
    pr, pi = carry_tab
    xr = xr + pr * cr - pi * ci
    xi = xi + pr * ci + pi * cr
    return xr, xi


def ssm_fwd(z, wb_re, wb_im, wc, e_re, e_im, dvec):
    s = z.shape[0]
    ts = SSM_TS
    nt = s // ts
    ucol0 = ZB_U * CW // CB

    def body(u_ref, wbr_ref, wbi_ref, wc_ref, er_ref, ei_ref, d_ref, xr_ref, xi_ref, y_ref, gl_ref, car_r, car_i):
        i = pl.program_id(1)

        @pl.when(i == 0)
        def _():
            car_r[...] = jnp.zeros_like(car_r)
            car_i[...] = jnp.zeros_like(car_i)

        u = u_ref[...]
        ub = u.astype(BF16)
        xr_ref[...] = jnp.dot(ub, wbr_ref[0], preferred_element_type=F32)
        xi_ref[...] = jnp.dot(ub, wbi_ref[0], preferred_element_type=F32)
        steps, ctab = _scan_tables(er_ref[0], ei_ref[0], False)

        def grp(r, carry):
            cr, ci = carry
            r0 = pl.multiple_of(r * GRP, GRP)
            xr, xi = _scan_group(xr_ref[pl.ds(r0, GRP), :], xi_ref[pl.ds(r0, GRP), :], steps, ctab, cr, ci, False)
            xr_ref[pl.ds(r0, GRP), :] = xr
            xi_ref[pl.ds(r0, GRP), :] = xi
            return (jnp.broadcast_to(xr[GRP - 1:GRP, :], (GRP, SB)), jnp.broadcast_to(xi[GRP - 1:GRP, :], (GRP, SB)))

        cr, ci = lax.fori_loop(0, ts // GRP, grp, (car_r[...], car_i[...]))
        car_r[...] = cr
        car_i[...] = ci
        y = (jnp.dot(xr_ref[...].astype(BF16), wc_ref[0, 0:SB, :], preferred_element_type=F32)
             + jnp.dot(xi_ref[...].astype(BF16), wc_ref[0, SB:2 * SB, :], preferred_element_type=F32)
             + d_ref[0] * u)
        y_ref[...] = y
        gl_ref[...] = _gelu(y)[0].astype(gl_ref.dtype)

    blk3 = lambda a: pl.BlockSpec((1,) + a.shape[1:], lambda j, i: (j, 0, 0))
    return pl.pallas_call(
        body, name="ssm_fwd", grid=(NBLK, nt),
        in_specs=[pl.BlockSpec((ts, CB), lambda j, i: (i, ucol0 + j)),
                  blk3(wb_re), blk3(wb_im), blk3(wc), blk3(e_re), blk3(e_im), blk3(dvec)],
        out_specs=[pl.BlockSpec((ts, SB), lambda j, i: (i, j)), pl.BlockSpec((ts, SB), lambda j, i: (i, j)),
                   pl.BlockSpec((ts, CB), lambda j, i: (i, j)), pl.BlockSpec((ts, CB), lambda j, i: (i, j))],
        out_shape=[jax.ShapeDtypeStruct((s, NST), F32), jax.ShapeDtypeStruct((s, NST), F32),
                   jax.ShapeDtypeStruct((s, SW), F32), jax.ShapeDtypeStruct((s, SW), BF16)],
        scratch_shapes=[pltpu.VMEM((GRP, SB), F32), pltpu.VMEM((GRP, SB), F32)],
        compiler_params=_params(("parallel", "arbitrary")),
    )(z, wb_re, wb_im, wc, e_re, e_im, dvec)


def ssm_bwd(dgl, ypre, z, xs_re, xs_im, wbt_re, wbt_im, wct, e_re, e_im, dvec, dz):
    s = z.shape[0]
    ts = SSM_TS
    nt = s // ts
    ucol0 = ZB_U * CW // CB
    tn_dims = (((0,), (0,)), ((), ()))

    def body(dgl_ref, y_ref, u_ref, xr_ref, xi_ref, wbtr_ref, wbti_ref, wct_ref, er_ref, ei_ref, d_ref, dz_in,
             du_ref, dd_ref, dar_ref, dai_ref, dwbr_ref, dwbi_ref, dwc_ref,
             lr_ref, li_ref, car_r, car_i, acc_r, acc_i):
        i = pl.program_id(1)

        @pl.when(i == 0)
        def _():
            for ref in (car_r, car_i, acc_r, acc_i, dd_ref, dwbr_ref, dwbi_ref, dwc_ref):
                ref[...] = jnp.zeros_like(ref)

        u = u_ref[...]
        y = y_ref[...]
        dy = dgl_ref[...] * _gelu_grad(y, _gelu(y)[1])
        dd_ref[0] += _colsum(dy * u)
        dyb = dy.astype(BF16)
        dxo = jnp.dot(dyb, wct_ref[0], preferred_element_type=F32)
        lr_ref[...] = dxo[:, 0:SB]
        li_ref[...] = dxo[:, SB:2 * SB]
        steps, ctab = _scan_tables(er_ref[0], -ei_ref[0], True)
        row = lax.broadcasted_iota(jnp.int32, (GRP, SB), 0)

        def grp(q, carry):
            cr, ci, ar, ai = carry
            r0 = pl.multiple_of((ts // GRP - 1 - q) * GRP, GRP)
            lr, li = _scan_group(lr_ref[pl.ds(r0, GRP), :], li_ref[pl.ds(r0, GRP), :], steps, ctab, cr, ci, True)
            lr_ref[pl.ds(r0, GRP), :] = lr
            li_ref[pl.ds(r0, GRP), :] = li
            nr = jnp.where(row == GRP - 1, cr, pltpu.roll(lr, GRP - 1, 0))
            ni = jnp.where(row == GRP - 1, ci, pltpu.roll(li, GRP - 1, 0))
            xr = xr_ref[pl.ds(r0, GRP), :]
            xi = xi_ref[pl.ds(r0, GRP), :]
            ar = ar + nr * xr + ni * xi
            ai = ai + ni * xr - nr * xi
            return (jnp.broadcast_to(lr[0:1, :], (GRP, SB)), jnp.broadcast_to(li[0:1, :], (GRP, SB)), ar, ai)

        cr, ci, ar, ai = lax.fori_loop(0, ts // GRP, grp, (car_r[...], car_i[...], acc_r[...], acc_i[...]))
        car_r[...] = cr
        car_i[...] = ci
        acc_r[...] = ar
        acc_i[...] = ai

        @pl.when(i == nt - 1)
        def _():
            dar_ref[0] = _colsum(ar)
            dai_ref[0] = _colsum(ai)

        lrb = lr_ref[...].astype(BF16)
        lib = li_ref[...].astype(BF16)
        du = (jnp.dot(lrb, wbtr_ref[0], preferred_element_type=F32)
              + jnp.dot(lib, wbti_ref[0], preferred_element_type=F32) + d_ref[0] * dy)
        du_ref[...] = du.astype(du_ref.dtype)
        ub = u.astype(BF16)
        dwbr_ref[0] += lax.dot_general(ub, lrb, tn_dims, preferred_element_type=F32)
        dwbi_ref[0] += lax.dot_general(ub, lib, tn_dims, preferred_element_type=F32)
        dwc_ref[0, 0:SB, :] += lax.dot_general(xr_ref[...].astype(BF16), dyb, tn_dims, preferred_element_type=F32)
        dwc_ref[0, SB:2 * SB, :] += lax.dot_general(xi_ref[...].astype(BF16), dyb, tn_dims, preferred_element_type=F32)

    rev = lambda i: nt - 1 - i
    blk3 = lambda a: pl.BlockSpec((1,) + a.shape[1:], lambda j, i: (j, 0, 0))
    acc3 = lambda r, c: pl.BlockSpec((1, r, c), lambda j, i: (j, 0, 0))
    return pl.pallas_call(
        body, name="ssm_bwd", grid=(NBLK, nt),
        in_specs=[pl.BlockSpec((ts, CB), lambda j, i: (rev(i), j)), pl.BlockSpec((ts, CB), lambda j, i: (rev(i), j)),
                  pl.BlockSpec((ts, CB), lambda j, i: (rev(i), ucol0 + j)),
                  pl.BlockSpec((ts, SB), lambda j, i: (rev(i), j)), pl.BlockSpec((ts, SB), lambda j, i: (rev(i), j)),
                  blk3(wbt_re), blk3(wbt_im), blk3(wct), blk3(e_re), blk3(e_im), blk3(dvec),
                  pl.BlockSpec(memory_space=pl.ANY)],
        out_specs=[pl.BlockSpec((ts, CB), lambda j, i: (rev(i), ucol0 + j)),
                   acc3(1, CB), acc3(1, SB), acc3(1, SB), acc3(CB, SB), acc3(CB, SB), acc3(2 * SB, CB)],
        out_shape=[jax.ShapeDtypeStruct(dz.shape, dz.dtype),
                   jax.ShapeDtypeStruct((NBLK, 1, CB), F32),
                   jax.ShapeDtypeStruct((NBLK, 1, SB), F32), jax.ShapeDtypeStruct((NBLK, 1, SB), F32),
                   jax.ShapeDtypeStruct((NBLK, CB, SB), F32), jax.ShapeDtypeStruct((NBLK, CB, SB), F32),
                   jax.ShapeDtypeStruct((NBLK, 2 * SB, CB), F32)],
        scratch_shapes=[pltpu.VMEM((ts, SB), F32), pltpu.VMEM((ts, SB), F32)] + [pltpu.VMEM((GRP, SB), F32)] * 4,
        input_output_aliases={11: 0},
        compiler_params=_params(("parallel", "arbitrary")),
    )(dgl, ypre, z, xs_re, xs_im, wbt_re, wbt_im, wct, e_re, e_im, dvec, dz)


def _disc(a_re, a_im, log_dt, b_re, b_im, expand):
    dt = jnp.dot(expand, jnp.exp(log_dt), preferred_element_type=F32, precision=lax.Precision.HIGHEST)
    mag = jnp.exp(dt * a_re)
    e_re, e_im = mag * jnp.cos(dt * a_im), mag * jnp.sin(dt * a_im)
    n_re, n_im = e_re - 1.0, e_im
    den = a_re * a_re + a_im * a_im
    q_re = (n_re * a_re + n_im * a_im) / den
    q_im = (n_im * a_re - n_re * a_im) / den
    return e_re, e_im, q_re * b_re - q_im * b_im, q_re * b_im + q_im * b_re


def _whole(a):
    return pl.BlockSpec(a.shape, functools.partial(lambda n: (0,) * n, n=a.ndim))


def disc_fwd(a_re, a_im, log_dt, b_re, b_im, expand):
    def body(ar, ai, ld, br, bi, ex, er_o, ei_o, bbr_o, bbi_o):
        er, ei, bbr, bbi = _disc(ar[...], ai[...], ld[...], br[...], bi[...], ex[...])
        er_o[...] = er
        ei_o[...] = ei
        bbr_o[...] = bbr
        bbi_o[...] = bbi

    ins = (a_re, a_im, log_dt, b_re, b_im, expand)
    outs = [jax.ShapeDtypeStruct(a_re.shape, F32)] * 2 + [jax.ShapeDtypeStruct(b_re.shape, F32)] * 2
    return pl.pallas_call(body, name="disc_fwd", in_specs=[_whole(a) for a in ins],
                          out_specs=[_whole(o) for o in outs], out_shape=outs, compiler_params=_params())(*ins)


def disc_bwd(a_re, a_im, log_dt, b_re, b_im, expand, de_re, de_im, dbb_re, dbb_im):
    def body(ar, ai, ld, br, bi, ex, der, dei, dbr, dbi, o_ar, o_ai, o_ld, o_br, o_bi):
        exv = ex[...]
        _, vjp = jax.vjp(lambda *p: _disc(*p, exv), ar[...], ai[...], ld[...], br[...], bi[...])
        g = vjp((der[...], dei[...], dbr[...], dbi[...]))
        for o, v in zip((o_ar, o_ai, o_ld, o_br, o_bi), g):
            o[...] = v

    ins = (a_re, a_im, log_dt, b_re, b_im, expand, de_re, de_im, dbb_re, dbb_im)
    outs = [jax.ShapeDtypeStruct(a.shape, F32) for a in (a_re, a_im, log_dt, b_re, b_im)]
    return pl.pallas_call(body, name="disc_bwd", in_specs=[_whole(a) for a in ins],
                          out_specs=[_whole(o) for o in outs], out_shape=outs, compiler_params=_params())(*ins)


def mod_fwd(c_all, w_ada, b_cols):
    def body(c_ref, w_ref, b_ref, act_ref, mod_ref):
        cv = c_ref[...]
        act = cv * _sig(cv)
        act_ref[...] = act
        mod_ref[...] = jnp.dot(act, w_ref[...], preferred_element_type=F32, precision=lax.Precision.HIGHEST) + b_ref[...]

    ins = (c_all, w_ada, b_cols)
    outs = [jax.ShapeDtypeStruct(c_all.shape, F32), jax.ShapeDtypeStruct((NDEV, w_ada.shape[1]), F32)]
    return pl.pallas_call(body, name="mod_fwd", in_specs=[_whole(a) for a in ins],
                          out_specs=[_whole(o) for o in outs], out_shape=outs, compiler_params=_params())(*ins)


def ada_grad(act_all, dmod_cols):
    def body(a_ref, d_ref, o_ref):
        o_ref[...] = lax.dot_general(a_ref[...], d_ref[...], (((0,), (0,)), ((), ())),
                                     preferred_element_type=F32, precision=lax.Precision.HIGHEST)

    out = jax.ShapeDtypeStruct((act_all.shape[1], dmod_cols.shape[1]), F32)
    return pl.pallas_call(body, name="ada_grad", in_specs=[_whole(act_all), _whole(dmod_cols)],
                          out_specs=_whole(out), out_shape=out, compiler_params=_params())(act_all, dmod_cols)


def _adam_math(w, g, m, v):
    m2 = ADAM_B1 * m + (1.0 - ADAM_B1) * g
    v2 = ADAM_B2 * v + (1.0 - ADAM_B2) * (g * g)
    m_hat = m2 / (1.0 - ADAM_B1 ** ADAM_STEP)
    v_hat = v2 / (1.0 - ADAM_B2 ** ADAM_STEP)
    delta = -ADAM_LR * (m_hat / (jnp.sqrt(v_hat) + ADAM_EPS) + ADAM_WD * w)
    return delta, m2, v2


def adam(name, w, g, m, v):
    r, c = w.shape
    tr = r
    for cand in (256, 128, 64, 32, 16, 8):
        if r % cand == 0 and r > cand:
            tr = cand
            break

    def body(w_ref, g_ref, m_ref, v_ref, d_o, m_o, v_o):
        d, m2, v2 = _adam_math(w_ref[...], g_ref[...], m_ref[...], v_ref[...])
        d_o[...] = d
        m_o[...] = m2
        v_o[...] = v2

    spec = pl.BlockSpec((tr, c), lambda i: (i, 0))
    out = jax.ShapeDtypeStruct((r, c), F32)
    return pl.pallas_call(body, name=name, grid=(r // tr,), in_specs=[spec] * 4, out_specs=[spec] * 3,
                          out_shape=[out] * 3, compiler_params=_params(("parallel",)))(w, g, m, v)


def adam_many(name, ws, gs, ms, vs):
    n = len(ws)

    def body(*refs):
        ins, outs = refs[:4 * n], refs[4 * n:]
        for q in range(n):
            d, m2, v2 = _adam_math(ins[q][...], ins[n + q][...], ins[2 * n + q][...], ins[3 * n + q][...])
            outs[q][...] = d
            outs[n + q][...] = m2
            outs[2 * n + q][...] = v2

    operands = list(ws) + list(gs) + list(ms) + list(vs)
    outs = [jax.ShapeDtypeStruct(w.shape, F32) for w in ws] * 3
    return pl.pallas_call(body, name=name, in_specs=[_whole(a) for a in operands],
                          out_specs=[_whole(o) for o in outs], out_shape=outs, compiler_params=_params())(*operands)


def _rows_tile(r, most):
    best = None
    for t in range(16, min(r, most) + 1, 16):
        if r % t == 0:
            best = t
    assert best is not None, r
    return best


def sum_slots(name, slots, out_dtype=F32):
    n, r, c = slots.shape
    tr = _rows_tile(r, max(16, (2 * 1024 * 1024) // (n * c)))

    def body(s_ref, o_ref):
        acc = s_ref[0].astype(F32)
        for q in range(1, n):
            acc = acc + s_ref[q].astype(F32)
        o_ref[...] = acc.astype(o_ref.dtype)

    return pl.pallas_call(body, name=name, grid=(r // tr,),
                          in_specs=[pl.BlockSpec((n, tr, c), lambda i: (0, i, 0))],
                          out_specs=pl.BlockSpec((tr, c), lambda i: (i, 0)),
                          out_shape=jax.ShapeDtypeStruct((r, c), out_dtype), compiler_params=_params(("parallel",)))(slots)


HBM_SPEC = pl.BlockSpec(memory_space=pltpu.HBM)


def _coords():
    return lax.axis_index("x"), lax.axis_index("y"), lax.axis_index("c")


def _linear(x, y, c):
    return 4 * x + 2 * y + c


def all_gather(name, shards):
    nq = len(shards)

    def body(*refs):
        xs, outs = refs[:nq], refs[nq:2 * nq]
        send_sems, recv_sems, local_sems = refs[2 * nq:2 * nq + 3]
        bufs = refs[2 * nq + 3:]
        x, y, cc = _coords()
        me, sibling = (x, y, cc), (x, y, 1 - cc)
        chips = [(1 - x, y), (x, 1 - y), (1 - x, 1 - y)]

        def slot(q, px, py, pc):
            return outs[q].at[_linear(px, py, pc)]

        def copy(q, k, block, to, src=None):
            return pltpu.make_async_remote_copy(
                src_ref=slot(q, *block) if src is None else src, dst_ref=slot(q, *block),
                send_sem=send_sems.at[7 * q + k], recv_sem=recv_sems.at[7 * q + k], device_id=to, device_id_type=MESH)

        loads = [pltpu.make_async_copy(xs[q], bufs[q], local_sems.at[q]) for q in range(nq)]
        for cp in loads:
            cp.start()
        for cp in loads:
            cp.wait()
        mine = [pltpu.make_async_copy(bufs[q], slot(q, *me), local_sems.at[q]) for q in range(nq)]
        first = []
        for q in range(nq):
            first.append(copy(q, 0, me, sibling, src=bufs[q]))
            first += [copy(q, 1 + j, me, (*chip, cc), src=bufs[q]) for j, chip in enumerate(chips)]
        for cp in mine + first:
            cp.start()
        passed = []
        for q in range(nq):
            for j, chip in enumerate(chips):
                copy(q, 1 + j, (*chip, cc), me).wait_recv()
                passed.append(copy(q, 4 + j, (*chip, cc), sibling))
                passed[-1].start()
        for q in range(nq):
            copy(q, 0, sibling, me).wait_recv()
            for j, chip in enumerate(chips):
                copy(q, 4 + j, (*chip, 1 - cc), me).wait_recv()
        for cp in first + passed:
            cp.wait_send()
        for cp in mine:
            cp.wait()

    return pl.pallas_call(
        body, name=name, in_specs=[HBM_SPEC] * nq, out_specs=[HBM_SPEC] * nq,
        out_shape=[jax.ShapeDtypeStruct((NDEV,) + s.shape, s.dtype) for s in shards],
        scratch_shapes=[pltpu.SemaphoreType.DMA((7 * nq,)), pltpu.SemaphoreType.DMA((7 * nq,)),
                        pltpu.SemaphoreType.DMA((nq,))] + [pltpu.VMEM(s.shape, s.dtype) for s in shards],
    )(*shards)


NCHIP = 4


SEM_SPEC = pl.BlockSpec(memory_space=pltpu.SEMAPHORE)
EFFECT = pltpu.SideEffectType.DATAFLOW_SIDE_EFFECTING


def _peer(x, y, cc, k):
    fx, fy, fc = (k >> 2) & 1, (k >> 1) & 1, k & 1
    return (x + fx - 2 * fx * x, y + fy - 2 * fy * y, cc + fc - 2 * fc * cc)


def gather_plan(srcs, lands, coords):
    x, y, cc = coords
    me = _linear(x, y, cc)
    return [(s, l.at[me], _peer(x, y, cc, k)) for s, l in zip(srcs, lands) for k in range(1, NDEV)]


def pair_plan(srcs, lands, coords):
    x, y, cc = coords
    return [(s.at[2 * chip + 1 - cc], l.at[chip], (x, y, 1 - cc)) for s, l in zip(srcs, lands) for chip in range(NCHIP)]


def chip_plan(srcs, lands, coords):
    x, y, cc = coords
    out = []
    for s, l in zip(srcs, lands):
        for k in range(1, NCHIP):
            px, py, _ = _peer(x, y, cc, 2 * k)
            out.append((s.at[2 * px + py], l.at[k - 1], (px, py, cc)))
    return out


def _remote(copy, i, send_sems, recv_sems):
    src, dst, dev = copy
    return pltpu.make_async_remote_copy(src_ref=src, dst_ref=dst, send_sem=send_sems.at[i], recv_sem=recv_sems.at[i],
                                        device_id=dev, device_id_type=MESH)


def exchange_start(name, plan, ncopy, srcs, land_shapes, deps=()):
    ns, nl, nd = len(srcs), len(land_shapes), len(deps)

    def body(*refs):
        s, l = refs[:ns], refs[ns:ns + nl]
        send_sems, recv_sems = refs[ns + nl + nd], refs[ns + nl + nd + 1]
        token = refs[-1]
        for i, cp in enumerate(plan(s, l, _coords())):
            _remote(cp, i, send_sems, recv_sems).start()
        token[...] = jnp.zeros_like(token)

    hbm = lambda a: pltpu.with_memory_space_constraint(a, pltpu.HBM)
    lands = [lax.empty(shp, dt) for shp, dt in land_shapes]
    thru = [pltpu.HBM(a.shape, a.dtype) for a in list(srcs) + lands]
    outs = pl.pallas_call(
        body, name=name,
        in_specs=[HBM_SPEC] * (ns + nl) + [ANY_SPEC] * nd,
        out_specs=(SEM_SPEC, SEM_SPEC, *[HBM_SPEC] * (ns + nl), pl.BlockSpec(memory_space=pltpu.VMEM)),
        out_shape=(pltpu.SemaphoreType.DMA((ncopy,)), pltpu.SemaphoreType.DMA((ncopy,)), *thru,
                   jax.ShapeDtypeStruct((8, LANE), F32)),
        input_output_aliases={i: 2 + i for i in range(ns + nl)},
        compiler_params=pltpu.CompilerParams(has_side_effects=EFFECT),
    )(*[hbm(a) for a in srcs], *[hbm(a) for a in lands], *deps)
    return outs[0], outs[1], list(outs[2:2 + ns]), list(outs[2 + ns:2 + ns + nl]), outs[-1]


def exchange_wait(name, plan, started, after, place_own=False):
    send_sems, recv_sems, srcs, lands, _ = started
    ns, nl = len(srcs), len(lands)

    def body(*refs):
        s, l = refs[:ns], refs[ns:ns + nl]
        send_sems, recv_sems = refs[ns + nl], refs[ns + nl + 1]
        l_out = refs[2 * ns + nl + 3:2 * ns + 2 * nl + 3]
        scratch = refs[2 * ns + 2 * nl + 3:]
        copies = [_remote(cp, i, send_sems, recv_sems) for i, cp in enumerate(plan(s, l, _coords()))]
        if place_own:
            me = _linear(*_coords())
            local_sems, bufs = scratch[0], scratch[1:]
            loads = [pltpu.make_async_copy(s[q], bufs[q], local_sems.at[q]) for q in range(ns)]
            for cp in loads:
                cp.start()
            for cp in loads:
                cp.wait()
            stores = [pltpu.make_async_copy(bufs[q], l_out[q].at[me], local_sems.at[q]) for q in range(ns)]
            for cp in stores:
                cp.start()
        for cp in copies:
            cp.wait_recv()
        for cp in copies:
            cp.wait_send()
        if place_own:
            for cp in stores:
                cp.wait()

    scratch_shapes = []
    if place_own:
        scratch_shapes = [pltpu.SemaphoreType.DMA((ns,))] + [pltpu.VMEM(a.shape, a.dtype) for a in srcs]
    outs = pl.pallas_call(
        body, name=name,
        in_specs=[HBM_SPEC] * (ns + nl) + [SEM_SPEC, SEM_SPEC, ANY_SPEC],
        out_specs=[HBM_SPEC] * (ns + nl),
        out_shape=[pltpu.HBM(a.shape, a.dtype) for a in srcs + lands],
        input_output_aliases={i: i for i in range(ns + nl)},
        scratch_shapes=scratch_shapes,
        compiler_params=pltpu.CompilerParams(has_side_effects=EFFECT),
    )(*srcs, *lands, send_sems, recv_sems, after)
    return list(outs[:ns]), list(outs[ns:])


def pair_sum(name, g, recv):
    _, r, c = g.shape
    tr = _rows_tile(r, 512)

    def body(g_ref, r_ref, o_ref):
        own = jnp.where(lax.axis_index("c") == 0, g_ref[0, 0], g_ref[0, 1])
        o_ref[0] = (own.astype(F32) + r_ref[0].astype(F32)).astype(o_ref.dtype)

    return pl.pallas_call(
        body, name=name, grid=(NCHIP, r // tr),
        in_specs=[pl.BlockSpec((1, 2, tr, c), lambda k, i: (k, 0, i, 0)), pl.BlockSpec((1, tr, c), lambda k, i: (k, i, 0))],
        out_specs=pl.BlockSpec((1, tr, c), lambda k, i: (k, i, 0)),
        out_shape=jax.ShapeDtypeStruct((NCHIP, r, c), g.dtype), compiler_params=_params(("parallel", "parallel")),
    )(g.reshape(NCHIP, 2, r, c), recv)


def chip_sum(name, partial, recv):
    _, r, c = partial.shape
    tr = _rows_tile(r, 512)

    def body(p_ref, r_ref, o_ref):
        chip = 2 * lax.axis_index("x") + lax.axis_index("y")
        own = p_ref[0]
        for k in range(1, NCHIP):
            own = jnp.where(chip == k, p_ref[k], own)
        acc = own.astype(F32)
        for k in range(NCHIP - 1):
            acc = acc + r_ref[k].astype(F32)
        o_ref[...] = acc

    return pl.pallas_call(
        body, name=name, grid=(r // tr,),
        in_specs=[pl.BlockSpec((NCHIP, tr, c), lambda i: (0, i, 0)), pl.BlockSpec((NCHIP - 1, tr, c), lambda i: (0, i, 0))],
        out_specs=pl.BlockSpec((tr, c), lambda i: (i, 0)),
        out_shape=jax.ShapeDtypeStruct((r, c), F32), compiler_params=_params(("parallel",)),
    )(partial, recv)


def _block_diag(w, rows_per, cols_per):
    w = w.reshape(NBLK, 8, rows_per, cols_per)
    eye = jnp.eye(8, dtype=w.dtype)
    out = w[:, :, :, None, :] * eye[None, :, None, :, None]
    return out.reshape(NBLK, 8 * rows_per, 8 * cols_per)


def _diag_blocks(wd, rows_per, cols_per):
    wd = wd.reshape(NBLK, 8, rows_per, 8, cols_per)
    idx = jnp.arange(8)
    return wd[:, idx, :, idx, :].transpose(1, 0, 2, 3).reshape(NG, rows_per, cols_per)


def _pad_rows(v, mult):
    n = v.shape[0]
    return jnp.pad(v, (0, (-n) % mult))


def kernel(x, c, w_ada, b_ada, norm1_g, w_in, conv_w, conv_b, conv_ln_g, conv_ln_b, conv_proj, ssm_a_re, ssm_a_im, ssm_b_re, ssm_b_im, ssm_c_re, ssm_c_im, ssm_d, ssm_log_dt, ssm_glu, w_out, norm2_g, w_ffn_in, w_ffn_out, final_g, loss_target, m_w_ada, m_b_ada, m_norm1_g, m_w_in, m_conv_w, m_conv_b, m_conv_ln_g, m_conv_ln_b, m_conv_proj, m_ssm_a_re, m_ssm_a_im, m_ssm_b_re, m_ssm_b_im, m_ssm_c_re, m_ssm_c_im, m_ssm_d, m_ssm_log_dt, m_ssm_glu, m_w_out, m_norm2_g, m_w_ffn_in, m_w_ffn_out, m_final_g, v_w_ada, v_b_ada, v_norm1_g, v_w_in, v_conv_w, v_conv_b, v_conv_ln_g, v_conv_ln_b, v_conv_proj, v_ssm_a_re, v_ssm_a_im, v_ssm_b_re, v_ssm_b_im, v_ssm_c_re, v_ssm_c_im, v_ssm_d, v_ssm_log_dt, v_ssm_glu, v_w_out, v_norm2_g, v_w_ffn_in, v_w_ffn_out, v_final_g):
    me = _linear(*_coords())
    xs = x[0]
    tgt = loss_target[0]
    seq = xs.shape[0]

    flat = lambda g: g.reshape(NDEV * g.shape[1], g.shape[2])
    c_all, cw_g, w_in_g = all_gather("gather_first", [c, conv_w[0], w_in[0].T.astype(BF16)])
    w_in_t = flat(w_in_g)
    mids = [p.astype(BF16) for p in (conv_proj[0].T, ssm_glu[0].T, w_out[0])]
    ffns = [p.astype(BF16) for p in (w_ffn_in[0].T, w_ffn_out[0])]
    zone = lambda p: ((NDEV,) + p.shape, p.dtype)
    mids_go = exchange_start("gather_mid_start", gather_plan, 7 * len(mids), mids, [zone(p) for p in mids],
                             deps=[c_all])
    ffns_go = exchange_start("gather_ffn_start", gather_plan, 7 * len(ffns), ffns, [zone(p) for p in ffns],
                             deps=[mids_go[4]])

    ncol = w_ada.shape[2]
    c_all = c_all.reshape(NDEV, D)
    b_cols = lax.dynamic_slice_in_dim(b_ada, me * ncol, ncol, axis=1)
    act_all, mod_cols = mod_fwd(c_all, w_ada[0], b_cols)
    (mod_all,) = all_gather("gather_mod", [mod_cols])
    mod = lax.dynamic_index_in_dim(mod_all, me, axis=1, keepdims=False).reshape(NMOD, D)
    sh1, sc1, g1, sh2, sc2, g2 = [mod[q:q + 1] for q in range(NMOD)]

    expand = jnp.repeat(jnp.eye(NG, dtype=F32), NP, axis=0)
    a_re_c, a_im_c = ssm_a_re.reshape(NST, 1), ssm_a_im.reshape(NST, 1)
    ldt_c = ssm_log_dt.reshape(NG, 1)
    b_re_r, b_im_r = ssm_b_re.reshape(NST, GH), ssm_b_im.reshape(NST, GH)
    e_re, e_im, bb_re, bb_im = disc_fwd(a_re_c, a_im_c, ldt_c, b_re_r, b_im_r, expand)
    e_re_b, e_im_b = e_re.reshape(NBLK, 1, SB), e_im.reshape(NBLK, 1, SB)
    bb_re_g, bb_im_g = bb_re.reshape(NG, NP, GH), bb_im.reshape(NG, NP, GH)
    wbt_re = _block_diag(bb_re_g, NP, GH)
    wbt_im = _block_diag(bb_im_g, NP, GH)
    wb_re, wb_im = wbt_re.transpose(0, 2, 1), wbt_im.transpose(0, 2, 1)
    wct = jnp.concatenate([_block_diag(ssm_c_re[0], GH, NP), -_block_diag(ssm_c_im[0], GH, NP)], axis=2)
    wc = wct.transpose(0, 2, 1)
    to_b = lambda a: a.astype(BF16)
    dvec = ssm_d.reshape(NBLK, 1, CB)

    n1g = norm1_g

    def f_norm1(xv, g, sc, sh):
        _, xh = _rms_stats(xv)
        return [xh * g * (1.0 + sc) + sh], []

    (h1,) = rowwise("norm1", f_norm1, [xs], [n1g, sc1, sh1], [(D, BF16)], [], 512, deps=[ffns_go[4]])
    z = mm("mm_in", h1, w_in_t, "nt", tiles=(1024, CW, 1024), b_rot=Z_ROT)

    conv_w_full = cw_g.transpose(1, 0, 2).reshape(KC, CW)
    w32 = jnp.pad(conv_w_full, ((0, HALO - KC), (0, 0)))
    yc, s_act = conv_fwd(z, w32, conv_b, conv_ln_g, conv_ln_b)
    conv_proj_t, ssm_glu_t, w_out_f = [
        flat(g) for g in exchange_wait("gather_mid_wait", gather_plan, mids_go, s_act, place_own=True)[1]]
    y_conv = mm("mm_conv_proj", s_act, conv_proj_t, "nt")

    xs_re, xs_im, ypre, gl = ssm_fwd(z, to_b(wb_re), to_b(wb_im), to_b(wc), e_re_b, e_im_b, dvec)
    n_mrg = D // MRG_BLK

    def ep_merge(accs, yc_v, glc, gls):
        za, zb = accs
        return [_sig(glc) * yc_v + _sig(gls) * (za * _sig(zb)), jnp.concatenate([za, zb], axis=1)]

    merged, z2_pair = mm_ep("mm_ssm_glu", gl, ssm_glu_t, 2, lambda j, q: j + q * n_mrg, ep_merge,
                            [(y_conv, 1, 0), (z, 1, 0), (z, 1, n_mrg)], [(D, BF16, 1), (2 * D, BF16, 2)],
                            (512, MRG_BLK, SW))
    o1 = mm("mm_out", merged, w_out_f, "nn")

    def f_norm2(xv, o1v, g1v, g, sc, sh):
        x1v = xv + g1v * o1v
        _, xh = _rms_stats(x1v)
        return [x1v, xh * g * (1.0 + sc) + sh], []

    x1, h2 = rowwise("norm2", f_norm2, [xs, o1], [g1, norm2_g, sc2, sh2], [(D, F32), (D, BF16)], [], 512)
    w_ffn_in_t, w_ffn_out_f = [
        flat(g) for g in exchange_wait("gather_ffn_wait", gather_plan, ffns_go, h2, place_own=True)[1]]
    ffn_tiles = (512, FFN_BLK, 1024)
    n_ffn_blk = FH // FFN_BLK
    pair_map = lambda t: t // 2 + (t % 2) * n_ffn_blk

    def ep_swiglu(accs):
        fg, fu = accs
        return [fg * _sig(fg) * fu, jnp.concatenate([fg, fu], axis=1)]

    act, f_pair = mm_ep("mm_ffn_in", h2, w_ffn_in_t, 2, lambda j, q: j + q * n_ffn_blk, ep_swiglu, [],
                        [(FH, BF16, 1), (2 * FH, BF16, 2)], ffn_tiles)
    o2 = mm("mm_ffn_out", act, w_ffn_out_f, "nn")

    fg_row = final_g.reshape(1, D)

    def f_final(x1v, o2v, tv, g2v, fg):
        x2v = x1v + g2v * o2v
        r, xh = _rms_stats(x2v)
        yv = xh * fg
        err = yv - tv
        loss = jnp.sum(_colsum(err * err), axis=1, keepdims=True) * (0.5 / D)
        dy = err * (1.0 / D)
        dx2 = _rms_bwd(dy * fg, xh, r)
        return ([dx2, g2v * dx2],
                [jnp.broadcast_to(loss, (1, LANE)), _colsum(dy * xh), _colsum(dx2 * o2v)])

    dx2, do2, loss_l, d_final_g, d_g2 = rowwise(
        "final", f_final, [x1, o2, tgt], [g2, fg_row], [(D, F32), (D, BF16)], [LANE, D, D], 256)

    g_ffn_out = mm("mm_g_ffn_out", act, do2, "tn", BF16)

    def ep_dswiglu(accs, fp):
        (da,) = accs
        fg, fu = fp[:, 0:FFN_BLK].astype(F32), fp[:, FFN_BLK:2 * FFN_BLK].astype(F32)
        sg = _sig(fg)
        return [jnp.concatenate([da * fu * (sg * (1.0 + fg * (1.0 - sg))), da * (fg * sg)], axis=1)]

    (df,) = mm_ep("mm_dact", do2, w_ffn_out_f, 1, lambda j, q: j, ep_dswiglu, [(f_pair, 2, 0)],
                  [(2 * FH, BF16, 2)], ffn_tiles)
    dh2 = mm("mm_dh2", df, w_ffn_in_t, "nn", tiles=(1024, 1024, FFN_BLK), b_rot=pair_map)
    g_ffn_in_t = mm("mm_g_ffn_in", df, h2, "tn", BF16, tiles=(FFN_BLK, 1024, 1024), o_rot=pair_map)

    def pair_go(tag, grads_t, deps=()):
        srcs = [g.reshape(NDEV, -1, D) for g in grads_t]
        return exchange_start("pair_" + tag + "_start", pair_plan, NCHIP * len(srcs), srcs,
                              [((NCHIP,) + s.shape[1:], s.dtype) for s in srcs], deps)

    def chip_go(tag, names, pair_started, after):
        own, from_sibling = exchange_wait("pair_" + tag + "_wait", pair_plan, pair_started, after)
        partials = [pair_sum("pair_sum_" + n, g, r) for n, g, r in zip(names, own, from_sibling)]
        return exchange_start("chip_" + tag + "_start", chip_plan, (NCHIP - 1) * len(partials), partials,
                              [((NCHIP - 1,) + p.shape[1:], p.dtype) for p in partials])

    def chip_done(tag, names, chip_started, after):
        partials, from_chips = exchange_wait("chip_" + tag + "_wait", chip_plan, chip_started, after)
        return [chip_sum("chip_sum_" + n, p, r) for n, p, r in zip(names, partials, from_chips)]

    pair_ffn = pair_go("ffn", [g_ffn_out, g_ffn_in_t])

    def f_dnorm2(dh, x1v, dx2v, o1v, g, sc, g1v):
        r, xh = _rms_stats(x1v)
        dxh = dh * (1.0 + sc) * g
        dx1 = dx2v + _rms_bwd(dxh, xh, r)
        return ([dx1, g1v * dx1],
                [_colsum(dh * xh * g), _colsum(dh), _colsum(dh * (1.0 + sc) * xh), _colsum(dx1 * o1v)])

    dx1, do1, d_sc2, d_sh2, d_n2g, d_g1 = rowwise(
        "dnorm2", f_dnorm2, [dh2, x1, dx2, o1], [norm2_g, sc2, g1], [(D, F32), (D, BF16)], [D, D, D, D], 256,
        deps=[pair_ffn[4]])

    dmerged = mm("mm_dmerged", do1, w_out_f, "nt")
    g_out = mm("mm_g_out", merged, do1, "tn", BF16)
    chip_ffn = chip_go("ffn", ("w_ffn_out", "w_ffn_in"), pair_ffn, g_out)

    def f_dmerge(dm, yc_v, *rest):
        pairs, (glc, gls) = rest[:n_mrg], rest[n_mrg:]
        za = jnp.concatenate([p[:, 0:MRG_BLK] for p in pairs], axis=1).astype(F32)
        zb = jnp.concatenate([p[:, MRG_BLK:2 * MRG_BLK] for p in pairs], axis=1).astype(F32)
        sc_ = _sig(glc)
        ss_ = _sig(gls)
        sb_ = _sig(zb)
        dys = dm * ss_
        dz2 = jnp.concatenate([dys * sb_, dys * za * sb_ * (1.0 - sb_)], axis=1)
        dgl = jnp.concatenate([dm * yc_v * sc_ * (1.0 - sc_), dm * (za * sb_) * ss_ * (1.0 - ss_)], axis=1)
        return [dm * sc_, dz2, dgl], []

    dyconv, dz2, dz = rowwise(
        "dmerge", f_dmerge,
        [dmerged, y_conv] + [(z2_pair, 2 * MRG_BLK, j) for j in range(n_mrg)] + [(z, D, 0), (z, D, 1)],
        [], [(D, BF16), (2 * D, BF16), (2 * D, BF16, ZW, 0)], [], 256, deps=[chip_ffn[4]])

    g_conv_proj_t = mm("mm_g_conv_proj", dyconv, s_act, "tn", BF16)
    dgl = mm("mm_dgl", dz2, ssm_glu_t, "nn")
    g_ssm_glu_t = mm("mm_g_ssm_glu", dz2, gl, "tn", BF16)
    pair_mid = pair_go("mid", [g_out, g_conv_proj_t, g_ssm_glu_t])
    ds = mm("mm_ds", dyconv, conv_proj_t, "nn", deps=[pair_mid[4]])
    dz, d_lng, d_lnb, d_cb, d_cw32 = conv_bwd(ds, yc, z, w32, conv_ln_g, conv_ln_b, dz)
    dz, d_d, d_ar, d_ai, d_wb_re, d_wb_im, d_wc = ssm_bwd(
        dgl, ypre, z, xs_re, xs_im, to_b(wbt_re), to_b(wbt_im), to_b(wct), e_re_b, e_im_b, dvec, dz)
    chip_mid = chip_go("mid", ("w_out", "conv_proj", "ssm_glu"), pair_mid, dz)

    dh1 = mm("mm_dh1", dz, w_in_t, "nn", tiles=(1024, 1024, CW), b_rot=Z_ROT, deps=[chip_mid[4]])
    g_in_t = mm("mm_g_in", dz, h1, "tn", BF16, tiles=(CW, 1024, 1024), o_rot=Z_ROT)
    pair_in = pair_go("in", [g_in_t])

    def f_dnorm1(dh, xv, dx1v, g, sc):
        r, xh = _rms_stats(xv)
        dxh = dh * (1.0 + sc) * g
        return ([dx1v + _rms_bwd(dxh, xh, r)],
                [_colsum(dh * xh * g), _colsum(dh), _colsum(dh * (1.0 + sc) * xh)])

    grad_x, d_sc1, d_sh1, d_n1g = rowwise(
        "dnorm1", f_dnorm1, [dh1, xs, dx1], [n1g, sc1], [(D, F32)], [D, D, D], 256, deps=[pair_in[4]])
    chip_in = chip_go("in", ("w_in",), pair_in, grad_x)

    weights = {
        "w_ada": (w_ada, m_w_ada, v_w_ada), "b_ada": (b_ada, m_b_ada, v_b_ada), "norm1_g": (norm1_g, m_norm1_g, v_norm1_g),
        "w_in": (w_in, m_w_in, v_w_in), "conv_w": (conv_w, m_conv_w, v_conv_w), "conv_b": (conv_b, m_conv_b, v_conv_b),
        "conv_ln_g": (conv_ln_g, m_conv_ln_g, v_conv_ln_g), "conv_ln_b": (conv_ln_b, m_conv_ln_b, v_conv_ln_b),
        "conv_proj": (conv_proj, m_conv_proj, v_conv_proj), "ssm_a_re": (ssm_a_re, m_ssm_a_re, v_ssm_a_re),
        "ssm_a_im": (ssm_a_im, m_ssm_a_im, v_ssm_a_im), "ssm_b_re": (ssm_b_re, m_ssm_b_re, v_ssm_b_re),
        "ssm_b_im": (ssm_b_im, m_ssm_b_im, v_ssm_b_im), "ssm_c_re": (ssm_c_re, m_ssm_c_re, v_ssm_c_re),
        "ssm_c_im": (ssm_c_im, m_ssm_c_im, v_ssm_c_im), "ssm_d": (ssm_d, m_ssm_d, v_ssm_d),
        "ssm_log_dt": (ssm_log_dt, m_ssm_log_dt, v_ssm_log_dt), "ssm_glu": (ssm_glu, m_ssm_glu, v_ssm_glu),
        "w_out": (w_out, m_w_out, v_w_out), "norm2_g": (norm2_g, m_norm2_g, v_norm2_g),
        "w_ffn_in": (w_ffn_in, m_w_ffn_in, v_w_ffn_in), "w_ffn_out": (w_ffn_out, m_w_ffn_out, v_w_ffn_out),
        "final_g": (final_g, m_final_g, v_final_g),
    }
    order = list(weights)
    big = ("w_ada", "w_in", "conv_proj", "ssm_glu", "w_out", "w_ffn_in", "w_ffn_out")
    grads, delta, new_m, new_v = {}, {}, {}, {}

    def adam_big(n, g2d):
        wv, mv, vv = weights[n]
        shp = wv.shape
        d_, m_, v_ = adam("adam_" + n, wv.reshape(shp[-2:]), g2d, mv.reshape(shp[-2:]), vv.reshape(shp[-2:]))
        grads[n], delta[n], new_m[n], new_v[n] = g2d.reshape(shp), d_.reshape(shp), m_.reshape(shp), v_.reshape(shp)
        return d_

    gs_ffn_out, gs_ffn_in = chip_done("ffn", ("w_ffn_out", "w_ffn_in"), chip_ffn, chip_in[4])
    adam_big("w_ffn_out", gs_ffn_out)
    last = adam_big("w_ffn_in", gs_ffn_in.T)
    gs_out, gs_conv_proj, gs_ssm_glu = chip_done("mid", ("w_out", "conv_proj", "ssm_glu"), chip_mid, last)
    adam_big("w_out", gs_out)
    adam_big("conv_proj", gs_conv_proj.reshape(-1, CW).T)
    adam_big("ssm_glu", gs_ssm_glu.reshape(-1, SW).T)

    d_bb_re = _diag_blocks(d_wb_re.transpose(0, 2, 1), NP, GH).reshape(NST, GH)
    d_bb_im = _diag_blocks(d_wb_im.transpose(0, 2, 1), NP, GH).reshape(NST, GH)
    d_wct = d_wc.transpose(0, 2, 1)
    d_c_re = _diag_blocks(d_wct[:, :, 0:SB], GH, NP)
    d_c_im = -_diag_blocks(d_wct[:, :, SB:2 * SB], GH, NP)
    d_a_re, d_a_im, d_ldt, d_b_re, d_b_im = disc_bwd(
        a_re_c, a_im_c, ldt_c, b_re_r, b_im_r, expand, d_ar.reshape(NST, 1), d_ai.reshape(NST, 1), d_bb_re, d_bb_im)

    dmod = jnp.concatenate([d_sh1, d_sc1, d_g1, d_sh2, d_sc2, d_g2], axis=1)
    small_local = [dmod.reshape(-1), d_n1g.reshape(-1), d_cw32[0:KC].reshape(-1), d_cb.reshape(-1), d_lng.reshape(-1),
                   d_lnb.reshape(-1), d_a_re.reshape(-1), d_a_im.reshape(-1), d_b_re.reshape(-1), d_b_im.reshape(-1),
                   d_c_re.reshape(-1), d_c_im.reshape(-1), d_d.reshape(-1), d_ldt.reshape(-1), d_n2g.reshape(-1),
                   d_final_g.reshape(-1), loss_l[0, 0:1]]
    small_sizes = [v.shape[0] for v in small_local]
    packed = _pad_rows(jnp.concatenate(small_local), 256 * LANE).reshape(-1, LANE)
    (small_all,) = all_gather("gather_small", [packed])
    small_sum = sum_slots("sum_small", small_all).reshape(-1)
    pieces, pos = [], 0
    for n in small_sizes:
        pieces.append(small_sum[pos:pos + n])
        pos += n
    (g_b_ada, g_n1g, g_cw_full, g_cb, g_lng, g_lnb, g_a_re, g_a_im, g_b_re, g_b_im, g_c_re, g_c_im, g_d, g_ldt,
     g_n2g, g_fg, loss_sum) = pieces
    loss = loss_sum[0]
    dmod_all = small_all.reshape(NDEV, -1)[:, 0:NMOD * D]
    g_w_ada = ada_grad(act_all, lax.dynamic_slice_in_dim(dmod_all, me * ncol, ncol, axis=1))
    ccol = conv_w.shape[2]
    g_conv_w = lax.dynamic_slice_in_dim(g_cw_full.reshape(KC, CW), me * ccol, ccol, axis=1)

    adam_big("w_ada", g_w_ada)
    grads.update({
        "b_ada": g_b_ada.reshape(b_ada.shape), "norm1_g": g_n1g.reshape(norm1_g.shape),
        "conv_w": g_conv_w[None], "conv_b": g_cb.reshape(conv_b.shape),
        "conv_ln_g": g_lng.reshape(conv_ln_g.shape), "conv_ln_b": g_lnb.reshape(conv_ln_b.shape),
        "ssm_a_re": g_a_re.reshape(ssm_a_re.shape),
        "ssm_a_im": g_a_im.reshape(ssm_a_im.shape), "ssm_b_re": g_b_re.reshape(ssm_b_re.shape),
        "ssm_b_im": g_b_im.reshape(ssm_b_im.shape), "ssm_c_re": g_c_re.reshape(ssm_c_re.shape),
        "ssm_c_im": g_c_im.reshape(ssm_c_im.shape), "ssm_d": g_d.reshape(ssm_d.shape),
        "ssm_log_dt": g_ldt.reshape(ssm_log_dt.shape),
        "norm2_g": g_n2g.reshape(norm2_g.shape),
        "final_g": g_fg.reshape(final_g.shape),
    })
    small = [n for n in order if n not in big]
    rows = lambda a: a.reshape(1, -1) if a.ndim == 1 else a
    small_out = adam_many("adam_small", [rows(weights[n][0]) for n in small], [rows(grads[n]) for n in small],
                          [rows(weights[n][1]) for n in small], [rows(weights[n][2]) for n in small])
    for q, n in enumerate(small):
        shp = weights[n][0].shape
        delta[n], new_m[n], new_v[n] = [small_out[t * len(small) + q].reshape(shp) for t in range(3)]

    (gs_in,) = chip_done("in", ("w_in",), chip_in, small_out[0])
    adam_big("w_in", gs_in.T)

    return (loss, grad_x[None], *[grads[n] for n in order], *[delta[n] for n in order],
            *[new_m[n] for n in order], *[new_v[n] for n in order])
```

```python
import functools
import math

import jax
import jax.numpy as jnp
from jax import lax
from jax.experimental import pallas as pl
from jax.experimental.pallas import tpu as pltpu

F32 = jnp.float32
BF16 = jnp.bfloat16

D = 1024
CW = 512
KC = 31
SW = 512
NG = 32
GH = 16
NP = 64
NST = NG * NP
FH = 2816
FFN_BLK = 1408
MRG_BLK = 512
NMOD = 6
NDEV = 8
EPS = 1e-6
CB = 128
SB = 512
NBLK = SW // CB
HALO = 32
ZW = 2 * CW + SW + 2 * D
Z_ROT = lambda j: (j + 3) % (ZW // CW)
ZB_A, ZB_G, ZB_U = 4, 5, 6

ADAM_LR = 0.001
ADAM_B1 = 0.9
ADAM_B2 = 0.999
ADAM_EPS = 1e-08
ADAM_WD = 0.01
ADAM_STEP = 10

V7X_VMEM_BYTES = 64 * 1024 * 1024
VMEM_LIMIT = V7X_VMEM_BYTES - 8 * 1024 * 1024
LANE = 128
MESH = pl.DeviceIdType.MESH
ANY_SPEC = pl.BlockSpec(memory_space=pl.ANY)


def _params(sem=None, **kw):
    if sem is not None:
        kw["dimension_semantics"] = sem
    return pltpu.CompilerParams(vmem_limit_bytes=VMEM_LIMIT, **kw)


def _tile(n, most):
    best = None
    for t in range(LANE, most + 1, LANE):
        if n % t == 0:
            best = t
    if best is None:
        raise ValueError(f"no tile for {n}")
    return best


def _sig(x):
    return jax.nn.sigmoid(x)


def mm(name, a, b, mode, out_dtype=F32, tiles=None, b_rot=None, o_rot=None, deps=()):
    if mode == "nn":
        (m, k), (k2, n) = a.shape, b.shape
    elif mode == "nt":
        (m, k), (n, k2) = a.shape, b.shape
    else:
        (k, m), (k2, n) = a.shape, b.shape
    assert k == k2, (name, a.shape, b.shape)
    bm, bn, bk = tiles or (_tile(m, 1024), _tile(n, 1408), _tile(k, 1408 if k % 1408 == 0 else 1024))
    bm, bn, bk = min(bm, m), min(bn, n), min(bk, k)
    assert m % bm == 0 and n % bn == 0 and k % bk == 0, (name, m, n, k, bm, bn, bk)
    nk = k // bk
    rot = lambda idx, r: idx if r is None else r(idx)
    if mode == "nn":
        a_spec = pl.BlockSpec((bm, bk), lambda i, j, kk: (i, kk))
        b_spec = pl.BlockSpec((bk, bn), lambda i, j, kk: (rot(kk, b_rot), j))
        dims = (((1,), (0,)), ((), ()))
    elif mode == "nt":
        a_spec = pl.BlockSpec((bm, bk), lambda i, j, kk: (i, kk))
        b_spec = pl.BlockSpec((bn, bk), lambda i, j, kk: (rot(j, b_rot), kk))
        dims = (((1,), (1,)), ((), ()))
    else:
        assert b_rot is None
        a_spec = pl.BlockSpec((bk, bm), lambda i, j, kk: (kk, i))
        b_spec = pl.BlockSpec((bk, bn), lambda i, j, kk: (kk, j))
        dims = (((0,), (0,)), ((), ()))

    def body(a_ref, b_ref, *rest):
        o_ref, acc_ref = rest[-2:]
        kk = pl.program_id(2)

        @pl.when(kk == 0)
        def _():
            acc_ref[...] = jnp.zeros_like(acc_ref)

        acc_ref[...] += lax.dot_general(a_ref[...], b_ref[...], dims, preferred_element_type=F32)

        @pl.when(kk == nk - 1)
        def _():
            o_ref[...] = acc_ref[...].astype(o_ref.dtype)

    return pl.pallas_call(
        body, name=name,
        grid=(m // bm, n // bn, nk),
        in_specs=[a_spec, b_spec] + [ANY_SPEC] * len(deps),
        out_specs=pl.BlockSpec((bm, bn), lambda i, j, kk: (rot(i, o_rot), j)),
        out_shape=jax.ShapeDtypeStruct((m, n), out_dtype),
        scratch_shapes=[pltpu.VMEM((bm, bn), F32)],
        compiler_params=_params(("parallel", "parallel", "arbitrary")),
    )(a, b, *deps)


def mm_ep(name, a, b, n_acc, acc_block, epilogue, extras, outs, tiles, deps=()):
    m, k = a.shape
    bm, bn, bk = tiles
    nj = outs[0][0] // (outs[0][2] * bn)
    nk = k // bk
    assert m % bm == 0 and k % bk == 0 and b.shape[1] == k, (name, a.shape, b.shape, tiles)
    ne, no, nd = len(extras), len(outs), len(deps)
    dims = (((1,), (1,)), ((), ()))

    def body(*refs):
        a_ref, b_refs = refs[0], refs[1:1 + n_acc]
        e_refs = refs[1 + n_acc:1 + n_acc + ne]
        o_refs = refs[1 + n_acc + ne + nd:1 + n_acc + ne + nd + no]
        acc_refs = refs[1 + n_acc + ne + nd + no:]
        av = a_ref[...]
        prods = [lax.dot_general(av, b_ref[...], dims, preferred_element_type=F32) for b_ref in b_refs]

        def finish(accs):
            for o_ref, v in zip(o_refs, epilogue(accs, *[e[...] for e in e_refs])):
                o_ref[...] = v.astype(o_ref.dtype)

        if nk == 1:
            finish(prods)
        else:
            kk = pl.program_id(2)

            @pl.when(kk == 0)
            def _():
                for acc_ref in acc_refs:
                    acc_ref[...] = jnp.zeros_like(acc_ref)

            for acc_ref, p in zip(acc_refs, prods):
                acc_ref[...] += p

            @pl.when(kk == nk - 1)
            def _():
                finish([acc_ref[...] for acc_ref in acc_refs])

    in_specs = [pl.BlockSpec((bm, bk), lambda i, j, kk: (i, kk))]
    in_specs += [pl.BlockSpec((bn, bk), functools.partial(lambda i, j, kk, q: (acc_block(j, q), kk), q=q))
                 for q in range(n_acc)]
    in_specs += [pl.BlockSpec((bm, w * bn), functools.partial(lambda i, j, kk, off: (i, j + off), off=off))
                 for (_, w, off) in extras]
    in_specs += [ANY_SPEC] * nd
    return pl.pallas_call(
        body, name=name, grid=(m // bm, nj, nk),
        in_specs=in_specs,
        out_specs=[pl.BlockSpec((bm, w * bn), lambda i, j, kk: (i, j)) for (_, _, w) in outs],
        out_shape=[jax.ShapeDtypeStruct((m, cols), dt) for (cols, dt, _) in outs],
        scratch_shapes=[pltpu.VMEM((bm, bn), F32)] * (n_acc if nk > 1 else 0),
        compiler_params=_params(("parallel", "parallel", "arbitrary")),
    )(a, *[b] * n_acc, *[e[0] for e in extras], *deps)


def rowwise(name, fn, rows, consts, out_rows, out_sums, ts, alias=None, deps=()):
    rows = [r if isinstance(r, tuple) else (r, r.shape[1], 0) for r in rows]
    out_rows = [o if len(o) == 4 else (o[0], o[1], o[0], 0) for o in out_rows]
    s = rows[0][0].shape[0]
    nt = s // ts
    nr, nc, no, ns = len(rows), len(consts), len(out_rows), len(out_sums)
    in_specs = [pl.BlockSpec((ts, w), functools.partial(lambda i, cb: (i, cb), cb=cb)) for (_, w, cb) in rows]
    in_specs += [pl.BlockSpec(c.shape, lambda i: (0, 0)) for c in consts]
    operands = [r[0] for r in rows] + list(consts)
    aliases = {}
    if alias is not None:
        in_specs.append(pl.BlockSpec(memory_space=pl.ANY))
        operands.append(alias[0])
        aliases = {nr + nc: alias[1]}
    in_specs += [ANY_SPEC] * len(deps)
    operands += list(deps)
    out_shape = [jax.ShapeDtypeStruct((s, tw), dt) for (_, dt, tw, _) in out_rows]
    out_shape += [jax.ShapeDtypeStruct((1, w), F32) for w in out_sums]
    out_specs = [pl.BlockSpec((ts, w), functools.partial(lambda i, cb: (i, cb), cb=cb)) for (w, _, _, cb) in out_rows]
    out_specs += [pl.BlockSpec((1, w), lambda i: (0, 0)) for w in out_sums]
    n_in = len(operands)

    def body(*refs):
        ins, outs = refs[:nr + nc], refs[n_in:]
        i = pl.program_id(0)
        ro, so = fn(*[r[...] for r in ins])
        for q in range(no):
            outs[q][...] = ro[q].astype(outs[q].dtype)
        if ns:
            @pl.when(i == 0)
            def _():
                for q in range(ns):
                    outs[no + q][...] = jnp.zeros_like(outs[no + q])

            for q in range(ns):
                outs[no + q][...] += so[q]

    return pl.pallas_call(
        body, name=name, grid=(nt,),
        in_specs=in_specs, out_specs=out_specs, out_shape=out_shape, input_output_aliases=aliases,
        compiler_params=_params(("arbitrary",) if ns else ("parallel",)),
    )(*operands)


def _colsum(v):
    return jnp.sum(v, axis=0, keepdims=True)


def _rms_stats(xv):
    r = lax.rsqrt(jnp.mean(xv * xv, axis=-1, keepdims=True) + EPS)
    return r, xv * r


def _rms_bwd(dxhat, xhat, r):
    return r * (dxhat - xhat * jnp.mean(dxhat * xhat, axis=-1, keepdims=True))


def _gelu(v):
    k = math.sqrt(2.0 / math.pi)
    t = jnp.tanh(k * (v + 0.044715 * v * v * v))
    return 0.5 * v * (1.0 + t), t


def _gelu_grad(v, t):
    k = math.sqrt(2.0 / math.pi)
    return 0.5 * (1.0 + t) + 0.5 * v * (1.0 - t * t) * k * (1.0 + 3.0 * 0.044715 * v * v)


CONV_TS = 256
CONV_CH = 64


def _ln_fwd(yc, g, b):
    mu = jnp.mean(yc, axis=-1, keepdims=True)
    xc = yc - mu
    rstd = lax.rsqrt(jnp.mean(xc * xc, axis=-1, keepdims=True) + EPS)
    nhat = xc * rstd
    return nhat, rstd, nhat * g + b


SUBL = 8


def _shifted_copies(buf, sh, ts):
    for b in range(1, SUBL):
        sh[b - 1] = buf[pl.ds(b, ts + HALO - SUBL), :]


def _shifted(buf, sh, start):
    b = start % SUBL
    if b == 0:
        return buf[pl.ds(start, CONV_CH), :]
    return sh[b - 1, pl.ds(start - b, CONV_CH), :]


def conv_fwd(z, w32, cb, lg, lb):
    s = z.shape[0]
    ts = CONV_TS
    nt = s // ts
    hb = ts // HALO

    def body(a_ref, g_ref, ah_ref, gh_ref, w_ref, cb_ref, lg_ref, lb_ref, yc_ref, s_ref, ubuf, ush):
        i = pl.program_id(0)
        first = (i > 0).astype(F32)
        ubuf[0:HALO, :] = ah_ref[...] * _sig(gh_ref[...]) * first
        ubuf[HALO:HALO + ts, :] = a_ref[...] * _sig(g_ref[...])
        _shifted_copies(ubuf, ush, ts)
        for c0 in range(0, ts, CONV_CH):
            acc = jnp.zeros((CONV_CH, CW), F32)
            for k in range(KC):
                acc = acc + w_ref[k:k + 1, :] * _shifted(ubuf, ush, c0 + k + 2)
            yc = acc + cb_ref[...]
            yc_ref[c0:c0 + CONV_CH, :] = yc
            _, _, ln = _ln_fwd(yc, lg_ref[...], lb_ref[...])
            s_ref[c0:c0 + CONV_CH, :] = (ln * _sig(ln)).astype(s_ref.dtype)

    cur = lambda cbk: pl.BlockSpec((ts, CW), functools.partial(lambda i, q: (i, q), q=cbk))
    prev = lambda cbk: pl.BlockSpec((HALO, CW), functools.partial(lambda i, q: (jnp.maximum(i * hb - 1, 0), q), q=cbk))
    const = lambda a: pl.BlockSpec(a.shape, lambda i: (0, 0))
    return pl.pallas_call(
        body, name="conv_fwd", grid=(nt,),
        in_specs=[cur(ZB_A), cur(ZB_G), prev(ZB_A), prev(ZB_G), const(w32), const(cb), const(lg), const(lb)],
        out_specs=[pl.BlockSpec((ts, CW), lambda i: (i, 0)), pl.BlockSpec((ts, CW), lambda i: (i, 0))],
        out_shape=[jax.ShapeDtypeStruct((s, CW), F32), jax.ShapeDtypeStruct((s, CW), BF16)],
        scratch_shapes=[pltpu.VMEM((HALO + ts, CW), F32), pltpu.VMEM((SUBL - 1, ts + HALO - SUBL, CW), F32)],
        compiler_params=_params(("parallel",)),
    )(z, z, z, z, w32, cb, lg, lb)


def conv_bwd(ds, yc, z, w32, lg, lb, dz):
    s = z.shape[0]
    ts = CONV_TS
    nt = s // ts
    hb = ts // HALO
    last_hb = s // HALO - 1

    def ln_bwd(dsv, ycv, g, b):
        nhat, rstd, ln = _ln_fwd(ycv, g, b)
        sg = _sig(ln)
        dln = dsv * (sg * (1.0 + ln * (1.0 - sg)))
        dnh = dln * g
        dyc = rstd * (dnh - jnp.mean(dnh, axis=-1, keepdims=True)
                      - nhat * jnp.mean(dnh * nhat, axis=-1, keepdims=True))
        return dyc, dln, nhat

    def body(ds_ref, yc_ref, dsn_ref, ycn_ref, a_ref, g_ref, ah_ref, gh_ref, w_ref, lg_ref, lb_ref, dz_in,
             dz_ref, dlg_ref, dlb_ref, dcb_ref, dw_ref, dbuf, ubuf, dsh, ush):
        i = pl.program_id(0)

        @pl.when(i == 0)
        def _():
            dlg_ref[...] = jnp.zeros_like(dlg_ref)
            dlb_ref[...] = jnp.zeros_like(dlb_ref)
            dcb_ref[...] = jnp.zeros_like(dcb_ref)
            dw_ref[...] = jnp.zeros_like(dw_ref)

        lg, lb = lg_ref[...], lb_ref[...]
        dyc, dln, nhat = ln_bwd(ds_ref[...], yc_ref[...], lg, lb)
        dlg_ref[...] += _colsum(dln * nhat)
        dlb_ref[...] += _colsum(dln)
        dcb_ref[...] += _colsum(dyc)
        dbuf[0:ts, :] = dyc
        nxt = (i < nt - 1).astype(F32)
        dbuf[ts:ts + HALO, :] = ln_bwd(dsn_ref[...], ycn_ref[...], lg, lb)[0] * nxt
        first = (i > 0).astype(F32)
        ubuf[0:HALO, :] = ah_ref[...] * _sig(gh_ref[...]) * first
        ubuf[HALO:HALO + ts, :] = a_ref[...] * _sig(g_ref[...])
        _shifted_copies(dbuf, dsh, ts)
        _shifted_copies(ubuf, ush, ts)
        for c0 in range(0, ts, CONV_CH):
            du = jnp.zeros((CONV_CH, CW), F32)
            dyc_c = dbuf[c0:c0 + CONV_CH, :]
            for k in range(KC):
                du = du + w_ref[k:k + 1, :] * _shifted(dbuf, dsh, c0 + KC - 1 - k)
                dw_ref[k:k + 1, :] += _colsum(dyc_c * _shifted(ubuf, ush, c0 + k + 2))
            av = a_ref[c0:c0 + CONV_CH, :]
            sg = _sig(g_ref[c0:c0 + CONV_CH, :])
            dz_ref[c0:c0 + CONV_CH, 0:CW] = (du * sg).astype(dz_ref.dtype)
            dz_ref[c0:c0 + CONV_CH, CW:2 * CW] = (du * av * sg * (1.0 - sg)).astype(dz_ref.dtype)

    cur = lambda w, cbk: pl.BlockSpec((ts, w), functools.partial(lambda i, q: (i, q), q=cbk))
    prev = lambda cbk: pl.BlockSpec((HALO, CW), functools.partial(lambda i, q: (jnp.maximum(i * hb - 1, 0), q), q=cbk))
    nxt_spec = pl.BlockSpec((HALO, CW), lambda i: (jnp.minimum((i + 1) * hb, last_hb), 0))
    const = lambda a: pl.BlockSpec(a.shape, lambda i: (0, 0))
    acc = lambda r: pl.BlockSpec((r, CW), lambda i: (0, 0))
    return pl.pallas_call(
        body, name="conv_bwd", grid=(nt,),
        in_specs=[cur(CW, 0), cur(CW, 0), nxt_spec, nxt_spec, cur(CW, ZB_A), cur(CW, ZB_G), prev(ZB_A), prev(ZB_G),
                  const(w32), const(lg), const(lb), pl.BlockSpec(memory_space=pl.ANY)],
        out_specs=[pl.BlockSpec((ts, 2 * CW), lambda i: (i, ZB_A // 2)), acc(1), acc(1), acc(1), acc(HALO)],
        out_shape=[jax.ShapeDtypeStruct(dz.shape, dz.dtype), jax.ShapeDtypeStruct((1, CW), F32),
                   jax.ShapeDtypeStruct((1, CW), F32), jax.ShapeDtypeStruct((1, CW), F32),
                   jax.ShapeDtypeStruct((HALO, CW), F32)],
        scratch_shapes=[pltpu.VMEM((ts + HALO, CW), F32), pltpu.VMEM((HALO + ts, CW), F32)]
        + [pltpu.VMEM((SUBL - 1, ts + HALO - SUBL, CW), F32)] * 2,
        input_output_aliases={11: 0},
        compiler_params=_params(("arbitrary",)),
    )(ds, yc, ds, yc, z, z, z, z, w32, lg, lb, dz)


SSM_TS = 512
GRP = 8


def _cmul(ar, ai, br, bi):
    return ar * br - ai * bi, ar * bi + ai * br


def _scan_tables(ar, ai, reverse):
    n = ar.shape[1]
    row = lax.broadcasted_iota(jnp.int32, (GRP, n), 0)
    dist = (GRP - 1 - row) if reverse else row
    one_r = jnp.broadcast_to(ar, (GRP, n))
    one_i = jnp.broadcast_to(ai, (GRP, n))
    p2r, p2i = _cmul(one_r, one_i, one_r, one_i)
    p4r, p4i = _cmul(p2r, p2i, p2r, p2i)
    steps = []
    for sft, (pr, pi) in ((1, (one_r, one_i)), (2, (p2r, p2i)), (4, (p4r, p4i))):
        keep = dist >= sft
        steps.append((jnp.where(keep, pr, 0.0), jnp.where(keep, pi, 0.0)))
    cr, ci = one_r, one_i
    accr, acci = one_r, one_i
    for e in range(1, GRP):
        cr, ci = _cmul(cr, ci, one_r, one_i)
        accr = jnp.where(dist == e, cr, accr)
        acci = jnp.where(dist == e, ci, acci)
    return steps, (accr, acci)


def _scan_group(xr, xi, steps, carry_tab, cr, ci, reverse):
    for sft, (tr, ti) in zip((1, 2, 4), steps):
        amt = (GRP - sft) if reverse else sft
        sr = pltpu.roll(xr, amt, 0)
        si = pltpu.roll(xi, amt, 0)
        xr, xi = xr + tr * sr - ti * si, xi + tr * si + ti * sr
    pr, pi = carry_tab
    xr = xr + pr * cr - pi * ci
    xi = xi + pr * ci + pi * cr
    return xr, xi


def ssm_fwd(z, wb_re, wb_im, wc, e_re, e_im, dvec):
    s = z.shape[0]
    ts = SSM_TS
    nt = s // ts
    ucol0 = ZB_U * CW // CB

    def body(u_ref, wbr_ref, wbi_ref, wc_ref, er_ref, ei_ref, d_ref, xr_ref, xi_ref, y_ref, gl_ref, car_r, car_i):
        i = pl.program_id(1)

        @pl.when(i == 0)
        def _():
            car_r[...] = jnp.zeros_like(car_r)
            car_i[...] = jnp.zeros_like(car_i)

        u = u_ref[...]
        ub = u.astype(BF16)
        xr_ref[...] = jnp.dot(ub, wbr_ref[0], preferred_element_type=F32)
        xi_ref[...] = jnp.dot(ub, wbi_ref[0], preferred_element_type=F32)
        steps, ctab = _scan_tables(er_ref[0], ei_ref[0], False)

        def grp(r, carry):
            cr, ci = carry
            r0 = pl.multiple_of(r * GRP, GRP)
            xr, xi = _scan_group(xr_ref[pl.ds(r0, GRP), :], xi_ref[pl.ds(r0, GRP), :], steps, ctab, cr, ci, False)
            xr_ref[pl.ds(r0, GRP), :] = xr
            xi_ref[pl.ds(r0, GRP), :] = xi
            return (jnp.broadcast_to(xr[GRP - 1:GRP, :], (GRP, SB)), jnp.broadcast_to(xi[GRP - 1:GRP, :], (GRP, SB)))

        cr, ci = lax.fori_loop(0, ts // GRP, grp, (car_r[...], car_i[...]))
        car_r[...] = cr
        car_i[...] = ci
        y = (jnp.dot(xr_ref[...].astype(BF16), wc_ref[0, 0:SB, :], preferred_element_type=F32)
             + jnp.dot(xi_ref[...].astype(BF16), wc_ref[0, SB:2 * SB, :], preferred_element_type=F32)
             + d_ref[0] * u)
        y_ref[...] = y
        gl_ref[...] = _gelu(y)[0].astype(gl_ref.dtype)

    blk3 = lambda a: pl.BlockSpec((1,) + a.shape[1:], lambda j, i: (j, 0, 0))
    return pl.pallas_call(
        body, name="ssm_fwd", grid=(NBLK, nt),
        in_specs=[pl.BlockSpec((ts, CB), lambda j, i: (i, ucol0 + j)),
                  blk3(wb_re), blk3(wb_im), blk3(wc), blk3(e_re), blk3(e_im), blk3(dvec)],
        out_specs=[pl.BlockSpec((ts, SB), lambda j, i: (i, j)), pl.BlockSpec((ts, SB), lambda j, i: (i, j)),
                   pl.BlockSpec((ts, CB), lambda j, i: (i, j)), pl.BlockSpec((ts, CB), lambda j, i: (i, j))],
        out_shape=[jax.ShapeDtypeStruct((s, NST), F32), jax.ShapeDtypeStruct((s, NST), F32),
                   jax.ShapeDtypeStruct((s, SW), F32), jax.ShapeDtypeStruct((s, SW), BF16)],
        scratch_shapes=[pltpu.VMEM((GRP, SB), F32), pltpu.VMEM((GRP, SB), F32)],
        compiler_params=_params(("parallel", "arbitrary")),
    )(z, wb_re, wb_im, wc, e_re, e_im, dvec)


def ssm_bwd(dgl, ypre, z, xs_re, xs_im, wbt_re, wbt_im, wct, e_re, e_im, dvec, dz):
    s = z.shape[0]
    ts = SSM_TS
    nt = s // ts
    ucol0 = ZB_U * CW // CB
    tn_dims = (((0,), (0,)), ((), ()))

    def body(dgl_ref, y_ref, u_ref, xr_ref, xi_ref, wbtr_ref, wbti_ref, wct_ref, er_ref, ei_ref, d_ref, dz_in,
             du_ref, dd_ref, dar_ref, dai_ref, dwbr_ref, dwbi_ref, dwc_ref,
             lr_ref, li_ref, car_r, car_i, acc_r, acc_i):
        i = pl.program_id(1)

        @pl.when(i == 0)
        def _():
            for ref in (car_r, car_i, acc_r, acc_i, dd_ref, dwbr_ref, dwbi_ref, dwc_ref):
                ref[...] = jnp.zeros_like(ref)

        u = u_ref[...]
        y = y_ref[...]
        dy = dgl_ref[...] * _gelu_grad(y, _gelu(y)[1])
        dd_ref[0] += _colsum(dy * u)
        dyb = dy.astype(BF16)
        dxo = jnp.dot(dyb, wct_ref[0], preferred_element_type=F32)
        lr_ref[...] = dxo[:, 0:SB]
        li_ref[...] = dxo[:, SB:2 * SB]
        steps, ctab = _scan_tables(er_ref[0], -ei_ref[0], True)
        row = lax.broadcasted_iota(jnp.int32, (GRP, SB), 0)

        def grp(q, carry):
            cr, ci, ar, ai = carry
            r0 = pl.multiple_of((ts // GRP - 1 - q) * GRP, GRP)
            lr, li = _scan_group(lr_ref[pl.ds(r0, GRP), :], li_ref[pl.ds(r0, GRP), :], steps, ctab, cr, ci, True)
            lr_ref[pl.ds(r0, GRP), :] = lr
            li_ref[pl.ds(r0, GRP), :] = li
            nr = jnp.where(row == GRP - 1, cr, pltpu.roll(lr, GRP - 1, 0))
            ni = jnp.where(row == GRP - 1, ci, pltpu.roll(li, GRP - 1, 0))
            xr = xr_ref[pl.ds(r0, GRP), :]
            xi = xi_ref[pl.ds(r0, GRP), :]
            ar = ar + nr * xr + ni * xi
            ai = ai + ni * xr - nr * xi
            return (jnp.broadcast_to(lr[0:1, :], (GRP, SB)), jnp.broadcast_to(li[0:1, :], (GRP, SB)), ar, ai)

        cr, ci, ar, ai = lax.fori_loop(0, ts // GRP, grp, (car_r[...], car_i[...], acc_r[...], acc_i[...]))
        car_r[...] = cr
        car_i[...] = ci
        acc_r[...] = ar
        acc_i[...] = ai

        @pl.when(i == nt - 1)
        def _():
            dar_ref[0] = _colsum(ar)
            dai_ref[0] = _colsum(ai)

        lrb = lr_ref[...].astype(BF16)
        lib = li_ref[...].astype(BF16)
        du = (jnp.dot(lrb, wbtr_ref[0], preferred_element_type=F32)
              + jnp.dot(lib, wbti_ref[0], preferred_element_type=F32) + d_ref[0] * dy)
        du_ref[...] = du.astype(du_ref.dtype)
        ub = u.astype(BF16)
        dwbr_ref[0] += lax.dot_general(ub, lrb, tn_dims, preferred_element_type=F32)
        dwbi_ref[0] += lax.dot_general(ub, lib, tn_dims, preferred_element_type=F32)
        dwc_ref[0, 0:SB, :] += lax.dot_general(xr_ref[...].astype(BF16), dyb, tn_dims, preferred_element_type=F32)
        dwc_ref[0, SB:2 * SB, :] += lax.dot_general(xi_ref[...].astype(BF16), dyb, tn_dims, preferred_element_type=F32)

    rev = lambda i: nt - 1 - i
    blk3 = lambda a: pl.BlockSpec((1,) + a.shape[1:], lambda j, i: (j, 0, 0))
    acc3 = lambda r, c: pl.BlockSpec((1, r, c), lambda j, i: (j, 0, 0))
    return pl.pallas_call(
        body, name="ssm_bwd", grid=(NBLK, nt),
        in_specs=[pl.BlockSpec((ts, CB), lambda j, i: (rev(i), j)), pl.BlockSpec((ts, CB), lambda j, i: (rev(i), j)),
                  pl.BlockSpec((ts, CB), lambda j, i: (rev(i), ucol0 + j)),
                  pl.BlockSpec((ts, SB), lambda j, i: (rev(i), j)), pl.BlockSpec((ts, SB), lambda j, i: (rev(i), j)),
                  blk3(wbt_re), blk3(wbt_im), blk3(wct), blk3(e_re), blk3(e_im), blk3(dvec),
                  pl.BlockSpec(memory_space=pl.ANY)],
        out_specs=[pl.BlockSpec((ts, CB), lambda j, i: (rev(i), ucol0 + j)),
                   acc3(1, CB), acc3(1, SB), acc3(1, SB), acc3(CB, SB), acc3(CB, SB), acc3(2 * SB, CB)],
        out_shape=[jax.ShapeDtypeStruct(dz.shape, dz.dtype),
                   jax.ShapeDtypeStruct((NBLK, 1, CB), F32),
                   jax.ShapeDtypeStruct((NBLK, 1, SB), F32), jax.ShapeDtypeStruct((NBLK, 1, SB), F32),
                   jax.ShapeDtypeStruct((NBLK, CB, SB), F32), jax.ShapeDtypeStruct((NBLK, CB, SB), F32),
                   jax.ShapeDtypeStruct((NBLK, 2 * SB, CB), F32)],
        scratch_shapes=[pltpu.VMEM((ts, SB), F32), pltpu.VMEM((ts, SB), F32)] + [pltpu.VMEM((GRP, SB), F32)] * 4,
        input_output_aliases={11: 0},
        compiler_params=_params(("parallel", "arbitrary")),
    )(dgl, ypre, z, xs_re, xs_im, wbt_re, wbt_im, wct, e_re, e_im, dvec, dz)


def _disc(a_re, a_im, log_dt, b_re, b_im, expand):
    dt = jnp.dot(expand, jnp.exp(log_dt), preferred_element_type=F32, precision=lax.Precision.HIGHEST)
    mag = jnp.exp(dt * a_re)
    e_re, e_im = mag * jnp.cos(dt * a_im), mag * jnp.sin(dt * a_im)
    n_re, n_im = e_re - 1.0, e_im
    den = a_re * a_re + a_im * a_im
    q_re = (n_re * a_re + n_im * a_im) / den
    q_im = (n_im * a_re - n_re * a_im) / den
    return e_re, e_im, q_re * b_re - q_im * b_im, q_re * b_im + q_im * b_re


def _whole(a):
    return pl.BlockSpec(a.shape, functools.partial(lambda n: (0,) * n, n=a.ndim))


def disc_fwd(a_re, a_im, log_dt, b_re, b_im, expand):
    def body(ar, ai, ld, br, bi, ex, er_o, ei_o, bbr_o, bbi_o):
        er, ei, bbr, bbi = _disc(ar[...], ai[...], ld[...], br[...], bi[...], ex[...])
        er_o[...] = er
        ei_o[...] = ei
        bbr_o[...] = bbr
        bbi_o[...] = bbi

    ins = (a_re, a_im, log_dt, b_re, b_im, expand)
    outs = [jax.ShapeDtypeStruct(a_re.shape, F32)] * 2 + [jax.ShapeDtypeStruct(b_re.shape, F32)] * 2
    return pl.pallas_call(body, name="disc_fwd", in_specs=[_whole(a) for a in ins],
                          out_specs=[_whole(o) for o in outs], out_shape=outs, compiler_params=_params())(*ins)


def disc_bwd(a_re, a_im, log_dt, b_re, b_im, expand, de_re, de_im, dbb_re, dbb_im):
    def body(ar, ai, ld, br, bi, ex, der, dei, dbr, dbi, o_ar, o_ai, o_ld, o_br, o_bi):
        exv = ex[...]
        _, vjp = jax.vjp(lambda *p: _disc(*p, exv), ar[...], ai[...], ld[...], br[...], bi[...])
        g = vjp((der[...], dei[...], dbr[...], dbi[...]))
        for o, v in zip((o_ar, o_ai, o_ld, o_br, o_bi), g):
            o[...] = v

    ins = (a_re, a_im, log_dt, b_re, b_im, expand, de_re, de_im, dbb_re, dbb_im)
    outs = [jax.ShapeDtypeStruct(a.shape, F32) for a in (a_re, a_im, log_dt, b_re, b_im)]
    return pl.pallas_call(body, name="disc_bwd", in_specs=[_whole(a) for a in ins],
                          out_specs=[_whole(o) for o in outs], out_shape=outs, compiler_params=_params())(*ins)


def mod_fwd(c_all, w_ada, b_cols):
    def body(c_ref, w_ref, b_ref, act_ref, mod_ref):
        cv = c_ref[...]
        act = cv * _sig(cv)
        act_ref[...] = act
        mod_ref[...] = jnp.dot(act, w_ref[...], preferred_element_type=F32, precision=lax.Precision.HIGHEST) + b_ref[...]

    ins = (c_all, w_ada, b_cols)
    outs = [jax.ShapeDtypeStruct(c_all.shape, F32), jax.ShapeDtypeStruct((NDEV, w_ada.shape[1]), F32)]
    return pl.pallas_call(body, name="mod_fwd", in_specs=[_whole(a) for a in ins],
                          out_specs=[_whole(o) for o in outs], out_shape=outs, compiler_params=_params())(*ins)


def ada_grad(act_all, dmod_cols):
    def body(a_ref, d_ref, o_ref):
        o_ref[...] = lax.dot_general(a_ref[...], d_ref[...], (((0,), (0,)), ((), ())),
                                     preferred_element_type=F32, precision=lax.Precision.HIGHEST)

    out = jax.ShapeDtypeStruct((act_all.shape[1], dmod_cols.shape[1]), F32)
    return pl.pallas_call(body, name="ada_grad", in_specs=[_whole(act_all), _whole(dmod_cols)],
                          out_specs=_whole(out), out_shape=out, compiler_params=_params())(act_all, dmod_cols)


def _adam_math(w, g, m, v):
    m2 = ADAM_B1 * m + (1.0 - ADAM_B1) * g
    v2 = ADAM_B2 * v + (1.0 - ADAM_B2) * (g * g)
    m_hat = m2 / (1.0 - ADAM_B1 ** ADAM_STEP)
    v_hat = v2 / (1.0 - ADAM_B2 ** ADAM_STEP)
    delta = -ADAM_LR * (m_hat / (jnp.sqrt(v_hat) + ADAM_EPS) + ADAM_WD * w)
    return delta, m2, v2


def adam(name, w, g, m, v):
    r, c = w.shape
    tr = r
    for cand in (256, 128, 64, 32, 16, 8):
        if r % cand == 0 and r > cand:
            tr = cand
            break

    def body(w_ref, g_ref, m_ref, v_ref, d_o, m_o, v_o):
        d, m2, v2 = _adam_math(w_ref[...], g_ref[...], m_ref[...], v_ref[...])
        d_o[...] = d
        m_o[...] = m2
        v_o[...] = v2

    spec = pl.BlockSpec((tr, c), lambda i: (i, 0))
    out = jax.ShapeDtypeStruct((r, c), F32)
    return pl.pallas_call(body, name=name, grid=(r // tr,), in_specs=[spec] * 4, out_specs=[spec] * 3,
                          out_shape=[out] * 3, compiler_params=_params(("parallel",)))(w, g, m, v)


def adam_many(name, ws, gs, ms, vs):
    n = len(ws)

    def body(*refs):
        ins, outs = refs[:4 * n], refs[4 * n:]
        for q in range(n):
            d, m2, v2 = _adam_math(ins[q][...], ins[n + q][...], ins[2 * n + q][...], ins[3 * n + q][...])
            outs[q][...] = d
            outs[n + q][...] = m2
            outs[2 * n + q][...] = v2

    operands = list(ws) + list(gs) + list(ms) + list(vs)
    outs = [jax.ShapeDtypeStruct(w.shape, F32) for w in ws] * 3
    return pl.pallas_call(body, name=name, in_specs=[_whole(a) for a in operands],
                          out_specs=[_whole(o) for o in outs], out_shape=outs, compiler_params=_params())(*operands)


def _rows_tile(r, most):
    best = None
    for t in range(16, min(r, most) + 1, 16):
        if r % t == 0:
            best = t
    assert best is not None, r
    return best


def sum_slots(name, slots, out_dtype=F32):
    n, r, c = slots.shape
    tr = _rows_tile(r, max(16, (2 * 1024 * 1024) // (n * c)))

    def body(s_ref, o_ref):
        acc = s_ref[0].astype(F32)
        for q in range(1, n):
            acc = acc + s_ref[q].astype(F32)
        o_ref[...] = acc.astype(o_ref.dtype)

    return pl.pallas_call(body, name=name, grid=(r // tr,),
                          in_specs=[pl.BlockSpec((n, tr, c), lambda i: (0, i, 0))],
                          out_specs=pl.BlockSpec((tr, c), lambda i: (i, 0)),
                          out_shape=jax.ShapeDtypeStruct((r, c), out_dtype), compiler_params=_params(("parallel",)))(slots)


HBM_SPEC = pl.BlockSpec(memory_space=pltpu.HBM)


def _coords():
    return lax.axis_index("x"), lax.axis_index("y"), lax.axis_index("c")


def _linear(x, y, c):
    return 4 * x + 2 * y + c


def all_gather(name, shards):
    nq = len(shards)

    def body(*refs):
        xs, outs = refs[:nq], refs[nq:2 * nq]
        send_sems, recv_sems, local_sems = refs[2 * nq:2 * nq + 3]
        bufs = refs[2 * nq + 3:]
        x, y, cc = _coords()
        me, sibling = (x, y, cc), (x, y, 1 - cc)
        chips = [(1 - x, y), (x, 1 - y), (1 - x, 1 - y)]

        def slot(q, px, py, pc):
            return outs[q].at[_linear(px, py, pc)]

        def copy(q, k, block, to, src=None):
            return pltpu.make_async_remote_copy(
                src_ref=slot(q, *block) if src is None else src, dst_ref=slot(q, *block),
                send_sem=send_sems.at[7 * q + k], recv_sem=recv_sems.at[7 * q + k], device_id=to, device_id_type=MESH)

        loads = [pltpu.make_async_copy(xs[q], bufs[q], local_sems.at[q]) for q in range(nq)]
        for cp in loads:
            cp.start()
        for cp in loads:
            cp.wait()
        mine = [pltpu.make_async_copy(bufs[q], slot(q, *me), local_sems.at[q]) for q in range(nq)]
        first = []
        for q in range(nq):
            first.append(copy(q, 0, me, sibling, src=bufs[q]))
            first += [copy(q, 1 + j, me, (*chip, cc), src=bufs[q]) for j, chip in enumerate(chips)]
        for cp in mine + first:
            cp.start()
        passed = []
        for q in range(nq):
            for j, chip in enumerate(chips):
                copy(q, 1 + j, (*chip, cc), me).wait_recv()
                passed.append(copy(q, 4 + j, (*chip, cc), sibling))
                passed[-1].start()
        for q in range(nq):
            copy(q, 0, sibling, me).wait_recv()
            for j, chip in enumerate(chips):
                copy(q, 4 + j, (*chip, 1 - cc), me).wait_recv()
        for cp in first + passed:
            cp.wait_send()
        for cp in mine:
            cp.wait()

    return pl.pallas_call(
        body, name=name, in_specs=[HBM_SPEC] * nq, out_specs=[HBM_SPEC] * nq,
        out_shape=[jax.ShapeDtypeStruct((NDEV,) + s.shape, s.dtype) for s in shards],
        scratch_shapes=[pltpu.SemaphoreType.DMA((7 * nq,)), pltpu.SemaphoreType.DMA((7 * nq,)),
                        pltpu.SemaphoreType.DMA((nq,))] + [pltpu.VMEM(s.shape, s.dtype) for s in shards],
    )(*shards)


NCHIP = 4


SEM_SPEC = pl.BlockSpec(memory_space=pltpu.SEMAPHORE)
EFFECT = pltpu.SideEffectType.DATAFLOW_SIDE_EFFECTING


def _peer(x, y, cc, k):
    fx, fy, fc = (k >> 2) & 1, (k >> 1) & 1, k & 1
    return (x + fx - 2 * fx * x, y + fy - 2 * fy * y, cc + fc - 2 * fc * cc)


def gather_plan(srcs, lands, coords):
    x, y, cc = coords
    me = _linear(x, y, cc)
    return [(s, l.at[me], _peer(x, y, cc, k)) for s, l in zip(srcs, lands) for k in range(1, NDEV)]


def near_plan(srcs, lands, coords):
    x, y, cc = coords
    me = _linear(x, y, cc)
    peers = [(x, y, 1 - cc)] + [_peer(x, y, cc, 2 * k) for k in range(1, NCHIP)]
    return [(s, l.at[me], p) for s, l in zip(srcs, lands) for p in peers]


def pass_on_plan(srcs, lands, coords):
    x, y, cc = coords
    out = []
    for l in srcs:
        for k in range(1, NCHIP):
            px, py, _ = _peer(x, y, cc, 2 * k)
            slot = _linear(px, py, cc)
            out.append((l.at[slot], l.at[slot], (x, y, 1 - cc)))
    return out


def pair_plan(srcs, lands, coords):
    x, y, cc = coords
    return [(s.at[2 * chip + 1 - cc], l.at[chip], (x, y, 1 - cc)) for s, l in zip(srcs, lands) for chip in range(NCHIP)]


def chip_plan(srcs, lands, coords):
    x, y, cc = coords
    out = []
    for s, l in zip(srcs, lands):
        for k in range(1, NCHIP):
            px, py, _ = _peer(x, y, cc, 2 * k)
            out.append((s.at[2 * px + py], l.at[k - 1], (px, py, cc)))
    return out


def _remote(copy, i, send_sems, recv_sems):
    src, dst, dev = copy
    return pltpu.make_async_remote_copy(src_ref=src, dst_ref=dst, send_sem=send_sems.at[i], recv_sem=recv_sems.at[i],
                                        device_id=dev, device_id_type=MESH)


def exchange_start(name, plan, ncopy, srcs, land_shapes, deps=()):
    ns, nl, nd = len(srcs), len(land_shapes), len(deps)

    def body(*refs):
        s, l = refs[:ns], refs[ns:ns + nl]
        send_sems, recv_sems = refs[ns + nl + nd], refs[ns + nl + nd + 1]
        token = refs[-1]
        for i, cp in enumerate(plan(s, l, _coords())):
            _remote(cp, i, send_sems, recv_sems).start()
        token[...] = jnp.zeros_like(token)

    hbm = lambda a: pltpu.with_memory_space_constraint(a, pltpu.HBM)
    lands = [lax.empty(shp, dt) for shp, dt in land_shapes]
    thru = [pltpu.HBM(a.shape, a.dtype) for a in list(srcs) + lands]
    outs = pl.pallas_call(
        body, name=name,
        in_specs=[HBM_SPEC] * (ns + nl) + [ANY_SPEC] * nd,
        out_specs=(SEM_SPEC, SEM_SPEC, *[HBM_SPEC] * (ns + nl), pl.BlockSpec(memory_space=pltpu.VMEM)),
        out_shape=(pltpu.SemaphoreType.DMA((ncopy,)), pltpu.SemaphoreType.DMA((ncopy,)), *thru,
                   jax.ShapeDtypeStruct((8, LANE), F32)),
        input_output_aliases={i: 2 + i for i in range(ns + nl)},
        compiler_params=pltpu.CompilerParams(has_side_effects=EFFECT),
    )(*[hbm(a) for a in srcs], *[hbm(a) for a in lands], *deps)
    return outs[0], outs[1], list(outs[2:2 + ns]), list(outs[2 + ns:2 + ns + nl]), outs[-1]


def exchange_wait(name, plan, started, after, place_own=False):
    send_sems, recv_sems, srcs, lands, _ = started
    ns, nl = len(srcs), len(lands)

    def body(*refs):
        s, l = refs[:ns], refs[ns:ns + nl]
        send_sems, recv_sems = refs[ns + nl], refs[ns + nl + 1]
        l_out = refs[2 * ns + nl + 3:2 * ns + 2 * nl + 3]
        scratch = refs[2 * ns + 2 * nl + 3:]
        copies = [_remote(cp, i, send_sems, recv_sems) for i, cp in enumerate(plan(s, l, _coords()))]
        if place_own:
            me = _linear(*_coords())
            local_sems, bufs = scratch[0], scratch[1:]
            loads = [pltpu.make_async_copy(s[q], bufs[q], local_sems.at[q]) for q in range(ns)]
            for cp in loads:
                cp.start()
            for cp in loads:
                cp.wait()
            stores = [pltpu.make_async_copy(bufs[q], l_out[q].at[me], local_sems.at[q]) for q in range(ns)]
            for cp in stores:
                cp.start()
        for cp in copies:
            cp.wait_recv()
        for cp in copies:
            cp.wait_send()
        if place_own:
            for cp in stores:
                cp.wait()

    scratch_shapes = []
    if place_own:
        scratch_shapes = [pltpu.SemaphoreType.DMA((ns,))] + [pltpu.VMEM(a.shape, a.dtype) for a in srcs]
    outs = pl.pallas_call(
        body, name=name,
        in_specs=[HBM_SPEC] * (ns + nl) + [SEM_SPEC, SEM_SPEC, ANY_SPEC],
        out_specs=[HBM_SPEC] * (ns + nl),
        out_shape=[pltpu.HBM(a.shape, a.dtype) for a in srcs + lands],
        input_output_aliases={i: i for i in range(ns + nl)},
        scratch_shapes=scratch_shapes,
        compiler_params=pltpu.CompilerParams(has_side_effects=EFFECT),
    )(*srcs, *lands, send_sems, recv_sems, after)
    return list(outs[:ns]), list(outs[ns:])


def pair_sum(name, g, recv):
    _, r, c = g.shape
    tr = _rows_tile(r, 512)

    def body(g_ref, r_ref, o_ref):
        own = jnp.where(lax.axis_index("c") == 0, g_ref[0, 0], g_ref[0, 1])
        o_ref[0] = (own.astype(F32) + r_ref[0].astype(F32)).astype(o_ref.dtype)

    return pl.pallas_call(
        body, name=name, grid=(NCHIP, r // tr),
        in_specs=[pl.BlockSpec((1, 2, tr, c), lambda k, i: (k, 0, i, 0)), pl.BlockSpec((1, tr, c), lambda k, i: (k, i, 0))],
        out_specs=pl.BlockSpec((1, tr, c), lambda k, i: (k, i, 0)),
        out_shape=jax.ShapeDtypeStruct((NCHIP, r, c), g.dtype), compiler_params=_params(("parallel", "parallel")),
    )(g.reshape(NCHIP, 2, r, c), recv)


def chip_sum(name, partial, recv):
    _, r, c = partial.shape
    tr = _rows_tile(r, 512)

    def body(p_ref, r_ref, o_ref):
        chip = 2 * lax.axis_index("x") + lax.axis_index("y")
        own = p_ref[0]
        for k in range(1, NCHIP):
            own = jnp.where(chip == k, p_ref[k], own)
        acc = own.astype(F32)
        for k in range(NCHIP - 1):
            acc = acc + r_ref[k].astype(F32)
        o_ref[...] = acc

    return pl.pallas_call(
        body, name=name, grid=(r // tr,),
        in_specs=[pl.BlockSpec((NCHIP, tr, c), lambda i: (0, i, 0)), pl.BlockSpec((NCHIP - 1, tr, c), lambda i: (0, i, 0))],
        out_specs=pl.BlockSpec((tr, c), lambda i: (i, 0)),
        out_shape=jax.ShapeDtypeStruct((r, c), F32), compiler_params=_params(("parallel",)),
    )(partial, recv)


def _block_diag(w, rows_per, cols_per):
    w = w.reshape(NBLK, 8, rows_per, cols_per)
    eye = jnp.eye(8, dtype=w.dtype)
    out = w[:, :, :, None, :] * eye[None, :, None, :, None]
    return out.reshape(NBLK, 8 * rows_per, 8 * cols_per)


def _diag_blocks(wd, rows_per, cols_per):
    wd = wd.reshape(NBLK, 8, rows_per, 8, cols_per)
    idx = jnp.arange(8)
    return wd[:, idx, :, idx, :].transpose(1, 0, 2, 3).reshape(NG, rows_per, cols_per)


def _pad_rows(v, mult):
    n = v.shape[0]
    return jnp.pad(v, (0, (-n) % mult))


def kernel(x, c, w_ada, b_ada, norm1_g, w_in, conv_w, conv_b, conv_ln_g, conv_ln_b, conv_proj, ssm_a_re, ssm_a_im, ssm_b_re, ssm_b_im, ssm_c_re, ssm_c_im, ssm_d, ssm_log_dt, ssm_glu, w_out, norm2_g, w_ffn_in, w_ffn_out, final_g, loss_target, m_w_ada, m_b_ada, m_norm1_g, m_w_in, m_conv_w, m_conv_b, m_conv_ln_g, m_conv_ln_b, m_conv_proj, m_ssm_a_re, m_ssm_a_im, m_ssm_b_re, m_ssm_b_im, m_ssm_c_re, m_ssm_c_im, m_ssm_d, m_ssm_log_dt, m_ssm_glu, m_w_out, m_norm2_g, m_w_ffn_in, m_w_ffn_out, m_final_g, v_w_ada, v_b_ada, v_norm1_g, v_w_in, v_conv_w, v_conv_b, v_conv_ln_g, v_conv_ln_b, v_conv_proj, v_ssm_a_re, v_ssm_a_im, v_ssm_b_re, v_ssm_b_im, v_ssm_c_re, v_ssm_c_im, v_ssm_d, v_ssm_log_dt, v_ssm_glu, v_w_out, v_norm2_g, v_w_ffn_in, v_w_ffn_out, v_final_g):
    me = _linear(*_coords())
    xs = x[0]
    tgt = loss_target[0]
    seq = xs.shape[0]

    flat = lambda g: g.reshape(NDEV * g.shape[1], g.shape[2])
    c_all, cw_g = all_gather("gather_c_conv_w", [c, conv_w[0]])
    w_in_s = w_in[0].T.astype(BF16)
    mids = [p.astype(BF16) for p in (conv_proj[0].T, ssm_glu[0].T, w_out[0])]
    ffns = [p.astype(BF16) for p in (w_ffn_in[0].T, w_ffn_out[0])]
    zone = lambda p: ((NDEV,) + p.shape, p.dtype)
    in_go = exchange_start("gather_in_start", near_plan, NCHIP, [w_in_s], [zone(w_in_s)], deps=[c_all])
    mids_go = exchange_start("gather_mid_start", gather_plan, 7 * len(mids), mids, [zone(p) for p in mids],
                             deps=[in_go[4]])
    ffns_go = exchange_start("gather_ffn_start", gather_plan, 7 * len(ffns), ffns, [zone(p) for p in ffns],
                             deps=[mids_go[4]])

    ncol = w_ada.shape[2]
    c_all = c_all.reshape(NDEV, D)
    b_cols = lax.dynamic_slice_in_dim(b_ada, me * ncol, ncol, axis=1)
    act_all, mod_cols = mod_fwd(c_all, w_ada[0], b_cols)
    (mod_all,) = all_gather("gather_mod", [mod_cols])
    mod = lax.dynamic_index_in_dim(mod_all, me, axis=1, keepdims=False).reshape(NMOD, D)
    sh1, sc1, g1, sh2, sc2, g2 = [mod[q:q + 1] for q in range(NMOD)]

    expand = jnp.repeat(jnp.eye(NG, dtype=F32), NP, axis=0)
    a_re_c, a_im_c = ssm_a_re.reshape(NST, 1), ssm_a_im.reshape(NST, 1)
    ldt_c = ssm_log_dt.reshape(NG, 1)
    b_re_r, b_im_r = ssm_b_re.reshape(NST, GH), ssm_b_im.reshape(NST, GH)
    e_re, e_im, bb_re, bb_im = disc_fwd(a_re_c, a_im_c, ldt_c, b_re_r, b_im_r, expand)
    e_re_b, e_im_b = e_re.reshape(NBLK, 1, SB), e_im.reshape(NBLK, 1, SB)
    bb_re_g, bb_im_g = bb_re.reshape(NG, NP, GH), bb_im.reshape(NG, NP, GH)
    wbt_re = _block_diag(bb_re_g, NP, GH)
    wbt_im = _block_diag(bb_im_g, NP, GH)
    wb_re, wb_im = wbt_re.transpose(0, 2, 1), wbt_im.transpose(0, 2, 1)
    wct = jnp.concatenate([_block_diag(ssm_c_re[0], GH, NP), -_block_diag(ssm_c_im[0], GH, NP)], axis=2)
    wc = wct.transpose(0, 2, 1)
    to_b = lambda a: a.astype(BF16)
    dvec = ssm_d.reshape(NBLK, 1, CB)

    n1g = norm1_g

    def f_norm1(xv, g, sc, sh):
        _, xh = _rms_stats(xv)
        return [xh * g * (1.0 + sc) + sh], []

    (h1,) = rowwise("norm1", f_norm1, [xs], [n1g, sc1, sh1], [(D, BF16)], [], 512, deps=[ffns_go[4]])
    _, (w_in_land,) = exchange_wait("gather_in_wait", near_plan, in_go, h1, place_own=True)
    pass_go = exchange_start("gather_in_pass_start", pass_on_plan, NCHIP - 1, [w_in_land], [])
    (w_in_g,), _ = exchange_wait("gather_in_pass_wait", pass_on_plan, pass_go, pass_go[4])
    w_in_t = flat(w_in_g)
    z = mm("mm_in", h1, w_in_t, "nt", tiles=(2048, CW, 1024), b_rot=Z_ROT)

    conv_w_full = cw_g.transpose(1, 0, 2).reshape(KC, CW)
    w32 = jnp.pad(conv_w_full, ((0, HALO - KC), (0, 0)))
    yc, s_act = conv_fwd(z, w32, conv_b, conv_ln_g, conv_ln_b)
    conv_proj_t, ssm_glu_t, w_out_f = [
        flat(g) for g in exchange_wait("gather_mid_wait", gather_plan, mids_go, s_act, place_own=True)[1]]
    y_conv = mm("mm_conv_proj", s_act, conv_proj_t, "nt")

    xs_re, xs_im, ypre, gl = ssm_fwd(z, to_b(wb_re), to_b(wb_im), to_b(wc), e_re_b, e_im_b, dvec)
    n_mrg = D // MRG_BLK

    def ep_merge(accs, yc_v, glc, gls):
        za, zb = accs
        return [_sig(glc) * yc_v + _sig(gls) * (za * _sig(zb)), jnp.concatenate([za, zb], axis=1)]

    merged, z2_pair = mm_ep("mm_ssm_glu", gl, ssm_glu_t, 2, lambda j, q: j + q * n_mrg, ep_merge,
                            [(y_conv, 1, 0), (z, 1, 0), (z, 1, n_mrg)], [(D, BF16, 1), (2 * D, BF16, 2)],
                            (512, MRG_BLK, SW))
    o1 = mm("mm_out", merged, w_out_f, "nn")

    def f_norm2(xv, o1v, g1v, g, sc, sh):
        x1v = xv + g1v * o1v
        _, xh = _rms_stats(x1v)
        return [x1v, xh * g * (1.0 + sc) + sh], []

    x1, h2 = rowwise("norm2", f_norm2, [xs, o1], [g1, norm2_g, sc2, sh2], [(D, F32), (D, BF16)], [], 512)
    w_ffn_in_t, w_ffn_out_f = [
        flat(g) for g in exchange_wait("gather_ffn_wait", gather_plan, ffns_go, h2, place_own=True)[1]]
    ffn_tiles = (512, FFN_BLK, 1024)
    n_ffn_blk = FH // FFN_BLK
    pair_map = lambda t: t // 2 + (t % 2) * n_ffn_blk

    def ep_swiglu(accs):
        fg, fu = accs
        return [fg * _sig(fg) * fu, jnp.concatenate([fg, fu], axis=1)]

    act, f_pair = mm_ep("mm_ffn_in", h2, w_ffn_in_t, 2, lambda j, q: j + q * n_ffn_blk, ep_swiglu, [],
                        [(FH, BF16, 1), (2 * FH, BF16, 2)], ffn_tiles)
    o2 = mm("mm_ffn_out", act, w_ffn_out_f, "nn")

    fg_row = final_g.reshape(1, D)

    def f_final(x1v, o2v, tv, g2v, fg):
        x2v = x1v + g2v * o2v
        r, xh = _rms_stats(x2v)
        yv = xh * fg
        err = yv - tv
        loss = jnp.sum(_colsum(err * err), axis=1, keepdims=True) * (0.5 / D)
        dy = err * (1.0 / D)
        dx2 = _rms_bwd(dy * fg, xh, r)
        return ([dx2, g2v * dx2],
                [jnp.broadcast_to(loss, (1, LANE)), _colsum(dy * xh), _colsum(dx2 * o2v)])

    dx2, do2, loss_l, d_final_g, d_g2 = rowwise(
        "final", f_final, [x1, o2, tgt], [g2, fg_row], [(D, F32), (D, BF16)], [LANE, D, D], 256)

    g_ffn_out = mm("mm_g_ffn_out", act, do2, "tn", BF16, tiles=(FFN_BLK, 1024, 1024))

    def ep_dswiglu(accs, fp):
        (da,) = accs
        fg, fu = fp[:, 0:FFN_BLK].astype(F32), fp[:, FFN_BLK:2 * FFN_BLK].astype(F32)
        sg = _sig(fg)
        return [jnp.concatenate([da * fu * (sg * (1.0 + fg * (1.0 - sg))), da * (fg * sg)], axis=1)]

    (df,) = mm_ep("mm_dact", do2, w_ffn_out_f, 1, lambda j, q: j, ep_dswiglu, [(f_pair, 2, 0)],
                  [(2 * FH, BF16, 2)], ffn_tiles)
    dh2 = mm("mm_dh2", df, w_ffn_in_t, "nn", tiles=(1024, 1024, FFN_BLK), b_rot=pair_map)
    g_ffn_in_t = mm("mm_g_ffn_in", df, h2, "tn", BF16, tiles=(FFN_BLK, 1024, 1024), o_rot=pair_map)

    def pair_go(tag, grads_t, deps=()):
        srcs = [g.reshape(NDEV, -1, D) for g in grads_t]
        return exchange_start("pair_" + tag + "_start", pair_plan, NCHIP * len(srcs), srcs,
                              [((NCHIP,) + s.shape[1:], s.dtype) for s in srcs], deps)

    def chip_go(tag, names, pair_started, after):
        own, from_sibling = exchange_wait("pair_" + tag + "_wait", pair_plan, pair_started, after)
        partials = [pair_sum("pair_sum_" + n, g, r) for n, g, r in zip(names, own, from_sibling)]
        return exchange_start("chip_" + tag + "_start", chip_plan, (NCHIP - 1) * len(partials), partials,
                              [((NCHIP - 1,) + p.shape[1:], p.dtype) for p in partials])

    def chip_done(tag, names, chip_started, after):
        partials, from_chips = exchange_wait("chip_" + tag + "_wait", chip_plan, chip_started, after)
        return [chip_sum("chip_sum_" + n, p, r) for n, p, r in zip(names, partials, from_chips)]

    pair_ffn = pair_go("ffn", [g_ffn_out, g_ffn_in_t])

    def f_dnorm2(dh, x1v, dx2v, o1v, g, sc, g1v):
        r, xh = _rms_stats(x1v)
        dxh = dh * (1.0 + sc) * g
        dx1 = dx2v + _rms_bwd(dxh, xh, r)
        return ([dx1, g1v * dx1],
                [_colsum(dh * xh * g), _colsum(dh), _colsum(dh * (1.0 + sc) * xh), _colsum(dx1 * o1v)])

    dx1, do1, d_sc2, d_sh2, d_n2g, d_g1 = rowwise(
        "dnorm2", f_dnorm2, [dh2, x1, dx2, o1], [norm2_g, sc2, g1], [(D, F32), (D, BF16)], [D, D, D, D], 256,
        deps=[pair_ffn[4]])

    dmerged = mm("mm_dmerged", do1, w_out_f, "nt")
    g_out = mm("mm_g_out", merged, do1, "tn", BF16)
    chip_ffn = chip_go("ffn", ("w_ffn_out", "w_ffn_in"), pair_ffn, g_out)

    def f_dmerge(dm, yc_v, *rest):
        pairs, (glc, gls) = rest[:n_mrg], rest[n_mrg:]
        za = jnp.concatenate([p[:, 0:MRG_BLK] for p in pairs], axis=1).astype(F32)
        zb = jnp.concatenate([p[:, MRG_BLK:2 * MRG_BLK] for p in pairs], axis=1).astype(F32)
        sc_ = _sig(glc)
        ss_ = _sig(gls)
        sb_ = _sig(zb)
        dys = dm * ss_
        dz2 = jnp.concatenate([dys * sb_, dys * za * sb_ * (1.0 - sb_)], axis=1)
        dgl = jnp.concatenate([dm * yc_v * sc_ * (1.0 - sc_), dm * (za * sb_) * ss_ * (1.0 - ss_)], axis=1)
        return [dm * sc_, dz2, dgl], []

    dyconv, dz2, dz = rowwise(
        "dmerge", f_dmerge,
        [dmerged, y_conv] + [(z2_pair, 2 * MRG_BLK, j) for j in range(n_mrg)] + [(z, D, 0), (z, D, 1)],
        [], [(D, BF16), (2 * D, BF16), (2 * D, BF16, ZW, 0)], [], 256, deps=[chip_ffn[4]])

    g_conv_proj_t = mm("mm_g_conv_proj", dyconv, s_act, "tn", BF16)
    dgl = mm("mm_dgl", dz2, ssm_glu_t, "nn")
    g_ssm_glu_t = mm("mm_g_ssm_glu", dz2, gl, "tn", BF16)
    pair_mid = pair_go("mid", [g_out, g_conv_proj_t, g_ssm_glu_t])
    ds = mm("mm_ds", dyconv, conv_proj_t, "nn", deps=[pair_mid[4]])
    dz, d_lng, d_lnb, d_cb, d_cw32 = conv_bwd(ds, yc, z, w32, conv_ln_g, conv_ln_b, dz)
    dz, d_d, d_ar, d_ai, d_wb_re, d_wb_im, d_wc = ssm_bwd(
        dgl, ypre, z, xs_re, xs_im, to_b(wbt_re), to_b(wbt_im), to_b(wct), e_re_b, e_im_b, dvec, dz)
    chip_mid = chip_go("mid", ("w_out", "conv_proj", "ssm_glu"), pair_mid, dz)

    d_bb_re = _diag_blocks(d_wb_re.transpose(0, 2, 1), NP, GH).reshape(NST, GH)
    d_bb_im = _diag_blocks(d_wb_im.transpose(0, 2, 1), NP, GH).reshape(NST, GH)
    d_wct = d_wc.transpose(0, 2, 1)
    d_c_re = _diag_blocks(d_wct[:, :, 0:SB], GH, NP)
    d_c_im = -_diag_blocks(d_wct[:, :, SB:2 * SB], GH, NP)
    d_a_re, d_a_im, d_ldt, d_b_re, d_b_im = disc_bwd(
        a_re_c, a_im_c, ldt_c, b_re_r, b_im_r, expand, d_ar.reshape(NST, 1), d_ai.reshape(NST, 1), d_bb_re, d_bb_im)

    small_local = [jnp.concatenate([d_g1, d_sh2, d_sc2, d_g2], axis=1).reshape(-1), d_cw32[0:KC].reshape(-1),
                   d_cb.reshape(-1), d_lng.reshape(-1), d_lnb.reshape(-1), d_a_re.reshape(-1), d_a_im.reshape(-1),
                   d_b_re.reshape(-1), d_b_im.reshape(-1), d_c_re.reshape(-1), d_c_im.reshape(-1), d_d.reshape(-1),
                   d_ldt.reshape(-1), d_n2g.reshape(-1), d_final_g.reshape(-1), loss_l[0, 0:1]]
    small_sizes = [v.shape[0] for v in small_local]
    small_pack = _pad_rows(jnp.concatenate(small_local), 256 * LANE).reshape(-1, LANE)
    small_go = exchange_start("gather_small_start", gather_plan, NDEV - 1, [small_pack],
                              [((NDEV,) + small_pack.shape, F32)], deps=[chip_mid[4]])

    dh1 = mm("mm_dh1", dz, w_in_t, "nn", tiles=(2048, 1024, CW), b_rot=Z_ROT, deps=[small_go[4]])
    g_in_t = mm("mm_g_in", dz, h1, "tn", BF16, tiles=(CW, 1024, 2048), o_rot=Z_ROT)
    pair_in = pair_go("in", [g_in_t])

    def f_dnorm1(dh, xv, dx1v, g, sc):
        r, xh = _rms_stats(xv)
        dxh = dh * (1.0 + sc) * g
        return ([dx1v + _rms_bwd(dxh, xh, r)],
                [_colsum(dh * xh * g), _colsum(dh), _colsum(dh * (1.0 + sc) * xh)])

    grad_x, d_sc1, d_sh1, d_n1g = rowwise(
        "dnorm1", f_dnorm1, [dh1, xs, dx1], [n1g, sc1], [(D, F32)], [D, D, D], 256, deps=[pair_in[4]])
    chip_in = chip_go("in", ("w_in",), pair_in, grad_x)

    weights = {
        "w_ada": (w_ada, m_w_ada, v_w_ada), "b_ada": (b_ada, m_b_ada, v_b_ada), "norm1_g": (norm1_g, m_norm1_g, v_norm1_g),
        "w_in": (w_in, m_w_in, v_w_in), "conv_w": (conv_w, m_conv_w, v_conv_w), "conv_b": (conv_b, m_conv_b, v_conv_b),
        "conv_ln_g": (conv_ln_g, m_conv_ln_g, v_conv_ln_g), "conv_ln_b": (conv_ln_b, m_conv_ln_b, v_conv_ln_b),
        "conv_proj": (conv_proj, m_conv_proj, v_conv_proj), "ssm_a_re": (ssm_a_re, m_ssm_a_re, v_ssm_a_re),
        "ssm_a_im": (ssm_a_im, m_ssm_a_im, v_ssm_a_im), "ssm_b_re": (ssm_b_re, m_ssm_b_re, v_ssm_b_re),
        "ssm_b_im": (ssm_b_im, m_ssm_b_im, v_ssm_b_im), "ssm_c_re": (ssm_c_re, m_ssm_c_re, v_ssm_c_re),
        "ssm_c_im": (ssm_c_im, m_ssm_c_im, v_ssm_c_im), "ssm_d": (ssm_d, m_ssm_d, v_ssm_d),
        "ssm_log_dt": (ssm_log_dt, m_ssm_log_dt, v_ssm_log_dt), "ssm_glu": (ssm_glu, m_ssm_glu, v_ssm_glu),
        "w_out": (w_out, m_w_out, v_w_out), "norm2_g": (norm2_g, m_norm2_g, v_norm2_g),
        "w_ffn_in": (w_ffn_in, m_w_ffn_in, v_w_ffn_in), "w_ffn_out": (w_ffn_out, m_w_ffn_out, v_w_ffn_out),
        "final_g": (final_g, m_final_g, v_final_g),
    }
    order = list(weights)
    big = ("w_ada", "w_in", "conv_proj", "ssm_glu", "w_out", "w_ffn_in", "w_ffn_out")
    grads, delta, new_m, new_v = {}, {}, {}, {}

    def adam_big(n, g2d):
        wv, mv, vv = weights[n]
        shp = wv.shape
        d_, m_, v_ = adam("adam_" + n, wv.reshape(shp[-2:]), g2d, mv.reshape(shp[-2:]), vv.reshape(shp[-2:]))
        grads[n], delta[n], new_m[n], new_v[n] = g2d.reshape(shp), d_.reshape(shp), m_.reshape(shp), v_.reshape(shp)
        return d_

    gs_ffn_out, gs_ffn_in = chip_done("ffn", ("w_ffn_out", "w_ffn_in"), chip_ffn, chip_in[4])
    adam_big("w_ffn_out", gs_ffn_out)
    last = adam_big("w_ffn_in", gs_ffn_in.T)
    gs_out, gs_conv_proj, gs_ssm_glu = chip_done("mid", ("w_out", "conv_proj", "ssm_glu"), chip_mid, last)
    adam_big("w_out", gs_out)
    adam_big("conv_proj", gs_conv_proj.reshape(-1, CW).T)
    adam_big("ssm_glu", gs_ssm_glu.reshape(-1, SW).T)

    late_local = [d_sh1.reshape(-1), d_sc1.reshape(-1), d_n1g.reshape(-1)]
    late_pack = _pad_rows(jnp.concatenate(late_local), 16 * LANE).reshape(-1, LANE)
    (late_all,) = all_gather("gather_small_late", [late_pack])
    _, (small_all,) = exchange_wait("gather_small_wait", gather_plan, small_go, late_all, place_own=True)

    def unpack(vec, sizes):
        out, pos = [], 0
        for n in sizes:
            out.append(vec[pos:pos + n])
            pos += n
        return out

    g_sh1, g_sc1, g_n1g = unpack(sum_slots("sum_small_late", late_all).reshape(-1), [D, D, D])
    (g_mod_rest, g_cw_full, g_cb, g_lng, g_lnb, g_a_re, g_a_im, g_b_re, g_b_im, g_c_re, g_c_im, g_d, g_ldt,
     g_n2g, g_fg, loss_sum) = unpack(sum_slots("sum_small", small_all).reshape(-1), small_sizes)
    g_b_ada = jnp.concatenate([g_sh1, g_sc1, g_mod_rest])
    loss = loss_sum[0]
    dmod_all = jnp.concatenate([late_all.reshape(NDEV, -1)[:, 0:2 * D], small_all.reshape(NDEV, -1)[:, 0:4 * D]],
                               axis=1)
    g_w_ada = ada_grad(act_all, lax.dynamic_slice_in_dim(dmod_all, me * ncol, ncol, axis=1))
    ccol = conv_w.shape[2]
    g_conv_w = lax.dynamic_slice_in_dim(g_cw_full.reshape(KC, CW), me * ccol, ccol, axis=1)

    adam_big("w_ada", g_w_ada)
    grads.update({
        "b_ada": g_b_ada.reshape(b_ada.shape), "norm1_g": g_n1g.reshape(norm1_g.shape),
        "conv_w": g_conv_w[None], "conv_b": g_cb.reshape(conv_b.shape),
        "conv_ln_g": g_lng.reshape(conv_ln_g.shape), "conv_ln_b": g_lnb.reshape(conv_ln_b.shape),
        "ssm_a_re": g_a_re.reshape(ssm_a_re.shape),
        "ssm_a_im": g_a_im.reshape(ssm_a_im.shape), "ssm_b_re": g_b_re.reshape(ssm_b_re.shape),
        "ssm_b_im": g_b_im.reshape(ssm_b_im.shape), "ssm_c_re": g_c_re.reshape(ssm_c_re.shape),
        "ssm_c_im": g_c_im.reshape(ssm_c_im.shape), "ssm_d": g_d.reshape(ssm_d.shape),
        "ssm_log_dt": g_ldt.reshape(ssm_log_dt.shape),
        "norm2_g": g_n2g.reshape(norm2_g.shape),
        "final_g": g_fg.reshape(final_g.shape),
    })
    small = [n for n in order if n not in big]
    rows = lambda a: a.reshape(1, -1) if a.ndim == 1 else a
    small_out = adam_many("adam_small", [rows(weights[n][0]) for n in small], [rows(grads[n]) for n in small],
                          [rows(weights[n][1]) for n in small], [rows(weights[n][2]) for n in small])
    for q, n in enumerate(small):
        shp = weights[n][0].shape
        delta[n], new_m[n], new_v[n] = [small_out[t * len(small) + q].reshape(shp) for t in range(3)]

    (gs_in,) = chip_done("in", ("w_in",), chip_in, small_out[0])
    adam_big("w_in", gs_in.T)

    return (loss, grad_x[None], *[grads[n] for n in order], *[delta[n] for n in order],
            *[new_m[n] for n in order], *[new_v[n] for n in order])
```

```python
import functools
import math

import jax
import jax.numpy as jnp
from jax import lax
from jax.experimental import pallas as pl
from jax.experimental.pallas import tpu as pltpu

F32 = jnp.float32
BF16 = jnp.bfloat16

D = 1024
CW = 512
KC = 31
SW = 512
NG = 32
GH = 16
NP = 64
NST = NG * NP
FH = 2816
FFN_BLK = 1408
MRG_BLK = 512
NMOD = 6
NDEV = 8
EPS = 1e-6
CB = 128
SB = 512
NBLK = SW // CB
HALO = 32
ZW = 2 * CW + SW + 2 * D
Z_ROT = lambda j: jnp.where(j < 4, 3 + 2 * (j % 2) + j // 2, j - 4)
ZB_A, ZB_G, ZB_U = 4, 5, 6

ADAM_LR = 0.001
ADAM_B1 = 0.9
ADAM_B2 = 0.999
ADAM_EPS = 1e-08
ADAM_WD = 0.01
ADAM_STEP = 10

V7X_VMEM_BYTES = 64 * 1024 * 1024
VMEM_LIMIT = V7X_VMEM_BYTES - 8 * 1024 * 1024
LANE = 128
MESH = pl.DeviceIdType.MESH
ANY_SPEC = pl.BlockSpec(memory_space=pl.ANY)


def _params(sem=None, **kw):
    if sem is not None:
        kw["dimension_semantics"] = sem
    return pltpu.CompilerParams(vmem_limit_bytes=VMEM_LIMIT, **kw)


def _tile(n, most):
    best = None
    for t in range(LANE, most + 1, LANE):
        if n % t == 0:
            best = t
    if best is None:
        raise ValueError(f"no tile for {n}")
    return best


def _sig(x):
    return jax.nn.sigmoid(x)


def mm(name, a, b, mode, out_dtype=F32, tiles=None, b_rot=None, o_rot=None, deps=()):
    if mode == "nn":
        (m, k), (k2, n) = a.shape, b.shape
    elif mode == "nt":
        (m, k), (n, k2) = a.shape, b.shape
    else:
        (k, m), (k2, n) = a.shape, b.shape
    assert k == k2, (name, a.shape, b.shape)
    bm, bn, bk = tiles or (_tile(m, 1024), _tile(n, 1408), _tile(k, 1408 if k % 1408 == 0 else 1024))
    bm, bn, bk = min(bm, m), min(bn, n), min(bk, k)
    assert m % bm == 0 and n % bn == 0 and k % bk == 0, (name, m, n, k, bm, bn, bk)
    nk = k // bk
    rot = lambda idx, r: idx if r is None else r(idx)
    if mode == "nn":
        a_spec = pl.BlockSpec((bm, bk), lambda i, j, kk: (i, kk))
        b_spec = pl.BlockSpec((bk, bn), lambda i, j, kk: (rot(kk, b_rot), j))
        dims = (((1,), (0,)), ((), ()))
    elif mode == "nt":
        a_spec = pl.BlockSpec((bm, bk), lambda i, j, kk: (i, kk))
        b_spec = pl.BlockSpec((bn, bk), lambda i, j, kk: (rot(j, b_rot), kk))
        dims = (((1,), (1,)), ((), ()))
    else:
        assert b_rot is None
        a_spec = pl.BlockSpec((bk, bm), lambda i, j, kk: (kk, i))
        b_spec = pl.BlockSpec((bk, bn), lambda i, j, kk: (kk, j))
        dims = (((0,), (0,)), ((), ()))

    def body(a_ref, b_ref, *rest):
        o_ref, acc_ref = rest[-2:]
        kk = pl.program_id(2)

        @pl.when(kk == 0)
        def _():
            acc_ref[...] = jnp.zeros_like(acc_ref)

        acc_ref[...] += lax.dot_general(a_ref[...], b_ref[...], dims, preferred_element_type=F32)

        @pl.when(kk == nk - 1)
        def _():
            o_ref[...] = acc_ref[...].astype(o_ref.dtype)

    return pl.pallas_call(
        body, name=name,
        grid=(m // bm, n // bn, nk),
        in_specs=[a_spec, b_spec] + [ANY_SPEC] * len(deps),
        out_specs=pl.BlockSpec((bm, bn), lambda i, j, kk: (rot(i, o_rot), j)),
        out_shape=jax.ShapeDtypeStruct((m, n), out_dtype),
        scratch_shapes=[pltpu.VMEM((bm, bn), F32)],
        compiler_params=_params(("parallel", "parallel", "arbitrary")),
    )(a, b, *deps)


def mm_ep(name, a, b, n_acc, acc_block, epilogue, extras, outs, tiles, deps=(), b_kn=False, k_map=None,
          consts=(), sums=()):
    m, k = a.shape
    bm, bn, bk = tiles
    bm = min(bm, m)
    nj = outs[0][0] // (outs[0][2] * bn)
    nk = k // bk
    assert m % bm == 0 and k % bk == 0 and b.shape[0 if b_kn else 1] == k, (name, a.shape, b.shape, tiles)
    assert not sums or nj == 1, name
    ne, nc, no, ns, nd = len(extras), len(consts), len(outs), len(sums), len(deps)
    dims = (((1,), (0,)), ((), ())) if b_kn else (((1,), (1,)), ((), ()))
    kmap = (lambda kk: kk) if k_map is None else k_map

    def body(*refs):
        a_ref, b_refs = refs[0], refs[1:1 + n_acc]
        e_refs = refs[1 + n_acc:1 + n_acc + ne + nc]
        first_out = 1 + n_acc + ne + nc + nd
        o_refs = refs[first_out:first_out + no]
        s_refs = refs[first_out + no:first_out + no + ns]
        acc_refs = refs[first_out + no + ns:]
        av = a_ref[...]
        prods = [lax.dot_general(av, b_ref[...], dims, preferred_element_type=F32) for b_ref in b_refs]

        if ns:
            @pl.when((pl.program_id(0) == 0) & (pl.program_id(2) == 0))
            def _():
                for s_ref in s_refs:
                    s_ref[...] = jnp.zeros_like(s_ref)

        def finish(accs):
            res = epilogue(accs, *[e[...] for e in e_refs])
            tiles_out, sums_out = res if ns else (res, ())
            for o_ref, v in zip(o_refs, tiles_out):
                o_ref[...] = v.astype(o_ref.dtype)
            for s_ref, v in zip(s_refs, sums_out):
                s_ref[...] += v

        if nk == 1:
            finish(prods)
        else:
            kk = pl.program_id(2)

            @pl.when(kk == 0)
            def _():
                for acc_ref in acc_refs:
                    acc_ref[...] = jnp.zeros_like(acc_ref)

            for acc_ref, p in zip(acc_refs, prods):
                acc_ref[...] += p

            @pl.when(kk == nk - 1)
            def _():
                finish([acc_ref[...] for acc_ref in acc_refs])

    in_specs = [pl.BlockSpec((bm, bk), lambda i, j, kk: (i, kk))]
    if b_kn:
        in_specs += [pl.BlockSpec((bk, bn), functools.partial(lambda i, j, kk, q: (kmap(kk), acc_block(j, q)), q=q))
                     for q in range(n_acc)]
    else:
        in_specs += [pl.BlockSpec((bn, bk), functools.partial(lambda i, j, kk, q: (acc_block(j, q), kmap(kk)), q=q))
                     for q in range(n_acc)]
    in_specs += [pl.BlockSpec((bm, w * bn), functools.partial(lambda i, j, kk, off: (i, j + off), off=off))
                 for (_, w, off) in extras]
    in_specs += [pl.BlockSpec((1, bn), lambda i, j, kk: (0, j)) for _ in consts]
    in_specs += [ANY_SPEC] * nd
    out_specs = [pl.BlockSpec((bm, w * bn), lambda i, j, kk: (i, j)) for (_, _, w) in outs]
    out_specs += [pl.BlockSpec((1, w), lambda i, j, kk: (0, 0)) for w in sums]
    out_shape = [jax.ShapeDtypeStruct((m, cols), dt) for (cols, dt, _) in outs]
    out_shape += [jax.ShapeDtypeStruct((1, w), F32) for w in sums]
    return pl.pallas_call(
        body, name=name, grid=(m // bm, nj, nk),
        in_specs=in_specs, out_specs=out_specs, out_shape=out_shape,
        scratch_shapes=[pltpu.VMEM((bm, bn), F32)] * (n_acc if nk > 1 else 0),
        compiler_params=_params(("arbitrary",) * 3 if sums else ("parallel", "parallel", "arbitrary")),
    )(a, *[b] * n_acc, *[e[0] for e in extras], *consts, *deps)


def rowwise(name, fn, rows, consts, out_rows, out_sums, ts, alias=None, deps=()):
    rows = [r if isinstance(r, tuple) else (r, r.shape[1], 0) for r in rows]
    out_rows = [o if len(o) == 4 else (o[0], o[1], o[0], 0) for o in out_rows]
    s = rows[0][0].shape[0]
    nt = s // ts
    nr, nc, no, ns = len(rows), len(consts), len(out_rows), len(out_sums)
    in_specs = [pl.BlockSpec((ts, w), functools.partial(lambda i, cb: (i, cb), cb=cb)) for (_, w, cb) in rows]
    in_specs += [pl.BlockSpec(c.shape, lambda i: (0, 0)) for c in consts]
    operands = [r[0] for r in rows] + list(consts)
    aliases = {}
    if alias is not None:
        in_specs.append(pl.BlockSpec(memory_space=pl.ANY))
        operands.append(alias[0])
        aliases = {nr + nc: alias[1]}
    in_specs += [ANY_SPEC] * len(deps)
    operands += list(deps)
    out_shape = [jax.ShapeDtypeStruct((s, tw), dt) for (_, dt, tw, _) in out_rows]
    out_shape += [jax.ShapeDtypeStruct((1, w), F32) for w in out_sums]
    out_specs = [pl.BlockSpec((ts, w), functools.partial(lambda i, cb: (i, cb), cb=cb)) for (w, _, _, cb) in out_rows]
    out_specs += [pl.BlockSpec((1, w), lambda i: (0, 0)) for w in out_sums]
    n_in = len(operands)

    def body(*refs):
        ins, outs = refs[:nr + nc], refs[n_in:]
        i = pl.program_id(0)
        ro, so = fn(*[r[...] for r in ins])
        for q in range(no):
            outs[q][...] = ro[q].astype(outs[q].dtype)
        if ns:
            @pl.when(i == 0)
            def _():
                for q in range(ns):
                    outs[no + q][...] = jnp.zeros_like(outs[no + q])

            for q in range(ns):
                outs[no + q][...] += so[q]

    return pl.pallas_call(
        body, name=name, grid=(nt,),
        in_specs=in_specs, out_specs=out_specs, out_shape=out_shape, input_output_aliases=aliases,
        compiler_params=_params(("arbitrary",) if ns else ("parallel",)),
    )(*operands)


def _colsum(v):
    return jnp.sum(v, axis=0, keepdims=True)


def _rms_stats(xv):
    r = lax.rsqrt(jnp.mean(xv * xv, axis=-1, keepdims=True) + EPS)
    return r, xv * r


def _rms_bwd(dxhat, xhat, r):
    return r * (dxhat - xhat * jnp.mean(dxhat * xhat, axis=-1, keepdims=True))


def _gelu(v):
    k = math.sqrt(2.0 / math.pi)
    t = jnp.tanh(k * (v + 0.044715 * v * v * v))
    return 0.5 * v * (1.0 + t), t


def _gelu_grad(v, t):
    k = math.sqrt(2.0 / math.pi)
    return 0.5 * (1.0 + t) + 0.5 * v * (1.0 - t * t) * k * (1.0 + 3.0 * 0.044715 * v * v)


CONV_TS = 256
CONV_CH = 64


def _ln_fwd(yc, g, b):
    mu = jnp.mean(yc, axis=-1, keepdims=True)
    xc = yc - mu
    rstd = lax.rsqrt(jnp.mean(xc * xc, axis=-1, keepdims=True) + EPS)
    nhat = xc * rstd
    return nhat, rstd, nhat * g + b


SUBL = 8


def _shifted_copies(buf, sh, ts):
    for b in range(1, SUBL):
        sh[b - 1] = buf[pl.ds(b, ts + HALO - SUBL), :]


def _shifted(buf, sh, start):
    b = start % SUBL
    if b == 0:
        return buf[pl.ds(start, CONV_CH), :]
    return sh[b - 1, pl.ds(start - b, CONV_CH), :]


def conv_fwd(z, w32, cb, lg, lb):
    s = z.shape[0]
    ts = CONV_TS
    nt = s // ts
    hb = ts // HALO

    def body(a_ref, g_ref, ah_ref, gh_ref, w_ref, cb_ref, lg_ref, lb_ref, yc_ref, s_ref, ubuf, ush):
        i = pl.program_id(0)
        first = (i > 0).astype(F32)
        ubuf[0:HALO, :] = ah_ref[...] * _sig(gh_ref[...]) * first
        ubuf[HALO:HALO + ts, :] = a_ref[...] * _sig(g_ref[...])
        _shifted_copies(ubuf, ush, ts)
        for c0 in range(0, ts, CONV_CH):
            acc = jnp.zeros((CONV_CH, CW), F32)
            for k in range(KC):
                acc = acc + w_ref[k:k + 1, :] * _shifted(ubuf, ush, c0 + k + 2)
            yc = acc + cb_ref[...]
            yc_ref[c0:c0 + CONV_CH, :] = yc
            _, _, ln = _ln_fwd(yc, lg_ref[...], lb_ref[...])
            s_ref[c0:c0 + CONV_CH, :] = (ln * _sig(ln)).astype(s_ref.dtype)

    cur = lambda cbk: pl.BlockSpec((ts, CW), functools.partial(lambda i, q: (i, q), q=cbk))
    prev = lambda cbk: pl.BlockSpec((HALO, CW), functools.partial(lambda i, q: (jnp.maximum(i * hb - 1, 0), q), q=cbk))
    const = lambda a: pl.BlockSpec(a.shape, lambda i: (0, 0))
    return pl.pallas_call(
        body, name="conv_fwd", grid=(nt,),
        in_specs=[cur(ZB_A), cur(ZB_G), prev(ZB_A), prev(ZB_G), const(w32), const(cb), const(lg), const(lb)],
        out_specs=[pl.BlockSpec((ts, CW), lambda i: (i, 0)), pl.BlockSpec((ts, CW), lambda i: (i, 0))],
        out_shape=[jax.ShapeDtypeStruct((s, CW), F32), jax.ShapeDtypeStruct((s, CW), BF16)],
        scratch_shapes=[pltpu.VMEM((HALO + ts, CW), F32), pltpu.VMEM((SUBL - 1, ts + HALO - SUBL, CW), F32)],
        compiler_params=_params(("parallel",)),
    )(z, z, z, z, w32, cb, lg, lb)


def conv_bwd(ds, yc, z, w32, lg, lb, dz):
    s = z.shape[0]
    ts = CONV_TS
    nt = s // ts
    hb = ts // HALO
    last_hb = s // HALO - 1

    def ln_bwd(dsv, ycv, g, b):
        nhat, rstd, ln = _ln_fwd(ycv, g, b)
        sg = _sig(ln)
        dln = dsv * (sg * (1.0 + ln * (1.0 - sg)))
        dnh = dln * g
        dyc = rstd * (dnh - jnp.mean(dnh, axis=-1, keepdims=True)
                      - nhat * jnp.mean(dnh * nhat, axis=-1, keepdims=True))
        return dyc, dln, nhat

    def body(ds_ref, yc_ref, dsn_ref, ycn_ref, a_ref, g_ref, ah_ref, gh_ref, w_ref, lg_ref, lb_ref, dz_in,
             dz_ref, dlg_ref, dlb_ref, dcb_ref, dw_ref, dbuf, ubuf, dsh, ush):
        i = pl.program_id(0)

        @pl.when(i == 0)
        def _():
            dlg_ref[...] = jnp.zeros_like(dlg_ref)
            dlb_ref[...] = jnp.zeros_like(dlb_ref)
            dcb_ref[...] = jnp.zeros_like(dcb_ref)
            dw_ref[...] = jnp.zeros_like(dw_ref)

        lg, lb = lg_ref[...], lb_ref[...]
        dyc, dln, nhat = ln_bwd(ds_ref[...], yc_ref[...], lg, lb)
        dlg_ref[...] += _colsum(dln * nhat)
        dlb_ref[...] += _colsum(dln)
        dcb_ref[...] += _colsum(dyc)
        dbuf[0:ts, :] = dyc
        nxt = (i < nt - 1).astype(F32)
        dbuf[ts:ts + HALO, :] = ln_bwd(dsn_ref[...], ycn_ref[...], lg, lb)[0] * nxt
        first = (i > 0).astype(F32)
        ubuf[0:HALO, :] = ah_ref[...] * _sig(gh_ref[...]) * first
        ubuf[HALO:HALO + ts, :] = a_ref[...] * _sig(g_ref[...])
        _shifted_copies(dbuf, dsh, ts)
        _shifted_copies(ubuf, ush, ts)
        for c0 in range(0, ts, CONV_CH):
            du = jnp.zeros((CONV_CH, CW), F32)
            dyc_c = dbuf[c0:c0 + CONV_CH, :]
            for k in range(KC):
                du = du + w_ref[k:k + 1, :] * _shifted(dbuf, dsh, c0 + KC - 1 - k)
                dw_ref[k:k + 1, :] += _colsum(dyc_c * _shifted(ubuf, ush, c0 + k + 2))
            av = a_ref[c0:c0 + CONV_CH, :]
            sg = _sig(g_ref[c0:c0 + CONV_CH, :])
            dz_ref[c0:c0 + CONV_CH, 0:CW] = (du * sg).astype(dz_ref.dtype)
            dz_ref[c0:c0 + CONV_CH, CW:2 * CW] = (du * av * sg * (1.0 - sg)).astype(dz_ref.dtype)

    cur = lambda w, cbk: pl.BlockSpec((ts, w), functools.partial(lambda i, q: (i, q), q=cbk))
    prev = lambda cbk: pl.BlockSpec((HALO, CW), functools.partial(lambda i, q: (jnp.maximum(i * hb - 1, 0), q), q=cbk))
    nxt_spec = pl.BlockSpec((HALO, CW), lambda i: (jnp.minimum((i + 1) * hb, last_hb), 0))
    const = lambda a: pl.BlockSpec(a.shape, lambda i: (0, 0))
    acc = lambda r: pl.BlockSpec((r, CW), lambda i: (0, 0))
    return pl.pallas_call(
        body, name="conv_bwd", grid=(nt,),
        in_specs=[cur(CW, 0), cur(CW, 0), nxt_spec, nxt_spec, cur(CW, ZB_A), cur(CW, ZB_G), prev(ZB_A), prev(ZB_G),
                  const(w32), const(lg), const(lb), pl.BlockSpec(memory_space=pl.ANY)],
        out_specs=[pl.BlockSpec((ts, 2 * CW), lambda i: (i, ZB_A // 2)), acc(1), acc(1), acc(1), acc(HALO)],
        out_shape=[jax.ShapeDtypeStruct(dz.shape, dz.dtype), jax.ShapeDtypeStruct((1, CW), F32),
                   jax.ShapeDtypeStruct((1, CW), F32), jax.ShapeDtypeStruct((1, CW), F32),
                   jax.ShapeDtypeStruct((HALO, CW), F32)],
        scratch_shapes=[pltpu.VMEM((ts + HALO, CW), F32), pltpu.VMEM((HALO + ts, CW), F32)]
        + [pltpu.VMEM((SUBL - 1, ts + HALO - SUBL, CW), F32)] * 2,
        input_output_aliases={11: 0},
        compiler_params=_params(("arbitrary",)),
    )(ds, yc, ds, yc, z, z, z, z, w32, lg, lb, dz)


SSM_TS = 512
GRP = 8


def _cmul(ar, ai, br, bi):
    return ar * br - ai * bi, ar * bi + ai * br


def _scan_tables(ar, ai, reverse):
    n = ar.shape[1]
    row = lax.broadcasted_iota(jnp.int32, (GRP, n), 0)
    dist = (GRP - 1 - row) if reverse else row
    one_r = jnp.broadcast_to(ar, (GRP, n))
    one_i = jnp.broadcast_to(ai, (GRP, n))
    p2r, p2i = _cmul(one_r, one_i, one_r, one_i)
    p4r, p4i = _cmul(p2r, p2i, p2r, p2i)
    steps = []
    for sft, (pr, pi) in ((1, (one_r, one_i)), (2, (p2r, p2i)), (4, (p4r, p4i))):
        keep = dist >= sft
        steps.append((jnp.where(keep, pr, 0.0), jnp.where(keep, pi, 0.0)))
    cr, ci = one_r, one_i
    accr, acci = one_r, one_i
    for e in range(1, GRP):
        cr, ci = _cmul(cr, ci, one_r, one_i)
        accr = jnp.where(dist == e, cr, accr)
        acci = jnp.where(dist == e, ci, acci)
    return steps, (accr, acci)


def _scan_group(xr, xi, steps, carry_tab, cr, ci, reverse):
    for sft, (tr, ti) in zip((1, 2, 4), steps):
        amt = (GRP - sft) if reverse else sft
        sr = pltpu.roll(xr, amt, 0)
        si = pltpu.roll(xi, amt, 0)
        xr, xi = xr + tr * sr - ti * si, xi + tr * si + ti * sr
    pr, pi = carry_tab
    xr = xr + pr * cr - pi * ci
    xi = xi + pr * ci + pi * cr
    return xr, xi


def ssm_fwd(z, wb_re, wb_im, wc, e_re, e_im, dvec):
    s = z.shape[0]
    ts = SSM_TS
    nt = s // ts
    ucol0 = ZB_U * CW // CB

    def body(u_ref, wbr_ref, wbi_ref, wc_ref, er_ref, ei_ref, d_ref, xr_ref, xi_ref, y_ref, gl_ref, car_r, car_i):
        i = pl.program_id(1)

        @pl.when(i == 0)
        def _():
            car_r[...] = jnp.zeros_like(car_r)
            car_i[...] = jnp.zeros_like(car_i)

        u = u_ref[...]
        ub = u.astype(BF16)
        xr_ref[...] = jnp.dot(ub, wbr_ref[0], preferred_element_type=F32)
        xi_ref[...] = jnp.dot(ub, wbi_ref[0], preferred_element_type=F32)
        steps, ctab = _scan_tables(er_ref[0], ei_ref[0], False)

        def grp(r, carry):
            cr, ci = carry
            r0 = pl.multiple_of(r * GRP, GRP)
            xr, xi = _scan_group(xr_ref[pl.ds(r0, GRP), :], xi_ref[pl.ds(r0, GRP), :], steps, ctab, cr, ci, False)
            xr_ref[pl.ds(r0, GRP), :] = xr
            xi_ref[pl.ds(r0, GRP), :] = xi
            return (jnp.broadcast_to(xr[GRP - 1:GRP, :], (GRP, SB)), jnp.broadcast_to(xi[GRP - 1:GRP, :], (GRP, SB)))

        cr, ci = lax.fori_loop(0, ts // GRP, grp, (car_r[...], car_i[...]))
        car_r[...] = cr
        car_i[...] = ci
        y = (jnp.dot(xr_ref[...].astype(BF16), wc_ref[0, 0:SB, :], preferred_element_type=F32)
             + jnp.dot(xi_ref[...].astype(BF16), wc_ref[0, SB:2 * SB, :], preferred_element_type=F32)
             + d_ref[0] * u)
        y_ref[...] = y
        gl_ref[...] = _gelu(y)[0].astype(gl_ref.dtype)

    blk3 = lambda a: pl.BlockSpec((1,) + a.shape[1:], lambda j, i: (j, 0, 0))
    return pl.pallas_call(
        body, name="ssm_fwd", grid=(NBLK, nt),
        in_specs=[pl.BlockSpec((ts, CB), lambda j, i: (i, ucol0 + j)),
                  blk3(wb_re), blk3(wb_im), blk3(wc), blk3(e_re), blk3(e_im), blk3(dvec)],
        out_specs=[pl.BlockSpec((ts, SB), lambda j, i: (i, j)), pl.BlockSpec((ts, SB), lambda j, i: (i, j)),
                   pl.BlockSpec((ts, CB), lambda j, i: (i, j)), pl.BlockSpec((ts, CB), lambda j, i: (i, j))],
        out_shape=[jax.ShapeDtypeStruct((s, NST), F32), jax.ShapeDtypeStruct((s, NST), F32),
                   jax.ShapeDtypeStruct((s, SW), F32), jax.ShapeDtypeStruct((s, SW), BF16)],
        scratch_shapes=[pltpu.VMEM((GRP, SB), F32), pltpu.VMEM((GRP, SB), F32)],
        compiler_params=_params(("parallel", "arbitrary")),
    )(z, wb_re, wb_im, wc, e_re, e_im, dvec)


def ssm_bwd(dgl, ypre, z, xs_re, xs_im, wbt_re, wbt_im, wct, e_re, e_im, dvec, dz):
    s = z.shape[0]
    ts = SSM_TS
    nt = s // ts
    ucol0 = ZB_U * CW // CB
    tn_dims = (((0,), (0,)), ((), ()))

    def body(dgl_ref, y_ref, u_ref, xr_ref, xi_ref, wbtr_ref, wbti_ref, wct_ref, er_ref, ei_ref, d_ref, dz_in,
             du_ref, dd_ref, dar_ref, dai_ref, dwbr_ref, dwbi_ref, dwc_ref,
             lr_ref, li_ref, car_r, car_i, acc_r, acc_i):
        i = pl.program_id(1)

        @pl.when(i == 0)
        def _():
            for ref in (car_r, car_i, acc_r, acc_i, dd_ref, dwbr_ref, dwbi_ref, dwc_ref):
                ref[...] = jnp.zeros_like(ref)

        u = u_ref[...]
        y = y_ref[...]
        dy = dgl_ref[...] * _gelu_grad(y, _gelu(y)[1])
        dd_ref[0] += _colsum(dy * u)
        dyb = dy.astype(BF16)
        dxo = jnp.dot(dyb, wct_ref[0], preferred_element_type=F32)
        lr_ref[...] = dxo[:, 0:SB]
        li_ref[...] = dxo[:, SB:2 * SB]
        steps, ctab = _scan_tables(er_ref[0], -ei_ref[0], True)
        row = lax.broadcasted_iota(jnp.int32, (GRP, SB), 0)

        def grp(q, carry):
            cr, ci, ar, ai = carry
            r0 = pl.multiple_of((ts // GRP - 1 - q) * GRP, GRP)
            lr, li = _scan_group(lr_ref[pl.ds(r0, GRP), :], li_ref[pl.ds(r0, GRP), :], steps, ctab, cr, ci, True)
            lr_ref[pl.ds(r0, GRP), :] = lr
            li_ref[pl.ds(r0, GRP), :] = li
            nr = jnp.where(row == GRP - 1, cr, pltpu.roll(lr, GRP - 1, 0))
            ni = jnp.where(row == GRP - 1, ci, pltpu.roll(li, GRP - 1, 0))
            xr = xr_ref[pl.ds(r0, GRP), :]
            xi = xi_ref[pl.ds(r0, GRP), :]
            ar = ar + nr * xr + ni * xi
            ai = ai + ni * xr - nr * xi
            return (jnp.broadcast_to(lr[0:1, :], (GRP, SB)), jnp.broadcast_to(li[0:1, :], (GRP, SB)), ar, ai)

        cr, ci, ar, ai = lax.fori_loop(0, ts // GRP, grp, (car_r[...], car_i[...], acc_r[...], acc_i[...]))
        car_r[...] = cr
        car_i[...] = ci
        acc_r[...] = ar
        acc_i[...] = ai

        @pl.when(i == nt - 1)
        def _():
            dar_ref[0] = _colsum(ar)
            dai_ref[0] = _colsum(ai)

        lrb = lr_ref[...].astype(BF16)
        lib = li_ref[...].astype(BF16)
        du = (jnp.dot(lrb, wbtr_ref[0], preferred_element_type=F32)
              + jnp.dot(lib, wbti_ref[0], preferred_element_type=F32) + d_ref[0] * dy)
        du_ref[...] = du.astype(du_ref.dtype)
        ub = u.astype(BF16)
        dwbr_ref[0] += lax.dot_general(ub, lrb, tn_dims, preferred_element_type=F32)
        dwbi_ref[0] += lax.dot_general(ub, lib, tn_dims, preferred_element_type=F32)
        dwc_ref[0, 0:SB, :] += lax.dot_general(xr_ref[...].astype(BF16), dyb, tn_dims, preferred_element_type=F32)
        dwc_ref[0, SB:2 * SB, :] += lax.dot_general(xi_ref[...].astype(BF16), dyb, tn_dims, preferred_element_type=F32)

    rev = lambda i: nt - 1 - i
    blk3 = lambda a: pl.BlockSpec((1,) + a.shape[1:], lambda j, i: (j, 0, 0))
    acc3 = lambda r, c: pl.BlockSpec((1, r, c), lambda j, i: (j, 0, 0))
    return pl.pallas_call(
        body, name="ssm_bwd", grid=(NBLK, nt),
        in_specs=[pl.BlockSpec((ts, CB), lambda j, i: (rev(i), j)), pl.BlockSpec((ts, CB), lambda j, i: (rev(i), j)),
                  pl.BlockSpec((ts, CB), lambda j, i: (rev(i), ucol0 + j)),
                  pl.BlockSpec((ts, SB), lambda j, i: (rev(i), j)), pl.BlockSpec((ts, SB), lambda j, i: (rev(i), j)),
                  blk3(wbt_re), blk3(wbt_im), blk3(wct), blk3(e_re), blk3(e_im), blk3(dvec),
                  pl.BlockSpec(memory_space=pl.ANY)],
        out_specs=[pl.BlockSpec((ts, CB), lambda j, i: (rev(i), ucol0 + j)),
                   acc3(1, CB), acc3(1, SB), acc3(1, SB), acc3(CB, SB), acc3(CB, SB), acc3(2 * SB, CB)],
        out_shape=[jax.ShapeDtypeStruct(dz.shape, dz.dtype),
                   jax.ShapeDtypeStruct((NBLK, 1, CB), F32),
                   jax.ShapeDtypeStruct((NBLK, 1, SB), F32), jax.ShapeDtypeStruct((NBLK, 1, SB), F32),
                   jax.ShapeDtypeStruct((NBLK, CB, SB), F32), jax.ShapeDtypeStruct((NBLK, CB, SB), F32),
                   jax.ShapeDtypeStruct((NBLK, 2 * SB, CB), F32)],
        scratch_shapes=[pltpu.VMEM((ts, SB), F32), pltpu.VMEM((ts, SB), F32)] + [pltpu.VMEM((GRP, SB), F32)] * 4,
        input_output_aliases={11: 0},
        compiler_params=_params(("parallel", "arbitrary")),
    )(dgl, ypre, z, xs_re, xs_im, wbt_re, wbt_im, wct, e_re, e_im, dvec, dz)


def _disc(a_re, a_im, log_dt, b_re, b_im, expand):
    dt = jnp.dot(expand, jnp.exp(log_dt), preferred_element_type=F32, precision=lax.Precision.HIGHEST)
    mag = jnp.exp(dt * a_re)
    e_re, e_im = mag * jnp.cos(dt * a_im), mag * jnp.sin(dt * a_im)
    n_re, n_im = e_re - 1.0, e_im
    den = a_re * a_re + a_im * a_im
    q_re = (n_re * a_re + n_im * a_im) / den
    q_im = (n_im * a_re - n_re * a_im) / den
    return e_re, e_im, q_re * b_re - q_im * b_im, q_re * b_im + q_im * b_re


def _whole(a):
    return pl.BlockSpec(a.shape, functools.partial(lambda n: (0,) * n, n=a.ndim))


def disc_fwd(a_re, a_im, log_dt, b_re, b_im, expand):
    def body(ar, ai, ld, br, bi, ex, er_o, ei_o, bbr_o, bbi_o):
        er, ei, bbr, bbi = _disc(ar[...], ai[...], ld[...], br[...], bi[...], ex[...])
        er_o[...] = er
        ei_o[...] = ei
        bbr_o[...] = bbr
        bbi_o[...] = bbi

    ins = (a_re, a_im, log_dt, b_re, b_im, expand)
    outs = [jax.ShapeDtypeStruct(a_re.shape, F32)] * 2 + [jax.ShapeDtypeStruct(b_re.shape, F32)] * 2
    return pl.pallas_call(body, name="disc_fwd", in_specs=[_whole(a) for a in ins],
                          out_specs=[_whole(o) for o in outs], out_shape=outs, compiler_params=_params())(*ins)


def disc_bwd(a_re, a_im, log_dt, b_re, b_im, expand, de_re, de_im, dbb_re, dbb_im):
    def body(ar, ai, ld, br, bi, ex, der, dei, dbr, dbi, o_ar, o_ai, o_ld, o_br, o_bi):
        exv = ex[...]
        _, vjp = jax.vjp(lambda *p: _disc(*p, exv), ar[...], ai[...], ld[...], br[...], bi[...])
        g = vjp((der[...], dei[...], dbr[...], dbi[...]))
        for o, v in zip((o_ar, o_ai, o_ld, o_br, o_bi), g):
            o[...] = v

    ins = (a_re, a_im, log_dt, b_re, b_im, expand, de_re, de_im, dbb_re, dbb_im)
    outs = [jax.ShapeDtypeStruct(a.shape, F32) for a in (a_re, a_im, log_dt, b_re, b_im)]
    return pl.pallas_call(body, name="disc_bwd", in_specs=[_whole(a) for a in ins],
                          out_specs=[_whole(o) for o in outs], out_shape=outs, compiler_params=_params())(*ins)


def mod_fwd(c_all, w_ada, b_cols):
    def body(c_ref, w_ref, b_ref, act_ref, mod_ref):
        cv = c_ref[...]
        act = cv * _sig(cv)
        act_ref[...] = act
        mod_ref[...] = jnp.dot(act, w_ref[...], preferred_element_type=F32, precision=lax.Precision.HIGHEST) + b_ref[...]

    ins = (c_all, w_ada, b_cols)
    outs = [jax.ShapeDtypeStruct(c_all.shape, F32), jax.ShapeDtypeStruct((NDEV, w_ada.shape[1]), F32)]
    return pl.pallas_call(body, name="mod_fwd", in_specs=[_whole(a) for a in ins],
                          out_specs=[_whole(o) for o in outs], out_shape=outs, compiler_params=_params())(*ins)


def ada_grad(act_all, dmod_cols):
    def body(a_ref, d_ref, o_ref):
        o_ref[...] = lax.dot_general(a_ref[...], d_ref[...], (((0,), (0,)), ((), ())),
                                     preferred_element_type=F32, precision=lax.Precision.HIGHEST)

    out = jax.ShapeDtypeStruct((act_all.shape[1], dmod_cols.shape[1]), F32)
    return pl.pallas_call(body, name="ada_grad", in_specs=[_whole(act_all), _whole(dmod_cols)],
                          out_specs=_whole(out), out_shape=out, compiler_params=_params())(act_all, dmod_cols)


def _adam_math(w, g, m, v):
    m2 = ADAM_B1 * m + (1.0 - ADAM_B1) * g
    v2 = ADAM_B2 * v + (1.0 - ADAM_B2) * (g * g)
    m_hat = m2 / (1.0 - ADAM_B1 ** ADAM_STEP)
    v_hat = v2 / (1.0 - ADAM_B2 ** ADAM_STEP)
    delta = -ADAM_LR * (m_hat / (jnp.sqrt(v_hat) + ADAM_EPS) + ADAM_WD * w)
    return delta, m2, v2


def adam(name, w, g, m, v):
    r, c = w.shape
    tr = r
    for cand in (256, 128, 64, 32, 16, 8):
        if r % cand == 0 and r > cand:
            tr = cand
            break

    def body(w_ref, g_ref, m_ref, v_ref, d_o, m_o, v_o):
        d, m2, v2 = _adam_math(w_ref[...], g_ref[...], m_ref[...], v_ref[...])
        d_o[...] = d
        m_o[...] = m2
        v_o[...] = v2

    spec = pl.BlockSpec((tr, c), lambda i: (i, 0))
    out = jax.ShapeDtypeStruct((r, c), F32)
    return pl.pallas_call(body, name=name, grid=(r // tr,), in_specs=[spec] * 4, out_specs=[spec] * 3,
                          out_shape=[out] * 3, compiler_params=_params(("parallel",)))(w, g, m, v)


def adam_many(name, ws, gs, ms, vs):
    n = len(ws)

    def body(*refs):
        ins, outs = refs[:4 * n], refs[4 * n:]
        for q in range(n):
            d, m2, v2 = _adam_math(ins[q][...], ins[n + q][...], ins[2 * n + q][...], ins[3 * n + q][...])
            outs[q][...] = d
            outs[n + q][...] = m2
            outs[2 * n + q][...] = v2

    operands = list(ws) + list(gs) + list(ms) + list(vs)
    outs = [jax.ShapeDtypeStruct(w.shape, F32) for w in ws] * 3
    return pl.pallas_call(body, name=name, in_specs=[_whole(a) for a in operands],
                          out_specs=[_whole(o) for o in outs], out_shape=outs, compiler_params=_params())(*operands)


def _rows_tile(r, most):
    best = None
    for t in range(16, min(r, most) + 1, 16):
        if r % t == 0:
            best = t
    assert best is not None, r
    return best


def sum_slots(name, slots, out_dtype=F32):
    n, r, c = slots.shape
    tr = _rows_tile(r, max(16, (2 * 1024 * 1024) // (n * c)))

    def body(s_ref, o_ref):
        acc = s_ref[0].astype(F32)
        for q in range(1, n):
            acc = acc + s_ref[q].astype(F32)
        o_ref[...] = acc.astype(o_ref.dtype)

    return pl.pallas_call(body, name=name, grid=(r // tr,),
                          in_specs=[pl.BlockSpec((n, tr, c), lambda i: (0, i, 0))],
                          out_specs=pl.BlockSpec((tr, c), lambda i: (i, 0)),
                          out_shape=jax.ShapeDtypeStruct((r, c), out_dtype), compiler_params=_params(("parallel",)))(slots)


HBM_SPEC = pl.BlockSpec(memory_space=pltpu.HBM)


def _coords():
    return lax.axis_index("x"), lax.axis_index("y"), lax.axis_index("c")


def _linear(x, y, c):
    return 4 * x + 2 * y + c


def all_gather(name, shards):
    nq = len(shards)

    def body(*refs):
        xs, outs = refs[:nq], refs[nq:2 * nq]
        send_sems, recv_sems, local_sems = refs[2 * nq:2 * nq + 3]
        bufs = refs[2 * nq + 3:]
        x, y, cc = _coords()
        me, sibling = (x, y, cc), (x, y, 1 - cc)
        chips = [(1 - x, y), (x, 1 - y), (1 - x, 1 - y)]

        def slot(q, px, py, pc):
            return outs[q].at[_linear(px, py, pc)]

        def copy(q, k, block, to, src=None):
            return pltpu.make_async_remote_copy(
                src_ref=slot(q, *block) if src is None else src, dst_ref=slot(q, *block),
                send_sem=send_sems.at[7 * q + k], recv_sem=recv_sems.at[7 * q + k], device_id=to, device_id_type=MESH)

        loads = [pltpu.make_async_copy(xs[q], bufs[q], local_sems.at[q]) for q in range(nq)]
        for cp in loads:
            cp.start()
        for cp in loads:
            cp.wait()
        mine = [pltpu.make_async_copy(bufs[q], slot(q, *me), local_sems.at[q]) for q in range(nq)]
        first = []
        for q in range(nq):
            first.append(copy(q, 0, me, sibling, src=bufs[q]))
            first += [copy(q, 1 + j, me, (*chip, cc), src=bufs[q]) for j, chip in enumerate(chips)]
        for cp in mine + first:
            cp.start()
        passed = []
        for q in range(nq):
            for j, chip in enumerate(chips):
                copy(q, 1 + j, (*chip, cc), me).wait_recv()
                passed.append(copy(q, 4 + j, (*chip, cc), sibling))
                passed[-1].start()
        for q in range(nq):
            copy(q, 0, sibling, me).wait_recv()
            for j, chip in enumerate(chips):
                copy(q, 4 + j, (*chip, 1 - cc), me).wait_recv()
        for cp in first + passed:
            cp.wait_send()
        for cp in mine:
            cp.wait()

    return pl.pallas_call(
        body, name=name, in_specs=[HBM_SPEC] * nq, out_specs=[HBM_SPEC] * nq,
        out_shape=[jax.ShapeDtypeStruct((NDEV,) + s.shape, s.dtype) for s in shards],
        scratch_shapes=[pltpu.SemaphoreType.DMA((7 * nq,)), pltpu.SemaphoreType.DMA((7 * nq,)),
                        pltpu.SemaphoreType.DMA((nq,))] + [pltpu.VMEM(s.shape, s.dtype) for s in shards],
    )(*shards)


NCHIP = 4


SEM_SPEC = pl.BlockSpec(memory_space=pltpu.SEMAPHORE)
EFFECT = pltpu.SideEffectType.DATAFLOW_SIDE_EFFECTING


def _peer(x, y, cc, k):
    fx, fy, fc = (k >> 2) & 1, (k >> 1) & 1, k & 1
    return (x + fx - 2 * fx * x, y + fy - 2 * fy * y, cc + fc - 2 * fc * cc)


def gather_plan(srcs, lands, coords):
    x, y, cc = coords
    me = _linear(x, y, cc)
    return [(s, l.at[me], _peer(x, y, cc, k)) for s, l in zip(srcs, lands) for k in range(1, NDEV)]


def near_plan(srcs, lands, coords):
    x, y, cc = coords
    me = _linear(x, y, cc)
    peers = [(x, y, 1 - cc)] + [_peer(x, y, cc, 2 * k) for k in range(1, NCHIP)]
    return [(s, l.at[me], p) for s, l in zip(srcs, lands) for p in peers]


def pass_on_plan(srcs, lands, coords):
    x, y, cc = coords
    out = []
    for l in srcs:
        for k in range(1, NCHIP):
            px, py, _ = _peer(x, y, cc, 2 * k)
            slot = _linear(px, py, cc)
            out.append((l.at[slot], l.at[slot], (x, y, 1 - cc)))
    return out


def pair_plan(srcs, lands, coords):
    x, y, cc = coords
    return [(s.at[2 * chip + 1 - cc], l.at[chip], (x, y, 1 - cc)) for s, l in zip(srcs, lands) for chip in range(NCHIP)]


def chip_plan(srcs, lands, coords):
    x, y, cc = coords
    out = []
    for s, l in zip(srcs, lands):
        for k in range(1, NCHIP):
            px, py, _ = _peer(x, y, cc, 2 * k)
            out.append((s.at[2 * px + py], l.at[k - 1], (px, py, cc)))
    return out


def _remote(copy, i, send_sems, recv_sems):
    src, dst, dev = copy
    return pltpu.make_async_remote_copy(src_ref=src, dst_ref=dst, send_sem=send_sems.at[i], recv_sem=recv_sems.at[i],
                                        device_id=dev, device_id_type=MESH)


def exchange_start(name, plan, ncopy, srcs, land_shapes, deps=()):
    ns, nl, nd = len(srcs), len(land_shapes), len(deps)

    def body(*refs):
        s, l = refs[:ns], refs[ns:ns + nl]
        send_sems, recv_sems = refs[ns + nl + nd], refs[ns + nl + nd + 1]
        token = refs[-1]
        for i, cp in enumerate(plan(s, l, _coords())):
            _remote(cp, i, send_sems, recv_sems).start()
        token[...] = jnp.zeros_like(token)

    hbm = lambda a: pltpu.with_memory_space_constraint(a, pltpu.HBM)
    lands = [lax.empty(shp, dt) for shp, dt in land_shapes]
    thru = [pltpu.HBM(a.shape, a.dtype) for a in list(srcs) + lands]
    outs = pl.pallas_call(
        body, name=name,
        in_specs=[HBM_SPEC] * (ns + nl) + [ANY_SPEC] * nd,
        out_specs=(SEM_SPEC, SEM_SPEC, *[HBM_SPEC] * (ns + nl), pl.BlockSpec(memory_space=pltpu.VMEM)),
        out_shape=(pltpu.SemaphoreType.DMA((ncopy,)), pltpu.SemaphoreType.DMA((ncopy,)), *thru,
                   jax.ShapeDtypeStruct((8, LANE), F32)),
        input_output_aliases={i: 2 + i for i in range(ns + nl)},
        compiler_params=pltpu.CompilerParams(has_side_effects=EFFECT),
    )(*[hbm(a) for a in srcs], *[hbm(a) for a in lands], *deps)
    return outs[0], outs[1], list(outs[2:2 + ns]), list(outs[2 + ns:2 + ns + nl]), outs[-1]


def exchange_wait(name, plan, started, after, place_own=False):
    send_sems, recv_sems, srcs, lands, _ = started
    ns, nl = len(srcs), len(lands)

    def body(*refs):
        s, l = refs[:ns], refs[ns:ns + nl]
        send_sems, recv_sems = refs[ns + nl], refs[ns + nl + 1]
        l_out = refs[2 * ns + nl + 3:2 * ns + 2 * nl + 3]
        scratch = refs[2 * ns + 2 * nl + 3:]
        copies = [_remote(cp, i, send_sems, recv_sems) for i, cp in enumerate(plan(s, l, _coords()))]
        if place_own:
            me = _linear(*_coords())
            local_sems, bufs = scratch[0], scratch[1:]
            loads = [pltpu.make_async_copy(s[q], bufs[q], local_sems.at[q]) for q in range(ns)]
            for cp in loads:
                cp.start()
            for cp in loads:
                cp.wait()
            stores = [pltpu.make_async_copy(bufs[q], l_out[q].at[me], local_sems.at[q]) for q in range(ns)]
            for cp in stores:
                cp.start()
        for cp in copies:
            cp.wait_recv()
        for cp in copies:
            cp.wait_send()
        if place_own:
            for cp in stores:
                cp.wait()

    scratch_shapes = []
    if place_own:
        scratch_shapes = [pltpu.SemaphoreType.DMA((ns,))] + [pltpu.VMEM(a.shape, a.dtype) for a in srcs]
    outs = pl.pallas_call(
        body, name=name,
        in_specs=[HBM_SPEC] * (ns + nl) + [SEM_SPEC, SEM_SPEC, ANY_SPEC],
        out_specs=[HBM_SPEC] * (ns + nl),
        out_shape=[pltpu.HBM(a.shape, a.dtype) for a in srcs + lands],
        input_output_aliases={i: i for i in range(ns + nl)},
        scratch_shapes=scratch_shapes,
        compiler_params=pltpu.CompilerParams(has_side_effects=EFFECT),
    )(*srcs, *lands, send_sems, recv_sems, after)
    return list(outs[:ns]), list(outs[ns:])


def pair_sum(name, g, recv):
    _, r, c = g.shape
    tr = _rows_tile(r, 512)

    def body(g_ref, r_ref, o_ref):
        own = jnp.where(lax.axis_index("c") == 0, g_ref[0, 0], g_ref[0, 1])
        o_ref[0] = (own.astype(F32) + r_ref[0].astype(F32)).astype(o_ref.dtype)

    return pl.pallas_call(
        body, name=name, grid=(NCHIP, r // tr),
        in_specs=[pl.BlockSpec((1, 2, tr, c), lambda k, i: (k, 0, i, 0)), pl.BlockSpec((1, tr, c), lambda k, i: (k, i, 0))],
        out_specs=pl.BlockSpec((1, tr, c), lambda k, i: (k, i, 0)),
        out_shape=jax.ShapeDtypeStruct((NCHIP, r, c), g.dtype), compiler_params=_params(("parallel", "parallel")),
    )(g.reshape(NCHIP, 2, r, c), recv)


def chip_sum(name, partial, recv):
    _, r, c = partial.shape
    tr = _rows_tile(r, 512)

    def body(p_ref, r_ref, o_ref):
        chip = 2 * lax.axis_index("x") + lax.axis_index("y")
        own = p_ref[0]
        for k in range(1, NCHIP):
            own = jnp.where(chip == k, p_ref[k], own)
        acc = own.astype(F32)
        for k in range(NCHIP - 1):
            acc = acc + r_ref[k].astype(F32)
        o_ref[...] = acc

    return pl.pallas_call(
        body, name=name, grid=(r // tr,),
        in_specs=[pl.BlockSpec((NCHIP, tr, c), lambda i: (0, i, 0)), pl.BlockSpec((NCHIP - 1, tr, c), lambda i: (0, i, 0))],
        out_specs=pl.BlockSpec((tr, c), lambda i: (i, 0)),
        out_shape=jax.ShapeDtypeStruct((r, c), F32), compiler_params=_params(("parallel",)),
    )(partial, recv)


def _block_diag(w, rows_per, cols_per):
    w = w.reshape(NBLK, 8, rows_per, cols_per)
    eye = jnp.eye(8, dtype=w.dtype)
    out = w[:, :, :, None, :] * eye[None, :, None, :, None]
    return out.reshape(NBLK, 8 * rows_per, 8 * cols_per)


def _diag_blocks(wd, rows_per, cols_per):
    wd = wd.reshape(NBLK, 8, rows_per, 8, cols_per)
    idx = jnp.arange(8)
    return wd[:, idx, :, idx, :].transpose(1, 0, 2, 3).reshape(NG, rows_per, cols_per)


def _pad_rows(v, mult):
    n = v.shape[0]
    return jnp.pad(v, (0, (-n) % mult))


def kernel(x, c, w_ada, b_ada, norm1_g, w_in, conv_w, conv_b, conv_ln_g, conv_ln_b, conv_proj, ssm_a_re, ssm_a_im, ssm_b_re, ssm_b_im, ssm_c_re, ssm_c_im, ssm_d, ssm_log_dt, ssm_glu, w_out, norm2_g, w_ffn_in, w_ffn_out, final_g, loss_target, m_w_ada, m_b_ada, m_norm1_g, m_w_in, m_conv_w, m_conv_b, m_conv_ln_g, m_conv_ln_b, m_conv_proj, m_ssm_a_re, m_ssm_a_im, m_ssm_b_re, m_ssm_b_im, m_ssm_c_re, m_ssm_c_im, m_ssm_d, m_ssm_log_dt, m_ssm_glu, m_w_out, m_norm2_g, m_w_ffn_in, m_w_ffn_out, m_final_g, v_w_ada, v_b_ada, v_norm1_g, v_w_in, v_conv_w, v_conv_b, v_conv_ln_g, v_conv_ln_b, v_conv_proj, v_ssm_a_re, v_ssm_a_im, v_ssm_b_re, v_ssm_b_im, v_ssm_c_re, v_ssm_c_im, v_ssm_d, v_ssm_log_dt, v_ssm_glu, v_w_out, v_norm2_g, v_w_ffn_in, v_w_ffn_out, v_final_g):
    me = _linear(*_coords())
    xs = x[0]
    tgt = loss_target[0]
    seq = xs.shape[0]

    flat = lambda g: g.reshape(NDEV * g.shape[1], g.shape[2])
    c_all, cw_g = all_gather("gather_c_conv_w", [c, conv_w[0]])
    w_in_s = w_in[0].T.astype(BF16)
    mids = [p.astype(BF16) for p in (conv_proj[0].T, ssm_glu[0].T, w_out[0])]
    ffns = [p.astype(BF16) for p in (w_ffn_in[0].T, w_ffn_out[0])]
    zone = lambda p: ((NDEV,) + p.shape, p.dtype)
    in_go = exchange_start("gather_in_start", near_plan, NCHIP, [w_in_s], [zone(w_in_s)], deps=[c_all])
    mids_go = exchange_start("gather_mid_start", gather_plan, 7 * len(mids), mids, [zone(p) for p in mids],
                             deps=[in_go[4]])
    ffns_go = exchange_start("gather_ffn_start", gather_plan, 7 * len(ffns), ffns, [zone(p) for p in ffns],
                             deps=[mids_go[4]])

    ncol = w_ada.shape[2]
    c_all = c_all.reshape(NDEV, D)
    b_cols = lax.dynamic_slice_in_dim(b_ada, me * ncol, ncol, axis=1)
    act_all, mod_cols = mod_fwd(c_all, w_ada[0], b_cols)
    (mod_all,) = all_gather("gather_mod", [mod_cols])
    mod = lax.dynamic_index_in_dim(mod_all, me, axis=1, keepdims=False).reshape(NMOD, D)
    sh1, sc1, g1, sh2, sc2, g2 = [mod[q:q + 1] for q in range(NMOD)]

    expand = jnp.repeat(jnp.eye(NG, dtype=F32), NP, axis=0)
    a_re_c, a_im_c = ssm_a_re.reshape(NST, 1), ssm_a_im.reshape(NST, 1)
    ldt_c = ssm_log_dt.reshape(NG, 1)
    b_re_r, b_im_r = ssm_b_re.reshape(NST, GH), ssm_b_im.reshape(NST, GH)
    e_re, e_im, bb_re, bb_im = disc_fwd(a_re_c, a_im_c, ldt_c, b_re_r, b_im_r, expand)
    e_re_b, e_im_b = e_re.reshape(NBLK, 1, SB), e_im.reshape(NBLK, 1, SB)
    bb_re_g, bb_im_g = bb_re.reshape(NG, NP, GH), bb_im.reshape(NG, NP, GH)
    wbt_re = _block_diag(bb_re_g, NP, GH)
    wbt_im = _block_diag(bb_im_g, NP, GH)
    wb_re, wb_im = wbt_re.transpose(0, 2, 1), wbt_im.transpose(0, 2, 1)
    wct = jnp.concatenate([_block_diag(ssm_c_re[0], GH, NP), -_block_diag(ssm_c_im[0], GH, NP)], axis=2)
    wc = wct.transpose(0, 2, 1)
    to_b = lambda a: a.astype(BF16)
    dvec = ssm_d.reshape(NBLK, 1, CB)

    n1g = norm1_g

    def f_norm1(xv, g, sc, sh):
        _, xh = _rms_stats(xv)
        return [xh * g * (1.0 + sc) + sh], []

    (h1,) = rowwise("norm1", f_norm1, [xs], [n1g, sc1, sh1], [(D, BF16)], [], 512, deps=[ffns_go[4]])
    _, (w_in_land,) = exchange_wait("gather_in_wait", near_plan, in_go, h1, place_own=True)
    pass_go = exchange_start("gather_in_pass_start", pass_on_plan, NCHIP - 1, [w_in_land], [])
    (w_in_g,), _ = exchange_wait("gather_in_pass_wait", pass_on_plan, pass_go, pass_go[4])
    w_in_t = flat(w_in_g)
    z = mm("mm_in", h1, w_in_t, "nt", tiles=(2048, CW, 1024), b_rot=Z_ROT)

    conv_w_full = cw_g.transpose(1, 0, 2).reshape(KC, CW)
    w32 = jnp.pad(conv_w_full, ((0, HALO - KC), (0, 0)))
    yc, s_act = conv_fwd(z, w32, conv_b, conv_ln_g, conv_ln_b)
    conv_proj_t, ssm_glu_t, w_out_f = [
        flat(g) for g in exchange_wait("gather_mid_wait", gather_plan, mids_go, s_act, place_own=True)[1]]
    y_conv = mm("mm_conv_proj", s_act, conv_proj_t, "nt")

    xs_re, xs_im, ypre, gl = ssm_fwd(z, to_b(wb_re), to_b(wb_im), to_b(wc), e_re_b, e_im_b, dvec)
    n_mrg = D // MRG_BLK

    pair_of = lambda t, n: t // 2 + (t % 2) * n

    def ep_merge(accs, yc_v, gates):
        za, zb = accs
        glc, gls = gates[:, 0:MRG_BLK], gates[:, MRG_BLK:2 * MRG_BLK]
        return [_sig(glc) * yc_v + _sig(gls) * (za * _sig(zb)), jnp.concatenate([za, zb], axis=1)]

    merged, z2_pair = mm_ep("mm_ssm_glu", gl, ssm_glu_t, 2, lambda j, q: j + q * n_mrg, ep_merge,
                            [(y_conv, 1, 0), (z, 2, 0)], [(D, BF16, 1), (2 * D, BF16, 2)], (512, MRG_BLK, SW))
    row_tiles = lambda bk: (512, D, bk)
    whole = lambda j, q: j

    def ep_norm2(accs, xv, g1v, g, sc, sh):
        (o1v,) = accs
        x1v = xv + g1v * o1v
        _, xh = _rms_stats(x1v)
        return [x1v, xh * g * (1.0 + sc) + sh, o1v]

    x1, h2, o1 = mm_ep("mm_out", merged, w_out_f, 1, whole, ep_norm2, [(xs, 1, 0)],
                       [(D, F32, 1), (D, BF16, 1), (D, F32, 1)], row_tiles(D), b_kn=True,
                       consts=[g1, norm2_g, sc2, sh2])
    w_ffn_in_t, w_ffn_out_f = [
        flat(g) for g in exchange_wait("gather_ffn_wait", gather_plan, ffns_go, h2, place_own=True)[1]]
    ffn_tiles = (512, FFN_BLK, 1024)
    n_ffn_blk = FH // FFN_BLK
    pair_map = lambda t: t // 2 + (t % 2) * n_ffn_blk

    def ep_swiglu(accs):
        fg, fu = accs
        return [fg * _sig(fg) * fu, jnp.concatenate([fg, fu], axis=1)]

    act, f_pair = mm_ep("mm_ffn_in", h2, w_ffn_in_t, 2, lambda j, q: j + q * n_ffn_blk, ep_swiglu, [],
                        [(FH, BF16, 1), (2 * FH, BF16, 2)], ffn_tiles)
    fg_row = final_g.reshape(1, D)

    def ep_final(accs, x1v, tv, g2v, fg):
        (o2v,) = accs
        x2v = x1v + g2v * o2v
        r, xh = _rms_stats(x2v)
        yv = xh * fg
        err = yv - tv
        loss = jnp.sum(_colsum(err * err), axis=1, keepdims=True) * (0.5 / D)
        dy = err * (1.0 / D)
        dx2 = _rms_bwd(dy * fg, xh, r)
        return ([dx2, g2v * dx2],
                [jnp.broadcast_to(loss, (1, LANE)), _colsum(dy * xh), _colsum(dx2 * o2v)])

    dx2, do2, loss_l, d_final_g, d_g2 = mm_ep(
        "mm_ffn_out", act, w_ffn_out_f, 1, whole, ep_final, [(x1, 1, 0), (tgt, 1, 0)],
        [(D, F32, 1), (D, BF16, 1)], row_tiles(FFN_BLK), b_kn=True, consts=[g2, fg_row], sums=[LANE, D, D])

    g_ffn_out = mm("mm_g_ffn_out", act, do2, "tn", BF16, tiles=(FFN_BLK, 1024, 1024))

    def ep_dswiglu(accs, fp):
        (da,) = accs
        fg, fu = fp[:, 0:FFN_BLK].astype(F32), fp[:, FFN_BLK:2 * FFN_BLK].astype(F32)
        sg = _sig(fg)
        return [jnp.concatenate([da * fu * (sg * (1.0 + fg * (1.0 - sg))), da * (fg * sg)], axis=1)]

    (df,) = mm_ep("mm_dact", do2, w_ffn_out_f, 1, lambda j, q: j, ep_dswiglu, [(f_pair, 2, 0)],
                  [(2 * FH, BF16, 2)], ffn_tiles)
    g_ffn_in_t = mm("mm_g_ffn_in", df, h2, "tn", BF16, tiles=(FFN_BLK, 1024, 1024), o_rot=pair_map)

    def pair_go(tag, grads_t, deps=()):
        srcs = [g.reshape(NDEV, -1, D) for g in grads_t]
        return exchange_start("pair_" + tag + "_start", pair_plan, NCHIP * len(srcs), srcs,
                              [((NCHIP,) + s.shape[1:], s.dtype) for s in srcs], deps)

    def chip_go(tag, names, pair_started, after):
        own, from_sibling = exchange_wait("pair_" + tag + "_wait", pair_plan, pair_started, after)
        partials = [pair_sum("pair_sum_" + n, g, r) for n, g, r in zip(names, own, from_sibling)]
        return exchange_start("chip_" + tag + "_start", chip_plan, (NCHIP - 1) * len(partials), partials,
                              [((NCHIP - 1,) + p.shape[1:], p.dtype) for p in partials])

    def chip_done(tag, names, chip_started, after):
        partials, from_chips = exchange_wait("chip_" + tag + "_wait", chip_plan, chip_started, after)
        return [chip_sum("chip_sum_" + n, p, r) for n, p, r in zip(names, partials, from_chips)]

    pair_ffn = pair_go("ffn", [g_ffn_out, g_ffn_in_t])

    def ep_dnorm2(accs, x1v, dx2v, o1v, g, sc, g1v):
        (dh,) = accs
        r, xh = _rms_stats(x1v)
        dxh = dh * (1.0 + sc) * g
        dx1 = dx2v + _rms_bwd(dxh, xh, r)
        return ([dx1, g1v * dx1],
                [_colsum(dh * xh * g), _colsum(dh), _colsum(dh * (1.0 + sc) * xh), _colsum(dx1 * o1v)])

    dx1, do1, d_sc2, d_sh2, d_n2g, d_g1 = mm_ep(
        "mm_dh2", df, w_ffn_in_t, 1, whole, ep_dnorm2, [(x1, 1, 0), (dx2, 1, 0), (o1, 1, 0)],
        [(D, F32, 1), (D, BF16, 1)], row_tiles(FFN_BLK), deps=[pair_ffn[4]], b_kn=True, k_map=pair_map,
        consts=[norm2_g, sc2, g1], sums=[D, D, D, D])

    g_out = mm("mm_g_out", merged, do1, "tn", BF16)
    chip_ffn = chip_go("ffn", ("w_ffn_out", "w_ffn_in"), pair_ffn, g_out)

    def ep_dmerge(accs, yc_v, z2p, gates):
        (dm,) = accs
        za, zb = z2p[:, 0:MRG_BLK].astype(F32), z2p[:, MRG_BLK:2 * MRG_BLK].astype(F32)
        sc_, ss_, sb_ = _sig(gates[:, 0:MRG_BLK]), _sig(gates[:, MRG_BLK:2 * MRG_BLK]), _sig(zb)
        dys = dm * ss_
        dz2 = jnp.concatenate([dys * sb_, dys * za * sb_ * (1.0 - sb_)], axis=1)
        dgates = jnp.concatenate([dm * yc_v * sc_ * (1.0 - sc_), dm * (za * sb_) * ss_ * (1.0 - ss_)], axis=1)
        return [dm * sc_, dz2, dgates]

    dyconv, dz2, dz = mm_ep("mm_dmerged", do1, w_out_f, 1, lambda j, q: j, ep_dmerge,
                            [(y_conv, 1, 0), (z2_pair, 2, 0), (z, 2, 0)],
                            [(D, BF16, 1), (2 * D, BF16, 2), (ZW, BF16, 2)], (512, MRG_BLK, 1024), deps=[chip_ffn[4]])

    g_conv_proj_t = mm("mm_g_conv_proj", dyconv, s_act, "tn", BF16)
    mrg_map = lambda t: pair_of(t, n_mrg)
    dgl = mm("mm_dgl", dz2, ssm_glu_t, "nn", tiles=(1024, SW, MRG_BLK), b_rot=mrg_map)
    g_ssm_glu_t = mm("mm_g_ssm_glu", dz2, gl, "tn", BF16, tiles=(MRG_BLK, SW, 1024), o_rot=mrg_map)
    pair_mid = pair_go("mid", [g_out, g_conv_proj_t, g_ssm_glu_t])
    ds = mm("mm_ds", dyconv, conv_proj_t, "nn", deps=[pair_mid[4]])
    dz, d_lng, d_lnb, d_cb, d_cw32 = conv_bwd(ds, yc, z, w32, conv_ln_g, conv_ln_b, dz)
    dz, d_d, d_ar, d_ai, d_wb_re, d_wb_im, d_wc = ssm_bwd(
        dgl, ypre, z, xs_re, xs_im, to_b(wbt_re), to_b(wbt_im), to_b(wct), e_re_b, e_im_b, dvec, dz)
    chip_mid = chip_go("mid", ("w_out", "conv_proj", "ssm_glu"), pair_mid, dz)

    d_bb_re = _diag_blocks(d_wb_re.transpose(0, 2, 1), NP, GH).reshape(NST, GH)
    d_bb_im = _diag_blocks(d_wb_im.transpose(0, 2, 1), NP, GH).reshape(NST, GH)
    d_wct = d_wc.transpose(0, 2, 1)
    d_c_re = _diag_blocks(d_wct[:, :, 0:SB], GH, NP)
    d_c_im = -_diag_blocks(d_wct[:, :, SB:2 * SB], GH, NP)
    d_a_re, d_a_im, d_ldt, d_b_re, d_b_im = disc_bwd(
        a_re_c, a_im_c, ldt_c, b_re_r, b_im_r, expand, d_ar.reshape(NST, 1), d_ai.reshape(NST, 1), d_bb_re, d_bb_im)

    small_local = [jnp.concatenate([d_g1, d_sh2, d_sc2, d_g2], axis=1).reshape(-1), d_cw32[0:KC].reshape(-1),
                   d_cb.reshape(-1), d_lng.reshape(-1), d_lnb.reshape(-1), d_a_re.reshape(-1), d_a_im.reshape(-1),
                   d_b_re.reshape(-1), d_b_im.reshape(-1), d_c_re.reshape(-1), d_c_im.reshape(-1), d_d.reshape(-1),
                   d_ldt.reshape(-1), d_n2g.reshape(-1), d_final_g.reshape(-1), loss_l[0, 0:1]]
    small_sizes = [v.shape[0] for v in small_local]
    small_pack = _pad_rows(jnp.concatenate(small_local), 256 * LANE).reshape(-1, LANE)
    small_go = exchange_start("gather_small_start", gather_plan, NDEV - 1, [small_pack],
                              [((NDEV,) + small_pack.shape, F32)], deps=[chip_mid[4]])

    g_in_t = mm("mm_g_in", dz, h1, "tn", BF16, tiles=(CW, 1024, 2048), o_rot=Z_ROT, deps=[small_go[4]])
    pair_in = pair_go("in", [g_in_t])

    def ep_dnorm1(accs, xv, dx1v, g, sc):
        (dh,) = accs
        r, xh = _rms_stats(xv)
        dxh = dh * (1.0 + sc) * g
        return ([dx1v + _rms_bwd(dxh, xh, r)],
                [_colsum(dh * xh * g), _colsum(dh), _colsum(dh * (1.0 + sc) * xh)])

    grad_x, d_sc1, d_sh1, d_n1g = mm_ep(
        "mm_dh1", dz, w_in_t, 1, whole, ep_dnorm1, [(xs, 1, 0), (dx1, 1, 0)], [(D, F32, 1)], row_tiles(CW),
        deps=[pair_in[4]], b_kn=True, k_map=Z_ROT, consts=[n1g, sc1], sums=[D, D, D])
    chip_in = chip_go("in", ("w_in",), pair_in, grad_x)

    weights = {
        "w_ada": (w_ada, m_w_ada, v_w_ada), "b_ada": (b_ada, m_b_ada, v_b_ada), "norm1_g": (norm1_g, m_norm1_g, v_norm1_g),
        "w_in": (w_in, m_w_in, v_w_in), "conv_w": (conv_w, m_conv_w, v_conv_w), "conv_b": (conv_b, m_conv_b, v_conv_b),
        "conv_ln_g": (conv_ln_g, m_conv_ln_g, v_conv_ln_g), "conv_ln_b": (conv_ln_b, m_conv_ln_b, v_conv_ln_b),
        "conv_proj": (conv_proj, m_conv_proj, v_conv_proj), "ssm_a_re": (ssm_a_re, m_ssm_a_re, v_ssm_a_re),
        "ssm_a_im": (ssm_a_im, m_ssm_a_im, v_ssm_a_im), "ssm_b_re": (ssm_b_re, m_ssm_b_re, v_ssm_b_re),
        "ssm_b_im": (ssm_b_im, m_ssm_b_im, v_ssm_b_im), "ssm_c_re": (ssm_c_re, m_ssm_c_re, v_ssm_c_re),
        "ssm_c_im": (ssm_c_im, m_ssm_c_im, v_ssm_c_im), "ssm_d": (ssm_d, m_ssm_d, v_ssm_d),
        "ssm_log_dt": (ssm_log_dt, m_ssm_log_dt, v_ssm_log_dt), "ssm_glu": (ssm_glu, m_ssm_glu, v_ssm_glu),
        "w_out": (w_out, m_w_out, v_w_out), "norm2_g": (norm2_g, m_norm2_g, v_norm2_g),
        "w_ffn_in": (w_ffn_in, m_w_ffn_in, v_w_ffn_in), "w_ffn_out": (w_ffn_out, m_w_ffn_out, v_w_ffn_out),
        "final_g": (final_g, m_final_g, v_final_g),
    }
    order = list(weights)
    big = ("w_ada", "w_in", "conv_proj", "ssm_glu", "w_out", "w_ffn_in", "w_ffn_out")
    grads, delta, new_m, new_v = {}, {}, {}, {}

    def adam_big(n, g2d):
        wv, mv, vv = weights[n]
        shp = wv.shape
        d_, m_, v_ = adam("adam_" + n, wv.reshape(shp[-2:]), g2d, mv.reshape(shp[-2:]), vv.reshape(shp[-2:]))
        grads[n], delta[n], new_m[n], new_v[n] = g2d.reshape(shp), d_.reshape(shp), m_.reshape(shp), v_.reshape(shp)
        return d_

    gs_ffn_out, gs_ffn_in = chip_done("ffn", ("w_ffn_out", "w_ffn_in"), chip_ffn, chip_in[4])
    adam_big("w_ffn_out", gs_ffn_out)
    last = adam_big("w_ffn_in", gs_ffn_in.T)
    gs_out, gs_conv_proj, gs_ssm_glu = chip_done("mid", ("w_out", "conv_proj", "ssm_glu"), chip_mid, last)
    adam_big("w_out", gs_out)
    adam_big("conv_proj", gs_conv_proj.reshape(-1, CW).T)
    adam_big("ssm_glu", gs_ssm_glu.reshape(-1, SW).T)

    late_local = [d_sh1.reshape(-1), d_sc1.reshape(-1), d_n1g.reshape(-1)]
    late_pack = _pad_rows(jnp.concatenate(late_local), 16 * LANE).reshape(-1, LANE)
    (late_all,) = all_gather("gather_small_late", [late_pack])
    _, (small_all,) = exchange_wait("gather_small_wait", gather_plan, small_go, late_all, place_own=True)

    def unpack(vec, sizes):
        out, pos = [], 0
        for n in sizes:
            out.append(vec[pos:pos + n])
            pos += n
        return out

    g_sh1, g_sc1, g_n1g = unpack(sum_slots("sum_small_late", late_all).reshape(-1), [D, D, D])
    (g_mod_rest, g_cw_full, g_cb, g_lng, g_lnb, g_a_re, g_a_im, g_b_re, g_b_im, g_c_re, g_c_im, g_d, g_ldt,
     g_n2g, g_fg, loss_sum) = unpack(sum_slots("sum_small", small_all).reshape(-1), small_sizes)
    g_b_ada = jnp.concatenate([g_sh1, g_sc1, g_mod_rest])
    loss = loss_sum[0]
    dmod_all = jnp.concatenate([late_all.reshape(NDEV, -1)[:, 0:2 * D], small_all.reshape(NDEV, -1)[:, 0:4 * D]],
                               axis=1)
    g_w_ada = ada_grad(act_all, lax.dynamic_slice_in_dim(dmod_all, me * ncol, ncol, axis=1))
    ccol = conv_w.shape[2]
    g_conv_w = lax.dynamic_slice_in_dim(g_cw_full.reshape(KC, CW), me * ccol, ccol, axis=1)

    adam_big("w_ada", g_w_ada)
    grads.update({
        "b_ada": g_b_ada.reshape(b_ada.shape), "norm1_g": g_n1g.reshape(norm1_g.shape),
        "conv_w": g_conv_w[None], "conv_b": g_cb.reshape(conv_b.shape),
        "conv_ln_g": g_lng.reshape(conv_ln_g.shape), "conv_ln_b": g_lnb.reshape(conv_ln_b.shape),
        "ssm_a_re": g_a_re.reshape(ssm_a_re.shape),
        "ssm_a_im": g_a_im.reshape(ssm_a_im.shape), "ssm_b_re": g_b_re.reshape(ssm_b_re.shape),
        "ssm_b_im": g_b_im.reshape(ssm_b_im.shape), "ssm_c_re": g_c_re.reshape(ssm_c_re.shape),
        "ssm_c_im": g_c_im.reshape(ssm_c_im.shape), "ssm_d": g_d.reshape(ssm_d.shape),
        "ssm_log_dt": g_ldt.reshape(ssm_log_dt.shape),
        "norm2_g": g_n2g.reshape(norm2_g.shape),
        "final_g": g_fg.reshape(final_g.shape),
    })
    small = [n for n in order if n not in big]
    rows = lambda a: a.reshape(1, -1) if a.ndim == 1 else a
    small_out = adam_many("adam_small", [rows(weights[n][0]) for n in small], [rows(grads[n]) for n in small],
                          [rows(weights[n][1]) for n in small], [rows(weights[n][2]) for n in small])
    for q, n in enumerate(small):
        shp = weights[n][0].shape
        delta[n], new_m[n], new_v[n] = [small_out[t * len(small) + q].reshape(shp) for t in range(3)]

    (gs_in,) = chip_done("in", ("w_in",), chip_in, small_out[0])
    adam_big("w_in", gs_in.T)

    return (loss, grad_x[None], *[grads[n] for n in order], *[delta[n] for n in order],
            *[new_m[n] for n in order], *[new_v[n] for n in order])
```

```python
import functools
import math

import jax
import jax.numpy as jnp
from jax import lax
from jax.experimental import pallas as pl
from jax.experimental.pallas import tpu as pltpu

F32 = jnp.float32
BF16 = jnp.bfloat16

D = 1024
CW = 512
KC = 31
SW = 512
NG = 32
GH = 16
NP = 64
NST = NG * NP
FH = 2816
FFN_BLK = 1408
MRG_BLK = 1024
NMOD = 6
NDEV = 8
EPS = 1e-6
CB = 128
SB = 512
NBLK = SW // CB
HALO = 32
ZW = 2 * CW + SW + 2 * D
Z_ROT = lambda j: (j + 3) % (ZW // CW)
ZB_A, ZB_G, ZB_U = 4, 5, 6

ADAM_LR = 0.001
ADAM_B1 = 0.9
ADAM_B2 = 0.999
ADAM_EPS = 1e-08
ADAM_WD = 0.01
ADAM_STEP = 10

V7X_VMEM_BYTES = 64 * 1024 * 1024
VMEM_LIMIT = V7X_VMEM_BYTES - 8 * 1024 * 1024
LANE = 128
MESH = pl.DeviceIdType.MESH
ANY_SPEC = pl.BlockSpec(memory_space=pl.ANY)


def _params(sem=None, **kw):
    if sem is not None:
        kw["dimension_semantics"] = sem
    return pltpu.CompilerParams(vmem_limit_bytes=VMEM_LIMIT, **kw)


def _tile(n, most):
    best = None
    for t in range(LANE, most + 1, LANE):
        if n % t == 0:
            best = t
    if best is None:
        raise ValueError(f"no tile for {n}")
    return best


def _sig(x):
    return jax.nn.sigmoid(x)


def mm(name, a, b, mode, out_dtype=F32, tiles=None, b_rot=None, o_rot=None, deps=()):
    if mode == "nn":
        (m, k), (k2, n) = a.shape, b.shape
    elif mode == "nt":
        (m, k), (n, k2) = a.shape, b.shape
    else:
        (k, m), (k2, n) = a.shape, b.shape
    assert k == k2, (name, a.shape, b.shape)
    bm, bn, bk = tiles or (_tile(m, 1024), _tile(n, 1408), _tile(k, 1408 if k % 1408 == 0 else 1024))
    bm, bn, bk = min(bm, m), min(bn, n), min(bk, k)
    assert m % bm == 0 and n % bn == 0 and k % bk == 0, (name, m, n, k, bm, bn, bk)
    nk = k // bk
    rot = lambda idx, r: idx if r is None else r(idx)
    if mode == "nn":
        a_spec = pl.BlockSpec((bm, bk), lambda i, j, kk: (i, kk))
        b_spec = pl.BlockSpec((bk, bn), lambda i, j, kk: (rot(kk, b_rot), j))
        dims = (((1,), (0,)), ((), ()))
    elif mode == "nt":
        a_spec = pl.BlockSpec((bm, bk), lambda i, j, kk: (i, kk))
        b_spec = pl.BlockSpec((bn, bk), lambda i, j, kk: (rot(j, b_rot), kk))
        dims = (((1,), (1,)), ((), ()))
    else:
        assert b_rot is None
        a_spec = pl.BlockSpec((bk, bm), lambda i, j, kk: (kk, i))
        b_spec = pl.BlockSpec((bk, bn), lambda i, j, kk: (kk, j))
        dims = (((0,), (0,)), ((), ()))

    def body(a_ref, b_ref, *rest):
        o_ref, acc_ref = rest[-2:]
        kk = pl.program_id(2)

        @pl.when(kk == 0)
        def _():
            acc_ref[...] = jnp.zeros_like(acc_ref)

        acc_ref[...] += lax.dot_general(a_ref[...], b_ref[...], dims, preferred_element_type=F32)

        @pl.when(kk == nk - 1)
        def _():
            o_ref[...] = acc_ref[...].astype(o_ref.dtype)

    return pl.pallas_call(
        body, name=name,
        grid=(m // bm, n // bn, nk),
        in_specs=[a_spec, b_spec] + [ANY_SPEC] * len(deps),
        out_specs=pl.BlockSpec((bm, bn), lambda i, j, kk: (rot(i, o_rot), j)),
        out_shape=jax.ShapeDtypeStruct((m, n), out_dtype),
        scratch_shapes=[pltpu.VMEM((bm, bn), F32)],
        compiler_params=_params(("parallel", "parallel", "arbitrary")),
    )(a, b, *deps)


def mm_ep(name, a, b, n_acc, acc_block, epilogue, extras, outs, tiles, deps=(), b_kn=False, k_map=None,
          consts=(), sums=()):
    m, k = a.shape
    bm, bn, bk = tiles
    bm = min(bm, m)
    nj = outs[0][0] // (outs[0][2] * bn)
    nk = k // bk
    assert m % bm == 0 and k % bk == 0 and b.shape[0 if b_kn else 1] == k, (name, a.shape, b.shape, tiles)
    assert not sums or nj == 1, name
    ne, nc, no, ns, nd = len(extras), len(consts), len(outs), len(sums), len(deps)
    dims = (((1,), (0,)), ((), ())) if b_kn else (((1,), (1,)), ((), ()))
    kmap = (lambda kk: kk) if k_map is None else k_map

    def body(*refs):
        a_ref, b_refs = refs[0], refs[1:1 + n_acc]
        e_refs = refs[1 + n_acc:1 + n_acc + ne + nc]
        first_out = 1 + n_acc + ne + nc + nd
        o_refs = refs[first_out:first_out + no]
        s_refs = refs[first_out + no:first_out + no + ns]
        acc_refs = refs[first_out + no + ns:]
        av = a_ref[...]
        prods = [lax.dot_general(av, b_ref[...], dims, preferred_element_type=F32) for b_ref in b_refs]

        if ns:
            @pl.when((pl.program_id(0) == 0) & (pl.program_id(2) == 0))
            def _():
                for s_ref in s_refs:
                    s_ref[...] = jnp.zeros_like(s_ref)

        def finish(accs):
            res = epilogue(accs, *[e[...] for e in e_refs])
            tiles_out, sums_out = res if ns else (res, ())
            for o_ref, v in zip(o_refs, tiles_out):
                o_ref[...] = v.astype(o_ref.dtype)
            for s_ref, v in zip(s_refs, sums_out):
                s_ref[...] += v

        if nk == 1:
            finish(prods)
        else:
            kk = pl.program_id(2)

            @pl.when(kk == 0)
            def _():
                for acc_ref in acc_refs:
                    acc_ref[...] = jnp.zeros_like(acc_ref)

            for acc_ref, p in zip(acc_refs, prods):
                acc_ref[...] += p

            @pl.when(kk == nk - 1)
            def _():
                finish([acc_ref[...] for acc_ref in acc_refs])

    in_specs = [pl.BlockSpec((bm, bk), lambda i, j, kk: (i, kk))]
    if b_kn:
        in_specs += [pl.BlockSpec((bk, bn), functools.partial(lambda i, j, kk, q: (kmap(kk), acc_block(j, q)), q=q))
                     for q in range(n_acc)]
    else:
        in_specs += [pl.BlockSpec((bn, bk), functools.partial(lambda i, j, kk, q: (acc_block(j, q), kmap(kk)), q=q))
                     for q in range(n_acc)]
    in_specs += [pl.BlockSpec((bm, w * bn), functools.partial(lambda i, j, kk, off: (i, j + off), off=off))
                 for (_, w, off) in extras]
    in_specs += [pl.BlockSpec((1, bn), lambda i, j, kk: (0, j)) for _ in consts]
    in_specs += [ANY_SPEC] * nd
    out_specs = [pl.BlockSpec((bm, w * bn), lambda i, j, kk: (i, j)) for (_, _, w) in outs]
    out_specs += [pl.BlockSpec((1, w), lambda i, j, kk: (0, 0)) for w in sums]
    out_shape = [jax.ShapeDtypeStruct((m, cols), dt) for (cols, dt, _) in outs]
    out_shape += [jax.ShapeDtypeStruct((1, w), F32) for w in sums]
    return pl.pallas_call(
        body, name=name, grid=(m // bm, nj, nk),
        in_specs=in_specs, out_specs=out_specs, out_shape=out_shape,
        scratch_shapes=[pltpu.VMEM((bm, bn), F32)] * (n_acc if nk > 1 else 0),
        compiler_params=_params(("arbitrary",) * 3 if sums else ("parallel", "parallel", "arbitrary")),
    )(a, *[b] * n_acc, *[e[0] for e in extras], *consts, *deps)


def rowwise(name, fn, rows, consts, out_rows, out_sums, ts, alias=None, deps=()):
    rows = [r if isinstance(r, tuple) else (r, r.shape[1], 0) for r in rows]
    out_rows = [o if len(o) == 4 else (o[0], o[1], o[0], 0) for o in out_rows]
    s = rows[0][0].shape[0]
    nt = s // ts
    nr, nc, no, ns = len(rows), len(consts), len(out_rows), len(out_sums)
    in_specs = [pl.BlockSpec((ts, w), functools.partial(lambda i, cb: (i, cb), cb=cb)) for (_, w, cb) in rows]
    in_specs += [pl.BlockSpec(c.shape, lambda i: (0, 0)) for c in consts]
    operands = [r[0] for r in rows] + list(consts)
    aliases = {}
    if alias is not None:
        in_specs.append(pl.BlockSpec(memory_space=pl.ANY))
        operands.append(alias[0])
        aliases = {nr + nc: alias[1]}
    in_specs += [ANY_SPEC] * len(deps)
    operands += list(deps)
    out_shape = [jax.ShapeDtypeStruct((s, tw), dt) for (_, dt, tw, _) in out_rows]
    out_shape += [jax.ShapeDtypeStruct((1, w), F32) for w in out_sums]
    out_specs = [pl.BlockSpec((ts, w), functools.partial(lambda i, cb: (i, cb), cb=cb)) for (w, _, _, cb) in out_rows]
    out_specs += [pl.BlockSpec((1, w), lambda i: (0, 0)) for w in out_sums]
    n_in = len(operands)

    def body(*refs):
        ins, outs = refs[:nr + nc], refs[n_in:]
        i = pl.program_id(0)
        ro, so = fn(*[r[...] for r in ins])
        for q in range(no):
            outs[q][...] = ro[q].astype(outs[q].dtype)
        if ns:
            @pl.when(i == 0)
            def _():
                for q in range(ns):
                    outs[no + q][...] = jnp.zeros_like(outs[no + q])

            for q in range(ns):
                outs[no + q][...] += so[q]

    return pl.pallas_call(
        body, name=name, grid=(nt,),
        in_specs=in_specs, out_specs=out_specs, out_shape=out_shape, input_output_aliases=aliases,
        compiler_params=_params(("arbitrary",) if ns else ("parallel",)),
    )(*operands)


def _colsum(v):
    return jnp.sum(v, axis=0, keepdims=True)


def _rms_stats(xv):
    r = lax.rsqrt(jnp.mean(xv * xv, axis=-1, keepdims=True) + EPS)
    return r, xv * r


def _rms_bwd(dxhat, xhat, r):
    return r * (dxhat - xhat * jnp.mean(dxhat * xhat, axis=-1, keepdims=True))


def _gelu(v):
    k = math.sqrt(2.0 / math.pi)
    t = jnp.tanh(k * (v + 0.044715 * v * v * v))
    return 0.5 * v * (1.0 + t), t


def _gelu_grad(v, t):
    k = math.sqrt(2.0 / math.pi)
    return 0.5 * (1.0 + t) + 0.5 * v * (1.0 - t * t) * k * (1.0 + 3.0 * 0.044715 * v * v)


CONV_TS = 256
CONV_CH = 64


def _ln_fwd(yc, g, b):
    mu = jnp.mean(yc, axis=-1, keepdims=True)
    xc = yc - mu
    rstd = lax.rsqrt(jnp.mean(xc * xc, axis=-1, keepdims=True) + EPS)
    nhat = xc * rstd
    return nhat, rstd, nhat * g + b


SUBL = 8


def _shifted_copies(buf, sh, ts):
    for b in range(1, SUBL):
        sh[b - 1] = buf[pl.ds(b, ts + HALO - SUBL), :]


def _shifted(buf, sh, start):
    b = start % SUBL
    if b == 0:
        return buf[pl.ds(start, CONV_CH), :]
    return sh[b - 1, pl.ds(start - b, CONV_CH), :]


def conv_fwd(z, w32, cb, lg, lb):
    s = z.shape[0]
    ts = CONV_TS
    nt = s // ts
    hb = ts // HALO

    def body(a_ref, g_ref, ah_ref, gh_ref, w_ref, cb_ref, lg_ref, lb_ref, yc_ref, s_ref, ubuf, ush):
        i = pl.program_id(0)
        first = (i > 0).astype(F32)
        ubuf[0:HALO, :] = ah_ref[...] * _sig(gh_ref[...]) * first
        ubuf[HALO:HALO + ts, :] = a_ref[...] * _sig(g_ref[...])
        _shifted_copies(ubuf, ush, ts)
        for c0 in range(0, ts, CONV_CH):
            acc = jnp.zeros((CONV_CH, CW), F32)
            for k in range(KC):
                acc = acc + w_ref[k:k + 1, :] * _shifted(ubuf, ush, c0 + k + 2)
            yc = acc + cb_ref[...]
            yc_ref[c0:c0 + CONV_CH, :] = yc
            _, _, ln = _ln_fwd(yc, lg_ref[...], lb_ref[...])
            s_ref[c0:c0 + CONV_CH, :] = (ln * _sig(ln)).astype(s_ref.dtype)

    cur = lambda cbk: pl.BlockSpec((ts, CW), functools.partial(lambda i, q: (i, q), q=cbk))
    prev = lambda cbk: pl.BlockSpec((HALO, CW), functools.partial(lambda i, q: (jnp.maximum(i * hb - 1, 0), q), q=cbk))
    const = lambda a: pl.BlockSpec(a.shape, lambda i: (0, 0))
    return pl.pallas_call(
        body, name="conv_fwd", grid=(nt,),
        in_specs=[cur(ZB_A), cur(ZB_G), prev(ZB_A), prev(ZB_G), const(w32), const(cb), const(lg), const(lb)],
        out_specs=[pl.BlockSpec((ts, CW), lambda i: (i, 0)), pl.BlockSpec((ts, CW), lambda i: (i, 0))],
        out_shape=[jax.ShapeDtypeStruct((s, CW), F32), jax.ShapeDtypeStruct((s, CW), BF16)],
        scratch_shapes=[pltpu.VMEM((HALO + ts, CW), F32), pltpu.VMEM((SUBL - 1, ts + HALO - SUBL, CW), F32)],
        compiler_params=_params(("parallel",)),
    )(z, z, z, z, w32, cb, lg, lb)


def conv_bwd(ds, yc, z, w32, lg, lb, dz):
    s = z.shape[0]
    ts = CONV_TS
    nt = s // ts
    hb = ts // HALO
    last_hb = s // HALO - 1

    def ln_bwd(dsv, ycv, g, b):
        nhat, rstd, ln = _ln_fwd(ycv, g, b)
        sg = _sig(ln)
        dln = dsv * (sg * (1.0 + ln * (1.0 - sg)))
        dnh = dln * g
        dyc = rstd * (dnh - jnp.mean(dnh, axis=-1, keepdims=True)
                      - nhat * jnp.mean(dnh * nhat, axis=-1, keepdims=True))
        return dyc, dln, nhat

    def body(ds_ref, yc_ref, dsn_ref, ycn_ref, a_ref, g_ref, ah_ref, gh_ref, w_ref, lg_ref, lb_ref, dz_in,
             dz_ref, dlg_ref, dlb_ref, dcb_ref, dw_ref, dbuf, ubuf, dsh, ush):
        i = pl.program_id(0)

        @pl.when(i == 0)
        def _():
            dlg_ref[...] = jnp.zeros_like(dlg_ref)
            dlb_ref[...] = jnp.zeros_like(dlb_ref)
            dcb_ref[...] = jnp.zeros_like(dcb_ref)
            dw_ref[...] = jnp.zeros_like(dw_ref)

        lg, lb = lg_ref[...], lb_ref[...]
        dyc, dln, nhat = ln_bwd(ds_ref[...], yc_ref[...], lg, lb)
        dlg_ref[...] += _colsum(dln * nhat)
        dlb_ref[...] += _colsum(dln)
        dcb_ref[...] += _colsum(dyc)
        dbuf[0:ts, :] = dyc
        nxt = (i < nt - 1).astype(F32)
        dbuf[ts:ts + HALO, :] = ln_bwd(dsn_ref[...], ycn_ref[...], lg, lb)[0] * nxt
        first = (i > 0).astype(F32)
        ubuf[0:HALO, :] = ah_ref[...] * _sig(gh_ref[...]) * first
        ubuf[HALO:HALO + ts, :] = a_ref[...] * _sig(g_ref[...])
        _shifted_copies(dbuf, dsh, ts)
        _shifted_copies(ubuf, ush, ts)
        for c0 in range(0, ts, CONV_CH):
            du = jnp.zeros((CONV_CH, CW), F32)
            dyc_c = dbuf[c0:c0 + CONV_CH, :]
            for k in range(KC):
                du = du + w_ref[k:k + 1, :] * _shifted(dbuf, dsh, c0 + KC - 1 - k)
                dw_ref[k:k + 1, :] += _colsum(dyc_c * _shifted(ubuf, ush, c0 + k + 2))
            av = a_ref[c0:c0 + CONV_CH, :]
            sg = _sig(g_ref[c0:c0 + CONV_CH, :])
            dz_ref[c0:c0 + CONV_CH, 0:CW] = (du * sg).astype(dz_ref.dtype)
            dz_ref[c0:c0 + CONV_CH, CW:2 * CW] = (du * av * sg * (1.0 - sg)).astype(dz_ref.dtype)

    cur = lambda w, cbk: pl.BlockSpec((ts, w), functools.partial(lambda i, q: (i, q), q=cbk))
    prev = lambda cbk: pl.BlockSpec((HALO, CW), functools.partial(lambda i, q: (jnp.maximum(i * hb - 1, 0), q), q=cbk))
    nxt_spec = pl.BlockSpec((HALO, CW), lambda i: (jnp.minimum((i + 1) * hb, last_hb), 0))
    const = lambda a: pl.BlockSpec(a.shape, lambda i: (0, 0))
    acc = lambda r: pl.BlockSpec((r, CW), lambda i: (0, 0))
    return pl.pallas_call(
        body, name="conv_bwd", grid=(nt,),
        in_specs=[cur(CW, 0), cur(CW, 0), nxt_spec, nxt_spec, cur(CW, ZB_A), cur(CW, ZB_G), prev(ZB_A), prev(ZB_G),
                  const(w32), const(lg), const(lb), pl.BlockSpec(memory_space=pl.ANY)],
        out_specs=[pl.BlockSpec((ts, 2 * CW), lambda i: (i, ZB_A // 2)), acc(1), acc(1), acc(1), acc(HALO)],
        out_shape=[jax.ShapeDtypeStruct(dz.shape, dz.dtype), jax.ShapeDtypeStruct((1, CW), F32),
                   jax.ShapeDtypeStruct((1, CW), F32), jax.ShapeDtypeStruct((1, CW), F32),
                   jax.ShapeDtypeStruct((HALO, CW), F32)],
        scratch_shapes=[pltpu.VMEM((ts + HALO, CW), F32), pltpu.VMEM((HALO + ts, CW), F32)]
        + [pltpu.VMEM((SUBL - 1, ts + HALO - SUBL, CW), F32)] * 2,
        input_output_aliases={11: 0},
        compiler_params=_params(("arbitrary",)),
    )(ds, yc, ds, yc, z, z, z, z, w32, lg, lb, dz)


SSM_TS = 512
GRP = 8


def _cmul(ar, ai, br, bi):
    return ar * br - ai * bi, ar * bi + ai * br


def _scan_tables(ar, ai, reverse):
    n = ar.shape[1]
    row = lax.broadcasted_iota(jnp.int32, (GRP, n), 0)
    dist = (GRP - 1 - row) if reverse else row
    one_r = jnp.broadcast_to(ar, (GRP, n))
    one_i = jnp.broadcast_to(ai, (GRP, n))
    p2r, p2i = _cmul(one_r, one_i, one_r, one_i)
    p4r, p4i = _cmul(p2r, p2i, p2r, p2i)
    steps = []
    for sft, (pr, pi) in ((1, (one_r, one_i)), (2, (p2r, p2i)), (4, (p4r, p4i))):
        keep = dist >= sft
        steps.append((jnp.where(keep, pr, 0.0), jnp.where(keep, pi, 0.0)))
    cr, ci = one_r, one_i
    accr, acci = one_r, one_i
    for e in range(1, GRP):
        cr, ci = _cmul(cr, ci, one_r, one_i)
        accr = jnp.where(dist == e, cr, accr)
        acci = jnp.where(dist == e, ci, acci)
    return steps, (accr, acci)


def _scan_group(xr, xi, steps, carry_tab, cr, ci, reverse):
    for sft, (tr, ti) in zip((1, 2, 4), steps):
        amt = (GRP - sft) if reverse else sft
        sr = pltpu.roll(xr, amt, 0)
        si = pltpu.roll(xi, amt, 0)
        xr, xi = xr + tr * sr - ti * si, xi + tr * si + ti * sr
    pr, pi = carry_tab
    xr = xr + pr * cr - pi * ci
    xi = xi + pr * ci + pi * cr
    return xr, xi


def ssm_fwd(z, wb_re, wb_im, wc, e_re, e_im, dvec):
    s = z.shape[0]
    ts = SSM_TS
    nt = s // ts
    ucol0 = ZB_U * CW // CB

    def body(u_ref, wbr_ref, wbi_ref, wc_ref, er_ref, ei_ref, d_ref, xr_ref, xi_ref, y_ref, gl_ref, car_r, car_i):
        i = pl.program_id(1)

        @pl.when(i == 0)
        def _():
            car_r[...] = jnp.zeros_like(car_r)
            car_i[...] = jnp.zeros_like(car_i)

        u = u_ref[...]
        ub = u.astype(BF16)
        xr_ref[...] = jnp.dot(ub, wbr_ref[0], preferred_element_type=F32)
        xi_ref[...] = jnp.dot(ub, wbi_ref[0], preferred_element_type=F32)
        steps, ctab = _scan_tables(er_ref[0], ei_ref[0], False)

        def grp(r, carry):
            cr, ci = carry
            r0 = pl.multiple_of(r * GRP, GRP)
            xr, xi = _scan_group(xr_ref[pl.ds(r0, GRP), :], xi_ref[pl.ds(r0, GRP), :], steps, ctab, cr, ci, False)
            xr_ref[pl.ds(r0, GRP), :] = xr
            xi_ref[pl.ds(r0, GRP), :] = xi
            return (jnp.broadcast_to(xr[GRP - 1:GRP, :], (GRP, SB)), jnp.broadcast_to(xi[GRP - 1:GRP, :], (GRP, SB)))

        cr, ci = lax.fori_loop(0, ts // GRP, grp, (car_r[...], car_i[...]))
        car_r[...] = cr
        car_i[...] = ci
        y = (jnp.dot(xr_ref[...].astype(BF16), wc_ref[0, 0:SB, :], preferred_element_type=F32)
             + jnp.dot(xi_ref[...].astype(BF16), wc_ref[0, SB:2 * SB, :], preferred_element_type=F32)
             + d_ref[0] * u)
        y_ref[...] = y
        gl_ref[...] = _gelu(y)[0].astype(gl_ref.dtype)

    blk3 = lambda a: pl.BlockSpec((1,) + a.shape[1:], lambda j, i: (j, 0, 0))
    return pl.pallas_call(
        body, name="ssm_fwd", grid=(NBLK, nt),
        in_specs=[pl.BlockSpec((ts, CB), lambda j, i: (i, ucol0 + j)),
                  blk3(wb_re), blk3(wb_im), blk3(wc), blk3(e_re), blk3(e_im), blk3(dvec)],
        out_specs=[pl.BlockSpec((ts, SB), lambda j, i: (i, j)), pl.BlockSpec((ts, SB), lambda j, i: (i, j)),
                   pl.BlockSpec((ts, CB), lambda j, i: (i, j)), pl.BlockSpec((ts, CB), lambda j, i: (i, j))],
        out_shape=[jax.ShapeDtypeStruct((s, NST), F32), jax.ShapeDtypeStruct((s, NST), F32),
                   jax.ShapeDtypeStruct((s, SW), F32), jax.ShapeDtypeStruct((s, SW), BF16)],
        scratch_shapes=[pltpu.VMEM((GRP, SB), F32), pltpu.VMEM((GRP, SB), F32)],
        compiler_params=_params(("parallel", "arbitrary")),
    )(z, wb_re, wb_im, wc, e_re, e_im, dvec)


def ssm_bwd(dgl, ypre, z, xs_re, xs_im, wbt_re, wbt_im, wct, e_re, e_im, dvec, dz):
    s = z.shape[0]
    ts = SSM_TS
    nt = s // ts
    ucol0 = ZB_U * CW // CB
    tn_dims = (((0,), (0,)), ((), ()))

    def body(dgl_ref, y_ref, u_ref, xr_ref, xi_ref, wbtr_ref, wbti_ref, wct_ref, er_ref, ei_ref, d_ref, dz_in,
             du_ref, dd_ref, dar_ref, dai_ref, dwbr_ref, dwbi_ref, dwc_ref,
             lr_ref, li_ref, car_r, car_i, acc_r, acc_i):
        i = pl.program_id(1)

        @pl.when(i == 0)
        def _():
            for ref in (car_r, car_i, acc_r, acc_i, dd_ref, dwbr_ref, dwbi_ref, dwc_ref):
                ref[...] = jnp.zeros_like(ref)

        u = u_ref[...]
        y = y_ref[...]
        dy = dgl_ref[...] * _gelu_grad(y, _gelu(y)[1])
        dd_ref[0] += _colsum(dy * u)
        dyb = dy.astype(BF16)
        dxo = jnp.dot(dyb, wct_ref[0], preferred_element_type=F32)
        lr_ref[...] = dxo[:, 0:SB]
        li_ref[...] = dxo[:, SB:2 * SB]
        steps, ctab = _scan_tables(er_ref[0], -ei_ref[0], True)
        row = lax.broadcasted_iota(jnp.int32, (GRP, SB), 0)

        def grp(q, carry):
            cr, ci, ar, ai = carry
            r0 = pl.multiple_of((ts // GRP - 1 - q) * GRP, GRP)
            lr, li = _scan_group(lr_ref[pl.ds(r0, GRP), :], li_ref[pl.ds(r0, GRP), :], steps, ctab, cr, ci, True)
            lr_ref[pl.ds(r0, GRP), :] = lr
            li_ref[pl.ds(r0, GRP), :] = li
            nr = jnp.where(row == GRP - 1, cr, pltpu.roll(lr, GRP - 1, 0))
            ni = jnp.where(row == GRP - 1, ci, pltpu.roll(li, GRP - 1, 0))
            xr = xr_ref[pl.ds(r0, GRP), :]
            xi = xi_ref[pl.ds(r0, GRP), :]
            ar = ar + nr * xr + ni * xi
            ai = ai + ni * xr - nr * xi
            return (jnp.broadcast_to(lr[0:1, :], (GRP, SB)), jnp.broadcast_to(li[0:1, :], (GRP, SB)), ar, ai)

        cr, ci, ar, ai = lax.fori_loop(0, ts // GRP, grp, (car_r[...], car_i[...], acc_r[...], acc_i[...]))
        car_r[...] = cr
        car_i[...] = ci
        acc_r[...] = ar
        acc_i[...] = ai

        @pl.when(i == nt - 1)
        def _():
            dar_ref[0] = _colsum(ar)
            dai_ref[0] = _colsum(ai)

        lrb = lr_ref[...].astype(BF16)
        lib = li_ref[...].astype(BF16)
        du = (jnp.dot(lrb, wbtr_ref[0], preferred_element_type=F32)
              + jnp.dot(lib, wbti_ref[0], preferred_element_type=F32) + d_ref[0] * dy)
        du_ref[...] = du.astype(du_ref.dtype)
        ub = u.astype(BF16)
        dwbr_ref[0] += lax.dot_general(ub, lrb, tn_dims, preferred_element_type=F32)
        dwbi_ref[0] += lax.dot_general(ub, lib, tn_dims, preferred_element_type=F32)
        dwc_ref[0, 0:SB, :] += lax.dot_general(xr_ref[...].astype(BF16), dyb, tn_dims, preferred_element_type=F32)
        dwc_ref[0, SB:2 * SB, :] += lax.dot_general(xi_ref[...].astype(BF16), dyb, tn_dims, preferred_element_type=F32)

    rev = lambda i: nt - 1 - i
    blk3 = lambda a: pl.BlockSpec((1,) + a.shape[1:], lambda j, i: (j, 0, 0))
    acc3 = lambda r, c: pl.BlockSpec((1, r, c), lambda j, i: (j, 0, 0))
    return pl.pallas_call(
        body, name="ssm_bwd", grid=(NBLK, nt),
        in_specs=[pl.BlockSpec((ts, CB), lambda j, i: (rev(i), j)), pl.BlockSpec((ts, CB), lambda j, i: (rev(i), j)),
                  pl.BlockSpec((ts, CB), lambda j, i: (rev(i), ucol0 + j)),
                  pl.BlockSpec((ts, SB), lambda j, i: (rev(i), j)), pl.BlockSpec((ts, SB), lambda j, i: (rev(i), j)),
                  blk3(wbt_re), blk3(wbt_im), blk3(wct), blk3(e_re), blk3(e_im), blk3(dvec),
                  pl.BlockSpec(memory_space=pl.ANY)],
        out_specs=[pl.BlockSpec((ts, CB), lambda j, i: (rev(i), ucol0 + j)),
                   acc3(1, CB), acc3(1, SB), acc3(1, SB), acc3(CB, SB), acc3(CB, SB), acc3(2 * SB, CB)],
        out_shape=[jax.ShapeDtypeStruct(dz.shape, dz.dtype),
                   jax.ShapeDtypeStruct((NBLK, 1, CB), F32),
                   jax.ShapeDtypeStruct((NBLK, 1, SB), F32), jax.ShapeDtypeStruct((NBLK, 1, SB), F32),
                   jax.ShapeDtypeStruct((NBLK, CB, SB), F32), jax.ShapeDtypeStruct((NBLK, CB, SB), F32),
                   jax.ShapeDtypeStruct((NBLK, 2 * SB, CB), F32)],
        scratch_shapes=[pltpu.VMEM((ts, SB), F32), pltpu.VMEM((ts, SB), F32)] + [pltpu.VMEM((GRP, SB), F32)] * 4,
        input_output_aliases={11: 0},
        compiler_params=_params(("parallel", "arbitrary")),
    )(dgl, ypre, z, xs_re, xs_im, wbt_re, wbt_im, wct, e_re, e_im, dvec, dz)


def _disc(a_re, a_im, log_dt, b_re, b_im, expand):
    dt = jnp.dot(expand, jnp.exp(log_dt), preferred_element_type=F32, precision=lax.Precision.HIGHEST)
    mag = jnp.exp(dt * a_re)
    e_re, e_im = mag * jnp.cos(dt * a_im), mag * jnp.sin(dt * a_im)
    n_re, n_im = e_re - 1.0, e_im
    den = a_re * a_re + a_im * a_im
    q_re = (n_re * a_re + n_im * a_im) / den
    q_im = (n_im * a_re - n_re * a_im) / den
    return e_re, e_im, q_re * b_re - q_im * b_im, q_re * b_im + q_im * b_re


def _whole(a):
    return pl.BlockSpec(a.shape, functools.partial(lambda n: (0,) * n, n=a.ndim))


def disc_fwd(a_re, a_im, log_dt, b_re, b_im, expand):
    def body(ar, ai, ld, br, bi, ex, er_o, ei_o, bbr_o, bbi_o):
        er, ei, bbr, bbi = _disc(ar[...], ai[...], ld[...], br[...], bi[...], ex[...])
        er_o[...] = er
        ei_o[...] = ei
        bbr_o[...] = bbr
        bbi_o[...] = bbi

    ins = (a_re, a_im, log_dt, b_re, b_im, expand)
    outs = [jax.ShapeDtypeStruct(a_re.shape, F32)] * 2 + [jax.ShapeDtypeStruct(b_re.shape, F32)] * 2
    return pl.pallas_call(body, name="disc_fwd", in_specs=[_whole(a) for a in ins],
                          out_specs=[_whole(o) for o in outs], out_shape=outs, compiler_params=_params())(*ins)


def disc_bwd(a_re, a_im, log_dt, b_re, b_im, expand, de_re, de_im, dbb_re, dbb_im):
    def body(ar, ai, ld, br, bi, ex, der, dei, dbr, dbi, o_ar, o_ai, o_ld, o_br, o_bi):
        exv = ex[...]
        _, vjp = jax.vjp(lambda *p: _disc(*p, exv), ar[...], ai[...], ld[...], br[...], bi[...])
        g = vjp((der[...], dei[...], dbr[...], dbi[...]))
        for o, v in zip((o_ar, o_ai, o_ld, o_br, o_bi), g):
            o[...] = v

    ins = (a_re, a_im, log_dt, b_re, b_im, expand, de_re, de_im, dbb_re, dbb_im)
    outs = [jax.ShapeDtypeStruct(a.shape, F32) for a in (a_re, a_im, log_dt, b_re, b_im)]
    return pl.pallas_call(body, name="disc_bwd", in_specs=[_whole(a) for a in ins],
                          out_specs=[_whole(o) for o in outs], out_shape=outs, compiler_params=_params())(*ins)


def mod_fwd(c_all, w_ada, b_cols):
    def body(c_ref, w_ref, b_ref, act_ref, mod_ref):
        cv = c_ref[...]
        act = cv * _sig(cv)
        act_ref[...] = act
        mod_ref[...] = jnp.dot(act, w_ref[...], preferred_element_type=F32, precision=lax.Precision.HIGHEST) + b_ref[...]

    ins = (c_all, w_ada, b_cols)
    outs = [jax.ShapeDtypeStruct(c_all.shape, F32), jax.ShapeDtypeStruct((NDEV, w_ada.shape[1]), F32)]
    return pl.pallas_call(body, name="mod_fwd", in_specs=[_whole(a) for a in ins],
                          out_specs=[_whole(o) for o in outs], out_shape=outs, compiler_params=_params())(*ins)


def ada_grad(act_all, dmod_cols):
    def body(a_ref, d_ref, o_ref):
        o_ref[...] = lax.dot_general(a_ref[...], d_ref[...], (((0,), (0,)), ((), ())),
                                     preferred_element_type=F32, precision=lax.Precision.HIGHEST)

    out = jax.ShapeDtypeStruct((act_all.shape[1], dmod_cols.shape[1]), F32)
    return pl.pallas_call(body, name="ada_grad", in_specs=[_whole(act_all), _whole(dmod_cols)],
                          out_specs=_whole(out), out_shape=out, compiler_params=_params())(act_all, dmod_cols)


def _adam_math(w, g, m, v):
    m2 = ADAM_B1 * m + (1.0 - ADAM_B1) * g
    v2 = ADAM_B2 * v + (1.0 - ADAM_B2) * (g * g)
    m_hat = m2 / (1.0 - ADAM_B1 ** ADAM_STEP)
    v_hat = v2 / (1.0 - ADAM_B2 ** ADAM_STEP)
    delta = -ADAM_LR * (m_hat / (jnp.sqrt(v_hat) + ADAM_EPS) + ADAM_WD * w)
    return delta, m2, v2


def adam(name, w, g, m, v):
    r, c = w.shape
    tr = max(t for t in range(8, min(r, 512) + 1, 8) if r % t == 0)

    def body(w_ref, g_ref, m_ref, v_ref, d_o, m_o, v_o):
        d, m2, v2 = _adam_math(w_ref[...], g_ref[...], m_ref[...], v_ref[...])
        d_o[...] = d
        m_o[...] = m2
        v_o[...] = v2

    spec = pl.BlockSpec((tr, c), lambda i: (i, 0))
    out = jax.ShapeDtypeStruct((r, c), F32)
    return pl.pallas_call(body, name=name, grid=(r // tr,), in_specs=[spec] * 4, out_specs=[spec] * 3,
                          out_shape=[out] * 3, compiler_params=_params(("parallel",)))(w, g, m, v)


def adam_many(name, ws, gs, ms, vs):
    n = len(ws)

    def body(*refs):
        ins, outs = refs[:4 * n], refs[4 * n:]
        for q in range(n):
            d, m2, v2 = _adam_math(ins[q][...], ins[n + q][...], ins[2 * n + q][...], ins[3 * n + q][...])
            outs[q][...] = d
            outs[n + q][...] = m2
            outs[2 * n + q][...] = v2

    operands = list(ws) + list(gs) + list(ms) + list(vs)
    outs = [jax.ShapeDtypeStruct(w.shape, F32) for w in ws] * 3
    return pl.pallas_call(body, name=name, in_specs=[_whole(a) for a in operands],
                          out_specs=[_whole(o) for o in outs], out_shape=outs, compiler_params=_params())(*operands)


def _rows_tile(r, most):
    best = None
    for t in range(16, min(r, most) + 1, 16):
        if r % t == 0:
            best = t
    assert best is not None, r
    return best


def sum_slots(name, slots, out_dtype=F32):
    n, r, c = slots.shape
    tr = _rows_tile(r, max(16, (2 * 1024 * 1024) // (n * c)))

    def body(s_ref, o_ref):
        acc = s_ref[0].astype(F32)
        for q in range(1, n):
            acc = acc + s_ref[q].astype(F32)
        o_ref[...] = acc.astype(o_ref.dtype)

    return pl.pallas_call(body, name=name, grid=(r // tr,),
                          in_specs=[pl.BlockSpec((n, tr, c), lambda i: (0, i, 0))],
                          out_specs=pl.BlockSpec((tr, c), lambda i: (i, 0)),
                          out_shape=jax.ShapeDtypeStruct((r, c), out_dtype), compiler_params=_params(("parallel",)))(slots)


HBM_SPEC = pl.BlockSpec(memory_space=pltpu.HBM)


def _coords():
    return lax.axis_index("x"), lax.axis_index("y"), lax.axis_index("c")


def _linear(x, y, c):
    return 4 * x + 2 * y + c


def all_gather(name, shards):
    nq = len(shards)

    def body(*refs):
        xs, outs = refs[:nq], refs[nq:2 * nq]
        send_sems, recv_sems, local_sems = refs[2 * nq:2 * nq + 3]
        bufs = refs[2 * nq + 3:]
        x, y, cc = _coords()
        me, sibling = (x, y, cc), (x, y, 1 - cc)
        chips = [(1 - x, y), (x, 1 - y), (1 - x, 1 - y)]

        def slot(q, px, py, pc):
            return outs[q].at[_linear(px, py, pc)]

        def copy(q, k, block, to, src=None):
            return pltpu.make_async_remote_copy(
                src_ref=slot(q, *block) if src is None else src, dst_ref=slot(q, *block),
                send_sem=send_sems.at[7 * q + k], recv_sem=recv_sems.at[7 * q + k], device_id=to, device_id_type=MESH)

        loads = [pltpu.make_async_copy(xs[q], bufs[q], local_sems.at[q]) for q in range(nq)]
        for cp in loads:
            cp.start()
        for cp in loads:
            cp.wait()
        mine = [pltpu.make_async_copy(bufs[q], slot(q, *me), local_sems.at[q]) for q in range(nq)]
        first = []
        for q in range(nq):
            first.append(copy(q, 0, me, sibling, src=bufs[q]))
            first += [copy(q, 1 + j, me, (*chip, cc), src=bufs[q]) for j, chip in enumerate(chips)]
        for cp in mine + first:
            cp.start()
        passed = []
        for q in range(nq):
            for j, chip in enumerate(chips):
                copy(q, 1 + j, (*chip, cc), me).wait_recv()
                passed.append(copy(q, 4 + j, (*chip, cc), sibling))
                passed[-1].start()
        for q in range(nq):
            copy(q, 0, sibling, me).wait_recv()
            for j, chip in enumerate(chips):
                copy(q, 4 + j, (*chip, 1 - cc), me).wait_recv()
        for cp in first + passed:
            cp.wait_send()
        for cp in mine:
            cp.wait()

    return pl.pallas_call(
        body, name=name, in_specs=[HBM_SPEC] * nq, out_specs=[HBM_SPEC] * nq,
        out_shape=[jax.ShapeDtypeStruct((NDEV,) + s.shape, s.dtype) for s in shards],
        scratch_shapes=[pltpu.SemaphoreType.DMA((7 * nq,)), pltpu.SemaphoreType.DMA((7 * nq,)),
                        pltpu.SemaphoreType.DMA((nq,))] + [pltpu.VMEM(s.shape, s.dtype) for s in shards],
    )(*shards)


NCHIP = 4


SEM_SPEC = pl.BlockSpec(memory_space=pltpu.SEMAPHORE)
EFFECT = pltpu.SideEffectType.DATAFLOW_SIDE_EFFECTING


def _peer(x, y, cc, k):
    fx, fy, fc = (k >> 2) & 1, (k >> 1) & 1, k & 1
    return (x + fx - 2 * fx * x, y + fy - 2 * fy * y, cc + fc - 2 * fc * cc)


def gather_plan(srcs, lands, coords):
    x, y, cc = coords
    me = _linear(x, y, cc)
    return [(s, l.at[me], _peer(x, y, cc, k)) for s, l in zip(srcs, lands) for k in range(1, NDEV)]


def near_plan(srcs, lands, coords):
    x, y, cc = coords
    me = _linear(x, y, cc)
    peers = [(x, y, 1 - cc)] + [_peer(x, y, cc, 2 * k) for k in range(1, NCHIP)]
    return [(s, l.at[me], p) for s, l in zip(srcs, lands) for p in peers]


def pass_on_plan(srcs, lands, coords):
    x, y, cc = coords
    out = []
    for l in srcs:
        for k in range(1, NCHIP):
            px, py, _ = _peer(x, y, cc, 2 * k)
            slot = _linear(px, py, cc)
            out.append((l.at[slot], l.at[slot], (x, y, 1 - cc)))
    return out


def pair_plan(srcs, lands, coords):
    x, y, cc = coords
    return [(s.at[2 * chip + 1 - cc], l.at[chip], (x, y, 1 - cc)) for s, l in zip(srcs, lands) for chip in range(NCHIP)]


def chip_plan(srcs, lands, coords):
    x, y, cc = coords
    out = []
    for s, l in zip(srcs, lands):
        for k in range(1, NCHIP):
            px, py, _ = _peer(x, y, cc, 2 * k)
            out.append((s.at[2 * px + py], l.at[k - 1], (px, py, cc)))
    return out


def _remote(copy, i, send_sems, recv_sems):
    src, dst, dev = copy
    return pltpu.make_async_remote_copy(src_ref=src, dst_ref=dst, send_sem=send_sems.at[i], recv_sem=recv_sems.at[i],
                                        device_id=dev, device_id_type=MESH)


def exchange_start(name, plan, ncopy, srcs, land_shapes, deps=()):
    ns, nl, nd = len(srcs), len(land_shapes), len(deps)

    def body(*refs):
        s, l = refs[:ns], refs[ns:ns + nl]
        send_sems, recv_sems = refs[ns + nl + nd], refs[ns + nl + nd + 1]
        token = refs[-1]
        for i, cp in enumerate(plan(s, l, _coords())):
            _remote(cp, i, send_sems, recv_sems).start()
        token[...] = jnp.zeros_like(token)

    hbm = lambda a: pltpu.with_memory_space_constraint(a, pltpu.HBM)
    lands = [lax.empty(shp, dt) for shp, dt in land_shapes]
    thru = [pltpu.HBM(a.shape, a.dtype) for a in list(srcs) + lands]
    outs = pl.pallas_call(
        body, name=name,
        in_specs=[HBM_SPEC] * (ns + nl) + [ANY_SPEC] * nd,
        out_specs=(SEM_SPEC, SEM_SPEC, *[HBM_SPEC] * (ns + nl), pl.BlockSpec(memory_space=pltpu.VMEM)),
        out_shape=(pltpu.SemaphoreType.DMA((ncopy,)), pltpu.SemaphoreType.DMA((ncopy,)), *thru,
                   jax.ShapeDtypeStruct((8, LANE), F32)),
        input_output_aliases={i: 2 + i for i in range(ns + nl)},
        compiler_params=pltpu.CompilerParams(has_side_effects=EFFECT),
    )(*[hbm(a) for a in srcs], *[hbm(a) for a in lands], *deps)
    return outs[0], outs[1], list(outs[2:2 + ns]), list(outs[2 + ns:2 + ns + nl]), outs[-1]


def exchange_wait(name, plan, started, after, place_own=False):
    send_sems, recv_sems, srcs, lands, _ = started
    ns, nl = len(srcs), len(lands)

    def body(*refs):
        s, l = refs[:ns], refs[ns:ns + nl]
        send_sems, recv_sems = refs[ns + nl], refs[ns + nl + 1]
        l_out = refs[2 * ns + nl + 3:2 * ns + 2 * nl + 3]
        scratch = refs[2 * ns + 2 * nl + 3:]
        copies = [_remote(cp, i, send_sems, recv_sems) for i, cp in enumerate(plan(s, l, _coords()))]
        if place_own:
            me = _linear(*_coords())
            local_sems, bufs = scratch[0], scratch[1:]
            loads = [pltpu.make_async_copy(s[q], bufs[q], local_sems.at[q]) for q in range(ns)]
            for cp in loads:
                cp.start()
            for cp in loads:
                cp.wait()
            stores = [pltpu.make_async_copy(bufs[q], l_out[q].at[me], local_sems.at[q]) for q in range(ns)]
            for cp in stores:
                cp.start()
        for cp in copies:
            cp.wait_recv()
        for cp in copies:
            cp.wait_send()
        if place_own:
            for cp in stores:
                cp.wait()

    scratch_shapes = []
    if place_own:
        scratch_shapes = [pltpu.SemaphoreType.DMA((ns,))] + [pltpu.VMEM(a.shape, a.dtype) for a in srcs]
    outs = pl.pallas_call(
        body, name=name,
        in_specs=[HBM_SPEC] * (ns + nl) + [SEM_SPEC, SEM_SPEC, ANY_SPEC],
        out_specs=[HBM_SPEC] * (ns + nl),
        out_shape=[pltpu.HBM(a.shape, a.dtype) for a in srcs + lands],
        input_output_aliases={i: i for i in range(ns + nl)},
        scratch_shapes=scratch_shapes,
        compiler_params=pltpu.CompilerParams(has_side_effects=EFFECT),
    )(*srcs, *lands, send_sems, recv_sems, after)
    return list(outs[:ns]), list(outs[ns:])


def pair_sum(name, g, recv):
    _, r, c = g.shape
    tr = _rows_tile(r, 512)

    def body(g_ref, r_ref, o_ref):
        own = jnp.where(lax.axis_index("c") == 0, g_ref[0, 0], g_ref[0, 1])
        o_ref[0] = (own.astype(F32) + r_ref[0].astype(F32)).astype(o_ref.dtype)

    return pl.pallas_call(
        body, name=name, grid=(NCHIP, r // tr),
        in_specs=[pl.BlockSpec((1, 2, tr, c), lambda k, i: (k, 0, i, 0)), pl.BlockSpec((1, tr, c), lambda k, i: (k, i, 0))],
        out_specs=pl.BlockSpec((1, tr, c), lambda k, i: (k, i, 0)),
        out_shape=jax.ShapeDtypeStruct((NCHIP, r, c), g.dtype), compiler_params=_params(("parallel", "parallel")),
    )(g.reshape(NCHIP, 2, r, c), recv)


def chip_sum(name, partial, recv):
    _, r, c = partial.shape
    tr = _rows_tile(r, 512)

    def body(p_ref, r_ref, o_ref):
        chip = 2 * lax.axis_index("x") + lax.axis_index("y")
        own = p_ref[0]
        for k in range(1, NCHIP):
            own = jnp.where(chip == k, p_ref[k], own)
        acc = own.astype(F32)
        for k in range(NCHIP - 1):
            acc = acc + r_ref[k].astype(F32)
        o_ref[...] = acc

    return pl.pallas_call(
        body, name=name, grid=(r // tr,),
        in_specs=[pl.BlockSpec((NCHIP, tr, c), lambda i: (0, i, 0)), pl.BlockSpec((NCHIP - 1, tr, c), lambda i: (0, i, 0))],
        out_specs=pl.BlockSpec((tr, c), lambda i: (i, 0)),
        out_shape=jax.ShapeDtypeStruct((r, c), F32), compiler_params=_params(("parallel",)),
    )(partial, recv)


def _block_diag(w, rows_per, cols_per):
    w = w.reshape(NBLK, 8, rows_per, cols_per)
    eye = jnp.eye(8, dtype=w.dtype)
    out = w[:, :, :, None, :] * eye[None, :, None, :, None]
    return out.reshape(NBLK, 8 * rows_per, 8 * cols_per)


def _diag_blocks(wd, rows_per, cols_per):
    wd = wd.reshape(NBLK, 8, rows_per, 8, cols_per)
    idx = jnp.arange(8)
    return wd[:, idx, :, idx, :].transpose(1, 0, 2, 3).reshape(NG, rows_per, cols_per)


def _pad_rows(v, mult):
    n = v.shape[0]
    return jnp.pad(v, (0, (-n) % mult))


def kernel(x, c, w_ada, b_ada, norm1_g, w_in, conv_w, conv_b, conv_ln_g, conv_ln_b, conv_proj, ssm_a_re, ssm_a_im, ssm_b_re, ssm_b_im, ssm_c_re, ssm_c_im, ssm_d, ssm_log_dt, ssm_glu, w_out, norm2_g, w_ffn_in, w_ffn_out, final_g, loss_target, m_w_ada, m_b_ada, m_norm1_g, m_w_in, m_conv_w, m_conv_b, m_conv_ln_g, m_conv_ln_b, m_conv_proj, m_ssm_a_re, m_ssm_a_im, m_ssm_b_re, m_ssm_b_im, m_ssm_c_re, m_ssm_c_im, m_ssm_d, m_ssm_log_dt, m_ssm_glu, m_w_out, m_norm2_g, m_w_ffn_in, m_w_ffn_out, m_final_g, v_w_ada, v_b_ada, v_norm1_g, v_w_in, v_conv_w, v_conv_b, v_conv_ln_g, v_conv_ln_b, v_conv_proj, v_ssm_a_re, v_ssm_a_im, v_ssm_b_re, v_ssm_b_im, v_ssm_c_re, v_ssm_c_im, v_ssm_d, v_ssm_log_dt, v_ssm_glu, v_w_out, v_norm2_g, v_w_ffn_in, v_w_ffn_out, v_final_g):
    me = _linear(*_coords())
    xs = x[0]
    tgt = loss_target[0]
    seq = xs.shape[0]

    flat = lambda g: g.reshape(NDEV * g.shape[1], g.shape[2])
    c_all, cw_g = all_gather("gather_c_conv_w", [c, conv_w[0]])
    w_in_s = w_in[0].T.astype(BF16)
    mids = [p.astype(BF16) for p in (conv_proj[0].T, ssm_glu[0].T, w_out[0])]
    ffns = [p.astype(BF16) for p in (w_ffn_in[0].T, w_ffn_out[0])]
    zone = lambda p: ((NDEV,) + p.shape, p.dtype)
    in_go = exchange_start("gather_in_start", near_plan, NCHIP, [w_in_s], [zone(w_in_s)], deps=[c_all])
    mids_go = exchange_start("gather_mid_start", gather_plan, 7 * len(mids), mids, [zone(p) for p in mids],
                             deps=[in_go[4]])
    ffns_go = exchange_start("gather_ffn_start", gather_plan, 7 * len(ffns), ffns, [zone(p) for p in ffns],
                             deps=[mids_go[4]])

    ncol = w_ada.shape[2]
    c_all = c_all.reshape(NDEV, D)
    b_cols = lax.dynamic_slice_in_dim(b_ada, me * ncol, ncol, axis=1)
    act_all, mod_cols = mod_fwd(c_all, w_ada[0], b_cols)
    (mod_all,) = all_gather("gather_mod", [mod_cols])
    mod = lax.dynamic_index_in_dim(mod_all, me, axis=1, keepdims=False).reshape(NMOD, D)
    sh1, sc1, g1, sh2, sc2, g2 = [mod[q:q + 1] for q in range(NMOD)]

    expand = jnp.repeat(jnp.eye(NG, dtype=F32), NP, axis=0)
    a_re_c, a_im_c = ssm_a_re.reshape(NST, 1), ssm_a_im.reshape(NST, 1)
    ldt_c = ssm_log_dt.reshape(NG, 1)
    b_re_r, b_im_r = ssm_b_re.reshape(NST, GH), ssm_b_im.reshape(NST, GH)
    e_re, e_im, bb_re, bb_im = disc_fwd(a_re_c, a_im_c, ldt_c, b_re_r, b_im_r, expand)
    e_re_b, e_im_b = e_re.reshape(NBLK, 1, SB), e_im.reshape(NBLK, 1, SB)
    bb_re_g, bb_im_g = bb_re.reshape(NG, NP, GH), bb_im.reshape(NG, NP, GH)
    wbt_re = _block_diag(bb_re_g, NP, GH)
    wbt_im = _block_diag(bb_im_g, NP, GH)
    wb_re, wb_im = wbt_re.transpose(0, 2, 1), wbt_im.transpose(0, 2, 1)
    wct = jnp.concatenate([_block_diag(ssm_c_re[0], GH, NP), -_block_diag(ssm_c_im[0], GH, NP)], axis=2)
    wc = wct.transpose(0, 2, 1)
    to_b = lambda a: a.astype(BF16)
    dvec = ssm_d.reshape(NBLK, 1, CB)

    n1g = norm1_g

    def f_norm1(xv, g, sc, sh):
        _, xh = _rms_stats(xv)
        return [xh * g * (1.0 + sc) + sh], []

    (h1,) = rowwise("norm1", f_norm1, [xs], [n1g, sc1, sh1], [(D, BF16)], [], 512, deps=[ffns_go[4]])
    _, (w_in_land,) = exchange_wait("gather_in_wait", near_plan, in_go, h1, place_own=True)
    pass_go = exchange_start("gather_in_pass_start", pass_on_plan, NCHIP - 1, [w_in_land], [])
    (w_in_g,), _ = exchange_wait("gather_in_pass_wait", pass_on_plan, pass_go, pass_go[4])
    w_in_t = flat(w_in_g)
    z = mm("mm_in", h1, w_in_t, "nt", tiles=(2048, CW, 1024), b_rot=Z_ROT)

    conv_w_full = cw_g.transpose(1, 0, 2).reshape(KC, CW)
    w32 = jnp.pad(conv_w_full, ((0, HALO - KC), (0, 0)))
    yc, s_act = conv_fwd(z, w32, conv_b, conv_ln_g, conv_ln_b)
    conv_proj_t, ssm_glu_t, w_out_f = [
        flat(g) for g in exchange_wait("gather_mid_wait", gather_plan, mids_go, s_act, place_own=True)[1]]
    y_conv = mm("mm_conv_proj", s_act, conv_proj_t, "nt")

    xs_re, xs_im, ypre, gl = ssm_fwd(z, to_b(wb_re), to_b(wb_im), to_b(wc), e_re_b, e_im_b, dvec)
    n_mrg = D // MRG_BLK

    pair_of = lambda t, n: t // 2 + (t % 2) * n

    def ep_merge(accs, yc_v, gates):
        za, zb = accs
        glc, gls = gates[:, 0:MRG_BLK], gates[:, MRG_BLK:2 * MRG_BLK]
        return [_sig(glc) * yc_v + _sig(gls) * (za * _sig(zb)), jnp.concatenate([za, zb], axis=1)]

    merged, z2_pair = mm_ep("mm_ssm_glu", gl, ssm_glu_t, 2, lambda j, q: j + q * n_mrg, ep_merge,
                            [(y_conv, 1, 0), (z, 2, 0)], [(D, BF16, 1), (2 * D, BF16, 2)], (512, MRG_BLK, SW))
    row_tiles = lambda bk: (512, D, bk)
    whole = lambda j, q: j

    def ep_norm2(accs, xv, g1v, g, sc, sh):
        (o1v,) = accs
        x1v = xv + g1v * o1v
        _, xh = _rms_stats(x1v)
        return [x1v, xh * g * (1.0 + sc) + sh, o1v]

    x1, h2, o1 = mm_ep("mm_out", merged, w_out_f, 1, whole, ep_norm2, [(xs, 1, 0)],
                       [(D, F32, 1), (D, BF16, 1), (D, F32, 1)], row_tiles(D), b_kn=True,
                       consts=[g1, norm2_g, sc2, sh2])
    w_ffn_in_t, w_ffn_out_f = [
        flat(g) for g in exchange_wait("gather_ffn_wait", gather_plan, ffns_go, h2, place_own=True)[1]]
    ffn_tiles = (512, FFN_BLK, 1024)
    n_ffn_blk = FH // FFN_BLK
    pair_map = lambda t: t // 2 + (t % 2) * n_ffn_blk

    def ep_swiglu(accs):
        fg, fu = accs
        return [fg * _sig(fg) * fu, jnp.concatenate([fg, fu], axis=1)]

    act, f_pair = mm_ep("mm_ffn_in", h2, w_ffn_in_t, 2, lambda j, q: j + q * n_ffn_blk, ep_swiglu, [],
                        [(FH, BF16, 1), (2 * FH, BF16, 2)], ffn_tiles)
    fg_row = final_g.reshape(1, D)

    def ep_final(accs, x1v, tv, g2v, fg):
        (o2v,) = accs
        x2v = x1v + g2v * o2v
        r, xh = _rms_stats(x2v)
        yv = xh * fg
        err = yv - tv
        loss = jnp.sum(_colsum(err * err), axis=1, keepdims=True) * (0.5 / D)
        dy = err * (1.0 / D)
        dx2 = _rms_bwd(dy * fg, xh, r)
        return ([dx2, g2v * dx2],
                [jnp.broadcast_to(loss, (1, LANE)), _colsum(dy * xh), _colsum(dx2 * o2v)])

    dx2, do2, loss_l, d_final_g, d_g2 = mm_ep(
        "mm_ffn_out", act, w_ffn_out_f, 1, whole, ep_final, [(x1, 1, 0), (tgt, 1, 0)],
        [(D, F32, 1), (D, BF16, 1)], row_tiles(FFN_BLK), b_kn=True, consts=[g2, fg_row], sums=[LANE, D, D])

    g_ffn_out = mm("mm_g_ffn_out", act, do2, "tn", BF16, tiles=(FFN_BLK, 1024, 1024))

    def ep_dswiglu(accs, fp):
        (da,) = accs
        fg, fu = fp[:, 0:FFN_BLK].astype(F32), fp[:, FFN_BLK:2 * FFN_BLK].astype(F32)
        sg = _sig(fg)
        return [jnp.concatenate([da * fu * (sg * (1.0 + fg * (1.0 - sg))), da * (fg * sg)], axis=1)]

    (df,) = mm_ep("mm_dact", do2, w_ffn_out_f, 1, lambda j, q: j, ep_dswiglu, [(f_pair, 2, 0)],
                  [(2 * FH, BF16, 2)], ffn_tiles)
    g_ffn_in_t = mm("mm_g_ffn_in", df, h2, "tn", BF16, tiles=(FFN_BLK, 1024, 1024), o_rot=pair_map)

    def pair_go(tag, grads_t, deps=()):
        srcs = [g.reshape(NDEV, -1, D) for g in grads_t]
        return exchange_start("pair_" + tag + "_start", pair_plan, NCHIP * len(srcs), srcs,
                              [((NCHIP,) + s.shape[1:], s.dtype) for s in srcs], deps)

    def chip_go(tag, names, pair_started, after):
        own, from_sibling = exchange_wait("pair_" + tag + "_wait", pair_plan, pair_started, after)
        partials = [pair_sum("pair_sum_" + n, g, r) for n, g, r in zip(names, own, from_sibling)]
        return exchange_start("chip_" + tag + "_start", chip_plan, (NCHIP - 1) * len(partials), partials,
                              [((NCHIP - 1,) + p.shape[1:], p.dtype) for p in partials])

    def chip_done(tag, names, chip_started, after):
        partials, from_chips = exchange_wait("chip_" + tag + "_wait", chip_plan, chip_started, after)
        return [chip_sum("chip_sum_" + n, p, r) for n, p, r in zip(names, partials, from_chips)]

    pair_ffn = pair_go("ffn", [g_ffn_out, g_ffn_in_t])

    dh2 = mm("mm_dh2", df, w_ffn_in_t, "nn", tiles=(1024, 1024, FFN_BLK), b_rot=pair_map,
             deps=[pair_ffn[4]])

    def f_dnorm2(dh, x1v, dx2v, o1v, g, sc, g1v):
        r, xh = _rms_stats(x1v)
        dxh = dh * (1.0 + sc) * g
        dx1 = dx2v + _rms_bwd(dxh, xh, r)
        return ([dx1, g1v * dx1],
                [_colsum(dh * xh * g), _colsum(dh), _colsum(dh * (1.0 + sc) * xh), _colsum(dx1 * o1v)])

    dx1, do1, d_sc2, d_sh2, d_n2g, d_g1 = rowwise(
        "dnorm2", f_dnorm2, [dh2, x1, dx2, o1], [norm2_g, sc2, g1], [(D, F32), (D, BF16)], [D, D, D, D], 256)

    g_out = mm("mm_g_out", merged, do1, "tn", BF16)
    chip_ffn = chip_go("ffn", ("w_ffn_out", "w_ffn_in"), pair_ffn, g_out)

    def ep_dmerge(accs, yc_v, z2p, gates):
        (dm,) = accs
        za, zb = z2p[:, 0:MRG_BLK].astype(F32), z2p[:, MRG_BLK:2 * MRG_BLK].astype(F32)
        sc_, ss_, sb_ = _sig(gates[:, 0:MRG_BLK]), _sig(gates[:, MRG_BLK:2 * MRG_BLK]), _sig(zb)
        dys = dm * ss_
        dz2 = jnp.concatenate([dys * sb_, dys * za * sb_ * (1.0 - sb_)], axis=1)
        dgates = jnp.concatenate([dm * yc_v * sc_ * (1.0 - sc_), dm * (za * sb_) * ss_ * (1.0 - ss_)], axis=1)
        return [dm * sc_, dz2, dgates]

    dyconv, dz2, dz = mm_ep("mm_dmerged", do1, w_out_f, 1, lambda j, q: j, ep_dmerge,
                            [(y_conv, 1, 0), (z2_pair, 2, 0), (z, 2, 0)],
                            [(D, BF16, 1), (2 * D, BF16, 2), (ZW, BF16, 2)], (512, MRG_BLK, 1024), deps=[chip_ffn[4]])

    g_conv_proj_t = mm("mm_g_conv_proj", dyconv, s_act, "tn", BF16)
    mrg_map = lambda t: pair_of(t, n_mrg)
    dgl = mm("mm_dgl", dz2, ssm_glu_t, "nn", tiles=(1024, SW, MRG_BLK), b_rot=mrg_map)
    g_ssm_glu_t = mm("mm_g_ssm_glu", dz2, gl, "tn", BF16, tiles=(MRG_BLK, SW, 1024), o_rot=mrg_map)
    pair_mid = pair_go("mid", [g_out, g_conv_proj_t, g_ssm_glu_t])
    ds = mm("mm_ds", dyconv, conv_proj_t, "nn", deps=[pair_mid[4]])
    dz, d_lng, d_lnb, d_cb, d_cw32 = conv_bwd(ds, yc, z, w32, conv_ln_g, conv_ln_b, dz)
    dz, d_d, d_ar, d_ai, d_wb_re, d_wb_im, d_wc = ssm_bwd(
        dgl, ypre, z, xs_re, xs_im, to_b(wbt_re), to_b(wbt_im), to_b(wct), e_re_b, e_im_b, dvec, dz)
    chip_mid = chip_go("mid", ("w_out", "conv_proj", "ssm_glu"), pair_mid, dz)

    d_bb_re = _diag_blocks(d_wb_re.transpose(0, 2, 1), NP, GH).reshape(NST, GH)
    d_bb_im = _diag_blocks(d_wb_im.transpose(0, 2, 1), NP, GH).reshape(NST, GH)
    d_wct = d_wc.transpose(0, 2, 1)
    d_c_re = _diag_blocks(d_wct[:, :, 0:SB], GH, NP)
    d_c_im = -_diag_blocks(d_wct[:, :, SB:2 * SB], GH, NP)
    d_a_re, d_a_im, d_ldt, d_b_re, d_b_im = disc_bwd(
        a_re_c, a_im_c, ldt_c, b_re_r, b_im_r, expand, d_ar.reshape(NST, 1), d_ai.reshape(NST, 1), d_bb_re, d_bb_im)

    small_local = [jnp.concatenate([d_g1, d_sh2, d_sc2, d_g2], axis=1).reshape(-1), d_cw32[0:KC].reshape(-1),
                   d_cb.reshape(-1), d_lng.reshape(-1), d_lnb.reshape(-1), d_a_re.reshape(-1), d_a_im.reshape(-1),
                   d_b_re.reshape(-1), d_b_im.reshape(-1), d_c_re.reshape(-1), d_c_im.reshape(-1), d_d.reshape(-1),
                   d_ldt.reshape(-1), d_n2g.reshape(-1), d_final_g.reshape(-1), loss_l[0, 0:1]]
    small_sizes = [v.shape[0] for v in small_local]
    small_pack = _pad_rows(jnp.concatenate(small_local), 256 * LANE).reshape(-1, LANE)
    small_go = exchange_start("gather_small_start", gather_plan, NDEV - 1, [small_pack],
                              [((NDEV,) + small_pack.shape, F32)], deps=[chip_mid[4]])

    g_in_t = mm("mm_g_in", dz, h1, "tn", BF16, tiles=(CW, 1024, 2048), o_rot=Z_ROT, deps=[small_go[4]])
    pair_in = pair_go("in", [g_in_t])

    dh1 = mm("mm_dh1", dz, w_in_t, "nn", tiles=(2048, 1024, CW), b_rot=Z_ROT, deps=[pair_in[4]])

    def f_dnorm1(dh, xv, dx1v, g, sc):
        r, xh = _rms_stats(xv)
        dxh = dh * (1.0 + sc) * g
        return ([dx1v + _rms_bwd(dxh, xh, r)],
                [_colsum(dh * xh * g), _colsum(dh), _colsum(dh * (1.0 + sc) * xh)])

    grad_x, d_sc1, d_sh1, d_n1g = rowwise(
        "dnorm1", f_dnorm1, [dh1, xs, dx1], [n1g, sc1], [(D, F32)], [D, D, D], 256)
    chip_in = chip_go("in", ("w_in",), pair_in, grad_x)

    weights = {
        "w_ada": (w_ada, m_w_ada, v_w_ada), "b_ada": (b_ada, m_b_ada, v_b_ada), "norm1_g": (norm1_g, m_norm1_g, v_norm1_g),
        "w_in": (w_in, m_w_in, v_w_in), "conv_w": (conv_w, m_conv_w, v_conv_w), "conv_b": (conv_b, m_conv_b, v_conv_b),
        "conv_ln_g": (conv_ln_g, m_conv_ln_g, v_conv_ln_g), "conv_ln_b": (conv_ln_b, m_conv_ln_b, v_conv_ln_b),
        "conv_proj": (conv_proj, m_conv_proj, v_conv_proj), "ssm_a_re": (ssm_a_re, m_ssm_a_re, v_ssm_a_re),
        "ssm_a_im": (ssm_a_im, m_ssm_a_im, v_ssm_a_im), "ssm_b_re": (ssm_b_re, m_ssm_b_re, v_ssm_b_re),
        "ssm_b_im": (ssm_b_im, m_ssm_b_im, v_ssm_b_im), "ssm_c_re": (ssm_c_re, m_ssm_c_re, v_ssm_c_re),
        "ssm_c_im": (ssm_c_im, m_ssm_c_im, v_ssm_c_im), "ssm_d": (ssm_d, m_ssm_d, v_ssm_d),
        "ssm_log_dt": (ssm_log_dt, m_ssm_log_dt, v_ssm_log_dt), "ssm_glu": (ssm_glu, m_ssm_glu, v_ssm_glu),
        "w_out": (w_out, m_w_out, v_w_out), "norm2_g": (norm2_g, m_norm2_g, v_norm2_g),
        "w_ffn_in": (w_ffn_in, m_w_ffn_in, v_w_ffn_in), "w_ffn_out": (w_ffn_out, m_w_ffn_out, v_w_ffn_out),
        "final_g": (final_g, m_final_g, v_final_g),
    }
    order = list(weights)
    big = ("w_ada", "w_in", "conv_proj", "ssm_glu", "w_out", "w_ffn_in", "w_ffn_out")
    grads, delta, new_m, new_v = {}, {}, {}, {}

    def adam_big(n, g2d, transposed=False):
        wv, mv, vv = weights[n]
        shp = wv.shape
        t_in = (lambda a: a.reshape(shp[-2:]).T) if transposed else (lambda a: a.reshape(shp[-2:]))
        t_out = (lambda a: a.T.reshape(shp)) if transposed else (lambda a: a.reshape(shp))
        d_, m_, v_ = adam("adam_" + n, t_in(wv), g2d, t_in(mv), t_in(vv))
        grads[n], delta[n], new_m[n], new_v[n] = t_out(g2d), t_out(d_), t_out(m_), t_out(v_)
        return d_

    gs_ffn_out, gs_ffn_in = chip_done("ffn", ("w_ffn_out", "w_ffn_in"), chip_ffn, chip_in[4])
    adam_big("w_ffn_out", gs_ffn_out)
    last = adam_big("w_ffn_in", gs_ffn_in, transposed=True)
    gs_out, gs_conv_proj, gs_ssm_glu = chip_done("mid", ("w_out", "conv_proj", "ssm_glu"), chip_mid, last)
    adam_big("w_out", gs_out)
    adam_big("conv_proj", gs_conv_proj.reshape(-1, CW), transposed=True)
    adam_big("ssm_glu", gs_ssm_glu.reshape(-1, SW), transposed=True)

    late_local = [d_sh1.reshape(-1), d_sc1.reshape(-1), d_n1g.reshape(-1)]
    late_pack = _pad_rows(jnp.concatenate(late_local), 16 * LANE).reshape(-1, LANE)
    (late_all,) = all_gather("gather_small_late", [late_pack])
    _, (small_all,) = exchange_wait("gather_small_wait", gather_plan, small_go, late_all, place_own=True)

    def unpack(vec, sizes):
        out, pos = [], 0
        for n in sizes:
            out.append(vec[pos:pos + n])
            pos += n
        return out

    g_sh1, g_sc1, g_n1g = unpack(sum_slots("sum_small_late", late_all).reshape(-1), [D, D, D])
    (g_mod_rest, g_cw_full, g_cb, g_lng, g_lnb, g_a_re, g_a_im, g_b_re, g_b_im, g_c_re, g_c_im, g_d, g_ldt,
     g_n2g, g_fg, loss_sum) = unpack(sum_slots("sum_small", small_all).reshape(-1), small_sizes)
    g_b_ada = jnp.concatenate([g_sh1, g_sc1, g_mod_rest])
    loss = loss_sum[0]
    dmod_all = jnp.concatenate([late_all.reshape(NDEV, -1)[:, 0:2 * D], small_all.reshape(NDEV, -1)[:, 0:4 * D]],
                               axis=1)
    g_w_ada = ada_grad(act_all, lax.dynamic_slice_in_dim(dmod_all, me * ncol, ncol, axis=1))
    ccol = conv_w.shape[2]
    g_conv_w = lax.dynamic_slice_in_dim(g_cw_full.reshape(KC, CW), me * ccol, ccol, axis=1)

    adam_big("w_ada", g_w_ada)
    grads.update({
        "b_ada": g_b_ada.reshape(b_ada.shape), "norm1_g": g_n1g.reshape(norm1_g.shape),
        "conv_w": g_conv_w[None], "conv_b": g_cb.reshape(conv_b.shape),
        "conv_ln_g": g_lng.reshape(conv_ln_g.shape), "conv_ln_b": g_lnb.reshape(conv_ln_b.shape),
        "ssm_a_re": g_a_re.reshape(ssm_a_re.shape),
        "ssm_a_im": g_a_im.reshape(ssm_a_im.shape), "ssm_b_re": g_b_re.reshape(ssm_b_re.shape),
        "ssm_b_im": g_b_im.reshape(ssm_b_im.shape), "ssm_c_re": g_c_re.reshape(ssm_c_re.shape),
        "ssm_c_im": g_c_im.reshape(ssm_c_im.shape), "ssm_d": g_d.reshape(ssm_d.shape),
        "ssm_log_dt": g_ldt.reshape(ssm_log_dt.shape),
        "norm2_g": g_n2g.reshape(norm2_g.shape),
        "final_g": g_fg.reshape(final_g.shape),
    })
    small = [n for n in order if n not in big]
    def rows(a):
        if a.ndim == 4 and a.shape[-1] < a.shape[-2]:
            a = a.swapaxes(-1, -2)
        return a.reshape(1, -1) if a.ndim == 1 else a.reshape(-1, a.shape[-1])

    def unrows(a, shp):
        if len(shp) == 4 and shp[-1] < shp[-2]:
            return a.reshape(shp[:-2] + (shp[-1], shp[-2])).swapaxes(-1, -2)
        return a.reshape(shp)

    small_out = adam_many("adam_small", [rows(weights[n][0]) for n in small], [rows(grads[n]) for n in small],
                          [rows(weights[n][1]) for n in small], [rows(weights[n][2]) for n in small])
    for q, n in enumerate(small):
        shp = weights[n][0].shape
        delta[n], new_m[n], new_v[n] = [unrows(small_out[t * len(small) + q], shp) for t in range(3)]

    (gs_in,) = chip_done("in", ("w_in",), chip_in, small_out[0])
    adam_big("w_in", gs_in, transposed=True)

    return (loss, grad_x[None], *[grads[n] for n in order], *[delta[n] for n in order],
            *[new_m[n] for n in order], *[new_v[n] for n in order])
```

```python
import functools
import math

import jax
import jax.numpy as jnp
from jax import lax
from jax.experimental import pallas as pl
from jax.experimental.pallas import tpu as pltpu

F32 = jnp.float32
BF16 = jnp.bfloat16

D = 1024
CW = 512
KC = 31
SW = 512
NG = 32
GH = 16
NP = 64
NST = NG * NP
FH = 2816
FFN_BLK = 1408
MRG_BLK = 1024
NMOD = 6
NDEV = 8
EPS = 1e-6
CB = 128
SB = 512
NBLK = SW // CB
HALO = 32
ZW = 2 * CW + SW + 2 * D
Z_ROT = lambda j: (j + 3) % (ZW // CW)
ZB_A, ZB_G, ZB_U = 4, 5, 6

ADAM_LR = 0.001
ADAM_B1 = 0.9
ADAM_B2 = 0.999
ADAM_EPS = 1e-08
ADAM_WD = 0.01
ADAM_STEP = 10

V7X_VMEM_BYTES = 64 * 1024 * 1024
VMEM_LIMIT = V7X_VMEM_BYTES - 8 * 1024 * 1024
LANE = 128
MESH = pl.DeviceIdType.MESH
ANY_SPEC = pl.BlockSpec(memory_space=pl.ANY)


def _params(sem=None, **kw):
    if sem is not None:
        kw["dimension_semantics"] = sem
    return pltpu.CompilerParams(vmem_limit_bytes=VMEM_LIMIT, **kw)


def _tile(n, most):
    best = None
    for t in range(LANE, most + 1, LANE):
        if n % t == 0:
            best = t
    if best is None:
        raise ValueError(f"no tile for {n}")
    return best


def _sig(x):
    return jax.nn.sigmoid(x)


def mm(name, a, b, mode, out_dtype=F32, tiles=None, b_rot=None, o_rot=None, deps=()):
    if mode == "nn":
        (m, k), (k2, n) = a.shape, b.shape
    elif mode == "nt":
        (m, k), (n, k2) = a.shape, b.shape
    else:
        (k, m), (k2, n) = a.shape, b.shape
    assert k == k2, (name, a.shape, b.shape)
    bm, bn, bk = tiles or (_tile(m, 1024), _tile(n, 1408), _tile(k, 1408 if k % 1408 == 0 else 1024))
    bm, bn, bk = min(bm, m), min(bn, n), min(bk, k)
    assert m % bm == 0 and n % bn == 0 and k % bk == 0, (name, m, n, k, bm, bn, bk)
    nk = k // bk
    rot = lambda idx, r: idx if r is None else r(idx)
    if mode == "nn":
        a_spec = pl.BlockSpec((bm, bk), lambda i, j, kk: (i, kk))
        b_spec = pl.BlockSpec((bk, bn), lambda i, j, kk: (rot(kk, b_rot), j))
        dims = (((1,), (0,)), ((), ()))
    elif mode == "nt":
        a_spec = pl.BlockSpec((bm, bk), lambda i, j, kk: (i, kk))
        b_spec = pl.BlockSpec((bn, bk), lambda i, j, kk: (rot(j, b_rot), kk))
        dims = (((1,), (1,)), ((), ()))
    else:
        assert b_rot is None
        a_spec = pl.BlockSpec((bk, bm), lambda i, j, kk: (kk, i))
        b_spec = pl.BlockSpec((bk, bn), lambda i, j, kk: (kk, j))
        dims = (((0,), (0,)), ((), ()))

    def body(a_ref, b_ref, *rest):
        o_ref, acc_ref = rest[-2:]
        kk = pl.program_id(2)

        @pl.when(kk == 0)
        def _():
            acc_ref[...] = jnp.zeros_like(acc_ref)

        acc_ref[...] += lax.dot_general(a_ref[...], b_ref[...], dims, preferred_element_type=F32)

        @pl.when(kk == nk - 1)
        def _():
            o_ref[...] = acc_ref[...].astype(o_ref.dtype)

    return pl.pallas_call(
        body, name=name,
        grid=(m // bm, n // bn, nk),
        in_specs=[a_spec, b_spec] + [ANY_SPEC] * len(deps),
        out_specs=pl.BlockSpec((bm, bn), lambda i, j, kk: (rot(i, o_rot), j)),
        out_shape=jax.ShapeDtypeStruct((m, n), out_dtype),
        scratch_shapes=[pltpu.VMEM((bm, bn), F32)],
        compiler_params=_params(("parallel", "parallel", "arbitrary")),
    )(a, b, *deps)


def mm_ep(name, a, b, n_acc, acc_block, epilogue, extras, outs, tiles, deps=(), b_kn=False, k_map=None,
          consts=(), sums=()):
    m, k = a.shape
    bm, bn, bk = tiles
    bm = min(bm, m)
    nj = outs[0][0] // (outs[0][2] * bn)
    nk = k // bk
    assert m % bm == 0 and k % bk == 0 and b.shape[0 if b_kn else 1] == k, (name, a.shape, b.shape, tiles)
    assert not sums or nj == 1, name
    ne, nc, no, ns, nd = len(extras), len(consts), len(outs), len(sums), len(deps)
    dims = (((1,), (0,)), ((), ())) if b_kn else (((1,), (1,)), ((), ()))
    kmap = (lambda kk: kk) if k_map is None else k_map

    def body(*refs):
        a_ref, b_refs = refs[0], refs[1:1 + n_acc]
        e_refs = refs[1 + n_acc:1 + n_acc + ne + nc]
        first_out = 1 + n_acc + ne + nc + nd
        o_refs = refs[first_out:first_out + no]
        s_refs = refs[first_out + no:first_out + no + ns]
        acc_refs = refs[first_out + no + ns:]
        av = a_ref[...]
        prods = [lax.dot_general(av, b_ref[...], dims, preferred_element_type=F32) for b_ref in b_refs]

        if ns:
            @pl.when((pl.program_id(0) == 0) & (pl.program_id(2) == 0))
            def _():
                for s_ref in s_refs:
                    s_ref[...] = jnp.zeros_like(s_ref)

        def finish(accs):
            res = epilogue(accs, *[e[...] for e in e_refs])
            tiles_out, sums_out = res if ns else (res, ())
            for o_ref, v in zip(o_refs, tiles_out):
                o_ref[...] = v.astype(o_ref.dtype)
            for s_ref, v in zip(s_refs, sums_out):
                s_ref[...] += v

        if nk == 1:
            finish(prods)
        else:
            kk = pl.program_id(2)

            @pl.when(kk == 0)
            def _():
                for acc_ref in acc_refs:
                    acc_ref[...] = jnp.zeros_like(acc_ref)

            for acc_ref, p in zip(acc_refs, prods):
                acc_ref[...] += p

            @pl.when(kk == nk - 1)
            def _():
                finish([acc_ref[...] for acc_ref in acc_refs])

    in_specs = [pl.BlockSpec((bm, bk), lambda i, j, kk: (i, kk))]
    if b_kn:
        in_specs += [pl.BlockSpec((bk, bn), functools.partial(lambda i, j, kk, q: (kmap(kk), acc_block(j, q)), q=q))
                     for q in range(n_acc)]
    else:
        in_specs += [pl.BlockSpec((bn, bk), functools.partial(lambda i, j, kk, q: (acc_block(j, q), kmap(kk)), q=q))
                     for q in range(n_acc)]
    in_specs += [pl.BlockSpec((bm, w * bn), functools.partial(lambda i, j, kk, off: (i, j + off), off=off))
                 for (_, w, off) in extras]
    in_specs += [pl.BlockSpec((1, bn), lambda i, j, kk: (0, j)) for _ in consts]
    in_specs += [ANY_SPEC] * nd
    out_specs = [pl.BlockSpec((bm, w * bn), lambda i, j, kk: (i, j)) for (_, _, w) in outs]
    out_specs += [pl.BlockSpec((1, w), lambda i, j, kk: (0, 0)) for w in sums]
    out_shape = [jax.ShapeDtypeStruct((m, cols), dt) for (cols, dt, _) in outs]
    out_shape += [jax.ShapeDtypeStruct((1, w), F32) for w in sums]
    return pl.pallas_call(
        body, name=name, grid=(m // bm, nj, nk),
        in_specs=in_specs, out_specs=out_specs, out_shape=out_shape,
        scratch_shapes=[pltpu.VMEM((bm, bn), F32)] * (n_acc if nk > 1 else 0),
        compiler_params=_params(("arbitrary",) * 3 if sums else ("parallel", "parallel", "arbitrary")),
    )(a, *[b] * n_acc, *[e[0] for e in extras], *consts, *deps)


def mm_ep_pipe(name, a, b, n_acc, acc_block, epilogue, extras, outs, tiles, deps=(), b_kn=False, consts=(), sums=()):
    m, k = a.shape
    bm, bn, bk = tiles
    bm = min(bm, m)
    assert bk == k and m % bm == 0 and b.shape[0 if b_kn else 1] == k, (name, a.shape, b.shape, tiles)
    ni, nj = m // bm, outs[0][0] // (outs[0][2] * bn)
    nt = ni * nj
    assert not sums or nj == 1, name
    ne, nc, no, ns, nd = len(extras), len(consts), len(outs), len(sums), len(deps)
    dims = (((1,), (0,)), ((), ())) if b_kn else (((1,), (1,)), ((), ()))
    cur_i = lambda t: jnp.minimum(t, nt - 1) // nj
    cur_j = lambda t: jnp.minimum(t, nt - 1) % nj
    prev_i = lambda t: jnp.maximum(t - 1, 0) // nj
    prev_j = lambda t: jnp.maximum(t - 1, 0) % nj

    def body(*refs):
        a_ref, b_refs = refs[0], refs[1:1 + n_acc]
        e_refs = refs[1 + n_acc:1 + n_acc + ne + nc]
        first_out = 1 + n_acc + ne + nc + nd
        o_refs = refs[first_out:first_out + no]
        s_refs = refs[first_out + no:first_out + no + ns]
        acc_ref = refs[first_out + no + ns]
        t = pl.program_id(0)

        @pl.when(t == 0)
        def _():
            acc_ref[...] = jnp.zeros_like(acc_ref)
            for s_ref in s_refs:
                s_ref[...] = jnp.zeros_like(s_ref)

        slot = t % 2
        done = [acc_ref[(1 - slot) * n_acc + q] for q in range(n_acc)]
        av = a_ref[...]
        for q, b_ref in enumerate(b_refs):
            acc_ref[slot * n_acc + q] = lax.dot_general(av, b_ref[...], dims, preferred_element_type=F32)
        res = epilogue(done, *[e[...] for e in e_refs])
        tiles_out, sums_out = res if ns else (res, ())
        for o_ref, v in zip(o_refs, tiles_out):
            o_ref[...] = v.astype(o_ref.dtype)
        live = (t >= 1).astype(F32)
        for s_ref, v in zip(s_refs, sums_out):
            s_ref[...] += v * live

    in_specs = [pl.BlockSpec((bm, k), lambda t: (cur_i(t), 0))]
    if b_kn:
        in_specs += [pl.BlockSpec((k, bn), functools.partial(lambda t, q: (0, acc_block(cur_j(t), q)), q=q))
                     for q in range(n_acc)]
    else:
        in_specs += [pl.BlockSpec((bn, k), functools.partial(lambda t, q: (acc_block(cur_j(t), q), 0), q=q))
                     for q in range(n_acc)]
    in_specs += [pl.BlockSpec((bm, w * bn), functools.partial(lambda t, off: (prev_i(t), prev_j(t) + off), off=off))
                 for (_, w, off) in extras]
    in_specs += [pl.BlockSpec((1, bn), lambda t: (0, prev_j(t))) for _ in consts]
    in_specs += [ANY_SPEC] * nd
    out_specs = [pl.BlockSpec((bm, w * bn), lambda t: (prev_i(t), prev_j(t))) for (_, _, w) in outs]
    out_specs += [pl.BlockSpec((1, w), lambda t: (0, 0)) for w in sums]
    out_shape = [jax.ShapeDtypeStruct((m, cols), dt) for (cols, dt, _) in outs]
    out_shape += [jax.ShapeDtypeStruct((1, w), F32) for w in sums]
    return pl.pallas_call(
        body, name=name, grid=(nt + 1,),
        in_specs=in_specs, out_specs=out_specs, out_shape=out_shape,
        scratch_shapes=[pltpu.VMEM((2 * n_acc, bm, bn), F32)],
        compiler_params=_params(("arbitrary",)),
    )(a, *[b] * n_acc, *[e[0] for e in extras], *consts, *deps)


def rowwise(name, fn, rows, consts, out_rows, out_sums, ts, alias=None, deps=()):
    rows = [r if isinstance(r, tuple) else (r, r.shape[1], 0) for r in rows]
    out_rows = [o if len(o) == 4 else (o[0], o[1], o[0], 0) for o in out_rows]
    s = rows[0][0].shape[0]
    nt = s // ts
    nr, nc, no, ns = len(rows), len(consts), len(out_rows), len(out_sums)
    in_specs = [pl.BlockSpec((ts, w), functools.partial(lambda i, cb: (i, cb), cb=cb)) for (_, w, cb) in rows]
    in_specs += [pl.BlockSpec(c.shape, lambda i: (0, 0)) for c in consts]
    operands = [r[0] for r in rows] + list(consts)
    aliases = {}
    if alias is not None:
        in_specs.append(pl.BlockSpec(memory_space=pl.ANY))
        operands.append(alias[0])
        aliases = {nr + nc: alias[1]}
    in_specs += [ANY_SPEC] * len(deps)
    operands += list(deps)
    out_shape = [jax.ShapeDtypeStruct((s, tw), dt) for (_, dt, tw, _) in out_rows]
    out_shape += [jax.ShapeDtypeStruct((1, w), F32) for w in out_sums]
    out_specs = [pl.BlockSpec((ts, w), functools.partial(lambda i, cb: (i, cb), cb=cb)) for (w, _, _, cb) in out_rows]
    out_specs += [pl.BlockSpec((1, w), lambda i: (0, 0)) for w in out_sums]
    n_in = len(operands)

    def body(*refs):
        ins, outs = refs[:nr + nc], refs[n_in:]
        i = pl.program_id(0)
        ro, so = fn(*[r[...] for r in ins])
        for q in range(no):
            outs[q][...] = ro[q].astype(outs[q].dtype)
        if ns:
            @pl.when(i == 0)
            def _():
                for q in range(ns):
                    outs[no + q][...] = jnp.zeros_like(outs[no + q])

            for q in range(ns):
                outs[no + q][...] += so[q]

    return pl.pallas_call(
        body, name=name, grid=(nt,),
        in_specs=in_specs, out_specs=out_specs, out_shape=out_shape, input_output_aliases=aliases,
        compiler_params=_params(("arbitrary",) if ns else ("parallel",)),
    )(*operands)


def _colsum(v):
    return jnp.sum(v, axis=0, keepdims=True)


def _rms_stats(xv):
    r = lax.rsqrt(jnp.mean(xv * xv, axis=-1, keepdims=True) + EPS)
    return r, xv * r


def _rms_bwd(dxhat, xhat, r):
    return r * (dxhat - xhat * jnp.mean(dxhat * xhat, axis=-1, keepdims=True))


def _gelu(v):
    k = math.sqrt(2.0 / math.pi)
    t = jnp.tanh(k * (v + 0.044715 * v * v * v))
    return 0.5 * v * (1.0 + t), t


def _gelu_grad(v, t):
    k = math.sqrt(2.0 / math.pi)
    return 0.5 * (1.0 + t) + 0.5 * v * (1.0 - t * t) * k * (1.0 + 3.0 * 0.044715 * v * v)


CONV_TS = 256
CONV_CH = 64


def _ln_fwd(yc, g, b):
    mu = jnp.mean(yc, axis=-1, keepdims=True)
    xc = yc - mu
    rstd = lax.rsqrt(jnp.mean(xc * xc, axis=-1, keepdims=True) + EPS)
    nhat = xc * rstd
    return nhat, rstd, nhat * g + b


SUBL = 8


def _shifted_copies(buf, sh, ts):
    for b in range(1, SUBL):
        sh[b - 1] = buf[pl.ds(b, ts + HALO - SUBL), :]


def _shifted(buf, sh, start):
    b = start % SUBL
    if b == 0:
        return buf[pl.ds(start, CONV_CH), :]
    return sh[b - 1, pl.ds(start - b, CONV_CH), :]


def conv_fwd(z, w32, cb, lg, lb):
    s = z.shape[0]
    ts = CONV_TS
    nt = s // ts
    hb = ts // HALO

    def body(a_ref, g_ref, ah_ref, gh_ref, w_ref, cb_ref, lg_ref, lb_ref, yc_ref, s_ref, ubuf, ush):
        i = pl.program_id(0)
        first = (i > 0).astype(F32)
        ubuf[0:HALO, :] = ah_ref[...] * _sig(gh_ref[...]) * first
        ubuf[HALO:HALO + ts, :] = a_ref[...] * _sig(g_ref[...])
        _shifted_copies(ubuf, ush, ts)
        for c0 in range(0, ts, CONV_CH):
            acc = jnp.zeros((CONV_CH, CW), F32)
            for k in range(KC):
                acc = acc + w_ref[k:k + 1, :] * _shifted(ubuf, ush, c0 + k + 2)
            yc = acc + cb_ref[...]
            yc_ref[c0:c0 + CONV_CH, :] = yc
            _, _, ln = _ln_fwd(yc, lg_ref[...], lb_ref[...])
            s_ref[c0:c0 + CONV_CH, :] = (ln * _sig(ln)).astype(s_ref.dtype)

    cur = lambda cbk: pl.BlockSpec((ts, CW), functools.partial(lambda i, q: (i, q), q=cbk))
    prev = lambda cbk: pl.BlockSpec((HALO, CW), functools.partial(lambda i, q: (jnp.maximum(i * hb - 1, 0), q), q=cbk))
    const = lambda a: pl.BlockSpec(a.shape, lambda i: (0, 0))
    return pl.pallas_call(
        body, name="conv_fwd", grid=(nt,),
        in_specs=[cur(ZB_A), cur(ZB_G), prev(ZB_A), prev(ZB_G), const(w32), const(cb), const(lg), const(lb)],
        out_specs=[pl.BlockSpec((ts, CW), lambda i: (i, 0)), pl.BlockSpec((ts, CW), lambda i: (i, 0))],
        out_shape=[jax.ShapeDtypeStruct((s, CW), F32), jax.ShapeDtypeStruct((s, CW), BF16)],
        scratch_shapes=[pltpu.VMEM((HALO + ts, CW), F32), pltpu.VMEM((SUBL - 1, ts + HALO - SUBL, CW), F32)],
        compiler_params=_params(("parallel",)),
    )(z, z, z, z, w32, cb, lg, lb)


def conv_bwd(ds, yc, z, w32, lg, lb, dz):
    s = z.shape[0]
    ts = CONV_TS
    nt = s // ts
    hb = ts // HALO
    last_hb = s // HALO - 1

    def ln_bwd(dsv, ycv, g, b):
        nhat, rstd, ln = _ln_fwd(ycv, g, b)
        sg = _sig(ln)
        dln = dsv * (sg * (1.0 + ln * (1.0 - sg)))
        dnh = dln * g
        dyc = rstd * (dnh - jnp.mean(dnh, axis=-1, keepdims=True)
                      - nhat * jnp.mean(dnh * nhat, axis=-1, keepdims=True))
        return dyc, dln, nhat

    def body(ds_ref, yc_ref, dsn_ref, ycn_ref, a_ref, g_ref, ah_ref, gh_ref, w_ref, lg_ref, lb_ref, dz_in,
             dz_ref, dlg_ref, dlb_ref, dcb_ref, dw_ref, dbuf, ubuf, dsh, ush):
        i = pl.program_id(0)

        @pl.when(i == 0)
        def _():
            dlg_ref[...] = jnp.zeros_like(dlg_ref)
            dlb_ref[...] = jnp.zeros_like(dlb_ref)
            dcb_ref[...] = jnp.zeros_like(dcb_ref)
            dw_ref[...] = jnp.zeros_like(dw_ref)

        lg, lb = lg_ref[...], lb_ref[...]
        dyc, dln, nhat = ln_bwd(ds_ref[...], yc_ref[...], lg, lb)
        dlg_ref[...] += _colsum(dln * nhat)
        dlb_ref[...] += _colsum(dln)
        dcb_ref[...] += _colsum(dyc)
        dbuf[0:ts, :] = dyc
        nxt = (i < nt - 1).astype(F32)
        dbuf[ts:ts + HALO, :] = ln_bwd(dsn_ref[...], ycn_ref[...], lg, lb)[0] * nxt
        first = (i > 0).astype(F32)
        ubuf[0:HALO, :] = ah_ref[...] * _sig(gh_ref[...]) * first
        ubuf[HALO:HALO + ts, :] = a_ref[...] * _sig(g_ref[...])
        _shifted_copies(dbuf, dsh, ts)
        _shifted_copies(ubuf, ush, ts)
        for c0 in range(0, ts, CONV_CH):
            du = jnp.zeros((CONV_CH, CW), F32)
            dyc_c = dbuf[c0:c0 + CONV_CH, :]
            for k in range(KC):
                du = du + w_ref[k:k + 1, :] * _shifted(dbuf, dsh, c0 + KC - 1 - k)
                dw_ref[k:k + 1, :] += _colsum(dyc_c * _shifted(ubuf, ush, c0 + k + 2))
            av = a_ref[c0:c0 + CONV_CH, :]
            sg = _sig(g_ref[c0:c0 + CONV_CH, :])
            dz_ref[c0:c0 + CONV_CH, 0:CW] = (du * sg).astype(dz_ref.dtype)
            dz_ref[c0:c0 + CONV_CH, CW:2 * CW] = (du * av * sg * (1.0 - sg)).astype(dz_ref.dtype)

    cur = lambda w, cbk: pl.BlockSpec((ts, w), functools.partial(lambda i, q: (i, q), q=cbk))
    prev = lambda cbk: pl.BlockSpec((HALO, CW), functools.partial(lambda i, q: (jnp.maximum(i * hb - 1, 0), q), q=cbk))
    nxt_spec = pl.BlockSpec((HALO, CW), lambda i: (jnp.minimum((i + 1) * hb, last_hb), 0))
    const = lambda a: pl.BlockSpec(a.shape, lambda i: (0, 0))
    acc = lambda r: pl.BlockSpec((r, CW), lambda i: (0, 0))
    return pl.pallas_call(
        body, name="conv_bwd", grid=(nt,),
        in_specs=[cur(CW, 0), cur(CW, 0), nxt_spec, nxt_spec, cur(CW, ZB_A), cur(CW, ZB_G), prev(ZB_A), prev(ZB_G),
                  const(w32), const(lg), const(lb), pl.BlockSpec(memory_space=pl.ANY)],
        out_specs=[pl.BlockSpec((ts, 2 * CW), lambda i: (i, ZB_A // 2)), acc(1), acc(1), acc(1), acc(HALO)],
        out_shape=[jax.ShapeDtypeStruct(dz.shape, dz.dtype), jax.ShapeDtypeStruct((1, CW), F32),
                   jax.ShapeDtypeStruct((1, CW), F32), jax.ShapeDtypeStruct((1, CW), F32),
                   jax.ShapeDtypeStruct((HALO, CW), F32)],
        scratch_shapes=[pltpu.VMEM((ts + HALO, CW), F32), pltpu.VMEM((HALO + ts, CW), F32)]
        + [pltpu.VMEM((SUBL - 1, ts + HALO - SUBL, CW), F32)] * 2,
        input_output_aliases={11: 0},
        compiler_params=_params(("arbitrary",)),
    )(ds, yc, ds, yc, z, z, z, z, w32, lg, lb, dz)


SSM_TS = 512
GRP = 8


def _cmul(ar, ai, br, bi):
    return ar * br - ai * bi, ar * bi + ai * br


def _scan_tables(ar, ai, reverse):
    n = ar.shape[1]
    row = lax.broadcasted_iota(jnp.int32, (GRP, n), 0)
    dist = (GRP - 1 - row) if reverse else row
    one_r = jnp.broadcast_to(ar, (GRP, n))
    one_i = jnp.broadcast_to(ai, (GRP, n))
    p2r, p2i = _cmul(one_r, one_i, one_r, one_i)
    p4r, p4i = _cmul(p2r, p2i, p2r, p2i)
    steps = []
    for sft, (pr, pi) in ((1, (one_r, one_i)), (2, (p2r, p2i)), (4, (p4r, p4i))):
        keep = dist >= sft
        steps.append((jnp.where(keep, pr, 0.0), jnp.where(keep, pi, 0.0)))
    cr, ci = one_r, one_i
    accr, acci = one_r, one_i
    for e in range(1, GRP):
        cr, ci = _cmul(cr, ci, one_r, one_i)
        accr = jnp.where(dist == e, cr, accr)
        acci = jnp.where(dist == e, ci, acci)
    return steps, (accr, acci)


def _scan_group(xr, xi, steps, carry_tab, cr, ci, reverse):
    for sft, (tr, ti) in zip((1, 2, 4), steps):
        amt = (GRP - sft) if reverse else sft
        sr = pltpu.roll(xr, amt, 0)
        si = pltpu.roll(xi, amt, 0)
        xr, xi = xr + tr * sr - ti * si, xi + tr * si + ti * sr
    pr, pi = carry_tab
    xr = xr + pr * cr - pi * ci
    xi = xi + pr * ci + pi * cr
    return xr, xi


def ssm_fwd(z, wb_re, wb_im, wc, e_re, e_im, dvec):
    s = z.shape[0]
    ts = SSM_TS
    nt = s // ts
    ucol0 = ZB_U * CW // CB

    def body(u_ref, wbr_ref, wbi_ref, wc_ref, er_ref, ei_ref, d_ref, xr_ref, xi_ref, y_ref, gl_ref, car_r, car_i):
        i = pl.program_id(1)

        @pl.when(i == 0)
        def _():
            car_r[...] = jnp.zeros_like(car_r)
            car_i[...] = jnp.zeros_like(car_i)

        u = u_ref[...]
        ub = u.astype(BF16)
        xr_ref[...] = jnp.dot(ub, wbr_ref[0], preferred_element_type=F32)
        xi_ref[...] = jnp.dot(ub, wbi_ref[0], preferred_element_type=F32)
        steps, ctab = _scan_tables(er_ref[0], ei_ref[0], False)

        def grp(r, carry):
            cr, ci = carry
            r0 = pl.multiple_of(r * GRP, GRP)
            xr, xi = _scan_group(xr_ref[pl.ds(r0, GRP), :], xi_ref[pl.ds(r0, GRP), :], steps, ctab, cr, ci, False)
            xr_ref[pl.ds(r0, GRP), :] = xr
            xi_ref[pl.ds(r0, GRP), :] = xi
            return (jnp.broadcast_to(xr[GRP - 1:GRP, :], (GRP, SB)), jnp.broadcast_to(xi[GRP - 1:GRP, :], (GRP, SB)))

        cr, ci = lax.fori_loop(0, ts // GRP, grp, (car_r[...], car_i[...]))
        car_r[...] = cr
        car_i[...] = ci
        y = (jnp.dot(xr_ref[...].astype(BF16), wc_ref[0, 0:SB, :], preferred_element_type=F32)
             + jnp.dot(xi_ref[...].astype(BF16), wc_ref[0, SB:2 * SB, :], preferred_element_type=F32)
             + d_ref[0] * u)
        y_ref[...] = y
        gl_ref[...] = _gelu(y)[0].astype(gl_ref.dtype)

    blk3 = lambda a: pl.BlockSpec((1,) + a.shape[1:], lambda j, i: (j, 0, 0))
    return pl.pallas_call(
        body, name="ssm_fwd", grid=(NBLK, nt),
        in_specs=[pl.BlockSpec((ts, CB), lambda j, i: (i, ucol0 + j)),
                  blk3(wb_re), blk3(wb_im), blk3(wc), blk3(e_re), blk3(e_im), blk3(dvec)],
        out_specs=[pl.BlockSpec((ts, SB), lambda j, i: (i, j)), pl.BlockSpec((ts, SB), lambda j, i: (i, j)),
                   pl.BlockSpec((ts, CB), lambda j, i: (i, j)), pl.BlockSpec((ts, CB), lambda j, i: (i, j))],
        out_shape=[jax.ShapeDtypeStruct((s, NST), F32), jax.ShapeDtypeStruct((s, NST), F32),
                   jax.ShapeDtypeStruct((s, SW), F32), jax.ShapeDtypeStruct((s, SW), BF16)],
        scratch_shapes=[pltpu.VMEM((GRP, SB), F32), pltpu.VMEM((GRP, SB), F32)],
        compiler_params=_params(("parallel", "arbitrary")),
    )(z, wb_re, wb_im, wc, e_re, e_im, dvec)


def ssm_bwd(dgl, ypre, z, xs_re, xs_im, wbt_re, wbt_im, wct, e_re, e_im, dvec, dz):
    s = z.shape[0]
    ts = SSM_TS
    nt = s // ts
    ucol0 = ZB_U * CW // CB
    tn_dims = (((0,), (0,)), ((), ()))

    def body(dgl_ref, y_ref, u_ref, xr_ref, xi_ref, wbtr_ref, wbti_ref, wct_ref, er_ref, ei_ref, d_ref, dz_in,
             du_ref, dd_ref, dar_ref, dai_ref, dwbr_ref, dwbi_ref, dwc_ref,
             lr_ref, li_ref, car_r, car_i, acc_r, acc_i):
        i = pl.program_id(1)

        @pl.when(i == 0)
        def _():
            for ref in (car_r, car_i, acc_r, acc_i, dd_ref, dwbr_ref, dwbi_ref, dwc_ref):
                ref[...] = jnp.zeros_like(ref)

        u = u_ref[...]
        y = y_ref[...]
        dy = dgl_ref[...] * _gelu_grad(y, _gelu(y)[1])
        dd_ref[0] += _colsum(dy * u)
        dyb = dy.astype(BF16)
        dxo = jnp.dot(dyb, wct_ref[0], preferred_element_type=F32)
        lr_ref[...] = dxo[:, 0:SB]
        li_ref[...] = dxo[:, SB:2 * SB]
        steps, ctab = _scan_tables(er_ref[0], -ei_ref[0], True)
        row = lax.broadcasted_iota(jnp.int32, (GRP, SB), 0)

        def grp(q, carry):
            cr, ci, ar, ai = carry
            r0 = pl.multiple_of((ts // GRP - 1 - q) * GRP, GRP)
            lr, li = _scan_group(lr_ref[pl.ds(r0, GRP), :], li_ref[pl.ds(r0, GRP), :], steps, ctab, cr, ci, True)
            lr_ref[pl.ds(r0, GRP), :] = lr
            li_ref[pl.ds(r0, GRP), :] = li
            nr = jnp.where(row == GRP - 1, cr, pltpu.roll(lr, GRP - 1, 0))
            ni = jnp.where(row == GRP - 1, ci, pltpu.roll(li, GRP - 1, 0))
            xr = xr_ref[pl.ds(r0, GRP), :]
            xi = xi_ref[pl.ds(r0, GRP), :]
            ar = ar + nr * xr + ni * xi
            ai = ai + ni * xr - nr * xi
            return (jnp.broadcast_to(lr[0:1, :], (GRP, SB)), jnp.broadcast_to(li[0:1, :], (GRP, SB)), ar, ai)

        cr, ci, ar, ai = lax.fori_loop(0, ts // GRP, grp, (car_r[...], car_i[...], acc_r[...], acc_i[...]))
        car_r[...] = cr
        car_i[...] = ci
        acc_r[...] = ar
        acc_i[...] = ai

        @pl.when(i == nt - 1)
        def _():
            dar_ref[0] = _colsum(ar)
            dai_ref[0] = _colsum(ai)

        lrb = lr_ref[...].astype(BF16)
        lib = li_ref[...].astype(BF16)
        du = (jnp.dot(lrb, wbtr_ref[0], preferred_element_type=F32)
              + jnp.dot(lib, wbti_ref[0], preferred_element_type=F32) + d_ref[0] * dy)
        du_ref[...] = du.astype(du_ref.dtype)
        ub = u.astype(BF16)
        dwbr_ref[0] += lax.dot_general(ub, lrb, tn_dims, preferred_element_type=F32)
        dwbi_ref[0] += lax.dot_general(ub, lib, tn_dims, preferred_element_type=F32)
        dwc_ref[0, 0:SB, :] += lax.dot_general(xr_ref[...].astype(BF16), dyb, tn_dims, preferred_element_type=F32)
        dwc_ref[0, SB:2 * SB, :] += lax.dot_general(xi_ref[...].astype(BF16), dyb, tn_dims, preferred_element_type=F32)

    rev = lambda i: nt - 1 - i
    blk3 = lambda a: pl.BlockSpec((1,) + a.shape[1:], lambda j, i: (j, 0, 0))
    acc3 = lambda r, c: pl.BlockSpec((1, r, c), lambda j, i: (j, 0, 0))
    return pl.pallas_call(
        body, name="ssm_bwd", grid=(NBLK, nt),
        in_specs=[pl.BlockSpec((ts, CB), lambda j, i: (rev(i), j)), pl.BlockSpec((ts, CB), lambda j, i: (rev(i), j)),
                  pl.BlockSpec((ts, CB), lambda j, i: (rev(i), ucol0 + j)),
                  pl.BlockSpec((ts, SB), lambda j, i: (rev(i), j)), pl.BlockSpec((ts, SB), lambda j, i: (rev(i), j)),
                  blk3(wbt_re), blk3(wbt_im), blk3(wct), blk3(e_re), blk3(e_im), blk3(dvec),
                  pl.BlockSpec(memory_space=pl.ANY)],
        out_specs=[pl.BlockSpec((ts, CB), lambda j, i: (rev(i), ucol0 + j)),
                   acc3(1, CB), acc3(1, SB), acc3(1, SB), acc3(CB, SB), acc3(CB, SB), acc3(2 * SB, CB)],
        out_shape=[jax.ShapeDtypeStruct(dz.shape, dz.dtype),
                   jax.ShapeDtypeStruct((NBLK, 1, CB), F32),
                   jax.ShapeDtypeStruct((NBLK, 1, SB), F32), jax.ShapeDtypeStruct((NBLK, 1, SB), F32),
                   jax.ShapeDtypeStruct((NBLK, CB, SB), F32), jax.ShapeDtypeStruct((NBLK, CB, SB), F32),
                   jax.ShapeDtypeStruct((NBLK, 2 * SB, CB), F32)],
        scratch_shapes=[pltpu.VMEM((ts, SB), F32), pltpu.VMEM((ts, SB), F32)] + [pltpu.VMEM((GRP, SB), F32)] * 4,
        input_output_aliases={11: 0},
        compiler_params=_params(("parallel", "arbitrary")),
    )(dgl, ypre, z, xs_re, xs_im, wbt_re, wbt_im, wct, e_re, e_im, dvec, dz)


def _disc(a_re, a_im, log_dt, b_re, b_im, expand):
    dt = jnp.dot(expand, jnp.exp(log_dt), preferred_element_type=F32, precision=lax.Precision.HIGHEST)
    mag = jnp.exp(dt * a_re)
    e_re, e_im = mag * jnp.cos(dt * a_im), mag * jnp.sin(dt * a_im)
    n_re, n_im = e_re - 1.0, e_im
    den = a_re * a_re + a_im * a_im
    q_re = (n_re * a_re + n_im * a_im) / den
    q_im = (n_im * a_re - n_re * a_im) / den
    return e_re, e_im, q_re * b_re - q_im * b_im, q_re * b_im + q_im * b_re


def _whole(a):
    return pl.BlockSpec(a.shape, functools.partial(lambda n: (0,) * n, n=a.ndim))


def disc_fwd(a_re, a_im, log_dt, b_re, b_im, expand):
    def body(ar, ai, ld, br, bi, ex, er_o, ei_o, bbr_o, bbi_o):
        er, ei, bbr, bbi = _disc(ar[...], ai[...], ld[...], br[...], bi[...], ex[...])
        er_o[...] = er
        ei_o[...] = ei
        bbr_o[...] = bbr
        bbi_o[...] = bbi

    ins = (a_re, a_im, log_dt, b_re, b_im, expand)
    outs = [jax.ShapeDtypeStruct(a_re.shape, F32)] * 2 + [jax.ShapeDtypeStruct(b_re.shape, F32)] * 2
    return pl.pallas_call(body, name="disc_fwd", in_specs=[_whole(a) for a in ins],
                          out_specs=[_whole(o) for o in outs], out_shape=outs, compiler_params=_params())(*ins)


def disc_bwd(a_re, a_im, log_dt, b_re, b_im, expand, de_re, de_im, dbb_re, dbb_im):
    def body(ar, ai, ld, br, bi, ex, der, dei, dbr, dbi, o_ar, o_ai, o_ld, o_br, o_bi):
        exv = ex[...]
        _, vjp = jax.vjp(lambda *p: _disc(*p, exv), ar[...], ai[...], ld[...], br[...], bi[...])
        g = vjp((der[...], dei[...], dbr[...], dbi[...]))
        for o, v in zip((o_ar, o_ai, o_ld, o_br, o_bi), g):
            o[...] = v

    ins = (a_re, a_im, log_dt, b_re, b_im, expand, de_re, de_im, dbb_re, dbb_im)
    outs = [jax.ShapeDtypeStruct(a.shape, F32) for a in (a_re, a_im, log_dt, b_re, b_im)]
    return pl.pallas_call(body, name="disc_bwd", in_specs=[_whole(a) for a in ins],
                          out_specs=[_whole(o) for o in outs], out_shape=outs, compiler_params=_params())(*ins)


def mod_fwd(c_all, w_ada, b_cols):
    def body(c_ref, w_ref, b_ref, act_ref, mod_ref):
        cv = c_ref[...]
        act = cv * _sig(cv)
        act_ref[...] = act
        mod_ref[...] = jnp.dot(act, w_ref[...], preferred_element_type=F32, precision=lax.Precision.HIGHEST) + b_ref[...]

    ins = (c_all, w_ada, b_cols)
    outs = [jax.ShapeDtypeStruct(c_all.shape, F32), jax.ShapeDtypeStruct((NDEV, w_ada.shape[1]), F32)]
    return pl.pallas_call(body, name="mod_fwd", in_specs=[_whole(a) for a in ins],
                          out_specs=[_whole(o) for o in outs], out_shape=outs, compiler_params=_params())(*ins)


def ada_grad(act_all, dmod_cols):
    def body(a_ref, d_ref, o_ref):
        o_ref[...] = lax.dot_general(a_ref[...], d_ref[...], (((0,), (0,)), ((), ())),
                                     preferred_element_type=F32, precision=lax.Precision.HIGHEST)

    out = jax.ShapeDtypeStruct((act_all.shape[1], dmod_cols.shape[1]), F32)
    return pl.pallas_call(body, name="ada_grad", in_specs=[_whole(act_all), _whole(dmod_cols)],
                          out_specs=_whole(out), out_shape=out, compiler_params=_params())(act_all, dmod_cols)


def _adam_math(w, g, m, v):
    m2 = ADAM_B1 * m + (1.0 - ADAM_B1) * g
    v2 = ADAM_B2 * v + (1.0 - ADAM_B2) * (g * g)
    m_hat = m2 / (1.0 - ADAM_B1 ** ADAM_STEP)
    v_hat = v2 / (1.0 - ADAM_B2 ** ADAM_STEP)
    delta = -ADAM_LR * (m_hat / (jnp.sqrt(v_hat) + ADAM_EPS) + ADAM_WD * w)
    return delta, m2, v2


def adam(name, w, g, m, v):
    r, c = w.shape
    tr = max(t for t in range(8, min(r, 512) + 1, 8) if r % t == 0)

    def body(w_ref, g_ref, m_ref, v_ref, d_o, m_o, v_o):
        d, m2, v2 = _adam_math(w_ref[...], g_ref[...], m_ref[...], v_ref[...])
        d_o[...] = d
        m_o[...] = m2
        v_o[...] = v2

    spec = pl.BlockSpec((tr, c), lambda i: (i, 0))
    out = jax.ShapeDtypeStruct((r, c), F32)
    return pl.pallas_call(body, name=name, grid=(r // tr,), in_specs=[spec] * 4, out_specs=[spec] * 3,
                          out_shape=[out] * 3, compiler_params=_params(("parallel",)))(w, g, m, v)


def adam_many(name, ws, gs, ms, vs):
    n = len(ws)

    def body(*refs):
        ins, outs = refs[:4 * n], refs[4 * n:]
        for q in range(n):
            d, m2, v2 = _adam_math(ins[q][...], ins[n + q][...], ins[2 * n + q][...], ins[3 * n + q][...])
            outs[q][...] = d
            outs[n + q][...] = m2
            outs[2 * n + q][...] = v2

    operands = list(ws) + list(gs) + list(ms) + list(vs)
    outs = [jax.ShapeDtypeStruct(w.shape, F32) for w in ws] * 3
    return pl.pallas_call(body, name=name, in_specs=[_whole(a) for a in operands],
                          out_specs=[_whole(o) for o in outs], out_shape=outs, compiler_params=_params())(*operands)


def _rows_tile(r, most):
    best = None
    for t in range(16, min(r, most) + 1, 16):
        if r % t == 0:
            best = t
    assert best is not None, r
    return best


def sum_slots(name, slots, out_dtype=F32):
    n, r, c = slots.shape
    tr = _rows_tile(r, max(16, (2 * 1024 * 1024) // (n * c)))

    def body(s_ref, o_ref):
        acc = s_ref[0].astype(F32)
        for q in range(1, n):
            acc = acc + s_ref[q].astype(F32)
        o_ref[...] = acc.astype(o_ref.dtype)

    return pl.pallas_call(body, name=name, grid=(r // tr,),
                          in_specs=[pl.BlockSpec((n, tr, c), lambda i: (0, i, 0))],
                          out_specs=pl.BlockSpec((tr, c), lambda i: (i, 0)),
                          out_shape=jax.ShapeDtypeStruct((r, c), out_dtype), compiler_params=_params(("parallel",)))(slots)


HBM_SPEC = pl.BlockSpec(memory_space=pltpu.HBM)


def _coords():
    return lax.axis_index("x"), lax.axis_index("y"), lax.axis_index("c")


def _linear(x, y, c):
    return 4 * x + 2 * y + c


def all_gather(name, shards):
    nq = len(shards)

    def body(*refs):
        xs, outs = refs[:nq], refs[nq:2 * nq]
        send_sems, recv_sems, local_sems = refs[2 * nq:2 * nq + 3]
        bufs = refs[2 * nq + 3:]
        x, y, cc = _coords()
        me, sibling = (x, y, cc), (x, y, 1 - cc)
        chips = [(1 - x, y), (x, 1 - y), (1 - x, 1 - y)]

        def slot(q, px, py, pc):
            return outs[q].at[_linear(px, py, pc)]

        def copy(q, k, block, to, src=None):
            return pltpu.make_async_remote_copy(
                src_ref=slot(q, *block) if src is None else src, dst_ref=slot(q, *block),
                send_sem=send_sems.at[7 * q + k], recv_sem=recv_sems.at[7 * q + k], device_id=to, device_id_type=MESH)

        loads = [pltpu.make_async_copy(xs[q], bufs[q], local_sems.at[q]) for q in range(nq)]
        for cp in loads:
            cp.start()
        for cp in loads:
            cp.wait()
        mine = [pltpu.make_async_copy(bufs[q], slot(q, *me), local_sems.at[q]) for q in range(nq)]
        first = []
        for q in range(nq):
            first.append(copy(q, 0, me, sibling, src=bufs[q]))
            first += [copy(q, 1 + j, me, (*chip, cc), src=bufs[q]) for j, chip in enumerate(chips)]
        for cp in mine + first:
            cp.start()
        passed = []
        for q in range(nq):
            for j, chip in enumerate(chips):
                copy(q, 1 + j, (*chip, cc), me).wait_recv()
                passed.append(copy(q, 4 + j, (*chip, cc), sibling))
                passed[-1].start()
        for q in range(nq):
            copy(q, 0, sibling, me).wait_recv()
            for j, chip in enumerate(chips):
                copy(q, 4 + j, (*chip, 1 - cc), me).wait_recv()
        for cp in first + passed:
            cp.wait_send()
        for cp in mine:
            cp.wait()

    return pl.pallas_call(
        body, name=name, in_specs=[HBM_SPEC] * nq, out_specs=[HBM_SPEC] * nq,
        out_shape=[jax.ShapeDtypeStruct((NDEV,) + s.shape, s.dtype) for s in shards],
        scratch_shapes=[pltpu.SemaphoreType.DMA((7 * nq,)), pltpu.SemaphoreType.DMA((7 * nq,)),
                        pltpu.SemaphoreType.DMA((nq,))] + [pltpu.VMEM(s.shape, s.dtype) for s in shards],
    )(*shards)


NCHIP = 4


SEM_SPEC = pl.BlockSpec(memory_space=pltpu.SEMAPHORE)
EFFECT = pltpu.SideEffectType.DATAFLOW_SIDE_EFFECTING


def _peer(x, y, cc, k):
    fx, fy, fc = (k >> 2) & 1, (k >> 1) & 1, k & 1
    return (x + fx - 2 * fx * x, y + fy - 2 * fy * y, cc + fc - 2 * fc * cc)


def gather_plan(srcs, lands, coords):
    x, y, cc = coords
    me = _linear(x, y, cc)
    return [(s, l.at[me], _peer(x, y, cc, k)) for s, l in zip(srcs, lands) for k in range(1, NDEV)]


def near_plan(srcs, lands, coords):
    x, y, cc = coords
    me = _linear(x, y, cc)
    peers = [(x, y, 1 - cc)] + [_peer(x, y, cc, 2 * k) for k in range(1, NCHIP)]
    return [(s, l.at[me], p) for s, l in zip(srcs, lands) for p in peers]


def pass_on_plan(srcs, lands, coords):
    x, y, cc = coords
    out = []
    for l in srcs:
        for k in range(1, NCHIP):
            px, py, _ = _peer(x, y, cc, 2 * k)
            slot = _linear(px, py, cc)
            out.append((l.at[slot], l.at[slot], (x, y, 1 - cc)))
    return out


def pair_plan(srcs, lands, coords):
    x, y, cc = coords
    return [(s.at[2 * chip + 1 - cc], l.at[chip], (x, y, 1 - cc)) for s, l in zip(srcs, lands) for chip in range(NCHIP)]


def chip_plan(srcs, lands, coords):
    x, y, cc = coords
    out = []
    for s, l in zip(srcs, lands):
        for k in range(1, NCHIP):
            px, py, _ = _peer(x, y, cc, 2 * k)
            out.append((s.at[2 * px + py], l.at[k - 1], (px, py, cc)))
    return out


def _remote(copy, i, send_sems, recv_sems):
    src, dst, dev = copy
    return pltpu.make_async_remote_copy(src_ref=src, dst_ref=dst, send_sem=send_sems.at[i], recv_sem=recv_sems.at[i],
                                        device_id=dev, device_id_type=MESH)


def exchange_start(name, plan, ncopy, srcs, land_shapes, deps=()):
    ns, nl, nd = len(srcs), len(land_shapes), len(deps)

    def body(*refs):
        s, l = refs[:ns], refs[ns:ns + nl]
        send_sems, recv_sems = refs[ns + nl + nd], refs[ns + nl + nd + 1]
        token = refs[-1]
        for i, cp in enumerate(plan(s, l, _coords())):
            _remote(cp, i, send_sems, recv_sems).start()
        token[...] = jnp.zeros_like(token)

    hbm = lambda a: pltpu.with_memory_space_constraint(a, pltpu.HBM)
    lands = [lax.empty(shp, dt) for shp, dt in land_shapes]
    thru = [pltpu.HBM(a.shape, a.dtype) for a in list(srcs) + lands]
    outs = pl.pallas_call(
        body, name=name,
        in_specs=[HBM_SPEC] * (ns + nl) + [ANY_SPEC] * nd,
        out_specs=(SEM_SPEC, SEM_SPEC, *[HBM_SPEC] * (ns + nl), pl.BlockSpec(memory_space=pltpu.VMEM)),
        out_shape=(pltpu.SemaphoreType.DMA((ncopy,)), pltpu.SemaphoreType.DMA((ncopy,)), *thru,
                   jax.ShapeDtypeStruct((8, LANE), F32)),
        input_output_aliases={i: 2 + i for i in range(ns + nl)},
        compiler_params=pltpu.CompilerParams(has_side_effects=EFFECT),
    )(*[hbm(a) for a in srcs], *[hbm(a) for a in lands], *deps)
    return outs[0], outs[1], list(outs[2:2 + ns]), list(outs[2 + ns:2 + ns + nl]), outs[-1]


def exchange_wait(name, plan, started, after, place_own=False):
    send_sems, recv_sems, srcs, lands, _ = started
    ns, nl = len(srcs), len(lands)

    def body(*refs):
        s, l = refs[:ns], refs[ns:ns + nl]
        send_sems, recv_sems = refs[ns + nl], refs[ns + nl + 1]
        l_out = refs[2 * ns + nl + 3:2 * ns + 2 * nl + 3]
        scratch = refs[2 * ns + 2 * nl + 3:]
        copies = [_remote(cp, i, send_sems, recv_sems) for i, cp in enumerate(plan(s, l, _coords()))]
        if place_own:
            me = _linear(*_coords())
            local_sems, bufs = scratch[0], scratch[1:]
            loads = [pltpu.make_async_copy(s[q], bufs[q], local_sems.at[q]) for q in range(ns)]
            for cp in loads:
                cp.start()
            for cp in loads:
                cp.wait()
            stores = [pltpu.make_async_copy(bufs[q], l_out[q].at[me], local_sems.at[q]) for q in range(ns)]
            for cp in stores:
                cp.start()
        for cp in copies:
            cp.wait_recv()
        for cp in copies:
            cp.wait_send()
        if place_own:
            for cp in stores:
                cp.wait()

    scratch_shapes = []
    if place_own:
        scratch_shapes = [pltpu.SemaphoreType.DMA((ns,))] + [pltpu.VMEM(a.shape, a.dtype) for a in srcs]
    outs = pl.pallas_call(
        body, name=name,
        in_specs=[HBM_SPEC] * (ns + nl) + [SEM_SPEC, SEM_SPEC, ANY_SPEC],
        out_specs=[HBM_SPEC] * (ns + nl),
        out_shape=[pltpu.HBM(a.shape, a.dtype) for a in srcs + lands],
        input_output_aliases={i: i for i in range(ns + nl)},
        scratch_shapes=scratch_shapes,
        compiler_params=pltpu.CompilerParams(has_side_effects=EFFECT),
    )(*srcs, *lands, send_sems, recv_sems, after)
    return list(outs[:ns]), list(outs[ns:])


def pair_sum(name, g, recv):
    _, r, c = g.shape
    tr = _rows_tile(r, 512)

    def body(g_ref, r_ref, o_ref):
        own = jnp.where(lax.axis_index("c") == 0, g_ref[0, 0], g_ref[0, 1])
        o_ref[0] = (own.astype(F32) + r_ref[0].astype(F32)).astype(o_ref.dtype)

    return pl.pallas_call(
        body, name=name, grid=(NCHIP, r // tr),
        in_specs=[pl.BlockSpec((1, 2, tr, c), lambda k, i: (k, 0, i, 0)), pl.BlockSpec((1, tr, c), lambda k, i: (k, i, 0))],
        out_specs=pl.BlockSpec((1, tr, c), lambda k, i: (k, i, 0)),
        out_shape=jax.ShapeDtypeStruct((NCHIP, r, c), g.dtype), compiler_params=_params(("parallel", "parallel")),
    )(g.reshape(NCHIP, 2, r, c), recv)


def chip_sum(name, partial, recv):
    _, r, c = partial.shape
    tr = _rows_tile(r, 512)

    def body(p_ref, r_ref, o_ref):
        chip = 2 * lax.axis_index("x") + lax.axis_index("y")
        own = p_ref[0]
        for k in range(1, NCHIP):
            own = jnp.where(chip == k, p_ref[k], own)
        acc = own.astype(F32)
        for k in range(NCHIP - 1):
            acc = acc + r_ref[k].astype(F32)
        o_ref[...] = acc

    return pl.pallas_call(
        body, name=name, grid=(r // tr,),
        in_specs=[pl.BlockSpec((NCHIP, tr, c), lambda i: (0, i, 0)), pl.BlockSpec((NCHIP - 1, tr, c), lambda i: (0, i, 0))],
        out_specs=pl.BlockSpec((tr, c), lambda i: (i, 0)),
        out_shape=jax.ShapeDtypeStruct((r, c), F32), compiler_params=_params(("parallel",)),
    )(partial, recv)


def _block_diag(w, rows_per, cols_per):
    w = w.reshape(NBLK, 8, rows_per, cols_per)
    eye = jnp.eye(8, dtype=w.dtype)
    out = w[:, :, :, None, :] * eye[None, :, None, :, None]
    return out.reshape(NBLK, 8 * rows_per, 8 * cols_per)


def _diag_blocks(wd, rows_per, cols_per):
    wd = wd.reshape(NBLK, 8, rows_per, 8, cols_per)
    idx = jnp.arange(8)
    return wd[:, idx, :, idx, :].transpose(1, 0, 2, 3).reshape(NG, rows_per, cols_per)


def _pad_rows(v, mult):
    n = v.shape[0]
    return jnp.pad(v, (0, (-n) % mult))


def kernel(x, c, w_ada, b_ada, norm1_g, w_in, conv_w, conv_b, conv_ln_g, conv_ln_b, conv_proj, ssm_a_re, ssm_a_im, ssm_b_re, ssm_b_im, ssm_c_re, ssm_c_im, ssm_d, ssm_log_dt, ssm_glu, w_out, norm2_g, w_ffn_in, w_ffn_out, final_g, loss_target, m_w_ada, m_b_ada, m_norm1_g, m_w_in, m_conv_w, m_conv_b, m_conv_ln_g, m_conv_ln_b, m_conv_proj, m_ssm_a_re, m_ssm_a_im, m_ssm_b_re, m_ssm_b_im, m_ssm_c_re, m_ssm_c_im, m_ssm_d, m_ssm_log_dt, m_ssm_glu, m_w_out, m_norm2_g, m_w_ffn_in, m_w_ffn_out, m_final_g, v_w_ada, v_b_ada, v_norm1_g, v_w_in, v_conv_w, v_conv_b, v_conv_ln_g, v_conv_ln_b, v_conv_proj, v_ssm_a_re, v_ssm_a_im, v_ssm_b_re, v_ssm_b_im, v_ssm_c_re, v_ssm_c_im, v_ssm_d, v_ssm_log_dt, v_ssm_glu, v_w_out, v_norm2_g, v_w_ffn_in, v_w_ffn_out, v_final_g):
    me = _linear(*_coords())
    xs = x[0]
    tgt = loss_target[0]
    seq = xs.shape[0]

    flat = lambda g: g.reshape(NDEV * g.shape[1], g.shape[2])
    c_all, cw_g = all_gather("gather_c_conv_w", [c, conv_w[0]])
    w_in_s = w_in[0].T.astype(BF16)
    mids = [p.astype(BF16) for p in (conv_proj[0].T, ssm_glu[0].T, w_out[0])]
    ffns = [p.astype(BF16) for p in (w_ffn_in[0].T, w_ffn_out[0])]
    zone = lambda p: ((NDEV,) + p.shape, p.dtype)
    in_go = exchange_start("gather_in_start", near_plan, NCHIP, [w_in_s], [zone(w_in_s)], deps=[c_all])
    mids_go = exchange_start("gather_mid_start", gather_plan, 7 * len(mids), mids, [zone(p) for p in mids],
                             deps=[in_go[4]])
    ffns_go = exchange_start("gather_ffn_start", gather_plan, 7 * len(ffns), ffns, [zone(p) for p in ffns],
                             deps=[mids_go[4]])

    ncol = w_ada.shape[2]
    c_all = c_all.reshape(NDEV, D)
    b_cols = lax.dynamic_slice_in_dim(b_ada, me * ncol, ncol, axis=1)
    act_all, mod_cols = mod_fwd(c_all, w_ada[0], b_cols)
    (mod_all,) = all_gather("gather_mod", [mod_cols])
    mod = lax.dynamic_index_in_dim(mod_all, me, axis=1, keepdims=False).reshape(NMOD, D)
    sh1, sc1, g1, sh2, sc2, g2 = [mod[q:q + 1] for q in range(NMOD)]

    expand = jnp.repeat(jnp.eye(NG, dtype=F32), NP, axis=0)
    a_re_c, a_im_c = ssm_a_re.reshape(NST, 1), ssm_a_im.reshape(NST, 1)
    ldt_c = ssm_log_dt.reshape(NG, 1)
    b_re_r, b_im_r = ssm_b_re.reshape(NST, GH), ssm_b_im.reshape(NST, GH)
    e_re, e_im, bb_re, bb_im = disc_fwd(a_re_c, a_im_c, ldt_c, b_re_r, b_im_r, expand)
    e_re_b, e_im_b = e_re.reshape(NBLK, 1, SB), e_im.reshape(NBLK, 1, SB)
    bb_re_g, bb_im_g = bb_re.reshape(NG, NP, GH), bb_im.reshape(NG, NP, GH)
    wbt_re = _block_diag(bb_re_g, NP, GH)
    wbt_im = _block_diag(bb_im_g, NP, GH)
    wb_re, wb_im = wbt_re.transpose(0, 2, 1), wbt_im.transpose(0, 2, 1)
    wct = jnp.concatenate([_block_diag(ssm_c_re[0], GH, NP), -_block_diag(ssm_c_im[0], GH, NP)], axis=2)
    wc = wct.transpose(0, 2, 1)
    to_b = lambda a: a.astype(BF16)
    dvec = ssm_d.reshape(NBLK, 1, CB)

    n1g = norm1_g

    def f_norm1(xv, g, sc, sh):
        _, xh = _rms_stats(xv)
        return [xh * g * (1.0 + sc) + sh], []

    (h1,) = rowwise("norm1", f_norm1, [xs], [n1g, sc1, sh1], [(D, BF16)], [], 512, deps=[ffns_go[4]])
    _, (w_in_land,) = exchange_wait("gather_in_wait", near_plan, in_go, h1, place_own=True)
    pass_go = exchange_start("gather_in_pass_start", pass_on_plan, NCHIP - 1, [w_in_land], [])
    (w_in_g,), _ = exchange_wait("gather_in_pass_wait", pass_on_plan, pass_go, pass_go[4])
    w_in_t = flat(w_in_g)
    z = mm("mm_in", h1, w_in_t, "nt", tiles=(2048, CW, 1024), b_rot=Z_ROT)

    conv_w_full = cw_g.transpose(1, 0, 2).reshape(KC, CW)
    w32 = jnp.pad(conv_w_full, ((0, HALO - KC), (0, 0)))
    yc, s_act = conv_fwd(z, w32, conv_b, conv_ln_g, conv_ln_b)
    conv_proj_t, ssm_glu_t, w_out_f = [
        flat(g) for g in exchange_wait("gather_mid_wait", gather_plan, mids_go, s_act, place_own=True)[1]]
    y_conv = mm("mm_conv_proj", s_act, conv_proj_t, "nt")

    xs_re, xs_im, ypre, gl = ssm_fwd(z, to_b(wb_re), to_b(wb_im), to_b(wc), e_re_b, e_im_b, dvec)
    n_mrg = D // MRG_BLK

    pair_of = lambda t, n: t // 2 + (t % 2) * n

    def ep_merge(accs, yc_v, gates):
        za, zb = accs
        glc, gls = gates[:, 0:MRG_BLK], gates[:, MRG_BLK:2 * MRG_BLK]
        return [_sig(glc) * yc_v + _sig(gls) * (za * _sig(zb)), jnp.concatenate([za, zb], axis=1)]

    merged, z2_pair = mm_ep_pipe("mm_ssm_glu", gl, ssm_glu_t, 2, lambda j, q: j + q * n_mrg, ep_merge,
                                 [(y_conv, 1, 0), (z, 2, 0)], [(D, BF16, 1), (2 * D, BF16, 2)], (512, MRG_BLK, SW))
    row_tiles = lambda bk: (512, D, bk)
    whole = lambda j, q: j

    def ep_norm2(accs, xv, g1v, g, sc, sh):
        (o1v,) = accs
        x1v = xv + g1v * o1v
        _, xh = _rms_stats(x1v)
        return [x1v, xh * g * (1.0 + sc) + sh, o1v]

    x1, h2, o1 = mm_ep_pipe("mm_out", merged, w_out_f, 1, whole, ep_norm2, [(xs, 1, 0)],
                            [(D, F32, 1), (D, BF16, 1), (D, F32, 1)], row_tiles(D), b_kn=True,
                            consts=[g1, norm2_g, sc2, sh2])
    w_ffn_in_t, w_ffn_out_f = [
        flat(g) for g in exchange_wait("gather_ffn_wait", gather_plan, ffns_go, h2, place_own=True)[1]]
    ffn_tiles = (512, FFN_BLK, 1024)
    n_ffn_blk = FH // FFN_BLK
    pair_map = lambda t: t // 2 + (t % 2) * n_ffn_blk

    def ep_swiglu(accs):
        fg, fu = accs
        return [fg * _sig(fg) * fu, jnp.concatenate([fg, fu], axis=1)]

    act, f_pair = mm_ep_pipe("mm_ffn_in", h2, w_ffn_in_t, 2, lambda j, q: j + q * n_ffn_blk, ep_swiglu, [],
                             [(FH, BF16, 1), (2 * FH, BF16, 2)], ffn_tiles)
    fg_row = final_g.reshape(1, D)

    def ep_final(accs, x1v, tv, g2v, fg):
        (o2v,) = accs
        x2v = x1v + g2v * o2v
        r, xh = _rms_stats(x2v)
        yv = xh * fg
        err = yv - tv
        loss = jnp.sum(_colsum(err * err), axis=1, keepdims=True) * (0.5 / D)
        dy = err * (1.0 / D)
        dx2 = _rms_bwd(dy * fg, xh, r)
        return ([dx2, g2v * dx2],
                [jnp.broadcast_to(loss, (1, LANE)), _colsum(dy * xh), _colsum(dx2 * o2v)])

    dx2, do2, loss_l, d_final_g, d_g2 = mm_ep_pipe(
        "mm_ffn_out", act, w_ffn_out_f, 1, whole, ep_final, [(x1, 1, 0), (tgt, 1, 0)],
        [(D, F32, 1), (D, BF16, 1)], row_tiles(FH), b_kn=True, consts=[g2, fg_row], sums=[LANE, D, D])

    g_ffn_out = mm("mm_g_ffn_out", act, do2, "tn", BF16, tiles=(FFN_BLK, 1024, 1024))

    def ep_dswiglu(accs, fp):
        (da,) = accs
        fg, fu = fp[:, 0:FFN_BLK].astype(F32), fp[:, FFN_BLK:2 * FFN_BLK].astype(F32)
        sg = _sig(fg)
        return [jnp.concatenate([da * fu * (sg * (1.0 + fg * (1.0 - sg))), da * (fg * sg)], axis=1)]

    (df,) = mm_ep_pipe("mm_dact", do2, w_ffn_out_f, 1, lambda j, q: j, ep_dswiglu, [(f_pair, 2, 0)],
                       [(2 * FH, BF16, 2)], ffn_tiles)
    g_ffn_in_t = mm("mm_g_ffn_in", df, h2, "tn", BF16, tiles=(FFN_BLK, 1024, 1024), o_rot=pair_map)

    def pair_go(tag, grads_t, deps=()):
        srcs = [g.reshape(NDEV, -1, D) for g in grads_t]
        return exchange_start("pair_" + tag + "_start", pair_plan, NCHIP * len(srcs), srcs,
                              [((NCHIP,) + s.shape[1:], s.dtype) for s in srcs], deps)

    def chip_go(tag, names, pair_started, after):
        own, from_sibling = exchange_wait("pair_" + tag + "_wait", pair_plan, pair_started, after)
        partials = [pair_sum("pair_sum_" + n, g, r) for n, g, r in zip(names, own, from_sibling)]
        return exchange_start("chip_" + tag + "_start", chip_plan, (NCHIP - 1) * len(partials), partials,
                              [((NCHIP - 1,) + p.shape[1:], p.dtype) for p in partials])

    def chip_done(tag, names, chip_started, after):
        partials, from_chips = exchange_wait("chip_" + tag + "_wait", chip_plan, chip_started, after)
        return [chip_sum("chip_sum_" + n, p, r) for n, p, r in zip(names, partials, from_chips)]

    pair_ffn = pair_go("ffn", [g_ffn_out, g_ffn_in_t])

    dh2 = mm("mm_dh2", df, w_ffn_in_t, "nn", tiles=(1024, 1024, FFN_BLK), b_rot=pair_map,
             deps=[pair_ffn[4]])

    def f_dnorm2(dh, x1v, dx2v, o1v, g, sc, g1v):
        r, xh = _rms_stats(x1v)
        dxh = dh * (1.0 + sc) * g
        dx1 = dx2v + _rms_bwd(dxh, xh, r)
        return ([dx1, g1v * dx1],
                [_colsum(dh * xh * g), _colsum(dh), _colsum(dh * (1.0 + sc) * xh), _colsum(dx1 * o1v)])

    dx1, do1, d_sc2, d_sh2, d_n2g, d_g1 = rowwise(
        "dnorm2", f_dnorm2, [dh2, x1, dx2, o1], [norm2_g, sc2, g1], [(D, F32), (D, BF16)], [D, D, D, D], 256)

    g_out = mm("mm_g_out", merged, do1, "tn", BF16)
    chip_ffn = chip_go("ffn", ("w_ffn_out", "w_ffn_in"), pair_ffn, g_out)

    def ep_dmerge(accs, yc_v, z2p, gates):
        (dm,) = accs
        za, zb = z2p[:, 0:MRG_BLK].astype(F32), z2p[:, MRG_BLK:2 * MRG_BLK].astype(F32)
        sc_, ss_, sb_ = _sig(gates[:, 0:MRG_BLK]), _sig(gates[:, MRG_BLK:2 * MRG_BLK]), _sig(zb)
        dys = dm * ss_
        dz2 = jnp.concatenate([dys * sb_, dys * za * sb_ * (1.0 - sb_)], axis=1)
        dgates = jnp.concatenate([dm * yc_v * sc_ * (1.0 - sc_), dm * (za * sb_) * ss_ * (1.0 - ss_)], axis=1)
        return [dm * sc_, dz2, dgates]

    dyconv, dz2, dz = mm_ep_pipe("mm_dmerged", do1, w_out_f, 1, lambda j, q: j, ep_dmerge,
                                 [(y_conv, 1, 0), (z2_pair, 2, 0), (z, 2, 0)],
                                 [(D, BF16, 1), (2 * D, BF16, 2), (ZW, BF16, 2)], (512, MRG_BLK, 1024),
                                 deps=[chip_ffn[4]])

    g_conv_proj_t = mm("mm_g_conv_proj", dyconv, s_act, "tn", BF16)
    mrg_map = lambda t: pair_of(t, n_mrg)
    dgl = mm("mm_dgl", dz2, ssm_glu_t, "nn", tiles=(1024, SW, MRG_BLK), b_rot=mrg_map)
    g_ssm_glu_t = mm("mm_g_ssm_glu", dz2, gl, "tn", BF16, tiles=(MRG_BLK, SW, 1024), o_rot=mrg_map)
    pair_mid = pair_go("mid", [g_out, g_conv_proj_t, g_ssm_glu_t])
    ds = mm("mm_ds", dyconv, conv_proj_t, "nn", deps=[pair_mid[4]])
    dz, d_lng, d_lnb, d_cb, d_cw32 = conv_bwd(ds, yc, z, w32, conv_ln_g, conv_ln_b, dz)
    dz, d_d, d_ar, d_ai, d_wb_re, d_wb_im, d_wc = ssm_bwd(
        dgl, ypre, z, xs_re, xs_im, to_b(wbt_re), to_b(wbt_im), to_b(wct), e_re_b, e_im_b, dvec, dz)
    chip_mid = chip_go("mid", ("w_out", "conv_proj", "ssm_glu"), pair_mid, dz)

    d_bb_re = _diag_blocks(d_wb_re.transpose(0, 2, 1), NP, GH).reshape(NST, GH)
    d_bb_im = _diag_blocks(d_wb_im.transpose(0, 2, 1), NP, GH).reshape(NST, GH)
    d_wct = d_wc.transpose(0, 2, 1)
    d_c_re = _diag_blocks(d_wct[:, :, 0:SB], GH, NP)
    d_c_im = -_diag_blocks(d_wct[:, :, SB:2 * SB], GH, NP)
    d_a_re, d_a_im, d_ldt, d_b_re, d_b_im = disc_bwd(
        a_re_c, a_im_c, ldt_c, b_re_r, b_im_r, expand, d_ar.reshape(NST, 1), d_ai.reshape(NST, 1), d_bb_re, d_bb_im)

    small_local = [jnp.concatenate([d_g1, d_sh2, d_sc2, d_g2], axis=1).reshape(-1), d_cw32[0:KC].reshape(-1),
                   d_cb.reshape(-1), d_lng.reshape(-1), d_lnb.reshape(-1), d_a_re.reshape(-1), d_a_im.reshape(-1),
                   d_b_re.reshape(-1), d_b_im.reshape(-1), d_c_re.reshape(-1), d_c_im.reshape(-1), d_d.reshape(-1),
                   d_ldt.reshape(-1), d_n2g.reshape(-1), d_final_g.reshape(-1), loss_l[0, 0:1]]
    small_sizes = [v.shape[0] for v in small_local]
    small_pack = _pad_rows(jnp.concatenate(small_local), 256 * LANE).reshape(-1, LANE)
    small_go = exchange_start("gather_small_start", gather_plan, NDEV - 1, [small_pack],
                              [((NDEV,) + small_pack.shape, F32)], deps=[chip_mid[4]])

    g_in_t = mm("mm_g_in", dz, h1, "tn", BF16, tiles=(CW, 1024, 2048), o_rot=Z_ROT, deps=[small_go[4]])
    pair_in = pair_go("in", [g_in_t])

    dh1 = mm("mm_dh1", dz, w_in_t, "nn", tiles=(2048, 1024, CW), b_rot=Z_ROT, deps=[pair_in[4]])

    def f_dnorm1(dh, xv, dx1v, g, sc):
        r, xh = _rms_stats(xv)
        dxh = dh * (1.0 + sc) * g
        return ([dx1v + _rms_bwd(dxh, xh, r)],
                [_colsum(dh * xh * g), _colsum(dh), _colsum(dh * (1.0 + sc) * xh)])

    grad_x, d_sc1, d_sh1, d_n1g = rowwise(
        "dnorm1", f_dnorm1, [dh1, xs, dx1], [n1g, sc1], [(D, F32)], [D, D, D], 256)
    chip_in = chip_go("in", ("w_in",), pair_in, grad_x)

    weights = {
        "w_ada": (w_ada, m_w_ada, v_w_ada), "b_ada": (b_ada, m_b_ada, v_b_ada), "norm1_g": (norm1_g, m_norm1_g, v_norm1_g),
        "w_in": (w_in, m_w_in, v_w_in), "conv_w": (conv_w, m_conv_w, v_conv_w), "conv_b": (conv_b, m_conv_b, v_conv_b),
        "conv_ln_g": (conv_ln_g, m_conv_ln_g, v_conv_ln_g), "conv_ln_b": (conv_ln_b, m_conv_ln_b, v_conv_ln_b),
        "conv_proj": (conv_proj, m_conv_proj, v_conv_proj), "ssm_a_re": (ssm_a_re, m_ssm_a_re, v_ssm_a_re),
        "ssm_a_im": (ssm_a_im, m_ssm_a_im, v_ssm_a_im), "ssm_b_re": (ssm_b_re, m_ssm_b_re, v_ssm_b_re),
        "ssm_b_im": (ssm_b_im, m_ssm_b_im, v_ssm_b_im), "ssm_c_re": (ssm_c_re, m_ssm_c_re, v_ssm_c_re),
        "ssm_c_im": (ssm_c_im, m_ssm_c_im, v_ssm_c_im), "ssm_d": (ssm_d, m_ssm_d, v_ssm_d),
        "ssm_log_dt": (ssm_log_dt, m_ssm_log_dt, v_ssm_log_dt), "ssm_glu": (ssm_glu, m_ssm_glu, v_ssm_glu),
        "w_out": (w_out, m_w_out, v_w_out), "norm2_g": (norm2_g, m_norm2_g, v_norm2_g),
        "w_ffn_in": (w_ffn_in, m_w_ffn_in, v_w_ffn_in), "w_ffn_out": (w_ffn_out, m_w_ffn_out, v_w_ffn_out),
        "final_g": (final_g, m_final_g, v_final_g),
    }
    order = list(weights)
    big = ("w_ada", "w_in", "conv_proj", "ssm_glu", "w_out", "w_ffn_in", "w_ffn_out")
    grads, delta, new_m, new_v = {}, {}, {}, {}

    def adam_big(n, g2d, transposed=False):
        wv, mv, vv = weights[n]
        shp = wv.shape
        t_in = (lambda a: a.reshape(shp[-2:]).T) if transposed else (lambda a: a.reshape(shp[-2:]))
        t_out = (lambda a: a.T.reshape(shp)) if transposed else (lambda a: a.reshape(shp))
        d_, m_, v_ = adam("adam_" + n, t_in(wv), g2d, t_in(mv), t_in(vv))
        grads[n], delta[n], new_m[n], new_v[n] = t_out(g2d), t_out(d_), t_out(m_), t_out(v_)
        return d_

    gs_ffn_out, gs_ffn_in = chip_done("ffn", ("w_ffn_out", "w_ffn_in"), chip_ffn, chip_in[4])
    adam_big("w_ffn_out", gs_ffn_out)
    last = adam_big("w_ffn_in", gs_ffn_in, transposed=True)
    gs_out, gs_conv_proj, gs_ssm_glu = chip_done("mid", ("w_out", "conv_proj", "ssm_glu"), chip_mid, last)
    adam_big("w_out", gs_out)
    adam_big("conv_proj", gs_conv_proj.reshape(-1, CW), transposed=True)
    adam_big("ssm_glu", gs_ssm_glu.reshape(-1, SW), transposed=True)

    late_local = [d_sh1.reshape(-1), d_sc1.reshape(-1), d_n1g.reshape(-1)]
    late_pack = _pad_rows(jnp.concatenate(late_local), 16 * LANE).reshape(-1, LANE)
    (late_all,) = all_gather("gather_small_late", [late_pack])
    _, (small_all,) = exchange_wait("gather_small_wait", gather_plan, small_go, late_all, place_own=True)

    def unpack(vec, sizes):
        out, pos = [], 0
        for n in sizes:
            out.append(vec[pos:pos + n])
            pos += n
        return out

    g_sh1, g_sc1, g_n1g = unpack(sum_slots("sum_small_late", late_all).reshape(-1), [D, D, D])
    (g_mod_rest, g_cw_full, g_cb, g_lng, g_lnb, g_a_re, g_a_im, g_b_re, g_b_im, g_c_re, g_c_im, g_d, g_ldt,
     g_n2g, g_fg, loss_sum) = unpack(sum_slots("sum_small", small_all).reshape(-1), small_sizes)
    g_b_ada = jnp.concatenate([g_sh1, g_sc1, g_mod_rest])
    loss = loss_sum[0]
    dmod_all = jnp.concatenate([late_all.reshape(NDEV, -1)[:, 0:2 * D], small_all.reshape(NDEV, -1)[:, 0:4 * D]],
                               axis=1)
    g_w_ada = ada_grad(act_all, lax.dynamic_slice_in_dim(dmod_all, me * ncol, ncol, axis=1))
    ccol = conv_w.shape[2]
    g_conv_w = lax.dynamic_slice_in_dim(g_cw_full.reshape(KC, CW), me * ccol, ccol, axis=1)

    adam_big("w_ada", g_w_ada)
    grads.update({
        "b_ada": g_b_ada.reshape(b_ada.shape), "norm1_g": g_n1g.reshape(norm1_g.shape),
        "conv_w": g_conv_w[None], "conv_b": g_cb.reshape(conv_b.shape),
        "conv_ln_g": g_lng.reshape(conv_ln_g.shape), "conv_ln_b": g_lnb.reshape(conv_ln_b.shape),
        "ssm_a_re": g_a_re.reshape(ssm_a_re.shape),
        "ssm_a_im": g_a_im.reshape(ssm_a_im.shape), "ssm_b_re": g_b_re.reshape(ssm_b_re.shape),
        "ssm_b_im": g_b_im.reshape(ssm_b_im.shape), "ssm_c_re": g_c_re.reshape(ssm_c_re.shape),
        "ssm_c_im": g_c_im.reshape(ssm_c_im.shape), "ssm_d": g_d.reshape(ssm_d.shape),
        "ssm_log_dt": g_ldt.reshape(ssm_log_dt.shape),
        "norm2_g": g_n2g.reshape(norm2_g.shape),
        "final_g": g_fg.reshape(final_g.shape),
    })
    small = [n for n in order if n not in big]
    def rows(a):
        if a.ndim == 4 and a.shape[-1] < a.shape[-2]:
            a = a.swapaxes(-1, -2)
        return a.reshape(1, -1) if a.ndim == 1 else a.reshape(-1, a.shape[-1])

    def unrows(a, shp):
        if len(shp) == 4 and shp[-1] < shp[-2]:
            return a.reshape(shp[:-2] + (shp[-1], shp[-2])).swapaxes(-1, -2)
        return a.reshape(shp)

    small_out = adam_many("adam_small", [rows(weights[n][0]) for n in small], [rows(grads[n]) for n in small],
                          [rows(weights[n][1]) for n in small], [rows(weights[n][2]) for n in small])
    for q, n in enumerate(small):
        shp = weights[n][0].shape
        delta[n], new_m[n], new_v[n] = [unrows(small_out[t * len(small) + q], shp) for t in range(3)]

    (gs_in,) = chip_done("in", ("w_in",), chip_in, small_out[0])
    adam_big("w_in", gs_in, transposed=True)

    return (loss, grad_x[None], *[grads[n] for n in order], *[delta[n] for n in order],
            *[new_m[n] for n in order], *[new_v[n] for n in order])
```

```python
import functools
import math

import jax
import jax.numpy as jnp
from jax import lax
from jax.experimental import pallas as pl
from jax.experimental.pallas import tpu as pltpu

F32 = jnp.float32
BF16 = jnp.bfloat16

D = 1024
CW = 512
KC = 31
SW = 512
NG = 32
GH = 16
NP = 64
NST = NG * NP
FH = 2816
FFN_BLK = 1408
MRG_BLK = 1024
NMOD = 6
NDEV = 8
EPS = 1e-6
CB = 128
SB = 512
NBLK = SW // CB
HALO = 32
ZW = 2 * CW + SW + 2 * D
Z_ROT = lambda j: (j + 3) % (ZW // CW)
ZB_A, ZB_G, ZB_U = 4, 5, 6

ADAM_LR = 0.001
ADAM_B1 = 0.9
ADAM_B2 = 0.999
ADAM_EPS = 1e-08
ADAM_WD = 0.01
ADAM_STEP = 10

V7X_VMEM_BYTES = 64 * 1024 * 1024
VMEM_LIMIT = V7X_VMEM_BYTES - 8 * 1024 * 1024
LANE = 128
MESH = pl.DeviceIdType.MESH
ANY_SPEC = pl.BlockSpec(memory_space=pl.ANY)


def _params(sem=None, **kw):
    if sem is not None:
        kw["dimension_semantics"] = sem
    return pltpu.CompilerParams(vmem_limit_bytes=VMEM_LIMIT, **kw)


def _tile(n, most):
    best = None
    for t in range(LANE, most + 1, LANE):
        if n % t == 0:
            best = t
    if best is None:
        raise ValueError(f"no tile for {n}")
    return best


def _sig(x):
    return jax.nn.sigmoid(x)


def mm(name, a, b, mode, out_dtype=F32, tiles=None, b_rot=None, o_rot=None, deps=()):
    if mode == "nn":
        (m, k), (k2, n) = a.shape, b.shape
    elif mode == "nt":
        (m, k), (n, k2) = a.shape, b.shape
    else:
        (k, m), (k2, n) = a.shape, b.shape
    assert k == k2, (name, a.shape, b.shape)
    bm, bn, bk = tiles or (_tile(m, 1024), _tile(n, 1408), _tile(k, 1408 if k % 1408 == 0 else 1024))
    bm, bn, bk = min(bm, m), min(bn, n), min(bk, k)
    assert m % bm == 0 and n % bn == 0 and k % bk == 0, (name, m, n, k, bm, bn, bk)
    nk = k // bk
    rot = lambda idx, r: idx if r is None else r(idx)
    if mode == "nn":
        a_spec = pl.BlockSpec((bm, bk), lambda i, j, kk: (i, kk))
        b_spec = pl.BlockSpec((bk, bn), lambda i, j, kk: (rot(kk, b_rot), j))
        dims = (((1,), (0,)), ((), ()))
    elif mode == "nt":
        a_spec = pl.BlockSpec((bm, bk), lambda i, j, kk: (i, kk))
        b_spec = pl.BlockSpec((bn, bk), lambda i, j, kk: (rot(j, b_rot), kk))
        dims = (((1,), (1,)), ((), ()))
    else:
        assert b_rot is None
        a_spec = pl.BlockSpec((bk, bm), lambda i, j, kk: (kk, i))
        b_spec = pl.BlockSpec((bk, bn), lambda i, j, kk: (kk, j))
        dims = (((0,), (0,)), ((), ()))

    def body(a_ref, b_ref, *rest):
        o_ref, acc_ref = rest[-2:]
        kk = pl.program_id(2)

        @pl.when(kk == 0)
        def _():
            acc_ref[...] = jnp.zeros_like(acc_ref)

        acc_ref[...] += lax.dot_general(a_ref[...], b_ref[...], dims, preferred_element_type=F32)

        @pl.when(kk == nk - 1)
        def _():
            o_ref[...] = acc_ref[...].astype(o_ref.dtype)

    return pl.pallas_call(
        body, name=name,
        grid=(m // bm, n // bn, nk),
        in_specs=[a_spec, b_spec] + [ANY_SPEC] * len(deps),
        out_specs=pl.BlockSpec((bm, bn), lambda i, j, kk: (rot(i, o_rot), j)),
        out_shape=jax.ShapeDtypeStruct((m, n), out_dtype),
        scratch_shapes=[pltpu.VMEM((bm, bn), F32)],
        compiler_params=_params(("parallel", "parallel", "arbitrary")),
    )(a, b, *deps)


def mm_ep(name, a, b, n_acc, acc_block, epilogue, extras, outs, tiles, deps=(), b_kn=False, k_map=None,
          consts=(), sums=()):
    m, k = a.shape
    bm, bn, bk = tiles
    bm = min(bm, m)
    nj = outs[0][0] // (outs[0][2] * bn)
    nk = k // bk
    assert m % bm == 0 and k % bk == 0 and b.shape[0 if b_kn else 1] == k, (name, a.shape, b.shape, tiles)
    assert not sums or nj == 1, name
    ne, nc, no, ns, nd = len(extras), len(consts), len(outs), len(sums), len(deps)
    dims = (((1,), (0,)), ((), ())) if b_kn else (((1,), (1,)), ((), ()))
    kmap = (lambda kk: kk) if k_map is None else k_map

    def body(*refs):
        a_ref, b_refs = refs[0], refs[1:1 + n_acc]
        e_refs = refs[1 + n_acc:1 + n_acc + ne + nc]
        first_out = 1 + n_acc + ne + nc + nd
        o_refs = refs[first_out:first_out + no]
        s_refs = refs[first_out + no:first_out + no + ns]
        acc_refs = refs[first_out + no + ns:]
        av = a_ref[...]
        prods = [lax.dot_general(av, b_ref[...], dims, preferred_element_type=F32) for b_ref in b_refs]

        if ns:
            @pl.when((pl.program_id(0) == 0) & (pl.program_id(2) == 0))
            def _():
                for s_ref in s_refs:
                    s_ref[...] = jnp.zeros_like(s_ref)

        def finish(accs):
            res = epilogue(accs, *[e[...] for e in e_refs])
            tiles_out, sums_out = res if ns else (res, ())
            for o_ref, v in zip(o_refs, tiles_out):
                o_ref[...] = v.astype(o_ref.dtype)
            for s_ref, v in zip(s_refs, sums_out):
                s_ref[...] += v

        if nk == 1:
            finish(prods)
        else:
            kk = pl.program_id(2)

            @pl.when(kk == 0)
            def _():
                for acc_ref in acc_refs:
                    acc_ref[...] = jnp.zeros_like(acc_ref)

            for acc_ref, p in zip(acc_refs, prods):
                acc_ref[...] += p

            @pl.when(kk == nk - 1)
            def _():
                finish([acc_ref[...] for acc_ref in acc_refs])

    in_specs = [pl.BlockSpec((bm, bk), lambda i, j, kk: (i, kk))]
    if b_kn:
        in_specs += [pl.BlockSpec((bk, bn), functools.partial(lambda i, j, kk, q: (kmap(kk), acc_block(j, q)), q=q))
                     for q in range(n_acc)]
    else:
        in_specs += [pl.BlockSpec((bn, bk), functools.partial(lambda i, j, kk, q: (acc_block(j, q), kmap(kk)), q=q))
                     for q in range(n_acc)]
    in_specs += [pl.BlockSpec((bm, w * bn), functools.partial(lambda i, j, kk, off: (i, j + off), off=off))
                 for (_, w, off) in extras]
    in_specs += [pl.BlockSpec((1, bn), lambda i, j, kk: (0, j)) for _ in consts]
    in_specs += [ANY_SPEC] * nd
    out_specs = [pl.BlockSpec((bm, w * bn), lambda i, j, kk: (i, j)) for (_, _, w) in outs]
    out_specs += [pl.BlockSpec((1, w), lambda i, j, kk: (0, 0)) for w in sums]
    out_shape = [jax.ShapeDtypeStruct((m, cols), dt) for (cols, dt, _) in outs]
    out_shape += [jax.ShapeDtypeStruct((1, w), F32) for w in sums]
    return pl.pallas_call(
        body, name=name, grid=(m // bm, nj, nk),
        in_specs=in_specs, out_specs=out_specs, out_shape=out_shape,
        scratch_shapes=[pltpu.VMEM((bm, bn), F32)] * (n_acc if nk > 1 else 0),
        compiler_params=_params(("arbitrary",) * 3 if sums else ("parallel", "parallel", "arbitrary")),
    )(a, *[b] * n_acc, *[e[0] for e in extras], *consts, *deps)


def mm_ep_pipe(name, a, b, n_acc, acc_block, epilogue, extras, outs, tiles, deps=(), b_kn=False, consts=(), sums=()):
    m, k = a.shape
    bm, bn, bk = tiles
    bm = min(bm, m)
    assert bk == k and m % bm == 0 and b.shape[0 if b_kn else 1] == k, (name, a.shape, b.shape, tiles)
    ni, nj = m // bm, outs[0][0] // (outs[0][2] * bn)
    nt = ni * nj
    assert not sums or nj == 1, name
    ne, nc, no, ns, nd = len(extras), len(consts), len(outs), len(sums), len(deps)
    dims = (((1,), (0,)), ((), ())) if b_kn else (((1,), (1,)), ((), ()))
    cur_i = lambda t: jnp.minimum(t, nt - 1) // nj
    cur_j = lambda t: jnp.minimum(t, nt - 1) % nj
    prev_i = lambda t: jnp.maximum(t - 1, 0) // nj
    prev_j = lambda t: jnp.maximum(t - 1, 0) % nj

    def body(*refs):
        a_ref, b_refs = refs[0], refs[1:1 + n_acc]
        e_refs = refs[1 + n_acc:1 + n_acc + ne + nc]
        first_out = 1 + n_acc + ne + nc + nd
        o_refs = refs[first_out:first_out + no]
        s_refs = refs[first_out + no:first_out + no + ns]
        acc_ref = refs[first_out + no + ns]
        t = pl.program_id(0)

        @pl.when(t == 0)
        def _():
            acc_ref[...] = jnp.zeros_like(acc_ref)
            for s_ref in s_refs:
                s_ref[...] = jnp.zeros_like(s_ref)

        slot = t % 2
        done = [acc_ref[(1 - slot) * n_acc + q] for q in range(n_acc)]
        av = a_ref[...]
        for q, b_ref in enumerate(b_refs):
            acc_ref[slot * n_acc + q] = lax.dot_general(av, b_ref[...], dims, preferred_element_type=F32)
        res = epilogue(done, *[e[...] for e in e_refs])
        tiles_out, sums_out = res if ns else (res, ())
        for o_ref, v in zip(o_refs, tiles_out):
            o_ref[...] = v.astype(o_ref.dtype)
        live = (t >= 1).astype(F32)
        for s_ref, v in zip(s_refs, sums_out):
            s_ref[...] += v * live

    in_specs = [pl.BlockSpec((bm, k), lambda t: (cur_i(t), 0))]
    if b_kn:
        in_specs += [pl.BlockSpec((k, bn), functools.partial(lambda t, q: (0, acc_block(cur_j(t), q)), q=q))
                     for q in range(n_acc)]
    else:
        in_specs += [pl.BlockSpec((bn, k), functools.partial(lambda t, q: (acc_block(cur_j(t), q), 0), q=q))
                     for q in range(n_acc)]
    in_specs += [pl.BlockSpec((bm, w * bn), functools.partial(lambda t, off: (prev_i(t), prev_j(t) + off), off=off))
                 for (_, w, off) in extras]
    in_specs += [pl.BlockSpec((1, bn), lambda t: (0, prev_j(t))) for _ in consts]
    in_specs += [ANY_SPEC] * nd
    out_specs = [pl.BlockSpec((bm, w * bn), lambda t: (prev_i(t), prev_j(t))) for (_, _, w) in outs]
    out_specs += [pl.BlockSpec((1, w), lambda t: (0, 0)) for w in sums]
    out_shape = [jax.ShapeDtypeStruct((m, cols), dt) for (cols, dt, _) in outs]
    out_shape += [jax.ShapeDtypeStruct((1, w), F32) for w in sums]
    return pl.pallas_call(
        body, name=name, grid=(nt + 1,),
        in_specs=in_specs, out_specs=out_specs, out_shape=out_shape,
        scratch_shapes=[pltpu.VMEM((2 * n_acc, bm, bn), F32)],
        compiler_params=_params(("arbitrary",)),
    )(a, *[b] * n_acc, *[e[0] for e in extras], *consts, *deps)


def rowwise(name, fn, rows, consts, out_rows, out_sums, ts, alias=None, deps=()):
    rows = [r if isinstance(r, tuple) else (r, r.shape[1], 0) for r in rows]
    out_rows = [o if len(o) == 4 else (o[0], o[1], o[0], 0) for o in out_rows]
    s = rows[0][0].shape[0]
    nt = s // ts
    nr, nc, no, ns = len(rows), len(consts), len(out_rows), len(out_sums)
    in_specs = [pl.BlockSpec((ts, w), functools.partial(lambda i, cb: (i, cb), cb=cb)) for (_, w, cb) in rows]
    in_specs += [pl.BlockSpec(c.shape, lambda i: (0, 0)) for c in consts]
    operands = [r[0] for r in rows] + list(consts)
    aliases = {}
    if alias is not None:
        in_specs.append(pl.BlockSpec(memory_space=pl.ANY))
        operands.append(alias[0])
        aliases = {nr + nc: alias[1]}
    in_specs += [ANY_SPEC] * len(deps)
    operands += list(deps)
    out_shape = [jax.ShapeDtypeStruct((s, tw), dt) for (_, dt, tw, _) in out_rows]
    out_shape += [jax.ShapeDtypeStruct((1, w), F32) for w in out_sums]
    out_specs = [pl.BlockSpec((ts, w), functools.partial(lambda i, cb: (i, cb), cb=cb)) for (w, _, _, cb) in out_rows]
    out_specs += [pl.BlockSpec((1, w), lambda i: (0, 0)) for w in out_sums]
    n_in = len(operands)

    def body(*refs):
        ins, outs = refs[:nr + nc], refs[n_in:]
        i = pl.program_id(0)
        ro, so = fn(*[r[...] for r in ins])
        for q in range(no):
            outs[q][...] = ro[q].astype(outs[q].dtype)
        if ns:
            @pl.when(i == 0)
            def _():
                for q in range(ns):
                    outs[no + q][...] = jnp.zeros_like(outs[no + q])

            for q in range(ns):
                outs[no + q][...] += so[q]

    return pl.pallas_call(
        body, name=name, grid=(nt,),
        in_specs=in_specs, out_specs=out_specs, out_shape=out_shape, input_output_aliases=aliases,
        compiler_params=_params(("arbitrary",) if ns else ("parallel",)),
    )(*operands)


def _colsum(v):
    return jnp.sum(v, axis=0, keepdims=True)


def _rms_stats(xv):
    r = lax.rsqrt(jnp.mean(xv * xv, axis=-1, keepdims=True) + EPS)
    return r, xv * r


def _rms_bwd(dxhat, xhat, r):
    return r * (dxhat - xhat * jnp.mean(dxhat * xhat, axis=-1, keepdims=True))


def _gelu(v):
    k = math.sqrt(2.0 / math.pi)
    t = jnp.tanh(k * (v + 0.044715 * v * v * v))
    return 0.5 * v * (1.0 + t), t


def _gelu_grad(v, t):
    k = math.sqrt(2.0 / math.pi)
    return 0.5 * (1.0 + t) + 0.5 * v * (1.0 - t * t) * k * (1.0 + 3.0 * 0.044715 * v * v)


CONV_TS = 256
CONV_CH = 64


def _ln_fwd(yc, g, b):
    mu = jnp.mean(yc, axis=-1, keepdims=True)
    xc = yc - mu
    rstd = lax.rsqrt(jnp.mean(xc * xc, axis=-1, keepdims=True) + EPS)
    nhat = xc * rstd
    return nhat, rstd, nhat * g + b


SUBL = 8


def _shifted_copies(buf, sh, ts):
    for b in range(1, SUBL):
        sh[b - 1] = buf[pl.ds(b, ts + HALO - SUBL), :]


def _shifted(buf, sh, start):
    b = start % SUBL
    if b == 0:
        return buf[pl.ds(start, CONV_CH), :]
    return sh[b - 1, pl.ds(start - b, CONV_CH), :]


def conv_fwd(z, w32, cb, lg, lb):
    s = z.shape[0]
    ts = CONV_TS
    nt = s // ts
    hb = ts // HALO

    def body(a_ref, g_ref, ah_ref, gh_ref, w_ref, cb_ref, lg_ref, lb_ref, yc_ref, s_ref, ubuf, ush):
        i = pl.program_id(0)
        first = (i > 0).astype(F32)
        ubuf[0:HALO, :] = ah_ref[...] * _sig(gh_ref[...]) * first
        ubuf[HALO:HALO + ts, :] = a_ref[...] * _sig(g_ref[...])
        _shifted_copies(ubuf, ush, ts)
        for c0 in range(0, ts, CONV_CH):
            acc = jnp.zeros((CONV_CH, CW), F32)
            for k in range(KC):
                acc = acc + w_ref[k:k + 1, :] * _shifted(ubuf, ush, c0 + k + 2)
            yc = acc + cb_ref[...]
            yc_ref[c0:c0 + CONV_CH, :] = yc
            _, _, ln = _ln_fwd(yc, lg_ref[...], lb_ref[...])
            s_ref[c0:c0 + CONV_CH, :] = (ln * _sig(ln)).astype(s_ref.dtype)

    cur = lambda cbk: pl.BlockSpec((ts, CW), functools.partial(lambda i, q: (i, q), q=cbk))
    prev = lambda cbk: pl.BlockSpec((HALO, CW), functools.partial(lambda i, q: (jnp.maximum(i * hb - 1, 0), q), q=cbk))
    const = lambda a: pl.BlockSpec(a.shape, lambda i: (0, 0))
    return pl.pallas_call(
        body, name="conv_fwd", grid=(nt,),
        in_specs=[cur(ZB_A), cur(ZB_G), prev(ZB_A), prev(ZB_G), const(w32), const(cb), const(lg), const(lb)],
        out_specs=[pl.BlockSpec((ts, CW), lambda i: (i, 0)), pl.BlockSpec((ts, CW), lambda i: (i, 0))],
        out_shape=[jax.ShapeDtypeStruct((s, CW), F32), jax.ShapeDtypeStruct((s, CW), BF16)],
        scratch_shapes=[pltpu.VMEM((HALO + ts, CW), F32), pltpu.VMEM((SUBL - 1, ts + HALO - SUBL, CW), F32)],
        compiler_params=_params(("parallel",)),
    )(z, z, z, z, w32, cb, lg, lb)


def conv_bwd(ds, yc, z, w32, lg, lb, dz):
    s = z.shape[0]
    ts = CONV_TS
    nt = s // ts
    hb = ts // HALO
    last_hb = s // HALO - 1

    def ln_bwd(dsv, ycv, g, b):
        nhat, rstd, ln = _ln_fwd(ycv, g, b)
        sg = _sig(ln)
        dln = dsv * (sg * (1.0 + ln * (1.0 - sg)))
        dnh = dln * g
        dyc = rstd * (dnh - jnp.mean(dnh, axis=-1, keepdims=True)
                      - nhat * jnp.mean(dnh * nhat, axis=-1, keepdims=True))
        return dyc, dln, nhat

    def body(ds_ref, yc_ref, dsn_ref, ycn_ref, a_ref, g_ref, ah_ref, gh_ref, w_ref, lg_ref, lb_ref, dz_in,
             dz_ref, dlg_ref, dlb_ref, dcb_ref, dw_ref, dbuf, ubuf, dsh, ush):
        i = pl.program_id(0)

        @pl.when(i == 0)
        def _():
            dlg_ref[...] = jnp.zeros_like(dlg_ref)
            dlb_ref[...] = jnp.zeros_like(dlb_ref)
            dcb_ref[...] = jnp.zeros_like(dcb_ref)
            dw_ref[...] = jnp.zeros_like(dw_ref)

        lg, lb = lg_ref[...], lb_ref[...]
        dyc, dln, nhat = ln_bwd(ds_ref[...], yc_ref[...], lg, lb)
        dlg_ref[...] += _colsum(dln * nhat)
        dlb_ref[...] += _colsum(dln)
        dcb_ref[...] += _colsum(dyc)
        dbuf[0:ts, :] = dyc
        nxt = (i < nt - 1).astype(F32)
        dbuf[ts:ts + HALO, :] = ln_bwd(dsn_ref[...], ycn_ref[...], lg, lb)[0] * nxt
        first = (i > 0).astype(F32)
        ubuf[0:HALO, :] = ah_ref[...] * _sig(gh_ref[...]) * first
        ubuf[HALO:HALO + ts, :] = a_ref[...] * _sig(g_ref[...])
        _shifted_copies(dbuf, dsh, ts)
        _shifted_copies(ubuf, ush, ts)
        for c0 in range(0, ts, CONV_CH):
            du = jnp.zeros((CONV_CH, CW), F32)
            dyc_c = dbuf[c0:c0 + CONV_CH, :]
            for k in range(KC):
                du = du + w_ref[k:k + 1, :] * _shifted(dbuf, dsh, c0 + KC - 1 - k)
                dw_ref[k:k + 1, :] += _colsum(dyc_c * _shifted(ubuf, ush, c0 + k + 2))
            av = a_ref[c0:c0 + CONV_CH, :]
            sg = _sig(g_ref[c0:c0 + CONV_CH, :])
            dz_ref[c0:c0 + CONV_CH, 0:CW] = (du * sg).astype(dz_ref.dtype)
            dz_ref[c0:c0 + CONV_CH, CW:2 * CW] = (du * av * sg * (1.0 - sg)).astype(dz_ref.dtype)

    cur = lambda w, cbk: pl.BlockSpec((ts, w), functools.partial(lambda i, q: (i, q), q=cbk))
    prev = lambda cbk: pl.BlockSpec((HALO, CW), functools.partial(lambda i, q: (jnp.maximum(i * hb - 1, 0), q), q=cbk))
    nxt_spec = pl.BlockSpec((HALO, CW), lambda i: (jnp.minimum((i + 1) * hb, last_hb), 0))
    const = lambda a: pl.BlockSpec(a.shape, lambda i: (0, 0))
    acc = lambda r: pl.BlockSpec((r, CW), lambda i: (0, 0))
    return pl.pallas_call(
        body, name="conv_bwd", grid=(nt,),
        in_specs=[cur(CW, 0), cur(CW, 0), nxt_spec, nxt_spec, cur(CW, ZB_A), cur(CW, ZB_G), prev(ZB_A), prev(ZB_G),
                  const(w32), const(lg), const(lb), pl.BlockSpec(memory_space=pl.ANY)],
        out_specs=[pl.BlockSpec((ts, 2 * CW), lambda i: (i, ZB_A // 2)), acc(1), acc(1), acc(1), acc(HALO)],
        out_shape=[jax.ShapeDtypeStruct(dz.shape, dz.dtype), jax.ShapeDtypeStruct((1, CW), F32),
                   jax.ShapeDtypeStruct((1, CW), F32), jax.ShapeDtypeStruct((1, CW), F32),
                   jax.ShapeDtypeStruct((HALO, CW), F32)],
        scratch_shapes=[pltpu.VMEM((ts + HALO, CW), F32), pltpu.VMEM((HALO + ts, CW), F32)]
        + [pltpu.VMEM((SUBL - 1, ts + HALO - SUBL, CW), F32)] * 2,
        input_output_aliases={11: 0},
        compiler_params=_params(("arbitrary",)),
    )(ds, yc, ds, yc, z, z, z, z, w32, lg, lb, dz)


SSM_TS = 512
GRP = 8


def _cmul(ar, ai, br, bi):
    return ar * br - ai * bi, ar * bi + ai * br


def _scan_tables(ar, ai, reverse):
    n = ar.shape[1]
    row = lax.broadcasted_iota(jnp.int32, (GRP, n), 0)
    dist = (GRP - 1 - row) if reverse else row
    one_r = jnp.broadcast_to(ar, (GRP, n))
    one_i = jnp.broadcast_to(ai, (GRP, n))
    p2r, p2i = _cmul(one_r, one_i, one_r, one_i)
    p4r, p4i = _cmul(p2r, p2i, p2r, p2i)
    steps = []
    for sft, (pr, pi) in ((1, (one_r, one_i)), (2, (p2r, p2i)), (4, (p4r, p4i))):
        keep = dist >= sft
        steps.append((jnp.where(keep, pr, 0.0), jnp.where(keep, pi, 0.0)))
    cr, ci = one_r, one_i
    accr, acci = one_r, one_i
    for e in range(1, GRP):
        cr, ci = _cmul(cr, ci, one_r, one_i)
        accr = jnp.where(dist == e, cr, accr)
        acci = jnp.where(dist == e, ci, acci)
    return steps, (accr, acci)


def _scan_group(xr, xi, steps, carry_tab, cr, ci, reverse):
    for sft, (tr, ti) in zip((1, 2, 4), steps):
        amt = (GRP - sft) if reverse else sft
        sr = pltpu.roll(xr, amt, 0)
        si = pltpu.roll(xi, amt, 0)
        xr, xi = xr + tr * sr - ti * si, xi + tr * si + ti * sr
    pr, pi = carry_tab
    xr = xr + pr * cr - pi * ci
    xi = xi + pr * ci + pi * cr
    return xr, xi


def ssm_fwd(z, wb_re, wb_im, wc, e_re, e_im, dvec):
    s = z.shape[0]
    ts = SSM_TS
    nt = s // ts
    ucol0 = ZB_U * CW // CB

    def body(u_ref, wbr_ref, wbi_ref, wc_ref, er_ref, ei_ref, d_ref, xr_ref, xi_ref, y_ref, gl_ref, car_r, car_i):
        i = pl.program_id(1)

        @pl.when(i == 0)
        def _():
            car_r[...] = jnp.zeros_like(car_r)
            car_i[...] = jnp.zeros_like(car_i)

        u = u_ref[...]
        ub = u.astype(BF16)
        xr_ref[...] = jnp.dot(ub, wbr_ref[0], preferred_element_type=F32)
        xi_ref[...] = jnp.dot(ub, wbi_ref[0], preferred_element_type=F32)
        steps, ctab = _scan_tables(er_ref[0], ei_ref[0], False)

        def grp(r, carry):
            cr, ci = carry
            r0 = pl.multiple_of(r * GRP, GRP)
            xr, xi = _scan_group(xr_ref[pl.ds(r0, GRP), :], xi_ref[pl.ds(r0, GRP), :], steps, ctab, cr, ci, False)
            xr_ref[pl.ds(r0, GRP), :] = xr
            xi_ref[pl.ds(r0, GRP), :] = xi
            return (jnp.broadcast_to(xr[GRP - 1:GRP, :], (GRP, SB)), jnp.broadcast_to(xi[GRP - 1:GRP, :], (GRP, SB)))

        cr, ci = lax.fori_loop(0, ts // GRP, grp, (car_r[...], car_i[...]))
        car_r[...] = cr
        car_i[...] = ci
        y = (jnp.dot(xr_ref[...].astype(BF16), wc_ref[0, 0:SB, :], preferred_element_type=F32)
             + jnp.dot(xi_ref[...].astype(BF16), wc_ref[0, SB:2 * SB, :], preferred_element_type=F32)
             + d_ref[0] * u)
        y_ref[...] = y
        gl_ref[...] = _gelu(y)[0].astype(gl_ref.dtype)

    blk3 = lambda a: pl.BlockSpec((1,) + a.shape[1:], lambda j, i: (j, 0, 0))
    return pl.pallas_call(
        body, name="ssm_fwd", grid=(NBLK, nt),
        in_specs=[pl.BlockSpec((ts, CB), lambda j, i: (i, ucol0 + j)),
                  blk3(wb_re), blk3(wb_im), blk3(wc), blk3(e_re), blk3(e_im), blk3(dvec)],
        out_specs=[pl.BlockSpec((ts, SB), lambda j, i: (i, j)), pl.BlockSpec((ts, SB), lambda j, i: (i, j)),
                   pl.BlockSpec((ts, CB), lambda j, i: (i, j)), pl.BlockSpec((ts, CB), lambda j, i: (i, j))],
        out_shape=[jax.ShapeDtypeStruct((s, NST), F32), jax.ShapeDtypeStruct((s, NST), F32),
                   jax.ShapeDtypeStruct((s, SW), F32), jax.ShapeDtypeStruct((s, SW), BF16)],
        scratch_shapes=[pltpu.VMEM((GRP, SB), F32), pltpu.VMEM((GRP, SB), F32)],
        compiler_params=_params(("parallel", "arbitrary")),
    )(z, wb_re, wb_im, wc, e_re, e_im, dvec)


def ssm_bwd(dgl, ypre, z, xs_re, xs_im, wbt_re, wbt_im, wct, e_re, e_im, dvec, dz):
    s = z.shape[0]
    ts = SSM_TS
    nt = s // ts
    ucol0 = ZB_U * CW // CB
    tn_dims = (((0,), (0,)), ((), ()))

    def body(dgl_ref, y_ref, u_ref, xr_ref, xi_ref, wbtr_ref, wbti_ref, wct_ref, er_ref, ei_ref, d_ref, dz_in,
             du_ref, dd_ref, dar_ref, dai_ref, dwbr_ref, dwbi_ref, dwc_ref,
             lr_ref, li_ref, car_r, car_i, acc_r, acc_i):
        i = pl.program_id(1)

        @pl.when(i == 0)
        def _():
            for ref in (car_r, car_i, acc_r, acc_i, dd_ref, dwbr_ref, dwbi_ref, dwc_ref):
                ref[...] = jnp.zeros_like(ref)

        u = u_ref[...]
        y = y_ref[...]
        dy = dgl_ref[...] * _gelu_grad(y, _gelu(y)[1])
        dd_ref[0] += _colsum(dy * u)
        dyb = dy.astype(BF16)
        dxo = jnp.dot(dyb, wct_ref[0], preferred_element_type=F32)
        lr_ref[...] = dxo[:, 0:SB]
        li_ref[...] = dxo[:, SB:2 * SB]
        steps, ctab = _scan_tables(er_ref[0], -ei_ref[0], True)
        row = lax.broadcasted_iota(jnp.int32, (GRP, SB), 0)

        def grp(q, carry):
            cr, ci, ar, ai = carry
            r0 = pl.multiple_of((ts // GRP - 1 - q) * GRP, GRP)
            lr, li = _scan_group(lr_ref[pl.ds(r0, GRP), :], li_ref[pl.ds(r0, GRP), :], steps, ctab, cr, ci, True)
            lr_ref[pl.ds(r0, GRP), :] = lr
            li_ref[pl.ds(r0, GRP), :] = li
            nr = jnp.where(row == GRP - 1, cr, pltpu.roll(lr, GRP - 1, 0))
            ni = jnp.where(row == GRP - 1, ci, pltpu.roll(li, GRP - 1, 0))
            xr = xr_ref[pl.ds(r0, GRP), :]
            xi = xi_ref[pl.ds(r0, GRP), :]
            ar = ar + nr * xr + ni * xi
            ai = ai + ni * xr - nr * xi
            return (jnp.broadcast_to(lr[0:1, :], (GRP, SB)), jnp.broadcast_to(li[0:1, :], (GRP, SB)), ar, ai)

        cr, ci, ar, ai = lax.fori_loop(0, ts // GRP, grp, (car_r[...], car_i[...], acc_r[...], acc_i[...]))
        car_r[...] = cr
        car_i[...] = ci
        acc_r[...] = ar
        acc_i[...] = ai

        @pl.when(i == nt - 1)
        def _():
            dar_ref[0] = _colsum(ar)
            dai_ref[0] = _colsum(ai)

        lrb = lr_ref[...].astype(BF16)
        lib = li_ref[...].astype(BF16)
        du = (jnp.dot(lrb, wbtr_ref[0], preferred_element_type=F32)
              + jnp.dot(lib, wbti_ref[0], preferred_element_type=F32) + d_ref[0] * dy)
        du_ref[...] = du.astype(du_ref.dtype)
        ub = u.astype(BF16)
        dwbr_ref[0] += lax.dot_general(ub, lrb, tn_dims, preferred_element_type=F32)
        dwbi_ref[0] += lax.dot_general(ub, lib, tn_dims, preferred_element_type=F32)
        dwc_ref[0, 0:SB, :] += lax.dot_general(xr_ref[...].astype(BF16), dyb, tn_dims, preferred_element_type=F32)
        dwc_ref[0, SB:2 * SB, :] += lax.dot_general(xi_ref[...].astype(BF16), dyb, tn_dims, preferred_element_type=F32)

    rev = lambda i: nt - 1 - i
    blk3 = lambda a: pl.BlockSpec((1,) + a.shape[1:], lambda j, i: (j, 0, 0))
    acc3 = lambda r, c: pl.BlockSpec((1, r, c), lambda j, i: (j, 0, 0))
    return pl.pallas_call(
        body, name="ssm_bwd", grid=(NBLK, nt),
        in_specs=[pl.BlockSpec((ts, CB), lambda j, i: (rev(i), j)), pl.BlockSpec((ts, CB), lambda j, i: (rev(i), j)),
                  pl.BlockSpec((ts, CB), lambda j, i: (rev(i), ucol0 + j)),
                  pl.BlockSpec((ts, SB), lambda j, i: (rev(i), j)), pl.BlockSpec((ts, SB), lambda j, i: (rev(i), j)),
                  blk3(wbt_re), blk3(wbt_im), blk3(wct), blk3(e_re), blk3(e_im), blk3(dvec),
                  pl.BlockSpec(memory_space=pl.ANY)],
        out_specs=[pl.BlockSpec((ts, CB), lambda j, i: (rev(i), ucol0 + j)),
                   acc3(1, CB), acc3(1, SB), acc3(1, SB), acc3(CB, SB), acc3(CB, SB), acc3(2 * SB, CB)],
        out_shape=[jax.ShapeDtypeStruct(dz.shape, dz.dtype),
                   jax.ShapeDtypeStruct((NBLK, 1, CB), F32),
                   jax.ShapeDtypeStruct((NBLK, 1, SB), F32), jax.ShapeDtypeStruct((NBLK, 1, SB), F32),
                   jax.ShapeDtypeStruct((NBLK, CB, SB), F32), jax.ShapeDtypeStruct((NBLK, CB, SB), F32),
                   jax.ShapeDtypeStruct((NBLK, 2 * SB, CB), F32)],
        scratch_shapes=[pltpu.VMEM((ts, SB), F32), pltpu.VMEM((ts, SB), F32)] + [pltpu.VMEM((GRP, SB), F32)] * 4,
        input_output_aliases={11: 0},
        compiler_params=_params(("parallel", "arbitrary")),
    )(dgl, ypre, z, xs_re, xs_im, wbt_re, wbt_im, wct, e_re, e_im, dvec, dz)


def _disc(a_re, a_im, log_dt, b_re, b_im, expand):
    dt = jnp.dot(expand, jnp.exp(log_dt), preferred_element_type=F32, precision=lax.Precision.HIGHEST)
    mag = jnp.exp(dt * a_re)
    e_re, e_im = mag * jnp.cos(dt * a_im), mag * jnp.sin(dt * a_im)
    n_re, n_im = e_re - 1.0, e_im
    den = a_re * a_re + a_im * a_im
    q_re = (n_re * a_re + n_im * a_im) / den
    q_im = (n_im * a_re - n_re * a_im) / den
    return e_re, e_im, q_re * b_re - q_im * b_im, q_re * b_im + q_im * b_re


def _whole(a):
    return pl.BlockSpec(a.shape, functools.partial(lambda n: (0,) * n, n=a.ndim))


def disc_fwd(a_re, a_im, log_dt, b_re, b_im, expand):
    def body(ar, ai, ld, br, bi, ex, er_o, ei_o, bbr_o, bbi_o):
        er, ei, bbr, bbi = _disc(ar[...], ai[...], ld[...], br[...], bi[...], ex[...])
        er_o[...] = er
        ei_o[...] = ei
        bbr_o[...] = bbr
        bbi_o[...] = bbi

    ins = (a_re, a_im, log_dt, b_re, b_im, expand)
    outs = [jax.ShapeDtypeStruct(a_re.shape, F32)] * 2 + [jax.ShapeDtypeStruct(b_re.shape, F32)] * 2
    return pl.pallas_call(body, name="disc_fwd", in_specs=[_whole(a) for a in ins],
                          out_specs=[_whole(o) for o in outs], out_shape=outs, compiler_params=_params())(*ins)


def disc_bwd(a_re, a_im, log_dt, b_re, b_im, expand, de_re, de_im, dbb_re, dbb_im):
    def body(ar, ai, ld, br, bi, ex, der, dei, dbr, dbi, o_ar, o_ai, o_ld, o_br, o_bi):
        exv = ex[...]
        _, vjp = jax.vjp(lambda *p: _disc(*p, exv), ar[...], ai[...], ld[...], br[...], bi[...])
        g = vjp((der[...], dei[...], dbr[...], dbi[...]))
        for o, v in zip((o_ar, o_ai, o_ld, o_br, o_bi), g):
            o[...] = v

    ins = (a_re, a_im, log_dt, b_re, b_im, expand, de_re, de_im, dbb_re, dbb_im)
    outs = [jax.ShapeDtypeStruct(a.shape, F32) for a in (a_re, a_im, log_dt, b_re, b_im)]
    return pl.pallas_call(body, name="disc_bwd", in_specs=[_whole(a) for a in ins],
                          out_specs=[_whole(o) for o in outs], out_shape=outs, compiler_params=_params())(*ins)


def mod_fwd(c_all, w_ada, b_cols):
    def body(c_ref, w_ref, b_ref, act_ref, mod_ref):
        cv = c_ref[...]
        act = cv * _sig(cv)
        act_ref[...] = act
        mod_ref[...] = jnp.dot(act, w_ref[...], preferred_element_type=F32, precision=lax.Precision.HIGHEST) + b_ref[...]

    ins = (c_all, w_ada, b_cols)
    outs = [jax.ShapeDtypeStruct(c_all.shape, F32), jax.ShapeDtypeStruct((NDEV, w_ada.shape[1]), F32)]
    return pl.pallas_call(body, name="mod_fwd", in_specs=[_whole(a) for a in ins],
                          out_specs=[_whole(o) for o in outs], out_shape=outs, compiler_params=_params())(*ins)


def ada_grad(act_all, dmod_cols):
    def body(a_ref, d_ref, o_ref):
        o_ref[...] = lax.dot_general(a_ref[...], d_ref[...], (((0,), (0,)), ((), ())),
                                     preferred_element_type=F32, precision=lax.Precision.HIGHEST)

    out = jax.ShapeDtypeStruct((act_all.shape[1], dmod_cols.shape[1]), F32)
    return pl.pallas_call(body, name="ada_grad", in_specs=[_whole(act_all), _whole(dmod_cols)],
                          out_specs=_whole(out), out_shape=out, compiler_params=_params())(act_all, dmod_cols)


def _adam_math(w, g, m, v):
    m2 = ADAM_B1 * m + (1.0 - ADAM_B1) * g
    v2 = ADAM_B2 * v + (1.0 - ADAM_B2) * (g * g)
    m_hat = m2 / (1.0 - ADAM_B1 ** ADAM_STEP)
    v_hat = v2 / (1.0 - ADAM_B2 ** ADAM_STEP)
    delta = -ADAM_LR * (m_hat / (jnp.sqrt(v_hat) + ADAM_EPS) + ADAM_WD * w)
    return delta, m2, v2


def adam(name, w, g, m, v):
    r, c = w.shape
    tr = max(t for t in range(8, min(r, 512) + 1, 8) if r % t == 0)

    def body(w_ref, g_ref, m_ref, v_ref, d_o, m_o, v_o):
        d, m2, v2 = _adam_math(w_ref[...], g_ref[...], m_ref[...], v_ref[...])
        d_o[...] = d
        m_o[...] = m2
        v_o[...] = v2

    spec = pl.BlockSpec((tr, c), lambda i: (i, 0))
    out = jax.ShapeDtypeStruct((r, c), F32)
    return pl.pallas_call(body, name=name, grid=(r // tr,), in_specs=[spec] * 4, out_specs=[spec] * 3,
                          out_shape=[out] * 3, compiler_params=_params(("parallel",)))(w, g, m, v)


def adam_many(name, ws, gs, ms, vs):
    n = len(ws)

    def body(*refs):
        ins, outs = refs[:4 * n], refs[4 * n:]
        for q in range(n):
            d, m2, v2 = _adam_math(ins[q][...], ins[n + q][...], ins[2 * n + q][...], ins[3 * n + q][...])
            outs[q][...] = d
            outs[n + q][...] = m2
            outs[2 * n + q][...] = v2

    operands = list(ws) + list(gs) + list(ms) + list(vs)
    outs = [jax.ShapeDtypeStruct(w.shape, F32) for w in ws] * 3
    return pl.pallas_call(body, name=name, in_specs=[_whole(a) for a in operands],
                          out_specs=[_whole(o) for o in outs], out_shape=outs, compiler_params=_params())(*operands)


def _rows_tile(r, most):
    best = None
    for t in range(16, min(r, most) + 1, 16):
        if r % t == 0:
            best = t
    assert best is not None, r
    return best


def sum_slots(name, slots, out_dtype=F32):
    n, r, c = slots.shape
    tr = _rows_tile(r, max(16, (2 * 1024 * 1024) // (n * c)))

    def body(s_ref, o_ref):
        acc = s_ref[0].astype(F32)
        for q in range(1, n):
            acc = acc + s_ref[q].astype(F32)
        o_ref[...] = acc.astype(o_ref.dtype)

    return pl.pallas_call(body, name=name, grid=(r // tr,),
                          in_specs=[pl.BlockSpec((n, tr, c), lambda i: (0, i, 0))],
                          out_specs=pl.BlockSpec((tr, c), lambda i: (i, 0)),
                          out_shape=jax.ShapeDtypeStruct((r, c), out_dtype), compiler_params=_params(("parallel",)))(slots)


HBM_SPEC = pl.BlockSpec(memory_space=pltpu.HBM)


def _coords():
    return lax.axis_index("x"), lax.axis_index("y"), lax.axis_index("c")


def _linear(x, y, c):
    return 4 * x + 2 * y + c


def all_gather(name, shards, deps=()):
    nq, nd = len(shards), len(deps)

    def body(*refs):
        xs, outs = refs[:nq], refs[nq + nd:2 * nq + nd]
        send_sems, recv_sems, local_sems = refs[2 * nq + nd:2 * nq + nd + 3]
        bufs = refs[2 * nq + nd + 3:]
        x, y, cc = _coords()
        me, sibling = (x, y, cc), (x, y, 1 - cc)
        chips = [(1 - x, y), (x, 1 - y), (1 - x, 1 - y)]

        def slot(q, px, py, pc):
            return outs[q].at[_linear(px, py, pc)]

        def copy(q, k, block, to, src=None):
            return pltpu.make_async_remote_copy(
                src_ref=slot(q, *block) if src is None else src, dst_ref=slot(q, *block),
                send_sem=send_sems.at[7 * q + k], recv_sem=recv_sems.at[7 * q + k], device_id=to, device_id_type=MESH)

        loads = [pltpu.make_async_copy(xs[q], bufs[q], local_sems.at[q]) for q in range(nq)]
        for cp in loads:
            cp.start()
        for cp in loads:
            cp.wait()
        mine = [pltpu.make_async_copy(bufs[q], slot(q, *me), local_sems.at[q]) for q in range(nq)]
        first = []
        for q in range(nq):
            first.append(copy(q, 0, me, sibling, src=bufs[q]))
            first += [copy(q, 1 + j, me, (*chip, cc), src=bufs[q]) for j, chip in enumerate(chips)]
        for cp in mine + first:
            cp.start()
        passed = []
        for q in range(nq):
            for j, chip in enumerate(chips):
                copy(q, 1 + j, (*chip, cc), me).wait_recv()
                passed.append(copy(q, 4 + j, (*chip, cc), sibling))
                passed[-1].start()
        for q in range(nq):
            copy(q, 0, sibling, me).wait_recv()
            for j, chip in enumerate(chips):
                copy(q, 4 + j, (*chip, 1 - cc), me).wait_recv()
        for cp in first + passed:
            cp.wait_send()
        for cp in mine:
            cp.wait()

    return pl.pallas_call(
        body, name=name, in_specs=[HBM_SPEC] * nq + [ANY_SPEC] * nd, out_specs=[HBM_SPEC] * nq,
        out_shape=[jax.ShapeDtypeStruct((NDEV,) + s.shape, s.dtype) for s in shards],
        scratch_shapes=[pltpu.SemaphoreType.DMA((7 * nq,)), pltpu.SemaphoreType.DMA((7 * nq,)),
                        pltpu.SemaphoreType.DMA((nq,))] + [pltpu.VMEM(s.shape, s.dtype) for s in shards],
    )(*shards, *deps)


NCHIP = 4


SEM_SPEC = pl.BlockSpec(memory_space=pltpu.SEMAPHORE)
EFFECT = pltpu.SideEffectType.DATAFLOW_SIDE_EFFECTING


def _peer(x, y, cc, k):
    fx, fy, fc = (k >> 2) & 1, (k >> 1) & 1, k & 1
    return (x + fx - 2 * fx * x, y + fy - 2 * fy * y, cc + fc - 2 * fc * cc)


def gather_plan(srcs, lands, coords):
    x, y, cc = coords
    me = _linear(x, y, cc)
    return [(s, l.at[me], _peer(x, y, cc, k)) for s, l in zip(srcs, lands) for k in range(1, NDEV)]


def near_plan(srcs, lands, coords):
    x, y, cc = coords
    me = _linear(x, y, cc)
    peers = [(x, y, 1 - cc)] + [_peer(x, y, cc, 2 * k) for k in range(1, NCHIP)]
    return [(s, l.at[me], p) for s, l in zip(srcs, lands) for p in peers]


def pass_on_plan(srcs, lands, coords):
    x, y, cc = coords
    out = []
    for l in srcs:
        for k in range(1, NCHIP):
            px, py, _ = _peer(x, y, cc, 2 * k)
            slot = _linear(px, py, cc)
            out.append((l.at[slot], l.at[slot], (x, y, 1 - cc)))
    return out


def pair_plan(srcs, lands, coords):
    x, y, cc = coords
    return [(s.at[2 * chip + 1 - cc], l.at[chip], (x, y, 1 - cc)) for s, l in zip(srcs, lands) for chip in range(NCHIP)]


def chip_plan(srcs, lands, coords):
    x, y, cc = coords
    out = []
    for s, l in zip(srcs, lands):
        for k in range(1, NCHIP):
            px, py, _ = _peer(x, y, cc, 2 * k)
            out.append((s.at[2 * px + py], l.at[k - 1], (px, py, cc)))
    return out


def _remote(copy, i, send_sems, recv_sems):
    src, dst, dev = copy
    return pltpu.make_async_remote_copy(src_ref=src, dst_ref=dst, send_sem=send_sems.at[i], recv_sem=recv_sems.at[i],
                                        device_id=dev, device_id_type=MESH)


def exchange_start(name, plan, ncopy, srcs, land_shapes, deps=()):
    ns, nl, nd = len(srcs), len(land_shapes), len(deps)

    def body(*refs):
        s, l = refs[:ns], refs[ns:ns + nl]
        send_sems, recv_sems = refs[ns + nl + nd], refs[ns + nl + nd + 1]
        token = refs[-1]
        for i, cp in enumerate(plan(s, l, _coords())):
            _remote(cp, i, send_sems, recv_sems).start()
        token[...] = jnp.zeros_like(token)

    hbm = lambda a: pltpu.with_memory_space_constraint(a, pltpu.HBM)
    lands = [lax.empty(shp, dt) for shp, dt in land_shapes]
    thru = [pltpu.HBM(a.shape, a.dtype) for a in list(srcs) + lands]
    outs = pl.pallas_call(
        body, name=name,
        in_specs=[HBM_SPEC] * (ns + nl) + [ANY_SPEC] * nd,
        out_specs=(SEM_SPEC, SEM_SPEC, *[HBM_SPEC] * (ns + nl), pl.BlockSpec(memory_space=pltpu.VMEM)),
        out_shape=(pltpu.SemaphoreType.DMA((ncopy,)), pltpu.SemaphoreType.DMA((ncopy,)), *thru,
                   jax.ShapeDtypeStruct((8, LANE), F32)),
        input_output_aliases={i: 2 + i for i in range(ns + nl)},
        compiler_params=pltpu.CompilerParams(has_side_effects=EFFECT),
    )(*[hbm(a) for a in srcs], *[hbm(a) for a in lands], *deps)
    return outs[0], outs[1], list(outs[2:2 + ns]), list(outs[2 + ns:2 + ns + nl]), outs[-1]


def exchange_wait(name, plan, started, after, place_own=False):
    send_sems, recv_sems, srcs, lands, _ = started
    ns, nl = len(srcs), len(lands)

    def body(*refs):
        s, l = refs[:ns], refs[ns:ns + nl]
        send_sems, recv_sems = refs[ns + nl], refs[ns + nl + 1]
        l_out = refs[2 * ns + nl + 3:2 * ns + 2 * nl + 3]
        scratch = refs[2 * ns + 2 * nl + 3:]
        copies = [_remote(cp, i, send_sems, recv_sems) for i, cp in enumerate(plan(s, l, _coords()))]
        if place_own:
            me = _linear(*_coords())
            local_sems, bufs = scratch[0], scratch[1:]
            loads = [pltpu.make_async_copy(s[q], bufs[q], local_sems.at[q]) for q in range(ns)]
            for cp in loads:
                cp.start()
            for cp in loads:
                cp.wait()
            stores = [pltpu.make_async_copy(bufs[q], l_out[q].at[me], local_sems.at[q]) for q in range(ns)]
            for cp in stores:
                cp.start()
        for cp in copies:
            cp.wait_recv()
        for cp in copies:
            cp.wait_send()
        if place_own:
            for cp in stores:
                cp.wait()

    scratch_shapes = []
    if place_own:
        scratch_shapes = [pltpu.SemaphoreType.DMA((ns,))] + [pltpu.VMEM(a.shape, a.dtype) for a in srcs]
    outs = pl.pallas_call(
        body, name=name,
        in_specs=[HBM_SPEC] * (ns + nl) + [SEM_SPEC, SEM_SPEC, ANY_SPEC],
        out_specs=[HBM_SPEC] * (ns + nl),
        out_shape=[pltpu.HBM(a.shape, a.dtype) for a in srcs + lands],
        input_output_aliases={i: i for i in range(ns + nl)},
        scratch_shapes=scratch_shapes,
        compiler_params=pltpu.CompilerParams(has_side_effects=EFFECT),
    )(*srcs, *lands, send_sems, recv_sems, after)
    return list(outs[:ns]), list(outs[ns:])


def pair_sum(name, g, recv):
    _, r, c = g.shape
    tr = _rows_tile(r, 512)

    def body(g_ref, r_ref, o_ref):
        own = jnp.where(lax.axis_index("c") == 0, g_ref[0, 0], g_ref[0, 1])
        o_ref[0] = (own.astype(F32) + r_ref[0].astype(F32)).astype(o_ref.dtype)

    return pl.pallas_call(
        body, name=name, grid=(NCHIP, r // tr),
        in_specs=[pl.BlockSpec((1, 2, tr, c), lambda k, i: (k, 0, i, 0)), pl.BlockSpec((1, tr, c), lambda k, i: (k, i, 0))],
        out_specs=pl.BlockSpec((1, tr, c), lambda k, i: (k, i, 0)),
        out_shape=jax.ShapeDtypeStruct((NCHIP, r, c), g.dtype), compiler_params=_params(("parallel", "parallel")),
    )(g.reshape(NCHIP, 2, r, c), recv)


def chip_sum(name, partial, recv):
    _, r, c = partial.shape
    tr = _rows_tile(r, 512)

    def body(p_ref, r_ref, o_ref):
        chip = 2 * lax.axis_index("x") + lax.axis_index("y")
        own = p_ref[0]
        for k in range(1, NCHIP):
            own = jnp.where(chip == k, p_ref[k], own)
        acc = own.astype(F32)
        for k in range(NCHIP - 1):
            acc = acc + r_ref[k].astype(F32)
        o_ref[...] = acc

    return pl.pallas_call(
        body, name=name, grid=(r // tr,),
        in_specs=[pl.BlockSpec((NCHIP, tr, c), lambda i: (0, i, 0)), pl.BlockSpec((NCHIP - 1, tr, c), lambda i: (0, i, 0))],
        out_specs=pl.BlockSpec((tr, c), lambda i: (i, 0)),
        out_shape=jax.ShapeDtypeStruct((r, c), F32), compiler_params=_params(("parallel",)),
    )(partial, recv)


def _block_diag(w, rows_per, cols_per):
    w = w.reshape(NBLK, 8, rows_per, cols_per)
    eye = jnp.eye(8, dtype=w.dtype)
    out = w[:, :, :, None, :] * eye[None, :, None, :, None]
    return out.reshape(NBLK, 8 * rows_per, 8 * cols_per)


def _diag_blocks(wd, rows_per, cols_per):
    wd = wd.reshape(NBLK, 8, rows_per, 8, cols_per)
    idx = jnp.arange(8)
    return wd[:, idx, :, idx, :].transpose(1, 0, 2, 3).reshape(NG, rows_per, cols_per)


def _pad_rows(v, mult):
    n = v.shape[0]
    return jnp.pad(v, (0, (-n) % mult))


def kernel(x, c, w_ada, b_ada, norm1_g, w_in, conv_w, conv_b, conv_ln_g, conv_ln_b, conv_proj, ssm_a_re, ssm_a_im, ssm_b_re, ssm_b_im, ssm_c_re, ssm_c_im, ssm_d, ssm_log_dt, ssm_glu, w_out, norm2_g, w_ffn_in, w_ffn_out, final_g, loss_target, m_w_ada, m_b_ada, m_norm1_g, m_w_in, m_conv_w, m_conv_b, m_conv_ln_g, m_conv_ln_b, m_conv_proj, m_ssm_a_re, m_ssm_a_im, m_ssm_b_re, m_ssm_b_im, m_ssm_c_re, m_ssm_c_im, m_ssm_d, m_ssm_log_dt, m_ssm_glu, m_w_out, m_norm2_g, m_w_ffn_in, m_w_ffn_out, m_final_g, v_w_ada, v_b_ada, v_norm1_g, v_w_in, v_conv_w, v_conv_b, v_conv_ln_g, v_conv_ln_b, v_conv_proj, v_ssm_a_re, v_ssm_a_im, v_ssm_b_re, v_ssm_b_im, v_ssm_c_re, v_ssm_c_im, v_ssm_d, v_ssm_log_dt, v_ssm_glu, v_w_out, v_norm2_g, v_w_ffn_in, v_w_ffn_out, v_final_g):
    me = _linear(*_coords())
    xs = x[0]
    tgt = loss_target[0]
    seq = xs.shape[0]

    flat = lambda g: g.reshape(NDEV * g.shape[1], g.shape[2])
    w_in_s = w_in[0].T.astype(BF16)
    mids = [p.astype(BF16) for p in (conv_proj[0].T, ssm_glu[0].T, w_out[0])]
    ffns = [p.astype(BF16) for p in (w_ffn_in[0].T, w_ffn_out[0])]
    zone = lambda p: ((NDEV,) + p.shape, p.dtype)
    in_go = exchange_start("gather_in_start", near_plan, NCHIP, [w_in_s], [zone(w_in_s)])
    c_all, cw_g = all_gather("gather_c_conv_w", [c, conv_w[0]], deps=[in_go[4]])
    mids_go = exchange_start("gather_mid_start", gather_plan, 7 * len(mids), mids, [zone(p) for p in mids],
                             deps=[c_all])
    ffns_go = exchange_start("gather_ffn_start", gather_plan, 7 * len(ffns), ffns, [zone(p) for p in ffns],
                             deps=[mids_go[4]])

    ncol = w_ada.shape[2]
    c_all = c_all.reshape(NDEV, D)
    b_cols = lax.dynamic_slice_in_dim(b_ada, me * ncol, ncol, axis=1)
    act_all, mod_cols = mod_fwd(c_all, w_ada[0], b_cols)
    (mod_all,) = all_gather("gather_mod", [mod_cols])
    mod = lax.dynamic_index_in_dim(mod_all, me, axis=1, keepdims=False).reshape(NMOD, D)
    sh1, sc1, g1, sh2, sc2, g2 = [mod[q:q + 1] for q in range(NMOD)]

    expand = jnp.repeat(jnp.eye(NG, dtype=F32), NP, axis=0)
    a_re_c, a_im_c = ssm_a_re.reshape(NST, 1), ssm_a_im.reshape(NST, 1)
    ldt_c = ssm_log_dt.reshape(NG, 1)
    b_re_r, b_im_r = ssm_b_re.reshape(NST, GH), ssm_b_im.reshape(NST, GH)
    e_re, e_im, bb_re, bb_im = disc_fwd(a_re_c, a_im_c, ldt_c, b_re_r, b_im_r, expand)
    e_re_b, e_im_b = e_re.reshape(NBLK, 1, SB), e_im.reshape(NBLK, 1, SB)
    bb_re_g, bb_im_g = bb_re.reshape(NG, NP, GH), bb_im.reshape(NG, NP, GH)
    wbt_re = _block_diag(bb_re_g, NP, GH)
    wbt_im = _block_diag(bb_im_g, NP, GH)
    wb_re, wb_im = wbt_re.transpose(0, 2, 1), wbt_im.transpose(0, 2, 1)
    wct = jnp.concatenate([_block_diag(ssm_c_re[0], GH, NP), -_block_diag(ssm_c_im[0], GH, NP)], axis=2)
    wc = wct.transpose(0, 2, 1)
    to_b = lambda a: a.astype(BF16)
    dvec = ssm_d.reshape(NBLK, 1, CB)

    n1g = norm1_g

    def f_norm1(xv, g, sc, sh):
        _, xh = _rms_stats(xv)
        return [xh * g * (1.0 + sc) + sh], []

    (h1,) = rowwise("norm1", f_norm1, [xs], [n1g, sc1, sh1], [(D, BF16)], [], 512, deps=[ffns_go[4]])
    _, (w_in_land,) = exchange_wait("gather_in_wait", near_plan, in_go, h1, place_own=True)
    pass_go = exchange_start("gather_in_pass_start", pass_on_plan, NCHIP - 1, [w_in_land], [])
    (w_in_g,), _ = exchange_wait("gather_in_pass_wait", pass_on_plan, pass_go, pass_go[4])
    w_in_t = flat(w_in_g)
    z = mm("mm_in", h1, w_in_t, "nt", tiles=(2048, CW, 1024), b_rot=Z_ROT)

    conv_w_full = cw_g.transpose(1, 0, 2).reshape(KC, CW)
    w32 = jnp.pad(conv_w_full, ((0, HALO - KC), (0, 0)))
    yc, s_act = conv_fwd(z, w32, conv_b, conv_ln_g, conv_ln_b)
    conv_proj_t, ssm_glu_t, w_out_f = [
        flat(g) for g in exchange_wait("gather_mid_wait", gather_plan, mids_go, s_act, place_own=True)[1]]
    y_conv = mm("mm_conv_proj", s_act, conv_proj_t, "nt")

    xs_re, xs_im, ypre, gl = ssm_fwd(z, to_b(wb_re), to_b(wb_im), to_b(wc), e_re_b, e_im_b, dvec)
    n_mrg = D // MRG_BLK

    pair_of = lambda t, n: t // 2 + (t % 2) * n

    def ep_merge(accs, yc_v, gates):
        za, zb = accs
        glc, gls = gates[:, 0:MRG_BLK], gates[:, MRG_BLK:2 * MRG_BLK]
        return [_sig(glc) * yc_v + _sig(gls) * (za * _sig(zb)), jnp.concatenate([za, zb], axis=1)]

    merged, z2_pair = mm_ep("mm_ssm_glu", gl, ssm_glu_t, 2, lambda j, q: j + q * n_mrg, ep_merge,
                            [(y_conv, 1, 0), (z, 2, 0)], [(D, BF16, 1), (2 * D, BF16, 2)], (512, MRG_BLK, SW))
    row_tiles = lambda bk: (512, D, bk)
    whole = lambda j, q: j

    def ep_norm2(accs, xv, g1v, g, sc, sh):
        (o1v,) = accs
        x1v = xv + g1v * o1v
        _, xh = _rms_stats(x1v)
        return [x1v, xh * g * (1.0 + sc) + sh, o1v]

    x1, h2, o1 = mm_ep("mm_out", merged, w_out_f, 1, whole, ep_norm2, [(xs, 1, 0)],
                       [(D, F32, 1), (D, BF16, 1), (D, BF16, 1)], row_tiles(D), b_kn=True,
                       consts=[g1, norm2_g, sc2, sh2])
    w_ffn_in_t, w_ffn_out_f = [
        flat(g) for g in exchange_wait("gather_ffn_wait", gather_plan, ffns_go, h2, place_own=True)[1]]
    ffn_tiles = (512, FFN_BLK, 1024)
    n_ffn_blk = FH // FFN_BLK
    pair_map = lambda t: t // 2 + (t % 2) * n_ffn_blk

    def ep_swiglu(accs):
        fg, fu = accs
        return [fg * _sig(fg) * fu, jnp.concatenate([fg, fu], axis=1)]

    act, f_pair = mm_ep("mm_ffn_in", h2, w_ffn_in_t, 2, lambda j, q: j + q * n_ffn_blk, ep_swiglu, [],
                        [(FH, BF16, 1), (2 * FH, BF16, 2)], ffn_tiles)
    fg_row = final_g.reshape(1, D)

    def ep_final(accs, x1v, tv, g2v, fg):
        (o2v,) = accs
        x2v = x1v + g2v * o2v
        r, xh = _rms_stats(x2v)
        yv = xh * fg
        err = yv - tv
        loss = jnp.sum(_colsum(err * err), axis=1, keepdims=True) * (0.5 / D)
        dy = err * (1.0 / D)
        dx2 = _rms_bwd(dy * fg, xh, r)
        return ([dx2, g2v * dx2],
                [jnp.broadcast_to(loss, (1, LANE)), _colsum(dy * xh), _colsum(dx2 * o2v)])

    dx2, do2, loss_l, d_final_g, d_g2 = mm_ep_pipe(
        "mm_ffn_out", act, w_ffn_out_f, 1, whole, ep_final, [(x1, 1, 0), (tgt, 1, 0)],
        [(D, F32, 1), (D, BF16, 1)], row_tiles(FH), b_kn=True, consts=[g2, fg_row], sums=[LANE, D, D])

    g_ffn_out = mm("mm_g_ffn_out", act, do2, "tn", BF16, tiles=(FFN_BLK, 1024, 1024))

    def ep_dswiglu(accs, fp):
        (da,) = accs
        fg, fu = fp[:, 0:FFN_BLK].astype(F32), fp[:, FFN_BLK:2 * FFN_BLK].astype(F32)
        sg = _sig(fg)
        return [jnp.concatenate([da * fu * (sg * (1.0 + fg * (1.0 - sg))), da * (fg * sg)], axis=1)]

    (df,) = mm_ep("mm_dact", do2, w_ffn_out_f, 1, lambda j, q: j, ep_dswiglu, [(f_pair, 2, 0)],
                  [(2 * FH, BF16, 2)], ffn_tiles)
    g_ffn_in_t = mm("mm_g_ffn_in", df, h2, "tn", BF16, tiles=(FFN_BLK, 1024, 1024), o_rot=pair_map)

    def pair_go(tag, grads_t, deps=()):
        srcs = [g.reshape(NDEV, -1, D) for g in grads_t]
        return exchange_start("pair_" + tag + "_start", pair_plan, NCHIP * len(srcs), srcs,
                              [((NCHIP,) + s.shape[1:], s.dtype) for s in srcs], deps)

    def chip_go(tag, names, pair_started, after):
        own, from_sibling = exchange_wait("pair_" + tag + "_wait", pair_plan, pair_started, after)
        partials = [pair_sum("pair_sum_" + n, g, r) for n, g, r in zip(names, own, from_sibling)]
        return exchange_start("chip_" + tag + "_start", chip_plan, (NCHIP - 1) * len(partials), partials,
                              [((NCHIP - 1,) + p.shape[1:], p.dtype) for p in partials])

    def chip_done(tag, names, chip_started, after):
        partials, from_chips = exchange_wait("chip_" + tag + "_wait", chip_plan, chip_started, after)
        return [chip_sum("chip_sum_" + n, p, r) for n, p, r in zip(names, partials, from_chips)]

    pair_ffn = pair_go("ffn", [g_ffn_out, g_ffn_in_t])

    dh2 = mm("mm_dh2", df, w_ffn_in_t, "nn", BF16, tiles=(1024, 1024, FFN_BLK), b_rot=pair_map,
             deps=[pair_ffn[4]])

    def f_dnorm2(dh, x1v, dx2v, o1v, g, sc, g1v):
        dh, o1v = dh.astype(F32), o1v.astype(F32)
        r, xh = _rms_stats(x1v)
        dxh = dh * (1.0 + sc) * g
        dx1 = dx2v + _rms_bwd(dxh, xh, r)
        return ([dx1, g1v * dx1],
                [_colsum(dh * xh * g), _colsum(dh), _colsum(dh * (1.0 + sc) * xh), _colsum(dx1 * o1v)])

    dx1, do1, d_sc2, d_sh2, d_n2g, d_g1 = rowwise(
        "dnorm2", f_dnorm2, [dh2, x1, dx2, o1], [norm2_g, sc2, g1], [(D, F32), (D, BF16)], [D, D, D, D], 256)

    g_out = mm("mm_g_out", merged, do1, "tn", BF16)
    chip_ffn = chip_go("ffn", ("w_ffn_out", "w_ffn_in"), pair_ffn, g_out)

    def ep_dmerge(accs, yc_v, z2p, gates):
        (dm,) = accs
        za, zb = z2p[:, 0:MRG_BLK].astype(F32), z2p[:, MRG_BLK:2 * MRG_BLK].astype(F32)
        sc_, ss_, sb_ = _sig(gates[:, 0:MRG_BLK]), _sig(gates[:, MRG_BLK:2 * MRG_BLK]), _sig(zb)
        dys = dm * ss_
        dz2 = jnp.concatenate([dys * sb_, dys * za * sb_ * (1.0 - sb_)], axis=1)
        dgates = jnp.concatenate([dm * yc_v * sc_ * (1.0 - sc_), dm * (za * sb_) * ss_ * (1.0 - ss_)], axis=1)
        return [dm * sc_, dz2, dgates]

    dyconv, dz2, dz = mm_ep("mm_dmerged", do1, w_out_f, 1, lambda j, q: j, ep_dmerge,
                            [(y_conv, 1, 0), (z2_pair, 2, 0), (z, 2, 0)],
                            [(D, BF16, 1), (2 * D, BF16, 2), (ZW, BF16, 2)], (512, MRG_BLK, 1024), deps=[chip_ffn[4]])

    g_conv_proj_t = mm("mm_g_conv_proj", dyconv, s_act, "tn", BF16)
    mrg_map = lambda t: pair_of(t, n_mrg)
    dgl = mm("mm_dgl", dz2, ssm_glu_t, "nn", tiles=(1024, SW, MRG_BLK), b_rot=mrg_map)
    g_ssm_glu_t = mm("mm_g_ssm_glu", dz2, gl, "tn", BF16, tiles=(MRG_BLK, SW, 1024), o_rot=mrg_map)
    pair_mid = pair_go("mid", [g_out, g_conv_proj_t, g_ssm_glu_t])
    ds = mm("mm_ds", dyconv, conv_proj_t, "nn", deps=[pair_mid[4]])
    dz, d_lng, d_lnb, d_cb, d_cw32 = conv_bwd(ds, yc, z, w32, conv_ln_g, conv_ln_b, dz)
    dz, d_d, d_ar, d_ai, d_wb_re, d_wb_im, d_wc = ssm_bwd(
        dgl, ypre, z, xs_re, xs_im, to_b(wbt_re), to_b(wbt_im), to_b(wct), e_re_b, e_im_b, dvec, dz)
    chip_mid = chip_go("mid", ("w_out", "conv_proj", "ssm_glu"), pair_mid, dz)

    d_bb_re = _diag_blocks(d_wb_re.transpose(0, 2, 1), NP, GH).reshape(NST, GH)
    d_bb_im = _diag_blocks(d_wb_im.transpose(0, 2, 1), NP, GH).reshape(NST, GH)
    d_wct = d_wc.transpose(0, 2, 1)
    d_c_re = _diag_blocks(d_wct[:, :, 0:SB], GH, NP)
    d_c_im = -_diag_blocks(d_wct[:, :, SB:2 * SB], GH, NP)
    d_a_re, d_a_im, d_ldt, d_b_re, d_b_im = disc_bwd(
        a_re_c, a_im_c, ldt_c, b_re_r, b_im_r, expand, d_ar.reshape(NST, 1), d_ai.reshape(NST, 1), d_bb_re, d_bb_im)

    small_local = [jnp.concatenate([d_g1, d_sh2, d_sc2, d_g2], axis=1).reshape(-1), d_cw32[0:KC].reshape(-1),
                   d_cb.reshape(-1), d_lng.reshape(-1), d_lnb.reshape(-1), d_a_re.reshape(-1), d_a_im.reshape(-1),
                   d_b_re.reshape(-1), d_b_im.reshape(-1), d_c_re.reshape(-1), d_c_im.reshape(-1), d_d.reshape(-1),
                   d_ldt.reshape(-1), d_n2g.reshape(-1), d_final_g.reshape(-1), loss_l[0, 0:1]]
    small_sizes = [v.shape[0] for v in small_local]
    small_pack = _pad_rows(jnp.concatenate(small_local), 256 * LANE).reshape(-1, LANE)
    small_go = exchange_start("gather_small_start", gather_plan, NDEV - 1, [small_pack],
                              [((NDEV,) + small_pack.shape, F32)], deps=[chip_mid[4]])

    g_in_t = mm("mm_g_in", dz, h1, "tn", BF16, tiles=(CW, 1024, 2048), o_rot=Z_ROT, deps=[small_go[4]])
    pair_in = pair_go("in", [g_in_t])

    dh1 = mm("mm_dh1", dz, w_in_t, "nn", BF16, tiles=(2048, 1024, CW), b_rot=Z_ROT, deps=[pair_in[4]])

    def f_dnorm1(dh, xv, dx1v, g, sc):
        dh = dh.astype(F32)
        r, xh = _rms_stats(xv)
        dxh = dh * (1.0 + sc) * g
        return ([dx1v + _rms_bwd(dxh, xh, r)],
                [_colsum(dh * xh * g), _colsum(dh), _colsum(dh * (1.0 + sc) * xh)])

    grad_x, d_sc1, d_sh1, d_n1g = rowwise(
        "dnorm1", f_dnorm1, [dh1, xs, dx1], [n1g, sc1], [(D, F32)], [D, D, D], 256)
    chip_in = chip_go("in", ("w_in",), pair_in, grad_x)

    weights = {
        "w_ada": (w_ada, m_w_ada, v_w_ada), "b_ada": (b_ada, m_b_ada, v_b_ada), "norm1_g": (norm1_g, m_norm1_g, v_norm1_g),
        "w_in": (w_in, m_w_in, v_w_in), "conv_w": (conv_w, m_conv_w, v_conv_w), "conv_b": (conv_b, m_conv_b, v_conv_b),
        "conv_ln_g": (conv_ln_g, m_conv_ln_g, v_conv_ln_g), "conv_ln_b": (conv_ln_b, m_conv_ln_b, v_conv_ln_b),
        "conv_proj": (conv_proj, m_conv_proj, v_conv_proj), "ssm_a_re": (ssm_a_re, m_ssm_a_re, v_ssm_a_re),
        "ssm_a_im": (ssm_a_im, m_ssm_a_im, v_ssm_a_im), "ssm_b_re": (ssm_b_re, m_ssm_b_re, v_ssm_b_re),
        "ssm_b_im": (ssm_b_im, m_ssm_b_im, v_ssm_b_im), "ssm_c_re": (ssm_c_re, m_ssm_c_re, v_ssm_c_re),
        "ssm_c_im": (ssm_c_im, m_ssm_c_im, v_ssm_c_im), "ssm_d": (ssm_d, m_ssm_d, v_ssm_d),
        "ssm_log_dt": (ssm_log_dt, m_ssm_log_dt, v_ssm_log_dt), "ssm_glu": (ssm_glu, m_ssm_glu, v_ssm_glu),
        "w_out": (w_out, m_w_out, v_w_out), "norm2_g": (norm2_g, m_norm2_g, v_norm2_g),
        "w_ffn_in": (w_ffn_in, m_w_ffn_in, v_w_ffn_in), "w_ffn_out": (w_ffn_out, m_w_ffn_out, v_w_ffn_out),
        "final_g": (final_g, m_final_g, v_final_g),
    }
    order = list(weights)
    big = ("w_ada", "w_in", "conv_proj", "ssm_glu", "w_out", "w_ffn_in", "w_ffn_out")
    grads, delta, new_m, new_v = {}, {}, {}, {}

    def adam_big(n, g2d, transposed=False):
        wv, mv, vv = weights[n]
        shp = wv.shape
        t_in = (lambda a: a.reshape(shp[-2:]).T) if transposed else (lambda a: a.reshape(shp[-2:]))
        t_out = (lambda a: a.T.reshape(shp)) if transposed else (lambda a: a.reshape(shp))
        d_, m_, v_ = adam("adam_" + n, t_in(wv), g2d, t_in(mv), t_in(vv))
        grads[n], delta[n], new_m[n], new_v[n] = t_out(g2d), t_out(d_), t_out(m_), t_out(v_)
        return d_

    gs_ffn_out, gs_ffn_in = chip_done("ffn", ("w_ffn_out", "w_ffn_in"), chip_ffn, chip_in[4])
    adam_big("w_ffn_out", gs_ffn_out)
    last = adam_big("w_ffn_in", gs_ffn_in, transposed=True)
    gs_out, gs_conv_proj, gs_ssm_glu = chip_done("mid", ("w_out", "conv_proj", "ssm_glu"), chip_mid, last)
    adam_big("w_out", gs_out)
    adam_big("conv_proj", gs_conv_proj.reshape(-1, CW), transposed=True)
    adam_big("ssm_glu", gs_ssm_glu.reshape(-1, SW), transposed=True)

    late_local = [d_sh1.reshape(-1), d_sc1.reshape(-1), d_n1g.reshape(-1)]
    late_pack = _pad_rows(jnp.concatenate(late_local), 16 * LANE).reshape(-1, LANE)
    (late_all,) = all_gather("gather_small_late", [late_pack])
    _, (small_all,) = exchange_wait("gather_small_wait", gather_plan, small_go, late_all, place_own=True)

    def unpack(vec, sizes):
        out, pos = [], 0
        for n in sizes:
            out.append(vec[pos:pos + n])
            pos += n
        return out

    g_sh1, g_sc1, g_n1g = unpack(sum_slots("sum_small_late", late_all).reshape(-1), [D, D, D])
    (g_mod_rest, g_cw_full, g_cb, g_lng, g_lnb, g_a_re, g_a_im, g_b_re, g_b_im, g_c_re, g_c_im, g_d, g_ldt,
     g_n2g, g_fg, loss_sum) = unpack(sum_slots("sum_small", small_all).reshape(-1), small_sizes)
    g_b_ada = jnp.concatenate([g_sh1, g_sc1, g_mod_rest])
    loss = loss_sum[0]
    dmod_all = jnp.concatenate([late_all.reshape(NDEV, -1)[:, 0:2 * D], small_all.reshape(NDEV, -1)[:, 0:4 * D]],
                               axis=1)
    g_w_ada = ada_grad(act_all, lax.dynamic_slice_in_dim(dmod_all, me * ncol, ncol, axis=1))
    ccol = conv_w.shape[2]
    g_conv_w = lax.dynamic_slice_in_dim(g_cw_full.reshape(KC, CW), me * ccol, ccol, axis=1)

    adam_big("w_ada", g_w_ada)
    grads.update({
        "b_ada": g_b_ada.reshape(b_ada.shape), "norm1_g": g_n1g.reshape(norm1_g.shape),
        "conv_w": g_conv_w[None], "conv_b": g_cb.reshape(conv_b.shape),
        "conv_ln_g": g_lng.reshape(conv_ln_g.shape), "conv_ln_b": g_lnb.reshape(conv_ln_b.shape),
        "ssm_a_re": g_a_re.reshape(ssm_a_re.shape),
        "ssm_a_im": g_a_im.reshape(ssm_a_im.shape), "ssm_b_re": g_b_re.reshape(ssm_b_re.shape),
        "ssm_b_im": g_b_im.reshape(ssm_b_im.shape), "ssm_c_re": g_c_re.reshape(ssm_c_re.shape),
        "ssm_c_im": g_c_im.reshape(ssm_c_im.shape), "ssm_d": g_d.reshape(ssm_d.shape),
        "ssm_log_dt": g_ldt.reshape(ssm_log_dt.shape),
        "norm2_g": g_n2g.reshape(norm2_g.shape),
        "final_g": g_fg.reshape(final_g.shape),
    })
    small = [n for n in order if n not in big]
    def rows(a):
        if a.ndim == 4 and a.shape[-1] < a.shape[-2]:
            a = a.swapaxes(-1, -2)
        return a.reshape(1, -1) if a.ndim == 1 else a.reshape(-1, a.shape[-1])

    def unrows(a, shp):
        if len(shp) == 4 and shp[-1] < shp[-2]:
            return a.reshape(shp[:-2] + (shp[-1], shp[-2])).swapaxes(-1, -2)
        return a.reshape(shp)

    small_out = adam_many("adam_small", [rows(weights[n][0]) for n in small], [rows(grads[n]) for n in small],
                          [rows(weights[n][1]) for n in small], [rows(weights[n][2]) for n in small])
    for q, n in enumerate(small):
        shp = weights[n][0].shape
        delta[n], new_m[n], new_v[n] = [unrows(small_out[t * len(small) + q], shp) for t in range(3)]

    (gs_in,) = chip_done("in", ("w_in",), chip_in, small_out[0])
    adam_big("w_in", gs_in, transposed=True)

    return (loss, grad_x[None], *[grads[n] for n in order], *[delta[n] for n in order],
            *[new_m[n] for n in order], *[new_v[n] for n in order])
```

```python
import functools
import math

import jax
import jax.numpy as jnp
from jax import lax
from jax.experimental import pallas as pl
from jax.experimental.pallas import tpu as pltpu

F32 = jnp.float32
BF16 = jnp.bfloat16

D = 1024
CW = 512
KC = 31
SW = 512
NG = 32
GH = 16
NP = 64
NST = NG * NP
FH = 2816
FFN_BLK = 1408
MRG_BLK = 1024
NMOD = 6
NDEV = 8
EPS = 1e-6
CB = 128
SB = 512
NBLK = SW // CB
HALO = 32
ZW = 2 * CW + SW + 2 * D
Z_ROT = lambda j: (j + 3) % (ZW // CW)
ZB_A, ZB_G, ZB_U = 4, 5, 6

ADAM_LR = 0.001
ADAM_B1 = 0.9
ADAM_B2 = 0.999
ADAM_EPS = 1e-08
ADAM_WD = 0.01
ADAM_STEP = 10

V7X_VMEM_BYTES = 64 * 1024 * 1024
VMEM_LIMIT = V7X_VMEM_BYTES - 8 * 1024 * 1024
LANE = 128
MESH = pl.DeviceIdType.MESH
ANY_SPEC = pl.BlockSpec(memory_space=pl.ANY)


def _params(sem=None, **kw):
    if sem is not None:
        kw["dimension_semantics"] = sem
    return pltpu.CompilerParams(vmem_limit_bytes=VMEM_LIMIT, **kw)


def _tile(n, most):
    best = None
    for t in range(LANE, most + 1, LANE):
        if n % t == 0:
            best = t
    if best is None:
        raise ValueError(f"no tile for {n}")
    return best


def _sig(x):
    return jax.nn.sigmoid(x)


def mm(name, a, b, mode, out_dtype=F32, tiles=None, b_rot=None, o_rot=None, deps=()):
    if mode == "nn":
        (m, k), (k2, n) = a.shape, b.shape
    elif mode == "nt":
        (m, k), (n, k2) = a.shape, b.shape
    else:
        (k, m), (k2, n) = a.shape, b.shape
    assert k == k2, (name, a.shape, b.shape)
    bm, bn, bk = tiles or (_tile(m, 1024), _tile(n, 1408), _tile(k, 1408 if k % 1408 == 0 else 1024))
    bm, bn, bk = min(bm, m), min(bn, n), min(bk, k)
    assert m % bm == 0 and n % bn == 0 and k % bk == 0, (name, m, n, k, bm, bn, bk)
    nk = k // bk
    rot = lambda idx, r: idx if r is None else r(idx)
    if mode == "nn":
        a_spec = pl.BlockSpec((bm, bk), lambda i, j, kk: (i, kk))
        b_spec = pl.BlockSpec((bk, bn), lambda i, j, kk: (rot(kk, b_rot), j))
        dims = (((1,), (0,)), ((), ()))
    elif mode == "nt":
        a_spec = pl.BlockSpec((bm, bk), lambda i, j, kk: (i, kk))
        b_spec = pl.BlockSpec((bn, bk), lambda i, j, kk: (rot(j, b_rot), kk))
        dims = (((1,), (1,)), ((), ()))
    else:
        assert b_rot is None
        a_spec = pl.BlockSpec((bk, bm), lambda i, j, kk: (kk, i))
        b_spec = pl.BlockSpec((bk, bn), lambda i, j, kk: (kk, j))
        dims = (((0,), (0,)), ((), ()))

    def body(a_ref, b_ref, *rest):
        o_ref, acc_ref = rest[-2:]
        kk = pl.program_id(2)

        @pl.when(kk == 0)
        def _():
            acc_ref[...] = jnp.zeros_like(acc_ref)

        acc_ref[...] += lax.dot_general(a_ref[...], b_ref[...], dims, preferred_element_type=F32)

        @pl.when(kk == nk - 1)
        def _():
            o_ref[...] = acc_ref[...].astype(o_ref.dtype)

    return pl.pallas_call(
        body, name=name,
        grid=(m // bm, n // bn, nk),
        in_specs=[a_spec, b_spec] + [ANY_SPEC] * len(deps),
        out_specs=pl.BlockSpec((bm, bn), lambda i, j, kk: (rot(i, o_rot), j)),
        out_shape=jax.ShapeDtypeStruct((m, n), out_dtype),
        scratch_shapes=[pltpu.VMEM((bm, bn), F32)],
        compiler_params=_params(("parallel", "parallel", "arbitrary")),
    )(a, b, *deps)


def mm_ep(name, a, b, n_acc, acc_block, epilogue, extras, outs, tiles, deps=(), b_kn=False, k_map=None,
          consts=(), sums=()):
    m, k = a.shape
    bm, bn, bk = tiles
    bm = min(bm, m)
    nj = outs[0][0] // (outs[0][2] * bn)
    nk = k // bk
    assert m % bm == 0 and k % bk == 0 and b.shape[0 if b_kn else 1] == k, (name, a.shape, b.shape, tiles)
    assert not sums or nj == 1, name
    ne, nc, no, ns, nd = len(extras), len(consts), len(outs), len(sums), len(deps)
    dims = (((1,), (0,)), ((), ())) if b_kn else (((1,), (1,)), ((), ()))
    kmap = (lambda kk: kk) if k_map is None else k_map

    def body(*refs):
        a_ref, b_refs = refs[0], refs[1:1 + n_acc]
        e_refs = refs[1 + n_acc:1 + n_acc + ne + nc]
        first_out = 1 + n_acc + ne + nc + nd
        o_refs = refs[first_out:first_out + no]
        s_refs = refs[first_out + no:first_out + no + ns]
        acc_refs = refs[first_out + no + ns:]
        av = a_ref[...]
        prods = [lax.dot_general(av, b_ref[...], dims, preferred_element_type=F32) for b_ref in b_refs]

        if ns:
            @pl.when((pl.program_id(0) == 0) & (pl.program_id(2) == 0))
            def _():
                for s_ref in s_refs:
                    s_ref[...] = jnp.zeros_like(s_ref)

        def finish(accs):
            res = epilogue(accs, *[e[...] for e in e_refs])
            tiles_out, sums_out = res if ns else (res, ())
            for o_ref, v in zip(o_refs, tiles_out):
                o_ref[...] = v.astype(o_ref.dtype)
            for s_ref, v in zip(s_refs, sums_out):
                s_ref[...] += v

        if nk == 1:
            finish(prods)
        else:
            kk = pl.program_id(2)

            @pl.when(kk == 0)
            def _():
                for acc_ref in acc_refs:
                    acc_ref[...] = jnp.zeros_like(acc_ref)

            for acc_ref, p in zip(acc_refs, prods):
                acc_ref[...] += p

            @pl.when(kk == nk - 1)
            def _():
                finish([acc_ref[...] for acc_ref in acc_refs])

    in_specs = [pl.BlockSpec((bm, bk), lambda i, j, kk: (i, kk))]
    if b_kn:
        in_specs += [pl.BlockSpec((bk, bn), functools.partial(lambda i, j, kk, q: (kmap(kk), acc_block(j, q)), q=q))
                     for q in range(n_acc)]
    else:
        in_specs += [pl.BlockSpec((bn, bk), functools.partial(lambda i, j, kk, q: (acc_block(j, q), kmap(kk)), q=q))
                     for q in range(n_acc)]
    in_specs += [pl.BlockSpec((bm, w * bn), functools.partial(lambda i, j, kk, off: (i, j + off), off=off))
                 for (_, w, off) in extras]
    in_specs += [pl.BlockSpec((1, bn), lambda i, j, kk: (0, j)) for _ in consts]
    in_specs += [ANY_SPEC] * nd
    out_specs = [pl.BlockSpec((bm, w * bn), lambda i, j, kk: (i, j)) for (_, _, w) in outs]
    out_specs += [pl.BlockSpec((1, w), lambda i, j, kk: (0, 0)) for w in sums]
    out_shape = [jax.ShapeDtypeStruct((m, cols), dt) for (cols, dt, _) in outs]
    out_shape += [jax.ShapeDtypeStruct((1, w), F32) for w in sums]
    return pl.pallas_call(
        body, name=name, grid=(m // bm, nj, nk),
        in_specs=in_specs, out_specs=out_specs, out_shape=out_shape,
        scratch_shapes=[pltpu.VMEM((bm, bn), F32)] * (n_acc if nk > 1 else 0),
        compiler_params=_params(("arbitrary",) * 3 if sums else ("parallel", "parallel", "arbitrary")),
    )(a, *[b] * n_acc, *[e[0] for e in extras], *consts, *deps)


def mm_ep_pipe(name, a, b, n_acc, acc_block, epilogue, extras, outs, tiles, deps=(), b_kn=False, consts=(), sums=()):
    m, k = a.shape
    bm, bn, bk = tiles
    bm = min(bm, m)
    assert bk == k and m % bm == 0 and b.shape[0 if b_kn else 1] == k, (name, a.shape, b.shape, tiles)
    ni, nj = m // bm, outs[0][0] // (outs[0][2] * bn)
    nt = ni * nj
    assert not sums or nj == 1, name
    ne, nc, no, ns, nd = len(extras), len(consts), len(outs), len(sums), len(deps)
    dims = (((1,), (0,)), ((), ())) if b_kn else (((1,), (1,)), ((), ()))
    cur_i = lambda t: jnp.minimum(t, nt - 1) // nj
    cur_j = lambda t: jnp.minimum(t, nt - 1) % nj
    prev_i = lambda t: jnp.maximum(t - 1, 0) // nj
    prev_j = lambda t: jnp.maximum(t - 1, 0) % nj

    def body(*refs):
        a_ref, b_refs = refs[0], refs[1:1 + n_acc]
        e_refs = refs[1 + n_acc:1 + n_acc + ne + nc]
        first_out = 1 + n_acc + ne + nc + nd
        o_refs = refs[first_out:first_out + no]
        s_refs = refs[first_out + no:first_out + no + ns]
        acc_ref = refs[first_out + no + ns]
        t = pl.program_id(0)

        @pl.when(t == 0)
        def _():
            acc_ref[...] = jnp.zeros_like(acc_ref)
            for s_ref in s_refs:
                s_ref[...] = jnp.zeros_like(s_ref)

        slot = t % 2
        done = [acc_ref[(1 - slot) * n_acc + q] for q in range(n_acc)]
        av = a_ref[...]
        for q, b_ref in enumerate(b_refs):
            acc_ref[slot * n_acc + q] = lax.dot_general(av, b_ref[...], dims, preferred_element_type=F32)
        res = epilogue(done, *[e[...] for e in e_refs])
        tiles_out, sums_out = res if ns else (res, ())
        for o_ref, v in zip(o_refs, tiles_out):
            o_ref[...] = v.astype(o_ref.dtype)
        live = (t >= 1).astype(F32)
        for s_ref, v in zip(s_refs, sums_out):
            s_ref[...] += v * live

    in_specs = [pl.BlockSpec((bm, k), lambda t: (cur_i(t), 0))]
    if b_kn:
        in_specs += [pl.BlockSpec((k, bn), functools.partial(lambda t, q: (0, acc_block(cur_j(t), q)), q=q))
                     for q in range(n_acc)]
    else:
        in_specs += [pl.BlockSpec((bn, k), functools.partial(lambda t, q: (acc_block(cur_j(t), q), 0), q=q))
                     for q in range(n_acc)]
    in_specs += [pl.BlockSpec((bm, w * bn), functools.partial(lambda t, off: (prev_i(t), prev_j(t) + off), off=off))
                 for (_, w, off) in extras]
    in_specs += [pl.BlockSpec((1, bn), lambda t: (0, prev_j(t))) for _ in consts]
    in_specs += [ANY_SPEC] * nd
    out_specs = [pl.BlockSpec((bm, w * bn), lambda t: (prev_i(t), prev_j(t))) for (_, _, w) in outs]
    out_specs += [pl.BlockSpec((1, w), lambda t: (0, 0)) for w in sums]
    out_shape = [jax.ShapeDtypeStruct((m, cols), dt) for (cols, dt, _) in outs]
    out_shape += [jax.ShapeDtypeStruct((1, w), F32) for w in sums]
    return pl.pallas_call(
        body, name=name, grid=(nt + 1,),
        in_specs=in_specs, out_specs=out_specs, out_shape=out_shape,
        scratch_shapes=[pltpu.VMEM((2 * n_acc, bm, bn), F32)],
        compiler_params=_params(("arbitrary",)),
    )(a, *[b] * n_acc, *[e[0] for e in extras], *consts, *deps)


def rowwise(name, fn, rows, consts, out_rows, out_sums, ts, alias=None, deps=()):
    rows = [r if isinstance(r, tuple) else (r, r.shape[1], 0) for r in rows]
    out_rows = [o if len(o) == 4 else (o[0], o[1], o[0], 0) for o in out_rows]
    s = rows[0][0].shape[0]
    nt = s // ts
    nr, nc, no, ns = len(rows), len(consts), len(out_rows), len(out_sums)
    in_specs = [pl.BlockSpec((ts, w), functools.partial(lambda i, cb: (i, cb), cb=cb)) for (_, w, cb) in rows]
    in_specs += [pl.BlockSpec(c.shape, lambda i: (0, 0)) for c in consts]
    operands = [r[0] for r in rows] + list(consts)
    aliases = {}
    if alias is not None:
        in_specs.append(pl.BlockSpec(memory_space=pl.ANY))
        operands.append(alias[0])
        aliases = {nr + nc: alias[1]}
    in_specs += [ANY_SPEC] * len(deps)
    operands += list(deps)
    out_shape = [jax.ShapeDtypeStruct((s, tw), dt) for (_, dt, tw, _) in out_rows]
    out_shape += [jax.ShapeDtypeStruct((1, w), F32) for w in out_sums]
    out_specs = [pl.BlockSpec((ts, w), functools.partial(lambda i, cb: (i, cb), cb=cb)) for (w, _, _, cb) in out_rows]
    out_specs += [pl.BlockSpec((1, w), lambda i: (0, 0)) for w in out_sums]
    n_in = len(operands)

    def body(*refs):
        ins, outs = refs[:nr + nc], refs[n_in:]
        i = pl.program_id(0)
        ro, so = fn(*[r[...] for r in ins])
        for q in range(no):
            outs[q][...] = ro[q].astype(outs[q].dtype)
        if ns:
            @pl.when(i == 0)
            def _():
                for q in range(ns):
                    outs[no + q][...] = jnp.zeros_like(outs[no + q])

            for q in range(ns):
                outs[no + q][...] += so[q]

    return pl.pallas_call(
        body, name=name, grid=(nt,),
        in_specs=in_specs, out_specs=out_specs, out_shape=out_shape, input_output_aliases=aliases,
        compiler_params=_params(("arbitrary",) if ns else ("parallel",)),
    )(*operands)


def _colsum(v):
    return jnp.sum(v, axis=0, keepdims=True)


def _rms_stats(xv):
    r = lax.rsqrt(jnp.mean(xv * xv, axis=-1, keepdims=True) + EPS)
    return r, xv * r


def _rms_bwd(dxhat, xhat, r):
    return r * (dxhat - xhat * jnp.mean(dxhat * xhat, axis=-1, keepdims=True))


def _gelu(v):
    k = math.sqrt(2.0 / math.pi)
    t = jnp.tanh(k * (v + 0.044715 * v * v * v))
    return 0.5 * v * (1.0 + t), t


def _gelu_grad(v, t):
    k = math.sqrt(2.0 / math.pi)
    return 0.5 * (1.0 + t) + 0.5 * v * (1.0 - t * t) * k * (1.0 + 3.0 * 0.044715 * v * v)


CONV_TS = 512
CONV_CH = 64


def _ln_fwd(yc, g, b):
    mu = jnp.mean(yc, axis=-1, keepdims=True)
    xc = yc - mu
    rstd = lax.rsqrt(jnp.mean(xc * xc, axis=-1, keepdims=True) + EPS)
    nhat = xc * rstd
    return nhat, rstd, nhat * g + b


SUBL = 8


def _shifted_copies(buf, sh, ts):
    for b in range(1, SUBL):
        sh[b - 1] = buf[pl.ds(b, ts + HALO - SUBL), :]


def _shifted(buf, sh, start):
    b = start % SUBL
    if b == 0:
        return buf[pl.ds(start, CONV_CH), :]
    return sh[b - 1, pl.ds(start - b, CONV_CH), :]


def conv_fwd(z, w32, cb, lg, lb):
    s = z.shape[0]
    ts = CONV_TS
    nt = s // ts
    hb = ts // HALO

    def body(a_ref, g_ref, ah_ref, gh_ref, w_ref, cb_ref, lg_ref, lb_ref, yc_ref, s_ref, ubuf, ush):
        i = pl.program_id(0)
        first = (i > 0).astype(F32)
        ubuf[0:HALO, :] = ah_ref[...] * _sig(gh_ref[...]) * first
        ubuf[HALO:HALO + ts, :] = a_ref[...] * _sig(g_ref[...])
        _shifted_copies(ubuf, ush, ts)
        for c0 in range(0, ts, CONV_CH):
            acc = jnp.zeros((CONV_CH, CW), F32)
            for k in range(KC):
                acc = acc + w_ref[k:k + 1, :] * _shifted(ubuf, ush, c0 + k + 2)
            yc = acc + cb_ref[...]
            yc_ref[c0:c0 + CONV_CH, :] = yc
            _, _, ln = _ln_fwd(yc, lg_ref[...], lb_ref[...])
            s_ref[c0:c0 + CONV_CH, :] = (ln * _sig(ln)).astype(s_ref.dtype)

    cur = lambda cbk: pl.BlockSpec((ts, CW), functools.partial(lambda i, q: (i, q), q=cbk))
    prev = lambda cbk: pl.BlockSpec((HALO, CW), functools.partial(lambda i, q: (jnp.maximum(i * hb - 1, 0), q), q=cbk))
    const = lambda a: pl.BlockSpec(a.shape, lambda i: (0, 0))
    return pl.pallas_call(
        body, name="conv_fwd", grid=(nt,),
        in_specs=[cur(ZB_A), cur(ZB_G), prev(ZB_A), prev(ZB_G), const(w32), const(cb), const(lg), const(lb)],
        out_specs=[pl.BlockSpec((ts, CW), lambda i: (i, 0)), pl.BlockSpec((ts, CW), lambda i: (i, 0))],
        out_shape=[jax.ShapeDtypeStruct((s, CW), F32), jax.ShapeDtypeStruct((s, CW), BF16)],
        scratch_shapes=[pltpu.VMEM((HALO + ts, CW), F32), pltpu.VMEM((SUBL - 1, ts + HALO - SUBL, CW), F32)],
        compiler_params=_params(("parallel",)),
    )(z, z, z, z, w32, cb, lg, lb)


def conv_bwd(ds, yc, z, w32, lg, lb, dz):
    s = z.shape[0]
    ts = CONV_TS
    nt = s // ts
    hb = ts // HALO
    last_hb = s // HALO - 1

    def ln_bwd(dsv, ycv, g, b):
        nhat, rstd, ln = _ln_fwd(ycv, g, b)
        sg = _sig(ln)
        dln = dsv * (sg * (1.0 + ln * (1.0 - sg)))
        dnh = dln * g
        dyc = rstd * (dnh - jnp.mean(dnh, axis=-1, keepdims=True)
                      - nhat * jnp.mean(dnh * nhat, axis=-1, keepdims=True))
        return dyc, dln, nhat

    def body(ds_ref, yc_ref, dsn_ref, ycn_ref, a_ref, g_ref, ah_ref, gh_ref, w_ref, lg_ref, lb_ref, dz_in,
             dz_ref, dlg_ref, dlb_ref, dcb_ref, dw_ref, dbuf, ubuf, dsh, ush):
        i = pl.program_id(0)

        @pl.when(i == 0)
        def _():
            dlg_ref[...] = jnp.zeros_like(dlg_ref)
            dlb_ref[...] = jnp.zeros_like(dlb_ref)
            dcb_ref[...] = jnp.zeros_like(dcb_ref)
            dw_ref[...] = jnp.zeros_like(dw_ref)

        lg, lb = lg_ref[...], lb_ref[...]
        dyc, dln, nhat = ln_bwd(ds_ref[...], yc_ref[...], lg, lb)
        dlg_ref[...] += _colsum(dln * nhat)
        dlb_ref[...] += _colsum(dln)
        dcb_ref[...] += _colsum(dyc)
        dbuf[0:ts, :] = dyc
        nxt = (i < nt - 1).astype(F32)
        dbuf[ts:ts + HALO, :] = ln_bwd(dsn_ref[...], ycn_ref[...], lg, lb)[0] * nxt
        first = (i > 0).astype(F32)
        ubuf[0:HALO, :] = ah_ref[...] * _sig(gh_ref[...]) * first
        ubuf[HALO:HALO + ts, :] = a_ref[...] * _sig(g_ref[...])
        _shifted_copies(dbuf, dsh, ts)
        _shifted_copies(ubuf, ush, ts)
        for c0 in range(0, ts, CONV_CH):
            du = jnp.zeros((CONV_CH, CW), F32)
            dyc_c = dbuf[c0:c0 + CONV_CH, :]
            for k in range(KC):
                du = du + w_ref[k:k + 1, :] * _shifted(dbuf, dsh, c0 + KC - 1 - k)
                dw_ref[k:k + 1, :] += _colsum(dyc_c * _shifted(ubuf, ush, c0 + k + 2))
            av = a_ref[c0:c0 + CONV_CH, :]
            sg = _sig(g_ref[c0:c0 + CONV_CH, :])
            dz_ref[c0:c0 + CONV_CH, 0:CW] = (du * sg).astype(dz_ref.dtype)
            dz_ref[c0:c0 + CONV_CH, CW:2 * CW] = (du * av * sg * (1.0 - sg)).astype(dz_ref.dtype)

    cur = lambda w, cbk: pl.BlockSpec((ts, w), functools.partial(lambda i, q: (i, q), q=cbk))
    prev = lambda cbk: pl.BlockSpec((HALO, CW), functools.partial(lambda i, q: (jnp.maximum(i * hb - 1, 0), q), q=cbk))
    nxt_spec = pl.BlockSpec((HALO, CW), lambda i: (jnp.minimum((i + 1) * hb, last_hb), 0))
    const = lambda a: pl.BlockSpec(a.shape, lambda i: (0, 0))
    acc = lambda r: pl.BlockSpec((r, CW), lambda i: (0, 0))
    return pl.pallas_call(
        body, name="conv_bwd", grid=(nt,),
        in_specs=[cur(CW, 0), cur(CW, 0), nxt_spec, nxt_spec, cur(CW, ZB_A), cur(CW, ZB_G), prev(ZB_A), prev(ZB_G),
                  const(w32), const(lg), const(lb), pl.BlockSpec(memory_space=pl.ANY)],
        out_specs=[pl.BlockSpec((ts, 2 * CW), lambda i: (i, ZB_A // 2)), acc(1), acc(1), acc(1), acc(HALO)],
        out_shape=[jax.ShapeDtypeStruct(dz.shape, dz.dtype), jax.ShapeDtypeStruct((1, CW), F32),
                   jax.ShapeDtypeStruct((1, CW), F32), jax.ShapeDtypeStruct((1, CW), F32),
                   jax.ShapeDtypeStruct((HALO, CW), F32)],
        scratch_shapes=[pltpu.VMEM((ts + HALO, CW), F32), pltpu.VMEM((HALO + ts, CW), F32)]
        + [pltpu.VMEM((SUBL - 1, ts + HALO - SUBL, CW), F32)] * 2,
        input_output_aliases={11: 0},
        compiler_params=_params(("arbitrary",)),
    )(ds, yc, ds, yc, z, z, z, z, w32, lg, lb, dz)


SSM_TS = 1024
GRP = 8


def _cmul(ar, ai, br, bi):
    return ar * br - ai * bi, ar * bi + ai * br


def _scan_tables(ar, ai, reverse):
    n = ar.shape[1]
    row = lax.broadcasted_iota(jnp.int32, (GRP, n), 0)
    dist = (GRP - 1 - row) if reverse else row
    one_r = jnp.broadcast_to(ar, (GRP, n))
    one_i = jnp.broadcast_to(ai, (GRP, n))
    p2r, p2i = _cmul(one_r, one_i, one_r, one_i)
    p4r, p4i = _cmul(p2r, p2i, p2r, p2i)
    steps = []
    for sft, (pr, pi) in ((1, (one_r, one_i)), (2, (p2r, p2i)), (4, (p4r, p4i))):
        keep = dist >= sft
        steps.append((jnp.where(keep, pr, 0.0), jnp.where(keep, pi, 0.0)))
    cr, ci = one_r, one_i
    accr, acci = one_r, one_i
    for e in range(1, GRP):
        cr, ci = _cmul(cr, ci, one_r, one_i)
        accr = jnp.where(dist == e, cr, accr)
        acci = jnp.where(dist == e, ci, acci)
    return steps, (accr, acci)


def _scan_group(xr, xi, steps, carry_tab, cr, ci, reverse):
    for sft, (tr, ti) in zip((1, 2, 4), steps):
        amt = (GRP - sft) if reverse else sft
        sr = pltpu.roll(xr, amt, 0)
        si = pltpu.roll(xi, amt, 0)
        xr, xi = xr + tr * sr - ti * si, xi + tr * si + ti * sr
    pr, pi = carry_tab
    xr = xr + pr * cr - pi * ci
    xi = xi + pr * ci + pi * cr
    return xr, xi


def ssm_fwd(z, wb_re, wb_im, wc, e_re, e_im, dvec):
    s = z.shape[0]
    ts = SSM_TS
    nt = s // ts
    ucol0 = ZB_U * CW // CB

    def body(u_ref, wbr_ref, wbi_ref, wc_ref, er_ref, ei_ref, d_ref, xr_ref, xi_ref, y_ref, gl_ref, car_r, car_i):
        i = pl.program_id(1)

        @pl.when(i == 0)
        def _():
            car_r[...] = jnp.zeros_like(car_r)
            car_i[...] = jnp.zeros_like(car_i)

        u = u_ref[...]
        ub = u.astype(BF16)
        xr_ref[...] = jnp.dot(ub, wbr_ref[0], preferred_element_type=F32)
        xi_ref[...] = jnp.dot(ub, wbi_ref[0], preferred_element_type=F32)
        steps, ctab = _scan_tables(er_ref[0], ei_ref[0], False)

        def grp(r, carry):
            cr, ci = carry
            r0 = pl.multiple_of(r * GRP, GRP)
            xr, xi = _scan_group(xr_ref[pl.ds(r0, GRP), :], xi_ref[pl.ds(r0, GRP), :], steps, ctab, cr, ci, False)
            xr_ref[pl.ds(r0, GRP), :] = xr
            xi_ref[pl.ds(r0, GRP), :] = xi
            return (jnp.broadcast_to(xr[GRP - 1:GRP, :], (GRP, SB)), jnp.broadcast_to(xi[GRP - 1:GRP, :], (GRP, SB)))

        cr, ci = lax.fori_loop(0, ts // GRP, grp, (car_r[...], car_i[...]))
        car_r[...] = cr
        car_i[...] = ci
        y = (jnp.dot(xr_ref[...].astype(BF16), wc_ref[0, 0:SB, :], preferred_element_type=F32)
             + jnp.dot(xi_ref[...].astype(BF16), wc_ref[0, SB:2 * SB, :], preferred_element_type=F32)
             + d_ref[0] * u)
        y_ref[...] = y
        gl_ref[...] = _gelu(y)[0].astype(gl_ref.dtype)

    blk3 = lambda a: pl.BlockSpec((1,) + a.shape[1:], lambda j, i: (j, 0, 0))
    return pl.pallas_call(
        body, name="ssm_fwd", grid=(NBLK, nt),
        in_specs=[pl.BlockSpec((ts, CB), lambda j, i: (i, ucol0 + j)),
                  blk3(wb_re), blk3(wb_im), blk3(wc), blk3(e_re), blk3(e_im), blk3(dvec)],
        out_specs=[pl.BlockSpec((ts, SB), lambda j, i: (i, j)), pl.BlockSpec((ts, SB), lambda j, i: (i, j)),
                   pl.BlockSpec((ts, CB), lambda j, i: (i, j)), pl.BlockSpec((ts, CB), lambda j, i: (i, j))],
        out_shape=[jax.ShapeDtypeStruct((s, NST), F32), jax.ShapeDtypeStruct((s, NST), F32),
                   jax.ShapeDtypeStruct((s, SW), F32), jax.ShapeDtypeStruct((s, SW), BF16)],
        scratch_shapes=[pltpu.VMEM((GRP, SB), F32), pltpu.VMEM((GRP, SB), F32)],
        compiler_params=_params(("parallel", "arbitrary")),
    )(z, wb_re, wb_im, wc, e_re, e_im, dvec)


def ssm_bwd(dgl, ypre, z, xs_re, xs_im, wbt_re, wbt_im, wct, e_re, e_im, dvec, dz):
    s = z.shape[0]
    ts = SSM_TS
    nt = s // ts
    ucol0 = ZB_U * CW // CB
    tn_dims = (((0,), (0,)), ((), ()))

    def body(dgl_ref, y_ref, u_ref, xr_ref, xi_ref, wbtr_ref, wbti_ref, wct_ref, er_ref, ei_ref, d_ref, dz_in,
             du_ref, dd_ref, dar_ref, dai_ref, dwbr_ref, dwbi_ref, dwc_ref,
             lr_ref, li_ref, car_r, car_i, acc_r, acc_i):
        i = pl.program_id(1)

        @pl.when(i == 0)
        def _():
            for ref in (car_r, car_i, acc_r, acc_i, dd_ref, dwbr_ref, dwbi_ref, dwc_ref):
                ref[...] = jnp.zeros_like(ref)

        u = u_ref[...]
        y = y_ref[...]
        dy = dgl_ref[...] * _gelu_grad(y, _gelu(y)[1])
        dd_ref[0] += _colsum(dy * u)
        dyb = dy.astype(BF16)
        dxo = jnp.dot(dyb, wct_ref[0], preferred_element_type=F32)
        lr_ref[...] = dxo[:, 0:SB]
        li_ref[...] = dxo[:, SB:2 * SB]
        steps, ctab = _scan_tables(er_ref[0], -ei_ref[0], True)
        row = lax.broadcasted_iota(jnp.int32, (GRP, SB), 0)

        def grp(q, carry):
            cr, ci, ar, ai = carry
            r0 = pl.multiple_of((ts // GRP - 1 - q) * GRP, GRP)
            lr, li = _scan_group(lr_ref[pl.ds(r0, GRP), :], li_ref[pl.ds(r0, GRP), :], steps, ctab, cr, ci, True)
            lr_ref[pl.ds(r0, GRP), :] = lr
            li_ref[pl.ds(r0, GRP), :] = li
            nr = jnp.where(row == GRP - 1, cr, pltpu.roll(lr, GRP - 1, 0))
            ni = jnp.where(row == GRP - 1, ci, pltpu.roll(li, GRP - 1, 0))
            xr = xr_ref[pl.ds(r0, GRP), :]
            xi = xi_ref[pl.ds(r0, GRP), :]
            ar = ar + nr * xr + ni * xi
            ai = ai + ni * xr - nr * xi
            return (jnp.broadcast_to(lr[0:1, :], (GRP, SB)), jnp.broadcast_to(li[0:1, :], (GRP, SB)), ar, ai)

        cr, ci, ar, ai = lax.fori_loop(0, ts // GRP, grp, (car_r[...], car_i[...], acc_r[...], acc_i[...]))
        car_r[...] = cr
        car_i[...] = ci
        acc_r[...] = ar
        acc_i[...] = ai

        @pl.when(i == nt - 1)
        def _():
            dar_ref[0] = _colsum(ar)
            dai_ref[0] = _colsum(ai)

        lrb = lr_ref[...].astype(BF16)
        lib = li_ref[...].astype(BF16)
        du = (jnp.dot(lrb, wbtr_ref[0], preferred_element_type=F32)
              + jnp.dot(lib, wbti_ref[0], preferred_element_type=F32) + d_ref[0] * dy)
        du_ref[...] = du.astype(du_ref.dtype)
        ub = u.astype(BF16)
        dwbr_ref[0] += lax.dot_general(ub, lrb, tn_dims, preferred_element_type=F32)
        dwbi_ref[0] += lax.dot_general(ub, lib, tn_dims, preferred_element_type=F32)
        dwc_ref[0, 0:SB, :] += lax.dot_general(xr_ref[...].astype(BF16), dyb, tn_dims, preferred_element_type=F32)
        dwc_ref[0, SB:2 * SB, :] += lax.dot_general(xi_ref[...].astype(BF16), dyb, tn_dims, preferred_element_type=F32)

    rev = lambda i: nt - 1 - i
    blk3 = lambda a: pl.BlockSpec((1,) + a.shape[1:], lambda j, i: (j, 0, 0))
    acc3 = lambda r, c: pl.BlockSpec((1, r, c), lambda j, i: (j, 0, 0))
    return pl.pallas_call(
        body, name="ssm_bwd", grid=(NBLK, nt),
        in_specs=[pl.BlockSpec((ts, CB), lambda j, i: (rev(i), j)), pl.BlockSpec((ts, CB), lambda j, i: (rev(i), j)),
                  pl.BlockSpec((ts, CB), lambda j, i: (rev(i), ucol0 + j)),
                  pl.BlockSpec((ts, SB), lambda j, i: (rev(i), j)), pl.BlockSpec((ts, SB), lambda j, i: (rev(i), j)),
                  blk3(wbt_re), blk3(wbt_im), blk3(wct), blk3(e_re), blk3(e_im), blk3(dvec),
                  pl.BlockSpec(memory_space=pl.ANY)],
        out_specs=[pl.BlockSpec((ts, CB), lambda j, i: (rev(i), ucol0 + j)),
                   acc3(1, CB), acc3(1, SB), acc3(1, SB), acc3(CB, SB), acc3(CB, SB), acc3(2 * SB, CB)],
        out_shape=[jax.ShapeDtypeStruct(dz.shape, dz.dtype),
                   jax.ShapeDtypeStruct((NBLK, 1, CB), F32),
                   jax.ShapeDtypeStruct((NBLK, 1, SB), F32), jax.ShapeDtypeStruct((NBLK, 1, SB), F32),
                   jax.ShapeDtypeStruct((NBLK, CB, SB), F32), jax.ShapeDtypeStruct((NBLK, CB, SB), F32),
                   jax.ShapeDtypeStruct((NBLK, 2 * SB, CB), F32)],
        scratch_shapes=[pltpu.VMEM((ts, SB), F32), pltpu.VMEM((ts, SB), F32)] + [pltpu.VMEM((GRP, SB), F32)] * 4,
        input_output_aliases={11: 0},
        compiler_params=_params(("parallel", "arbitrary")),
    )(dgl, ypre, z, xs_re, xs_im, wbt_re, wbt_im, wct, e_re, e_im, dvec, dz)


def _disc(a_re, a_im, log_dt, b_re, b_im, expand):
    dt = jnp.dot(expand, jnp.exp(log_dt), preferred_element_type=F32, precision=lax.Precision.HIGHEST)
    mag = jnp.exp(dt * a_re)
    e_re, e_im = mag * jnp.cos(dt * a_im), mag * jnp.sin(dt * a_im)
    n_re, n_im = e_re - 1.0, e_im
    den = a_re * a_re + a_im * a_im
    q_re = (n_re * a_re + n_im * a_im) / den
    q_im = (n_im * a_re - n_re * a_im) / den
    return e_re, e_im, q_re * b_re - q_im * b_im, q_re * b_im + q_im * b_re


def _whole(a):
    return pl.BlockSpec(a.shape, functools.partial(lambda n: (0,) * n, n=a.ndim))


def disc_fwd(a_re, a_im, log_dt, b_re, b_im, expand):
    def body(ar, ai, ld, br, bi, ex, er_o, ei_o, bbr_o, bbi_o):
        er, ei, bbr, bbi = _disc(ar[...], ai[...], ld[...], br[...], bi[...], ex[...])
        er_o[...] = er
        ei_o[...] = ei
        bbr_o[...] = bbr
        bbi_o[...] = bbi

    ins = (a_re, a_im, log_dt, b_re, b_im, expand)
    outs = [jax.ShapeDtypeStruct(a_re.shape, F32)] * 2 + [jax.ShapeDtypeStruct(b_re.shape, F32)] * 2
    return pl.pallas_call(body, name="disc_fwd", in_specs=[_whole(a) for a in ins],
                          out_specs=[_whole(o) for o in outs], out_shape=outs, compiler_params=_params())(*ins)


def disc_bwd(a_re, a_im, log_dt, b_re, b_im, expand, de_re, de_im, dbb_re, dbb_im):
    def body(ar, ai, ld, br, bi, ex, der, dei, dbr, dbi, o_ar, o_ai, o_ld, o_br, o_bi):
        exv = ex[...]
        _, vjp = jax.vjp(lambda *p: _disc(*p, exv), ar[...], ai[...], ld[...], br[...], bi[...])
        g = vjp((der[...], dei[...], dbr[...], dbi[...]))
        for o, v in zip((o_ar, o_ai, o_ld, o_br, o_bi), g):
            o[...] = v

    ins = (a_re, a_im, log_dt, b_re, b_im, expand, de_re, de_im, dbb_re, dbb_im)
    outs = [jax.ShapeDtypeStruct(a.shape, F32) for a in (a_re, a_im, log_dt, b_re, b_im)]
    return pl.pallas_call(body, name="disc_bwd", in_specs=[_whole(a) for a in ins],
                          out_specs=[_whole(o) for o in outs], out_shape=outs, compiler_params=_params())(*ins)


def mod_fwd(c_all, w_ada, b_cols):
    def body(c_ref, w_ref, b_ref, act_ref, mod_ref):
        cv = c_ref[...]
        act = cv * _sig(cv)
        act_ref[...] = act
        mod_ref[...] = jnp.dot(act, w_ref[...], preferred_element_type=F32, precision=lax.Precision.HIGHEST) + b_ref[...]

    ins = (c_all, w_ada, b_cols)
    outs = [jax.ShapeDtypeStruct(c_all.shape, F32), jax.ShapeDtypeStruct((NDEV, w_ada.shape[1]), F32)]
    return pl.pallas_call(body, name="mod_fwd", in_specs=[_whole(a) for a in ins],
                          out_specs=[_whole(o) for o in outs], out_shape=outs, compiler_params=_params())(*ins)


def ada_grad(act_all, dmod_cols):
    def body(a_ref, d_ref, o_ref):
        o_ref[...] = lax.dot_general(a_ref[...], d_ref[...], (((0,), (0,)), ((), ())),
                                     preferred_element_type=F32, precision=lax.Precision.HIGHEST)

    out = jax.ShapeDtypeStruct((act_all.shape[1], dmod_cols.shape[1]), F32)
    return pl.pallas_call(body, name="ada_grad", in_specs=[_whole(act_all), _whole(dmod_cols)],
                          out_specs=_whole(out), out_shape=out, compiler_params=_params())(act_all, dmod_cols)


def _adam_math(w, g, m, v):
    m2 = ADAM_B1 * m + (1.0 - ADAM_B1) * g
    v2 = ADAM_B2 * v + (1.0 - ADAM_B2) * (g * g)
    m_hat = m2 / (1.0 - ADAM_B1 ** ADAM_STEP)
    v_hat = v2 / (1.0 - ADAM_B2 ** ADAM_STEP)
    delta = -ADAM_LR * (m_hat / (jnp.sqrt(v_hat) + ADAM_EPS) + ADAM_WD * w)
    return delta, m2, v2


def adam(name, w, g, m, v):
    r, c = w.shape
    tr = max(t for t in range(8, min(r, 512) + 1, 8) if r % t == 0)

    def body(w_ref, g_ref, m_ref, v_ref, d_o, m_o, v_o):
        d, m2, v2 = _adam_math(w_ref[...], g_ref[...], m_ref[...], v_ref[...])
        d_o[...] = d
        m_o[...] = m2
        v_o[...] = v2

    spec = pl.BlockSpec((tr, c), lambda i: (i, 0))
    out = jax.ShapeDtypeStruct((r, c), F32)
    return pl.pallas_call(body, name=name, grid=(r // tr,), in_specs=[spec] * 4, out_specs=[spec] * 3,
                          out_shape=[out] * 3, compiler_params=_params(("parallel",)))(w, g, m, v)


def adam_many(name, ws, gs, ms, vs):
    n = len(ws)

    def body(*refs):
        ins, outs = refs[:4 * n], refs[4 * n:]
        for q in range(n):
            d, m2, v2 = _adam_math(ins[q][...], ins[n + q][...], ins[2 * n + q][...], ins[3 * n + q][...])
            outs[q][...] = d
            outs[n + q][...] = m2
            outs[2 * n + q][...] = v2

    operands = list(ws) + list(gs) + list(ms) + list(vs)
    outs = [jax.ShapeDtypeStruct(w.shape, F32) for w in ws] * 3
    return pl.pallas_call(body, name=name, in_specs=[_whole(a) for a in operands],
                          out_specs=[_whole(o) for o in outs], out_shape=outs, compiler_params=_params())(*operands)


def _rows_tile(r, most):
    best = None
    for t in range(16, min(r, most) + 1, 16):
        if r % t == 0:
            best = t
    assert best is not None, r
    return best


def sum_slots(name, slots, out_dtype=F32):
    n, r, c = slots.shape
    tr = _rows_tile(r, max(16, (2 * 1024 * 1024) // (n * c)))

    def body(s_ref, o_ref):
        acc = s_ref[0].astype(F32)
        for q in range(1, n):
            acc = acc + s_ref[q].astype(F32)
        o_ref[...] = acc.astype(o_ref.dtype)

    return pl.pallas_call(body, name=name, grid=(r // tr,),
                          in_specs=[pl.BlockSpec((n, tr, c), lambda i: (0, i, 0))],
                          out_specs=pl.BlockSpec((tr, c), lambda i: (i, 0)),
                          out_shape=jax.ShapeDtypeStruct((r, c), out_dtype), compiler_params=_params(("parallel",)))(slots)


HBM_SPEC = pl.BlockSpec(memory_space=pltpu.HBM)


def _coords():
    return lax.axis_index("x"), lax.axis_index("y"), lax.axis_index("c")


def _linear(x, y, c):
    return 4 * x + 2 * y + c


def all_gather(name, shards, deps=()):
    nq, nd = len(shards), len(deps)

    def body(*refs):
        xs, outs = refs[:nq], refs[nq + nd:2 * nq + nd]
        send_sems, recv_sems, local_sems = refs[2 * nq + nd:2 * nq + nd + 3]
        bufs = refs[2 * nq + nd + 3:]
        x, y, cc = _coords()
        me, sibling = (x, y, cc), (x, y, 1 - cc)
        chips = [(1 - x, y), (x, 1 - y), (1 - x, 1 - y)]

        def slot(q, px, py, pc):
            return outs[q].at[_linear(px, py, pc)]

        def copy(q, k, block, to, src=None):
            return pltpu.make_async_remote_copy(
                src_ref=slot(q, *block) if src is None else src, dst_ref=slot(q, *block),
                send_sem=send_sems.at[7 * q + k], recv_sem=recv_sems.at[7 * q + k], device_id=to, device_id_type=MESH)

        loads = [pltpu.make_async_copy(xs[q], bufs[q], local_sems.at[q]) for q in range(nq)]
        for cp in loads:
            cp.start()
        for cp in loads:
            cp.wait()
        mine = [pltpu.make_async_copy(bufs[q], slot(q, *me), local_sems.at[q]) for q in range(nq)]
        first = []
        for q in range(nq):
            first.append(copy(q, 0, me, sibling, src=bufs[q]))
            first += [copy(q, 1 + j, me, (*chip, cc), src=bufs[q]) for j, chip in enumerate(chips)]
        for cp in mine + first:
            cp.start()
        passed = []
        for q in range(nq):
            for j, chip in enumerate(chips):
                copy(q, 1 + j, (*chip, cc), me).wait_recv()
                passed.append(copy(q, 4 + j, (*chip, cc), sibling))
                passed[-1].start()
        for q in range(nq):
            copy(q, 0, sibling, me).wait_recv()
            for j, chip in enumerate(chips):
                copy(q, 4 + j, (*chip, 1 - cc), me).wait_recv()
        for cp in first + passed:
            cp.wait_send()
        for cp in mine:
            cp.wait()

    return pl.pallas_call(
        body, name=name, in_specs=[HBM_SPEC] * nq + [ANY_SPEC] * nd, out_specs=[HBM_SPEC] * nq,
        out_shape=[jax.ShapeDtypeStruct((NDEV,) + s.shape, s.dtype) for s in shards],
        scratch_shapes=[pltpu.SemaphoreType.DMA((7 * nq,)), pltpu.SemaphoreType.DMA((7 * nq,)),
                        pltpu.SemaphoreType.DMA((nq,))] + [pltpu.VMEM(s.shape, s.dtype) for s in shards],
    )(*shards, *deps)


NCHIP = 4


SEM_SPEC = pl.BlockSpec(memory_space=pltpu.SEMAPHORE)
EFFECT = pltpu.SideEffectType.DATAFLOW_SIDE_EFFECTING


def _peer(x, y, cc, k):
    fx, fy, fc = (k >> 2) & 1, (k >> 1) & 1, k & 1
    return (x + fx - 2 * fx * x, y + fy - 2 * fy * y, cc + fc - 2 * fc * cc)


def gather_plan(srcs, lands, coords):
    x, y, cc = coords
    me = _linear(x, y, cc)
    return [(s, l.at[me], _peer(x, y, cc, k)) for s, l in zip(srcs, lands) for k in range(1, NDEV)]


def near_plan(srcs, lands, coords):
    x, y, cc = coords
    me = _linear(x, y, cc)
    peers = [(x, y, 1 - cc)] + [_peer(x, y, cc, 2 * k) for k in range(1, NCHIP)]
    return [(s, l.at[me], p) for s, l in zip(srcs, lands) for p in peers]


def pass_on_plan(srcs, lands, coords):
    x, y, cc = coords
    out = []
    for l in srcs:
        for k in range(1, NCHIP):
            px, py, _ = _peer(x, y, cc, 2 * k)
            slot = _linear(px, py, cc)
            out.append((l.at[slot], l.at[slot], (x, y, 1 - cc)))
    return out


def pair_plan(srcs, lands, coords):
    x, y, cc = coords
    return [(s.at[2 * chip + 1 - cc], l.at[chip], (x, y, 1 - cc)) for s, l in zip(srcs, lands) for chip in range(NCHIP)]


def chip_plan(srcs, lands, coords):
    x, y, cc = coords
    out = []
    for s, l in zip(srcs, lands):
        for k in range(1, NCHIP):
            px, py, _ = _peer(x, y, cc, 2 * k)
            out.append((s.at[2 * px + py], l.at[k - 1], (px, py, cc)))
    return out


def _remote(copy, i, send_sems, recv_sems):
    src, dst, dev = copy
    return pltpu.make_async_remote_copy(src_ref=src, dst_ref=dst, send_sem=send_sems.at[i], recv_sem=recv_sems.at[i],
                                        device_id=dev, device_id_type=MESH)


def exchange_start(name, plan, ncopy, srcs, land_shapes, deps=()):
    ns, nl, nd = len(srcs), len(land_shapes), len(deps)

    def body(*refs):
        s, l = refs[:ns], refs[ns:ns + nl]
        send_sems, recv_sems = refs[ns + nl + nd], refs[ns + nl + nd + 1]
        token = refs[-1]
        for i, cp in enumerate(plan(s, l, _coords())):
            _remote(cp, i, send_sems, recv_sems).start()
        token[...] = jnp.zeros_like(token)

    hbm = lambda a: pltpu.with_memory_space_constraint(a, pltpu.HBM)
    lands = [lax.empty(shp, dt) for shp, dt in land_shapes]
    thru = [pltpu.HBM(a.shape, a.dtype) for a in list(srcs) + lands]
    outs = pl.pallas_call(
        body, name=name,
        in_specs=[HBM_SPEC] * (ns + nl) + [ANY_SPEC] * nd,
        out_specs=(SEM_SPEC, SEM_SPEC, *[HBM_SPEC] * (ns + nl), pl.BlockSpec(memory_space=pltpu.VMEM)),
        out_shape=(pltpu.SemaphoreType.DMA((ncopy,)), pltpu.SemaphoreType.DMA((ncopy,)), *thru,
                   jax.ShapeDtypeStruct((8, LANE), F32)),
        input_output_aliases={i: 2 + i for i in range(ns + nl)},
        compiler_params=pltpu.CompilerParams(has_side_effects=EFFECT),
    )(*[hbm(a) for a in srcs], *[hbm(a) for a in lands], *deps)
    return outs[0], outs[1], list(outs[2:2 + ns]), list(outs[2 + ns:2 + ns + nl]), outs[-1]


def exchange_wait(name, plan, started, after, place_own=False):
    send_sems, recv_sems, srcs, lands, _ = started
    ns, nl = len(srcs), len(lands)

    def body(*refs):
        s, l = refs[:ns], refs[ns:ns + nl]
        send_sems, recv_sems = refs[ns + nl], refs[ns + nl + 1]
        l_out = refs[2 * ns + nl + 3:2 * ns + 2 * nl + 3]
        scratch = refs[2 * ns + 2 * nl + 3:]
        copies = [_remote(cp, i, send_sems, recv_sems) for i, cp in enumerate(plan(s, l, _coords()))]
        if place_own:
            me = _linear(*_coords())
            local_sems, bufs = scratch[0], scratch[1:]
            loads = [pltpu.make_async_copy(s[q], bufs[q], local_sems.at[q]) for q in range(ns)]
            for cp in loads:
                cp.start()
            for cp in loads:
                cp.wait()
            stores = [pltpu.make_async_copy(bufs[q], l_out[q].at[me], local_sems.at[q]) for q in range(ns)]
            for cp in stores:
                cp.start()
        for cp in copies:
            cp.wait_recv()
        for cp in copies:
            cp.wait_send()
        if place_own:
            for cp in stores:
                cp.wait()

    scratch_shapes = []
    if place_own:
        scratch_shapes = [pltpu.SemaphoreType.DMA((ns,))] + [pltpu.VMEM(a.shape, a.dtype) for a in srcs]
    outs = pl.pallas_call(
        body, name=name,
        in_specs=[HBM_SPEC] * (ns + nl) + [SEM_SPEC, SEM_SPEC, ANY_SPEC],
        out_specs=[HBM_SPEC] * (ns + nl),
        out_shape=[pltpu.HBM(a.shape, a.dtype) for a in srcs + lands],
        input_output_aliases={i: i for i in range(ns + nl)},
        scratch_shapes=scratch_shapes,
        compiler_params=pltpu.CompilerParams(has_side_effects=EFFECT),
    )(*srcs, *lands, send_sems, recv_sems, after)
    return list(outs[:ns]), list(outs[ns:])


def pair_sum(name, g, recv):
    _, r, c = g.shape
    tr = _rows_tile(r, 512)

    def body(g_ref, r_ref, o_ref):
        own = jnp.where(lax.axis_index("c") == 0, g_ref[0, 0], g_ref[0, 1])
        o_ref[0] = (own.astype(F32) + r_ref[0].astype(F32)).astype(o_ref.dtype)

    return pl.pallas_call(
        body, name=name, grid=(NCHIP, r // tr),
        in_specs=[pl.BlockSpec((1, 2, tr, c), lambda k, i: (k, 0, i, 0)), pl.BlockSpec((1, tr, c), lambda k, i: (k, i, 0))],
        out_specs=pl.BlockSpec((1, tr, c), lambda k, i: (k, i, 0)),
        out_shape=jax.ShapeDtypeStruct((NCHIP, r, c), g.dtype), compiler_params=_params(("parallel", "parallel")),
    )(g.reshape(NCHIP, 2, r, c), recv)


def chip_sum(name, partial, recv):
    _, r, c = partial.shape
    tr = _rows_tile(r, 512)

    def body(p_ref, r_ref, o_ref):
        chip = 2 * lax.axis_index("x") + lax.axis_index("y")
        own = p_ref[0]
        for k in range(1, NCHIP):
            own = jnp.where(chip == k, p_ref[k], own)
        acc = own.astype(F32)
        for k in range(NCHIP - 1):
            acc = acc + r_ref[k].astype(F32)
        o_ref[...] = acc

    return pl.pallas_call(
        body, name=name, grid=(r // tr,),
        in_specs=[pl.BlockSpec((NCHIP, tr, c), lambda i: (0, i, 0)), pl.BlockSpec((NCHIP - 1, tr, c), lambda i: (0, i, 0))],
        out_specs=pl.BlockSpec((tr, c), lambda i: (i, 0)),
        out_shape=jax.ShapeDtypeStruct((r, c), F32), compiler_params=_params(("parallel",)),
    )(partial, recv)


def _block_diag(w, rows_per, cols_per):
    w = w.reshape(NBLK, 8, rows_per, cols_per)
    eye = jnp.eye(8, dtype=w.dtype)
    out = w[:, :, :, None, :] * eye[None, :, None, :, None]
    return out.reshape(NBLK, 8 * rows_per, 8 * cols_per)


def _diag_blocks(wd, rows_per, cols_per):
    wd = wd.reshape(NBLK, 8, rows_per, 8, cols_per)
    idx = jnp.arange(8)
    return wd[:, idx, :, idx, :].transpose(1, 0, 2, 3).reshape(NG, rows_per, cols_per)


def _pad_rows(v, mult):
    n = v.shape[0]
    return jnp.pad(v, (0, (-n) % mult))


def kernel(x, c, w_ada, b_ada, norm1_g, w_in, conv_w, conv_b, conv_ln_g, conv_ln_b, conv_proj, ssm_a_re, ssm_a_im, ssm_b_re, ssm_b_im, ssm_c_re, ssm_c_im, ssm_d, ssm_log_dt, ssm_glu, w_out, norm2_g, w_ffn_in, w_ffn_out, final_g, loss_target, m_w_ada, m_b_ada, m_norm1_g, m_w_in, m_conv_w, m_conv_b, m_conv_ln_g, m_conv_ln_b, m_conv_proj, m_ssm_a_re, m_ssm_a_im, m_ssm_b_re, m_ssm_b_im, m_ssm_c_re, m_ssm_c_im, m_ssm_d, m_ssm_log_dt, m_ssm_glu, m_w_out, m_norm2_g, m_w_ffn_in, m_w_ffn_out, m_final_g, v_w_ada, v_b_ada, v_norm1_g, v_w_in, v_conv_w, v_conv_b, v_conv_ln_g, v_conv_ln_b, v_conv_proj, v_ssm_a_re, v_ssm_a_im, v_ssm_b_re, v_ssm_b_im, v_ssm_c_re, v_ssm_c_im, v_ssm_d, v_ssm_log_dt, v_ssm_glu, v_w_out, v_norm2_g, v_w_ffn_in, v_w_ffn_out, v_final_g):
    me = _linear(*_coords())
    xs = x[0]
    tgt = loss_target[0]
    seq = xs.shape[0]

    flat = lambda g: g.reshape(NDEV * g.shape[1], g.shape[2])
    w_in_s = w_in[0].T.astype(BF16)
    mids = [p.astype(BF16) for p in (conv_proj[0].T, ssm_glu[0].T, w_out[0])]
    ffns = [p.astype(BF16) for p in (w_ffn_in[0].T, w_ffn_out[0])]
    zone = lambda p: ((NDEV,) + p.shape, p.dtype)
    c_all, cw_g = all_gather("gather_c_conv_w", [c, conv_w[0]])
    in_go = exchange_start("gather_in_start", near_plan, NCHIP, [w_in_s], [zone(w_in_s)], deps=[c_all])
    mids_go = exchange_start("gather_mid_start", gather_plan, 7 * len(mids), mids, [zone(p) for p in mids],
                             deps=[in_go[4]])
    ffns_go = exchange_start("gather_ffn_start", gather_plan, 7 * len(ffns), ffns, [zone(p) for p in ffns],
                             deps=[mids_go[4]])

    ncol = w_ada.shape[2]
    c_all = c_all.reshape(NDEV, D)
    b_cols = lax.dynamic_slice_in_dim(b_ada, me * ncol, ncol, axis=1)
    act_all, mod_cols = mod_fwd(c_all, w_ada[0], b_cols)
    (mod_all,) = all_gather("gather_mod", [mod_cols])
    mod = lax.dynamic_index_in_dim(mod_all, me, axis=1, keepdims=False).reshape(NMOD, D)
    sh1, sc1, g1, sh2, sc2, g2 = [mod[q:q + 1] for q in range(NMOD)]

    expand = jnp.repeat(jnp.eye(NG, dtype=F32), NP, axis=0)
    a_re_c, a_im_c = ssm_a_re.reshape(NST, 1), ssm_a_im.reshape(NST, 1)
    ldt_c = ssm_log_dt.reshape(NG, 1)
    b_re_r, b_im_r = ssm_b_re.reshape(NST, GH), ssm_b_im.reshape(NST, GH)
    e_re, e_im, bb_re, bb_im = disc_fwd(a_re_c, a_im_c, ldt_c, b_re_r, b_im_r, expand)
    e_re_b, e_im_b = e_re.reshape(NBLK, 1, SB), e_im.reshape(NBLK, 1, SB)
    bb_re_g, bb_im_g = bb_re.reshape(NG, NP, GH), bb_im.reshape(NG, NP, GH)
    wbt_re = _block_diag(bb_re_g, NP, GH)
    wbt_im = _block_diag(bb_im_g, NP, GH)
    wb_re, wb_im = wbt_re.transpose(0, 2, 1), wbt_im.transpose(0, 2, 1)
    wct = jnp.concatenate([_block_diag(ssm_c_re[0], GH, NP), -_block_diag(ssm_c_im[0], GH, NP)], axis=2)
    wc = wct.transpose(0, 2, 1)
    to_b = lambda a: a.astype(BF16)
    dvec = ssm_d.reshape(NBLK, 1, CB)

    n1g = norm1_g

    def f_norm1(xv, g, sc, sh):
        _, xh = _rms_stats(xv)
        return [xh * g * (1.0 + sc) + sh], []

    (h1,) = rowwise("norm1", f_norm1, [xs], [n1g, sc1, sh1], [(D, BF16)], [], 512, deps=[ffns_go[4]])
    _, (w_in_land,) = exchange_wait("gather_in_wait", near_plan, in_go, h1, place_own=True)
    pass_go = exchange_start("gather_in_pass_start", pass_on_plan, NCHIP - 1, [w_in_land], [])
    (w_in_g,), _ = exchange_wait("gather_in_pass_wait", pass_on_plan, pass_go, pass_go[4])
    w_in_t = flat(w_in_g)
    z = mm("mm_in", h1, w_in_t, "nt", tiles=(2048, CW, 1024), b_rot=Z_ROT)

    conv_w_full = cw_g.transpose(1, 0, 2).reshape(KC, CW)
    w32 = jnp.pad(conv_w_full, ((0, HALO - KC), (0, 0)))
    yc, s_act = conv_fwd(z, w32, conv_b, conv_ln_g, conv_ln_b)
    conv_proj_t, ssm_glu_t, w_out_f = [
        flat(g) for g in exchange_wait("gather_mid_wait", gather_plan, mids_go, s_act, place_own=True)[1]]
    y_conv = mm("mm_conv_proj", s_act, conv_proj_t, "nt")

    xs_re, xs_im, ypre, gl = ssm_fwd(z, to_b(wb_re), to_b(wb_im), to_b(wc), e_re_b, e_im_b, dvec)
    n_mrg = D // MRG_BLK

    pair_of = lambda t, n: t // 2 + (t % 2) * n

    def ep_merge(accs, yc_v, gates):
        za, zb = accs
        glc, gls = gates[:, 0:MRG_BLK], gates[:, MRG_BLK:2 * MRG_BLK]
        return [_sig(glc) * yc_v + _sig(gls) * (za * _sig(zb)), jnp.concatenate([za, zb], axis=1)]

    merged, z2_pair = mm_ep("mm_ssm_glu", gl, ssm_glu_t, 2, lambda j, q: j + q * n_mrg, ep_merge,
                            [(y_conv, 1, 0), (z, 2, 0)], [(D, BF16, 1), (2 * D, BF16, 2)], (512, MRG_BLK, SW))
    row_tiles = lambda bk: (512, D, bk)
    whole = lambda j, q: j

    def ep_norm2(accs, xv, g1v, g, sc, sh):
        (o1v,) = accs
        x1v = xv + g1v * o1v
        _, xh = _rms_stats(x1v)
        return [x1v, xh * g * (1.0 + sc) + sh, o1v]

    x1, h2, o1 = mm_ep("mm_out", merged, w_out_f, 1, whole, ep_norm2, [(xs, 1, 0)],
                       [(D, F32, 1), (D, BF16, 1), (D, BF16, 1)], row_tiles(D), b_kn=True,
                       consts=[g1, norm2_g, sc2, sh2])
    w_ffn_in_t, w_ffn_out_f = [
        flat(g) for g in exchange_wait("gather_ffn_wait", gather_plan, ffns_go, h2, place_own=True)[1]]
    ffn_tiles = (512, FFN_BLK, 1024)
    n_ffn_blk = FH // FFN_BLK
    pair_map = lambda t: t // 2 + (t % 2) * n_ffn_blk

    def ep_swiglu(accs):
        fg, fu = accs
        return [fg * _sig(fg) * fu, jnp.concatenate([fg, fu], axis=1)]

    act, f_pair = mm_ep("mm_ffn_in", h2, w_ffn_in_t, 2, lambda j, q: j + q * n_ffn_blk, ep_swiglu, [],
                        [(FH, BF16, 1), (2 * FH, BF16, 2)], ffn_tiles)
    fg_row = final_g.reshape(1, D)

    def ep_final(accs, x1v, tv, g2v, fg):
        (o2v,) = accs
        x2v = x1v + g2v * o2v
        r, xh = _rms_stats(x2v)
        yv = xh * fg
        err = yv - tv
        loss = jnp.sum(_colsum(err * err), axis=1, keepdims=True) * (0.5 / D)
        dy = err * (1.0 / D)
        dx2 = _rms_bwd(dy * fg, xh, r)
        return ([dx2, g2v * dx2],
                [jnp.broadcast_to(loss, (1, LANE)), _colsum(dy * xh), _colsum(dx2 * o2v)])

    dx2, do2, loss_l, d_final_g, d_g2 = mm_ep_pipe(
        "mm_ffn_out", act, w_ffn_out_f, 1, whole, ep_final, [(x1, 1, 0), (tgt, 1, 0)],
        [(D, F32, 1), (D, BF16, 1)], row_tiles(FH), b_kn=True, consts=[g2, fg_row], sums=[LANE, D, D])

    g_ffn_out = mm("mm_g_ffn_out", act, do2, "tn", BF16, tiles=(FFN_BLK, 1024, 1024))

    def ep_dswiglu(accs, fp):
        (da,) = accs
        fg, fu = fp[:, 0:FFN_BLK].astype(F32), fp[:, FFN_BLK:2 * FFN_BLK].astype(F32)
        sg = _sig(fg)
        return [jnp.concatenate([da * fu * (sg * (1.0 + fg * (1.0 - sg))), da * (fg * sg)], axis=1)]

    (df,) = mm_ep("mm_dact", do2, w_ffn_out_f, 1, lambda j, q: j, ep_dswiglu, [(f_pair, 2, 0)],
                  [(2 * FH, BF16, 2)], ffn_tiles)
    g_ffn_in_t = mm("mm_g_ffn_in", df, h2, "tn", BF16, tiles=(FFN_BLK, 1024, 1024), o_rot=pair_map)

    def pair_go(tag, grads_t, deps=()):
        srcs = [g.reshape(NDEV, -1, D) for g in grads_t]
        return exchange_start("pair_" + tag + "_start", pair_plan, NCHIP * len(srcs), srcs,
                              [((NCHIP,) + s.shape[1:], s.dtype) for s in srcs], deps)

    def chip_go(tag, names, pair_started, after):
        own, from_sibling = exchange_wait("pair_" + tag + "_wait", pair_plan, pair_started, after)
        partials = [pair_sum("pair_sum_" + n, g, r) for n, g, r in zip(names, own, from_sibling)]
        return exchange_start("chip_" + tag + "_start", chip_plan, (NCHIP - 1) * len(partials), partials,
                              [((NCHIP - 1,) + p.shape[1:], p.dtype) for p in partials])

    def chip_done(tag, names, chip_started, after):
        partials, from_chips = exchange_wait("chip_" + tag + "_wait", chip_plan, chip_started, after)
        return [chip_sum("chip_sum_" + n, p, r) for n, p, r in zip(names, partials, from_chips)]

    pair_ffn = pair_go("ffn", [g_ffn_out, g_ffn_in_t])

    dh2 = mm("mm_dh2", df, w_ffn_in_t, "nn", BF16, tiles=(1024, 1024, FFN_BLK), b_rot=pair_map,
             deps=[pair_ffn[4]])

    def f_dnorm2(dh, x1v, dx2v, o1v, g, sc, g1v):
        dh, o1v = dh.astype(F32), o1v.astype(F32)
        r, xh = _rms_stats(x1v)
        dxh = dh * (1.0 + sc) * g
        dx1 = dx2v + _rms_bwd(dxh, xh, r)
        return ([dx1, g1v * dx1],
                [_colsum(dh * xh * g), _colsum(dh), _colsum(dh * (1.0 + sc) * xh), _colsum(dx1 * o1v)])

    dx1, do1, d_sc2, d_sh2, d_n2g, d_g1 = rowwise(
        "dnorm2", f_dnorm2, [dh2, x1, dx2, o1], [norm2_g, sc2, g1], [(D, F32), (D, BF16)], [D, D, D, D], 256)

    g_out = mm("mm_g_out", merged, do1, "tn", BF16)
    chip_ffn = chip_go("ffn", ("w_ffn_out", "w_ffn_in"), pair_ffn, g_out)

    def ep_dmerge(accs, yc_v, z2p, gates):
        (dm,) = accs
        za, zb = z2p[:, 0:MRG_BLK].astype(F32), z2p[:, MRG_BLK:2 * MRG_BLK].astype(F32)
        sc_, ss_, sb_ = _sig(gates[:, 0:MRG_BLK]), _sig(gates[:, MRG_BLK:2 * MRG_BLK]), _sig(zb)
        dys = dm * ss_
        dz2 = jnp.concatenate([dys * sb_, dys * za * sb_ * (1.0 - sb_)], axis=1)
        dgates = jnp.concatenate([dm * yc_v * sc_ * (1.0 - sc_), dm * (za * sb_) * ss_ * (1.0 - ss_)], axis=1)
        return [dm * sc_, dz2, dgates]

    dyconv, dz2, dz = mm_ep("mm_dmerged", do1, w_out_f, 1, lambda j, q: j, ep_dmerge,
                            [(y_conv, 1, 0), (z2_pair, 2, 0), (z, 2, 0)],
                            [(D, BF16, 1), (2 * D, BF16, 2), (ZW, BF16, 2)], (512, MRG_BLK, 1024), deps=[chip_ffn[4]])

    g_conv_proj_t = mm("mm_g_conv_proj", dyconv, s_act, "tn", BF16)
    mrg_map = lambda t: pair_of(t, n_mrg)
    dgl = mm("mm_dgl", dz2, ssm_glu_t, "nn", tiles=(1024, SW, MRG_BLK), b_rot=mrg_map)
    g_ssm_glu_t = mm("mm_g_ssm_glu", dz2, gl, "tn", BF16, tiles=(MRG_BLK, SW, 1024), o_rot=mrg_map)
    pair_mid = pair_go("mid", [g_out, g_conv_proj_t, g_ssm_glu_t])
    ds = mm("mm_ds", dyconv, conv_proj_t, "nn", deps=[pair_mid[4]])
    dz, d_lng, d_lnb, d_cb, d_cw32 = conv_bwd(ds, yc, z, w32, conv_ln_g, conv_ln_b, dz)
    dz, d_d, d_ar, d_ai, d_wb_re, d_wb_im, d_wc = ssm_bwd(
        dgl, ypre, z, xs_re, xs_im, to_b(wbt_re), to_b(wbt_im), to_b(wct), e_re_b, e_im_b, dvec, dz)
    chip_mid = chip_go("mid", ("w_out", "conv_proj", "ssm_glu"), pair_mid, dz)

    d_bb_re = _diag_blocks(d_wb_re.transpose(0, 2, 1), NP, GH).reshape(NST, GH)
    d_bb_im = _diag_blocks(d_wb_im.transpose(0, 2, 1), NP, GH).reshape(NST, GH)
    d_wct = d_wc.transpose(0, 2, 1)
    d_c_re = _diag_blocks(d_wct[:, :, 0:SB], GH, NP)
    d_c_im = -_diag_blocks(d_wct[:, :, SB:2 * SB], GH, NP)
    d_a_re, d_a_im, d_ldt, d_b_re, d_b_im = disc_bwd(
        a_re_c, a_im_c, ldt_c, b_re_r, b_im_r, expand, d_ar.reshape(NST, 1), d_ai.reshape(NST, 1), d_bb_re, d_bb_im)

    small_local = [jnp.concatenate([d_g1, d_sh2, d_sc2, d_g2], axis=1).reshape(-1), d_cw32[0:KC].reshape(-1),
                   d_cb.reshape(-1), d_lng.reshape(-1), d_lnb.reshape(-1), d_a_re.reshape(-1), d_a_im.reshape(-1),
                   d_b_re.reshape(-1), d_b_im.reshape(-1), d_c_re.reshape(-1), d_c_im.reshape(-1), d_d.reshape(-1),
                   d_ldt.reshape(-1), d_n2g.reshape(-1), d_final_g.reshape(-1), loss_l[0, 0:1]]
    small_sizes = [v.shape[0] for v in small_local]
    small_pack = _pad_rows(jnp.concatenate(small_local), 256 * LANE).reshape(-1, LANE)
    small_go = exchange_start("gather_small_start", gather_plan, NDEV - 1, [small_pack],
                              [((NDEV,) + small_pack.shape, F32)], deps=[chip_mid[4]])

    g_in_t = mm("mm_g_in", dz, h1, "tn", BF16, tiles=(CW, 1024, 2048), o_rot=Z_ROT, deps=[small_go[4]])
    pair_in = pair_go("in", [g_in_t])

    dh1 = mm("mm_dh1", dz, w_in_t, "nn", BF16, tiles=(2048, 1024, CW), b_rot=Z_ROT, deps=[pair_in[4]])

    def f_dnorm1(dh, xv, dx1v, g, sc):
        dh = dh.astype(F32)
        r, xh = _rms_stats(xv)
        dxh = dh * (1.0 + sc) * g
        return ([dx1v + _rms_bwd(dxh, xh, r)],
                [_colsum(dh * xh * g), _colsum(dh), _colsum(dh * (1.0 + sc) * xh)])

    grad_x, d_sc1, d_sh1, d_n1g = rowwise(
        "dnorm1", f_dnorm1, [dh1, xs, dx1], [n1g, sc1], [(D, F32)], [D, D, D], 256)
    chip_in = chip_go("in", ("w_in",), pair_in, grad_x)

    weights = {
        "w_ada": (w_ada, m_w_ada, v_w_ada), "b_ada": (b_ada, m_b_ada, v_b_ada), "norm1_g": (norm1_g, m_norm1_g, v_norm1_g),
        "w_in": (w_in, m_w_in, v_w_in), "conv_w": (conv_w, m_conv_w, v_conv_w), "conv_b": (conv_b, m_conv_b, v_conv_b),
        "conv_ln_g": (conv_ln_g, m_conv_ln_g, v_conv_ln_g), "conv_ln_b": (conv_ln_b, m_conv_ln_b, v_conv_ln_b),
        "conv_proj": (conv_proj, m_conv_proj, v_conv_proj), "ssm_a_re": (ssm_a_re, m_ssm_a_re, v_ssm_a_re),
        "ssm_a_im": (ssm_a_im, m_ssm_a_im, v_ssm_a_im), "ssm_b_re": (ssm_b_re, m_ssm_b_re, v_ssm_b_re),
        "ssm_b_im": (ssm_b_im, m_ssm_b_im, v_ssm_b_im), "ssm_c_re": (ssm_c_re, m_ssm_c_re, v_ssm_c_re),
        "ssm_c_im": (ssm_c_im, m_ssm_c_im, v_ssm_c_im), "ssm_d": (ssm_d, m_ssm_d, v_ssm_d),
        "ssm_log_dt": (ssm_log_dt, m_ssm_log_dt, v_ssm_log_dt), "ssm_glu": (ssm_glu, m_ssm_glu, v_ssm_glu),
        "w_out": (w_out, m_w_out, v_w_out), "norm2_g": (norm2_g, m_norm2_g, v_norm2_g),
        "w_ffn_in": (w_ffn_in, m_w_ffn_in, v_w_ffn_in), "w_ffn_out": (w_ffn_out, m_w_ffn_out, v_w_ffn_out),
        "final_g": (final_g, m_final_g, v_final_g),
    }
    order = list(weights)
    big = ("w_ada", "w_in", "conv_proj", "ssm_glu", "w_out", "w_ffn_in", "w_ffn_out")
    grads, delta, new_m, new_v = {}, {}, {}, {}

    def adam_big(n, g2d, transposed=False):
        wv, mv, vv = weights[n]
        shp = wv.shape
        t_in = (lambda a: a.reshape(shp[-2:]).T) if transposed else (lambda a: a.reshape(shp[-2:]))
        t_out = (lambda a: a.T.reshape(shp)) if transposed else (lambda a: a.reshape(shp))
        d_, m_, v_ = adam("adam_" + n, t_in(wv), g2d, t_in(mv), t_in(vv))
        grads[n], delta[n], new_m[n], new_v[n] = t_out(g2d), t_out(d_), t_out(m_), t_out(v_)
        return d_

    gs_ffn_out, gs_ffn_in = chip_done("ffn", ("w_ffn_out", "w_ffn_in"), chip_ffn, chip_in[4])
    adam_big("w_ffn_out", gs_ffn_out)
    last = adam_big("w_ffn_in", gs_ffn_in, transposed=True)
    gs_out, gs_conv_proj, gs_ssm_glu = chip_done("mid", ("w_out", "conv_proj", "ssm_glu"), chip_mid, last)
    adam_big("w_out", gs_out)
    adam_big("conv_proj", gs_conv_proj.reshape(-1, CW), transposed=True)
    adam_big("ssm_glu", gs_ssm_glu.reshape(-1, SW), transposed=True)

    late_local = [d_sh1.reshape(-1), d_sc1.reshape(-1), d_n1g.reshape(-1)]
    late_pack = _pad_rows(jnp.concatenate(late_local), 16 * LANE).reshape(-1, LANE)
    (late_all,) = all_gather("gather_small_late", [late_pack])
    _, (small_all,) = exchange_wait("gather_small_wait", gather_plan, small_go, late_all, place_own=True)

    def unpack(vec, sizes):
        out, pos = [], 0
        for n in sizes:
            out.append(vec[pos:pos + n])
            pos += n
        return out

    g_sh1, g_sc1, g_n1g = unpack(sum_slots("sum_small_late", late_all).reshape(-1), [D, D, D])
    (g_mod_rest, g_cw_full, g_cb, g_lng, g_lnb, g_a_re, g_a_im, g_b_re, g_b_im, g_c_re, g_c_im, g_d, g_ldt,
     g_n2g, g_fg, loss_sum) = unpack(sum_slots("sum_small", small_all).reshape(-1), small_sizes)
    g_b_ada = jnp.concatenate([g_sh1, g_sc1, g_mod_rest])
    loss = loss_sum[0]
    dmod_all = jnp.concatenate([late_all.reshape(NDEV, -1)[:, 0:2 * D], small_all.reshape(NDEV, -1)[:, 0:4 * D]],
                               axis=1)
    g_w_ada = ada_grad(act_all, lax.dynamic_slice_in_dim(dmod_all, me * ncol, ncol, axis=1))
    ccol = conv_w.shape[2]
    g_conv_w = lax.dynamic_slice_in_dim(g_cw_full.reshape(KC, CW), me * ccol, ccol, axis=1)

    adam_big("w_ada", g_w_ada)
    grads.update({
        "b_ada": g_b_ada.reshape(b_ada.shape), "norm1_g": g_n1g.reshape(norm1_g.shape),
        "conv_w": g_conv_w[None], "conv_b": g_cb.reshape(conv_b.shape),
        "conv_ln_g": g_lng.reshape(conv_ln_g.shape), "conv_ln_b": g_lnb.reshape(conv_ln_b.shape),
        "ssm_a_re": g_a_re.reshape(ssm_a_re.shape),
        "ssm_a_im": g_a_im.reshape(ssm_a_im.shape), "ssm_b_re": g_b_re.reshape(ssm_b_re.shape),
        "ssm_b_im": g_b_im.reshape(ssm_b_im.shape), "ssm_c_re": g_c_re.reshape(ssm_c_re.shape),
        "ssm_c_im": g_c_im.reshape(ssm_c_im.shape), "ssm_d": g_d.reshape(ssm_d.shape),
        "ssm_log_dt": g_ldt.reshape(ssm_log_dt.shape),
        "norm2_g": g_n2g.reshape(norm2_g.shape),
        "final_g": g_fg.reshape(final_g.shape),
    })
    small = [n for n in order if n not in big]
    def rows(a):
        if a.ndim == 4 and a.shape[-1] < a.shape[-2]:
            a = a.swapaxes(-1, -2)
        return a.reshape(1, -1) if a.ndim == 1 else a.reshape(-1, a.shape[-1])

    def unrows(a, shp):
        if len(shp) == 4 and shp[-1] < shp[-2]:
            return a.reshape(shp[:-2] + (shp[-1], shp[-2])).swapaxes(-1, -2)
        return a.reshape(shp)

    small_out = adam_many("adam_small", [rows(weights[n][0]) for n in small], [rows(grads[n]) for n in small],
                          [rows(weights[n][1]) for n in small], [rows(weights[n][2]) for n in small])
    for q, n in enumerate(small):
        shp = weights[n][0].shape
        delta[n], new_m[n], new_v[n] = [unrows(small_out[t * len(small) + q], shp) for t in range(3)]

    (gs_in,) = chip_done("in", ("w_in",), chip_in, small_out[0])
    adam_big("w_in", gs_in, transposed=True)

    return (loss, grad_x[None], *[grads[n] for n in order], *[delta[n] for n in order],
            *[new_m[n] for n in order], *[new_v[n] for n in order])
```

```python
import functools
import math

import jax
import jax.numpy as jnp
from jax import lax
from jax.experimental import pallas as pl
from jax.experimental.pallas import tpu as pltpu

F32 = jnp.float32
BF16 = jnp.bfloat16

D = 1024
CW = 512
KC = 31
SW = 512
NG = 32
GH = 16
NP = 64
NST = NG * NP
FH = 2816
FFN_BLK = 1408
MRG_BLK = 1024
NMOD = 6
NDEV = 8
EPS = 1e-6
CB = 128
SB = 512
NBLK = SW // CB
HALO = 32
ZW = 2 * CW + SW + 2 * D
Z_ROT = lambda j: (j + 3) % (ZW // CW)
ZB_A, ZB_G, ZB_U = 4, 5, 6

ADAM_LR = 0.001
ADAM_B1 = 0.9
ADAM_B2 = 0.999
ADAM_EPS = 1e-08
ADAM_WD = 0.01
ADAM_STEP = 10

V7X_VMEM_BYTES = 64 * 1024 * 1024
VMEM_LIMIT = V7X_VMEM_BYTES - 8 * 1024 * 1024
LANE = 128
MESH = pl.DeviceIdType.MESH
ANY_SPEC = pl.BlockSpec(memory_space=pl.ANY)


def _params(sem=None, **kw):
    if sem is not None:
        kw["dimension_semantics"] = sem
    return pltpu.CompilerParams(vmem_limit_bytes=VMEM_LIMIT, **kw)


def _tile(n, most):
    best = None
    for t in range(LANE, most + 1, LANE):
        if n % t == 0:
            best = t
    if best is None:
        raise ValueError(f"no tile for {n}")
    return best


def _sig(x):
    return jax.nn.sigmoid(x)


def mm(name, a, b, mode, out_dtype=F32, tiles=None, b_rot=None, o_rot=None, deps=()):
    if mode == "nn":
        (m, k), (k2, n) = a.shape, b.shape
    elif mode == "nt":
        (m, k), (n, k2) = a.shape, b.shape
    else:
        (k, m), (k2, n) = a.shape, b.shape
    assert k == k2, (name, a.shape, b.shape)
    bm, bn, bk = tiles or (_tile(m, 1024), _tile(n, 1408), _tile(k, 1408 if k % 1408 == 0 else 1024))
    bm, bn, bk = min(bm, m), min(bn, n), min(bk, k)
    assert m % bm == 0 and n % bn == 0 and k % bk == 0, (name, m, n, k, bm, bn, bk)
    nk = k // bk
    rot = lambda idx, r: idx if r is None else r(idx)
    if mode == "nn":
        a_spec = pl.BlockSpec((bm, bk), lambda i, j, kk: (i, kk))
        b_spec = pl.BlockSpec((bk, bn), lambda i, j, kk: (rot(kk, b_rot), j))
        dims = (((1,), (0,)), ((), ()))
    elif mode == "nt":
        a_spec = pl.BlockSpec((bm, bk), lambda i, j, kk: (i, kk))
        b_spec = pl.BlockSpec((bn, bk), lambda i, j, kk: (rot(j, b_rot), kk))
        dims = (((1,), (1,)), ((), ()))
    else:
        assert b_rot is None
        a_spec = pl.BlockSpec((bk, bm), lambda i, j, kk: (kk, i))
        b_spec = pl.BlockSpec((bk, bn), lambda i, j, kk: (kk, j))
        dims = (((0,), (0,)), ((), ()))

    def body(a_ref, b_ref, *rest):
        o_ref, acc_ref = rest[-2:]
        kk = pl.program_id(2)

        @pl.when(kk == 0)
        def _():
            acc_ref[...] = jnp.zeros_like(acc_ref)

        acc_ref[...] += lax.dot_general(a_ref[...], b_ref[...], dims, preferred_element_type=F32)

        @pl.when(kk == nk - 1)
        def _():
            o_ref[...] = acc_ref[...].astype(o_ref.dtype)

    return pl.pallas_call(
        body, name=name,
        grid=(m // bm, n // bn, nk),
        in_specs=[a_spec, b_spec] + [ANY_SPEC] * len(deps),
        out_specs=pl.BlockSpec((bm, bn), lambda i, j, kk: (rot(i, o_rot), j)),
        out_shape=jax.ShapeDtypeStruct((m, n), out_dtype),
        scratch_shapes=[pltpu.VMEM((bm, bn), F32)],
        compiler_params=_params(("parallel", "parallel", "arbitrary")),
    )(a, b, *deps)


def mm_ep(name, a, b, n_acc, acc_block, epilogue, extras, outs, tiles, deps=(), b_kn=False, k_map=None,
          consts=(), sums=()):
    m, k = a.shape
    bm, bn, bk = tiles
    bm = min(bm, m)
    nj = outs[0][0] // (outs[0][2] * bn)
    nk = k // bk
    assert m % bm == 0 and k % bk == 0 and b.shape[0 if b_kn else 1] == k, (name, a.shape, b.shape, tiles)
    assert not sums or nj == 1, name
    ne, nc, no, ns, nd = len(extras), len(consts), len(outs), len(sums), len(deps)
    dims = (((1,), (0,)), ((), ())) if b_kn else (((1,), (1,)), ((), ()))
    kmap = (lambda kk: kk) if k_map is None else k_map

    def body(*refs):
        a_ref, b_refs = refs[0], refs[1:1 + n_acc]
        e_refs = refs[1 + n_acc:1 + n_acc + ne + nc]
        first_out = 1 + n_acc + ne + nc + nd
        o_refs = refs[first_out:first_out + no]
        s_refs = refs[first_out + no:first_out + no + ns]
        acc_refs = refs[first_out + no + ns:]
        av = a_ref[...]
        prods = [lax.dot_general(av, b_ref[...], dims, preferred_element_type=F32) for b_ref in b_refs]

        if ns:
            @pl.when((pl.program_id(0) == 0) & (pl.program_id(2) == 0))
            def _():
                for s_ref in s_refs:
                    s_ref[...] = jnp.zeros_like(s_ref)

        def finish(accs):
            res = epilogue(accs, *[e[...] for e in e_refs])
            tiles_out, sums_out = res if ns else (res, ())
            for o_ref, v in zip(o_refs, tiles_out):
                o_ref[...] = v.astype(o_ref.dtype)
            for s_ref, v in zip(s_refs, sums_out):
                s_ref[...] += v

        if nk == 1:
            finish(prods)
        else:
            kk = pl.program_id(2)

            @pl.when(kk == 0)
            def _():
                for acc_ref in acc_refs:
                    acc_ref[...] = jnp.zeros_like(acc_ref)

            for acc_ref, p in zip(acc_refs, prods):
                acc_ref[...] += p

            @pl.when(kk == nk - 1)
            def _():
                finish([acc_ref[...] for acc_ref in acc_refs])

    in_specs = [pl.BlockSpec((bm, bk), lambda i, j, kk: (i, kk))]
    if b_kn:
        in_specs += [pl.BlockSpec((bk, bn), functools.partial(lambda i, j, kk, q: (kmap(kk), acc_block(j, q)), q=q))
                     for q in range(n_acc)]
    else:
        in_specs += [pl.BlockSpec((bn, bk), functools.partial(lambda i, j, kk, q: (acc_block(j, q), kmap(kk)), q=q))
                     for q in range(n_acc)]
    in_specs += [pl.BlockSpec((bm, w * bn), functools.partial(lambda i, j, kk, off: (i, j + off), off=off))
                 for (_, w, off) in extras]
    in_specs += [pl.BlockSpec((1, bn), lambda i, j, kk: (0, j)) for _ in consts]
    in_specs += [ANY_SPEC] * nd
    out_specs = [pl.BlockSpec((bm, w * bn), lambda i, j, kk: (i, j)) for (_, _, w) in outs]
    out_specs += [pl.BlockSpec((1, w), lambda i, j, kk: (0, 0)) for w in sums]
    out_shape = [jax.ShapeDtypeStruct((m, cols), dt) for (cols, dt, _) in outs]
    out_shape += [jax.ShapeDtypeStruct((1, w), F32) for w in sums]
    return pl.pallas_call(
        body, name=name, grid=(m // bm, nj, nk),
        in_specs=in_specs, out_specs=out_specs, out_shape=out_shape,
        scratch_shapes=[pltpu.VMEM((bm, bn), F32)] * (n_acc if nk > 1 else 0),
        compiler_params=_params(("arbitrary",) * 3 if sums else ("parallel", "parallel", "arbitrary")),
    )(a, *[b] * n_acc, *[e[0] for e in extras], *consts, *deps)


def mm_ep_pipe(name, a, b, n_acc, acc_block, epilogue, extras, outs, tiles, deps=(), b_kn=False, consts=(), sums=()):
    m, k = a.shape
    bm, bn, bk = tiles
    bm = min(bm, m)
    assert bk == k and m % bm == 0 and b.shape[0 if b_kn else 1] == k, (name, a.shape, b.shape, tiles)
    ni, nj = m // bm, outs[0][0] // (outs[0][2] * bn)
    nt = ni * nj
    assert not sums or nj == 1, name
    ne, nc, no, ns, nd = len(extras), len(consts), len(outs), len(sums), len(deps)
    dims = (((1,), (0,)), ((), ())) if b_kn else (((1,), (1,)), ((), ()))
    cur_i = lambda t: jnp.minimum(t, nt - 1) // nj
    cur_j = lambda t: jnp.minimum(t, nt - 1) % nj
    prev_i = lambda t: jnp.maximum(t - 1, 0) // nj
    prev_j = lambda t: jnp.maximum(t - 1, 0) % nj

    def body(*refs):
        a_ref, b_refs = refs[0], refs[1:1 + n_acc]
        e_refs = refs[1 + n_acc:1 + n_acc + ne + nc]
        first_out = 1 + n_acc + ne + nc + nd
        o_refs = refs[first_out:first_out + no]
        s_refs = refs[first_out + no:first_out + no + ns]
        acc_ref = refs[first_out + no + ns]
        t = pl.program_id(0)

        @pl.when(t == 0)
        def _():
            acc_ref[...] = jnp.zeros_like(acc_ref)
            for s_ref in s_refs:
                s_ref[...] = jnp.zeros_like(s_ref)

        slot = t % 2
        done = [acc_ref[(1 - slot) * n_acc + q] for q in range(n_acc)]
        av = a_ref[...]
        for q, b_ref in enumerate(b_refs):
            acc_ref[slot * n_acc + q] = lax.dot_general(av, b_ref[...], dims, preferred_element_type=F32)
        res = epilogue(done, *[e[...] for e in e_refs])
        tiles_out, sums_out = res if ns else (res, ())
        for o_ref, v in zip(o_refs, tiles_out):
            o_ref[...] = v.astype(o_ref.dtype)
        live = (t >= 1).astype(F32)
        for s_ref, v in zip(s_refs, sums_out):
            s_ref[...] += v * live

    in_specs = [pl.BlockSpec((bm, k), lambda t: (cur_i(t), 0))]
    if b_kn:
        in_specs += [pl.BlockSpec((k, bn), functools.partial(lambda t, q: (0, acc_block(cur_j(t), q)), q=q))
                     for q in range(n_acc)]
    else:
        in_specs += [pl.BlockSpec((bn, k), functools.partial(lambda t, q: (acc_block(cur_j(t), q), 0), q=q))
                     for q in range(n_acc)]
    in_specs += [pl.BlockSpec((bm, w * bn), functools.partial(lambda t, off: (prev_i(t), prev_j(t) + off), off=off))
                 for (_, w, off) in extras]
    in_specs += [pl.BlockSpec((1, bn), lambda t: (0, prev_j(t))) for _ in consts]
    in_specs += [ANY_SPEC] * nd
    out_specs = [pl.BlockSpec((bm, w * bn), lambda t: (prev_i(t), prev_j(t))) for (_, _, w) in outs]
    out_specs += [pl.BlockSpec((1, w), lambda t: (0, 0)) for w in sums]
    out_shape = [jax.ShapeDtypeStruct((m, cols), dt) for (cols, dt, _) in outs]
    out_shape += [jax.ShapeDtypeStruct((1, w), F32) for w in sums]
    return pl.pallas_call(
        body, name=name, grid=(nt + 1,),
        in_specs=in_specs, out_specs=out_specs, out_shape=out_shape,
        scratch_shapes=[pltpu.VMEM((2 * n_acc, bm, bn), F32)],
        compiler_params=_params(("arbitrary",)),
    )(a, *[b] * n_acc, *[e[0] for e in extras], *consts, *deps)


def rowwise(name, fn, rows, consts, out_rows, out_sums, ts, alias=None, deps=()):
    rows = [r if isinstance(r, tuple) else (r, r.shape[1], 0) for r in rows]
    out_rows = [o if len(o) == 4 else (o[0], o[1], o[0], 0) for o in out_rows]
    s = rows[0][0].shape[0]
    nt = s // ts
    nr, nc, no, ns = len(rows), len(consts), len(out_rows), len(out_sums)
    in_specs = [pl.BlockSpec((ts, w), functools.partial(lambda i, cb: (i, cb), cb=cb)) for (_, w, cb) in rows]
    in_specs += [pl.BlockSpec(c.shape, lambda i: (0, 0)) for c in consts]
    operands = [r[0] for r in rows] + list(consts)
    aliases = {}
    if alias is not None:
        in_specs.append(pl.BlockSpec(memory_space=pl.ANY))
        operands.append(alias[0])
        aliases = {nr + nc: alias[1]}
    in_specs += [ANY_SPEC] * len(deps)
    operands += list(deps)
    out_shape = [jax.ShapeDtypeStruct((s, tw), dt) for (_, dt, tw, _) in out_rows]
    out_shape += [jax.ShapeDtypeStruct((1, w), F32) for w in out_sums]
    out_specs = [pl.BlockSpec((ts, w), functools.partial(lambda i, cb: (i, cb), cb=cb)) for (w, _, _, cb) in out_rows]
    out_specs += [pl.BlockSpec((1, w), lambda i: (0, 0)) for w in out_sums]
    n_in = len(operands)

    def body(*refs):
        ins, outs = refs[:nr + nc], refs[n_in:]
        i = pl.program_id(0)
        ro, so = fn(*[r[...] for r in ins])
        for q in range(no):
            outs[q][...] = ro[q].astype(outs[q].dtype)
        if ns:
            @pl.when(i == 0)
            def _():
                for q in range(ns):
                    outs[no + q][...] = jnp.zeros_like(outs[no + q])

            for q in range(ns):
                outs[no + q][...] += so[q]

    return pl.pallas_call(
        body, name=name, grid=(nt,),
        in_specs=in_specs, out_specs=out_specs, out_shape=out_shape, input_output_aliases=aliases,
        compiler_params=_params(("arbitrary",) if ns else ("parallel",)),
    )(*operands)


def _colsum(v):
    return jnp.sum(v, axis=0, keepdims=True)


def _rms_stats(xv):
    r = lax.rsqrt(jnp.mean(xv * xv, axis=-1, keepdims=True) + EPS)
    return r, xv * r


def _rms_bwd(dxhat, xhat, r):
    return r * (dxhat - xhat * jnp.mean(dxhat * xhat, axis=-1, keepdims=True))


def _gelu(v):
    k = math.sqrt(2.0 / math.pi)
    t = jnp.tanh(k * (v + 0.044715 * v * v * v))
    return 0.5 * v * (1.0 + t), t


def _gelu_grad(v, t):
    k = math.sqrt(2.0 / math.pi)
    return 0.5 * (1.0 + t) + 0.5 * v * (1.0 - t * t) * k * (1.0 + 3.0 * 0.044715 * v * v)


CONV_TS = 512
CONV_CH = 64


def _ln_fwd(yc, g, b):
    mu = jnp.mean(yc, axis=-1, keepdims=True)
    xc = yc - mu
    rstd = lax.rsqrt(jnp.mean(xc * xc, axis=-1, keepdims=True) + EPS)
    nhat = xc * rstd
    return nhat, rstd, nhat * g + b


SUBL = 8


def _shifted_copies(buf, sh, ts):
    for b in range(1, SUBL):
        sh[b - 1] = buf[pl.ds(b, ts + HALO - SUBL), :]


def _shifted(buf, sh, start):
    b = start % SUBL
    if b == 0:
        return buf[pl.ds(start, CONV_CH), :]
    return sh[b - 1, pl.ds(start - b, CONV_CH), :]


def conv_fwd(z, w32, cb, lg, lb):
    s = z.shape[0]
    ts = CONV_TS
    nt = s // ts
    hb = ts // HALO

    def body(a_ref, g_ref, ah_ref, gh_ref, w_ref, cb_ref, lg_ref, lb_ref, yc_ref, s_ref, ubuf, ush):
        i = pl.program_id(0)
        first = (i > 0).astype(F32)
        ubuf[0:HALO, :] = ah_ref[...] * _sig(gh_ref[...]) * first
        ubuf[HALO:HALO + ts, :] = a_ref[...] * _sig(g_ref[...])
        _shifted_copies(ubuf, ush, ts)
        for c0 in range(0, ts, CONV_CH):
            acc = jnp.zeros((CONV_CH, CW), F32)
            for k in range(KC):
                acc = acc + w_ref[k:k + 1, :] * _shifted(ubuf, ush, c0 + k + 2)
            yc = acc + cb_ref[...]
            yc_ref[c0:c0 + CONV_CH, :] = yc
            _, _, ln = _ln_fwd(yc, lg_ref[...], lb_ref[...])
            s_ref[c0:c0 + CONV_CH, :] = (ln * _sig(ln)).astype(s_ref.dtype)

    cur = lambda cbk: pl.BlockSpec((ts, CW), functools.partial(lambda i, q: (i, q), q=cbk))
    prev = lambda cbk: pl.BlockSpec((HALO, CW), functools.partial(lambda i, q: (jnp.maximum(i * hb - 1, 0), q), q=cbk))
    const = lambda a: pl.BlockSpec(a.shape, lambda i: (0, 0))
    return pl.pallas_call(
        body, name="conv_fwd", grid=(nt,),
        in_specs=[cur(ZB_A), cur(ZB_G), prev(ZB_A), prev(ZB_G), const(w32), const(cb), const(lg), const(lb)],
        out_specs=[pl.BlockSpec((ts, CW), lambda i: (i, 0)), pl.BlockSpec((ts, CW), lambda i: (i, 0))],
        out_shape=[jax.ShapeDtypeStruct((s, CW), F32), jax.ShapeDtypeStruct((s, CW), BF16)],
        scratch_shapes=[pltpu.VMEM((HALO + ts, CW), F32), pltpu.VMEM((SUBL - 1, ts + HALO - SUBL, CW), F32)],
        compiler_params=_params(("parallel",)),
    )(z, z, z, z, w32, cb, lg, lb)


def conv_bwd(ds, yc, z, w32, lg, lb, dz):
    s = z.shape[0]
    ts = CONV_TS
    nt = s // ts
    hb = ts // HALO
    last_hb = s // HALO - 1

    def ln_bwd(dsv, ycv, g, b):
        nhat, rstd, ln = _ln_fwd(ycv, g, b)
        sg = _sig(ln)
        dln = dsv * (sg * (1.0 + ln * (1.0 - sg)))
        dnh = dln * g
        dyc = rstd * (dnh - jnp.mean(dnh, axis=-1, keepdims=True)
                      - nhat * jnp.mean(dnh * nhat, axis=-1, keepdims=True))
        return dyc, dln, nhat

    def body(ds_ref, yc_ref, dsn_ref, ycn_ref, a_ref, g_ref, ah_ref, gh_ref, w_ref, lg_ref, lb_ref, dz_in,
             dz_ref, dlg_ref, dlb_ref, dcb_ref, dw_ref, dbuf, ubuf, dsh, ush, dwacc):
        i = pl.program_id(0)

        @pl.when(i == 0)
        def _():
            dlg_ref[...] = jnp.zeros_like(dlg_ref)
            dlb_ref[...] = jnp.zeros_like(dlb_ref)
            dcb_ref[...] = jnp.zeros_like(dcb_ref)
            dw_ref[...] = jnp.zeros_like(dw_ref)
            dwacc[...] = jnp.zeros_like(dwacc)

        lg, lb = lg_ref[...], lb_ref[...]
        dyc, dln, nhat = ln_bwd(ds_ref[...], yc_ref[...], lg, lb)
        dlg_ref[...] += _colsum(dln * nhat)
        dlb_ref[...] += _colsum(dln)
        dcb_ref[...] += _colsum(dyc)
        dbuf[0:ts, :] = dyc
        nxt = (i < nt - 1).astype(F32)
        dbuf[ts:ts + HALO, :] = ln_bwd(dsn_ref[...], ycn_ref[...], lg, lb)[0] * nxt
        first = (i > 0).astype(F32)
        ubuf[0:HALO, :] = ah_ref[...] * _sig(gh_ref[...]) * first
        ubuf[HALO:HALO + ts, :] = a_ref[...] * _sig(g_ref[...])
        _shifted_copies(dbuf, dsh, ts)
        _shifted_copies(ubuf, ush, ts)
        for c0 in range(0, ts, CONV_CH):
            du = jnp.zeros((CONV_CH, CW), F32)
            dyc_c = dbuf[c0:c0 + CONV_CH, :]
            for k in range(KC):
                du = du + w_ref[k:k + 1, :] * _shifted(dbuf, dsh, c0 + KC - 1 - k)
                prod = dyc_c * _shifted(ubuf, ush, c0 + k + 2)
                dwacc[k] += jnp.sum(prod.reshape(CONV_CH // SUBL, SUBL, CW), axis=0)
            av = a_ref[c0:c0 + CONV_CH, :]
            sg = _sig(g_ref[c0:c0 + CONV_CH, :])
            dz_ref[c0:c0 + CONV_CH, 0:CW] = (du * sg).astype(dz_ref.dtype)
            dz_ref[c0:c0 + CONV_CH, CW:2 * CW] = (du * av * sg * (1.0 - sg)).astype(dz_ref.dtype)

        @pl.when(i == nt - 1)
        def _():
            for k in range(KC):
                dw_ref[k:k + 1, :] = _colsum(dwacc[k])

    cur = lambda w, cbk: pl.BlockSpec((ts, w), functools.partial(lambda i, q: (i, q), q=cbk))
    prev = lambda cbk: pl.BlockSpec((HALO, CW), functools.partial(lambda i, q: (jnp.maximum(i * hb - 1, 0), q), q=cbk))
    nxt_spec = pl.BlockSpec((HALO, CW), lambda i: (jnp.minimum((i + 1) * hb, last_hb), 0))
    const = lambda a: pl.BlockSpec(a.shape, lambda i: (0, 0))
    acc = lambda r: pl.BlockSpec((r, CW), lambda i: (0, 0))
    return pl.pallas_call(
        body, name="conv_bwd", grid=(nt,),
        in_specs=[cur(CW, 0), cur(CW, 0), nxt_spec, nxt_spec, cur(CW, ZB_A), cur(CW, ZB_G), prev(ZB_A), prev(ZB_G),
                  const(w32), const(lg), const(lb), pl.BlockSpec(memory_space=pl.ANY)],
        out_specs=[pl.BlockSpec((ts, 2 * CW), lambda i: (i, ZB_A // 2)), acc(1), acc(1), acc(1), acc(HALO)],
        out_shape=[jax.ShapeDtypeStruct(dz.shape, dz.dtype), jax.ShapeDtypeStruct((1, CW), F32),
                   jax.ShapeDtypeStruct((1, CW), F32), jax.ShapeDtypeStruct((1, CW), F32),
                   jax.ShapeDtypeStruct((HALO, CW), F32)],
        scratch_shapes=[pltpu.VMEM((ts + HALO, CW), F32), pltpu.VMEM((HALO + ts, CW), F32)]
        + [pltpu.VMEM((SUBL - 1, ts + HALO - SUBL, CW), F32)] * 2 + [pltpu.VMEM((KC, SUBL, CW), F32)],
        input_output_aliases={11: 0},
        compiler_params=_params(("arbitrary",)),
    )(ds, yc, ds, yc, z, z, z, z, w32, lg, lb, dz)


SSM_TS = 1024
GRP = 8


def _cmul(ar, ai, br, bi):
    return ar * br - ai * bi, ar * bi + ai * br


def _scan_tables(ar, ai, reverse):
    n = ar.shape[1]
    row = lax.broadcasted_iota(jnp.int32, (GRP, n), 0)
    dist = (GRP - 1 - row) if reverse else row
    one_r = jnp.broadcast_to(ar, (GRP, n))
    one_i = jnp.broadcast_to(ai, (GRP, n))
    p2r, p2i = _cmul(one_r, one_i, one_r, one_i)
    p4r, p4i = _cmul(p2r, p2i, p2r, p2i)
    steps = []
    for sft, (pr, pi) in ((1, (one_r, one_i)), (2, (p2r, p2i)), (4, (p4r, p4i))):
        keep = dist >= sft
        steps.append((jnp.where(keep, pr, 0.0), jnp.where(keep, pi, 0.0)))
    cr, ci = one_r, one_i
    accr, acci = one_r, one_i
    for e in range(1, GRP):
        cr, ci = _cmul(cr, ci, one_r, one_i)
        accr = jnp.where(dist == e, cr, accr)
        acci = jnp.where(dist == e, ci, acci)
    return steps, (accr, acci)


def _scan_group(xr, xi, steps, carry_tab, cr, ci, reverse):
    for sft, (tr, ti) in zip((1, 2, 4), steps):
        amt = (GRP - sft) if reverse else sft
        sr = pltpu.roll(xr, amt, 0)
        si = pltpu.roll(xi, amt, 0)
        xr, xi = xr + tr * sr - ti * si, xi + tr * si + ti * sr
    pr, pi = carry_tab
    xr = xr + pr * cr - pi * ci
    xi = xi + pr * ci + pi * cr
    return xr, xi


def ssm_fwd(z, wb_re, wb_im, wc, e_re, e_im, dvec):
    s = z.shape[0]
    ts = SSM_TS
    nt = s // ts
    ucol0 = ZB_U * CW // CB

    def body(u_ref, wbr_ref, wbi_ref, wc_ref, er_ref, ei_ref, d_ref, xr_ref, xi_ref, y_ref, gl_ref, car_r, car_i):
        i = pl.program_id(1)

        @pl.when(i == 0)
        def _():
            car_r[...] = jnp.zeros_like(car_r)
            car_i[...] = jnp.zeros_like(car_i)

        u = u_ref[...]
        ub = u.astype(BF16)
        xr_ref[...] = jnp.dot(ub, wbr_ref[0], preferred_element_type=F32)
        xi_ref[...] = jnp.dot(ub, wbi_ref[0], preferred_element_type=F32)
        steps, ctab = _scan_tables(er_ref[0], ei_ref[0], False)

        def grp(r, carry):
            cr, ci = carry
            r0 = pl.multiple_of(r * GRP, GRP)
            xr, xi = _scan_group(xr_ref[pl.ds(r0, GRP), :], xi_ref[pl.ds(r0, GRP), :], steps, ctab, cr, ci, False)
            xr_ref[pl.ds(r0, GRP), :] = xr
            xi_ref[pl.ds(r0, GRP), :] = xi
            return (jnp.broadcast_to(xr[GRP - 1:GRP, :], (GRP, SB)), jnp.broadcast_to(xi[GRP - 1:GRP, :], (GRP, SB)))

        cr, ci = lax.fori_loop(0, ts // GRP, grp, (car_r[...], car_i[...]))
        car_r[...] = cr
        car_i[...] = ci
        y = (jnp.dot(xr_ref[...].astype(BF16), wc_ref[0, 0:SB, :], preferred_element_type=F32)
             + jnp.dot(xi_ref[...].astype(BF16), wc_ref[0, SB:2 * SB, :], preferred_element_type=F32)
             + d_ref[0] * u)
        y_ref[...] = y
        gl_ref[...] = _gelu(y)[0].astype(gl_ref.dtype)

    blk3 = lambda a: pl.BlockSpec((1,) + a.shape[1:], lambda j, i: (j, 0, 0))
    return pl.pallas_call(
        body, name="ssm_fwd", grid=(NBLK, nt),
        in_specs=[pl.BlockSpec((ts, CB), lambda j, i: (i, ucol0 + j)),
                  blk3(wb_re), blk3(wb_im), blk3(wc), blk3(e_re), blk3(e_im), blk3(dvec)],
        out_specs=[pl.BlockSpec((ts, SB), lambda j, i: (i, j)), pl.BlockSpec((ts, SB), lambda j, i: (i, j)),
                   pl.BlockSpec((ts, CB), lambda j, i: (i, j)), pl.BlockSpec((ts, CB), lambda j, i: (i, j))],
        out_shape=[jax.ShapeDtypeStruct((s, NST), F32), jax.ShapeDtypeStruct((s, NST), F32),
                   jax.ShapeDtypeStruct((s, SW), F32), jax.ShapeDtypeStruct((s, SW), BF16)],
        scratch_shapes=[pltpu.VMEM((GRP, SB), F32), pltpu.VMEM((GRP, SB), F32)],
        compiler_params=_params(("parallel", "arbitrary")),
    )(z, wb_re, wb_im, wc, e_re, e_im, dvec)


def ssm_bwd(dgl, ypre, z, xs_re, xs_im, wbt_re, wbt_im, wct, e_re, e_im, dvec, dz):
    s = z.shape[0]
    ts = SSM_TS
    nt = s // ts
    ucol0 = ZB_U * CW // CB
    tn_dims = (((0,), (0,)), ((), ()))

    def body(dgl_ref, y_ref, u_ref, xr_ref, xi_ref, wbtr_ref, wbti_ref, wct_ref, er_ref, ei_ref, d_ref, dz_in,
             du_ref, dd_ref, dar_ref, dai_ref, dwbr_ref, dwbi_ref, dwc_ref,
             lr_ref, li_ref, car_r, car_i, acc_r, acc_i):
        i = pl.program_id(1)

        @pl.when(i == 0)
        def _():
            for ref in (car_r, car_i, acc_r, acc_i, dd_ref, dwbr_ref, dwbi_ref, dwc_ref):
                ref[...] = jnp.zeros_like(ref)

        u = u_ref[...]
        y = y_ref[...]
        dy = dgl_ref[...] * _gelu_grad(y, _gelu(y)[1])
        dd_ref[0] += _colsum(dy * u)
        dyb = dy.astype(BF16)
        dxo = jnp.dot(dyb, wct_ref[0], preferred_element_type=F32)
        lr_ref[...] = dxo[:, 0:SB]
        li_ref[...] = dxo[:, SB:2 * SB]
        steps, ctab = _scan_tables(er_ref[0], -ei_ref[0], True)
        row = lax.broadcasted_iota(jnp.int32, (GRP, SB), 0)

        def grp(q, carry):
            cr, ci, ar, ai = carry
            r0 = pl.multiple_of((ts // GRP - 1 - q) * GRP, GRP)
            lr, li = _scan_group(lr_ref[pl.ds(r0, GRP), :], li_ref[pl.ds(r0, GRP), :], steps, ctab, cr, ci, True)
            lr_ref[pl.ds(r0, GRP), :] = lr
            li_ref[pl.ds(r0, GRP), :] = li
            nr = jnp.where(row == GRP - 1, cr, pltpu.roll(lr, GRP - 1, 0))
            ni = jnp.where(row == GRP - 1, ci, pltpu.roll(li, GRP - 1, 0))
            xr = xr_ref[pl.ds(r0, GRP), :]
            xi = xi_ref[pl.ds(r0, GRP), :]
            ar = ar + nr * xr + ni * xi
            ai = ai + ni * xr - nr * xi
            return (jnp.broadcast_to(lr[0:1, :], (GRP, SB)), jnp.broadcast_to(li[0:1, :], (GRP, SB)), ar, ai)

        cr, ci, ar, ai = lax.fori_loop(0, ts // GRP, grp, (car_r[...], car_i[...], acc_r[...], acc_i[...]))
        car_r[...] = cr
        car_i[...] = ci
        acc_r[...] = ar
        acc_i[...] = ai

        @pl.when(i == nt - 1)
        def _():
            dar_ref[0] = _colsum(ar)
            dai_ref[0] = _colsum(ai)

        lrb = lr_ref[...].astype(BF16)
        lib = li_ref[...].astype(BF16)
        du = (jnp.dot(lrb, wbtr_ref[0], preferred_element_type=F32)
              + jnp.dot(lib, wbti_ref[0], preferred_element_type=F32) + d_ref[0] * dy)
        du_ref[...] = du.astype(du_ref.dtype)
        ub = u.astype(BF16)
        dwbr_ref[0] += lax.dot_general(ub, lrb, tn_dims, preferred_element_type=F32)
        dwbi_ref[0] += lax.dot_general(ub, lib, tn_dims, preferred_element_type=F32)
        dwc_ref[0, 0:SB, :] += lax.dot_general(xr_ref[...].astype(BF16), dyb, tn_dims, preferred_element_type=F32)
        dwc_ref[0, SB:2 * SB, :] += lax.dot_general(xi_ref[...].astype(BF16), dyb, tn_dims, preferred_element_type=F32)

    rev = lambda i: nt - 1 - i
    blk3 = lambda a: pl.BlockSpec((1,) + a.shape[1:], lambda j, i: (j, 0, 0))
    acc3 = lambda r, c: pl.BlockSpec((1, r, c), lambda j, i: (j, 0, 0))
    return pl.pallas_call(
        body, name="ssm_bwd", grid=(NBLK, nt),
        in_specs=[pl.BlockSpec((ts, CB), lambda j, i: (rev(i), j)), pl.BlockSpec((ts, CB), lambda j, i: (rev(i), j)),
                  pl.BlockSpec((ts, CB), lambda j, i: (rev(i), ucol0 + j)),
                  pl.BlockSpec((ts, SB), lambda j, i: (rev(i), j)), pl.BlockSpec((ts, SB), lambda j, i: (rev(i), j)),
                  blk3(wbt_re), blk3(wbt_im), blk3(wct), blk3(e_re), blk3(e_im), blk3(dvec),
                  pl.BlockSpec(memory_space=pl.ANY)],
        out_specs=[pl.BlockSpec((ts, CB), lambda j, i: (rev(i), ucol0 + j)),
                   acc3(1, CB), acc3(1, SB), acc3(1, SB), acc3(CB, SB), acc3(CB, SB), acc3(2 * SB, CB)],
        out_shape=[jax.ShapeDtypeStruct(dz.shape, dz.dtype),
                   jax.ShapeDtypeStruct((NBLK, 1, CB), F32),
                   jax.ShapeDtypeStruct((NBLK, 1, SB), F32), jax.ShapeDtypeStruct((NBLK, 1, SB), F32),
                   jax.ShapeDtypeStruct((NBLK, CB, SB), F32), jax.ShapeDtypeStruct((NBLK, CB, SB), F32),
                   jax.ShapeDtypeStruct((NBLK, 2 * SB, CB), F32)],
        scratch_shapes=[pltpu.VMEM((ts, SB), F32), pltpu.VMEM((ts, SB), F32)] + [pltpu.VMEM((GRP, SB), F32)] * 4,
        input_output_aliases={11: 0},
        compiler_params=_params(("parallel", "arbitrary")),
    )(dgl, ypre, z, xs_re, xs_im, wbt_re, wbt_im, wct, e_re, e_im, dvec, dz)


def _disc(a_re, a_im, log_dt, b_re, b_im, expand):
    dt = jnp.dot(expand, jnp.exp(log_dt), preferred_element_type=F32, precision=lax.Precision.HIGHEST)
    mag = jnp.exp(dt * a_re)
    e_re, e_im = mag * jnp.cos(dt * a_im), mag * jnp.sin(dt * a_im)
    n_re, n_im = e_re - 1.0, e_im
    den = a_re * a_re + a_im * a_im
    q_re = (n_re * a_re + n_im * a_im) / den
    q_im = (n_im * a_re - n_re * a_im) / den
    return e_re, e_im, q_re * b_re - q_im * b_im, q_re * b_im + q_im * b_re


def _whole(a):
    return pl.BlockSpec(a.shape, functools.partial(lambda n: (0,) * n, n=a.ndim))


def disc_fwd(a_re, a_im, log_dt, b_re, b_im, expand):
    def body(ar, ai, ld, br, bi, ex, er_o, ei_o, bbr_o, bbi_o):
        er, ei, bbr, bbi = _disc(ar[...], ai[...], ld[...], br[...], bi[...], ex[...])
        er_o[...] = er
        ei_o[...] = ei
        bbr_o[...] = bbr
        bbi_o[...] = bbi

    ins = (a_re, a_im, log_dt, b_re, b_im, expand)
    outs = [jax.ShapeDtypeStruct(a_re.shape, F32)] * 2 + [jax.ShapeDtypeStruct(b_re.shape, F32)] * 2
    return pl.pallas_call(body, name="disc_fwd", in_specs=[_whole(a) for a in ins],
                          out_specs=[_whole(o) for o in outs], out_shape=outs, compiler_params=_params())(*ins)


def disc_bwd(a_re, a_im, log_dt, b_re, b_im, expand, de_re, de_im, dbb_re, dbb_im):
    def body(ar, ai, ld, br, bi, ex, der, dei, dbr, dbi, o_ar, o_ai, o_ld, o_br, o_bi):
        exv = ex[...]
        _, vjp = jax.vjp(lambda *p: _disc(*p, exv), ar[...], ai[...], ld[...], br[...], bi[...])
        g = vjp((der[...], dei[...], dbr[...], dbi[...]))
        for o, v in zip((o_ar, o_ai, o_ld, o_br, o_bi), g):
            o[...] = v

    ins = (a_re, a_im, log_dt, b_re, b_im, expand, de_re, de_im, dbb_re, dbb_im)
    outs = [jax.ShapeDtypeStruct(a.shape, F32) for a in (a_re, a_im, log_dt, b_re, b_im)]
    return pl.pallas_call(body, name="disc_bwd", in_specs=[_whole(a) for a in ins],
                          out_specs=[_whole(o) for o in outs], out_shape=outs, compiler_params=_params())(*ins)


def mod_fwd(c_all, w_ada, b_cols):
    def body(c_ref, w_ref, b_ref, act_ref, mod_ref):
        cv = c_ref[...]
        act = cv * _sig(cv)
        act_ref[...] = act
        mod_ref[...] = jnp.dot(act, w_ref[...], preferred_element_type=F32, precision=lax.Precision.HIGHEST) + b_ref[...]

    ins = (c_all, w_ada, b_cols)
    outs = [jax.ShapeDtypeStruct(c_all.shape, F32), jax.ShapeDtypeStruct((NDEV, w_ada.shape[1]), F32)]
    return pl.pallas_call(body, name="mod_fwd", in_specs=[_whole(a) for a in ins],
                          out_specs=[_whole(o) for o in outs], out_shape=outs, compiler_params=_params())(*ins)


def ada_grad(act_all, dmod_cols):
    def body(a_ref, d_ref, o_ref):
        o_ref[...] = lax.dot_general(a_ref[...], d_ref[...], (((0,), (0,)), ((), ())),
                                     preferred_element_type=F32, precision=lax.Precision.HIGHEST)

    out = jax.ShapeDtypeStruct((act_all.shape[1], dmod_cols.shape[1]), F32)
    return pl.pallas_call(body, name="ada_grad", in_specs=[_whole(act_all), _whole(dmod_cols)],
                          out_specs=_whole(out), out_shape=out, compiler_params=_params())(act_all, dmod_cols)


def _adam_math(w, g, m, v):
    m2 = ADAM_B1 * m + (1.0 - ADAM_B1) * g
    v2 = ADAM_B2 * v + (1.0 - ADAM_B2) * (g * g)
    m_hat = m2 / (1.0 - ADAM_B1 ** ADAM_STEP)
    v_hat = v2 / (1.0 - ADAM_B2 ** ADAM_STEP)
    delta = -ADAM_LR * (m_hat / (jnp.sqrt(v_hat) + ADAM_EPS) + ADAM_WD * w)
    return delta, m2, v2


def adam(name, w, g, m, v):
    r, c = w.shape
    tr = max(t for t in range(8, min(r, 512) + 1, 8) if r % t == 0)

    def body(w_ref, g_ref, m_ref, v_ref, d_o, m_o, v_o):
        d, m2, v2 = _adam_math(w_ref[...], g_ref[...], m_ref[...], v_ref[...])
        d_o[...] = d
        m_o[...] = m2
        v_o[...] = v2

    spec = pl.BlockSpec((tr, c), lambda i: (i, 0))
    out = jax.ShapeDtypeStruct((r, c), F32)
    return pl.pallas_call(body, name=name, grid=(r // tr,), in_specs=[spec] * 4, out_specs=[spec] * 3,
                          out_shape=[out] * 3, compiler_params=_params(("parallel",)))(w, g, m, v)


def adam_many(name, ws, gs, ms, vs):
    n = len(ws)

    def body(*refs):
        ins, outs = refs[:4 * n], refs[4 * n:]
        for q in range(n):
            d, m2, v2 = _adam_math(ins[q][...], ins[n + q][...], ins[2 * n + q][...], ins[3 * n + q][...])
            outs[q][...] = d
            outs[n + q][...] = m2
            outs[2 * n + q][...] = v2

    operands = list(ws) + list(gs) + list(ms) + list(vs)
    outs = [jax.ShapeDtypeStruct(w.shape, F32) for w in ws] * 3
    return pl.pallas_call(body, name=name, in_specs=[_whole(a) for a in operands],
                          out_specs=[_whole(o) for o in outs], out_shape=outs, compiler_params=_params())(*operands)


def _rows_tile(r, most):
    best = None
    for t in range(16, min(r, most) + 1, 16):
        if r % t == 0:
            best = t
    assert best is not None, r
    return best


def sum_slots(name, slots, out_dtype=F32):
    n, r, c = slots.shape
    tr = _rows_tile(r, max(16, (2 * 1024 * 1024) // (n * c)))

    def body(s_ref, o_ref):
        acc = s_ref[0].astype(F32)
        for q in range(1, n):
            acc = acc + s_ref[q].astype(F32)
        o_ref[...] = acc.astype(o_ref.dtype)

    return pl.pallas_call(body, name=name, grid=(r // tr,),
                          in_specs=[pl.BlockSpec((n, tr, c), lambda i: (0, i, 0))],
                          out_specs=pl.BlockSpec((tr, c), lambda i: (i, 0)),
                          out_shape=jax.ShapeDtypeStruct((r, c), out_dtype), compiler_params=_params(("parallel",)))(slots)


HBM_SPEC = pl.BlockSpec(memory_space=pltpu.HBM)


def _coords():
    return lax.axis_index("x"), lax.axis_index("y"), lax.axis_index("c")


def _linear(x, y, c):
    return 4 * x + 2 * y + c


def all_gather(name, shards, deps=()):
    nq, nd = len(shards), len(deps)

    def body(*refs):
        xs, outs = refs[:nq], refs[nq + nd:2 * nq + nd]
        send_sems, recv_sems, local_sems = refs[2 * nq + nd:2 * nq + nd + 3]
        bufs = refs[2 * nq + nd + 3:]
        x, y, cc = _coords()
        me, sibling = (x, y, cc), (x, y, 1 - cc)
        chips = [(1 - x, y), (x, 1 - y), (1 - x, 1 - y)]

        def slot(q, px, py, pc):
            return outs[q].at[_linear(px, py, pc)]

        def copy(q, k, block, to, src=None):
            return pltpu.make_async_remote_copy(
                src_ref=slot(q, *block) if src is None else src, dst_ref=slot(q, *block),
                send_sem=send_sems.at[7 * q + k], recv_sem=recv_sems.at[7 * q + k], device_id=to, device_id_type=MESH)

        loads = [pltpu.make_async_copy(xs[q], bufs[q], local_sems.at[q]) for q in range(nq)]
        for cp in loads:
            cp.start()
        for cp in loads:
            cp.wait()
        mine = [pltpu.make_async_copy(bufs[q], slot(q, *me), local_sems.at[q]) for q in range(nq)]
        first = []
        for q in range(nq):
            first.append(copy(q, 0, me, sibling, src=bufs[q]))
            first += [copy(q, 1 + j, me, (*chip, cc), src=bufs[q]) for j, chip in enumerate(chips)]
        for cp in mine + first:
            cp.start()
        passed = []
        for q in range(nq):
            for j, chip in enumerate(chips):
                copy(q, 1 + j, (*chip, cc), me).wait_recv()
                passed.append(copy(q, 4 + j, (*chip, cc), sibling))
                passed[-1].start()
        for q in range(nq):
            copy(q, 0, sibling, me).wait_recv()
            for j, chip in enumerate(chips):
                copy(q, 4 + j, (*chip, 1 - cc), me).wait_recv()
        for cp in first + passed:
            cp.wait_send()
        for cp in mine:
            cp.wait()

    return pl.pallas_call(
        body, name=name, in_specs=[HBM_SPEC] * nq + [ANY_SPEC] * nd, out_specs=[HBM_SPEC] * nq,
        out_shape=[jax.ShapeDtypeStruct((NDEV,) + s.shape, s.dtype) for s in shards],
        scratch_shapes=[pltpu.SemaphoreType.DMA((7 * nq,)), pltpu.SemaphoreType.DMA((7 * nq,)),
                        pltpu.SemaphoreType.DMA((nq,))] + [pltpu.VMEM(s.shape, s.dtype) for s in shards],
    )(*shards, *deps)


NCHIP = 4


SEM_SPEC = pl.BlockSpec(memory_space=pltpu.SEMAPHORE)
EFFECT = pltpu.SideEffectType.DATAFLOW_SIDE_EFFECTING


def _peer(x, y, cc, k):
    fx, fy, fc = (k >> 2) & 1, (k >> 1) & 1, k & 1
    return (x + fx - 2 * fx * x, y + fy - 2 * fy * y, cc + fc - 2 * fc * cc)


def gather_plan(srcs, lands, coords):
    x, y, cc = coords
    me = _linear(x, y, cc)
    return [(s, l.at[me], _peer(x, y, cc, k)) for s, l in zip(srcs, lands) for k in range(1, NDEV)]


def near_plan(srcs, lands, coords):
    x, y, cc = coords
    me = _linear(x, y, cc)
    peers = [(x, y, 1 - cc)] + [_peer(x, y, cc, 2 * k) for k in range(1, NCHIP)]
    return [(s, l.at[me], p) for s, l in zip(srcs, lands) for p in peers]


def pass_on_plan(srcs, lands, coords):
    x, y, cc = coords
    out = []
    for l in srcs:
        for k in range(1, NCHIP):
            px, py, _ = _peer(x, y, cc, 2 * k)
            slot = _linear(px, py, cc)
            out.append((l.at[slot], l.at[slot], (x, y, 1 - cc)))
    return out


def pair_plan(srcs, lands, coords):
    x, y, cc = coords
    return [(s.at[2 * chip + 1 - cc], l.at[chip], (x, y, 1 - cc)) for s, l in zip(srcs, lands) for chip in range(NCHIP)]


def chip_plan(srcs, lands, coords):
    x, y, cc = coords
    out = []
    for s, l in zip(srcs, lands):
        for k in range(1, NCHIP):
            px, py, _ = _peer(x, y, cc, 2 * k)
            out.append((s.at[2 * px + py], l.at[k - 1], (px, py, cc)))
    return out


def _remote(copy, i, send_sems, recv_sems):
    src, dst, dev = copy
    return pltpu.make_async_remote_copy(src_ref=src, dst_ref=dst, send_sem=send_sems.at[i], recv_sem=recv_sems.at[i],
                                        device_id=dev, device_id_type=MESH)


def exchange_start(name, plan, ncopy, srcs, land_shapes, deps=()):
    ns, nl, nd = len(srcs), len(land_shapes), len(deps)

    def body(*refs):
        s, l = refs[:ns], refs[ns:ns + nl]
        send_sems, recv_sems = refs[ns + nl + nd], refs[ns + nl + nd + 1]
        token = refs[-1]
        for i, cp in enumerate(plan(s, l, _coords())):
            _remote(cp, i, send_sems, recv_sems).start()
        token[...] = jnp.zeros_like(token)

    hbm = lambda a: pltpu.with_memory_space_constraint(a, pltpu.HBM)
    lands = [lax.empty(shp, dt) for shp, dt in land_shapes]
    thru = [pltpu.HBM(a.shape, a.dtype) for a in list(srcs) + lands]
    outs = pl.pallas_call(
        body, name=name,
        in_specs=[HBM_SPEC] * (ns + nl) + [ANY_SPEC] * nd,
        out_specs=(SEM_SPEC, SEM_SPEC, *[HBM_SPEC] * (ns + nl), pl.BlockSpec(memory_space=pltpu.VMEM)),
        out_shape=(pltpu.SemaphoreType.DMA((ncopy,)), pltpu.SemaphoreType.DMA((ncopy,)), *thru,
                   jax.ShapeDtypeStruct((8, LANE), F32)),
        input_output_aliases={i: 2 + i for i in range(ns + nl)},
        compiler_params=pltpu.CompilerParams(has_side_effects=EFFECT),
    )(*[hbm(a) for a in srcs], *[hbm(a) for a in lands], *deps)
    return outs[0], outs[1], list(outs[2:2 + ns]), list(outs[2 + ns:2 + ns + nl]), outs[-1]


def exchange_wait(name, plan, started, after, place_own=False):
    send_sems, recv_sems, srcs, lands, _ = started
    ns, nl = len(srcs), len(lands)

    def body(*refs):
        s, l = refs[:ns], refs[ns:ns + nl]
        send_sems, recv_sems = refs[ns + nl], refs[ns + nl + 1]
        l_out = refs[2 * ns + nl + 3:2 * ns + 2 * nl + 3]
        scratch = refs[2 * ns + 2 * nl + 3:]
        copies = [_remote(cp, i, send_sems, recv_sems) for i, cp in enumerate(plan(s, l, _coords()))]
        if place_own:
            me = _linear(*_coords())
            local_sems, bufs = scratch[0], scratch[1:]
            loads = [pltpu.make_async_copy(s[q], bufs[q], local_sems.at[q]) for q in range(ns)]
            for cp in loads:
                cp.start()
            for cp in loads:
                cp.wait()
            stores = [pltpu.make_async_copy(bufs[q], l_out[q].at[me], local_sems.at[q]) for q in range(ns)]
            for cp in stores:
                cp.start()
        for cp in copies:
            cp.wait_recv()
        for cp in copies:
            cp.wait_send()
        if place_own:
            for cp in stores:
                cp.wait()

    scratch_shapes = []
    if place_own:
        scratch_shapes = [pltpu.SemaphoreType.DMA((ns,))] + [pltpu.VMEM(a.shape, a.dtype) for a in srcs]
    outs = pl.pallas_call(
        body, name=name,
        in_specs=[HBM_SPEC] * (ns + nl) + [SEM_SPEC, SEM_SPEC, ANY_SPEC],
        out_specs=[HBM_SPEC] * (ns + nl),
        out_shape=[pltpu.HBM(a.shape, a.dtype) for a in srcs + lands],
        input_output_aliases={i: i for i in range(ns + nl)},
        scratch_shapes=scratch_shapes,
        compiler_params=pltpu.CompilerParams(has_side_effects=EFFECT),
    )(*srcs, *lands, send_sems, recv_sems, after)
    return list(outs[:ns]), list(outs[ns:])


def pair_sum(name, g, recv):
    _, r, c = g.shape
    tr = _rows_tile(r, 512)

    def body(g_ref, r_ref, o_ref):
        own = jnp.where(lax.axis_index("c") == 0, g_ref[0, 0], g_ref[0, 1])
        o_ref[0] = (own.astype(F32) + r_ref[0].astype(F32)).astype(o_ref.dtype)

    return pl.pallas_call(
        body, name=name, grid=(NCHIP, r // tr),
        in_specs=[pl.BlockSpec((1, 2, tr, c), lambda k, i: (k, 0, i, 0)), pl.BlockSpec((1, tr, c), lambda k, i: (k, i, 0))],
        out_specs=pl.BlockSpec((1, tr, c), lambda k, i: (k, i, 0)),
        out_shape=jax.ShapeDtypeStruct((NCHIP, r, c), g.dtype), compiler_params=_params(("parallel", "parallel")),
    )(g.reshape(NCHIP, 2, r, c), recv)


def chip_sum_adam(name, partial, recv, w, m, v):
    _, r, c = partial.shape
    tr = _rows_tile(r, 512)

    def body(p_ref, r_ref, w_ref, m_ref, v_ref, g_o, d_o, m_o, v_o):
        chip = 2 * lax.axis_index("x") + lax.axis_index("y")
        own = p_ref[0]
        for k in range(1, NCHIP):
            own = jnp.where(chip == k, p_ref[k], own)
        g = own.astype(F32)
        for k in range(NCHIP - 1):
            g = g + r_ref[k].astype(F32)
        d, m2, v2 = _adam_math(w_ref[...], g, m_ref[...], v_ref[...])
        g_o[...] = g
        d_o[...] = d
        m_o[...] = m2
        v_o[...] = v2

    spec = pl.BlockSpec((tr, c), lambda i: (i, 0))
    out = jax.ShapeDtypeStruct((r, c), F32)
    return pl.pallas_call(
        body, name=name, grid=(r // tr,),
        in_specs=[pl.BlockSpec((NCHIP, tr, c), lambda i: (0, i, 0)), pl.BlockSpec((NCHIP - 1, tr, c), lambda i: (0, i, 0)),
                  spec, spec, spec],
        out_specs=[spec] * 4, out_shape=[out] * 4, compiler_params=_params(("parallel",)),
    )(partial, recv, w, m, v)


def _block_diag(w, rows_per, cols_per):
    w = w.reshape(NBLK, 8, rows_per, cols_per)
    eye = jnp.eye(8, dtype=w.dtype)
    out = w[:, :, :, None, :] * eye[None, :, None, :, None]
    return out.reshape(NBLK, 8 * rows_per, 8 * cols_per)


def _diag_blocks(wd, rows_per, cols_per):
    wd = wd.reshape(NBLK, 8, rows_per, 8, cols_per)
    idx = jnp.arange(8)
    return wd[:, idx, :, idx, :].transpose(1, 0, 2, 3).reshape(NG, rows_per, cols_per)


def _pad_rows(v, mult):
    n = v.shape[0]
    return jnp.pad(v, (0, (-n) % mult))


def kernel(x, c, w_ada, b_ada, norm1_g, w_in, conv_w, conv_b, conv_ln_g, conv_ln_b, conv_proj, ssm_a_re, ssm_a_im, ssm_b_re, ssm_b_im, ssm_c_re, ssm_c_im, ssm_d, ssm_log_dt, ssm_glu, w_out, norm2_g, w_ffn_in, w_ffn_out, final_g, loss_target, m_w_ada, m_b_ada, m_norm1_g, m_w_in, m_conv_w, m_conv_b, m_conv_ln_g, m_conv_ln_b, m_conv_proj, m_ssm_a_re, m_ssm_a_im, m_ssm_b_re, m_ssm_b_im, m_ssm_c_re, m_ssm_c_im, m_ssm_d, m_ssm_log_dt, m_ssm_glu, m_w_out, m_norm2_g, m_w_ffn_in, m_w_ffn_out, m_final_g, v_w_ada, v_b_ada, v_norm1_g, v_w_in, v_conv_w, v_conv_b, v_conv_ln_g, v_conv_ln_b, v_conv_proj, v_ssm_a_re, v_ssm_a_im, v_ssm_b_re, v_ssm_b_im, v_ssm_c_re, v_ssm_c_im, v_ssm_d, v_ssm_log_dt, v_ssm_glu, v_w_out, v_norm2_g, v_w_ffn_in, v_w_ffn_out, v_final_g):
    me = _linear(*_coords())
    xs = x[0]
    tgt = loss_target[0]
    seq = xs.shape[0]

    flat = lambda g: g.reshape(NDEV * g.shape[1], g.shape[2])
    w_in_s = w_in[0].T.astype(BF16)
    mids = [p.astype(BF16) for p in (conv_proj[0].T, ssm_glu[0].T, w_out[0])]
    ffns = [p.astype(BF16) for p in (w_ffn_in[0].T, w_ffn_out[0])]
    zone = lambda p: ((NDEV,) + p.shape, p.dtype)
    c_all, cw_g = all_gather("gather_c_conv_w", [c, conv_w[0]])
    in_go = exchange_start("gather_in_start", near_plan, NCHIP, [w_in_s], [zone(w_in_s)], deps=[c_all])
    mids_go = exchange_start("gather_mid_start", gather_plan, 7 * len(mids), mids, [zone(p) for p in mids],
                             deps=[in_go[4]])
    ffns_go = exchange_start("gather_ffn_start", gather_plan, 7 * len(ffns), ffns, [zone(p) for p in ffns],
                             deps=[mids_go[4]])

    ncol = w_ada.shape[2]
    c_all = c_all.reshape(NDEV, D)
    b_cols = lax.dynamic_slice_in_dim(b_ada, me * ncol, ncol, axis=1)
    act_all, mod_cols = mod_fwd(c_all, w_ada[0], b_cols)
    (mod_all,) = all_gather("gather_mod", [mod_cols])
    mod = lax.dynamic_index_in_dim(mod_all, me, axis=1, keepdims=False).reshape(NMOD, D)
    sh1, sc1, g1, sh2, sc2, g2 = [mod[q:q + 1] for q in range(NMOD)]

    expand = jnp.repeat(jnp.eye(NG, dtype=F32), NP, axis=0)
    a_re_c, a_im_c = ssm_a_re.reshape(NST, 1), ssm_a_im.reshape(NST, 1)
    ldt_c = ssm_log_dt.reshape(NG, 1)
    b_re_r, b_im_r = ssm_b_re.reshape(NST, GH), ssm_b_im.reshape(NST, GH)
    e_re, e_im, bb_re, bb_im = disc_fwd(a_re_c, a_im_c, ldt_c, b_re_r, b_im_r, expand)
    e_re_b, e_im_b = e_re.reshape(NBLK, 1, SB), e_im.reshape(NBLK, 1, SB)
    bb_re_g, bb_im_g = bb_re.reshape(NG, NP, GH), bb_im.reshape(NG, NP, GH)
    wbt_re = _block_diag(bb_re_g, NP, GH)
    wbt_im = _block_diag(bb_im_g, NP, GH)
    wb_re, wb_im = wbt_re.transpose(0, 2, 1), wbt_im.transpose(0, 2, 1)
    wct = jnp.concatenate([_block_diag(ssm_c_re[0], GH, NP), -_block_diag(ssm_c_im[0], GH, NP)], axis=2)
    wc = wct.transpose(0, 2, 1)
    to_b = lambda a: a.astype(BF16)
    dvec = ssm_d.reshape(NBLK, 1, CB)

    n1g = norm1_g

    def f_norm1(xv, g, sc, sh):
        _, xh = _rms_stats(xv)
        return [xh * g * (1.0 + sc) + sh], []

    (h1,) = rowwise("norm1", f_norm1, [xs], [n1g, sc1, sh1], [(D, BF16)], [], 512, deps=[ffns_go[4]])
    _, (w_in_land,) = exchange_wait("gather_in_wait", near_plan, in_go, h1, place_own=True)
    pass_go = exchange_start("gather_in_pass_start", pass_on_plan, NCHIP - 1, [w_in_land], [])
    (w_in_g,), _ = exchange_wait("gather_in_pass_wait", pass_on_plan, pass_go, pass_go[4])
    w_in_t = flat(w_in_g)
    z = mm("mm_in", h1, w_in_t, "nt", tiles=(2048, CW, 1024), b_rot=Z_ROT)

    conv_w_full = cw_g.transpose(1, 0, 2).reshape(KC, CW)
    w32 = jnp.pad(conv_w_full, ((0, HALO - KC), (0, 0)))
    yc, s_act = conv_fwd(z, w32, conv_b, conv_ln_g, conv_ln_b)
    conv_proj_t, ssm_glu_t, w_out_f = [
        flat(g) for g in exchange_wait("gather_mid_wait", gather_plan, mids_go, s_act, place_own=True)[1]]
    y_conv = mm("mm_conv_proj", s_act, conv_proj_t, "nt")

    xs_re, xs_im, ypre, gl = ssm_fwd(z, to_b(wb_re), to_b(wb_im), to_b(wc), e_re_b, e_im_b, dvec)
    n_mrg = D // MRG_BLK

    pair_of = lambda t, n: t // 2 + (t % 2) * n

    def ep_merge(accs, yc_v, gates):
        za, zb = accs
        glc, gls = gates[:, 0:MRG_BLK], gates[:, MRG_BLK:2 * MRG_BLK]
        return [_sig(glc) * yc_v + _sig(gls) * (za * _sig(zb)), jnp.concatenate([za, zb], axis=1)]

    merged, z2_pair = mm_ep("mm_ssm_glu", gl, ssm_glu_t, 2, lambda j, q: j + q * n_mrg, ep_merge,
                            [(y_conv, 1, 0), (z, 2, 0)], [(D, BF16, 1), (2 * D, BF16, 2)], (512, MRG_BLK, SW))
    row_tiles = lambda bk: (512, D, bk)
    whole = lambda j, q: j

    def ep_norm2(accs, xv, g1v, g, sc, sh):
        (o1v,) = accs
        x1v = xv + g1v * o1v
        _, xh = _rms_stats(x1v)
        return [x1v, xh * g * (1.0 + sc) + sh, o1v]

    x1, h2, o1 = mm_ep("mm_out", merged, w_out_f, 1, whole, ep_norm2, [(xs, 1, 0)],
                       [(D, F32, 1), (D, BF16, 1), (D, BF16, 1)], row_tiles(D), b_kn=True,
                       consts=[g1, norm2_g, sc2, sh2])
    w_ffn_in_t, w_ffn_out_f = [
        flat(g) for g in exchange_wait("gather_ffn_wait", gather_plan, ffns_go, h2, place_own=True)[1]]
    ffn_tiles = (512, FFN_BLK, 1024)
    n_ffn_blk = FH // FFN_BLK
    pair_map = lambda t: t // 2 + (t % 2) * n_ffn_blk

    def ep_swiglu(accs):
        fg, fu = accs
        return [fg * _sig(fg) * fu, jnp.concatenate([fg, fu], axis=1)]

    act, f_pair = mm_ep("mm_ffn_in", h2, w_ffn_in_t, 2, lambda j, q: j + q * n_ffn_blk, ep_swiglu, [],
                        [(FH, BF16, 1), (2 * FH, BF16, 2)], ffn_tiles)
    fg_row = final_g.reshape(1, D)

    def ep_final(accs, x1v, tv, g2v, fg):
        (o2v,) = accs
        x2v = x1v + g2v * o2v
        r, xh = _rms_stats(x2v)
        yv = xh * fg
        err = yv - tv
        loss = jnp.sum(_colsum(err * err), axis=1, keepdims=True) * (0.5 / D)
        dy = err * (1.0 / D)
        dx2 = _rms_bwd(dy * fg, xh, r)
        return ([dx2, g2v * dx2],
                [jnp.broadcast_to(loss, (1, LANE)), _colsum(dy * xh), _colsum(dx2 * o2v)])

    dx2, do2, loss_l, d_final_g, d_g2 = mm_ep_pipe(
        "mm_ffn_out", act, w_ffn_out_f, 1, whole, ep_final, [(x1, 1, 0), (tgt, 1, 0)],
        [(D, F32, 1), (D, BF16, 1)], row_tiles(FH), b_kn=True, consts=[g2, fg_row], sums=[LANE, D, D])

    g_ffn_out = mm("mm_g_ffn_out", act, do2, "tn", BF16, tiles=(FFN_BLK, 1024, 1024))

    def ep_dswiglu(accs, fp):
        (da,) = accs
        fg, fu = fp[:, 0:FFN_BLK].astype(F32), fp[:, FFN_BLK:2 * FFN_BLK].astype(F32)
        sg = _sig(fg)
        return [jnp.concatenate([da * fu * (sg * (1.0 + fg * (1.0 - sg))), da * (fg * sg)], axis=1)]

    (df,) = mm_ep("mm_dact", do2, w_ffn_out_f, 1, lambda j, q: j, ep_dswiglu, [(f_pair, 2, 0)],
                  [(2 * FH, BF16, 2)], ffn_tiles)
    g_ffn_in_t = mm("mm_g_ffn_in", df, h2, "tn", BF16, tiles=(FFN_BLK, 1024, 1024), o_rot=pair_map)

    def pair_go(tag, grads_t, deps=()):
        srcs = [g.reshape(NDEV, -1, D) for g in grads_t]
        return exchange_start("pair_" + tag + "_start", pair_plan, NCHIP * len(srcs), srcs,
                              [((NCHIP,) + s.shape[1:], s.dtype) for s in srcs], deps)

    def chip_go(tag, names, pair_started, after):
        own, from_sibling = exchange_wait("pair_" + tag + "_wait", pair_plan, pair_started, after)
        partials = [pair_sum("pair_sum_" + n, g, r) for n, g, r in zip(names, own, from_sibling)]
        return exchange_start("chip_" + tag + "_start", chip_plan, (NCHIP - 1) * len(partials), partials,
                              [((NCHIP - 1,) + p.shape[1:], p.dtype) for p in partials])

    def chip_done(tag, chip_started, after):
        partials, from_chips = exchange_wait("chip_" + tag + "_wait", chip_plan, chip_started, after)
        return list(zip(partials, from_chips))

    pair_ffn = pair_go("ffn", [g_ffn_out, g_ffn_in_t])

    dh2 = mm("mm_dh2", df, w_ffn_in_t, "nn", BF16, tiles=(1024, 1024, FFN_BLK), b_rot=pair_map,
             deps=[pair_ffn[4]])

    def f_dnorm2(dh, x1v, dx2v, o1v, g, sc, g1v):
        dh, o1v = dh.astype(F32), o1v.astype(F32)
        r, xh = _rms_stats(x1v)
        dxh = dh * (1.0 + sc) * g
        dx1 = dx2v + _rms_bwd(dxh, xh, r)
        return ([dx1, g1v * dx1],
                [_colsum(dh * xh * g), _colsum(dh), _colsum(dh * (1.0 + sc) * xh), _colsum(dx1 * o1v)])

    dx1, do1, d_sc2, d_sh2, d_n2g, d_g1 = rowwise(
        "dnorm2", f_dnorm2, [dh2, x1, dx2, o1], [norm2_g, sc2, g1], [(D, F32), (D, BF16)], [D, D, D, D], 256)

    g_out = mm("mm_g_out", merged, do1, "tn", BF16)
    chip_ffn = chip_go("ffn", ("w_ffn_out", "w_ffn_in"), pair_ffn, g_out)

    def ep_dmerge(accs, yc_v, z2p, gates):
        (dm,) = accs
        za, zb = z2p[:, 0:MRG_BLK].astype(F32), z2p[:, MRG_BLK:2 * MRG_BLK].astype(F32)
        sc_, ss_, sb_ = _sig(gates[:, 0:MRG_BLK]), _sig(gates[:, MRG_BLK:2 * MRG_BLK]), _sig(zb)
        dys = dm * ss_
        dz2 = jnp.concatenate([dys * sb_, dys * za * sb_ * (1.0 - sb_)], axis=1)
        dgates = jnp.concatenate([dm * yc_v * sc_ * (1.0 - sc_), dm * (za * sb_) * ss_ * (1.0 - ss_)], axis=1)
        return [dm * sc_, dz2, dgates]

    dyconv, dz2, dz = mm_ep("mm_dmerged", do1, w_out_f, 1, lambda j, q: j, ep_dmerge,
                            [(y_conv, 1, 0), (z2_pair, 2, 0), (z, 2, 0)],
                            [(D, BF16, 1), (2 * D, BF16, 2), (ZW, BF16, 2)], (512, MRG_BLK, 1024), deps=[chip_ffn[4]])

    g_conv_proj_t = mm("mm_g_conv_proj", dyconv, s_act, "tn", BF16)
    mrg_map = lambda t: pair_of(t, n_mrg)
    dgl = mm("mm_dgl", dz2, ssm_glu_t, "nn", tiles=(1024, SW, MRG_BLK), b_rot=mrg_map)
    g_ssm_glu_t = mm("mm_g_ssm_glu", dz2, gl, "tn", BF16, tiles=(MRG_BLK, SW, 1024), o_rot=mrg_map)
    pair_mid = pair_go("mid", [g_out, g_conv_proj_t, g_ssm_glu_t])
    ds = mm("mm_ds", dyconv, conv_proj_t, "nn", deps=[pair_mid[4]])
    dz, d_lng, d_lnb, d_cb, d_cw32 = conv_bwd(ds, yc, z, w32, conv_ln_g, conv_ln_b, dz)
    dz, d_d, d_ar, d_ai, d_wb_re, d_wb_im, d_wc = ssm_bwd(
        dgl, ypre, z, xs_re, xs_im, to_b(wbt_re), to_b(wbt_im), to_b(wct), e_re_b, e_im_b, dvec, dz)
    chip_mid = chip_go("mid", ("w_out", "conv_proj", "ssm_glu"), pair_mid, dz)

    d_bb_re = _diag_blocks(d_wb_re.transpose(0, 2, 1), NP, GH).reshape(NST, GH)
    d_bb_im = _diag_blocks(d_wb_im.transpose(0, 2, 1), NP, GH).reshape(NST, GH)
    d_wct = d_wc.transpose(0, 2, 1)
    d_c_re = _diag_blocks(d_wct[:, :, 0:SB], GH, NP)
    d_c_im = -_diag_blocks(d_wct[:, :, SB:2 * SB], GH, NP)
    d_a_re, d_a_im, d_ldt, d_b_re, d_b_im = disc_bwd(
        a_re_c, a_im_c, ldt_c, b_re_r, b_im_r, expand, d_ar.reshape(NST, 1), d_ai.reshape(NST, 1), d_bb_re, d_bb_im)

    small_local = [jnp.concatenate([d_g1, d_sh2, d_sc2, d_g2], axis=1).reshape(-1), d_cw32[0:KC].reshape(-1),
                   d_cb.reshape(-1), d_lng.reshape(-1), d_lnb.reshape(-1), d_a_re.reshape(-1), d_a_im.reshape(-1),
                   d_b_re.reshape(-1), d_b_im.reshape(-1), d_c_re.reshape(-1), d_c_im.reshape(-1), d_d.reshape(-1),
                   d_ldt.reshape(-1), d_n2g.reshape(-1), d_final_g.reshape(-1), loss_l[0, 0:1]]
    small_sizes = [v.shape[0] for v in small_local]
    small_pack = _pad_rows(jnp.concatenate(small_local), 256 * LANE).reshape(-1, LANE)
    small_go = exchange_start("gather_small_start", gather_plan, NDEV - 1, [small_pack],
                              [((NDEV,) + small_pack.shape, F32)], deps=[chip_mid[4]])

    g_in_t = mm("mm_g_in", dz, h1, "tn", BF16, tiles=(CW, 1024, 2048), o_rot=Z_ROT, deps=[small_go[4]])
    pair_in = pair_go("in", [g_in_t])

    dh1 = mm("mm_dh1", dz, w_in_t, "nn", BF16, tiles=(2048, 1024, CW), b_rot=Z_ROT, deps=[pair_in[4]])

    def f_dnorm1(dh, xv, dx1v, g, sc):
        dh = dh.astype(F32)
        r, xh = _rms_stats(xv)
        dxh = dh * (1.0 + sc) * g
        return ([dx1v + _rms_bwd(dxh, xh, r)],
                [_colsum(dh * xh * g), _colsum(dh), _colsum(dh * (1.0 + sc) * xh)])

    grad_x, d_sc1, d_sh1, d_n1g = rowwise(
        "dnorm1", f_dnorm1, [dh1, xs, dx1], [n1g, sc1], [(D, F32)], [D, D, D], 256)
    chip_in = chip_go("in", ("w_in",), pair_in, grad_x)

    weights = {
        "w_ada": (w_ada, m_w_ada, v_w_ada), "b_ada": (b_ada, m_b_ada, v_b_ada), "norm1_g": (norm1_g, m_norm1_g, v_norm1_g),
        "w_in": (w_in, m_w_in, v_w_in), "conv_w": (conv_w, m_conv_w, v_conv_w), "conv_b": (conv_b, m_conv_b, v_conv_b),
        "conv_ln_g": (conv_ln_g, m_conv_ln_g, v_conv_ln_g), "conv_ln_b": (conv_ln_b, m_conv_ln_b, v_conv_ln_b),
        "conv_proj": (conv_proj, m_conv_proj, v_conv_proj), "ssm_a_re": (ssm_a_re, m_ssm_a_re, v_ssm_a_re),
        "ssm_a_im": (ssm_a_im, m_ssm_a_im, v_ssm_a_im), "ssm_b_re": (ssm_b_re, m_ssm_b_re, v_ssm_b_re),
        "ssm_b_im": (ssm_b_im, m_ssm_b_im, v_ssm_b_im), "ssm_c_re": (ssm_c_re, m_ssm_c_re, v_ssm_c_re),
        "ssm_c_im": (ssm_c_im, m_ssm_c_im, v_ssm_c_im), "ssm_d": (ssm_d, m_ssm_d, v_ssm_d),
        "ssm_log_dt": (ssm_log_dt, m_ssm_log_dt, v_ssm_log_dt), "ssm_glu": (ssm_glu, m_ssm_glu, v_ssm_glu),
        "w_out": (w_out, m_w_out, v_w_out), "norm2_g": (norm2_g, m_norm2_g, v_norm2_g),
        "w_ffn_in": (w_ffn_in, m_w_ffn_in, v_w_ffn_in), "w_ffn_out": (w_ffn_out, m_w_ffn_out, v_w_ffn_out),
        "final_g": (final_g, m_final_g, v_final_g),
    }
    order = list(weights)
    big = ("w_ada", "w_in", "conv_proj", "ssm_glu", "w_out", "w_ffn_in", "w_ffn_out")
    grads, delta, new_m, new_v = {}, {}, {}, {}

    def adam_big(n, g2d, transposed=False):
        wv, mv, vv = weights[n]
        shp = wv.shape
        t_in = (lambda a: a.reshape(shp[-2:]).T) if transposed else (lambda a: a.reshape(shp[-2:]))
        t_out = (lambda a: a.T.reshape(shp)) if transposed else (lambda a: a.reshape(shp))
        w2 = t_in(wv)
        if isinstance(g2d, tuple):
            partial, recv = [p.reshape((p.shape[0],) + w2.shape) for p in g2d]
            g2d, d_, m_, v_ = chip_sum_adam("adam_" + n, partial, recv, w2, t_in(mv), t_in(vv))
        else:
            d_, m_, v_ = adam("adam_" + n, w2, g2d, t_in(mv), t_in(vv))
        grads[n], delta[n], new_m[n], new_v[n] = t_out(g2d), t_out(d_), t_out(m_), t_out(v_)
        return d_

    parts_ffn_out, parts_ffn_in = chip_done("ffn", chip_ffn, chip_in[4])
    adam_big("w_ffn_out", parts_ffn_out)
    last = adam_big("w_ffn_in", parts_ffn_in, transposed=True)
    parts_out, parts_conv_proj, parts_ssm_glu = chip_done("mid", chip_mid, last)
    adam_big("w_out", parts_out)
    adam_big("conv_proj", parts_conv_proj, transposed=True)
    adam_big("ssm_glu", parts_ssm_glu, transposed=True)

    late_local = [d_sh1.reshape(-1), d_sc1.reshape(-1), d_n1g.reshape(-1)]
    late_pack = _pad_rows(jnp.concatenate(late_local), 16 * LANE).reshape(-1, LANE)
    (late_all,) = all_gather("gather_small_late", [late_pack])
    _, (small_all,) = exchange_wait("gather_small_wait", gather_plan, small_go, late_all, place_own=True)

    def unpack(vec, sizes):
        out, pos = [], 0
        for n in sizes:
            out.append(vec[pos:pos + n])
            pos += n
        return out

    g_sh1, g_sc1, g_n1g = unpack(sum_slots("sum_small_late", late_all).reshape(-1), [D, D, D])
    (g_mod_rest, g_cw_full, g_cb, g_lng, g_lnb, g_a_re, g_a_im, g_b_re, g_b_im, g_c_re, g_c_im, g_d, g_ldt,
     g_n2g, g_fg, loss_sum) = unpack(sum_slots("sum_small", small_all).reshape(-1), small_sizes)
    g_b_ada = jnp.concatenate([g_sh1, g_sc1, g_mod_rest])
    loss = loss_sum[0]
    dmod_all = jnp.concatenate([late_all.reshape(NDEV, -1)[:, 0:2 * D], small_all.reshape(NDEV, -1)[:, 0:4 * D]],
                               axis=1)
    g_w_ada = ada_grad(act_all, lax.dynamic_slice_in_dim(dmod_all, me * ncol, ncol, axis=1))
    ccol = conv_w.shape[2]
    g_conv_w = lax.dynamic_slice_in_dim(g_cw_full.reshape(KC, CW), me * ccol, ccol, axis=1)

    adam_big("w_ada", g_w_ada)
    grads.update({
        "b_ada": g_b_ada.reshape(b_ada.shape), "norm1_g": g_n1g.reshape(norm1_g.shape),
        "conv_w": g_conv_w[None], "conv_b": g_cb.reshape(conv_b.shape),
        "conv_ln_g": g_lng.reshape(conv_ln_g.shape), "conv_ln_b": g_lnb.reshape(conv_ln_b.shape),
        "ssm_a_re": g_a_re.reshape(ssm_a_re.shape),
        "ssm_a_im": g_a_im.reshape(ssm_a_im.shape), "ssm_b_re": g_b_re.reshape(ssm_b_re.shape),
        "ssm_b_im": g_b_im.reshape(ssm_b_im.shape), "ssm_c_re": g_c_re.reshape(ssm_c_re.shape),
        "ssm_c_im": g_c_im.reshape(ssm_c_im.shape), "ssm_d": g_d.reshape(ssm_d.shape),
        "ssm_log_dt": g_ldt.reshape(ssm_log_dt.shape),
        "norm2_g": g_n2g.reshape(norm2_g.shape),
        "final_g": g_fg.reshape(final_g.shape),
    })
    small = [n for n in order if n not in big]
    def rows(a):
        if a.ndim == 4 and a.shape[-1] < a.shape[-2]:
            a = a.swapaxes(-1, -2)
        return a.reshape(1, -1) if a.ndim == 1 else a.reshape(-1, a.shape[-1])

    def unrows(a, shp):
        if len(shp) == 4 and shp[-1] < shp[-2]:
            return a.reshape(shp[:-2] + (shp[-1], shp[-2])).swapaxes(-1, -2)
        return a.reshape(shp)

    small_out = adam_many("adam_small", [rows(weights[n][0]) for n in small], [rows(grads[n]) for n in small],
                          [rows(weights[n][1]) for n in small], [rows(weights[n][2]) for n in small])
    for q, n in enumerate(small):
        shp = weights[n][0].shape
        delta[n], new_m[n], new_v[n] = [unrows(small_out[t * len(small) + q], shp) for t in range(3)]

    (parts_in,) = chip_done("in", chip_in, small_out[0])
    adam_big("w_in", parts_in, transposed=True)

    return (loss, grad_x[None], *[grads[n] for n in order], *[delta[n] for n in order],
            *[new_m[n] for n in order], *[new_v[n] for n in order])
```

```python
import functools
import math

import jax
import jax.numpy as jnp
from jax import lax
from jax.experimental import pallas as pl
from jax.experimental.pallas import tpu as pltpu

F32 = jnp.float32
BF16 = jnp.bfloat16

D = 1024
CW = 512
KC = 31
SW = 512
NG = 32
GH = 16
NP = 64
NST = NG * NP
FH = 2816
FFN_BLK = 1408
MRG_BLK = 1024
NMOD = 6
NDEV = 8
EPS = 1e-6
CB = 128
SB = 512
NBLK = SW // CB
HALO = 32
ZW = 2 * CW + SW + 2 * D
Z_ROT = lambda j: (j + 3) % (ZW // CW)
ZB_A, ZB_G, ZB_U = 4, 5, 6

ADAM_LR = 0.001
ADAM_B1 = 0.9
ADAM_B2 = 0.999
ADAM_EPS = 1e-08
ADAM_WD = 0.01
ADAM_STEP = 10

V7X_VMEM_BYTES = 64 * 1024 * 1024
VMEM_LIMIT = V7X_VMEM_BYTES - 8 * 1024 * 1024
LANE = 128
MESH = pl.DeviceIdType.MESH
ANY_SPEC = pl.BlockSpec(memory_space=pl.ANY)


def _params(sem=None, **kw):
    if sem is not None:
        kw["dimension_semantics"] = sem
    return pltpu.CompilerParams(vmem_limit_bytes=VMEM_LIMIT, **kw)


def _tile(n, most):
    best = None
    for t in range(LANE, most + 1, LANE):
        if n % t == 0:
            best = t
    if best is None:
        raise ValueError(f"no tile for {n}")
    return best


def _sig(x):
    return jax.nn.sigmoid(x)


def mm(name, a, b, mode, out_dtype=F32, tiles=None, b_rot=None, o_rot=None, deps=()):
    if mode == "nn":
        (m, k), (k2, n) = a.shape, b.shape
    elif mode == "nt":
        (m, k), (n, k2) = a.shape, b.shape
    else:
        (k, m), (k2, n) = a.shape, b.shape
    assert k == k2, (name, a.shape, b.shape)
    bm, bn, bk = tiles or (_tile(m, 1024), _tile(n, 1408), _tile(k, 1408 if k % 1408 == 0 else 1024))
    bm, bn, bk = min(bm, m), min(bn, n), min(bk, k)
    assert m % bm == 0 and n % bn == 0 and k % bk == 0, (name, m, n, k, bm, bn, bk)
    nk = k // bk
    rot = lambda idx, r: idx if r is None else r(idx)
    if mode == "nn":
        a_spec = pl.BlockSpec((bm, bk), lambda i, j, kk: (i, kk))
        b_spec = pl.BlockSpec((bk, bn), lambda i, j, kk: (rot(kk, b_rot), j))
        dims = (((1,), (0,)), ((), ()))
    elif mode == "nt":
        a_spec = pl.BlockSpec((bm, bk), lambda i, j, kk: (i, kk))
        b_spec = pl.BlockSpec((bn, bk), lambda i, j, kk: (rot(j, b_rot), kk))
        dims = (((1,), (1,)), ((), ()))
    else:
        assert b_rot is None
        a_spec = pl.BlockSpec((bk, bm), lambda i, j, kk: (kk, i))
        b_spec = pl.BlockSpec((bk, bn), lambda i, j, kk: (kk, j))
        dims = (((0,), (0,)), ((), ()))

    def body(a_ref, b_ref, *rest):
        o_ref, acc_ref = rest[-2:]
        kk = pl.program_id(2)

        @pl.when(kk == 0)
        def _():
            acc_ref[...] = jnp.zeros_like(acc_ref)

        acc_ref[...] += lax.dot_general(a_ref[...], b_ref[...], dims, preferred_element_type=F32)

        @pl.when(kk == nk - 1)
        def _():
            o_ref[...] = acc_ref[...].astype(o_ref.dtype)

    return pl.pallas_call(
        body, name=name,
        grid=(m // bm, n // bn, nk),
        in_specs=[a_spec, b_spec] + [ANY_SPEC] * len(deps),
        out_specs=pl.BlockSpec((bm, bn), lambda i, j, kk: (rot(i, o_rot), j)),
        out_shape=jax.ShapeDtypeStruct((m, n), out_dtype),
        scratch_shapes=[pltpu.VMEM((bm, bn), F32)],
        compiler_params=_params(("parallel", "parallel", "arbitrary")),
    )(a, b, *deps)


def mm_ep(name, a, b, n_acc, acc_block, epilogue, extras, outs, tiles, deps=(), b_kn=False, k_map=None,
          consts=(), sums=()):
    m, k = a.shape
    bm, bn, bk = tiles
    bm = min(bm, m)
    nj = outs[0][0] // (outs[0][2] * bn)
    nk = k // bk
    assert m % bm == 0 and k % bk == 0 and b.shape[0 if b_kn else 1] == k, (name, a.shape, b.shape, tiles)
    assert not sums or nj == 1, name
    ne, nc, no, ns, nd = len(extras), len(consts), len(outs), len(sums), len(deps)
    dims = (((1,), (0,)), ((), ())) if b_kn else (((1,), (1,)), ((), ()))
    kmap = (lambda kk: kk) if k_map is None else k_map

    def body(*refs):
        a_ref, b_refs = refs[0], refs[1:1 + n_acc]
        e_refs = refs[1 + n_acc:1 + n_acc + ne + nc]
        first_out = 1 + n_acc + ne + nc + nd
        o_refs = refs[first_out:first_out + no]
        s_refs = refs[first_out + no:first_out + no + ns]
        acc_refs = refs[first_out + no + ns:]
        av = a_ref[...]
        prods = [lax.dot_general(av, b_ref[...], dims, preferred_element_type=F32) for b_ref in b_refs]

        if ns:
            @pl.when((pl.program_id(0) == 0) & (pl.program_id(2) == 0))
            def _():
                for s_ref in s_refs:
                    s_ref[...] = jnp.zeros_like(s_ref)

        def finish(accs):
            res = epilogue(accs, *[e[...] for e in e_refs])
            tiles_out, sums_out = res if ns else (res, ())
            for o_ref, v in zip(o_refs, tiles_out):
                o_ref[...] = v.astype(o_ref.dtype)
            for s_ref, v in zip(s_refs, sums_out):
                s_ref[...] += v

        if nk == 1:
            finish(prods)
        else:
            kk = pl.program_id(2)

            @pl.when(kk == 0)
            def _():
                for acc_ref in acc_refs:
                    acc_ref[...] = jnp.zeros_like(acc_ref)

            for acc_ref, p in zip(acc_refs, prods):
                acc_ref[...] += p

            @pl.when(kk == nk - 1)
            def _():
                finish([acc_ref[...] for acc_ref in acc_refs])

    in_specs = [pl.BlockSpec((bm, bk), lambda i, j, kk: (i, kk))]
    if b_kn:
        in_specs += [pl.BlockSpec((bk, bn), functools.partial(lambda i, j, kk, q: (kmap(kk), acc_block(j, q)), q=q))
                     for q in range(n_acc)]
    else:
        in_specs += [pl.BlockSpec((bn, bk), functools.partial(lambda i, j, kk, q: (acc_block(j, q), kmap(kk)), q=q))
                     for q in range(n_acc)]
    in_specs += [pl.BlockSpec((bm, w * bn), functools.partial(lambda i, j, kk, off: (i, j + off), off=off))
                 for (_, w, off) in extras]
    in_specs += [pl.BlockSpec((1, bn), lambda i, j, kk: (0, j)) for _ in consts]
    in_specs += [ANY_SPEC] * nd
    out_specs = [pl.BlockSpec((bm, w * bn), lambda i, j, kk: (i, j)) for (_, _, w) in outs]
    out_specs += [pl.BlockSpec((1, w), lambda i, j, kk: (0, 0)) for w in sums]
    out_shape = [jax.ShapeDtypeStruct((m, cols), dt) for (cols, dt, _) in outs]
    out_shape += [jax.ShapeDtypeStruct((1, w), F32) for w in sums]
    return pl.pallas_call(
        body, name=name, grid=(m // bm, nj, nk),
        in_specs=in_specs, out_specs=out_specs, out_shape=out_shape,
        scratch_shapes=[pltpu.VMEM((bm, bn), F32)] * (n_acc if nk > 1 else 0),
        compiler_params=_params(("arbitrary",) * 3 if sums else ("parallel", "parallel", "arbitrary")),
    )(a, *[b] * n_acc, *[e[0] for e in extras], *consts, *deps)


def mm_ep_pipe(name, a, b, n_acc, acc_block, epilogue, extras, outs, tiles, deps=(), b_kn=False, consts=(), sums=()):
    m, k = a.shape
    bm, bn, bk = tiles
    bm = min(bm, m)
    assert bk == k and m % bm == 0 and b.shape[0 if b_kn else 1] == k, (name, a.shape, b.shape, tiles)
    ni, nj = m // bm, outs[0][0] // (outs[0][2] * bn)
    nt = ni * nj
    assert not sums or nj == 1, name
    ne, nc, no, ns, nd = len(extras), len(consts), len(outs), len(sums), len(deps)
    dims = (((1,), (0,)), ((), ())) if b_kn else (((1,), (1,)), ((), ()))
    cur_i = lambda t: jnp.minimum(t, nt - 1) // nj
    cur_j = lambda t: jnp.minimum(t, nt - 1) % nj
    prev_i = lambda t: jnp.maximum(t - 1, 0) // nj
    prev_j = lambda t: jnp.maximum(t - 1, 0) % nj

    def body(*refs):
        a_ref, b_refs = refs[0], refs[1:1 + n_acc]
        e_refs = refs[1 + n_acc:1 + n_acc + ne + nc]
        first_out = 1 + n_acc + ne + nc + nd
        o_refs = refs[first_out:first_out + no]
        s_refs = refs[first_out + no:first_out + no + ns]
        acc_ref = refs[first_out + no + ns]
        t = pl.program_id(0)

        @pl.when(t == 0)
        def _():
            acc_ref[...] = jnp.zeros_like(acc_ref)
            for s_ref in s_refs:
                s_ref[...] = jnp.zeros_like(s_ref)

        slot = t % 2
        done = [acc_ref[(1 - slot) * n_acc + q] for q in range(n_acc)]
        av = a_ref[...]
        for q, b_ref in enumerate(b_refs):
            acc_ref[slot * n_acc + q] = lax.dot_general(av, b_ref[...], dims, preferred_element_type=F32)
        res = epilogue(done, *[e[...] for e in e_refs])
        tiles_out, sums_out = res if ns else (res, ())
        for o_ref, v in zip(o_refs, tiles_out):
            o_ref[...] = v.astype(o_ref.dtype)
        live = (t >= 1).astype(F32)
        for s_ref, v in zip(s_refs, sums_out):
            s_ref[...] += v * live

    in_specs = [pl.BlockSpec((bm, k), lambda t: (cur_i(t), 0))]
    if b_kn:
        in_specs += [pl.BlockSpec((k, bn), functools.partial(lambda t, q: (0, acc_block(cur_j(t), q)), q=q))
                     for q in range(n_acc)]
    else:
        in_specs += [pl.BlockSpec((bn, k), functools.partial(lambda t, q: (acc_block(cur_j(t), q), 0), q=q))
                     for q in range(n_acc)]
    in_specs += [pl.BlockSpec((bm, w * bn), functools.partial(lambda t, off: (prev_i(t), prev_j(t) + off), off=off))
                 for (_, w, off) in extras]
    in_specs += [pl.BlockSpec((1, bn), lambda t: (0, prev_j(t))) for _ in consts]
    in_specs += [ANY_SPEC] * nd
    out_specs = [pl.BlockSpec((bm, w * bn), lambda t: (prev_i(t), prev_j(t))) for (_, _, w) in outs]
    out_specs += [pl.BlockSpec((1, w), lambda t: (0, 0)) for w in sums]
    out_shape = [jax.ShapeDtypeStruct((m, cols), dt) for (cols, dt, _) in outs]
    out_shape += [jax.ShapeDtypeStruct((1, w), F32) for w in sums]
    return pl.pallas_call(
        body, name=name, grid=(nt + 1,),
        in_specs=in_specs, out_specs=out_specs, out_shape=out_shape,
        scratch_shapes=[pltpu.VMEM((2 * n_acc, bm, bn), F32)],
        compiler_params=_params(("arbitrary",)),
    )(a, *[b] * n_acc, *[e[0] for e in extras], *consts, *deps)


def rowwise(name, fn, rows, consts, out_rows, out_sums, ts, alias=None, deps=()):
    rows = [r if isinstance(r, tuple) else (r, r.shape[1], 0) for r in rows]
    out_rows = [o if len(o) == 4 else (o[0], o[1], o[0], 0) for o in out_rows]
    s = rows[0][0].shape[0]
    nt = s // ts
    nr, nc, no, ns = len(rows), len(consts), len(out_rows), len(out_sums)
    in_specs = [pl.BlockSpec((ts, w), functools.partial(lambda i, cb: (i, cb), cb=cb)) for (_, w, cb) in rows]
    in_specs += [pl.BlockSpec(c.shape, lambda i: (0, 0)) for c in consts]
    operands = [r[0] for r in rows] + list(consts)
    aliases = {}
    if alias is not None:
        in_specs.append(pl.BlockSpec(memory_space=pl.ANY))
        operands.append(alias[0])
        aliases = {nr + nc: alias[1]}
    in_specs += [ANY_SPEC] * len(deps)
    operands += list(deps)
    out_shape = [jax.ShapeDtypeStruct((s, tw), dt) for (_, dt, tw, _) in out_rows]
    out_shape += [jax.ShapeDtypeStruct((1, w), F32) for w in out_sums]
    out_specs = [pl.BlockSpec((ts, w), functools.partial(lambda i, cb: (i, cb), cb=cb)) for (w, _, _, cb) in out_rows]
    out_specs += [pl.BlockSpec((1, w), lambda i: (0, 0)) for w in out_sums]
    n_in = len(operands)

    def body(*refs):
        ins, outs = refs[:nr + nc], refs[n_in:]
        i = pl.program_id(0)
        ro, so = fn(*[r[...] for r in ins])
        for q in range(no):
            outs[q][...] = ro[q].astype(outs[q].dtype)
        if ns:
            @pl.when(i == 0)
            def _():
                for q in range(ns):
                    outs[no + q][...] = jnp.zeros_like(outs[no + q])

            for q in range(ns):
                outs[no + q][...] += so[q]

    return pl.pallas_call(
        body, name=name, grid=(nt,),
        in_specs=in_specs, out_specs=out_specs, out_shape=out_shape, input_output_aliases=aliases,
        compiler_params=_params(("arbitrary",) if ns else ("parallel",)),
    )(*operands)


def _colsum(v):
    return jnp.sum(v, axis=0, keepdims=True)


def _rms_stats(xv):
    r = lax.rsqrt(jnp.mean(xv * xv, axis=-1, keepdims=True) + EPS)
    return r, xv * r


def _rms_bwd(dxhat, xhat, r):
    return r * (dxhat - xhat * jnp.mean(dxhat * xhat, axis=-1, keepdims=True))


def _gelu(v):
    k = math.sqrt(2.0 / math.pi)
    t = jnp.tanh(k * (v + 0.044715 * v * v * v))
    return 0.5 * v * (1.0 + t), t


def _gelu_grad(v, t):
    k = math.sqrt(2.0 / math.pi)
    return 0.5 * (1.0 + t) + 0.5 * v * (1.0 - t * t) * k * (1.0 + 3.0 * 0.044715 * v * v)


CONV_TS = 512
CONV_CH = 64


def _ln_fwd(yc, g, b):
    mu = jnp.mean(yc, axis=-1, keepdims=True)
    xc = yc - mu
    rstd = lax.rsqrt(jnp.mean(xc * xc, axis=-1, keepdims=True) + EPS)
    nhat = xc * rstd
    return nhat, rstd, nhat * g + b


SUBL = 8


def _shifted_copies(buf, sh, ts):
    for b in range(1, SUBL):
        sh[b - 1] = buf[pl.ds(b, ts + HALO - SUBL), :]


def _shifted(buf, sh, start):
    b = start % SUBL
    if b == 0:
        return buf[pl.ds(start, CONV_CH), :]
    return sh[b - 1, pl.ds(start - b, CONV_CH), :]


def conv_fwd(z, w32, cb, lg, lb):
    s = z.shape[0]
    ts = CONV_TS
    nt = s // ts
    hb = ts // HALO

    def body(a_ref, g_ref, ah_ref, gh_ref, w_ref, cb_ref, lg_ref, lb_ref, yc_ref, s_ref, ubuf, ush):
        i = pl.program_id(0)
        first = (i > 0).astype(F32)
        ubuf[0:HALO, :] = ah_ref[...] * _sig(gh_ref[...]) * first
        ubuf[HALO:HALO + ts, :] = a_ref[...] * _sig(g_ref[...])
        _shifted_copies(ubuf, ush, ts)
        for c0 in range(0, ts, CONV_CH):
            acc = jnp.zeros((CONV_CH, CW), F32)
            for k in range(KC):
                acc = acc + w_ref[k:k + 1, :] * _shifted(ubuf, ush, c0 + k + 2)
            yc = acc + cb_ref[...]
            yc_ref[c0:c0 + CONV_CH, :] = yc
            _, _, ln = _ln_fwd(yc, lg_ref[...], lb_ref[...])
            s_ref[c0:c0 + CONV_CH, :] = (ln * _sig(ln)).astype(s_ref.dtype)

    cur = lambda cbk: pl.BlockSpec((ts, CW), functools.partial(lambda i, q: (i, q), q=cbk))
    prev = lambda cbk: pl.BlockSpec((HALO, CW), functools.partial(lambda i, q: (jnp.maximum(i * hb - 1, 0), q), q=cbk))
    const = lambda a: pl.BlockSpec(a.shape, lambda i: (0, 0))
    return pl.pallas_call(
        body, name="conv_fwd", grid=(nt,),
        in_specs=[cur(ZB_A), cur(ZB_G), prev(ZB_A), prev(ZB_G), const(w32), const(cb), const(lg), const(lb)],
        out_specs=[pl.BlockSpec((ts, CW), lambda i: (i, 0)), pl.BlockSpec((ts, CW), lambda i: (i, 0))],
        out_shape=[jax.ShapeDtypeStruct((s, CW), F32), jax.ShapeDtypeStruct((s, CW), BF16)],
        scratch_shapes=[pltpu.VMEM((HALO + ts, CW), F32), pltpu.VMEM((SUBL - 1, ts + HALO - SUBL, CW), F32)],
        compiler_params=_params(("parallel",)),
    )(z, z, z, z, w32, cb, lg, lb)


def conv_bwd(ds, yc, z, w32, lg, lb, dz):
    s = z.shape[0]
    ts = CONV_TS
    nt = s // ts
    hb = ts // HALO
    last_hb = s // HALO - 1

    def ln_bwd(dsv, ycv, g, b):
        nhat, rstd, ln = _ln_fwd(ycv, g, b)
        sg = _sig(ln)
        dln = dsv * (sg * (1.0 + ln * (1.0 - sg)))
        dnh = dln * g
        dyc = rstd * (dnh - jnp.mean(dnh, axis=-1, keepdims=True)
                      - nhat * jnp.mean(dnh * nhat, axis=-1, keepdims=True))
        return dyc, dln, nhat

    def body(ds_ref, yc_ref, dsn_ref, ycn_ref, a_ref, g_ref, ah_ref, gh_ref, w_ref, lg_ref, lb_ref, dz_in,
             dz_ref, dlg_ref, dlb_ref, dcb_ref, dw_ref, dbuf, ubuf, dsh, ush, dwacc):
        i = pl.program_id(0)

        @pl.when(i == 0)
        def _():
            dlg_ref[...] = jnp.zeros_like(dlg_ref)
            dlb_ref[...] = jnp.zeros_like(dlb_ref)
            dcb_ref[...] = jnp.zeros_like(dcb_ref)
            dw_ref[...] = jnp.zeros_like(dw_ref)
            dwacc[...] = jnp.zeros_like(dwacc)

        lg, lb = lg_ref[...], lb_ref[...]
        dyc, dln, nhat = ln_bwd(ds_ref[...], yc_ref[...], lg, lb)
        dlg_ref[...] += _colsum(dln * nhat)
        dlb_ref[...] += _colsum(dln)
        dcb_ref[...] += _colsum(dyc)
        dbuf[0:ts, :] = dyc
        nxt = (i < nt - 1).astype(F32)
        dbuf[ts:ts + HALO, :] = ln_bwd(dsn_ref[...], ycn_ref[...], lg, lb)[0] * nxt
        first = (i > 0).astype(F32)
        ubuf[0:HALO, :] = ah_ref[...] * _sig(gh_ref[...]) * first
        ubuf[HALO:HALO + ts, :] = a_ref[...] * _sig(g_ref[...])
        _shifted_copies(dbuf, dsh, ts)
        _shifted_copies(ubuf, ush, ts)
        for c0 in range(0, ts, CONV_CH):
            du = jnp.zeros((CONV_CH, CW), F32)
            dyc_c = dbuf[c0:c0 + CONV_CH, :]
            for k in range(KC):
                du = du + w_ref[k:k + 1, :] * _shifted(dbuf, dsh, c0 + KC - 1 - k)
                prod = dyc_c * _shifted(ubuf, ush, c0 + k + 2)
                dwacc[k] += jnp.sum(prod.reshape(CONV_CH // SUBL, SUBL, CW), axis=0)
            av = a_ref[c0:c0 + CONV_CH, :]
            sg = _sig(g_ref[c0:c0 + CONV_CH, :])
            dz_ref[c0:c0 + CONV_CH, 0:CW] = (du * sg).astype(dz_ref.dtype)
            dz_ref[c0:c0 + CONV_CH, CW:2 * CW] = (du * av * sg * (1.0 - sg)).astype(dz_ref.dtype)

        @pl.when(i == nt - 1)
        def _():
            for k in range(KC):
                dw_ref[k:k + 1, :] = _colsum(dwacc[k])

    cur = lambda w, cbk: pl.BlockSpec((ts, w), functools.partial(lambda i, q: (i, q), q=cbk))
    prev = lambda cbk: pl.BlockSpec((HALO, CW), functools.partial(lambda i, q: (jnp.maximum(i * hb - 1, 0), q), q=cbk))
    nxt_spec = pl.BlockSpec((HALO, CW), lambda i: (jnp.minimum((i + 1) * hb, last_hb), 0))
    const = lambda a: pl.BlockSpec(a.shape, lambda i: (0, 0))
    acc = lambda r: pl.BlockSpec((r, CW), lambda i: (0, 0))
    return pl.pallas_call(
        body, name="conv_bwd", grid=(nt,),
        in_specs=[cur(CW, 0), cur(CW, 0), nxt_spec, nxt_spec, cur(CW, ZB_A), cur(CW, ZB_G), prev(ZB_A), prev(ZB_G),
                  const(w32), const(lg), const(lb), pl.BlockSpec(memory_space=pl.ANY)],
        out_specs=[pl.BlockSpec((ts, 2 * CW), lambda i: (i, ZB_A // 2)), acc(1), acc(1), acc(1), acc(HALO)],
        out_shape=[jax.ShapeDtypeStruct(dz.shape, dz.dtype), jax.ShapeDtypeStruct((1, CW), F32),
                   jax.ShapeDtypeStruct((1, CW), F32), jax.ShapeDtypeStruct((1, CW), F32),
                   jax.ShapeDtypeStruct((HALO, CW), F32)],
        scratch_shapes=[pltpu.VMEM((ts + HALO, CW), F32), pltpu.VMEM((HALO + ts, CW), F32)]
        + [pltpu.VMEM((SUBL - 1, ts + HALO - SUBL, CW), F32)] * 2 + [pltpu.VMEM((KC, SUBL, CW), F32)],
        input_output_aliases={11: 0},
        compiler_params=_params(("arbitrary",)),
    )(ds, yc, ds, yc, z, z, z, z, w32, lg, lb, dz)


SSM_TS = 1024
GRP = 8


def _cmul(ar, ai, br, bi):
    return ar * br - ai * bi, ar * bi + ai * br


def _scan_tables(ar, ai, reverse):
    n = ar.shape[1]
    row = lax.broadcasted_iota(jnp.int32, (GRP, n), 0)
    dist = (GRP - 1 - row) if reverse else row
    one_r = jnp.broadcast_to(ar, (GRP, n))
    one_i = jnp.broadcast_to(ai, (GRP, n))
    p2r, p2i = _cmul(one_r, one_i, one_r, one_i)
    p4r, p4i = _cmul(p2r, p2i, p2r, p2i)
    steps = []
    for sft, (pr, pi) in ((1, (one_r, one_i)), (2, (p2r, p2i)), (4, (p4r, p4i))):
        keep = dist >= sft
        steps.append((jnp.where(keep, pr, 0.0), jnp.where(keep, pi, 0.0)))
    cr, ci = one_r, one_i
    accr, acci = one_r, one_i
    for e in range(1, GRP):
        cr, ci = _cmul(cr, ci, one_r, one_i)
        accr = jnp.where(dist == e, cr, accr)
        acci = jnp.where(dist == e, ci, acci)
    return steps, (accr, acci)


def _scan_group(xr, xi, steps, carry_tab, cr, ci, reverse):
    for sft, (tr, ti) in zip((1, 2, 4), steps):
        amt = (GRP - sft) if reverse else sft
        sr = pltpu.roll(xr, amt, 0)
        si = pltpu.roll(xi, amt, 0)
        xr, xi = xr + tr * sr - ti * si, xi + tr * si + ti * sr
    pr, pi = carry_tab
    xr = xr + pr * cr - pi * ci
    xi = xi + pr * ci + pi * cr
    return xr, xi


def ssm_fwd(z, wb_re, wb_im, wc, e_re, e_im, dvec):
    s = z.shape[0]
    ts = SSM_TS
    nt = s // ts
    ucol0 = ZB_U * CW // CB

    def body(u_ref, wbr_ref, wbi_ref, wc_ref, er_ref, ei_ref, d_ref, xr_ref, xi_ref, y_ref, gl_ref, car_r, car_i):
        i = pl.program_id(1)

        @pl.when(i == 0)
        def _():
            car_r[...] = jnp.zeros_like(car_r)
            car_i[...] = jnp.zeros_like(car_i)

        u = u_ref[...]
        ub = u.astype(BF16)
        xr_ref[...] = jnp.dot(ub, wbr_ref[0], preferred_element_type=F32)
        xi_ref[...] = jnp.dot(ub, wbi_ref[0], preferred_element_type=F32)
        steps, ctab = _scan_tables(er_ref[0], ei_ref[0], False)

        def grp(r, carry):
            cr, ci = carry
            r0 = pl.multiple_of(r * GRP, GRP)
            xr, xi = _scan_group(xr_ref[pl.ds(r0, GRP), :], xi_ref[pl.ds(r0, GRP), :], steps, ctab, cr, ci, False)
            xr_ref[pl.ds(r0, GRP), :] = xr
            xi_ref[pl.ds(r0, GRP), :] = xi
            return (jnp.broadcast_to(xr[GRP - 1:GRP, :], (GRP, SB)), jnp.broadcast_to(xi[GRP - 1:GRP, :], (GRP, SB)))

        cr, ci = lax.fori_loop(0, ts // GRP, grp, (car_r[...], car_i[...]))
        car_r[...] = cr
        car_i[...] = ci
        y = (jnp.dot(xr_ref[...].astype(BF16), wc_ref[0, 0:SB, :], preferred_element_type=F32)
             + jnp.dot(xi_ref[...].astype(BF16), wc_ref[0, SB:2 * SB, :], preferred_element_type=F32)
             + d_ref[0] * u)
        y_ref[...] = y
        gl_ref[...] = _gelu(y)[0].astype(gl_ref.dtype)

    blk3 = lambda a: pl.BlockSpec((1,) + a.shape[1:], lambda j, i: (j, 0, 0))
    return pl.pallas_call(
        body, name="ssm_fwd", grid=(NBLK, nt),
        in_specs=[pl.BlockSpec((ts, CB), lambda j, i: (i, ucol0 + j)),
                  blk3(wb_re), blk3(wb_im), blk3(wc), blk3(e_re), blk3(e_im), blk3(dvec)],
        out_specs=[pl.BlockSpec((ts, SB), lambda j, i: (i, j)), pl.BlockSpec((ts, SB), lambda j, i: (i, j)),
                   pl.BlockSpec((ts, CB), lambda j, i: (i, j)), pl.BlockSpec((ts, CB), lambda j, i: (i, j))],
        out_shape=[jax.ShapeDtypeStruct((s, NST), F32), jax.ShapeDtypeStruct((s, NST), F32),
                   jax.ShapeDtypeStruct((s, SW), F32), jax.ShapeDtypeStruct((s, SW), BF16)],
        scratch_shapes=[pltpu.VMEM((GRP, SB), F32), pltpu.VMEM((GRP, SB), F32)],
        compiler_params=_params(("parallel", "arbitrary")),
    )(z, wb_re, wb_im, wc, e_re, e_im, dvec)


def ssm_bwd(dgl, ypre, z, xs_re, xs_im, wbt_re, wbt_im, wct, e_re, e_im, dvec, dz):
    s = z.shape[0]
    ts = SSM_TS
    nt = s // ts
    ucol0 = ZB_U * CW // CB
    tn_dims = (((0,), (0,)), ((), ()))

    def body(dgl_ref, y_ref, u_ref, xr_ref, xi_ref, wbtr_ref, wbti_ref, wct_ref, er_ref, ei_ref, d_ref, dz_in,
             du_ref, dd_ref, dar_ref, dai_ref, dwbr_ref, dwbi_ref, dwc_ref,
             lr_ref, li_ref, car_r, car_i, acc_r, acc_i):
        i = pl.program_id(1)

        @pl.when(i == 0)
        def _():
            for ref in (car_r, car_i, acc_r, acc_i, dd_ref, dwbr_ref, dwbi_ref, dwc_ref):
                ref[...] = jnp.zeros_like(ref)

        u = u_ref[...]
        y = y_ref[...]
        dy = dgl_ref[...] * _gelu_grad(y, _gelu(y)[1])
        dd_ref[0] += _colsum(dy * u)
        dyb = dy.astype(BF16)
        dxo = jnp.dot(dyb, wct_ref[0], preferred_element_type=F32)
        lr_ref[...] = dxo[:, 0:SB]
        li_ref[...] = dxo[:, SB:2 * SB]
        steps, ctab = _scan_tables(er_ref[0], -ei_ref[0], True)
        row = lax.broadcasted_iota(jnp.int32, (GRP, SB), 0)

        def grp(q, carry):
            cr, ci, ar, ai = carry
            r0 = pl.multiple_of((ts // GRP - 1 - q) * GRP, GRP)
            lr, li = _scan_group(lr_ref[pl.ds(r0, GRP), :], li_ref[pl.ds(r0, GRP), :], steps, ctab, cr, ci, True)
            lr_ref[pl.ds(r0, GRP), :] = lr
            li_ref[pl.ds(r0, GRP), :] = li
            nr = jnp.where(row == GRP - 1, cr, pltpu.roll(lr, GRP - 1, 0))
            ni = jnp.where(row == GRP - 1, ci, pltpu.roll(li, GRP - 1, 0))
            xr = xr_ref[pl.ds(r0, GRP), :]
            xi = xi_ref[pl.ds(r0, GRP), :]
            ar = ar + nr * xr + ni * xi
            ai = ai + ni * xr - nr * xi
            return (jnp.broadcast_to(lr[0:1, :], (GRP, SB)), jnp.broadcast_to(li[0:1, :], (GRP, SB)), ar, ai)

        cr, ci, ar, ai = lax.fori_loop(0, ts // GRP, grp, (car_r[...], car_i[...], acc_r[...], acc_i[...]))
        car_r[...] = cr
        car_i[...] = ci
        acc_r[...] = ar
        acc_i[...] = ai

        @pl.when(i == nt - 1)
        def _():
            dar_ref[0] = _colsum(ar)
            dai_ref[0] = _colsum(ai)

        lrb = lr_ref[...].astype(BF16)
        lib = li_ref[...].astype(BF16)
        du = (jnp.dot(lrb, wbtr_ref[0], preferred_element_type=F32)
              + jnp.dot(lib, wbti_ref[0], preferred_element_type=F32) + d_ref[0] * dy)
        du_ref[...] = du.astype(du_ref.dtype)
        ub = u.astype(BF16)
        dwbr_ref[0] += lax.dot_general(ub, lrb, tn_dims, preferred_element_type=F32)
        dwbi_ref[0] += lax.dot_general(ub, lib, tn_dims, preferred_element_type=F32)
        dwc_ref[0, 0:SB, :] += lax.dot_general(xr_ref[...].astype(BF16), dyb, tn_dims, preferred_element_type=F32)
        dwc_ref[0, SB:2 * SB, :] += lax.dot_general(xi_ref[...].astype(BF16), dyb, tn_dims, preferred_element_type=F32)

    rev = lambda i: nt - 1 - i
    blk3 = lambda a: pl.BlockSpec((1,) + a.shape[1:], lambda j, i: (j, 0, 0))
    acc3 = lambda r, c: pl.BlockSpec((1, r, c), lambda j, i: (j, 0, 0))
    return pl.pallas_call(
        body, name="ssm_bwd", grid=(NBLK, nt),
        in_specs=[pl.BlockSpec((ts, CB), lambda j, i: (rev(i), j)), pl.BlockSpec((ts, CB), lambda j, i: (rev(i), j)),
                  pl.BlockSpec((ts, CB), lambda j, i: (rev(i), ucol0 + j)),
                  pl.BlockSpec((ts, SB), lambda j, i: (rev(i), j)), pl.BlockSpec((ts, SB), lambda j, i: (rev(i), j)),
                  blk3(wbt_re), blk3(wbt_im), blk3(wct), blk3(e_re), blk3(e_im), blk3(dvec),
                  pl.BlockSpec(memory_space=pl.ANY)],
        out_specs=[pl.BlockSpec((ts, CB), lambda j, i: (rev(i), ucol0 + j)),
                   acc3(1, CB), acc3(1, SB), acc3(1, SB), acc3(CB, SB), acc3(CB, SB), acc3(2 * SB, CB)],
        out_shape=[jax.ShapeDtypeStruct(dz.shape, dz.dtype),
                   jax.ShapeDtypeStruct((NBLK, 1, CB), F32),
                   jax.ShapeDtypeStruct((NBLK, 1, SB), F32), jax.ShapeDtypeStruct((NBLK, 1, SB), F32),
                   jax.ShapeDtypeStruct((NBLK, CB, SB), F32), jax.ShapeDtypeStruct((NBLK, CB, SB), F32),
                   jax.ShapeDtypeStruct((NBLK, 2 * SB, CB), F32)],
        scratch_shapes=[pltpu.VMEM((ts, SB), F32), pltpu.VMEM((ts, SB), F32)] + [pltpu.VMEM((GRP, SB), F32)] * 4,
        input_output_aliases={11: 0},
        compiler_params=_params(("parallel", "arbitrary")),
    )(dgl, ypre, z, xs_re, xs_im, wbt_re, wbt_im, wct, e_re, e_im, dvec, dz)


def _disc(a_re, a_im, log_dt, b_re, b_im, expand):
    dt = jnp.dot(expand, jnp.exp(log_dt), preferred_element_type=F32, precision=lax.Precision.HIGHEST)
    mag = jnp.exp(dt * a_re)
    e_re, e_im = mag * jnp.cos(dt * a_im), mag * jnp.sin(dt * a_im)
    n_re, n_im = e_re - 1.0, e_im
    den = a_re * a_re + a_im * a_im
    q_re = (n_re * a_re + n_im * a_im) / den
    q_im = (n_im * a_re - n_re * a_im) / den
    return e_re, e_im, q_re * b_re - q_im * b_im, q_re * b_im + q_im * b_re


def _whole(a):
    return pl.BlockSpec(a.shape, functools.partial(lambda n: (0,) * n, n=a.ndim))


def disc_fwd(a_re, a_im, log_dt, b_re, b_im, expand):
    def body(ar, ai, ld, br, bi, ex, er_o, ei_o, bbr_o, bbi_o):
        er, ei, bbr, bbi = _disc(ar[...], ai[...], ld[...], br[...], bi[...], ex[...])
        er_o[...] = er
        ei_o[...] = ei
        bbr_o[...] = bbr
        bbi_o[...] = bbi

    ins = (a_re, a_im, log_dt, b_re, b_im, expand)
    outs = [jax.ShapeDtypeStruct(a_re.shape, F32)] * 2 + [jax.ShapeDtypeStruct(b_re.shape, F32)] * 2
    return pl.pallas_call(body, name="disc_fwd", in_specs=[_whole(a) for a in ins],
                          out_specs=[_whole(o) for o in outs], out_shape=outs, compiler_params=_params())(*ins)


def disc_bwd(a_re, a_im, log_dt, b_re, b_im, expand, de_re, de_im, dbb_re, dbb_im):
    def body(ar, ai, ld, br, bi, ex, der, dei, dbr, dbi, o_ar, o_ai, o_ld, o_br, o_bi):
        exv = ex[...]
        _, vjp = jax.vjp(lambda *p: _disc(*p, exv), ar[...], ai[...], ld[...], br[...], bi[...])
        g = vjp((der[...], dei[...], dbr[...], dbi[...]))
        for o, v in zip((o_ar, o_ai, o_ld, o_br, o_bi), g):
            o[...] = v

    ins = (a_re, a_im, log_dt, b_re, b_im, expand, de_re, de_im, dbb_re, dbb_im)
    outs = [jax.ShapeDtypeStruct(a.shape, F32) for a in (a_re, a_im, log_dt, b_re, b_im)]
    return pl.pallas_call(body, name="disc_bwd", in_specs=[_whole(a) for a in ins],
                          out_specs=[_whole(o) for o in outs], out_shape=outs, compiler_params=_params())(*ins)


def mod_fwd(c_all, w_ada, b_cols):
    def body(c_ref, w_ref, b_ref, act_ref, mod_ref):
        cv = c_ref[...]
        act = cv * _sig(cv)
        act_ref[...] = act
        mod_ref[...] = jnp.dot(act, w_ref[...], preferred_element_type=F32, precision=lax.Precision.HIGHEST) + b_ref[...]

    ins = (c_all, w_ada, b_cols)
    outs = [jax.ShapeDtypeStruct(c_all.shape, F32), jax.ShapeDtypeStruct((NDEV, w_ada.shape[1]), F32)]
    return pl.pallas_call(body, name="mod_fwd", in_specs=[_whole(a) for a in ins],
                          out_specs=[_whole(o) for o in outs], out_shape=outs, compiler_params=_params())(*ins)


def ada_grad(act_all, dmod_cols):
    def body(a_ref, d_ref, o_ref):
        o_ref[...] = lax.dot_general(a_ref[...], d_ref[...], (((0,), (0,)), ((), ())),
                                     preferred_element_type=F32, precision=lax.Precision.HIGHEST)

    out = jax.ShapeDtypeStruct((act_all.shape[1], dmod_cols.shape[1]), F32)
    return pl.pallas_call(body, name="ada_grad", in_specs=[_whole(act_all), _whole(dmod_cols)],
                          out_specs=_whole(out), out_shape=out, compiler_params=_params())(act_all, dmod_cols)


def _adam_math(w, g, m, v):
    m2 = ADAM_B1 * m + (1.0 - ADAM_B1) * g
    v2 = ADAM_B2 * v + (1.0 - ADAM_B2) * (g * g)
    m_hat = m2 / (1.0 - ADAM_B1 ** ADAM_STEP)
    v_hat = v2 / (1.0 - ADAM_B2 ** ADAM_STEP)
    delta = -ADAM_LR * (m_hat / (jnp.sqrt(v_hat) + ADAM_EPS) + ADAM_WD * w)
    return delta, m2, v2


def adam(name, w, g, m, v):
    r, c = w.shape
    tr = max(t for t in range(8, min(r, 512) + 1, 8) if r % t == 0)

    def body(w_ref, g_ref, m_ref, v_ref, d_o, m_o, v_o):
        d, m2, v2 = _adam_math(w_ref[...], g_ref[...], m_ref[...], v_ref[...])
        d_o[...] = d
        m_o[...] = m2
        v_o[...] = v2

    spec = pl.BlockSpec((tr, c), lambda i: (i, 0))
    out = jax.ShapeDtypeStruct((r, c), F32)
    return pl.pallas_call(body, name=name, grid=(r // tr,), in_specs=[spec] * 4, out_specs=[spec] * 3,
                          out_shape=[out] * 3, compiler_params=_params(("parallel",)))(w, g, m, v)


def adam_many(name, ws, gs, ms, vs):
    n = len(ws)

    def body(*refs):
        ins, outs = refs[:4 * n], refs[4 * n:]
        for q in range(n):
            d, m2, v2 = _adam_math(ins[q][...], ins[n + q][...], ins[2 * n + q][...], ins[3 * n + q][...])
            outs[q][...] = d
            outs[n + q][...] = m2
            outs[2 * n + q][...] = v2

    operands = list(ws) + list(gs) + list(ms) + list(vs)
    outs = [jax.ShapeDtypeStruct(w.shape, F32) for w in ws] * 3
    return pl.pallas_call(body, name=name, in_specs=[_whole(a) for a in operands],
                          out_specs=[_whole(o) for o in outs], out_shape=outs, compiler_params=_params())(*operands)


def _rows_tile(r, most):
    best = None
    for t in range(16, min(r, most) + 1, 16):
        if r % t == 0:
            best = t
    assert best is not None, r
    return best


def sum_slots(name, slots, out_dtype=F32):
    n, r, c = slots.shape
    tr = _rows_tile(r, max(16, (2 * 1024 * 1024) // (n * c)))

    def body(s_ref, o_ref):
        acc = s_ref[0].astype(F32)
        for q in range(1, n):
            acc = acc + s_ref[q].astype(F32)
        o_ref[...] = acc.astype(o_ref.dtype)

    return pl.pallas_call(body, name=name, grid=(r // tr,),
                          in_specs=[pl.BlockSpec((n, tr, c), lambda i: (0, i, 0))],
                          out_specs=pl.BlockSpec((tr, c), lambda i: (i, 0)),
                          out_shape=jax.ShapeDtypeStruct((r, c), out_dtype), compiler_params=_params(("parallel",)))(slots)


HBM_SPEC = pl.BlockSpec(memory_space=pltpu.HBM)


def _coords():
    return lax.axis_index("x"), lax.axis_index("y"), lax.axis_index("c")


def _linear(x, y, c):
    return 4 * x + 2 * y + c


def all_gather(name, shards, deps=()):
    nq, nd = len(shards), len(deps)

    def body(*refs):
        xs, outs = refs[:nq], refs[nq + nd:2 * nq + nd]
        send_sems, recv_sems, local_sems = refs[2 * nq + nd:2 * nq + nd + 3]
        bufs = refs[2 * nq + nd + 3:]
        x, y, cc = _coords()
        me, sibling = (x, y, cc), (x, y, 1 - cc)
        chips = [(1 - x, y), (x, 1 - y), (1 - x, 1 - y)]

        def slot(q, px, py, pc):
            return outs[q].at[_linear(px, py, pc)]

        def copy(q, k, block, to, src=None):
            return pltpu.make_async_remote_copy(
                src_ref=slot(q, *block) if src is None else src, dst_ref=slot(q, *block),
                send_sem=send_sems.at[7 * q + k], recv_sem=recv_sems.at[7 * q + k], device_id=to, device_id_type=MESH)

        loads = [pltpu.make_async_copy(xs[q], bufs[q], local_sems.at[q]) for q in range(nq)]
        for cp in loads:
            cp.start()
        for cp in loads:
            cp.wait()
        mine = [pltpu.make_async_copy(bufs[q], slot(q, *me), local_sems.at[q]) for q in range(nq)]
        first = []
        for q in range(nq):
            first.append(copy(q, 0, me, sibling, src=bufs[q]))
            first += [copy(q, 1 + j, me, (*chip, cc), src=bufs[q]) for j, chip in enumerate(chips)]
        for cp in mine + first:
            cp.start()
        passed = []
        for q in range(nq):
            for j, chip in enumerate(chips):
                copy(q, 1 + j, (*chip, cc), me).wait_recv()
                passed.append(copy(q, 4 + j, (*chip, cc), sibling))
                passed[-1].start()
        for q in range(nq):
            copy(q, 0, sibling, me).wait_recv()
            for j, chip in enumerate(chips):
                copy(q, 4 + j, (*chip, 1 - cc), me).wait_recv()
        for cp in first + passed:
            cp.wait_send()
        for cp in mine:
            cp.wait()

    return pl.pallas_call(
        body, name=name, in_specs=[HBM_SPEC] * nq + [ANY_SPEC] * nd, out_specs=[HBM_SPEC] * nq,
        out_shape=[jax.ShapeDtypeStruct((NDEV,) + s.shape, s.dtype) for s in shards],
        scratch_shapes=[pltpu.SemaphoreType.DMA((7 * nq,)), pltpu.SemaphoreType.DMA((7 * nq,)),
                        pltpu.SemaphoreType.DMA((nq,))] + [pltpu.VMEM(s.shape, s.dtype) for s in shards],
    )(*shards, *deps)


NCHIP = 4


SEM_SPEC = pl.BlockSpec(memory_space=pltpu.SEMAPHORE)
EFFECT = pltpu.SideEffectType.DATAFLOW_SIDE_EFFECTING


def _peer(x, y, cc, k):
    fx, fy, fc = (k >> 2) & 1, (k >> 1) & 1, k & 1
    return (x + fx - 2 * fx * x, y + fy - 2 * fy * y, cc + fc - 2 * fc * cc)


def gather_plan(srcs, lands, coords):
    x, y, cc = coords
    me = _linear(x, y, cc)
    return [(s, l.at[me], _peer(x, y, cc, k)) for s, l in zip(srcs, lands) for k in range(1, NDEV)]


def near_plan(srcs, lands, coords):
    x, y, cc = coords
    me = _linear(x, y, cc)
    peers = [(x, y, 1 - cc)] + [_peer(x, y, cc, 2 * k) for k in range(1, NCHIP)]
    return [(s, l.at[me], p) for s, l in zip(srcs, lands) for p in peers]


def pass_on_plan(srcs, lands, coords):
    x, y, cc = coords
    out = []
    for l in srcs:
        for k in range(1, NCHIP):
            px, py, _ = _peer(x, y, cc, 2 * k)
            slot = _linear(px, py, cc)
            out.append((l.at[slot], l.at[slot], (x, y, 1 - cc)))
    return out


def pair_plan(srcs, lands, coords):
    x, y, cc = coords
    return [(s.at[2 * chip + 1 - cc], l.at[chip], (x, y, 1 - cc)) for s, l in zip(srcs, lands) for chip in range(NCHIP)]


def chip_plan(srcs, lands, coords):
    x, y, cc = coords
    out = []
    for s, l in zip(srcs, lands):
        for k in range(1, NCHIP):
            px, py, _ = _peer(x, y, cc, 2 * k)
            out.append((s.at[2 * px + py], l.at[k - 1], (px, py, cc)))
    return out


def _remote(copy, i, send_sems, recv_sems):
    src, dst, dev = copy
    return pltpu.make_async_remote_copy(src_ref=src, dst_ref=dst, send_sem=send_sems.at[i], recv_sem=recv_sems.at[i],
                                        device_id=dev, device_id_type=MESH)


def exchange_start(name, plan, ncopy, srcs, land_shapes, deps=()):
    ns, nl, nd = len(srcs), len(land_shapes), len(deps)

    def body(*refs):
        s, l = refs[:ns], refs[ns:ns + nl]
        send_sems, recv_sems = refs[ns + nl + nd], refs[ns + nl + nd + 1]
        token = refs[-1]
        for i, cp in enumerate(plan(s, l, _coords())):
            _remote(cp, i, send_sems, recv_sems).start()
        token[...] = jnp.zeros_like(token)

    hbm = lambda a: pltpu.with_memory_space_constraint(a, pltpu.HBM)
    lands = [lax.empty(shp, dt) for shp, dt in land_shapes]
    thru = [pltpu.HBM(a.shape, a.dtype) for a in list(srcs) + lands]
    outs = pl.pallas_call(
        body, name=name,
        in_specs=[HBM_SPEC] * (ns + nl) + [ANY_SPEC] * nd,
        out_specs=(SEM_SPEC, SEM_SPEC, *[HBM_SPEC] * (ns + nl), pl.BlockSpec(memory_space=pltpu.VMEM)),
        out_shape=(pltpu.SemaphoreType.DMA((ncopy,)), pltpu.SemaphoreType.DMA((ncopy,)), *thru,
                   jax.ShapeDtypeStruct((8, LANE), F32)),
        input_output_aliases={i: 2 + i for i in range(ns + nl)},
        compiler_params=pltpu.CompilerParams(has_side_effects=EFFECT),
    )(*[hbm(a) for a in srcs], *[hbm(a) for a in lands], *deps)
    return outs[0], outs[1], list(outs[2:2 + ns]), list(outs[2 + ns:2 + ns + nl]), outs[-1]


def exchange_wait(name, plan, started, after, place_own=False):
    send_sems, recv_sems, srcs, lands, _ = started
    ns, nl = len(srcs), len(lands)

    def body(*refs):
        s, l = refs[:ns], refs[ns:ns + nl]
        send_sems, recv_sems = refs[ns + nl], refs[ns + nl + 1]
        l_out = refs[2 * ns + nl + 3:2 * ns + 2 * nl + 3]
        scratch = refs[2 * ns + 2 * nl + 3:]
        copies = [_remote(cp, i, send_sems, recv_sems) for i, cp in enumerate(plan(s, l, _coords()))]
        if place_own:
            me = _linear(*_coords())
            local_sems, bufs = scratch[0], scratch[1:]
            loads = [pltpu.make_async_copy(s[q], bufs[q], local_sems.at[q]) for q in range(ns)]
            for cp in loads:
                cp.start()
            for cp in loads:
                cp.wait()
            stores = [pltpu.make_async_copy(bufs[q], l_out[q].at[me], local_sems.at[q]) for q in range(ns)]
            for cp in stores:
                cp.start()
        for cp in copies:
            cp.wait_recv()
        for cp in copies:
            cp.wait_send()
        if place_own:
            for cp in stores:
                cp.wait()

    scratch_shapes = []
    if place_own:
        scratch_shapes = [pltpu.SemaphoreType.DMA((ns,))] + [pltpu.VMEM(a.shape, a.dtype) for a in srcs]
    outs = pl.pallas_call(
        body, name=name,
        in_specs=[HBM_SPEC] * (ns + nl) + [SEM_SPEC, SEM_SPEC, ANY_SPEC],
        out_specs=[HBM_SPEC] * (ns + nl),
        out_shape=[pltpu.HBM(a.shape, a.dtype) for a in srcs + lands],
        input_output_aliases={i: i for i in range(ns + nl)},
        scratch_shapes=scratch_shapes,
        compiler_params=pltpu.CompilerParams(has_side_effects=EFFECT),
    )(*srcs, *lands, send_sems, recv_sems, after)
    return list(outs[:ns]), list(outs[ns:])


def pair_sum(name, g, recv):
    _, r, c = g.shape
    tr = _rows_tile(r, 512)

    def body(g_ref, r_ref, o_ref):
        own = jnp.where(lax.axis_index("c") == 0, g_ref[0, 0], g_ref[0, 1])
        o_ref[0] = (own.astype(F32) + r_ref[0].astype(F32)).astype(o_ref.dtype)

    return pl.pallas_call(
        body, name=name, grid=(NCHIP, r // tr),
        in_specs=[pl.BlockSpec((1, 2, tr, c), lambda k, i: (k, 0, i, 0)), pl.BlockSpec((1, tr, c), lambda k, i: (k, i, 0))],
        out_specs=pl.BlockSpec((1, tr, c), lambda k, i: (k, i, 0)),
        out_shape=jax.ShapeDtypeStruct((NCHIP, r, c), g.dtype), compiler_params=_params(("parallel", "parallel")),
    )(g.reshape(NCHIP, 2, r, c), recv)


def chip_sum_adam(name, partial, recv, w, m, v):
    _, r, c = partial.shape
    tr = _rows_tile(r, 512)

    def body(p_ref, r_ref, w_ref, m_ref, v_ref, g_o, d_o, m_o, v_o):
        chip = 2 * lax.axis_index("x") + lax.axis_index("y")
        own = p_ref[0]
        for k in range(1, NCHIP):
            own = jnp.where(chip == k, p_ref[k], own)
        g = own.astype(F32)
        for k in range(NCHIP - 1):
            g = g + r_ref[k].astype(F32)
        d, m2, v2 = _adam_math(w_ref[...], g, m_ref[...], v_ref[...])
        g_o[...] = g
        d_o[...] = d
        m_o[...] = m2
        v_o[...] = v2

    spec = pl.BlockSpec((tr, c), lambda i: (i, 0))
    out = jax.ShapeDtypeStruct((r, c), F32)
    return pl.pallas_call(
        body, name=name, grid=(r // tr,),
        in_specs=[pl.BlockSpec((NCHIP, tr, c), lambda i: (0, i, 0)), pl.BlockSpec((NCHIP - 1, tr, c), lambda i: (0, i, 0)),
                  spec, spec, spec],
        out_specs=[spec] * 4, out_shape=[out] * 4, compiler_params=_params(("parallel",)),
    )(partial, recv, w, m, v)


def _block_diag(w, rows_per, cols_per):
    w = w.reshape(NBLK, 8, rows_per, cols_per)
    eye = jnp.eye(8, dtype=w.dtype)
    out = w[:, :, :, None, :] * eye[None, :, None, :, None]
    return out.reshape(NBLK, 8 * rows_per, 8 * cols_per)


def _diag_blocks(wd, rows_per, cols_per):
    wd = wd.reshape(NBLK, 8, rows_per, 8, cols_per)
    idx = jnp.arange(8)
    return wd[:, idx, :, idx, :].transpose(1, 0, 2, 3).reshape(NG, rows_per, cols_per)


def _pad_rows(v, mult):
    n = v.shape[0]
    return jnp.pad(v, (0, (-n) % mult))


def kernel(x, c, w_ada, b_ada, norm1_g, w_in, conv_w, conv_b, conv_ln_g, conv_ln_b, conv_proj, ssm_a_re, ssm_a_im, ssm_b_re, ssm_b_im, ssm_c_re, ssm_c_im, ssm_d, ssm_log_dt, ssm_glu, w_out, norm2_g, w_ffn_in, w_ffn_out, final_g, loss_target, m_w_ada, m_b_ada, m_norm1_g, m_w_in, m_conv_w, m_conv_b, m_conv_ln_g, m_conv_ln_b, m_conv_proj, m_ssm_a_re, m_ssm_a_im, m_ssm_b_re, m_ssm_b_im, m_ssm_c_re, m_ssm_c_im, m_ssm_d, m_ssm_log_dt, m_ssm_glu, m_w_out, m_norm2_g, m_w_ffn_in, m_w_ffn_out, m_final_g, v_w_ada, v_b_ada, v_norm1_g, v_w_in, v_conv_w, v_conv_b, v_conv_ln_g, v_conv_ln_b, v_conv_proj, v_ssm_a_re, v_ssm_a_im, v_ssm_b_re, v_ssm_b_im, v_ssm_c_re, v_ssm_c_im, v_ssm_d, v_ssm_log_dt, v_ssm_glu, v_w_out, v_norm2_g, v_w_ffn_in, v_w_ffn_out, v_final_g):
    me = _linear(*_coords())
    xs = x[0]
    tgt = loss_target[0]
    seq = xs.shape[0]

    flat = lambda g: g.reshape(NDEV * g.shape[1], g.shape[2])
    w_in_s = w_in[0].T.astype(BF16)
    mids = [p.astype(BF16) for p in (conv_proj[0].T, ssm_glu[0].T, w_out[0])]
    ffns = [p.astype(BF16) for p in (w_ffn_in[0].T, w_ffn_out[0])]
    zone = lambda p: ((NDEV,) + p.shape, p.dtype)
    c_all, cw_g = all_gather("gather_c_conv_w", [c, conv_w[0]])
    in_go = exchange_start("gather_in_start", near_plan, NCHIP, [w_in_s], [zone(w_in_s)], deps=[c_all])
    mids_go = exchange_start("gather_mid_start", gather_plan, 7 * len(mids), mids, [zone(p) for p in mids],
                             deps=[in_go[4]])
    ffns_go = exchange_start("gather_ffn_start", gather_plan, 7 * len(ffns), ffns, [zone(p) for p in ffns],
                             deps=[mids_go[4]])

    ncol = w_ada.shape[2]
    c_all = c_all.reshape(NDEV, D)
    b_cols = lax.dynamic_slice_in_dim(b_ada, me * ncol, ncol, axis=1)
    act_all, mod_cols = mod_fwd(c_all, w_ada[0], b_cols)
    (mod_all,) = all_gather("gather_mod", [mod_cols])
    mod = lax.dynamic_index_in_dim(mod_all, me, axis=1, keepdims=False).reshape(NMOD, D)
    sh1, sc1, g1, sh2, sc2, g2 = [mod[q:q + 1] for q in range(NMOD)]

    expand = jnp.repeat(jnp.eye(NG, dtype=F32), NP, axis=0)
    a_re_c, a_im_c = ssm_a_re.reshape(NST, 1), ssm_a_im.reshape(NST, 1)
    ldt_c = ssm_log_dt.reshape(NG, 1)
    b_re_r, b_im_r = ssm_b_re.reshape(NST, GH), ssm_b_im.reshape(NST, GH)
    e_re, e_im, bb_re, bb_im = disc_fwd(a_re_c, a_im_c, ldt_c, b_re_r, b_im_r, expand)
    e_re_b, e_im_b = e_re.reshape(NBLK, 1, SB), e_im.reshape(NBLK, 1, SB)
    bb_re_g, bb_im_g = bb_re.reshape(NG, NP, GH), bb_im.reshape(NG, NP, GH)
    wbt_re = _block_diag(bb_re_g, NP, GH)
    wbt_im = _block_diag(bb_im_g, NP, GH)
    wb_re, wb_im = wbt_re.transpose(0, 2, 1), wbt_im.transpose(0, 2, 1)
    wct = jnp.concatenate([_block_diag(ssm_c_re[0], GH, NP), -_block_diag(ssm_c_im[0], GH, NP)], axis=2)
    wc = wct.transpose(0, 2, 1)
    to_b = lambda a: a.astype(BF16)
    dvec = ssm_d.reshape(NBLK, 1, CB)

    n1g = norm1_g

    def f_norm1(xv, g, sc, sh):
        _, xh = _rms_stats(xv)
        return [xh * g * (1.0 + sc) + sh], []

    (h1,) = rowwise("norm1", f_norm1, [xs], [n1g, sc1, sh1], [(D, BF16)], [], 512, deps=[ffns_go[4]])
    _, (w_in_land,) = exchange_wait("gather_in_wait", near_plan, in_go, h1, place_own=True)
    pass_go = exchange_start("gather_in_pass_start", pass_on_plan, NCHIP - 1, [w_in_land], [])
    (w_in_g,), _ = exchange_wait("gather_in_pass_wait", pass_on_plan, pass_go, pass_go[4])
    w_in_t = flat(w_in_g)
    z = mm("mm_in", h1, w_in_t, "nt", tiles=(2048, CW, 1024), b_rot=Z_ROT)

    conv_w_full = cw_g.transpose(1, 0, 2).reshape(KC, CW)
    w32 = jnp.pad(conv_w_full, ((0, HALO - KC), (0, 0)))
    yc, s_act = conv_fwd(z, w32, conv_b, conv_ln_g, conv_ln_b)
    conv_proj_t, ssm_glu_t, w_out_f = [
        flat(g) for g in exchange_wait("gather_mid_wait", gather_plan, mids_go, s_act, place_own=True)[1]]
    y_conv = mm("mm_conv_proj", s_act, conv_proj_t, "nt")

    xs_re, xs_im, ypre, gl = ssm_fwd(z, to_b(wb_re), to_b(wb_im), to_b(wc), e_re_b, e_im_b, dvec)
    n_mrg = D // MRG_BLK

    pair_of = lambda t, n: t // 2 + (t % 2) * n

    def ep_merge(accs, yc_v, gates):
        za, zb = accs
        glc, gls = gates[:, 0:MRG_BLK], gates[:, MRG_BLK:2 * MRG_BLK]
        return [_sig(glc) * yc_v + _sig(gls) * (za * _sig(zb)), jnp.concatenate([za, zb], axis=1)]

    merged, z2_pair = mm_ep("mm_ssm_glu", gl, ssm_glu_t, 2, lambda j, q: j + q * n_mrg, ep_merge,
                            [(y_conv, 1, 0), (z, 2, 0)], [(D, BF16, 1), (2 * D, BF16, 2)], (512, MRG_BLK, SW))
    row_tiles = lambda bk: (512, D, bk)
    whole = lambda j, q: j

    def ep_norm2(accs, xv, g1v, g, sc, sh):
        (o1v,) = accs
        x1v = xv + g1v * o1v
        _, xh = _rms_stats(x1v)
        return [x1v, xh * g * (1.0 + sc) + sh, o1v]

    x1, h2, o1 = mm_ep("mm_out", merged, w_out_f, 1, whole, ep_norm2, [(xs, 1, 0)],
                       [(D, F32, 1), (D, BF16, 1), (D, BF16, 1)], row_tiles(D), b_kn=True,
                       consts=[g1, norm2_g, sc2, sh2])
    w_ffn_in_t, w_ffn_out_f = [
        flat(g) for g in exchange_wait("gather_ffn_wait", gather_plan, ffns_go, h2, place_own=True)[1]]
    ffn_tiles = (512, FFN_BLK, 1024)
    n_ffn_blk = FH // FFN_BLK
    pair_map = lambda t: t // 2 + (t % 2) * n_ffn_blk

    def ep_swiglu(accs):
        fg, fu = accs
        return [fg * _sig(fg) * fu, jnp.concatenate([fg, fu], axis=1)]

    act, f_pair = mm_ep("mm_ffn_in", h2, w_ffn_in_t, 2, lambda j, q: j + q * n_ffn_blk, ep_swiglu, [],
                        [(FH, BF16, 1), (2 * FH, BF16, 2)], ffn_tiles)
    fg_row = final_g.reshape(1, D)

    def ep_final(accs, x1v, tv, g2v, fg):
        (o2v,) = accs
        x2v = x1v + g2v * o2v
        r, xh = _rms_stats(x2v)
        yv = xh * fg
        err = yv - tv
        loss = jnp.sum(_colsum(err * err), axis=1, keepdims=True) * (0.5 / D)
        dy = err * (1.0 / D)
        dx2 = _rms_bwd(dy * fg, xh, r)
        return ([dx2, g2v * dx2],
                [jnp.broadcast_to(loss, (1, LANE)), _colsum(dy * xh), _colsum(dx2 * o2v)])

    dx2, do2, loss_l, d_final_g, d_g2 = mm_ep_pipe(
        "mm_ffn_out", act, w_ffn_out_f, 1, whole, ep_final, [(x1, 1, 0), (tgt, 1, 0)],
        [(D, F32, 1), (D, BF16, 1)], row_tiles(FH), b_kn=True, consts=[g2, fg_row], sums=[LANE, D, D])

    g_ffn_out = mm("mm_g_ffn_out", act, do2, "tn", BF16, tiles=(FFN_BLK, 1024, 1024))

    def ep_dswiglu(accs, fp):
        (da,) = accs
        fg, fu = fp[:, 0:FFN_BLK].astype(F32), fp[:, FFN_BLK:2 * FFN_BLK].astype(F32)
        sg = _sig(fg)
        return [jnp.concatenate([da * fu * (sg * (1.0 + fg * (1.0 - sg))), da * (fg * sg)], axis=1)]

    (df,) = mm_ep("mm_dact", do2, w_ffn_out_f, 1, lambda j, q: j, ep_dswiglu, [(f_pair, 2, 0)],
                  [(2 * FH, BF16, 2)], ffn_tiles)
    g_ffn_in_t = mm("mm_g_ffn_in", df, h2, "tn", BF16, tiles=(FFN_BLK, 1024, 1024), o_rot=pair_map)

    def pair_go(tag, grads_t, deps=()):
        srcs = [g.reshape(NDEV, -1, D) for g in grads_t]
        return exchange_start("pair_" + tag + "_start", pair_plan, NCHIP * len(srcs), srcs,
                              [((NCHIP,) + s.shape[1:], s.dtype) for s in srcs], deps)

    def chip_go(tag, names, pair_started, after):
        own, from_sibling = exchange_wait("pair_" + tag + "_wait", pair_plan, pair_started, after)
        partials = [pair_sum("pair_sum_" + n, g, r) for n, g, r in zip(names, own, from_sibling)]
        return exchange_start("chip_" + tag + "_start", chip_plan, (NCHIP - 1) * len(partials), partials,
                              [((NCHIP - 1,) + p.shape[1:], p.dtype) for p in partials])

    def chip_done(tag, chip_started, after):
        partials, from_chips = exchange_wait("chip_" + tag + "_wait", chip_plan, chip_started, after)
        return list(zip(partials, from_chips))

    pair_ffn = pair_go("ffn", [g_ffn_out, g_ffn_in_t])

    dh2 = mm("mm_dh2", df, w_ffn_in_t, "nn", BF16, tiles=(1024, 1024, FFN_BLK), b_rot=pair_map,
             deps=[pair_ffn[4]])

    def f_dnorm2(dh, x1v, dx2v, o1v, g, sc, g1v):
        dh, o1v = dh.astype(F32), o1v.astype(F32)
        r, xh = _rms_stats(x1v)
        dxh = dh * (1.0 + sc) * g
        dx1 = dx2v + _rms_bwd(dxh, xh, r)
        return ([dx1, g1v * dx1],
                [_colsum(dh * xh * g), _colsum(dh), _colsum(dh * (1.0 + sc) * xh), _colsum(dx1 * o1v)])

    dx1, do1, d_sc2, d_sh2, d_n2g, d_g1 = rowwise(
        "dnorm2", f_dnorm2, [dh2, x1, dx2, o1], [norm2_g, sc2, g1], [(D, F32), (D, BF16)], [D, D, D, D], 512)

    g_out = mm("mm_g_out", merged, do1, "tn", BF16)
    chip_ffn = chip_go("ffn", ("w_ffn_out", "w_ffn_in"), pair_ffn, g_out)

    def ep_dmerge(accs, yc_v, z2p, gates):
        (dm,) = accs
        za, zb = z2p[:, 0:MRG_BLK].astype(F32), z2p[:, MRG_BLK:2 * MRG_BLK].astype(F32)
        sc_, ss_, sb_ = _sig(gates[:, 0:MRG_BLK]), _sig(gates[:, MRG_BLK:2 * MRG_BLK]), _sig(zb)
        dys = dm * ss_
        dz2 = jnp.concatenate([dys * sb_, dys * za * sb_ * (1.0 - sb_)], axis=1)
        dgates = jnp.concatenate([dm * yc_v * sc_ * (1.0 - sc_), dm * (za * sb_) * ss_ * (1.0 - ss_)], axis=1)
        return [dm * sc_, dz2, dgates]

    dyconv, dz2, dz = mm_ep("mm_dmerged", do1, w_out_f, 1, lambda j, q: j, ep_dmerge,
                            [(y_conv, 1, 0), (z2_pair, 2, 0), (z, 2, 0)],
                            [(D, BF16, 1), (2 * D, BF16, 2), (ZW, BF16, 2)], (512, MRG_BLK, 1024), deps=[chip_ffn[4]])

    g_conv_proj_t = mm("mm_g_conv_proj", dyconv, s_act, "tn", BF16)
    mrg_map = lambda t: pair_of(t, n_mrg)
    dgl = mm("mm_dgl", dz2, ssm_glu_t, "nn", tiles=(1024, SW, MRG_BLK), b_rot=mrg_map)
    g_ssm_glu_t = mm("mm_g_ssm_glu", dz2, gl, "tn", BF16, tiles=(MRG_BLK, SW, 1024), o_rot=mrg_map)
    pair_mid = pair_go("mid", [g_out, g_conv_proj_t, g_ssm_glu_t])
    ds = mm("mm_ds", dyconv, conv_proj_t, "nn", deps=[pair_mid[4]])
    dz, d_lng, d_lnb, d_cb, d_cw32 = conv_bwd(ds, yc, z, w32, conv_ln_g, conv_ln_b, dz)
    dz, d_d, d_ar, d_ai, d_wb_re, d_wb_im, d_wc = ssm_bwd(
        dgl, ypre, z, xs_re, xs_im, to_b(wbt_re), to_b(wbt_im), to_b(wct), e_re_b, e_im_b, dvec, dz)
    chip_mid = chip_go("mid", ("w_out", "conv_proj", "ssm_glu"), pair_mid, dz)

    d_bb_re = _diag_blocks(d_wb_re.transpose(0, 2, 1), NP, GH).reshape(NST, GH)
    d_bb_im = _diag_blocks(d_wb_im.transpose(0, 2, 1), NP, GH).reshape(NST, GH)
    d_wct = d_wc.transpose(0, 2, 1)
    d_c_re = _diag_blocks(d_wct[:, :, 0:SB], GH, NP)
    d_c_im = -_diag_blocks(d_wct[:, :, SB:2 * SB], GH, NP)
    d_a_re, d_a_im, d_ldt, d_b_re, d_b_im = disc_bwd(
        a_re_c, a_im_c, ldt_c, b_re_r, b_im_r, expand, d_ar.reshape(NST, 1), d_ai.reshape(NST, 1), d_bb_re, d_bb_im)

    small_local = [jnp.concatenate([d_g1, d_sh2, d_sc2, d_g2], axis=1).reshape(-1), d_cw32[0:KC].reshape(-1),
                   d_cb.reshape(-1), d_lng.reshape(-1), d_lnb.reshape(-1), d_a_re.reshape(-1), d_a_im.reshape(-1),
                   d_b_re.reshape(-1), d_b_im.reshape(-1), d_c_re.reshape(-1), d_c_im.reshape(-1), d_d.reshape(-1),
                   d_ldt.reshape(-1), d_n2g.reshape(-1), d_final_g.reshape(-1), loss_l[0, 0:1]]
    small_sizes = [v.shape[0] for v in small_local]
    small_pack = _pad_rows(jnp.concatenate(small_local), 256 * LANE).reshape(-1, LANE)
    small_go = exchange_start("gather_small_start", gather_plan, NDEV - 1, [small_pack],
                              [((NDEV,) + small_pack.shape, F32)], deps=[chip_mid[4]])

    g_in_t = mm("mm_g_in", dz, h1, "tn", BF16, tiles=(CW, 1024, 2048), o_rot=Z_ROT, deps=[small_go[4]])
    pair_in = pair_go("in", [g_in_t])

    dh1 = mm("mm_dh1", dz, w_in_t, "nn", BF16, tiles=(2048, 1024, CW), b_rot=Z_ROT, deps=[pair_in[4]])

    def f_dnorm1(dh, xv, dx1v, g, sc):
        dh = dh.astype(F32)
        r, xh = _rms_stats(xv)
        dxh = dh * (1.0 + sc) * g
        return ([dx1v + _rms_bwd(dxh, xh, r)],
                [_colsum(dh * xh * g), _colsum(dh), _colsum(dh * (1.0 + sc) * xh)])

    grad_x, d_sc1, d_sh1, d_n1g = rowwise(
        "dnorm1", f_dnorm1, [dh1, xs, dx1], [n1g, sc1], [(D, F32)], [D, D, D], 512)
    late_local = [d_sh1.reshape(-1), d_sc1.reshape(-1), d_n1g.reshape(-1)]
    late_pack = _pad_rows(jnp.concatenate(late_local), 16 * LANE).reshape(-1, LANE)
    late_go = exchange_start("gather_late_start", gather_plan, NDEV - 1, [late_pack],
                             [((NDEV,) + late_pack.shape, F32)])
    chip_in = chip_go("in", ("w_in",), pair_in, late_go[4])

    weights = {
        "w_ada": (w_ada, m_w_ada, v_w_ada), "b_ada": (b_ada, m_b_ada, v_b_ada), "norm1_g": (norm1_g, m_norm1_g, v_norm1_g),
        "w_in": (w_in, m_w_in, v_w_in), "conv_w": (conv_w, m_conv_w, v_conv_w), "conv_b": (conv_b, m_conv_b, v_conv_b),
        "conv_ln_g": (conv_ln_g, m_conv_ln_g, v_conv_ln_g), "conv_ln_b": (conv_ln_b, m_conv_ln_b, v_conv_ln_b),
        "conv_proj": (conv_proj, m_conv_proj, v_conv_proj), "ssm_a_re": (ssm_a_re, m_ssm_a_re, v_ssm_a_re),
        "ssm_a_im": (ssm_a_im, m_ssm_a_im, v_ssm_a_im), "ssm_b_re": (ssm_b_re, m_ssm_b_re, v_ssm_b_re),
        "ssm_b_im": (ssm_b_im, m_ssm_b_im, v_ssm_b_im), "ssm_c_re": (ssm_c_re, m_ssm_c_re, v_ssm_c_re),
        "ssm_c_im": (ssm_c_im, m_ssm_c_im, v_ssm_c_im), "ssm_d": (ssm_d, m_ssm_d, v_ssm_d),
        "ssm_log_dt": (ssm_log_dt, m_ssm_log_dt, v_ssm_log_dt), "ssm_glu": (ssm_glu, m_ssm_glu, v_ssm_glu),
        "w_out": (w_out, m_w_out, v_w_out), "norm2_g": (norm2_g, m_norm2_g, v_norm2_g),
        "w_ffn_in": (w_ffn_in, m_w_ffn_in, v_w_ffn_in), "w_ffn_out": (w_ffn_out, m_w_ffn_out, v_w_ffn_out),
        "final_g": (final_g, m_final_g, v_final_g),
    }
    order = list(weights)
    big = ("w_ada", "w_in", "conv_proj", "ssm_glu", "w_out", "w_ffn_in", "w_ffn_out")
    grads, delta, new_m, new_v = {}, {}, {}, {}

    def adam_big(n, g2d, transposed=False):
        wv, mv, vv = weights[n]
        shp = wv.shape
        t_in = (lambda a: a.reshape(shp[-2:]).T) if transposed else (lambda a: a.reshape(shp[-2:]))
        t_out = (lambda a: a.T.reshape(shp)) if transposed else (lambda a: a.reshape(shp))
        w2 = t_in(wv)
        if isinstance(g2d, tuple):
            partial, recv = [p.reshape((p.shape[0],) + w2.shape) for p in g2d]
            g2d, d_, m_, v_ = chip_sum_adam("adam_" + n, partial, recv, w2, t_in(mv), t_in(vv))
        else:
            d_, m_, v_ = adam("adam_" + n, w2, g2d, t_in(mv), t_in(vv))
        grads[n], delta[n], new_m[n], new_v[n] = t_out(g2d), t_out(d_), t_out(m_), t_out(v_)
        return d_

    parts_ffn_out, parts_ffn_in = chip_done("ffn", chip_ffn, chip_in[4])
    adam_big("w_ffn_out", parts_ffn_out)
    last = adam_big("w_ffn_in", parts_ffn_in, transposed=True)
    parts_out, parts_conv_proj, parts_ssm_glu = chip_done("mid", chip_mid, last)
    adam_big("w_out", parts_out)
    adam_big("conv_proj", parts_conv_proj, transposed=True)
    last_mid = adam_big("ssm_glu", parts_ssm_glu, transposed=True)

    _, (late_all,) = exchange_wait("gather_late_wait", gather_plan, late_go, last_mid, place_own=True)
    _, (small_all,) = exchange_wait("gather_small_wait", gather_plan, small_go, late_all, place_own=True)

    def unpack(vec, sizes):
        out, pos = [], 0
        for n in sizes:
            out.append(vec[pos:pos + n])
            pos += n
        return out

    g_sh1, g_sc1, g_n1g = unpack(sum_slots("sum_small_late", late_all).reshape(-1), [D, D, D])
    (g_mod_rest, g_cw_full, g_cb, g_lng, g_lnb, g_a_re, g_a_im, g_b_re, g_b_im, g_c_re, g_c_im, g_d, g_ldt,
     g_n2g, g_fg, loss_sum) = unpack(sum_slots("sum_small", small_all).reshape(-1), small_sizes)
    g_b_ada = jnp.concatenate([g_sh1, g_sc1, g_mod_rest])
    loss = loss_sum[0]
    dmod_all = jnp.concatenate([late_all.reshape(NDEV, -1)[:, 0:2 * D], small_all.reshape(NDEV, -1)[:, 0:4 * D]],
                               axis=1)
    g_w_ada = ada_grad(act_all, lax.dynamic_slice_in_dim(dmod_all, me * ncol, ncol, axis=1))
    ccol = conv_w.shape[2]
    g_conv_w = lax.dynamic_slice_in_dim(g_cw_full.reshape(KC, CW), me * ccol, ccol, axis=1)

    adam_big("w_ada", g_w_ada)
    grads.update({
        "b_ada": g_b_ada.reshape(b_ada.shape), "norm1_g": g_n1g.reshape(norm1_g.shape),
        "conv_w": g_conv_w[None], "conv_b": g_cb.reshape(conv_b.shape),
        "conv_ln_g": g_lng.reshape(conv_ln_g.shape), "conv_ln_b": g_lnb.reshape(conv_ln_b.shape),
        "ssm_a_re": g_a_re.reshape(ssm_a_re.shape),
        "ssm_a_im": g_a_im.reshape(ssm_a_im.shape), "ssm_b_re": g_b_re.reshape(ssm_b_re.shape),
        "ssm_b_im": g_b_im.reshape(ssm_b_im.shape), "ssm_c_re": g_c_re.reshape(ssm_c_re.shape),
        "ssm_c_im": g_c_im.reshape(ssm_c_im.shape), "ssm_d": g_d.reshape(ssm_d.shape),
        "ssm_log_dt": g_ldt.reshape(ssm_log_dt.shape),
        "norm2_g": g_n2g.reshape(norm2_g.shape),
        "final_g": g_fg.reshape(final_g.shape),
    })
    small = [n for n in order if n not in big]
    def rows(a):
        if a.ndim == 4 and a.shape[-1] < a.shape[-2]:
            a = a.swapaxes(-1, -2)
        return a.reshape(1, -1) if a.ndim == 1 else a.reshape(-1, a.shape[-1])

    def unrows(a, shp):
        if len(shp) == 4 and shp[-1] < shp[-2]:
            return a.reshape(shp[:-2] + (shp[-1], shp[-2])).swapaxes(-1, -2)
        return a.reshape(shp)

    small_out = adam_many("adam_small", [rows(weights[n][0]) for n in small], [rows(grads[n]) for n in small],
                          [rows(weights[n][1]) for n in small], [rows(weights[n][2]) for n in small])
    for q, n in enumerate(small):
        shp = weights[n][0].shape
        delta[n], new_m[n], new_v[n] = [unrows(small_out[t * len(small) + q], shp) for t in range(3)]

    (parts_in,) = chip_done("in", chip_in, small_out[0])
    adam_big("w_in", parts_in, transposed=True)

    return (loss, grad_x[None], *[grads[n] for n in order], *[delta[n] for n in order],
            *[new_m[n] for n in order], *[new_v[n] for n in order])
```

```python
import functools
import math

import jax
import jax.numpy as jnp
from jax import lax
from jax.experimental import pallas as pl
from jax.experimental.pallas import tpu as pltpu

F32 = jnp.float32
BF16 = jnp.bfloat16

D = 1024
CW = 512
KC = 31
SW = 512
NG = 32
GH = 16
NP = 64
NST = NG * NP
FH = 2816
FFN_BLK = 1408
MRG_BLK = 1024
NMOD = 6
NDEV = 8
EPS = 1e-6
CB = 128
SB = 512
NBLK = SW // CB
HALO = 32
ZW = 2 * CW + SW + 2 * D
Z_ROT = lambda j: (j + 3) % (ZW // CW)
ZB_A, ZB_G, ZB_U = 4, 5, 6

ADAM_LR = 0.001
ADAM_B1 = 0.9
ADAM_B2 = 0.999
ADAM_EPS = 1e-08
ADAM_WD = 0.01
ADAM_STEP = 10

V7X_VMEM_BYTES = 64 * 1024 * 1024
VMEM_LIMIT = V7X_VMEM_BYTES - 8 * 1024 * 1024
LANE = 128
MESH = pl.DeviceIdType.MESH
ANY_SPEC = pl.BlockSpec(memory_space=pl.ANY)


def _params(sem=None, **kw):
    if sem is not None:
        kw["dimension_semantics"] = sem
    return pltpu.CompilerParams(vmem_limit_bytes=VMEM_LIMIT, **kw)


def _tile(n, most):
    best = None
    for t in range(LANE, most + 1, LANE):
        if n % t == 0:
            best = t
    if best is None:
        raise ValueError(f"no tile for {n}")
    return best


def _sig(x):
    return jax.nn.sigmoid(x)


def mm(name, a, b, mode, out_dtype=F32, tiles=None, b_rot=None, o_rot=None, deps=()):
    if mode == "nn":
        (m, k), (k2, n) = a.shape, b.shape
    elif mode == "nt":
        (m, k), (n, k2) = a.shape, b.shape
    else:
        (k, m), (k2, n) = a.shape, b.shape
    assert k == k2, (name, a.shape, b.shape)
    bm, bn, bk = tiles or (_tile(m, 1024), _tile(n, 1408), _tile(k, 1408 if k % 1408 == 0 else 1024))
    bm, bn, bk = min(bm, m), min(bn, n), min(bk, k)
    assert m % bm == 0 and n % bn == 0 and k % bk == 0, (name, m, n, k, bm, bn, bk)
    nk = k // bk
    rot = lambda idx, r: idx if r is None else r(idx)
    if mode == "nn":
        a_spec = pl.BlockSpec((bm, bk), lambda i, j, kk: (i, kk))
        b_spec = pl.BlockSpec((bk, bn), lambda i, j, kk: (rot(kk, b_rot), j))
        dims = (((1,), (0,)), ((), ()))
    elif mode == "nt":
        a_spec = pl.BlockSpec((bm, bk), lambda i, j, kk: (i, kk))
        b_spec = pl.BlockSpec((bn, bk), lambda i, j, kk: (rot(j, b_rot), kk))
        dims = (((1,), (1,)), ((), ()))
    else:
        assert b_rot is None
        a_spec = pl.BlockSpec((bk, bm), lambda i, j, kk: (kk, i))
        b_spec = pl.BlockSpec((bk, bn), lambda i, j, kk: (kk, j))
        dims = (((0,), (0,)), ((), ()))

    def body(a_ref, b_ref, *rest):
        o_ref, acc_ref = rest[-2:]
        kk = pl.program_id(2)

        @pl.when(kk == 0)
        def _():
            acc_ref[...] = jnp.zeros_like(acc_ref)

        acc_ref[...] += lax.dot_general(a_ref[...], b_ref[...], dims, preferred_element_type=F32)

        @pl.when(kk == nk - 1)
        def _():
            o_ref[...] = acc_ref[...].astype(o_ref.dtype)

    return pl.pallas_call(
        body, name=name,
        grid=(m // bm, n // bn, nk),
        in_specs=[a_spec, b_spec] + [ANY_SPEC] * len(deps),
        out_specs=pl.BlockSpec((bm, bn), lambda i, j, kk: (rot(i, o_rot), j)),
        out_shape=jax.ShapeDtypeStruct((m, n), out_dtype),
        scratch_shapes=[pltpu.VMEM((bm, bn), F32)],
        compiler_params=_params(("parallel", "parallel", "arbitrary")),
    )(a, b, *deps)


def mm_ep(name, a, b, n_acc, acc_block, epilogue, extras, outs, tiles, deps=(), b_kn=False, consts=()):
    m, k = a.shape
    bm, bn, bk = tiles
    bm = min(bm, m)
    nj = outs[0][0] // (outs[0][2] * bn)
    assert m % bm == 0 and bk == k and b.shape[0 if b_kn else 1] == k, (name, a.shape, b.shape, tiles)
    ne, nc, no, nd = len(extras), len(consts), len(outs), len(deps)
    dims = (((1,), (0,)), ((), ())) if b_kn else (((1,), (1,)), ((), ()))

    def body(*refs):
        a_ref, b_refs = refs[0], refs[1:1 + n_acc]
        e_refs = refs[1 + n_acc:1 + n_acc + ne + nc]
        first_out = 1 + n_acc + ne + nc + nd
        o_refs = refs[first_out:first_out + no]
        av = a_ref[...]
        prods = [lax.dot_general(av, b_ref[...], dims, preferred_element_type=F32) for b_ref in b_refs]
        for o_ref, v in zip(o_refs, epilogue(prods, *[e[...] for e in e_refs])):
            o_ref[...] = v.astype(o_ref.dtype)

    in_specs = [pl.BlockSpec((bm, bk), lambda i, j: (i, 0))]
    if b_kn:
        in_specs += [pl.BlockSpec((bk, bn), functools.partial(lambda i, j, q: (0, acc_block(j, q)), q=q))
                     for q in range(n_acc)]
    else:
        in_specs += [pl.BlockSpec((bn, bk), functools.partial(lambda i, j, q: (acc_block(j, q), 0), q=q))
                     for q in range(n_acc)]
    in_specs += [pl.BlockSpec((bm, w * bn), functools.partial(lambda i, j, off: (i, j + off), off=off))
                 for (_, w, off) in extras]
    in_specs += [pl.BlockSpec((1, bn), lambda i, j: (0, j)) for _ in consts]
    in_specs += [ANY_SPEC] * nd
    return pl.pallas_call(
        body, name=name, grid=(m // bm, nj),
        in_specs=in_specs,
        out_specs=[pl.BlockSpec((bm, w * bn), lambda i, j: (i, j)) for (_, _, w) in outs],
        out_shape=[jax.ShapeDtypeStruct((m, cols), dt) for (cols, dt, _) in outs],
        compiler_params=_params(("parallel", "parallel")),
    )(a, *[b] * n_acc, *[e[0] for e in extras], *consts, *deps)


def mm_ep_pipe(name, a, b, n_acc, acc_block, epilogue, extras, outs, tiles, deps=(), b_kn=False, consts=(), sums=()):
    m, k = a.shape
    bm, bn, bk = tiles
    bm = min(bm, m)
    assert bk == k and m % bm == 0 and b.shape[0 if b_kn else 1] == k, (name, a.shape, b.shape, tiles)
    ni, nj = m // bm, outs[0][0] // (outs[0][2] * bn)
    nt = ni * nj
    assert not sums or nj == 1, name
    ne, nc, no, ns, nd = len(extras), len(consts), len(outs), len(sums), len(deps)
    dims = (((1,), (0,)), ((), ())) if b_kn else (((1,), (1,)), ((), ()))
    cur_i = lambda t: jnp.minimum(t, nt - 1) // nj
    cur_j = lambda t: jnp.minimum(t, nt - 1) % nj
    prev_i = lambda t: jnp.maximum(t - 1, 0) // nj
    prev_j = lambda t: jnp.maximum(t - 1, 0) % nj

    def body(*refs):
        a_ref, b_refs = refs[0], refs[1:1 + n_acc]
        e_refs = refs[1 + n_acc:1 + n_acc + ne + nc]
        first_out = 1 + n_acc + ne + nc + nd
        o_refs = refs[first_out:first_out + no]
        s_refs = refs[first_out + no:first_out + no + ns]
        acc_ref = refs[first_out + no + ns]
        t = pl.program_id(0)

        @pl.when(t == 0)
        def _():
            acc_ref[...] = jnp.zeros_like(acc_ref)
            for s_ref in s_refs:
                s_ref[...] = jnp.zeros_like(s_ref)

        slot = t % 2
        done = [acc_ref[(1 - slot) * n_acc + q] for q in range(n_acc)]
        av = a_ref[...]
        for q, b_ref in enumerate(b_refs):
            acc_ref[slot * n_acc + q] = lax.dot_general(av, b_ref[...], dims, preferred_element_type=F32)
        res = epilogue(done, *[e[...] for e in e_refs])
        tiles_out, sums_out = res if ns else (res, ())
        for o_ref, v in zip(o_refs, tiles_out):
            o_ref[...] = v.astype(o_ref.dtype)
        live = (t >= 1).astype(F32)
        for s_ref, v in zip(s_refs, sums_out):
            s_ref[...] += v * live

    in_specs = [pl.BlockSpec((bm, k), lambda t: (cur_i(t), 0))]
    if b_kn:
        in_specs += [pl.BlockSpec((k, bn), functools.partial(lambda t, q: (0, acc_block(cur_j(t), q)), q=q))
                     for q in range(n_acc)]
    else:
        in_specs += [pl.BlockSpec((bn, k), functools.partial(lambda t, q: (acc_block(cur_j(t), q), 0), q=q))
                     for q in range(n_acc)]
    in_specs += [pl.BlockSpec((bm, w * bn), functools.partial(lambda t, off: (prev_i(t), prev_j(t) + off), off=off))
                 for (_, w, off) in extras]
    in_specs += [pl.BlockSpec((1, bn), lambda t: (0, prev_j(t))) for _ in consts]
    in_specs += [ANY_SPEC] * nd
    out_specs = [pl.BlockSpec((bm, w * bn), lambda t: (prev_i(t), prev_j(t))) for (_, _, w) in outs]
    out_specs += [pl.BlockSpec((1, w), lambda t: (0, 0)) for w in sums]
    out_shape = [jax.ShapeDtypeStruct((m, cols), dt) for (cols, dt, _) in outs]
    out_shape += [jax.ShapeDtypeStruct((1, w), F32) for w in sums]
    return pl.pallas_call(
        body, name=name, grid=(nt + 1,),
        in_specs=in_specs, out_specs=out_specs, out_shape=out_shape,
        scratch_shapes=[pltpu.VMEM((2 * n_acc, bm, bn), F32)],
        compiler_params=_params(("arbitrary",)),
    )(a, *[b] * n_acc, *[e[0] for e in extras], *consts, *deps)


def rowwise(name, fn, rows, consts, out_rows, out_sums, ts, alias=None, deps=()):
    rows = [r if isinstance(r, tuple) else (r, r.shape[1], 0) for r in rows]
    out_rows = [o if len(o) == 4 else (o[0], o[1], o[0], 0) for o in out_rows]
    s = rows[0][0].shape[0]
    nt = s // ts
    nr, nc, no, ns = len(rows), len(consts), len(out_rows), len(out_sums)
    in_specs = [pl.BlockSpec((ts, w), functools.partial(lambda i, cb: (i, cb), cb=cb)) for (_, w, cb) in rows]
    in_specs += [pl.BlockSpec(c.shape, lambda i: (0, 0)) for c in consts]
    operands = [r[0] for r in rows] + list(consts)
    aliases = {}
    if alias is not None:
        in_specs.append(pl.BlockSpec(memory_space=pl.ANY))
        operands.append(alias[0])
        aliases = {nr + nc: alias[1]}
    in_specs += [ANY_SPEC] * len(deps)
    operands += list(deps)
    out_shape = [jax.ShapeDtypeStruct((s, tw), dt) for (_, dt, tw, _) in out_rows]
    out_shape += [jax.ShapeDtypeStruct((1, w), F32) for w in out_sums]
    out_specs = [pl.BlockSpec((ts, w), functools.partial(lambda i, cb: (i, cb), cb=cb)) for (w, _, _, cb) in out_rows]
    out_specs += [pl.BlockSpec((1, w), lambda i: (0, 0)) for w in out_sums]
    n_in = len(operands)

    def body(*refs):
        ins, outs = refs[:nr + nc], refs[n_in:]
        i = pl.program_id(0)
        ro, so = fn(*[r[...] for r in ins])
        for q in range(no):
            outs[q][...] = ro[q].astype(outs[q].dtype)
        if ns:
            @pl.when(i == 0)
            def _():
                for q in range(ns):
                    outs[no + q][...] = jnp.zeros_like(outs[no + q])

            for q in range(ns):
                outs[no + q][...] += so[q]

    return pl.pallas_call(
        body, name=name, grid=(nt,),
        in_specs=in_specs, out_specs=out_specs, out_shape=out_shape, input_output_aliases=aliases,
        compiler_params=_params(("arbitrary",) if ns else ("parallel",)),
    )(*operands)


def _colsum(v):
    return jnp.sum(v, axis=0, keepdims=True)


def _rms_stats(xv):
    r = lax.rsqrt(jnp.mean(xv * xv, axis=-1, keepdims=True) + EPS)
    return r, xv * r


def _rms_bwd(dxhat, xhat, r):
    return r * (dxhat - xhat * jnp.mean(dxhat * xhat, axis=-1, keepdims=True))


def _gelu(v):
    k = math.sqrt(2.0 / math.pi)
    t = jnp.tanh(k * (v + 0.044715 * v * v * v))
    return 0.5 * v * (1.0 + t), t


def _gelu_grad(v, t):
    k = math.sqrt(2.0 / math.pi)
    return 0.5 * (1.0 + t) + 0.5 * v * (1.0 - t * t) * k * (1.0 + 3.0 * 0.044715 * v * v)


CONV_TS = 512
CONV_CH = 64


def _ln_fwd(yc, g, b):
    mu = jnp.mean(yc, axis=-1, keepdims=True)
    xc = yc - mu
    rstd = lax.rsqrt(jnp.mean(xc * xc, axis=-1, keepdims=True) + EPS)
    nhat = xc * rstd
    return nhat, rstd, nhat * g + b


SUBL = 8


def _shifted_copies(buf, sh, ts):
    for b in range(1, SUBL):
        sh[b - 1] = buf[pl.ds(b, ts + HALO - SUBL), :]


def _shifted(buf, sh, start):
    b = start % SUBL
    if b == 0:
        return buf[pl.ds(start, CONV_CH), :]
    return sh[b - 1, pl.ds(start - b, CONV_CH), :]


def conv_fwd(z, w32, cb, lg, lb):
    s = z.shape[0]
    ts = CONV_TS
    nt = s // ts
    hb = ts // HALO

    def body(a_ref, g_ref, ah_ref, gh_ref, w_ref, cb_ref, lg_ref, lb_ref, yc_ref, s_ref, ubuf, ush):
        i = pl.program_id(0)
        first = (i > 0).astype(F32)
        ubuf[0:HALO, :] = ah_ref[...] * _sig(gh_ref[...]) * first
        ubuf[HALO:HALO + ts, :] = a_ref[...] * _sig(g_ref[...])
        _shifted_copies(ubuf, ush, ts)
        for c0 in range(0, ts, CONV_CH):
            acc = jnp.zeros((CONV_CH, CW), F32)
            for k in range(KC):
                acc = acc + w_ref[k:k + 1, :] * _shifted(ubuf, ush, c0 + k + 2)
            yc = acc + cb_ref[...]
            yc_ref[c0:c0 + CONV_CH, :] = yc
            _, _, ln = _ln_fwd(yc, lg_ref[...], lb_ref[...])
            s_ref[c0:c0 + CONV_CH, :] = (ln * _sig(ln)).astype(s_ref.dtype)

    cur = lambda cbk: pl.BlockSpec((ts, CW), functools.partial(lambda i, q: (i, q), q=cbk))
    prev = lambda cbk: pl.BlockSpec((HALO, CW), functools.partial(lambda i, q: (jnp.maximum(i * hb - 1, 0), q), q=cbk))
    const = lambda a: pl.BlockSpec(a.shape, lambda i: (0, 0))
    return pl.pallas_call(
        body, name="conv_fwd", grid=(nt,),
        in_specs=[cur(ZB_A), cur(ZB_G), prev(ZB_A), prev(ZB_G), const(w32), const(cb), const(lg), const(lb)],
        out_specs=[pl.BlockSpec((ts, CW), lambda i: (i, 0)), pl.BlockSpec((ts, CW), lambda i: (i, 0))],
        out_shape=[jax.ShapeDtypeStruct((s, CW), F32), jax.ShapeDtypeStruct((s, CW), BF16)],
        scratch_shapes=[pltpu.VMEM((HALO + ts, CW), F32), pltpu.VMEM((SUBL - 1, ts + HALO - SUBL, CW), F32)],
        compiler_params=_params(("parallel",)),
    )(z, z, z, z, w32, cb, lg, lb)


def conv_bwd(ds, yc, z, w32, lg, lb, dz):
    s = z.shape[0]
    ts = CONV_TS
    nt = s // ts
    hb = ts // HALO
    last_hb = s // HALO - 1

    def ln_bwd(dsv, ycv, g, b):
        nhat, rstd, ln = _ln_fwd(ycv, g, b)
        sg = _sig(ln)
        dln = dsv * (sg * (1.0 + ln * (1.0 - sg)))
        dnh = dln * g
        dyc = rstd * (dnh - jnp.mean(dnh, axis=-1, keepdims=True)
                      - nhat * jnp.mean(dnh * nhat, axis=-1, keepdims=True))
        return dyc, dln, nhat

    def body(ds_ref, yc_ref, dsn_ref, ycn_ref, a_ref, g_ref, ah_ref, gh_ref, w_ref, lg_ref, lb_ref, dz_in,
             dz_ref, dlg_ref, dlb_ref, dcb_ref, dw_ref, dbuf, ubuf, dsh, ush, dwacc):
        i = pl.program_id(0)

        @pl.when(i == 0)
        def _():
            dlg_ref[...] = jnp.zeros_like(dlg_ref)
            dlb_ref[...] = jnp.zeros_like(dlb_ref)
            dcb_ref[...] = jnp.zeros_like(dcb_ref)
            dw_ref[...] = jnp.zeros_like(dw_ref)
            dwacc[...] = jnp.zeros_like(dwacc)

        lg, lb = lg_ref[...], lb_ref[...]
        dyc, dln, nhat = ln_bwd(ds_ref[...], yc_ref[...], lg, lb)
        dlg_ref[...] += _colsum(dln * nhat)
        dlb_ref[...] += _colsum(dln)
        dcb_ref[...] += _colsum(dyc)
        dbuf[0:ts, :] = dyc
        nxt = (i < nt - 1).astype(F32)
        dbuf[ts:ts + HALO, :] = ln_bwd(dsn_ref[...], ycn_ref[...], lg, lb)[0] * nxt
        first = (i > 0).astype(F32)
        ubuf[0:HALO, :] = ah_ref[...] * _sig(gh_ref[...]) * first
        ubuf[HALO:HALO + ts, :] = a_ref[...] * _sig(g_ref[...])
        _shifted_copies(dbuf, dsh, ts)
        _shifted_copies(ubuf, ush, ts)
        for c0 in range(0, ts, CONV_CH):
            du = jnp.zeros((CONV_CH, CW), F32)
            dyc_c = dbuf[c0:c0 + CONV_CH, :]
            for k in range(KC):
                du = du + w_ref[k:k + 1, :] * _shifted(dbuf, dsh, c0 + KC - 1 - k)
                prod = dyc_c * _shifted(ubuf, ush, c0 + k + 2)
                dwacc[k] += jnp.sum(prod.reshape(CONV_CH // SUBL, SUBL, CW), axis=0)
            av = a_ref[c0:c0 + CONV_CH, :]
            sg = _sig(g_ref[c0:c0 + CONV_CH, :])
            dz_ref[c0:c0 + CONV_CH, 0:CW] = (du * sg).astype(dz_ref.dtype)
            dz_ref[c0:c0 + CONV_CH, CW:2 * CW] = (du * av * sg * (1.0 - sg)).astype(dz_ref.dtype)

        @pl.when(i == nt - 1)
        def _():
            for k in range(KC):
                dw_ref[k:k + 1, :] = _colsum(dwacc[k])

    cur = lambda w, cbk: pl.BlockSpec((ts, w), functools.partial(lambda i, q: (i, q), q=cbk))
    prev = lambda cbk: pl.BlockSpec((HALO, CW), functools.partial(lambda i, q: (jnp.maximum(i * hb - 1, 0), q), q=cbk))
    nxt_spec = pl.BlockSpec((HALO, CW), lambda i: (jnp.minimum((i + 1) * hb, last_hb), 0))
    const = lambda a: pl.BlockSpec(a.shape, lambda i: (0, 0))
    acc = lambda r: pl.BlockSpec((r, CW), lambda i: (0, 0))
    return pl.pallas_call(
        body, name="conv_bwd", grid=(nt,),
        in_specs=[cur(CW, 0), cur(CW, 0), nxt_spec, nxt_spec, cur(CW, ZB_A), cur(CW, ZB_G), prev(ZB_A), prev(ZB_G),
                  const(w32), const(lg), const(lb), pl.BlockSpec(memory_space=pl.ANY)],
        out_specs=[pl.BlockSpec((ts, 2 * CW), lambda i: (i, ZB_A // 2)), acc(1), acc(1), acc(1), acc(HALO)],
        out_shape=[jax.ShapeDtypeStruct(dz.shape, dz.dtype), jax.ShapeDtypeStruct((1, CW), F32),
                   jax.ShapeDtypeStruct((1, CW), F32), jax.ShapeDtypeStruct((1, CW), F32),
                   jax.ShapeDtypeStruct((HALO, CW), F32)],
        scratch_shapes=[pltpu.VMEM((ts + HALO, CW), F32), pltpu.VMEM((HALO + ts, CW), F32)]
        + [pltpu.VMEM((SUBL - 1, ts + HALO - SUBL, CW), F32)] * 2 + [pltpu.VMEM((KC, SUBL, CW), F32)],
        input_output_aliases={11: 0},
        compiler_params=_params(("arbitrary",)),
    )(ds, yc, ds, yc, z, z, z, z, w32, lg, lb, dz)


SSM_TS = 1024
GRP = 8


def _cmul(ar, ai, br, bi):
    return ar * br - ai * bi, ar * bi + ai * br


def _scan_tables(ar, ai, reverse):
    n = ar.shape[1]
    row = lax.broadcasted_iota(jnp.int32, (GRP, n), 0)
    dist = (GRP - 1 - row) if reverse else row
    one_r = jnp.broadcast_to(ar, (GRP, n))
    one_i = jnp.broadcast_to(ai, (GRP, n))
    p2r, p2i = _cmul(one_r, one_i, one_r, one_i)
    p4r, p4i = _cmul(p2r, p2i, p2r, p2i)
    steps = []
    for sft, (pr, pi) in ((1, (one_r, one_i)), (2, (p2r, p2i)), (4, (p4r, p4i))):
        keep = dist >= sft
        steps.append((jnp.where(keep, pr, 0.0), jnp.where(keep, pi, 0.0)))
    cr, ci = one_r, one_i
    accr, acci = one_r, one_i
    for e in range(1, GRP):
        cr, ci = _cmul(cr, ci, one_r, one_i)
        accr = jnp.where(dist == e, cr, accr)
        acci = jnp.where(dist == e, ci, acci)
    return steps, (accr, acci)


def _scan_group(xr, xi, steps, carry_tab, cr, ci, reverse):
    for sft, (tr, ti) in zip((1, 2, 4), steps):
        amt = (GRP - sft) if reverse else sft
        sr = pltpu.roll(xr, amt, 0)
        si = pltpu.roll(xi, amt, 0)
        xr, xi = xr + tr * sr - ti * si, xi + tr * si + ti * sr
    pr, pi = carry_tab
    xr = xr + pr * cr - pi * ci
    xi = xi + pr * ci + pi * cr
    return xr, xi


def ssm_fwd(z, wb_re, wb_im, wc, e_re, e_im, dvec):
    s = z.shape[0]
    ts = SSM_TS
    nt = s // ts
    ucol0 = ZB_U * CW // CB

    def body(u_ref, wbr_ref, wbi_ref, wc_ref, er_ref, ei_ref, d_ref, xr_ref, xi_ref, y_ref, gl_ref, car_r, car_i):
        i = pl.program_id(1)

        @pl.when(i == 0)
        def _():
            car_r[...] = jnp.zeros_like(car_r)
            car_i[...] = jnp.zeros_like(car_i)

        u = u_ref[...]
        ub = u.astype(BF16)
        xr_ref[...] = jnp.dot(ub, wbr_ref[0], preferred_element_type=F32)
        xi_ref[...] = jnp.dot(ub, wbi_ref[0], preferred_element_type=F32)
        steps, ctab = _scan_tables(er_ref[0], ei_ref[0], False)

        def grp(r, carry):
            cr, ci = carry
            r0 = pl.multiple_of(r * GRP, GRP)
            xr, xi = _scan_group(xr_ref[pl.ds(r0, GRP), :], xi_ref[pl.ds(r0, GRP), :], steps, ctab, cr, ci, False)
            xr_ref[pl.ds(r0, GRP), :] = xr
            xi_ref[pl.ds(r0, GRP), :] = xi
            return (jnp.broadcast_to(xr[GRP - 1:GRP, :], (GRP, SB)), jnp.broadcast_to(xi[GRP - 1:GRP, :], (GRP, SB)))

        cr, ci = lax.fori_loop(0, ts // GRP, grp, (car_r[...], car_i[...]))
        car_r[...] = cr
        car_i[...] = ci
        y = (jnp.dot(xr_ref[...].astype(BF16), wc_ref[0, 0:SB, :], preferred_element_type=F32)
             + jnp.dot(xi_ref[...].astype(BF16), wc_ref[0, SB:2 * SB, :], preferred_element_type=F32)
             + d_ref[0] * u)
        y_ref[...] = y
        gl_ref[...] = _gelu(y)[0].astype(gl_ref.dtype)

    blk3 = lambda a: pl.BlockSpec((1,) + a.shape[1:], lambda j, i: (j, 0, 0))
    return pl.pallas_call(
        body, name="ssm_fwd", grid=(NBLK, nt),
        in_specs=[pl.BlockSpec((ts, CB), lambda j, i: (i, ucol0 + j)),
                  blk3(wb_re), blk3(wb_im), blk3(wc), blk3(e_re), blk3(e_im), blk3(dvec)],
        out_specs=[pl.BlockSpec((ts, SB), lambda j, i: (i, j)), pl.BlockSpec((ts, SB), lambda j, i: (i, j)),
                   pl.BlockSpec((ts, CB), lambda j, i: (i, j)), pl.BlockSpec((ts, CB), lambda j, i: (i, j))],
        out_shape=[jax.ShapeDtypeStruct((s, NST), F32), jax.ShapeDtypeStruct((s, NST), F32),
                   jax.ShapeDtypeStruct((s, SW), F32), jax.ShapeDtypeStruct((s, SW), BF16)],
        scratch_shapes=[pltpu.VMEM((GRP, SB), F32), pltpu.VMEM((GRP, SB), F32)],
        compiler_params=_params(("parallel", "arbitrary")),
    )(z, wb_re, wb_im, wc, e_re, e_im, dvec)


def ssm_bwd(dgl, ypre, z, xs_re, xs_im, wbt_re, wbt_im, wct, e_re, e_im, dvec, dz):
    s = z.shape[0]
    ts = SSM_TS
    nt = s // ts
    ucol0 = ZB_U * CW // CB
    tn_dims = (((0,), (0,)), ((), ()))

    def body(dgl_ref, y_ref, u_ref, xr_ref, xi_ref, wbtr_ref, wbti_ref, wct_ref, er_ref, ei_ref, d_ref, dz_in,
             du_ref, dd_ref, dar_ref, dai_ref, dwbr_ref, dwbi_ref, dwc_ref,
             lr_ref, li_ref, car_r, car_i, acc_r, acc_i):
        i = pl.program_id(1)

        @pl.when(i == 0)
        def _():
            for ref in (car_r, car_i, acc_r, acc_i, dd_ref, dwbr_ref, dwbi_ref, dwc_ref):
                ref[...] = jnp.zeros_like(ref)

        u = u_ref[...]
        y = y_ref[...]
        dy = dgl_ref[...] * _gelu_grad(y, _gelu(y)[1])
        dd_ref[0] += _colsum(dy * u)
        dyb = dy.astype(BF16)
        dxo = jnp.dot(dyb, wct_ref[0], preferred_element_type=F32)
        lr_ref[...] = dxo[:, 0:SB]
        li_ref[...] = dxo[:, SB:2 * SB]
        steps, ctab = _scan_tables(er_ref[0], -ei_ref[0], True)
        row = lax.broadcasted_iota(jnp.int32, (GRP, SB), 0)

        def grp(q, carry):
            cr, ci, ar, ai = carry
            r0 = pl.multiple_of((ts // GRP - 1 - q) * GRP, GRP)
            lr, li = _scan_group(lr_ref[pl.ds(r0, GRP), :], li_ref[pl.ds(r0, GRP), :], steps, ctab, cr, ci, True)
            lr_ref[pl.ds(r0, GRP), :] = lr
            li_ref[pl.ds(r0, GRP), :] = li
            nr = jnp.where(row == GRP - 1, cr, pltpu.roll(lr, GRP - 1, 0))
            ni = jnp.where(row == GRP - 1, ci, pltpu.roll(li, GRP - 1, 0))
            xr = xr_ref[pl.ds(r0, GRP), :]
            xi = xi_ref[pl.ds(r0, GRP), :]
            ar = ar + nr * xr + ni * xi
            ai = ai + ni * xr - nr * xi
            return (jnp.broadcast_to(lr[0:1, :], (GRP, SB)), jnp.broadcast_to(li[0:1, :], (GRP, SB)), ar, ai)

        cr, ci, ar, ai = lax.fori_loop(0, ts // GRP, grp, (car_r[...], car_i[...], acc_r[...], acc_i[...]))
        car_r[...] = cr
        car_i[...] = ci
        acc_r[...] = ar
        acc_i[...] = ai

        @pl.when(i == nt - 1)
        def _():
            dar_ref[0] = _colsum(ar)
            dai_ref[0] = _colsum(ai)

        lrb = lr_ref[...].astype(BF16)
        lib = li_ref[...].astype(BF16)
        du = (jnp.dot(lrb, wbtr_ref[0], preferred_element_type=F32)
              + jnp.dot(lib, wbti_ref[0], preferred_element_type=F32) + d_ref[0] * dy)
        du_ref[...] = du.astype(du_ref.dtype)
        ub = u.astype(BF16)
        dwbr_ref[0] += lax.dot_general(ub, lrb, tn_dims, preferred_element_type=F32)
        dwbi_ref[0] += lax.dot_general(ub, lib, tn_dims, preferred_element_type=F32)
        dwc_ref[0, 0:SB, :] += lax.dot_general(xr_ref[...].astype(BF16), dyb, tn_dims, preferred_element_type=F32)
        dwc_ref[0, SB:2 * SB, :] += lax.dot_general(xi_ref[...].astype(BF16), dyb, tn_dims, preferred_element_type=F32)

    rev = lambda i: nt - 1 - i
    blk3 = lambda a: pl.BlockSpec((1,) + a.shape[1:], lambda j, i: (j, 0, 0))
    acc3 = lambda r, c: pl.BlockSpec((1, r, c), lambda j, i: (j, 0, 0))
    return pl.pallas_call(
        body, name="ssm_bwd", grid=(NBLK, nt),
        in_specs=[pl.BlockSpec((ts, CB), lambda j, i: (rev(i), j)), pl.BlockSpec((ts, CB), lambda j, i: (rev(i), j)),
                  pl.BlockSpec((ts, CB), lambda j, i: (rev(i), ucol0 + j)),
                  pl.BlockSpec((ts, SB), lambda j, i: (rev(i), j)), pl.BlockSpec((ts, SB), lambda j, i: (rev(i), j)),
                  blk3(wbt_re), blk3(wbt_im), blk3(wct), blk3(e_re), blk3(e_im), blk3(dvec),
                  pl.BlockSpec(memory_space=pl.ANY)],
        out_specs=[pl.BlockSpec((ts, CB), lambda j, i: (rev(i), ucol0 + j)),
                   acc3(1, CB), acc3(1, SB), acc3(1, SB), acc3(CB, SB), acc3(CB, SB), acc3(2 * SB, CB)],
        out_shape=[jax.ShapeDtypeStruct(dz.shape, dz.dtype),
                   jax.ShapeDtypeStruct((NBLK, 1, CB), F32),
                   jax.ShapeDtypeStruct((NBLK, 1, SB), F32), jax.ShapeDtypeStruct((NBLK, 1, SB), F32),
                   jax.ShapeDtypeStruct((NBLK, CB, SB), F32), jax.ShapeDtypeStruct((NBLK, CB, SB), F32),
                   jax.ShapeDtypeStruct((NBLK, 2 * SB, CB), F32)],
        scratch_shapes=[pltpu.VMEM((ts, SB), F32), pltpu.VMEM((ts, SB), F32)] + [pltpu.VMEM((GRP, SB), F32)] * 4,
        input_output_aliases={11: 0},
        compiler_params=_params(("parallel", "arbitrary")),
    )(dgl, ypre, z, xs_re, xs_im, wbt_re, wbt_im, wct, e_re, e_im, dvec, dz)


def _disc(a_re, a_im, log_dt, b_re, b_im, expand):
    dt = jnp.dot(expand, jnp.exp(log_dt), preferred_element_type=F32, precision=lax.Precision.HIGHEST)
    mag = jnp.exp(dt * a_re)
    e_re, e_im = mag * jnp.cos(dt * a_im), mag * jnp.sin(dt * a_im)
    n_re, n_im = e_re - 1.0, e_im
    den = a_re * a_re + a_im * a_im
    q_re = (n_re * a_re + n_im * a_im) / den
    q_im = (n_im * a_re - n_re * a_im) / den
    return e_re, e_im, q_re * b_re - q_im * b_im, q_re * b_im + q_im * b_re


def _whole(a):
    return pl.BlockSpec(a.shape, functools.partial(lambda n: (0,) * n, n=a.ndim))


def disc_fwd(a_re, a_im, log_dt, b_re, b_im, expand):
    def body(ar, ai, ld, br, bi, ex, er_o, ei_o, bbr_o, bbi_o):
        er, ei, bbr, bbi = _disc(ar[...], ai[...], ld[...], br[...], bi[...], ex[...])
        er_o[...] = er
        ei_o[...] = ei
        bbr_o[...] = bbr
        bbi_o[...] = bbi

    ins = (a_re, a_im, log_dt, b_re, b_im, expand)
    outs = [jax.ShapeDtypeStruct(a_re.shape, F32)] * 2 + [jax.ShapeDtypeStruct(b_re.shape, F32)] * 2
    return pl.pallas_call(body, name="disc_fwd", in_specs=[_whole(a) for a in ins],
                          out_specs=[_whole(o) for o in outs], out_shape=outs, compiler_params=_params())(*ins)


def disc_bwd(a_re, a_im, log_dt, b_re, b_im, expand, de_re, de_im, dbb_re, dbb_im):
    def body(ar, ai, ld, br, bi, ex, der, dei, dbr, dbi, o_ar, o_ai, o_ld, o_br, o_bi):
        exv = ex[...]
        _, vjp = jax.vjp(lambda *p: _disc(*p, exv), ar[...], ai[...], ld[...], br[...], bi[...])
        g = vjp((der[...], dei[...], dbr[...], dbi[...]))
        for o, v in zip((o_ar, o_ai, o_ld, o_br, o_bi), g):
            o[...] = v

    ins = (a_re, a_im, log_dt, b_re, b_im, expand, de_re, de_im, dbb_re, dbb_im)
    outs = [jax.ShapeDtypeStruct(a.shape, F32) for a in (a_re, a_im, log_dt, b_re, b_im)]
    return pl.pallas_call(body, name="disc_bwd", in_specs=[_whole(a) for a in ins],
                          out_specs=[_whole(o) for o in outs], out_shape=outs, compiler_params=_params())(*ins)


def mod_fwd(c_all, w_ada, b_cols):
    def body(c_ref, w_ref, b_ref, act_ref, mod_ref):
        cv = c_ref[...]
        act = cv * _sig(cv)
        act_ref[...] = act
        mod_ref[...] = jnp.dot(act, w_ref[...], preferred_element_type=F32, precision=lax.Precision.HIGHEST) + b_ref[...]

    ins = (c_all, w_ada, b_cols)
    outs = [jax.ShapeDtypeStruct(c_all.shape, F32), jax.ShapeDtypeStruct((NDEV, w_ada.shape[1]), F32)]
    return pl.pallas_call(body, name="mod_fwd", in_specs=[_whole(a) for a in ins],
                          out_specs=[_whole(o) for o in outs], out_shape=outs, compiler_params=_params())(*ins)


def ada_grad(act_all, dmod_cols):
    def body(a_ref, d_ref, o_ref):
        o_ref[...] = lax.dot_general(a_ref[...], d_ref[...], (((0,), (0,)), ((), ())),
                                     preferred_element_type=F32, precision=lax.Precision.HIGHEST)

    out = jax.ShapeDtypeStruct((act_all.shape[1], dmod_cols.shape[1]), F32)
    return pl.pallas_call(body, name="ada_grad", in_specs=[_whole(act_all), _whole(dmod_cols)],
                          out_specs=_whole(out), out_shape=out, compiler_params=_params())(act_all, dmod_cols)


def _adam_math(w, g, m, v):
    m2 = ADAM_B1 * m + (1.0 - ADAM_B1) * g
    v2 = ADAM_B2 * v + (1.0 - ADAM_B2) * (g * g)
    m_hat = m2 / (1.0 - ADAM_B1 ** ADAM_STEP)
    v_hat = v2 / (1.0 - ADAM_B2 ** ADAM_STEP)
    delta = -ADAM_LR * (m_hat / (jnp.sqrt(v_hat) + ADAM_EPS) + ADAM_WD * w)
    return delta, m2, v2


def adam(name, w, g, m, v):
    r, c = w.shape
    tr = max(t for t in range(8, min(r, 512) + 1, 8) if r % t == 0)

    def body(w_ref, g_ref, m_ref, v_ref, d_o, m_o, v_o):
        d, m2, v2 = _adam_math(w_ref[...], g_ref[...], m_ref[...], v_ref[...])
        d_o[...] = d
        m_o[...] = m2
        v_o[...] = v2

    spec = pl.BlockSpec((tr, c), lambda i: (i, 0))
    out = jax.ShapeDtypeStruct((r, c), F32)
    return pl.pallas_call(body, name=name, grid=(r // tr,), in_specs=[spec] * 4, out_specs=[spec] * 3,
                          out_shape=[out] * 3, compiler_params=_params(("parallel",)))(w, g, m, v)


def adam_many(name, ws, gs, ms, vs):
    n = len(ws)

    def body(*refs):
        ins, outs = refs[:4 * n], refs[4 * n:]
        for q in range(n):
            d, m2, v2 = _adam_math(ins[q][...], ins[n + q][...], ins[2 * n + q][...], ins[3 * n + q][...])
            outs[q][...] = d
            outs[n + q][...] = m2
            outs[2 * n + q][...] = v2

    operands = list(ws) + list(gs) + list(ms) + list(vs)
    outs = [jax.ShapeDtypeStruct(w.shape, F32) for w in ws] * 3
    return pl.pallas_call(body, name=name, in_specs=[_whole(a) for a in operands],
                          out_specs=[_whole(o) for o in outs], out_shape=outs, compiler_params=_params())(*operands)


def _rows_tile(r, most):
    best = None
    for t in range(16, min(r, most) + 1, 16):
        if r % t == 0:
            best = t
    assert best is not None, r
    return best


def sum_slots(name, slots, out_dtype=F32):
    n, r, c = slots.shape
    tr = _rows_tile(r, max(16, (2 * 1024 * 1024) // (n * c)))

    def body(s_ref, o_ref):
        acc = s_ref[0].astype(F32)
        for q in range(1, n):
            acc = acc + s_ref[q].astype(F32)
        o_ref[...] = acc.astype(o_ref.dtype)

    return pl.pallas_call(body, name=name, grid=(r // tr,),
                          in_specs=[pl.BlockSpec((n, tr, c), lambda i: (0, i, 0))],
                          out_specs=pl.BlockSpec((tr, c), lambda i: (i, 0)),
                          out_shape=jax.ShapeDtypeStruct((r, c), out_dtype), compiler_params=_params(("parallel",)))(slots)


HBM_SPEC = pl.BlockSpec(memory_space=pltpu.HBM)


def _coords():
    return lax.axis_index("x"), lax.axis_index("y"), lax.axis_index("c")


def _linear(x, y, c):
    return 4 * x + 2 * y + c


def all_gather(name, shards):
    nq = len(shards)

    def body(*refs):
        xs, outs = refs[:nq], refs[nq:2 * nq]
        send_sems, recv_sems, local_sems = refs[2 * nq:2 * nq + 3]
        bufs = refs[2 * nq + 3:]
        x, y, cc = _coords()
        me, sibling = (x, y, cc), (x, y, 1 - cc)
        chips = [(1 - x, y), (x, 1 - y), (1 - x, 1 - y)]

        def slot(q, px, py, pc):
            return outs[q].at[_linear(px, py, pc)]

        def copy(q, k, block, to, src=None):
            return pltpu.make_async_remote_copy(
                src_ref=slot(q, *block) if src is None else src, dst_ref=slot(q, *block),
                send_sem=send_sems.at[7 * q + k], recv_sem=recv_sems.at[7 * q + k], device_id=to, device_id_type=MESH)

        loads = [pltpu.make_async_copy(xs[q], bufs[q], local_sems.at[q]) for q in range(nq)]
        for cp in loads:
            cp.start()
        for cp in loads:
            cp.wait()
        mine = [pltpu.make_async_copy(bufs[q], slot(q, *me), local_sems.at[q]) for q in range(nq)]
        first = []
        for q in range(nq):
            first.append(copy(q, 0, me, sibling, src=bufs[q]))
            first += [copy(q, 1 + j, me, (*chip, cc), src=bufs[q]) for j, chip in enumerate(chips)]
        for cp in mine + first:
            cp.start()
        passed = []
        for q in range(nq):
            for j, chip in enumerate(chips):
                copy(q, 1 + j, (*chip, cc), me).wait_recv()
                passed.append(copy(q, 4 + j, (*chip, cc), sibling))
                passed[-1].start()
        for q in range(nq):
            copy(q, 0, sibling, me).wait_recv()
            for j, chip in enumerate(chips):
                copy(q, 4 + j, (*chip, 1 - cc), me).wait_recv()
        for cp in first + passed:
            cp.wait_send()
        for cp in mine:
            cp.wait()

    return pl.pallas_call(
        body, name=name, in_specs=[HBM_SPEC] * nq, out_specs=[HBM_SPEC] * nq,
        out_shape=[jax.ShapeDtypeStruct((NDEV,) + s.shape, s.dtype) for s in shards],
        scratch_shapes=[pltpu.SemaphoreType.DMA((7 * nq,)), pltpu.SemaphoreType.DMA((7 * nq,)),
                        pltpu.SemaphoreType.DMA((nq,))] + [pltpu.VMEM(s.shape, s.dtype) for s in shards],
    )(*shards)


NCHIP = 4


SEM_SPEC = pl.BlockSpec(memory_space=pltpu.SEMAPHORE)
EFFECT = pltpu.SideEffectType.DATAFLOW_SIDE_EFFECTING


def _peer(x, y, cc, k):
    fx, fy, fc = (k >> 2) & 1, (k >> 1) & 1, k & 1
    return (x + fx - 2 * fx * x, y + fy - 2 * fy * y, cc + fc - 2 * fc * cc)


def gather_plan(srcs, lands, coords):
    x, y, cc = coords
    me = _linear(x, y, cc)
    return [(s, l.at[me], _peer(x, y, cc, k)) for s, l in zip(srcs, lands) for k in range(1, NDEV)]


def near_plan(srcs, lands, coords):
    x, y, cc = coords
    me = _linear(x, y, cc)
    peers = [(x, y, 1 - cc)] + [_peer(x, y, cc, 2 * k) for k in range(1, NCHIP)]
    return [(s, l.at[me], p) for s, l in zip(srcs, lands) for p in peers]


def pass_on_plan(srcs, lands, coords):
    x, y, cc = coords
    out = []
    for l in srcs:
        for k in range(1, NCHIP):
            px, py, _ = _peer(x, y, cc, 2 * k)
            slot = _linear(px, py, cc)
            out.append((l.at[slot], l.at[slot], (x, y, 1 - cc)))
    return out


def pair_plan(srcs, lands, coords):
    x, y, cc = coords
    return [(s.at[2 * chip + 1 - cc], l.at[chip], (x, y, 1 - cc)) for s, l in zip(srcs, lands) for chip in range(NCHIP)]


def chip_plan(srcs, lands, coords):
    x, y, cc = coords
    out = []
    for s, l in zip(srcs, lands):
        for k in range(1, NCHIP):
            px, py, _ = _peer(x, y, cc, 2 * k)
            out.append((s.at[2 * px + py], l.at[k - 1], (px, py, cc)))
    return out


def _remote(copy, i, send_sems, recv_sems):
    src, dst, dev = copy
    return pltpu.make_async_remote_copy(src_ref=src, dst_ref=dst, send_sem=send_sems.at[i], recv_sem=recv_sems.at[i],
                                        device_id=dev, device_id_type=MESH)


def exchange_start(name, plan, ncopy, srcs, land_shapes, deps=()):
    ns, nl, nd = len(srcs), len(land_shapes), len(deps)

    def body(*refs):
        s, l = refs[:ns], refs[ns:ns + nl]
        send_sems, recv_sems = refs[ns + nl + nd], refs[ns + nl + nd + 1]
        token = refs[-1]
        for i, cp in enumerate(plan(s, l, _coords())):
            _remote(cp, i, send_sems, recv_sems).start()
        token[...] = jnp.zeros_like(token)

    hbm = lambda a: pltpu.with_memory_space_constraint(a, pltpu.HBM)
    lands = [lax.empty(shp, dt) for shp, dt in land_shapes]
    thru = [pltpu.HBM(a.shape, a.dtype) for a in list(srcs) + lands]
    outs = pl.pallas_call(
        body, name=name,
        in_specs=[HBM_SPEC] * (ns + nl) + [ANY_SPEC] * nd,
        out_specs=(SEM_SPEC, SEM_SPEC, *[HBM_SPEC] * (ns + nl), pl.BlockSpec(memory_space=pltpu.VMEM)),
        out_shape=(pltpu.SemaphoreType.DMA((ncopy,)), pltpu.SemaphoreType.DMA((ncopy,)), *thru,
                   jax.ShapeDtypeStruct((8, LANE), F32)),
        input_output_aliases={i: 2 + i for i in range(ns + nl)},
        compiler_params=pltpu.CompilerParams(has_side_effects=EFFECT),
    )(*[hbm(a) for a in srcs], *[hbm(a) for a in lands], *deps)
    return outs[0], outs[1], list(outs[2:2 + ns]), list(outs[2 + ns:2 + ns + nl]), outs[-1]


def exchange_wait(name, plan, started, after, place_own=False):
    send_sems, recv_sems, srcs, lands, _ = started
    ns, nl = len(srcs), len(lands)

    def body(*refs):
        s, l = refs[:ns], refs[ns:ns + nl]
        send_sems, recv_sems = refs[ns + nl], refs[ns + nl + 1]
        l_out = refs[2 * ns + nl + 3:2 * ns + 2 * nl + 3]
        scratch = refs[2 * ns + 2 * nl + 3:]
        copies = [_remote(cp, i, send_sems, recv_sems) for i, cp in enumerate(plan(s, l, _coords()))]
        if place_own:
            me = _linear(*_coords())
            local_sems, bufs = scratch[0], scratch[1:]
            loads = [pltpu.make_async_copy(s[q], bufs[q], local_sems.at[q]) for q in range(ns)]
            for cp in loads:
                cp.start()
            for cp in loads:
                cp.wait()
            stores = [pltpu.make_async_copy(bufs[q], l_out[q].at[me], local_sems.at[q]) for q in range(ns)]
            for cp in stores:
                cp.start()
        for cp in copies:
            cp.wait_recv()
        for cp in copies:
            cp.wait_send()
        if place_own:
            for cp in stores:
                cp.wait()

    scratch_shapes = []
    if place_own:
        scratch_shapes = [pltpu.SemaphoreType.DMA((ns,))] + [pltpu.VMEM(a.shape, a.dtype) for a in srcs]
    outs = pl.pallas_call(
        body, name=name,
        in_specs=[HBM_SPEC] * (ns + nl) + [SEM_SPEC, SEM_SPEC, ANY_SPEC],
        out_specs=[HBM_SPEC] * (ns + nl),
        out_shape=[pltpu.HBM(a.shape, a.dtype) for a in srcs + lands],
        input_output_aliases={i: i for i in range(ns + nl)},
        scratch_shapes=scratch_shapes,
        compiler_params=pltpu.CompilerParams(has_side_effects=EFFECT),
    )(*srcs, *lands, send_sems, recv_sems, after)
    return list(outs[:ns]), list(outs[ns:])


def pair_sum(name, g, recv):
    _, r, c = g.shape
    tr = _rows_tile(r, 512)

    def body(g_ref, r_ref, o_ref):
        own = jnp.where(lax.axis_index("c") == 0, g_ref[0, 0], g_ref[0, 1])
        o_ref[0] = (own.astype(F32) + r_ref[0].astype(F32)).astype(o_ref.dtype)

    return pl.pallas_call(
        body, name=name, grid=(NCHIP, r // tr),
        in_specs=[pl.BlockSpec((1, 2, tr, c), lambda k, i: (k, 0, i, 0)), pl.BlockSpec((1, tr, c), lambda k, i: (k, i, 0))],
        out_specs=pl.BlockSpec((1, tr, c), lambda k, i: (k, i, 0)),
        out_shape=jax.ShapeDtypeStruct((NCHIP, r, c), g.dtype), compiler_params=_params(("parallel", "parallel")),
    )(g.reshape(NCHIP, 2, r, c), recv)


def chip_sum_adam(name, partial, recv, w, m, v):
    _, r, c = partial.shape
    tr = _rows_tile(r, 512)

    def body(p_ref, r_ref, w_ref, m_ref, v_ref, g_o, d_o, m_o, v_o):
        chip = 2 * lax.axis_index("x") + lax.axis_index("y")
        own = p_ref[0]
        for k in range(1, NCHIP):
            own = jnp.where(chip == k, p_ref[k], own)
        g = own.astype(F32)
        for k in range(NCHIP - 1):
            g = g + r_ref[k].astype(F32)
        d, m2, v2 = _adam_math(w_ref[...], g, m_ref[...], v_ref[...])
        g_o[...] = g
        d_o[...] = d
        m_o[...] = m2
        v_o[...] = v2

    spec = pl.BlockSpec((tr, c), lambda i: (i, 0))
    out = jax.ShapeDtypeStruct((r, c), F32)
    return pl.pallas_call(
        body, name=name, grid=(r // tr,),
        in_specs=[pl.BlockSpec((NCHIP, tr, c), lambda i: (0, i, 0)), pl.BlockSpec((NCHIP - 1, tr, c), lambda i: (0, i, 0)),
                  spec, spec, spec],
        out_specs=[spec] * 4, out_shape=[out] * 4, compiler_params=_params(("parallel",)),
    )(partial, recv, w, m, v)


def _block_diag(w, rows_per, cols_per):
    w = w.reshape(NBLK, 8, rows_per, cols_per)
    eye = jnp.eye(8, dtype=w.dtype)
    out = w[:, :, :, None, :] * eye[None, :, None, :, None]
    return out.reshape(NBLK, 8 * rows_per, 8 * cols_per)


def _diag_blocks(wd, rows_per, cols_per):
    wd = wd.reshape(NBLK, 8, rows_per, 8, cols_per)
    idx = jnp.arange(8)
    return wd[:, idx, :, idx, :].transpose(1, 0, 2, 3).reshape(NG, rows_per, cols_per)


def _pad_rows(v, mult):
    n = v.shape[0]
    return jnp.pad(v, (0, (-n) % mult))


def kernel(x, c, w_ada, b_ada, norm1_g, w_in, conv_w, conv_b, conv_ln_g, conv_ln_b, conv_proj, ssm_a_re, ssm_a_im, ssm_b_re, ssm_b_im, ssm_c_re, ssm_c_im, ssm_d, ssm_log_dt, ssm_glu, w_out, norm2_g, w_ffn_in, w_ffn_out, final_g, loss_target, m_w_ada, m_b_ada, m_norm1_g, m_w_in, m_conv_w, m_conv_b, m_conv_ln_g, m_conv_ln_b, m_conv_proj, m_ssm_a_re, m_ssm_a_im, m_ssm_b_re, m_ssm_b_im, m_ssm_c_re, m_ssm_c_im, m_ssm_d, m_ssm_log_dt, m_ssm_glu, m_w_out, m_norm2_g, m_w_ffn_in, m_w_ffn_out, m_final_g, v_w_ada, v_b_ada, v_norm1_g, v_w_in, v_conv_w, v_conv_b, v_conv_ln_g, v_conv_ln_b, v_conv_proj, v_ssm_a_re, v_ssm_a_im, v_ssm_b_re, v_ssm_b_im, v_ssm_c_re, v_ssm_c_im, v_ssm_d, v_ssm_log_dt, v_ssm_glu, v_w_out, v_norm2_g, v_w_ffn_in, v_w_ffn_out, v_final_g):
    me = _linear(*_coords())
    xs = x[0]
    tgt = loss_target[0]

    flat = lambda g: g.reshape(NDEV * g.shape[1], g.shape[2])
    w_in_s = w_in[0].T.astype(BF16)
    mids = [p.astype(BF16) for p in (conv_proj[0].T, ssm_glu[0].T, w_out[0])]
    ffns = [p.astype(BF16) for p in (w_ffn_in[0].T, w_ffn_out[0])]
    zone = lambda p: ((NDEV,) + p.shape, p.dtype)
    c_all, cw_g = all_gather("gather_c_conv_w", [c, conv_w[0]])
    in_go = exchange_start("gather_in_start", near_plan, NCHIP, [w_in_s], [zone(w_in_s)], deps=[c_all])
    mids_go = exchange_start("gather_mid_start", gather_plan, 7 * len(mids), mids, [zone(p) for p in mids],
                             deps=[in_go[4]])
    ffns_go = exchange_start("gather_ffn_start", gather_plan, 7 * len(ffns), ffns, [zone(p) for p in ffns],
                             deps=[mids_go[4]])

    ncol = w_ada.shape[2]
    c_all = c_all.reshape(NDEV, D)
    b_cols = lax.dynamic_slice_in_dim(b_ada, me * ncol, ncol, axis=1)
    act_all, mod_cols = mod_fwd(c_all, w_ada[0], b_cols)
    (mod_all,) = all_gather("gather_mod", [mod_cols])
    mod = lax.dynamic_index_in_dim(mod_all, me, axis=1, keepdims=False).reshape(NMOD, D)
    sh1, sc1, g1, sh2, sc2, g2 = [mod[q:q + 1] for q in range(NMOD)]

    expand = jnp.repeat(jnp.eye(NG, dtype=F32), NP, axis=0)
    a_re_c, a_im_c = ssm_a_re.reshape(NST, 1), ssm_a_im.reshape(NST, 1)
    ldt_c = ssm_log_dt.reshape(NG, 1)
    b_re_r, b_im_r = ssm_b_re.reshape(NST, GH), ssm_b_im.reshape(NST, GH)
    e_re, e_im, bb_re, bb_im = disc_fwd(a_re_c, a_im_c, ldt_c, b_re_r, b_im_r, expand)
    e_re_b, e_im_b = e_re.reshape(NBLK, 1, SB), e_im.reshape(NBLK, 1, SB)
    bb_re_g, bb_im_g = bb_re.reshape(NG, NP, GH), bb_im.reshape(NG, NP, GH)
    wbt_re = _block_diag(bb_re_g, NP, GH)
    wbt_im = _block_diag(bb_im_g, NP, GH)
    wb_re, wb_im = wbt_re.transpose(0, 2, 1), wbt_im.transpose(0, 2, 1)
    wct = jnp.concatenate([_block_diag(ssm_c_re[0], GH, NP), -_block_diag(ssm_c_im[0], GH, NP)], axis=2)
    wc = wct.transpose(0, 2, 1)
    to_b = lambda a: a.astype(BF16)
    dvec = ssm_d.reshape(NBLK, 1, CB)

    n1g = norm1_g

    def f_norm1(xv, g, sc, sh):
        _, xh = _rms_stats(xv)
        return [xh * g * (1.0 + sc) + sh], []

    (h1,) = rowwise("norm1", f_norm1, [xs], [n1g, sc1, sh1], [(D, BF16)], [], 512, deps=[ffns_go[4]])
    _, (w_in_land,) = exchange_wait("gather_in_wait", near_plan, in_go, h1, place_own=True)
    pass_go = exchange_start("gather_in_pass_start", pass_on_plan, NCHIP - 1, [w_in_land], [])
    (w_in_g,), _ = exchange_wait("gather_in_pass_wait", pass_on_plan, pass_go, pass_go[4])
    w_in_t = flat(w_in_g)
    z = mm("mm_in", h1, w_in_t, "nt", tiles=(2048, CW, 1024), b_rot=Z_ROT)

    conv_w_full = cw_g.transpose(1, 0, 2).reshape(KC, CW)
    w32 = jnp.pad(conv_w_full, ((0, HALO - KC), (0, 0)))
    yc, s_act = conv_fwd(z, w32, conv_b, conv_ln_g, conv_ln_b)
    conv_proj_t, ssm_glu_t, w_out_f = [
        flat(g) for g in exchange_wait("gather_mid_wait", gather_plan, mids_go, s_act, place_own=True)[1]]
    y_conv = mm("mm_conv_proj", s_act, conv_proj_t, "nt")

    xs_re, xs_im, ypre, gl = ssm_fwd(z, to_b(wb_re), to_b(wb_im), to_b(wc), e_re_b, e_im_b, dvec)
    n_mrg = D // MRG_BLK

    pair_of = lambda t, n: t // 2 + (t % 2) * n

    def ep_merge(accs, yc_v, gates):
        za, zb = accs
        glc, gls = gates[:, 0:MRG_BLK], gates[:, MRG_BLK:2 * MRG_BLK]
        return [_sig(glc) * yc_v + _sig(gls) * (za * _sig(zb)), jnp.concatenate([za, zb], axis=1)]

    merged, z2_pair = mm_ep("mm_ssm_glu", gl, ssm_glu_t, 2, lambda j, q: j + q * n_mrg, ep_merge,
                            [(y_conv, 1, 0), (z, 2, 0)], [(D, BF16, 1), (2 * D, BF16, 2)], (512, MRG_BLK, SW))
    row_tiles = lambda bk: (512, D, bk)
    whole = lambda j, q: j

    def ep_norm2(accs, xv, g1v, g, sc, sh):
        (o1v,) = accs
        x1v = xv + g1v * o1v
        _, xh = _rms_stats(x1v)
        return [x1v, xh * g * (1.0 + sc) + sh, o1v]

    x1, h2, o1 = mm_ep("mm_out", merged, w_out_f, 1, whole, ep_norm2, [(xs, 1, 0)],
                       [(D, F32, 1), (D, BF16, 1), (D, BF16, 1)], row_tiles(D), b_kn=True,
                       consts=[g1, norm2_g, sc2, sh2])
    w_ffn_in_t, w_ffn_out_f = [
        flat(g) for g in exchange_wait("gather_ffn_wait", gather_plan, ffns_go, h2, place_own=True)[1]]
    ffn_tiles = (512, FFN_BLK, 1024)
    n_ffn_blk = FH // FFN_BLK
    pair_map = lambda t: t // 2 + (t % 2) * n_ffn_blk

    def ep_swiglu(accs):
        fg, fu = accs
        return [fg * _sig(fg) * fu, jnp.concatenate([fg, fu], axis=1)]

    act, f_pair = mm_ep("mm_ffn_in", h2, w_ffn_in_t, 2, lambda j, q: j + q * n_ffn_blk, ep_swiglu, [],
                        [(FH, BF16, 1), (2 * FH, BF16, 2)], ffn_tiles)
    fg_row = final_g.reshape(1, D)

    def ep_final(accs, x1v, tv, g2v, fg):
        (o2v,) = accs
        x2v = x1v + g2v * o2v
        r, xh = _rms_stats(x2v)
        yv = xh * fg
        err = yv - tv
        loss = jnp.sum(_colsum(err * err), axis=1, keepdims=True) * (0.5 / D)
        dy = err * (1.0 / D)
        dx2 = _rms_bwd(dy * fg, xh, r)
        return ([dx2, g2v * dx2],
                [jnp.broadcast_to(loss, (1, LANE)), _colsum(dy * xh), _colsum(dx2 * o2v)])

    dx2, do2, loss_l, d_final_g, d_g2 = mm_ep_pipe(
        "mm_ffn_out", act, w_ffn_out_f, 1, whole, ep_final, [(x1, 1, 0), (tgt, 1, 0)],
        [(D, F32, 1), (D, BF16, 1)], row_tiles(FH), b_kn=True, consts=[g2, fg_row], sums=[LANE, D, D])

    g_ffn_out = mm("mm_g_ffn_out", act, do2, "tn", BF16, tiles=(FFN_BLK, 1024, 1024))

    def ep_dswiglu(accs, fp):
        (da,) = accs
        fg, fu = fp[:, 0:FFN_BLK].astype(F32), fp[:, FFN_BLK:2 * FFN_BLK].astype(F32)
        sg = _sig(fg)
        return [jnp.concatenate([da * fu * (sg * (1.0 + fg * (1.0 - sg))), da * (fg * sg)], axis=1)]

    (df,) = mm_ep("mm_dact", do2, w_ffn_out_f, 1, lambda j, q: j, ep_dswiglu, [(f_pair, 2, 0)],
                  [(2 * FH, BF16, 2)], ffn_tiles)
    g_ffn_in_t = mm("mm_g_ffn_in", df, h2, "tn", BF16, tiles=(FFN_BLK, 1024, 1024), o_rot=pair_map)

    def pair_go(tag, grads_t, deps=()):
        srcs = [g.reshape(NDEV, -1, D) for g in grads_t]
        return exchange_start("pair_" + tag + "_start", pair_plan, NCHIP * len(srcs), srcs,
                              [((NCHIP,) + s.shape[1:], s.dtype) for s in srcs], deps)

    def chip_go(tag, names, pair_started, after):
        own, from_sibling = exchange_wait("pair_" + tag + "_wait", pair_plan, pair_started, after)
        partials = [pair_sum("pair_sum_" + n, g, r) for n, g, r in zip(names, own, from_sibling)]
        return exchange_start("chip_" + tag + "_start", chip_plan, (NCHIP - 1) * len(partials), partials,
                              [((NCHIP - 1,) + p.shape[1:], p.dtype) for p in partials])

    def chip_done(tag, chip_started, after):
        partials, from_chips = exchange_wait("chip_" + tag + "_wait", chip_plan, chip_started, after)
        return list(zip(partials, from_chips))

    pair_ffn = pair_go("ffn", [g_ffn_out, g_ffn_in_t])

    dh2 = mm("mm_dh2", df, w_ffn_in_t, "nn", BF16, tiles=(1024, 1024, FFN_BLK), b_rot=pair_map,
             deps=[pair_ffn[4]])

    def f_dnorm2(dh, x1v, dx2v, o1v, g, sc, g1v):
        dh, o1v = dh.astype(F32), o1v.astype(F32)
        r, xh = _rms_stats(x1v)
        dxh = dh * (1.0 + sc) * g
        dx1 = dx2v + _rms_bwd(dxh, xh, r)
        return ([dx1, g1v * dx1],
                [_colsum(dh * xh * g), _colsum(dh), _colsum(dh * (1.0 + sc) * xh), _colsum(dx1 * o1v)])

    dx1, do1, d_sc2, d_sh2, d_n2g, d_g1 = rowwise(
        "dnorm2", f_dnorm2, [dh2, x1, dx2, o1], [norm2_g, sc2, g1], [(D, F32), (D, BF16)], [D, D, D, D], 512)

    g_out = mm("mm_g_out", merged, do1, "tn", BF16)
    chip_ffn = chip_go("ffn", ("w_ffn_out", "w_ffn_in"), pair_ffn, g_out)

    def ep_dmerge(accs, yc_v, z2p, gates):
        (dm,) = accs
        za, zb = z2p[:, 0:MRG_BLK].astype(F32), z2p[:, MRG_BLK:2 * MRG_BLK].astype(F32)
        sc_, ss_, sb_ = _sig(gates[:, 0:MRG_BLK]), _sig(gates[:, MRG_BLK:2 * MRG_BLK]), _sig(zb)
        dys = dm * ss_
        dz2 = jnp.concatenate([dys * sb_, dys * za * sb_ * (1.0 - sb_)], axis=1)
        dgates = jnp.concatenate([dm * yc_v * sc_ * (1.0 - sc_), dm * (za * sb_) * ss_ * (1.0 - ss_)], axis=1)
        return [dm * sc_, dz2, dgates]

    dyconv, dz2, dz = mm_ep("mm_dmerged", do1, w_out_f, 1, lambda j, q: j, ep_dmerge,
                            [(y_conv, 1, 0), (z2_pair, 2, 0), (z, 2, 0)],
                            [(D, BF16, 1), (2 * D, BF16, 2), (ZW, BF16, 2)], (512, MRG_BLK, 1024), deps=[chip_ffn[4]])

    g_conv_proj_t = mm("mm_g_conv_proj", dyconv, s_act, "tn", BF16)
    mrg_map = lambda t: pair_of(t, n_mrg)
    dgl = mm("mm_dgl", dz2, ssm_glu_t, "nn", tiles=(2048, SW, MRG_BLK), b_rot=mrg_map)
    g_ssm_glu_t = mm("mm_g_ssm_glu", dz2, gl, "tn", BF16, tiles=(MRG_BLK, SW, 1024), o_rot=mrg_map)
    pair_mid = pair_go("mid", [g_out, g_conv_proj_t, g_ssm_glu_t])
    ds = mm("mm_ds", dyconv, conv_proj_t, "nn", tiles=(2048, CW, 1024), deps=[pair_mid[4]])
    dz, d_lng, d_lnb, d_cb, d_cw32 = conv_bwd(ds, yc, z, w32, conv_ln_g, conv_ln_b, dz)
    dz, d_d, d_ar, d_ai, d_wb_re, d_wb_im, d_wc = ssm_bwd(
        dgl, ypre, z, xs_re, xs_im, to_b(wbt_re), to_b(wbt_im), to_b(wct), e_re_b, e_im_b, dvec, dz)
    chip_mid = chip_go("mid", ("w_out", "conv_proj", "ssm_glu"), pair_mid, dz)

    d_bb_re = _diag_blocks(d_wb_re.transpose(0, 2, 1), NP, GH).reshape(NST, GH)
    d_bb_im = _diag_blocks(d_wb_im.transpose(0, 2, 1), NP, GH).reshape(NST, GH)
    d_wct = d_wc.transpose(0, 2, 1)
    d_c_re = _diag_blocks(d_wct[:, :, 0:SB], GH, NP)
    d_c_im = -_diag_blocks(d_wct[:, :, SB:2 * SB], GH, NP)
    d_a_re, d_a_im, d_ldt, d_b_re, d_b_im = disc_bwd(
        a_re_c, a_im_c, ldt_c, b_re_r, b_im_r, expand, d_ar.reshape(NST, 1), d_ai.reshape(NST, 1), d_bb_re, d_bb_im)

    small_local = [jnp.concatenate([d_g1, d_sh2, d_sc2, d_g2], axis=1).reshape(-1), d_cw32[0:KC].reshape(-1),
                   d_cb.reshape(-1), d_lng.reshape(-1), d_lnb.reshape(-1), d_a_re.reshape(-1), d_a_im.reshape(-1),
                   d_b_re.reshape(-1), d_b_im.reshape(-1), d_c_re.reshape(-1), d_c_im.reshape(-1), d_d.reshape(-1),
                   d_ldt.reshape(-1), d_n2g.reshape(-1), d_final_g.reshape(-1), loss_l[0, 0:1]]
    small_sizes = [v.shape[0] for v in small_local]
    small_pack = _pad_rows(jnp.concatenate(small_local), 256 * LANE).reshape(-1, LANE)
    small_go = exchange_start("gather_small_start", gather_plan, NDEV - 1, [small_pack],
                              [((NDEV,) + small_pack.shape, F32)], deps=[chip_mid[4]])

    g_in_t = mm("mm_g_in", dz, h1, "tn", BF16, tiles=(CW, 1024, 2048), o_rot=Z_ROT, deps=[small_go[4]])
    pair_in = pair_go("in", [g_in_t])

    dh1 = mm("mm_dh1", dz, w_in_t, "nn", BF16, tiles=(2048, 1024, CW), b_rot=Z_ROT, deps=[pair_in[4]])

    def f_dnorm1(dh, xv, dx1v, g, sc):
        dh = dh.astype(F32)
        r, xh = _rms_stats(xv)
        dxh = dh * (1.0 + sc) * g
        return ([dx1v + _rms_bwd(dxh, xh, r)],
                [_colsum(dh * xh * g), _colsum(dh), _colsum(dh * (1.0 + sc) * xh)])

    grad_x, d_sc1, d_sh1, d_n1g = rowwise(
        "dnorm1", f_dnorm1, [dh1, xs, dx1], [n1g, sc1], [(D, F32)], [D, D, D], 512)
    late_local = [d_sh1.reshape(-1), d_sc1.reshape(-1), d_n1g.reshape(-1)]
    late_pack = _pad_rows(jnp.concatenate(late_local), 16 * LANE).reshape(-1, LANE)
    late_go = exchange_start("gather_late_start", gather_plan, NDEV - 1, [late_pack],
                             [((NDEV,) + late_pack.shape, F32)])
    chip_in = chip_go("in", ("w_in",), pair_in, late_go[4])

    weights = {
        "w_ada": (w_ada, m_w_ada, v_w_ada), "b_ada": (b_ada, m_b_ada, v_b_ada), "norm1_g": (norm1_g, m_norm1_g, v_norm1_g),
        "w_in": (w_in, m_w_in, v_w_in), "conv_w": (conv_w, m_conv_w, v_conv_w), "conv_b": (conv_b, m_conv_b, v_conv_b),
        "conv_ln_g": (conv_ln_g, m_conv_ln_g, v_conv_ln_g), "conv_ln_b": (conv_ln_b, m_conv_ln_b, v_conv_ln_b),
        "conv_proj": (conv_proj, m_conv_proj, v_conv_proj), "ssm_a_re": (ssm_a_re, m_ssm_a_re, v_ssm_a_re),
        "ssm_a_im": (ssm_a_im, m_ssm_a_im, v_ssm_a_im), "ssm_b_re": (ssm_b_re, m_ssm_b_re, v_ssm_b_re),
        "ssm_b_im": (ssm_b_im, m_ssm_b_im, v_ssm_b_im), "ssm_c_re": (ssm_c_re, m_ssm_c_re, v_ssm_c_re),
        "ssm_c_im": (ssm_c_im, m_ssm_c_im, v_ssm_c_im), "ssm_d": (ssm_d, m_ssm_d, v_ssm_d),
        "ssm_log_dt": (ssm_log_dt, m_ssm_log_dt, v_ssm_log_dt), "ssm_glu": (ssm_glu, m_ssm_glu, v_ssm_glu),
        "w_out": (w_out, m_w_out, v_w_out), "norm2_g": (norm2_g, m_norm2_g, v_norm2_g),
        "w_ffn_in": (w_ffn_in, m_w_ffn_in, v_w_ffn_in), "w_ffn_out": (w_ffn_out, m_w_ffn_out, v_w_ffn_out),
        "final_g": (final_g, m_final_g, v_final_g),
    }
    order = list(weights)
    big = ("w_ada", "w_in", "conv_proj", "ssm_glu", "w_out", "w_ffn_in", "w_ffn_out")
    grads, delta, new_m, new_v = {}, {}, {}, {}

    def adam_big(n, g2d, transposed=False):
        wv, mv, vv = weights[n]
        shp = wv.shape
        t_in = (lambda a: a.reshape(shp[-2:]).T) if transposed else (lambda a: a.reshape(shp[-2:]))
        t_out = (lambda a: a.T.reshape(shp)) if transposed else (lambda a: a.reshape(shp))
        w2 = t_in(wv)
        if isinstance(g2d, tuple):
            partial, recv = [p.reshape((p.shape[0],) + w2.shape) for p in g2d]
            g2d, d_, m_, v_ = chip_sum_adam("adam_" + n, partial, recv, w2, t_in(mv), t_in(vv))
        else:
            d_, m_, v_ = adam("adam_" + n, w2, g2d, t_in(mv), t_in(vv))
        grads[n], delta[n], new_m[n], new_v[n] = t_out(g2d), t_out(d_), t_out(m_), t_out(v_)
        return d_

    parts_ffn_out, parts_ffn_in = chip_done("ffn", chip_ffn, chip_in[4])
    adam_big("w_ffn_out", parts_ffn_out)
    last = adam_big("w_ffn_in", parts_ffn_in, transposed=True)
    parts_out, parts_conv_proj, parts_ssm_glu = chip_done("mid", chip_mid, last)
    adam_big("w_out", parts_out)
    adam_big("conv_proj", parts_conv_proj, transposed=True)
    last_mid = adam_big("ssm_glu", parts_ssm_glu, transposed=True)

    _, (late_all,) = exchange_wait("gather_late_wait", gather_plan, late_go, last_mid, place_own=True)
    _, (small_all,) = exchange_wait("gather_small_wait", gather_plan, small_go, late_all, place_own=True)

    def unpack(vec, sizes):
        out, pos = [], 0
        for n in sizes:
            out.append(vec[pos:pos + n])
            pos += n
        return out

    g_sh1, g_sc1, g_n1g = unpack(sum_slots("sum_small_late", late_all).reshape(-1), [D, D, D])
    (g_mod_rest, g_cw_full, g_cb, g_lng, g_lnb, g_a_re, g_a_im, g_b_re, g_b_im, g_c_re, g_c_im, g_d, g_ldt,
     g_n2g, g_fg, loss_sum) = unpack(sum_slots("sum_small", small_all).reshape(-1), small_sizes)
    g_b_ada = jnp.concatenate([g_sh1, g_sc1, g_mod_rest])
    loss = loss_sum[0]
    dmod_all = jnp.concatenate([late_all.reshape(NDEV, -1)[:, 0:2 * D], small_all.reshape(NDEV, -1)[:, 0:4 * D]],
                               axis=1)
    g_w_ada = ada_grad(act_all, lax.dynamic_slice_in_dim(dmod_all, me * ncol, ncol, axis=1))
    ccol = conv_w.shape[2]
    g_conv_w = lax.dynamic_slice_in_dim(g_cw_full.reshape(KC, CW), me * ccol, ccol, axis=1)

    adam_big("w_ada", g_w_ada)
    grads.update({
        "b_ada": g_b_ada.reshape(b_ada.shape), "norm1_g": g_n1g.reshape(norm1_g.shape),
        "conv_w": g_conv_w[None], "conv_b": g_cb.reshape(conv_b.shape),
        "conv_ln_g": g_lng.reshape(conv_ln_g.shape), "conv_ln_b": g_lnb.reshape(conv_ln_b.shape),
        "ssm_a_re": g_a_re.reshape(ssm_a_re.shape),
        "ssm_a_im": g_a_im.reshape(ssm_a_im.shape), "ssm_b_re": g_b_re.reshape(ssm_b_re.shape),
        "ssm_b_im": g_b_im.reshape(ssm_b_im.shape), "ssm_c_re": g_c_re.reshape(ssm_c_re.shape),
        "ssm_c_im": g_c_im.reshape(ssm_c_im.shape), "ssm_d": g_d.reshape(ssm_d.shape),
        "ssm_log_dt": g_ldt.reshape(ssm_log_dt.shape),
        "norm2_g": g_n2g.reshape(norm2_g.shape),
        "final_g": g_fg.reshape(final_g.shape),
    })
    small = [n for n in order if n not in big]
    def rows(a):
        if a.ndim == 4 and a.shape[-1] < a.shape[-2]:
            a = a.swapaxes(-1, -2)
        return a.reshape(1, -1) if a.ndim == 1 else a.reshape(-1, a.shape[-1])

    def unrows(a, shp):
        if len(shp) == 4 and shp[-1] < shp[-2]:
            return a.reshape(shp[:-2] + (shp[-1], shp[-2])).swapaxes(-1, -2)
        return a.reshape(shp)

    small_out = adam_many("adam_small", [rows(weights[n][0]) for n in small], [rows(grads[n]) for n in small],
                          [rows(weights[n][1]) for n in small], [rows(weights[n][2]) for n in small])
    for q, n in enumerate(small):
        shp = weights[n][0].shape
        delta[n], new_m[n], new_v[n] = [unrows(small_out[t * len(small) + q], shp) for t in range(3)]

    (parts_in,) = chip_done("in", chip_in, small_out[0])
    adam_big("w_in", parts_in, transposed=True)

    return (loss, grad_x[None], *[grads[n] for n in order], *[delta[n] for n in order],
            *[new_m[n] for n in order], *[new_v[n] for n in order])
```

```python
import functools
import math

import jax
import jax.numpy as jnp
from jax import lax
from jax.experimental import pallas as pl
from jax.experimental.pallas import tpu as pltpu

F32 = jnp.float32
BF16 = jnp.bfloat16

D = 1024
CW = 512
KC = 31
SW = 512
NG = 32
GH = 16
NP = 64
NST = NG * NP
FH = 2816
FFN_BLK = 1408
MRG_BLK = 1024
NMOD = 6
NDEV = 8
EPS = 1e-6
CB = 128
SB = 512
NBLK = SW // CB
HALO = 32
ZW = 2 * CW + SW + 2 * D
Z_ROT = lambda j: (j + 3) % (ZW // CW)
ZB_A, ZB_G, ZB_U = 4, 5, 6

ADAM_LR = 0.001
ADAM_B1 = 0.9
ADAM_B2 = 0.999
ADAM_EPS = 1e-08
ADAM_WD = 0.01
ADAM_STEP = 10

V7X_VMEM_BYTES = 64 * 1024 * 1024
VMEM_LIMIT = V7X_VMEM_BYTES - 8 * 1024 * 1024
LANE = 128
MESH = pl.DeviceIdType.MESH
ANY_SPEC = pl.BlockSpec(memory_space=pl.ANY)


def _params(sem=None, **kw):
    if sem is not None:
        kw["dimension_semantics"] = sem
    return pltpu.CompilerParams(vmem_limit_bytes=VMEM_LIMIT, **kw)


def _tile(n, most):
    best = None
    for t in range(LANE, most + 1, LANE):
        if n % t == 0:
            best = t
    if best is None:
        raise ValueError(f"no tile for {n}")
    return best


def _sig(x):
    return jax.nn.sigmoid(x)


def mm(name, a, b, mode, out_dtype=F32, tiles=None, b_rot=None, o_rot=None, deps=()):
    if mode == "nn":
        (m, k), (k2, n) = a.shape, b.shape
    elif mode == "nt":
        (m, k), (n, k2) = a.shape, b.shape
    else:
        (k, m), (k2, n) = a.shape, b.shape
    assert k == k2, (name, a.shape, b.shape)
    bm, bn, bk = tiles or (_tile(m, 1024), _tile(n, 1408), _tile(k, 1408 if k % 1408 == 0 else 1024))
    bm, bn, bk = min(bm, m), min(bn, n), min(bk, k)
    assert m % bm == 0 and n % bn == 0 and k % bk == 0, (name, m, n, k, bm, bn, bk)
    nk = k // bk
    rot = lambda idx, r: idx if r is None else r(idx)
    if mode == "nn":
        a_spec = pl.BlockSpec((bm, bk), lambda i, j, kk: (i, kk))
        b_spec = pl.BlockSpec((bk, bn), lambda i, j, kk: (rot(kk, b_rot), j))
        dims = (((1,), (0,)), ((), ()))
    elif mode == "nt":
        a_spec = pl.BlockSpec((bm, bk), lambda i, j, kk: (i, kk))
        b_spec = pl.BlockSpec((bn, bk), lambda i, j, kk: (rot(j, b_rot), kk))
        dims = (((1,), (1,)), ((), ()))
    else:
        assert b_rot is None
        a_spec = pl.BlockSpec((bk, bm), lambda i, j, kk: (kk, i))
        b_spec = pl.BlockSpec((bk, bn), lambda i, j, kk: (kk, j))
        dims = (((0,), (0,)), ((), ()))

    def body(a_ref, b_ref, *rest):
        o_ref, acc_ref = rest[-2:]
        kk = pl.program_id(2)

        @pl.when(kk == 0)
        def _():
            acc_ref[...] = jnp.zeros_like(acc_ref)

        acc_ref[...] += lax.dot_general(a_ref[...], b_ref[...], dims, preferred_element_type=F32)

        @pl.when(kk == nk - 1)
        def _():
            o_ref[...] = acc_ref[...].astype(o_ref.dtype)

    return pl.pallas_call(
        body, name=name,
        grid=(m // bm, n // bn, nk),
        in_specs=[a_spec, b_spec] + [ANY_SPEC] * len(deps),
        out_specs=pl.BlockSpec((bm, bn), lambda i, j, kk: (rot(i, o_rot), j)),
        out_shape=jax.ShapeDtypeStruct((m, n), out_dtype),
        scratch_shapes=[pltpu.VMEM((bm, bn), F32)],
        compiler_params=_params(("parallel", "parallel", "arbitrary")),
    )(a, b, *deps)


def mm_ep(name, a, b, n_acc, acc_block, epilogue, extras, outs, tiles, deps=(), b_kn=False, consts=()):
    m, k = a.shape
    bm, bn, bk = tiles
    bm = min(bm, m)
    nj = outs[0][0] // (outs[0][2] * bn)
    assert m % bm == 0 and bk == k and b.shape[0 if b_kn else 1] == k, (name, a.shape, b.shape, tiles)
    ne, nc, no, nd = len(extras), len(consts), len(outs), len(deps)
    dims = (((1,), (0,)), ((), ())) if b_kn else (((1,), (1,)), ((), ()))

    def body(*refs):
        a_ref, b_refs = refs[0], refs[1:1 + n_acc]
        e_refs = refs[1 + n_acc:1 + n_acc + ne + nc]
        first_out = 1 + n_acc + ne + nc + nd
        o_refs = refs[first_out:first_out + no]
        av = a_ref[...]
        prods = [lax.dot_general(av, b_ref[...], dims, preferred_element_type=F32) for b_ref in b_refs]
        for o_ref, v in zip(o_refs, epilogue(prods, *[e[...] for e in e_refs])):
            o_ref[...] = v.astype(o_ref.dtype)

    in_specs = [pl.BlockSpec((bm, bk), lambda i, j: (i, 0))]
    if b_kn:
        in_specs += [pl.BlockSpec((bk, bn), functools.partial(lambda i, j, q: (0, acc_block(j, q)), q=q))
                     for q in range(n_acc)]
    else:
        in_specs += [pl.BlockSpec((bn, bk), functools.partial(lambda i, j, q: (acc_block(j, q), 0), q=q))
                     for q in range(n_acc)]
    in_specs += [pl.BlockSpec((bm, w * bn), functools.partial(lambda i, j, off: (i, j + off), off=off))
                 for (_, w, off) in extras]
    in_specs += [pl.BlockSpec((1, bn), lambda i, j: (0, j)) for _ in consts]
    in_specs += [ANY_SPEC] * nd
    return pl.pallas_call(
        body, name=name, grid=(m // bm, nj),
        in_specs=in_specs,
        out_specs=[pl.BlockSpec((bm, w * bn), lambda i, j: (i, j)) for (_, _, w) in outs],
        out_shape=[jax.ShapeDtypeStruct((m, cols), dt) for (cols, dt, _) in outs],
        compiler_params=_params(("parallel", "parallel")),
    )(a, *[b] * n_acc, *[e[0] for e in extras], *consts, *deps)


def mm_ep_pipe(name, a, b, n_acc, acc_block, epilogue, extras, outs, tiles, deps=(), b_kn=False, consts=(), sums=()):
    m, k = a.shape
    bm, bn, bk = tiles
    bm = min(bm, m)
    assert bk == k and m % bm == 0 and b.shape[0 if b_kn else 1] == k, (name, a.shape, b.shape, tiles)
    ni, nj = m // bm, outs[0][0] // (outs[0][2] * bn)
    nt = ni * nj
    assert not sums or nj == 1, name
    ne, nc, no, ns, nd = len(extras), len(consts), len(outs), len(sums), len(deps)
    dims = (((1,), (0,)), ((), ())) if b_kn else (((1,), (1,)), ((), ()))
    cur_i = lambda t: jnp.minimum(t, nt - 1) // nj
    cur_j = lambda t: jnp.minimum(t, nt - 1) % nj
    prev_i = lambda t: jnp.maximum(t - 1, 0) // nj
    prev_j = lambda t: jnp.maximum(t - 1, 0) % nj

    def body(*refs):
        a_ref, b_refs = refs[0], refs[1:1 + n_acc]
        e_refs = refs[1 + n_acc:1 + n_acc + ne + nc]
        first_out = 1 + n_acc + ne + nc + nd
        o_refs = refs[first_out:first_out + no]
        s_refs = refs[first_out + no:first_out + no + ns]
        acc_ref = refs[first_out + no + ns]
        t = pl.program_id(0)

        @pl.when(t == 0)
        def _():
            acc_ref[...] = jnp.zeros_like(acc_ref)
            for s_ref in s_refs:
                s_ref[...] = jnp.zeros_like(s_ref)

        slot = t % 2
        done = [acc_ref[(1 - slot) * n_acc + q] for q in range(n_acc)]
        av = a_ref[...]
        for q, b_ref in enumerate(b_refs):
            acc_ref[slot * n_acc + q] = lax.dot_general(av, b_ref[...], dims, preferred_element_type=F32)
        res = epilogue(done, *[e[...] for e in e_refs])
        tiles_out, sums_out = res if ns else (res, ())
        for o_ref, v in zip(o_refs, tiles_out):
            o_ref[...] = v.astype(o_ref.dtype)
        live = (t >= 1).astype(F32)
        for s_ref, v in zip(s_refs, sums_out):
            s_ref[...] += v * live

    in_specs = [pl.BlockSpec((bm, k), lambda t: (cur_i(t), 0))]
    if b_kn:
        in_specs += [pl.BlockSpec((k, bn), functools.partial(lambda t, q: (0, acc_block(cur_j(t), q)), q=q))
                     for q in range(n_acc)]
    else:
        in_specs += [pl.BlockSpec((bn, k), functools.partial(lambda t, q: (acc_block(cur_j(t), q), 0), q=q))
                     for q in range(n_acc)]
    in_specs += [pl.BlockSpec((bm, w * bn), functools.partial(lambda t, off: (prev_i(t), prev_j(t) + off), off=off))
                 for (_, w, off) in extras]
    in_specs += [pl.BlockSpec((1, bn), lambda t: (0, prev_j(t))) for _ in consts]
    in_specs += [ANY_SPEC] * nd
    out_specs = [pl.BlockSpec((bm, w * bn), lambda t: (prev_i(t), prev_j(t))) for (_, _, w) in outs]
    out_specs += [pl.BlockSpec((1, w), lambda t: (0, 0)) for w in sums]
    out_shape = [jax.ShapeDtypeStruct((m, cols), dt) for (cols, dt, _) in outs]
    out_shape += [jax.ShapeDtypeStruct((1, w), F32) for w in sums]
    return pl.pallas_call(
        body, name=name, grid=(nt + 1,),
        in_specs=in_specs, out_specs=out_specs, out_shape=out_shape,
        scratch_shapes=[pltpu.VMEM((2 * n_acc, bm, bn), F32)],
        compiler_params=_params(("arbitrary",)),
    )(a, *[b] * n_acc, *[e[0] for e in extras], *consts, *deps)


def rowwise(name, fn, rows, consts, out_rows, out_sums, ts, alias=None, deps=()):
    rows = [r if isinstance(r, tuple) else (r, r.shape[1], 0) for r in rows]
    out_rows = [o if len(o) == 4 else (o[0], o[1], o[0], 0) for o in out_rows]
    s = rows[0][0].shape[0]
    nt = s // ts
    nr, nc, no, ns = len(rows), len(consts), len(out_rows), len(out_sums)
    in_specs = [pl.BlockSpec((ts, w), functools.partial(lambda i, cb: (i, cb), cb=cb)) for (_, w, cb) in rows]
    in_specs += [pl.BlockSpec(c.shape, lambda i: (0, 0)) for c in consts]
    operands = [r[0] for r in rows] + list(consts)
    aliases = {}
    if alias is not None:
        in_specs.append(pl.BlockSpec(memory_space=pl.ANY))
        operands.append(alias[0])
        aliases = {nr + nc: alias[1]}
    in_specs += [ANY_SPEC] * len(deps)
    operands += list(deps)
    out_shape = [jax.ShapeDtypeStruct((s, tw), dt) for (_, dt, tw, _) in out_rows]
    out_shape += [jax.ShapeDtypeStruct((1, w), F32) for w in out_sums]
    out_specs = [pl.BlockSpec((ts, w), functools.partial(lambda i, cb: (i, cb), cb=cb)) for (w, _, _, cb) in out_rows]
    out_specs += [pl.BlockSpec((1, w), lambda i: (0, 0)) for w in out_sums]
    n_in = len(operands)

    def body(*refs):
        ins, outs = refs[:nr + nc], refs[n_in:]
        i = pl.program_id(0)
        ro, so = fn(*[r[...] for r in ins])
        for q in range(no):
            outs[q][...] = ro[q].astype(outs[q].dtype)
        if ns:
            @pl.when(i == 0)
            def _():
                for q in range(ns):
                    outs[no + q][...] = jnp.zeros_like(outs[no + q])

            for q in range(ns):
                outs[no + q][...] += so[q]

    return pl.pallas_call(
        body, name=name, grid=(nt,),
        in_specs=in_specs, out_specs=out_specs, out_shape=out_shape, input_output_aliases=aliases,
        compiler_params=_params(("arbitrary",) if ns else ("parallel",)),
    )(*operands)


def _colsum(v):
    return jnp.sum(v, axis=0, keepdims=True)


def _rms_stats(xv):
    r = lax.rsqrt(jnp.mean(xv * xv, axis=-1, keepdims=True) + EPS)
    return r, xv * r


def _rms_bwd(dxhat, xhat, r):
    return r * (dxhat - xhat * jnp.mean(dxhat * xhat, axis=-1, keepdims=True))


def _gelu(v):
    k = math.sqrt(2.0 / math.pi)
    t = jnp.tanh(k * (v + 0.044715 * v * v * v))
    return 0.5 * v * (1.0 + t), t


def _gelu_grad(v, t):
    k = math.sqrt(2.0 / math.pi)
    return 0.5 * (1.0 + t) + 0.5 * v * (1.0 - t * t) * k * (1.0 + 3.0 * 0.044715 * v * v)


CONV_TS = 512
CONV_CH = 64


def _ln_fwd(yc, g, b):
    mu = jnp.mean(yc, axis=-1, keepdims=True)
    xc = yc - mu
    rstd = lax.rsqrt(jnp.mean(xc * xc, axis=-1, keepdims=True) + EPS)
    nhat = xc * rstd
    return nhat, rstd, nhat * g + b


SUBL = 8


def _shifted_copies(buf, sh, ts):
    for b in range(1, SUBL):
        sh[b - 1] = buf[pl.ds(b, ts + HALO - SUBL), :]


def _shifted(buf, sh, start):
    b = start % SUBL
    if b == 0:
        return buf[pl.ds(start, CONV_CH), :]
    return sh[b - 1, pl.ds(start - b, CONV_CH), :]


def conv_fwd(z, w32, cb, lg, lb):
    s = z.shape[0]
    ts = CONV_TS
    nt = s // ts
    hb = ts // HALO

    def body(a_ref, g_ref, ah_ref, gh_ref, w_ref, cb_ref, lg_ref, lb_ref, yc_ref, s_ref, ubuf, ush):
        i = pl.program_id(0)
        first = (i > 0).astype(F32)
        ubuf[0:HALO, :] = ah_ref[...] * _sig(gh_ref[...]) * first
        ubuf[HALO:HALO + ts, :] = a_ref[...] * _sig(g_ref[...])
        _shifted_copies(ubuf, ush, ts)
        for c0 in range(0, ts, CONV_CH):
            acc = jnp.zeros((CONV_CH, CW), F32)
            for k in range(KC):
                acc = acc + w_ref[k:k + 1, :] * _shifted(ubuf, ush, c0 + k + 2)
            yc = acc + cb_ref[...]
            yc_ref[c0:c0 + CONV_CH, :] = yc
            _, _, ln = _ln_fwd(yc, lg_ref[...], lb_ref[...])
            s_ref[c0:c0 + CONV_CH, :] = (ln * _sig(ln)).astype(s_ref.dtype)

    cur = lambda cbk: pl.BlockSpec((ts, CW), functools.partial(lambda i, q: (i, q), q=cbk))
    prev = lambda cbk: pl.BlockSpec((HALO, CW), functools.partial(lambda i, q: (jnp.maximum(i * hb - 1, 0), q), q=cbk))
    const = lambda a: pl.BlockSpec(a.shape, lambda i: (0, 0))
    return pl.pallas_call(
        body, name="conv_fwd", grid=(nt,),
        in_specs=[cur(ZB_A), cur(ZB_G), prev(ZB_A), prev(ZB_G), const(w32), const(cb), const(lg), const(lb)],
        out_specs=[pl.BlockSpec((ts, CW), lambda i: (i, 0)), pl.BlockSpec((ts, CW), lambda i: (i, 0))],
        out_shape=[jax.ShapeDtypeStruct((s, CW), F32), jax.ShapeDtypeStruct((s, CW), BF16)],
        scratch_shapes=[pltpu.VMEM((HALO + ts, CW), F32), pltpu.VMEM((SUBL - 1, ts + HALO - SUBL, CW), F32)],
        compiler_params=_params(("parallel",)),
    )(z, z, z, z, w32, cb, lg, lb)


def conv_bwd(ds, yc, z, w32, lg, lb, dz):
    s = z.shape[0]
    ts = CONV_TS
    nt = s // ts
    hb = ts // HALO
    last_hb = s // HALO - 1

    def ln_bwd(dsv, ycv, g, b):
        nhat, rstd, ln = _ln_fwd(ycv, g, b)
        sg = _sig(ln)
        dln = dsv * (sg * (1.0 + ln * (1.0 - sg)))
        dnh = dln * g
        dyc = rstd * (dnh - jnp.mean(dnh, axis=-1, keepdims=True)
                      - nhat * jnp.mean(dnh * nhat, axis=-1, keepdims=True))
        return dyc, dln, nhat

    def body(ds_ref, yc_ref, dsn_ref, ycn_ref, a_ref, g_ref, ah_ref, gh_ref, w_ref, lg_ref, lb_ref, dz_in,
             dz_ref, dlg_ref, dlb_ref, dcb_ref, dw_ref, dbuf, ubuf, dsh, ush, dwacc):
        i = pl.program_id(0)

        @pl.when(i == 0)
        def _():
            dlg_ref[...] = jnp.zeros_like(dlg_ref)
            dlb_ref[...] = jnp.zeros_like(dlb_ref)
            dcb_ref[...] = jnp.zeros_like(dcb_ref)
            dw_ref[...] = jnp.zeros_like(dw_ref)
            dwacc[...] = jnp.zeros_like(dwacc)

        lg, lb = lg_ref[...], lb_ref[...]
        dyc, dln, nhat = ln_bwd(ds_ref[...], yc_ref[...], lg, lb)
        dlg_ref[...] += _colsum(dln * nhat)
        dlb_ref[...] += _colsum(dln)
        dcb_ref[...] += _colsum(dyc)
        dbuf[0:ts, :] = dyc
        nxt = (i < nt - 1).astype(F32)
        dbuf[ts:ts + HALO, :] = ln_bwd(dsn_ref[...], ycn_ref[...], lg, lb)[0] * nxt
        first = (i > 0).astype(F32)
        ubuf[0:HALO, :] = ah_ref[...] * _sig(gh_ref[...]) * first
        ubuf[HALO:HALO + ts, :] = a_ref[...] * _sig(g_ref[...])
        _shifted_copies(dbuf, dsh, ts)
        _shifted_copies(ubuf, ush, ts)
        for c0 in range(0, ts, CONV_CH):
            du = jnp.zeros((CONV_CH, CW), F32)
            dyc_c = dbuf[c0:c0 + CONV_CH, :]
            for k in range(KC):
                du = du + w_ref[k:k + 1, :] * _shifted(dbuf, dsh, c0 + KC - 1 - k)
                prod = dyc_c * _shifted(ubuf, ush, c0 + k + 2)
                dwacc[k] += jnp.sum(prod.reshape(CONV_CH // SUBL, SUBL, CW), axis=0)
            av = a_ref[c0:c0 + CONV_CH, :]
            sg = _sig(g_ref[c0:c0 + CONV_CH, :])
            dz_ref[c0:c0 + CONV_CH, 0:CW] = (du * sg).astype(dz_ref.dtype)
            dz_ref[c0:c0 + CONV_CH, CW:2 * CW] = (du * av * sg * (1.0 - sg)).astype(dz_ref.dtype)

        @pl.when(i == nt - 1)
        def _():
            for k in range(KC):
                dw_ref[k:k + 1, :] = _colsum(dwacc[k])

    cur = lambda w, cbk: pl.BlockSpec((ts, w), functools.partial(lambda i, q: (i, q), q=cbk))
    prev = lambda cbk: pl.BlockSpec((HALO, CW), functools.partial(lambda i, q: (jnp.maximum(i * hb - 1, 0), q), q=cbk))
    nxt_spec = pl.BlockSpec((HALO, CW), lambda i: (jnp.minimum((i + 1) * hb, last_hb), 0))
    const = lambda a: pl.BlockSpec(a.shape, lambda i: (0, 0))
    acc = lambda r: pl.BlockSpec((r, CW), lambda i: (0, 0))
    return pl.pallas_call(
        body, name="conv_bwd", grid=(nt,),
        in_specs=[cur(CW, 0), cur(CW, 0), nxt_spec, nxt_spec, cur(CW, ZB_A), cur(CW, ZB_G), prev(ZB_A), prev(ZB_G),
                  const(w32), const(lg), const(lb), pl.BlockSpec(memory_space=pl.ANY)],
        out_specs=[pl.BlockSpec((ts, 2 * CW), lambda i: (i, ZB_A // 2)), acc(1), acc(1), acc(1), acc(HALO)],
        out_shape=[jax.ShapeDtypeStruct(dz.shape, dz.dtype), jax.ShapeDtypeStruct((1, CW), F32),
                   jax.ShapeDtypeStruct((1, CW), F32), jax.ShapeDtypeStruct((1, CW), F32),
                   jax.ShapeDtypeStruct((HALO, CW), F32)],
        scratch_shapes=[pltpu.VMEM((ts + HALO, CW), F32), pltpu.VMEM((HALO + ts, CW), F32)]
        + [pltpu.VMEM((SUBL - 1, ts + HALO - SUBL, CW), F32)] * 2 + [pltpu.VMEM((KC, SUBL, CW), F32)],
        input_output_aliases={11: 0},
        compiler_params=_params(("arbitrary",)),
    )(ds, yc, ds, yc, z, z, z, z, w32, lg, lb, dz)


SSM_TS = 2048
GRP = 8


def _cmul(ar, ai, br, bi):
    return ar * br - ai * bi, ar * bi + ai * br


def _scan_tables(ar, ai, reverse):
    n = ar.shape[1]
    row = lax.broadcasted_iota(jnp.int32, (GRP, n), 0)
    dist = (GRP - 1 - row) if reverse else row
    one_r = jnp.broadcast_to(ar, (GRP, n))
    one_i = jnp.broadcast_to(ai, (GRP, n))
    p2r, p2i = _cmul(one_r, one_i, one_r, one_i)
    p4r, p4i = _cmul(p2r, p2i, p2r, p2i)
    steps = []
    for sft, (pr, pi) in ((1, (one_r, one_i)), (2, (p2r, p2i)), (4, (p4r, p4i))):
        keep = dist >= sft
        steps.append((jnp.where(keep, pr, 0.0), jnp.where(keep, pi, 0.0)))
    cr, ci = one_r, one_i
    accr, acci = one_r, one_i
    for e in range(1, GRP):
        cr, ci = _cmul(cr, ci, one_r, one_i)
        accr = jnp.where(dist == e, cr, accr)
        acci = jnp.where(dist == e, ci, acci)
    return steps, (accr, acci)


def _scan_group(xr, xi, steps, carry_tab, cr, ci, reverse):
    for sft, (tr, ti) in zip((1, 2, 4), steps):
        amt = (GRP - sft) if reverse else sft
        sr = pltpu.roll(xr, amt, 0)
        si = pltpu.roll(xi, amt, 0)
        xr, xi = xr + tr * sr - ti * si, xi + tr * si + ti * sr
    pr, pi = carry_tab
    xr = xr + pr * cr - pi * ci
    xi = xi + pr * ci + pi * cr
    return xr, xi


def ssm_fwd(z, wb_re, wb_im, wc, e_re, e_im, dvec):
    s = z.shape[0]
    ts = SSM_TS
    nt = s // ts
    ucol0 = ZB_U * CW // CB

    def body(u_ref, wbr_ref, wbi_ref, wc_ref, er_ref, ei_ref, d_ref, xr_ref, xi_ref, y_ref, gl_ref, car_r, car_i):
        i = pl.program_id(1)

        @pl.when(i == 0)
        def _():
            car_r[...] = jnp.zeros_like(car_r)
            car_i[...] = jnp.zeros_like(car_i)

        u = u_ref[...]
        ub = u.astype(BF16)
        xr_ref[...] = jnp.dot(ub, wbr_ref[0], preferred_element_type=F32)
        xi_ref[...] = jnp.dot(ub, wbi_ref[0], preferred_element_type=F32)
        steps, ctab = _scan_tables(er_ref[0], ei_ref[0], False)

        def grp(r, carry):
            cr, ci = carry
            r0 = pl.multiple_of(r * GRP, GRP)
            xr, xi = _scan_group(xr_ref[pl.ds(r0, GRP), :], xi_ref[pl.ds(r0, GRP), :], steps, ctab, cr, ci, False)
            xr_ref[pl.ds(r0, GRP), :] = xr
            xi_ref[pl.ds(r0, GRP), :] = xi
            return (jnp.broadcast_to(xr[GRP - 1:GRP, :], (GRP, SB)), jnp.broadcast_to(xi[GRP - 1:GRP, :], (GRP, SB)))

        cr, ci = lax.fori_loop(0, ts // GRP, grp, (car_r[...], car_i[...]))
        car_r[...] = cr
        car_i[...] = ci
        y = (jnp.dot(xr_ref[...].astype(BF16), wc_ref[0, 0:SB, :], preferred_element_type=F32)
             + jnp.dot(xi_ref[...].astype(BF16), wc_ref[0, SB:2 * SB, :], preferred_element_type=F32)
             + d_ref[0] * u)
        y_ref[...] = y
        gl_ref[...] = _gelu(y)[0].astype(gl_ref.dtype)

    blk3 = lambda a: pl.BlockSpec((1,) + a.shape[1:], lambda j, i: (j, 0, 0))
    return pl.pallas_call(
        body, name="ssm_fwd", grid=(NBLK, nt),
        in_specs=[pl.BlockSpec((ts, CB), lambda j, i: (i, ucol0 + j)),
                  blk3(wb_re), blk3(wb_im), blk3(wc), blk3(e_re), blk3(e_im), blk3(dvec)],
        out_specs=[pl.BlockSpec((ts, SB), lambda j, i: (i, j)), pl.BlockSpec((ts, SB), lambda j, i: (i, j)),
                   pl.BlockSpec((ts, CB), lambda j, i: (i, j)), pl.BlockSpec((ts, CB), lambda j, i: (i, j))],
        out_shape=[jax.ShapeDtypeStruct((s, NST), F32), jax.ShapeDtypeStruct((s, NST), F32),
                   jax.ShapeDtypeStruct((s, SW), F32), jax.ShapeDtypeStruct((s, SW), BF16)],
        scratch_shapes=[pltpu.VMEM((GRP, SB), F32), pltpu.VMEM((GRP, SB), F32)],
        compiler_params=_params(("parallel", "arbitrary")),
    )(z, wb_re, wb_im, wc, e_re, e_im, dvec)


def ssm_bwd(dgl, ypre, z, xs_re, xs_im, wbt_re, wbt_im, wct, e_re, e_im, dvec, dz):
    s = z.shape[0]
    ts = SSM_TS
    nt = s // ts
    ucol0 = ZB_U * CW // CB
    tn_dims = (((0,), (0,)), ((), ()))

    def body(dgl_ref, y_ref, u_ref, xr_ref, xi_ref, wbtr_ref, wbti_ref, wct_ref, er_ref, ei_ref, d_ref, dz_in,
             du_ref, dd_ref, dar_ref, dai_ref, dwbr_ref, dwbi_ref, dwc_ref,
             lr_ref, li_ref, car_r, car_i, acc_r, acc_i):
        i = pl.program_id(1)

        @pl.when(i == 0)
        def _():
            for ref in (car_r, car_i, acc_r, acc_i, dd_ref, dwbr_ref, dwbi_ref, dwc_ref):
                ref[...] = jnp.zeros_like(ref)

        u = u_ref[...]
        y = y_ref[...]
        dy = dgl_ref[...] * _gelu_grad(y, _gelu(y)[1])
        dd_ref[0] += _colsum(dy * u)
        dyb = dy.astype(BF16)
        dxo = jnp.dot(dyb, wct_ref[0], preferred_element_type=F32)
        lr_ref[...] = dxo[:, 0:SB]
        li_ref[...] = dxo[:, SB:2 * SB]
        steps, ctab = _scan_tables(er_ref[0], -ei_ref[0], True)
        row = lax.broadcasted_iota(jnp.int32, (GRP, SB), 0)

        def grp(q, carry):
            cr, ci, ar, ai = carry
            r0 = pl.multiple_of((ts // GRP - 1 - q) * GRP, GRP)
            lr, li = _scan_group(lr_ref[pl.ds(r0, GRP), :], li_ref[pl.ds(r0, GRP), :], steps, ctab, cr, ci, True)
            lr_ref[pl.ds(r0, GRP), :] = lr
            li_ref[pl.ds(r0, GRP), :] = li
            nr = jnp.where(row == GRP - 1, cr, pltpu.roll(lr, GRP - 1, 0))
            ni = jnp.where(row == GRP - 1, ci, pltpu.roll(li, GRP - 1, 0))
            xr = xr_ref[pl.ds(r0, GRP), :]
            xi = xi_ref[pl.ds(r0, GRP), :]
            ar = ar + nr * xr + ni * xi
            ai = ai + ni * xr - nr * xi
            return (jnp.broadcast_to(lr[0:1, :], (GRP, SB)), jnp.broadcast_to(li[0:1, :], (GRP, SB)), ar, ai)

        cr, ci, ar, ai = lax.fori_loop(0, ts // GRP, grp, (car_r[...], car_i[...], acc_r[...], acc_i[...]))
        car_r[...] = cr
        car_i[...] = ci
        acc_r[...] = ar
        acc_i[...] = ai

        @pl.when(i == nt - 1)
        def _():
            dar_ref[0] = _colsum(ar)
            dai_ref[0] = _colsum(ai)

        lrb = lr_ref[...].astype(BF16)
        lib = li_ref[...].astype(BF16)
        du = (jnp.dot(lrb, wbtr_ref[0], preferred_element_type=F32)
              + jnp.dot(lib, wbti_ref[0], preferred_element_type=F32) + d_ref[0] * dy)
        du_ref[...] = du.astype(du_ref.dtype)
        ub = u.astype(BF16)
        dwbr_ref[0] += lax.dot_general(ub, lrb, tn_dims, preferred_element_type=F32)
        dwbi_ref[0] += lax.dot_general(ub, lib, tn_dims, preferred_element_type=F32)
        dwc_ref[0, 0:SB, :] += lax.dot_general(xr_ref[...].astype(BF16), dyb, tn_dims, preferred_element_type=F32)
        dwc_ref[0, SB:2 * SB, :] += lax.dot_general(xi_ref[...].astype(BF16), dyb, tn_dims, preferred_element_type=F32)

    rev = lambda i: nt - 1 - i
    blk3 = lambda a: pl.BlockSpec((1,) + a.shape[1:], lambda j, i: (j, 0, 0))
    acc3 = lambda r, c: pl.BlockSpec((1, r, c), lambda j, i: (j, 0, 0))
    return pl.pallas_call(
        body, name="ssm_bwd", grid=(NBLK, nt),
        in_specs=[pl.BlockSpec((ts, CB), lambda j, i: (rev(i), j)), pl.BlockSpec((ts, CB), lambda j, i: (rev(i), j)),
                  pl.BlockSpec((ts, CB), lambda j, i: (rev(i), ucol0 + j)),
                  pl.BlockSpec((ts, SB), lambda j, i: (rev(i), j)), pl.BlockSpec((ts, SB), lambda j, i: (rev(i), j)),
                  blk3(wbt_re), blk3(wbt_im), blk3(wct), blk3(e_re), blk3(e_im), blk3(dvec),
                  pl.BlockSpec(memory_space=pl.ANY)],
        out_specs=[pl.BlockSpec((ts, CB), lambda j, i: (rev(i), ucol0 + j)),
                   acc3(1, CB), acc3(1, SB), acc3(1, SB), acc3(CB, SB), acc3(CB, SB), acc3(2 * SB, CB)],
        out_shape=[jax.ShapeDtypeStruct(dz.shape, dz.dtype),
                   jax.ShapeDtypeStruct((NBLK, 1, CB), F32),
                   jax.ShapeDtypeStruct((NBLK, 1, SB), F32), jax.ShapeDtypeStruct((NBLK, 1, SB), F32),
                   jax.ShapeDtypeStruct((NBLK, CB, SB), F32), jax.ShapeDtypeStruct((NBLK, CB, SB), F32),
                   jax.ShapeDtypeStruct((NBLK, 2 * SB, CB), F32)],
        scratch_shapes=[pltpu.VMEM((ts, SB), F32), pltpu.VMEM((ts, SB), F32)] + [pltpu.VMEM((GRP, SB), F32)] * 4,
        input_output_aliases={11: 0},
        compiler_params=_params(("parallel", "arbitrary")),
    )(dgl, ypre, z, xs_re, xs_im, wbt_re, wbt_im, wct, e_re, e_im, dvec, dz)


def _disc(a_re, a_im, log_dt, b_re, b_im, expand):
    dt = jnp.dot(expand, jnp.exp(log_dt), preferred_element_type=F32, precision=lax.Precision.HIGHEST)
    mag = jnp.exp(dt * a_re)
    e_re, e_im = mag * jnp.cos(dt * a_im), mag * jnp.sin(dt * a_im)
    n_re, n_im = e_re - 1.0, e_im
    den = a_re * a_re + a_im * a_im
    q_re = (n_re * a_re + n_im * a_im) / den
    q_im = (n_im * a_re - n_re * a_im) / den
    return e_re, e_im, q_re * b_re - q_im * b_im, q_re * b_im + q_im * b_re


def _whole(a):
    return pl.BlockSpec(a.shape, functools.partial(lambda n: (0,) * n, n=a.ndim))


def disc_fwd(a_re, a_im, log_dt, b_re, b_im, expand):
    def body(ar, ai, ld, br, bi, ex, er_o, ei_o, bbr_o, bbi_o):
        er, ei, bbr, bbi = _disc(ar[...], ai[...], ld[...], br[...], bi[...], ex[...])
        er_o[...] = er
        ei_o[...] = ei
        bbr_o[...] = bbr
        bbi_o[...] = bbi

    ins = (a_re, a_im, log_dt, b_re, b_im, expand)
    outs = [jax.ShapeDtypeStruct(a_re.shape, F32)] * 2 + [jax.ShapeDtypeStruct(b_re.shape, F32)] * 2
    return pl.pallas_call(body, name="disc_fwd", in_specs=[_whole(a) for a in ins],
                          out_specs=[_whole(o) for o in outs], out_shape=outs, compiler_params=_params())(*ins)


def disc_bwd(a_re, a_im, log_dt, b_re, b_im, expand, de_re, de_im, dbb_re, dbb_im):
    def body(ar, ai, ld, br, bi, ex, der, dei, dbr, dbi, o_ar, o_ai, o_ld, o_br, o_bi):
        exv = ex[...]
        _, vjp = jax.vjp(lambda *p: _disc(*p, exv), ar[...], ai[...], ld[...], br[...], bi[...])
        g = vjp((der[...], dei[...], dbr[...], dbi[...]))
        for o, v in zip((o_ar, o_ai, o_ld, o_br, o_bi), g):
            o[...] = v

    ins = (a_re, a_im, log_dt, b_re, b_im, expand, de_re, de_im, dbb_re, dbb_im)
    outs = [jax.ShapeDtypeStruct(a.shape, F32) for a in (a_re, a_im, log_dt, b_re, b_im)]
    return pl.pallas_call(body, name="disc_bwd", in_specs=[_whole(a) for a in ins],
                          out_specs=[_whole(o) for o in outs], out_shape=outs, compiler_params=_params())(*ins)


def mod_fwd(c_all, w_ada, b_cols):
    def body(c_ref, w_ref, b_ref, act_ref, mod_ref):
        cv = c_ref[...]
        act = cv * _sig(cv)
        act_ref[...] = act
        mod_ref[...] = jnp.dot(act, w_ref[...], preferred_element_type=F32, precision=lax.Precision.HIGHEST) + b_ref[...]

    ins = (c_all, w_ada, b_cols)
    outs = [jax.ShapeDtypeStruct(c_all.shape, F32), jax.ShapeDtypeStruct((NDEV, w_ada.shape[1]), F32)]
    return pl.pallas_call(body, name="mod_fwd", in_specs=[_whole(a) for a in ins],
                          out_specs=[_whole(o) for o in outs], out_shape=outs, compiler_params=_params())(*ins)


def ada_grad(act_all, dmod_cols):
    def body(a_ref, d_ref, o_ref):
        o_ref[...] = lax.dot_general(a_ref[...], d_ref[...], (((0,), (0,)), ((), ())),
                                     preferred_element_type=F32, precision=lax.Precision.HIGHEST)

    out = jax.ShapeDtypeStruct((act_all.shape[1], dmod_cols.shape[1]), F32)
    return pl.pallas_call(body, name="ada_grad", in_specs=[_whole(act_all), _whole(dmod_cols)],
                          out_specs=_whole(out), out_shape=out, compiler_params=_params())(act_all, dmod_cols)


def _adam_math(w, g, m, v):
    m2 = ADAM_B1 * m + (1.0 - ADAM_B1) * g
    v2 = ADAM_B2 * v + (1.0 - ADAM_B2) * (g * g)
    m_hat = m2 / (1.0 - ADAM_B1 ** ADAM_STEP)
    v_hat = v2 / (1.0 - ADAM_B2 ** ADAM_STEP)
    delta = -ADAM_LR * (m_hat / (jnp.sqrt(v_hat) + ADAM_EPS) + ADAM_WD * w)
    return delta, m2, v2


def adam(name, w, g, m, v):
    r, c = w.shape
    tr = max(t for t in range(8, min(r, 512) + 1, 8) if r % t == 0)

    def body(w_ref, g_ref, m_ref, v_ref, d_o, m_o, v_o):
        d, m2, v2 = _adam_math(w_ref[...], g_ref[...], m_ref[...], v_ref[...])
        d_o[...] = d
        m_o[...] = m2
        v_o[...] = v2

    spec = pl.BlockSpec((tr, c), lambda i: (i, 0))
    out = jax.ShapeDtypeStruct((r, c), F32)
    return pl.pallas_call(body, name=name, grid=(r // tr,), in_specs=[spec] * 4, out_specs=[spec] * 3,
                          out_shape=[out] * 3, compiler_params=_params(("parallel",)))(w, g, m, v)


def adam_many(name, ws, gs, ms, vs):
    n = len(ws)

    def body(*refs):
        ins, outs = refs[:4 * n], refs[4 * n:]
        for q in range(n):
            d, m2, v2 = _adam_math(ins[q][...], ins[n + q][...], ins[2 * n + q][...], ins[3 * n + q][...])
            outs[q][...] = d
            outs[n + q][...] = m2
            outs[2 * n + q][...] = v2

    operands = list(ws) + list(gs) + list(ms) + list(vs)
    outs = [jax.ShapeDtypeStruct(w.shape, F32) for w in ws] * 3
    return pl.pallas_call(body, name=name, in_specs=[_whole(a) for a in operands],
                          out_specs=[_whole(o) for o in outs], out_shape=outs, compiler_params=_params())(*operands)


def _rows_tile(r, most):
    best = None
    for t in range(16, min(r, most) + 1, 16):
        if r % t == 0:
            best = t
    assert best is not None, r
    return best


def sum_slots(name, slots, out_dtype=F32):
    n, r, c = slots.shape
    tr = _rows_tile(r, max(16, (2 * 1024 * 1024) // (n * c)))

    def body(s_ref, o_ref):
        acc = s_ref[0].astype(F32)
        for q in range(1, n):
            acc = acc + s_ref[q].astype(F32)
        o_ref[...] = acc.astype(o_ref.dtype)

    return pl.pallas_call(body, name=name, grid=(r // tr,),
                          in_specs=[pl.BlockSpec((n, tr, c), lambda i: (0, i, 0))],
                          out_specs=pl.BlockSpec((tr, c), lambda i: (i, 0)),
                          out_shape=jax.ShapeDtypeStruct((r, c), out_dtype), compiler_params=_params(("parallel",)))(slots)


HBM_SPEC = pl.BlockSpec(memory_space=pltpu.HBM)


def _coords():
    return lax.axis_index("x"), lax.axis_index("y"), lax.axis_index("c")


def _linear(x, y, c):
    return 4 * x + 2 * y + c


def all_gather(name, shards):
    nq = len(shards)

    def body(*refs):
        xs, outs = refs[:nq], refs[nq:2 * nq]
        send_sems, recv_sems, local_sems = refs[2 * nq:2 * nq + 3]
        bufs = refs[2 * nq + 3:]
        x, y, cc = _coords()
        me, sibling = (x, y, cc), (x, y, 1 - cc)
        chips = [(1 - x, y), (x, 1 - y), (1 - x, 1 - y)]

        def slot(q, px, py, pc):
            return outs[q].at[_linear(px, py, pc)]

        def copy(q, k, block, to, src=None):
            return pltpu.make_async_remote_copy(
                src_ref=slot(q, *block) if src is None else src, dst_ref=slot(q, *block),
                send_sem=send_sems.at[7 * q + k], recv_sem=recv_sems.at[7 * q + k], device_id=to, device_id_type=MESH)

        loads = [pltpu.make_async_copy(xs[q], bufs[q], local_sems.at[q]) for q in range(nq)]
        for cp in loads:
            cp.start()
        for cp in loads:
            cp.wait()
        mine = [pltpu.make_async_copy(bufs[q], slot(q, *me), local_sems.at[q]) for q in range(nq)]
        first = []
        for q in range(nq):
            first.append(copy(q, 0, me, sibling, src=bufs[q]))
            first += [copy(q, 1 + j, me, (*chip, cc), src=bufs[q]) for j, chip in enumerate(chips)]
        for cp in mine + first:
            cp.start()
        passed = []
        for q in range(nq):
            for j, chip in enumerate(chips):
                copy(q, 1 + j, (*chip, cc), me).wait_recv()
                passed.append(copy(q, 4 + j, (*chip, cc), sibling))
                passed[-1].start()
        for q in range(nq):
            copy(q, 0, sibling, me).wait_recv()
            for j, chip in enumerate(chips):
                copy(q, 4 + j, (*chip, 1 - cc), me).wait_recv()
        for cp in first + passed:
            cp.wait_send()
        for cp in mine:
            cp.wait()

    return pl.pallas_call(
        body, name=name, in_specs=[HBM_SPEC] * nq, out_specs=[HBM_SPEC] * nq,
        out_shape=[jax.ShapeDtypeStruct((NDEV,) + s.shape, s.dtype) for s in shards],
        scratch_shapes=[pltpu.SemaphoreType.DMA((7 * nq,)), pltpu.SemaphoreType.DMA((7 * nq,)),
                        pltpu.SemaphoreType.DMA((nq,))] + [pltpu.VMEM(s.shape, s.dtype) for s in shards],
    )(*shards)


NCHIP = 4


SEM_SPEC = pl.BlockSpec(memory_space=pltpu.SEMAPHORE)
EFFECT = pltpu.SideEffectType.DATAFLOW_SIDE_EFFECTING


def _peer(x, y, cc, k):
    fx, fy, fc = (k >> 2) & 1, (k >> 1) & 1, k & 1
    return (x + fx - 2 * fx * x, y + fy - 2 * fy * y, cc + fc - 2 * fc * cc)


def gather_plan(srcs, lands, coords):
    x, y, cc = coords
    me = _linear(x, y, cc)
    return [(s, l.at[me], _peer(x, y, cc, k)) for s, l in zip(srcs, lands) for k in range(1, NDEV)]


def near_plan(srcs, lands, coords):
    x, y, cc = coords
    me = _linear(x, y, cc)
    peers = [(x, y, 1 - cc)] + [_peer(x, y, cc, 2 * k) for k in range(1, NCHIP)]
    return [(s, l.at[me], p) for s, l in zip(srcs, lands) for p in peers]


def pass_on_plan(srcs, lands, coords):
    x, y, cc = coords
    out = []
    for l in srcs:
        for k in range(1, NCHIP):
            px, py, _ = _peer(x, y, cc, 2 * k)
            slot = _linear(px, py, cc)
            out.append((l.at[slot], l.at[slot], (x, y, 1 - cc)))
    return out


def pair_plan(srcs, lands, coords):
    x, y, cc = coords
    return [(s.at[2 * chip + 1 - cc], l.at[chip], (x, y, 1 - cc)) for s, l in zip(srcs, lands) for chip in range(NCHIP)]


def chip_plan(srcs, lands, coords):
    x, y, cc = coords
    out = []
    for s, l in zip(srcs, lands):
        for k in range(1, NCHIP):
            px, py, _ = _peer(x, y, cc, 2 * k)
            out.append((s.at[2 * px + py], l.at[k - 1], (px, py, cc)))
    return out


def _remote(copy, i, send_sems, recv_sems):
    src, dst, dev = copy
    return pltpu.make_async_remote_copy(src_ref=src, dst_ref=dst, send_sem=send_sems.at[i], recv_sem=recv_sems.at[i],
                                        device_id=dev, device_id_type=MESH)


def exchange_start(name, plan, ncopy, srcs, land_shapes, deps=()):
    ns, nl, nd = len(srcs), len(land_shapes), len(deps)

    def body(*refs):
        s, l = refs[:ns], refs[ns:ns + nl]
        send_sems, recv_sems = refs[ns + nl + nd], refs[ns + nl + nd + 1]
        token = refs[-1]
        for i, cp in enumerate(plan(s, l, _coords())):
            _remote(cp, i, send_sems, recv_sems).start()
        token[...] = jnp.zeros_like(token)

    hbm = lambda a: pltpu.with_memory_space_constraint(a, pltpu.HBM)
    lands = [lax.empty(shp, dt) for shp, dt in land_shapes]
    thru = [pltpu.HBM(a.shape, a.dtype) for a in list(srcs) + lands]
    outs = pl.pallas_call(
        body, name=name,
        in_specs=[HBM_SPEC] * (ns + nl) + [ANY_SPEC] * nd,
        out_specs=(SEM_SPEC, SEM_SPEC, *[HBM_SPEC] * (ns + nl), pl.BlockSpec(memory_space=pltpu.VMEM)),
        out_shape=(pltpu.SemaphoreType.DMA((ncopy,)), pltpu.SemaphoreType.DMA((ncopy,)), *thru,
                   jax.ShapeDtypeStruct((8, LANE), F32)),
        input_output_aliases={i: 2 + i for i in range(ns + nl)},
        compiler_params=pltpu.CompilerParams(has_side_effects=EFFECT),
    )(*[hbm(a) for a in srcs], *[hbm(a) for a in lands], *deps)
    return outs[0], outs[1], list(outs[2:2 + ns]), list(outs[2 + ns:2 + ns + nl]), outs[-1]


def exchange_wait(name, plan, started, after, place_own=False):
    send_sems, recv_sems, srcs, lands, _ = started
    ns, nl = len(srcs), len(lands)

    def body(*refs):
        s, l = refs[:ns], refs[ns:ns + nl]
        send_sems, recv_sems = refs[ns + nl], refs[ns + nl + 1]
        l_out = refs[2 * ns + nl + 3:2 * ns + 2 * nl + 3]
        scratch = refs[2 * ns + 2 * nl + 3:]
        copies = [_remote(cp, i, send_sems, recv_sems) for i, cp in enumerate(plan(s, l, _coords()))]
        if place_own:
            me = _linear(*_coords())
            local_sems, bufs = scratch[0], scratch[1:]
            loads = [pltpu.make_async_copy(s[q], bufs[q], local_sems.at[q]) for q in range(ns)]
            for cp in loads:
                cp.start()
            for cp in loads:
                cp.wait()
            stores = [pltpu.make_async_copy(bufs[q], l_out[q].at[me], local_sems.at[q]) for q in range(ns)]
            for cp in stores:
                cp.start()
        for cp in copies:
            cp.wait_recv()
        for cp in copies:
            cp.wait_send()
        if place_own:
            for cp in stores:
                cp.wait()

    scratch_shapes = []
    if place_own:
        scratch_shapes = [pltpu.SemaphoreType.DMA((ns,))] + [pltpu.VMEM(a.shape, a.dtype) for a in srcs]
    outs = pl.pallas_call(
        body, name=name,
        in_specs=[HBM_SPEC] * (ns + nl) + [SEM_SPEC, SEM_SPEC, ANY_SPEC],
        out_specs=[HBM_SPEC] * (ns + nl),
        out_shape=[pltpu.HBM(a.shape, a.dtype) for a in srcs + lands],
        input_output_aliases={i: i for i in range(ns + nl)},
        scratch_shapes=scratch_shapes,
        compiler_params=pltpu.CompilerParams(has_side_effects=EFFECT),
    )(*srcs, *lands, send_sems, recv_sems, after)
    return list(outs[:ns]), list(outs[ns:])


def pair_sum(name, g, recv):
    _, r, c = g.shape
    tr = _rows_tile(r, 512)

    def body(g_ref, r_ref, o_ref):
        own = jnp.where(lax.axis_index("c") == 0, g_ref[0, 0], g_ref[0, 1])
        o_ref[0] = (own.astype(F32) + r_ref[0].astype(F32)).astype(o_ref.dtype)

    return pl.pallas_call(
        body, name=name, grid=(NCHIP, r // tr),
        in_specs=[pl.BlockSpec((1, 2, tr, c), lambda k, i: (k, 0, i, 0)), pl.BlockSpec((1, tr, c), lambda k, i: (k, i, 0))],
        out_specs=pl.BlockSpec((1, tr, c), lambda k, i: (k, i, 0)),
        out_shape=jax.ShapeDtypeStruct((NCHIP, r, c), g.dtype), compiler_params=_params(("parallel", "parallel")),
    )(g.reshape(NCHIP, 2, r, c), recv)


def chip_sum_adam(name, partial, recv, w, m, v):
    _, r, c = partial.shape
    tr = _rows_tile(r, 512)

    def body(p_ref, r_ref, w_ref, m_ref, v_ref, g_o, d_o, m_o, v_o):
        chip = 2 * lax.axis_index("x") + lax.axis_index("y")
        own = p_ref[0]
        for k in range(1, NCHIP):
            own = jnp.where(chip == k, p_ref[k], own)
        g = own.astype(F32)
        for k in range(NCHIP - 1):
            g = g + r_ref[k].astype(F32)
        d, m2, v2 = _adam_math(w_ref[...], g, m_ref[...], v_ref[...])
        g_o[...] = g
        d_o[...] = d
        m_o[...] = m2
        v_o[...] = v2

    spec = pl.BlockSpec((tr, c), lambda i: (i, 0))
    out = jax.ShapeDtypeStruct((r, c), F32)
    return pl.pallas_call(
        body, name=name, grid=(r // tr,),
        in_specs=[pl.BlockSpec((NCHIP, tr, c), lambda i: (0, i, 0)), pl.BlockSpec((NCHIP - 1, tr, c), lambda i: (0, i, 0)),
                  spec, spec, spec],
        out_specs=[spec] * 4, out_shape=[out] * 4, compiler_params=_params(("parallel",)),
    )(partial, recv, w, m, v)


def _block_diag(w, rows_per, cols_per):
    w = w.reshape(NBLK, 8, rows_per, cols_per)
    eye = jnp.eye(8, dtype=w.dtype)
    out = w[:, :, :, None, :] * eye[None, :, None, :, None]
    return out.reshape(NBLK, 8 * rows_per, 8 * cols_per)


def _diag_blocks(wd, rows_per, cols_per):
    wd = wd.reshape(NBLK, 8, rows_per, 8, cols_per)
    idx = jnp.arange(8)
    return wd[:, idx, :, idx, :].transpose(1, 0, 2, 3).reshape(NG, rows_per, cols_per)


def _pad_rows(v, mult):
    n = v.shape[0]
    return jnp.pad(v, (0, (-n) % mult))


def kernel(x, c, w_ada, b_ada, norm1_g, w_in, conv_w, conv_b, conv_ln_g, conv_ln_b, conv_proj, ssm_a_re, ssm_a_im, ssm_b_re, ssm_b_im, ssm_c_re, ssm_c_im, ssm_d, ssm_log_dt, ssm_glu, w_out, norm2_g, w_ffn_in, w_ffn_out, final_g, loss_target, m_w_ada, m_b_ada, m_norm1_g, m_w_in, m_conv_w, m_conv_b, m_conv_ln_g, m_conv_ln_b, m_conv_proj, m_ssm_a_re, m_ssm_a_im, m_ssm_b_re, m_ssm_b_im, m_ssm_c_re, m_ssm_c_im, m_ssm_d, m_ssm_log_dt, m_ssm_glu, m_w_out, m_norm2_g, m_w_ffn_in, m_w_ffn_out, m_final_g, v_w_ada, v_b_ada, v_norm1_g, v_w_in, v_conv_w, v_conv_b, v_conv_ln_g, v_conv_ln_b, v_conv_proj, v_ssm_a_re, v_ssm_a_im, v_ssm_b_re, v_ssm_b_im, v_ssm_c_re, v_ssm_c_im, v_ssm_d, v_ssm_log_dt, v_ssm_glu, v_w_out, v_norm2_g, v_w_ffn_in, v_w_ffn_out, v_final_g):
    me = _linear(*_coords())
    xs = x[0]
    tgt = loss_target[0]

    flat = lambda g: g.reshape(NDEV * g.shape[1], g.shape[2])
    w_in_s = w_in[0].T.astype(BF16)
    mids = [p.astype(BF16) for p in (conv_proj[0].T, ssm_glu[0].T, w_out[0])]
    ffns = [p.astype(BF16) for p in (w_ffn_in[0].T, w_ffn_out[0])]
    zone = lambda p: ((NDEV,) + p.shape, p.dtype)
    c_all, cw_g = all_gather("gather_c_conv_w", [c, conv_w[0]])
    in_go = exchange_start("gather_in_start", near_plan, NCHIP, [w_in_s], [zone(w_in_s)], deps=[c_all])
    mids_go = exchange_start("gather_mid_start", gather_plan, 7 * len(mids), mids, [zone(p) for p in mids],
                             deps=[in_go[4]])
    ffns_go = exchange_start("gather_ffn_start", gather_plan, 7 * len(ffns), ffns, [zone(p) for p in ffns],
                             deps=[mids_go[4]])

    ncol = w_ada.shape[2]
    c_all = c_all.reshape(NDEV, D)
    b_cols = lax.dynamic_slice_in_dim(b_ada, me * ncol, ncol, axis=1)
    act_all, mod_cols = mod_fwd(c_all, w_ada[0], b_cols)
    (mod_all,) = all_gather("gather_mod", [mod_cols])
    mod = lax.dynamic_index_in_dim(mod_all, me, axis=1, keepdims=False).reshape(NMOD, D)
    sh1, sc1, g1, sh2, sc2, g2 = [mod[q:q + 1] for q in range(NMOD)]

    expand = jnp.repeat(jnp.eye(NG, dtype=F32), NP, axis=0)
    a_re_c, a_im_c = ssm_a_re.reshape(NST, 1), ssm_a_im.reshape(NST, 1)
    ldt_c = ssm_log_dt.reshape(NG, 1)
    b_re_r, b_im_r = ssm_b_re.reshape(NST, GH), ssm_b_im.reshape(NST, GH)
    e_re, e_im, bb_re, bb_im = disc_fwd(a_re_c, a_im_c, ldt_c, b_re_r, b_im_r, expand)
    e_re_b, e_im_b = e_re.reshape(NBLK, 1, SB), e_im.reshape(NBLK, 1, SB)
    bb_re_g, bb_im_g = bb_re.reshape(NG, NP, GH), bb_im.reshape(NG, NP, GH)
    wbt_re = _block_diag(bb_re_g, NP, GH)
    wbt_im = _block_diag(bb_im_g, NP, GH)
    wb_re, wb_im = wbt_re.transpose(0, 2, 1), wbt_im.transpose(0, 2, 1)
    wct = jnp.concatenate([_block_diag(ssm_c_re[0], GH, NP), -_block_diag(ssm_c_im[0], GH, NP)], axis=2)
    wc = wct.transpose(0, 2, 1)
    to_b = lambda a: a.astype(BF16)
    dvec = ssm_d.reshape(NBLK, 1, CB)

    n1g = norm1_g

    def f_norm1(xv, g, sc, sh):
        _, xh = _rms_stats(xv)
        return [xh * g * (1.0 + sc) + sh], []

    (h1,) = rowwise("norm1", f_norm1, [xs], [n1g, sc1, sh1], [(D, BF16)], [], 512, deps=[ffns_go[4]])
    _, (w_in_land,) = exchange_wait("gather_in_wait", near_plan, in_go, h1, place_own=True)
    pass_go = exchange_start("gather_in_pass_start", pass_on_plan, NCHIP - 1, [w_in_land], [])
    (w_in_g,), _ = exchange_wait("gather_in_pass_wait", pass_on_plan, pass_go, pass_go[4])
    w_in_t = flat(w_in_g)
    z = mm("mm_in", h1, w_in_t, "nt", tiles=(2048, CW, 1024), b_rot=Z_ROT)

    conv_w_full = cw_g.transpose(1, 0, 2).reshape(KC, CW)
    w32 = jnp.pad(conv_w_full, ((0, HALO - KC), (0, 0)))
    yc, s_act = conv_fwd(z, w32, conv_b, conv_ln_g, conv_ln_b)
    conv_proj_t, ssm_glu_t, w_out_f = [
        flat(g) for g in exchange_wait("gather_mid_wait", gather_plan, mids_go, s_act, place_own=True)[1]]
    y_conv = mm("mm_conv_proj", s_act, conv_proj_t, "nt")

    xs_re, xs_im, ypre, gl = ssm_fwd(z, to_b(wb_re), to_b(wb_im), to_b(wc), e_re_b, e_im_b, dvec)
    n_mrg = D // MRG_BLK

    pair_of = lambda t, n: t // 2 + (t % 2) * n

    def ep_merge(accs, yc_v, gates):
        za, zb = accs
        glc, gls = gates[:, 0:MRG_BLK], gates[:, MRG_BLK:2 * MRG_BLK]
        return [_sig(glc) * yc_v + _sig(gls) * (za * _sig(zb)), jnp.concatenate([za, zb], axis=1)]

    merged, z2_pair = mm_ep("mm_ssm_glu", gl, ssm_glu_t, 2, lambda j, q: j + q * n_mrg, ep_merge,
                            [(y_conv, 1, 0), (z, 2, 0)], [(D, BF16, 1), (2 * D, BF16, 2)], (512, MRG_BLK, SW))
    row_tiles = lambda bk: (512, D, bk)
    whole = lambda j, q: j

    def ep_norm2(accs, xv, g1v, g, sc, sh):
        (o1v,) = accs
        x1v = xv + g1v * o1v
        _, xh = _rms_stats(x1v)
        return [x1v, xh * g * (1.0 + sc) + sh, o1v]

    x1, h2, o1 = mm_ep("mm_out", merged, w_out_f, 1, whole, ep_norm2, [(xs, 1, 0)],
                       [(D, F32, 1), (D, BF16, 1), (D, BF16, 1)], row_tiles(D), b_kn=True,
                       consts=[g1, norm2_g, sc2, sh2])
    w_ffn_in_t, w_ffn_out_f = [
        flat(g) for g in exchange_wait("gather_ffn_wait", gather_plan, ffns_go, h2, place_own=True)[1]]
    ffn_tiles = (512, FFN_BLK, 1024)
    n_ffn_blk = FH // FFN_BLK
    pair_map = lambda t: t // 2 + (t % 2) * n_ffn_blk

    def ep_swiglu(accs):
        fg, fu = accs
        return [fg * _sig(fg) * fu, jnp.concatenate([fg, fu], axis=1)]

    act, f_pair = mm_ep("mm_ffn_in", h2, w_ffn_in_t, 2, lambda j, q: j + q * n_ffn_blk, ep_swiglu, [],
                        [(FH, BF16, 1), (2 * FH, BF16, 2)], ffn_tiles)
    fg_row = final_g.reshape(1, D)

    def ep_final(accs, x1v, tv, g2v, fg):
        (o2v,) = accs
        x2v = x1v + g2v * o2v
        r, xh = _rms_stats(x2v)
        yv = xh * fg
        err = yv - tv
        loss = jnp.sum(_colsum(err * err), axis=1, keepdims=True) * (0.5 / D)
        dy = err * (1.0 / D)
        dx2 = _rms_bwd(dy * fg, xh, r)
        return ([dx2, g2v * dx2],
                [jnp.broadcast_to(loss, (1, LANE)), _colsum(dy * xh), _colsum(dx2 * o2v)])

    dx2, do2, loss_l, d_final_g, d_g2 = mm_ep_pipe(
        "mm_ffn_out", act, w_ffn_out_f, 1, whole, ep_final, [(x1, 1, 0), (tgt, 1, 0)],
        [(D, F32, 1), (D, BF16, 1)], row_tiles(FH), b_kn=True, consts=[g2, fg_row], sums=[LANE, D, D])

    g_ffn_out = mm("mm_g_ffn_out", act, do2, "tn", BF16, tiles=(FFN_BLK, 1024, 1024))

    def ep_dswiglu(accs, fp):
        (da,) = accs
        fg, fu = fp[:, 0:FFN_BLK].astype(F32), fp[:, FFN_BLK:2 * FFN_BLK].astype(F32)
        sg = _sig(fg)
        return [jnp.concatenate([da * fu * (sg * (1.0 + fg * (1.0 - sg))), da * (fg * sg)], axis=1)]

    (df,) = mm_ep("mm_dact", do2, w_ffn_out_f, 1, lambda j, q: j, ep_dswiglu, [(f_pair, 2, 0)],
                  [(2 * FH, BF16, 2)], ffn_tiles)
    g_ffn_in_t = mm("mm_g_ffn_in", df, h2, "tn", BF16, tiles=(FFN_BLK, 1024, 1024), o_rot=pair_map)

    def pair_go(tag, grads_t, deps=()):
        srcs = [g.reshape(NDEV, -1, D) for g in grads_t]
        return exchange_start("pair_" + tag + "_start", pair_plan, NCHIP * len(srcs), srcs,
                              [((NCHIP,) + s.shape[1:], s.dtype) for s in srcs], deps)

    def chip_go(tag, names, pair_started, after):
        own, from_sibling = exchange_wait("pair_" + tag + "_wait", pair_plan, pair_started, after)
        partials = [pair_sum("pair_sum_" + n, g, r) for n, g, r in zip(names, own, from_sibling)]
        return exchange_start("chip_" + tag + "_start", chip_plan, (NCHIP - 1) * len(partials), partials,
                              [((NCHIP - 1,) + p.shape[1:], p.dtype) for p in partials])

    def chip_done(tag, chip_started, after):
        partials, from_chips = exchange_wait("chip_" + tag + "_wait", chip_plan, chip_started, after)
        return list(zip(partials, from_chips))

    pair_ffn = pair_go("ffn", [g_ffn_out, g_ffn_in_t])

    dh2 = mm("mm_dh2", df, w_ffn_in_t, "nn", BF16, tiles=(1024, 1024, FFN_BLK), b_rot=pair_map,
             deps=[pair_ffn[4]])

    def f_dnorm2(dh, x1v, dx2v, o1v, g, sc, g1v):
        dh, o1v = dh.astype(F32), o1v.astype(F32)
        r, xh = _rms_stats(x1v)
        dxh = dh * (1.0 + sc) * g
        dx1 = dx2v + _rms_bwd(dxh, xh, r)
        return ([dx1, g1v * dx1],
                [_colsum(dh * xh * g), _colsum(dh), _colsum(dh * (1.0 + sc) * xh), _colsum(dx1 * o1v)])

    dx1, do1, d_sc2, d_sh2, d_n2g, d_g1 = rowwise(
        "dnorm2", f_dnorm2, [dh2, x1, dx2, o1], [norm2_g, sc2, g1], [(D, F32), (D, BF16)], [D, D, D, D], 512)

    g_out = mm("mm_g_out", merged, do1, "tn", BF16)
    chip_ffn = chip_go("ffn", ("w_ffn_out", "w_ffn_in"), pair_ffn, g_out)

    def ep_dmerge(accs, yc_v, z2p, gates):
        (dm,) = accs
        za, zb = z2p[:, 0:MRG_BLK].astype(F32), z2p[:, MRG_BLK:2 * MRG_BLK].astype(F32)
        sc_, ss_, sb_ = _sig(gates[:, 0:MRG_BLK]), _sig(gates[:, MRG_BLK:2 * MRG_BLK]), _sig(zb)
        dys = dm * ss_
        dz2 = jnp.concatenate([dys * sb_, dys * za * sb_ * (1.0 - sb_)], axis=1)
        dgates = jnp.concatenate([dm * yc_v * sc_ * (1.0 - sc_), dm * (za * sb_) * ss_ * (1.0 - ss_)], axis=1)
        return [dm * sc_, dz2, dgates]

    dyconv, dz2, dz = mm_ep("mm_dmerged", do1, w_out_f, 1, lambda j, q: j, ep_dmerge,
                            [(y_conv, 1, 0), (z2_pair, 2, 0), (z, 2, 0)],
                            [(D, BF16, 1), (2 * D, BF16, 2), (ZW, BF16, 2)], (512, MRG_BLK, 1024), deps=[chip_ffn[4]])

    g_conv_proj_t = mm("mm_g_conv_proj", dyconv, s_act, "tn", BF16)
    mrg_map = lambda t: pair_of(t, n_mrg)
    dgl = mm("mm_dgl", dz2, ssm_glu_t, "nn", tiles=(2048, SW, MRG_BLK), b_rot=mrg_map)
    g_ssm_glu_t = mm("mm_g_ssm_glu", dz2, gl, "tn", BF16, tiles=(MRG_BLK, SW, 1024), o_rot=mrg_map)
    pair_mid = pair_go("mid", [g_out, g_conv_proj_t, g_ssm_glu_t])
    ds = mm("mm_ds", dyconv, conv_proj_t, "nn", tiles=(2048, CW, 1024), deps=[pair_mid[4]])
    dz, d_lng, d_lnb, d_cb, d_cw32 = conv_bwd(ds, yc, z, w32, conv_ln_g, conv_ln_b, dz)
    dz, d_d, d_ar, d_ai, d_wb_re, d_wb_im, d_wc = ssm_bwd(
        dgl, ypre, z, xs_re, xs_im, to_b(wbt_re), to_b(wbt_im), to_b(wct), e_re_b, e_im_b, dvec, dz)
    chip_mid = chip_go("mid", ("w_out", "conv_proj", "ssm_glu"), pair_mid, dz)

    d_bb_re = _diag_blocks(d_wb_re.transpose(0, 2, 1), NP, GH).reshape(NST, GH)
    d_bb_im = _diag_blocks(d_wb_im.transpose(0, 2, 1), NP, GH).reshape(NST, GH)
    d_wct = d_wc.transpose(0, 2, 1)
    d_c_re = _diag_blocks(d_wct[:, :, 0:SB], GH, NP)
    d_c_im = -_diag_blocks(d_wct[:, :, SB:2 * SB], GH, NP)
    d_a_re, d_a_im, d_ldt, d_b_re, d_b_im = disc_bwd(
        a_re_c, a_im_c, ldt_c, b_re_r, b_im_r, expand, d_ar.reshape(NST, 1), d_ai.reshape(NST, 1), d_bb_re, d_bb_im)

    small_local = [jnp.concatenate([d_g1, d_sh2, d_sc2, d_g2], axis=1).reshape(-1), d_cw32[0:KC].reshape(-1),
                   d_cb.reshape(-1), d_lng.reshape(-1), d_lnb.reshape(-1), d_a_re.reshape(-1), d_a_im.reshape(-1),
                   d_b_re.reshape(-1), d_b_im.reshape(-1), d_c_re.reshape(-1), d_c_im.reshape(-1), d_d.reshape(-1),
                   d_ldt.reshape(-1), d_n2g.reshape(-1), d_final_g.reshape(-1), loss_l[0, 0:1]]
    small_sizes = [v.shape[0] for v in small_local]
    small_pack = _pad_rows(jnp.concatenate(small_local), 256 * LANE).reshape(-1, LANE)
    small_go = exchange_start("gather_small_start", gather_plan, NDEV - 1, [small_pack],
                              [((NDEV,) + small_pack.shape, F32)], deps=[chip_mid[4]])

    g_in_t = mm("mm_g_in", dz, h1, "tn", BF16, tiles=(CW, 1024, 2048), o_rot=Z_ROT, deps=[small_go[4]])
    pair_in = pair_go("in", [g_in_t])

    dh1 = mm("mm_dh1", dz, w_in_t, "nn", BF16, tiles=(2048, 1024, CW), b_rot=Z_ROT, deps=[pair_in[4]])

    def f_dnorm1(dh, xv, dx1v, g, sc):
        dh = dh.astype(F32)
        r, xh = _rms_stats(xv)
        dxh = dh * (1.0 + sc) * g
        return ([dx1v + _rms_bwd(dxh, xh, r)],
                [_colsum(dh * xh * g), _colsum(dh), _colsum(dh * (1.0 + sc) * xh)])

    grad_x, d_sc1, d_sh1, d_n1g = rowwise(
        "dnorm1", f_dnorm1, [dh1, xs, dx1], [n1g, sc1], [(D, F32)], [D, D, D], 512)
    late_local = [d_sh1.reshape(-1), d_sc1.reshape(-1), d_n1g.reshape(-1)]
    late_pack = _pad_rows(jnp.concatenate(late_local), 16 * LANE).reshape(-1, LANE)
    late_go = exchange_start("gather_late_start", gather_plan, NDEV - 1, [late_pack],
                             [((NDEV,) + late_pack.shape, F32)])
    chip_in = chip_go("in", ("w_in",), pair_in, late_go[4])

    weights = {
        "w_ada": (w_ada, m_w_ada, v_w_ada), "b_ada": (b_ada, m_b_ada, v_b_ada), "norm1_g": (norm1_g, m_norm1_g, v_norm1_g),
        "w_in": (w_in, m_w_in, v_w_in), "conv_w": (conv_w, m_conv_w, v_conv_w), "conv_b": (conv_b, m_conv_b, v_conv_b),
        "conv_ln_g": (conv_ln_g, m_conv_ln_g, v_conv_ln_g), "conv_ln_b": (conv_ln_b, m_conv_ln_b, v_conv_ln_b),
        "conv_proj": (conv_proj, m_conv_proj, v_conv_proj), "ssm_a_re": (ssm_a_re, m_ssm_a_re, v_ssm_a_re),
        "ssm_a_im": (ssm_a_im, m_ssm_a_im, v_ssm_a_im), "ssm_b_re": (ssm_b_re, m_ssm_b_re, v_ssm_b_re),
        "ssm_b_im": (ssm_b_im, m_ssm_b_im, v_ssm_b_im), "ssm_c_re": (ssm_c_re, m_ssm_c_re, v_ssm_c_re),
        "ssm_c_im": (ssm_c_im, m_ssm_c_im, v_ssm_c_im), "ssm_d": (ssm_d, m_ssm_d, v_ssm_d),
        "ssm_log_dt": (ssm_log_dt, m_ssm_log_dt, v_ssm_log_dt), "ssm_glu": (ssm_glu, m_ssm_glu, v_ssm_glu),
        "w_out": (w_out, m_w_out, v_w_out), "norm2_g": (norm2_g, m_norm2_g, v_norm2_g),
        "w_ffn_in": (w_ffn_in, m_w_ffn_in, v_w_ffn_in), "w_ffn_out": (w_ffn_out, m_w_ffn_out, v_w_ffn_out),
        "final_g": (final_g, m_final_g, v_final_g),
    }
    order = list(weights)
    big = ("w_ada", "w_in", "conv_proj", "ssm_glu", "w_out", "w_ffn_in", "w_ffn_out")
    grads, delta, new_m, new_v = {}, {}, {}, {}

    def adam_big(n, g2d, transposed=False):
        wv, mv, vv = weights[n]
        shp = wv.shape
        t_in = (lambda a: a.reshape(shp[-2:]).T) if transposed else (lambda a: a.reshape(shp[-2:]))
        t_out = (lambda a: a.T.reshape(shp)) if transposed else (lambda a: a.reshape(shp))
        w2 = t_in(wv)
        if isinstance(g2d, tuple):
            partial, recv = [p.reshape((p.shape[0],) + w2.shape) for p in g2d]
            g2d, d_, m_, v_ = chip_sum_adam("adam_" + n, partial, recv, w2, t_in(mv), t_in(vv))
        else:
            d_, m_, v_ = adam("adam_" + n, w2, g2d, t_in(mv), t_in(vv))
        grads[n], delta[n], new_m[n], new_v[n] = t_out(g2d), t_out(d_), t_out(m_), t_out(v_)
        return d_

    parts_ffn_out, parts_ffn_in = chip_done("ffn", chip_ffn, chip_in[4])
    adam_big("w_ffn_out", parts_ffn_out)
    last = adam_big("w_ffn_in", parts_ffn_in, transposed=True)
    parts_out, parts_conv_proj, parts_ssm_glu = chip_done("mid", chip_mid, last)
    adam_big("w_out", parts_out)
    adam_big("conv_proj", parts_conv_proj, transposed=True)
    last_mid = adam_big("ssm_glu", parts_ssm_glu, transposed=True)

    _, (late_all,) = exchange_wait("gather_late_wait", gather_plan, late_go, last_mid, place_own=True)
    _, (small_all,) = exchange_wait("gather_small_wait", gather_plan, small_go, late_all, place_own=True)

    def unpack(vec, sizes):
        out, pos = [], 0
        for n in sizes:
            out.append(vec[pos:pos + n])
            pos += n
        return out

    g_sh1, g_sc1, g_n1g = unpack(sum_slots("sum_small_late", late_all).reshape(-1), [D, D, D])
    (g_mod_rest, g_cw_full, g_cb, g_lng, g_lnb, g_a_re, g_a_im, g_b_re, g_b_im, g_c_re, g_c_im, g_d, g_ldt,
     g_n2g, g_fg, loss_sum) = unpack(sum_slots("sum_small", small_all).reshape(-1), small_sizes)
    g_b_ada = jnp.concatenate([g_sh1, g_sc1, g_mod_rest])
    loss = loss_sum[0]
    dmod_all = jnp.concatenate([late_all.reshape(NDEV, -1)[:, 0:2 * D], small_all.reshape(NDEV, -1)[:, 0:4 * D]],
                               axis=1)
    g_w_ada = ada_grad(act_all, lax.dynamic_slice_in_dim(dmod_all, me * ncol, ncol, axis=1))
    ccol = conv_w.shape[2]
    g_conv_w = lax.dynamic_slice_in_dim(g_cw_full.reshape(KC, CW), me * ccol, ccol, axis=1)

    adam_big("w_ada", g_w_ada)
    grads.update({
        "b_ada": g_b_ada.reshape(b_ada.shape), "norm1_g": g_n1g.reshape(norm1_g.shape),
        "conv_w": g_conv_w[None], "conv_b": g_cb.reshape(conv_b.shape),
        "conv_ln_g": g_lng.reshape(conv_ln_g.shape), "conv_ln_b": g_lnb.reshape(conv_ln_b.shape),
        "ssm_a_re": g_a_re.reshape(ssm_a_re.shape),
        "ssm_a_im": g_a_im.reshape(ssm_a_im.shape), "ssm_b_re": g_b_re.reshape(ssm_b_re.shape),
        "ssm_b_im": g_b_im.reshape(ssm_b_im.shape), "ssm_c_re": g_c_re.reshape(ssm_c_re.shape),
        "ssm_c_im": g_c_im.reshape(ssm_c_im.shape), "ssm_d": g_d.reshape(ssm_d.shape),
        "ssm_log_dt": g_ldt.reshape(ssm_log_dt.shape),
        "norm2_g": g_n2g.reshape(norm2_g.shape),
        "final_g": g_fg.reshape(final_g.shape),
    })
    small = [n for n in order if n not in big]
    def rows(a):
        if a.ndim == 4 and a.shape[-1] < a.shape[-2]:
            a = a.swapaxes(-1, -2)
        return a.reshape(1, -1) if a.ndim == 1 else a.reshape(-1, a.shape[-1])

    def unrows(a, shp):
        if len(shp) == 4 and shp[-1] < shp[-2]:
            return a.reshape(shp[:-2] + (shp[-1], shp[-2])).swapaxes(-1, -2)
        return a.reshape(shp)

    small_out = adam_many("adam_small", [rows(weights[n][0]) for n in small], [rows(grads[n]) for n in small],
                          [rows(weights[n][1]) for n in small], [rows(weights[n][2]) for n in small])
    for q, n in enumerate(small):
        shp = weights[n][0].shape
        delta[n], new_m[n], new_v[n] = [unrows(small_out[t * len(small) + q], shp) for t in range(3)]

    (parts_in,) = chip_done("in", chip_in, small_out[0])
    adam_big("w_in", parts_in, transposed=True)

    return (loss, grad_x[None], *[grads[n] for n in order], *[delta[n] for n in order],
            *[new_m[n] for n in order], *[new_v[n] for n in order])
```

```python
import functools
import math

import jax
import jax.numpy as jnp
from jax import lax
from jax.experimental import pallas as pl
from jax.experimental.pallas import tpu as pltpu

F32 = jnp.float32
BF16 = jnp.bfloat16

D = 1024
CW = 512
KC = 31
SW = 512
NG = 32
GH = 16
NP = 64
NST = NG * NP
FH = 2816
FFN_BLK = 1408
MRG_BLK = 1024
NMOD = 6
NDEV = 8
EPS = 1e-6
CB = 128
SB = 512
NBLK = SW // CB
HALO = 32
ZW = 2 * CW + SW + 2 * D
Z_ROT = lambda j: (j + 3) % (ZW // CW)
ZB_A, ZB_G, ZB_U = 4, 5, 6

ADAM_LR = 0.001
ADAM_B1 = 0.9
ADAM_B2 = 0.999
ADAM_EPS = 1e-08
ADAM_WD = 0.01
ADAM_STEP = 10

V7X_VMEM_BYTES = 64 * 1024 * 1024
VMEM_LIMIT = V7X_VMEM_BYTES - 8 * 1024 * 1024
LANE = 128
MESH = pl.DeviceIdType.MESH
ANY_SPEC = pl.BlockSpec(memory_space=pl.ANY)


def _params(sem=None, **kw):
    if sem is not None:
        kw["dimension_semantics"] = sem
    return pltpu.CompilerParams(vmem_limit_bytes=VMEM_LIMIT, **kw)


def _tile(n, most):
    best = None
    for t in range(LANE, most + 1, LANE):
        if n % t == 0:
            best = t
    if best is None:
        raise ValueError(f"no tile for {n}")
    return best


def _sig(x):
    return jax.nn.sigmoid(x)


def mm(name, a, b, mode, out_dtype=F32, tiles=None, b_rot=None, o_rot=None, deps=()):
    if mode == "nn":
        (m, k), (k2, n) = a.shape, b.shape
    elif mode == "nt":
        (m, k), (n, k2) = a.shape, b.shape
    else:
        (k, m), (k2, n) = a.shape, b.shape
    assert k == k2, (name, a.shape, b.shape)
    bm, bn, bk = tiles or (_tile(m, 1024), _tile(n, 1408), _tile(k, 1408 if k % 1408 == 0 else 1024))
    bm, bn, bk = min(bm, m), min(bn, n), min(bk, k)
    assert m % bm == 0 and n % bn == 0 and k % bk == 0, (name, m, n, k, bm, bn, bk)
    nk = k // bk
    rot = lambda idx, r: idx if r is None else r(idx)
    if mode == "nn":
        a_spec = pl.BlockSpec((bm, bk), lambda i, j, kk: (i, kk))
        b_spec = pl.BlockSpec((bk, bn), lambda i, j, kk: (rot(kk, b_rot), j))
        dims = (((1,), (0,)), ((), ()))
    elif mode == "nt":
        a_spec = pl.BlockSpec((bm, bk), lambda i, j, kk: (i, kk))
        b_spec = pl.BlockSpec((bn, bk), lambda i, j, kk: (rot(j, b_rot), kk))
        dims = (((1,), (1,)), ((), ()))
    else:
        assert b_rot is None
        a_spec = pl.BlockSpec((bk, bm), lambda i, j, kk: (kk, i))
        b_spec = pl.BlockSpec((bk, bn), lambda i, j, kk: (kk, j))
        dims = (((0,), (0,)), ((), ()))

    def body(a_ref, b_ref, *rest):
        o_ref, acc_ref = rest[-2:]
        kk = pl.program_id(2)

        @pl.when(kk == 0)
        def _():
            acc_ref[...] = jnp.zeros_like(acc_ref)

        acc_ref[...] += lax.dot_general(a_ref[...], b_ref[...], dims, preferred_element_type=F32)

        @pl.when(kk == nk - 1)
        def _():
            o_ref[...] = acc_ref[...].astype(o_ref.dtype)

    return pl.pallas_call(
        body, name=name,
        grid=(m // bm, n // bn, nk),
        in_specs=[a_spec, b_spec] + [ANY_SPEC] * len(deps),
        out_specs=pl.BlockSpec((bm, bn), lambda i, j, kk: (rot(i, o_rot), j)),
        out_shape=jax.ShapeDtypeStruct((m, n), out_dtype),
        scratch_shapes=[pltpu.VMEM((bm, bn), F32)],
        compiler_params=_params(("parallel", "parallel", "arbitrary")),
    )(a, b, *deps)


def mm_ep(name, a, b, n_acc, acc_block, epilogue, extras, outs, tiles, deps=(), b_kn=False, consts=()):
    m, k = a.shape
    bm, bn, bk = tiles
    bm = min(bm, m)
    nj = outs[0][0] // (outs[0][2] * bn)
    assert m % bm == 0 and bk == k and b.shape[0 if b_kn else 1] == k, (name, a.shape, b.shape, tiles)
    ne, nc, no, nd = len(extras), len(consts), len(outs), len(deps)
    dims = (((1,), (0,)), ((), ())) if b_kn else (((1,), (1,)), ((), ()))

    def body(*refs):
        a_ref, b_refs = refs[0], refs[1:1 + n_acc]
        e_refs = refs[1 + n_acc:1 + n_acc + ne + nc]
        first_out = 1 + n_acc + ne + nc + nd
        o_refs = refs[first_out:first_out + no]
        av = a_ref[...]
        prods = [lax.dot_general(av, b_ref[...], dims, preferred_element_type=F32) for b_ref in b_refs]
        for o_ref, v in zip(o_refs, epilogue(prods, *[e[...] for e in e_refs])):
            o_ref[...] = v.astype(o_ref.dtype)

    in_specs = [pl.BlockSpec((bm, bk), lambda i, j: (i, 0))]
    if b_kn:
        in_specs += [pl.BlockSpec((bk, bn), functools.partial(lambda i, j, q: (0, acc_block(j, q)), q=q))
                     for q in range(n_acc)]
    else:
        in_specs += [pl.BlockSpec((bn, bk), functools.partial(lambda i, j, q: (acc_block(j, q), 0), q=q))
                     for q in range(n_acc)]
    in_specs += [pl.BlockSpec((bm, w * bn), functools.partial(lambda i, j, off: (i, j + off), off=off))
                 for (_, w, off) in extras]
    in_specs += [pl.BlockSpec((1, bn), lambda i, j: (0, j)) for _ in consts]
    in_specs += [ANY_SPEC] * nd
    return pl.pallas_call(
        body, name=name, grid=(m // bm, nj),
        in_specs=in_specs,
        out_specs=[pl.BlockSpec((bm, w * bn), lambda i, j: (i, j)) for (_, _, w) in outs],
        out_shape=[jax.ShapeDtypeStruct((m, cols), dt) for (cols, dt, _) in outs],
        compiler_params=_params(("parallel", "parallel")),
    )(a, *[b] * n_acc, *[e[0] for e in extras], *consts, *deps)


def mm_ep_pipe(name, a, b, n_acc, acc_block, epilogue, extras, outs, tiles, deps=(), b_kn=False, consts=(), sums=()):
    m, k = a.shape
    bm, bn, bk = tiles
    bm = min(bm, m)
    assert bk == k and m % bm == 0 and b.shape[0 if b_kn else 1] == k, (name, a.shape, b.shape, tiles)
    ni, nj = m // bm, outs[0][0] // (outs[0][2] * bn)
    nt = ni * nj
    assert not sums or nj == 1, name
    ne, nc, no, ns, nd = len(extras), len(consts), len(outs), len(sums), len(deps)
    dims = (((1,), (0,)), ((), ())) if b_kn else (((1,), (1,)), ((), ()))
    cur_i = lambda t: jnp.minimum(t, nt - 1) // nj
    cur_j = lambda t: jnp.minimum(t, nt - 1) % nj
    prev_i = lambda t: jnp.maximum(t - 1, 0) // nj
    prev_j = lambda t: jnp.maximum(t - 1, 0) % nj

    def body(*refs):
        a_ref, b_refs = refs[0], refs[1:1 + n_acc]
        e_refs = refs[1 + n_acc:1 + n_acc + ne + nc]
        first_out = 1 + n_acc + ne + nc + nd
        o_refs = refs[first_out:first_out + no]
        s_refs = refs[first_out + no:first_out + no + ns]
        acc_ref = refs[first_out + no + ns]
        t = pl.program_id(0)

        @pl.when(t == 0)
        def _():
            acc_ref[...] = jnp.zeros_like(acc_ref)
            for s_ref in s_refs:
                s_ref[...] = jnp.zeros_like(s_ref)

        slot = t % 2
        done = [acc_ref[(1 - slot) * n_acc + q] for q in range(n_acc)]
        av = a_ref[...]
        for q, b_ref in enumerate(b_refs):
            acc_ref[slot * n_acc + q] = lax.dot_general(av, b_ref[...], dims, preferred_element_type=F32)
        res = epilogue(done, *[e[...] for e in e_refs])
        tiles_out, sums_out = res if ns else (res, ())
        for o_ref, v in zip(o_refs, tiles_out):
            o_ref[...] = v.astype(o_ref.dtype)
        live = (t >= 1).astype(F32)
        for s_ref, v in zip(s_refs, sums_out):
            s_ref[...] += v * live

    in_specs = [pl.BlockSpec((bm, k), lambda t: (cur_i(t), 0))]
    if b_kn:
        in_specs += [pl.BlockSpec((k, bn), functools.partial(lambda t, q: (0, acc_block(cur_j(t), q)), q=q))
                     for q in range(n_acc)]
    else:
        in_specs += [pl.BlockSpec((bn, k), functools.partial(lambda t, q: (acc_block(cur_j(t), q), 0), q=q))
                     for q in range(n_acc)]
    in_specs += [pl.BlockSpec((bm, w * bn), functools.partial(lambda t, off: (prev_i(t), prev_j(t) + off), off=off))
                 for (_, w, off) in extras]
    in_specs += [pl.BlockSpec((1, bn), lambda t: (0, prev_j(t))) for _ in consts]
    in_specs += [ANY_SPEC] * nd
    out_specs = [pl.BlockSpec((bm, w * bn), lambda t: (prev_i(t), prev_j(t))) for (_, _, w) in outs]
    out_specs += [pl.BlockSpec((1, w), lambda t: (0, 0)) for w in sums]
    out_shape = [jax.ShapeDtypeStruct((m, cols), dt) for (cols, dt, _) in outs]
    out_shape += [jax.ShapeDtypeStruct((1, w), F32) for w in sums]
    return pl.pallas_call(
        body, name=name, grid=(nt + 1,),
        in_specs=in_specs, out_specs=out_specs, out_shape=out_shape,
        scratch_shapes=[pltpu.VMEM((2 * n_acc, bm, bn), F32)],
        compiler_params=_params(("arbitrary",)),
    )(a, *[b] * n_acc, *[e[0] for e in extras], *consts, *deps)


def rowwise(name, fn, rows, consts, out_rows, out_sums, ts, alias=None, deps=()):
    rows = [r if isinstance(r, tuple) else (r, r.shape[1], 0) for r in rows]
    out_rows = [o if len(o) == 4 else (o[0], o[1], o[0], 0) for o in out_rows]
    s = rows[0][0].shape[0]
    nt = s // ts
    nr, nc, no, ns = len(rows), len(consts), len(out_rows), len(out_sums)
    in_specs = [pl.BlockSpec((ts, w), functools.partial(lambda i, cb: (i, cb), cb=cb)) for (_, w, cb) in rows]
    in_specs += [pl.BlockSpec(c.shape, lambda i: (0, 0)) for c in consts]
    operands = [r[0] for r in rows] + list(consts)
    aliases = {}
    if alias is not None:
        in_specs.append(pl.BlockSpec(memory_space=pl.ANY))
        operands.append(alias[0])
        aliases = {nr + nc: alias[1]}
    in_specs += [ANY_SPEC] * len(deps)
    operands += list(deps)
    out_shape = [jax.ShapeDtypeStruct((s, tw), dt) for (_, dt, tw, _) in out_rows]
    out_shape += [jax.ShapeDtypeStruct((1, w), F32) for w in out_sums]
    out_specs = [pl.BlockSpec((ts, w), functools.partial(lambda i, cb: (i, cb), cb=cb)) for (w, _, _, cb) in out_rows]
    out_specs += [pl.BlockSpec((1, w), lambda i: (0, 0)) for w in out_sums]
    n_in = len(operands)

    def body(*refs):
        ins, outs = refs[:nr + nc], refs[n_in:]
        i = pl.program_id(0)
        ro, so = fn(*[r[...] for r in ins])
        for q in range(no):
            outs[q][...] = ro[q].astype(outs[q].dtype)
        if ns:
            @pl.when(i == 0)
            def _():
                for q in range(ns):
                    outs[no + q][...] = jnp.zeros_like(outs[no + q])

            for q in range(ns):
                outs[no + q][...] += so[q]

    return pl.pallas_call(
        body, name=name, grid=(nt,),
        in_specs=in_specs, out_specs=out_specs, out_shape=out_shape, input_output_aliases=aliases,
        compiler_params=_params(("arbitrary",) if ns else ("parallel",)),
    )(*operands)


def _colsum(v):
    return jnp.sum(v, axis=0, keepdims=True)


def _rms_stats(xv):
    r = lax.rsqrt(jnp.mean(xv * xv, axis=-1, keepdims=True) + EPS)
    return r, xv * r


def _rms_bwd(dxhat, xhat, r):
    return r * (dxhat - xhat * jnp.mean(dxhat * xhat, axis=-1, keepdims=True))


def _gelu(v):
    k = math.sqrt(2.0 / math.pi)
    t = jnp.tanh(k * (v + 0.044715 * v * v * v))
    return 0.5 * v * (1.0 + t), t


def _gelu_grad(v, t):
    k = math.sqrt(2.0 / math.pi)
    return 0.5 * (1.0 + t) + 0.5 * v * (1.0 - t * t) * k * (1.0 + 3.0 * 0.044715 * v * v)


CONV_TS = 512
CONV_CH = 64


def _ln_fwd(yc, g, b):
    mu = jnp.mean(yc, axis=-1, keepdims=True)
    xc = yc - mu
    rstd = lax.rsqrt(jnp.mean(xc * xc, axis=-1, keepdims=True) + EPS)
    nhat = xc * rstd
    return nhat, rstd, nhat * g + b


SUBL = 8


def _shifted_copies(buf, sh, ts):
    for b in range(1, SUBL):
        sh[b - 1] = buf[pl.ds(b, ts + HALO - SUBL), :]


def _shifted(buf, sh, start):
    b = start % SUBL
    if b == 0:
        return buf[pl.ds(start, CONV_CH), :]
    return sh[b - 1, pl.ds(start - b, CONV_CH), :]


def conv_fwd(z, w32, cb, lg, lb):
    s = z.shape[0]
    ts = CONV_TS
    nt = s // ts
    hb = ts // HALO

    def body(a_ref, g_ref, ah_ref, gh_ref, w_ref, cb_ref, lg_ref, lb_ref, yc_ref, s_ref, ubuf, ush):
        i = pl.program_id(0)
        first = (i > 0).astype(F32)
        ubuf[0:HALO, :] = ah_ref[...] * _sig(gh_ref[...]) * first
        ubuf[HALO:HALO + ts, :] = a_ref[...] * _sig(g_ref[...])
        _shifted_copies(ubuf, ush, ts)
        for c0 in range(0, ts, CONV_CH):
            acc = jnp.zeros((CONV_CH, CW), F32)
            for k in range(KC):
                acc = acc + w_ref[k:k + 1, :] * _shifted(ubuf, ush, c0 + k + 2)
            yc = acc + cb_ref[...]
            yc_ref[c0:c0 + CONV_CH, :] = yc
            _, _, ln = _ln_fwd(yc, lg_ref[...], lb_ref[...])
            s_ref[c0:c0 + CONV_CH, :] = (ln * _sig(ln)).astype(s_ref.dtype)

    cur = lambda cbk: pl.BlockSpec((ts, CW), functools.partial(lambda i, q: (i, q), q=cbk))
    prev = lambda cbk: pl.BlockSpec((HALO, CW), functools.partial(lambda i, q: (jnp.maximum(i * hb - 1, 0), q), q=cbk))
    const = lambda a: pl.BlockSpec(a.shape, lambda i: (0, 0))
    return pl.pallas_call(
        body, name="conv_fwd", grid=(nt,),
        in_specs=[cur(ZB_A), cur(ZB_G), prev(ZB_A), prev(ZB_G), const(w32), const(cb), const(lg), const(lb)],
        out_specs=[pl.BlockSpec((ts, CW), lambda i: (i, 0)), pl.BlockSpec((ts, CW), lambda i: (i, 0))],
        out_shape=[jax.ShapeDtypeStruct((s, CW), F32), jax.ShapeDtypeStruct((s, CW), BF16)],
        scratch_shapes=[pltpu.VMEM((HALO + ts, CW), F32), pltpu.VMEM((SUBL - 1, ts + HALO - SUBL, CW), F32)],
        compiler_params=_params(("parallel",)),
    )(z, z, z, z, w32, cb, lg, lb)


def conv_bwd(ds, yc, z, w32, lg, lb, dz):
    s = z.shape[0]
    ts = CONV_TS
    nt = s // ts
    hb = ts // HALO
    last_hb = s // HALO - 1

    def ln_bwd(dsv, ycv, g, b):
        nhat, rstd, ln = _ln_fwd(ycv, g, b)
        sg = _sig(ln)
        dln = dsv * (sg * (1.0 + ln * (1.0 - sg)))
        dnh = dln * g
        dyc = rstd * (dnh - jnp.mean(dnh, axis=-1, keepdims=True)
                      - nhat * jnp.mean(dnh * nhat, axis=-1, keepdims=True))
        return dyc, dln, nhat

    def body(ds_ref, yc_ref, dsn_ref, ycn_ref, a_ref, g_ref, ah_ref, gh_ref, w_ref, lg_ref, lb_ref, dz_in,
             dz_ref, dlg_ref, dlb_ref, dcb_ref, dw_ref, dbuf, ubuf, dsh, ush, dwacc):
        i = pl.program_id(0)

        @pl.when(i == 0)
        def _():
            dlg_ref[...] = jnp.zeros_like(dlg_ref)
            dlb_ref[...] = jnp.zeros_like(dlb_ref)
            dcb_ref[...] = jnp.zeros_like(dcb_ref)
            dw_ref[...] = jnp.zeros_like(dw_ref)
            dwacc[...] = jnp.zeros_like(dwacc)

        lg, lb = lg_ref[...], lb_ref[...]
        dyc, dln, nhat = ln_bwd(ds_ref[...], yc_ref[...], lg, lb)
        dlg_ref[...] += _colsum(dln * nhat)
        dlb_ref[...] += _colsum(dln)
        dcb_ref[...] += _colsum(dyc)
        dbuf[0:ts, :] = dyc
        nxt = (i < nt - 1).astype(F32)
        dbuf[ts:ts + HALO, :] = ln_bwd(dsn_ref[...], ycn_ref[...], lg, lb)[0] * nxt
        first = (i > 0).astype(F32)
        ubuf[0:HALO, :] = ah_ref[...] * _sig(gh_ref[...]) * first
        ubuf[HALO:HALO + ts, :] = a_ref[...] * _sig(g_ref[...])
        _shifted_copies(dbuf, dsh, ts)
        _shifted_copies(ubuf, ush, ts)
        for c0 in range(0, ts, CONV_CH):
            du = jnp.zeros((CONV_CH, CW), F32)
            dyc_c = dbuf[c0:c0 + CONV_CH, :]
            for k in range(KC):
                du = du + w_ref[k:k + 1, :] * _shifted(dbuf, dsh, c0 + KC - 1 - k)
                prod = dyc_c * _shifted(ubuf, ush, c0 + k + 2)
                dwacc[k] += jnp.sum(prod.reshape(CONV_CH // SUBL, SUBL, CW), axis=0)
            av = a_ref[c0:c0 + CONV_CH, :]
            sg = _sig(g_ref[c0:c0 + CONV_CH, :])
            dz_ref[c0:c0 + CONV_CH, 0:CW] = (du * sg).astype(dz_ref.dtype)
            dz_ref[c0:c0 + CONV_CH, CW:2 * CW] = (du * av * sg * (1.0 - sg)).astype(dz_ref.dtype)

        @pl.when(i == nt - 1)
        def _():
            for k in range(KC):
                dw_ref[k:k + 1, :] = _colsum(dwacc[k])

    cur = lambda w, cbk: pl.BlockSpec((ts, w), functools.partial(lambda i, q: (i, q), q=cbk))
    prev = lambda cbk: pl.BlockSpec((HALO, CW), functools.partial(lambda i, q: (jnp.maximum(i * hb - 1, 0), q), q=cbk))
    nxt_spec = pl.BlockSpec((HALO, CW), lambda i: (jnp.minimum((i + 1) * hb, last_hb), 0))
    const = lambda a: pl.BlockSpec(a.shape, lambda i: (0, 0))
    acc = lambda r: pl.BlockSpec((r, CW), lambda i: (0, 0))
    return pl.pallas_call(
        body, name="conv_bwd", grid=(nt,),
        in_specs=[cur(CW, 0), cur(CW, 0), nxt_spec, nxt_spec, cur(CW, ZB_A), cur(CW, ZB_G), prev(ZB_A), prev(ZB_G),
                  const(w32), const(lg), const(lb), pl.BlockSpec(memory_space=pl.ANY)],
        out_specs=[pl.BlockSpec((ts, 2 * CW), lambda i: (i, ZB_A // 2)), acc(1), acc(1), acc(1), acc(HALO)],
        out_shape=[jax.ShapeDtypeStruct(dz.shape, dz.dtype), jax.ShapeDtypeStruct((1, CW), F32),
                   jax.ShapeDtypeStruct((1, CW), F32), jax.ShapeDtypeStruct((1, CW), F32),
                   jax.ShapeDtypeStruct((HALO, CW), F32)],
        scratch_shapes=[pltpu.VMEM((ts + HALO, CW), F32), pltpu.VMEM((HALO + ts, CW), F32)]
        + [pltpu.VMEM((SUBL - 1, ts + HALO - SUBL, CW), F32)] * 2 + [pltpu.VMEM((KC, SUBL, CW), F32)],
        input_output_aliases={11: 0},
        compiler_params=_params(("arbitrary",)),
    )(ds, yc, ds, yc, z, z, z, z, w32, lg, lb, dz)


SSM_TS = 2048
GRP = 8


def _cmul(ar, ai, br, bi):
    return ar * br - ai * bi, ar * bi + ai * br


def _scan_tables(ar, ai, reverse):
    n = ar.shape[1]
    row = lax.broadcasted_iota(jnp.int32, (GRP, n), 0)
    dist = (GRP - 1 - row) if reverse else row
    one_r = jnp.broadcast_to(ar, (GRP, n))
    one_i = jnp.broadcast_to(ai, (GRP, n))
    p2r, p2i = _cmul(one_r, one_i, one_r, one_i)
    p4r, p4i = _cmul(p2r, p2i, p2r, p2i)
    steps = []
    for sft, (pr, pi) in ((1, (one_r, one_i)), (2, (p2r, p2i)), (4, (p4r, p4i))):
        keep = dist >= sft
        steps.append((jnp.where(keep, pr, 0.0), jnp.where(keep, pi, 0.0)))
    cr, ci = one_r, one_i
    accr, acci = one_r, one_i
    for e in range(1, GRP):
        cr, ci = _cmul(cr, ci, one_r, one_i)
        accr = jnp.where(dist == e, cr, accr)
        acci = jnp.where(dist == e, ci, acci)
    return steps, (accr, acci)


def _scan_group(xr, xi, steps, carry_tab, cr, ci, reverse):
    for sft, (tr, ti) in zip((1, 2, 4), steps):
        amt = (GRP - sft) if reverse else sft
        sr = pltpu.roll(xr, amt, 0)
        si = pltpu.roll(xi, amt, 0)
        xr, xi = xr + tr * sr - ti * si, xi + tr * si + ti * sr
    pr, pi = carry_tab
    xr = xr + pr * cr - pi * ci
    xi = xi + pr * ci + pi * cr
    return xr, xi


def ssm_fwd(z, wb_re, wb_im, wc, e_re, e_im, dvec):
    s = z.shape[0]
    ts = SSM_TS
    nt = s // ts
    ucol0 = ZB_U * CW // CB

    def body(u_ref, wbr_ref, wbi_ref, wc_ref, er_ref, ei_ref, d_ref, xr_ref, xi_ref, y_ref, gl_ref, car_r, car_i):
        i = pl.program_id(1)

        @pl.when(i == 0)
        def _():
            car_r[...] = jnp.zeros_like(car_r)
            car_i[...] = jnp.zeros_like(car_i)

        u = u_ref[...]
        ub = u.astype(BF16)
        xr_ref[...] = jnp.dot(ub, wbr_ref[0], preferred_element_type=F32)
        xi_ref[...] = jnp.dot(ub, wbi_ref[0], preferred_element_type=F32)
        steps, ctab = _scan_tables(er_ref[0], ei_ref[0], False)

        def grp(r, carry):
            cr, ci = carry
            r0 = pl.multiple_of(r * GRP, GRP)
            xr, xi = _scan_group(xr_ref[pl.ds(r0, GRP), :], xi_ref[pl.ds(r0, GRP), :], steps, ctab, cr, ci, False)
            xr_ref[pl.ds(r0, GRP), :] = xr
            xi_ref[pl.ds(r0, GRP), :] = xi
            return (jnp.broadcast_to(xr[GRP - 1:GRP, :], (GRP, SB)), jnp.broadcast_to(xi[GRP - 1:GRP, :], (GRP, SB)))

        cr, ci = lax.fori_loop(0, ts // GRP, grp, (car_r[...], car_i[...]))
        car_r[...] = cr
        car_i[...] = ci
        y = (jnp.dot(xr_ref[...].astype(BF16), wc_ref[0, 0:SB, :], preferred_element_type=F32)
             + jnp.dot(xi_ref[...].astype(BF16), wc_ref[0, SB:2 * SB, :], preferred_element_type=F32)
             + d_ref[0] * u)
        y_ref[...] = y
        gl_ref[...] = _gelu(y)[0].astype(gl_ref.dtype)

    blk3 = lambda a: pl.BlockSpec((1,) + a.shape[1:], lambda j, i: (j, 0, 0))
    return pl.pallas_call(
        body, name="ssm_fwd", grid=(NBLK, nt),
        in_specs=[pl.BlockSpec((ts, CB), lambda j, i: (i, ucol0 + j)),
                  blk3(wb_re), blk3(wb_im), blk3(wc), blk3(e_re), blk3(e_im), blk3(dvec)],
        out_specs=[pl.BlockSpec((ts, SB), lambda j, i: (i, j)), pl.BlockSpec((ts, SB), lambda j, i: (i, j)),
                   pl.BlockSpec((ts, CB), lambda j, i: (i, j)), pl.BlockSpec((ts, CB), lambda j, i: (i, j))],
        out_shape=[jax.ShapeDtypeStruct((s, NST), F32), jax.ShapeDtypeStruct((s, NST), F32),
                   jax.ShapeDtypeStruct((s, SW), F32), jax.ShapeDtypeStruct((s, SW), BF16)],
        scratch_shapes=[pltpu.VMEM((GRP, SB), F32), pltpu.VMEM((GRP, SB), F32)],
        compiler_params=_params(("parallel", "arbitrary")),
    )(z, wb_re, wb_im, wc, e_re, e_im, dvec)


def ssm_bwd(dgl, ypre, z, xs_re, xs_im, wbt_re, wbt_im, wct, e_re, e_im, dvec, dz):
    s = z.shape[0]
    ts = SSM_TS
    nt = s // ts
    ucol0 = ZB_U * CW // CB
    tn_dims = (((0,), (0,)), ((), ()))

    def body(dgl_ref, y_ref, u_ref, xr_ref, xi_ref, wbtr_ref, wbti_ref, wct_ref, er_ref, ei_ref, d_ref, dz_in,
             du_ref, dd_ref, dar_ref, dai_ref, dwbr_ref, dwbi_ref, dwc_ref,
             lr_ref, li_ref, car_r, car_i, acc_r, acc_i):
        i = pl.program_id(1)

        @pl.when(i == 0)
        def _():
            for ref in (car_r, car_i, acc_r, acc_i, dd_ref, dwbr_ref, dwbi_ref, dwc_ref):
                ref[...] = jnp.zeros_like(ref)

        u = u_ref[...]
        y = y_ref[...]
        dy = dgl_ref[...] * _gelu_grad(y, _gelu(y)[1])
        dd_ref[0] += _colsum(dy * u)
        dyb = dy.astype(BF16)
        dxo = jnp.dot(dyb, wct_ref[0], preferred_element_type=F32)
        lr_ref[...] = dxo[:, 0:SB]
        li_ref[...] = dxo[:, SB:2 * SB]
        steps, ctab = _scan_tables(er_ref[0], -ei_ref[0], True)
        row = lax.broadcasted_iota(jnp.int32, (GRP, SB), 0)

        def grp(q, carry):
            cr, ci, ar, ai = carry
            r0 = pl.multiple_of((ts // GRP - 1 - q) * GRP, GRP)
            lr, li = _scan_group(lr_ref[pl.ds(r0, GRP), :], li_ref[pl.ds(r0, GRP), :], steps, ctab, cr, ci, True)
            lr_ref[pl.ds(r0, GRP), :] = lr
            li_ref[pl.ds(r0, GRP), :] = li
            nr = jnp.where(row == GRP - 1, cr, pltpu.roll(lr, GRP - 1, 0))
            ni = jnp.where(row == GRP - 1, ci, pltpu.roll(li, GRP - 1, 0))
            xr = xr_ref[pl.ds(r0, GRP), :]
            xi = xi_ref[pl.ds(r0, GRP), :]
            ar = ar + nr * xr + ni * xi
            ai = ai + ni * xr - nr * xi
            return (jnp.broadcast_to(lr[0:1, :], (GRP, SB)), jnp.broadcast_to(li[0:1, :], (GRP, SB)), ar, ai)

        cr, ci, ar, ai = lax.fori_loop(0, ts // GRP, grp, (car_r[...], car_i[...], acc_r[...], acc_i[...]))
        car_r[...] = cr
        car_i[...] = ci
        acc_r[...] = ar
        acc_i[...] = ai

        @pl.when(i == nt - 1)
        def _():
            dar_ref[0] = _colsum(ar)
            dai_ref[0] = _colsum(ai)

        lrb = lr_ref[...].astype(BF16)
        lib = li_ref[...].astype(BF16)
        du = (jnp.dot(lrb, wbtr_ref[0], preferred_element_type=F32)
              + jnp.dot(lib, wbti_ref[0], preferred_element_type=F32) + d_ref[0] * dy)
        du_ref[...] = du.astype(du_ref.dtype)
        ub = u.astype(BF16)
        dwbr_ref[0] += lax.dot_general(ub, lrb, tn_dims, preferred_element_type=F32)
        dwbi_ref[0] += lax.dot_general(ub, lib, tn_dims, preferred_element_type=F32)
        dwc_ref[0, 0:SB, :] += lax.dot_general(xr_ref[...].astype(BF16), dyb, tn_dims, preferred_element_type=F32)
        dwc_ref[0, SB:2 * SB, :] += lax.dot_general(xi_ref[...].astype(BF16), dyb, tn_dims, preferred_element_type=F32)

    rev = lambda i: nt - 1 - i
    blk3 = lambda a: pl.BlockSpec((1,) + a.shape[1:], lambda j, i: (j, 0, 0))
    acc3 = lambda r, c: pl.BlockSpec((1, r, c), lambda j, i: (j, 0, 0))
    return pl.pallas_call(
        body, name="ssm_bwd", grid=(NBLK, nt),
        in_specs=[pl.BlockSpec((ts, CB), lambda j, i: (rev(i), j)), pl.BlockSpec((ts, CB), lambda j, i: (rev(i), j)),
                  pl.BlockSpec((ts, CB), lambda j, i: (rev(i), ucol0 + j)),
                  pl.BlockSpec((ts, SB), lambda j, i: (rev(i), j)), pl.BlockSpec((ts, SB), lambda j, i: (rev(i), j)),
                  blk3(wbt_re), blk3(wbt_im), blk3(wct), blk3(e_re), blk3(e_im), blk3(dvec),
                  pl.BlockSpec(memory_space=pl.ANY)],
        out_specs=[pl.BlockSpec((ts, CB), lambda j, i: (rev(i), ucol0 + j)),
                   acc3(1, CB), acc3(1, SB), acc3(1, SB), acc3(CB, SB), acc3(CB, SB), acc3(2 * SB, CB)],
        out_shape=[jax.ShapeDtypeStruct(dz.shape, dz.dtype),
                   jax.ShapeDtypeStruct((NBLK, 1, CB), F32),
                   jax.ShapeDtypeStruct((NBLK, 1, SB), F32), jax.ShapeDtypeStruct((NBLK, 1, SB), F32),
                   jax.ShapeDtypeStruct((NBLK, CB, SB), F32), jax.ShapeDtypeStruct((NBLK, CB, SB), F32),
                   jax.ShapeDtypeStruct((NBLK, 2 * SB, CB), F32)],
        scratch_shapes=[pltpu.VMEM((ts, SB), F32), pltpu.VMEM((ts, SB), F32)] + [pltpu.VMEM((GRP, SB), F32)] * 4,
        input_output_aliases={11: 0},
        compiler_params=_params(("parallel", "arbitrary")),
    )(dgl, ypre, z, xs_re, xs_im, wbt_re, wbt_im, wct, e_re, e_im, dvec, dz)


def _disc(a_re, a_im, log_dt, b_re, b_im, expand):
    dt = jnp.dot(expand, jnp.exp(log_dt), preferred_element_type=F32, precision=lax.Precision.HIGHEST)
    mag = jnp.exp(dt * a_re)
    e_re, e_im = mag * jnp.cos(dt * a_im), mag * jnp.sin(dt * a_im)
    n_re, n_im = e_re - 1.0, e_im
    den = a_re * a_re + a_im * a_im
    q_re = (n_re * a_re + n_im * a_im) / den
    q_im = (n_im * a_re - n_re * a_im) / den
    return e_re, e_im, q_re * b_re - q_im * b_im, q_re * b_im + q_im * b_re


def _whole(a):
    return pl.BlockSpec(a.shape, functools.partial(lambda n: (0,) * n, n=a.ndim))


def disc_fwd(a_re, a_im, log_dt, b_re, b_im, expand):
    def body(ar, ai, ld, br, bi, ex, er_o, ei_o, bbr_o, bbi_o):
        er, ei, bbr, bbi = _disc(ar[...], ai[...], ld[...], br[...], bi[...], ex[...])
        er_o[...] = er
        ei_o[...] = ei
        bbr_o[...] = bbr
        bbi_o[...] = bbi

    ins = (a_re, a_im, log_dt, b_re, b_im, expand)
    outs = [jax.ShapeDtypeStruct(a_re.shape, F32)] * 2 + [jax.ShapeDtypeStruct(b_re.shape, F32)] * 2
    return pl.pallas_call(body, name="disc_fwd", in_specs=[_whole(a) for a in ins],
                          out_specs=[_whole(o) for o in outs], out_shape=outs, compiler_params=_params())(*ins)


def disc_bwd(a_re, a_im, log_dt, b_re, b_im, expand, de_re, de_im, dbb_re, dbb_im):
    def body(ar, ai, ld, br, bi, ex, der, dei, dbr, dbi, o_ar, o_ai, o_ld, o_br, o_bi):
        exv = ex[...]
        _, vjp = jax.vjp(lambda *p: _disc(*p, exv), ar[...], ai[...], ld[...], br[...], bi[...])
        g = vjp((der[...], dei[...], dbr[...], dbi[...]))
        for o, v in zip((o_ar, o_ai, o_ld, o_br, o_bi), g):
            o[...] = v

    ins = (a_re, a_im, log_dt, b_re, b_im, expand, de_re, de_im, dbb_re, dbb_im)
    outs = [jax.ShapeDtypeStruct(a.shape, F32) for a in (a_re, a_im, log_dt, b_re, b_im)]
    return pl.pallas_call(body, name="disc_bwd", in_specs=[_whole(a) for a in ins],
                          out_specs=[_whole(o) for o in outs], out_shape=outs, compiler_params=_params())(*ins)


def mod_fwd(c_all, w_ada, b_cols):
    def body(c_ref, w_ref, b_ref, act_ref, mod_ref):
        cv = c_ref[...]
        act = cv * _sig(cv)
        act_ref[...] = act
        mod_ref[...] = jnp.dot(act, w_ref[...], preferred_element_type=F32, precision=lax.Precision.HIGHEST) + b_ref[...]

    ins = (c_all, w_ada, b_cols)
    outs = [jax.ShapeDtypeStruct(c_all.shape, F32), jax.ShapeDtypeStruct((NDEV, w_ada.shape[1]), F32)]
    return pl.pallas_call(body, name="mod_fwd", in_specs=[_whole(a) for a in ins],
                          out_specs=[_whole(o) for o in outs], out_shape=outs, compiler_params=_params())(*ins)


def ada_grad(act_all, dmod_cols):
    def body(a_ref, d_ref, o_ref):
        o_ref[...] = lax.dot_general(a_ref[...], d_ref[...], (((0,), (0,)), ((), ())),
                                     preferred_element_type=F32, precision=lax.Precision.HIGHEST)

    out = jax.ShapeDtypeStruct((act_all.shape[1], dmod_cols.shape[1]), F32)
    return pl.pallas_call(body, name="ada_grad", in_specs=[_whole(act_all), _whole(dmod_cols)],
                          out_specs=_whole(out), out_shape=out, compiler_params=_params())(act_all, dmod_cols)


def _adam_math(w, g, m, v):
    m2 = ADAM_B1 * m + (1.0 - ADAM_B1) * g
    v2 = ADAM_B2 * v + (1.0 - ADAM_B2) * (g * g)
    m_hat = m2 / (1.0 - ADAM_B1 ** ADAM_STEP)
    v_hat = v2 / (1.0 - ADAM_B2 ** ADAM_STEP)
    delta = -ADAM_LR * (m_hat / (jnp.sqrt(v_hat) + ADAM_EPS) + ADAM_WD * w)
    return delta, m2, v2


def adam(name, w, g, m, v):
    r, c = w.shape
    tr = max(t for t in range(8, min(r, 512) + 1, 8) if r % t == 0)

    def body(w_ref, g_ref, m_ref, v_ref, d_o, m_o, v_o):
        d, m2, v2 = _adam_math(w_ref[...], g_ref[...], m_ref[...], v_ref[...])
        d_o[...] = d
        m_o[...] = m2
        v_o[...] = v2

    spec = pl.BlockSpec((tr, c), lambda i: (i, 0))
    out = jax.ShapeDtypeStruct((r, c), F32)
    return pl.pallas_call(body, name=name, grid=(r // tr,), in_specs=[spec] * 4, out_specs=[spec] * 3,
                          out_shape=[out] * 3, compiler_params=_params(("parallel",)))(w, g, m, v)


def adam_many(name, ws, gs, ms, vs):
    n = len(ws)

    def body(*refs):
        ins, outs = refs[:4 * n], refs[4 * n:]
        for q in range(n):
            d, m2, v2 = _adam_math(ins[q][...], ins[n + q][...], ins[2 * n + q][...], ins[3 * n + q][...])
            outs[q][...] = d
            outs[n + q][...] = m2
            outs[2 * n + q][...] = v2

    operands = list(ws) + list(gs) + list(ms) + list(vs)
    outs = [jax.ShapeDtypeStruct(w.shape, F32) for w in ws] * 3
    return pl.pallas_call(body, name=name, in_specs=[_whole(a) for a in operands],
                          out_specs=[_whole(o) for o in outs], out_shape=outs, compiler_params=_params())(*operands)


def _rows_tile(r, most):
    best = None
    for t in range(16, min(r, most) + 1, 16):
        if r % t == 0:
            best = t
    assert best is not None, r
    return best


def sum_slots(name, slots, out_dtype=F32):
    n, r, c = slots.shape
    tr = _rows_tile(r, max(16, (2 * 1024 * 1024) // (n * c)))

    def body(s_ref, o_ref):
        acc = s_ref[0].astype(F32)
        for q in range(1, n):
            acc = acc + s_ref[q].astype(F32)
        o_ref[...] = acc.astype(o_ref.dtype)

    return pl.pallas_call(body, name=name, grid=(r // tr,),
                          in_specs=[pl.BlockSpec((n, tr, c), lambda i: (0, i, 0))],
                          out_specs=pl.BlockSpec((tr, c), lambda i: (i, 0)),
                          out_shape=jax.ShapeDtypeStruct((r, c), out_dtype), compiler_params=_params(("parallel",)))(slots)


HBM_SPEC = pl.BlockSpec(memory_space=pltpu.HBM)


def _coords():
    return lax.axis_index("x"), lax.axis_index("y"), lax.axis_index("c")


def _linear(x, y, c):
    return 4 * x + 2 * y + c


def all_gather(name, shards):
    nq = len(shards)

    def body(*refs):
        xs, outs = refs[:nq], refs[nq:2 * nq]
        send_sems, recv_sems, local_sems = refs[2 * nq:2 * nq + 3]
        bufs = refs[2 * nq + 3:]
        x, y, cc = _coords()
        me, sibling = (x, y, cc), (x, y, 1 - cc)
        chips = [(1 - x, y), (x, 1 - y), (1 - x, 1 - y)]

        def slot(q, px, py, pc):
            return outs[q].at[_linear(px, py, pc)]

        def copy(q, k, block, to, src=None):
            return pltpu.make_async_remote_copy(
                src_ref=slot(q, *block) if src is None else src, dst_ref=slot(q, *block),
                send_sem=send_sems.at[7 * q + k], recv_sem=recv_sems.at[7 * q + k], device_id=to, device_id_type=MESH)

        loads = [pltpu.make_async_copy(xs[q], bufs[q], local_sems.at[q]) for q in range(nq)]
        for cp in loads:
            cp.start()
        for cp in loads:
            cp.wait()
        mine = [pltpu.make_async_copy(bufs[q], slot(q, *me), local_sems.at[q]) for q in range(nq)]
        first = []
        for q in range(nq):
            first.append(copy(q, 0, me, sibling, src=bufs[q]))
            first += [copy(q, 1 + j, me, (*chip, cc), src=bufs[q]) for j, chip in enumerate(chips)]
        for cp in mine + first:
            cp.start()
        passed = []
        for q in range(nq):
            for j, chip in enumerate(chips):
                copy(q, 1 + j, (*chip, cc), me).wait_recv()
                passed.append(copy(q, 4 + j, (*chip, cc), sibling))
                passed[-1].start()
        for q in range(nq):
            copy(q, 0, sibling, me).wait_recv()
            for j, chip in enumerate(chips):
                copy(q, 4 + j, (*chip, 1 - cc), me).wait_recv()
        for cp in first + passed:
            cp.wait_send()
        for cp in mine:
            cp.wait()

    return pl.pallas_call(
        body, name=name, in_specs=[HBM_SPEC] * nq, out_specs=[HBM_SPEC] * nq,
        out_shape=[jax.ShapeDtypeStruct((NDEV,) + s.shape, s.dtype) for s in shards],
        scratch_shapes=[pltpu.SemaphoreType.DMA((7 * nq,)), pltpu.SemaphoreType.DMA((7 * nq,)),
                        pltpu.SemaphoreType.DMA((nq,))] + [pltpu.VMEM(s.shape, s.dtype) for s in shards],
    )(*shards)


NCHIP = 4


SEM_SPEC = pl.BlockSpec(memory_space=pltpu.SEMAPHORE)
EFFECT = pltpu.SideEffectType.DATAFLOW_SIDE_EFFECTING


def _peer(x, y, cc, k):
    fx, fy, fc = (k >> 2) & 1, (k >> 1) & 1, k & 1
    return (x + fx - 2 * fx * x, y + fy - 2 * fy * y, cc + fc - 2 * fc * cc)


def gather_plan(srcs, lands, coords):
    x, y, cc = coords
    me = _linear(x, y, cc)
    return [(s, l.at[me], _peer(x, y, cc, k)) for s, l in zip(srcs, lands) for k in range(1, NDEV)]


def near_plan(srcs, lands, coords):
    x, y, cc = coords
    me = _linear(x, y, cc)
    peers = [(x, y, 1 - cc)] + [_peer(x, y, cc, 2 * k) for k in range(1, NCHIP)]
    return [(s, l.at[me], p) for s, l in zip(srcs, lands) for p in peers]


def pass_on_plan(srcs, lands, coords):
    x, y, cc = coords
    out = []
    for l in srcs:
        for k in range(1, NCHIP):
            px, py, _ = _peer(x, y, cc, 2 * k)
            slot = _linear(px, py, cc)
            out.append((l.at[slot], l.at[slot], (x, y, 1 - cc)))
    return out


def pair_plan(srcs, lands, coords):
    x, y, cc = coords
    return [(s.at[2 * chip + 1 - cc], l.at[chip], (x, y, 1 - cc)) for s, l in zip(srcs, lands) for chip in range(NCHIP)]


def chip_plan(srcs, lands, coords):
    x, y, cc = coords
    out = []
    for s, l in zip(srcs, lands):
        for k in range(1, NCHIP):
            px, py, _ = _peer(x, y, cc, 2 * k)
            out.append((s.at[2 * px + py], l.at[k - 1], (px, py, cc)))
    return out


def _remote(copy, i, send_sems, recv_sems):
    src, dst, dev = copy
    return pltpu.make_async_remote_copy(src_ref=src, dst_ref=dst, send_sem=send_sems.at[i], recv_sem=recv_sems.at[i],
                                        device_id=dev, device_id_type=MESH)


def exchange_start(name, plan, ncopy, srcs, land_shapes, deps=()):
    ns, nl, nd = len(srcs), len(land_shapes), len(deps)

    def body(*refs):
        s, l = refs[:ns], refs[ns:ns + nl]
        send_sems, recv_sems = refs[ns + nl + nd], refs[ns + nl + nd + 1]
        token = refs[-1]
        for i, cp in enumerate(plan(s, l, _coords())):
            _remote(cp, i, send_sems, recv_sems).start()
        token[...] = jnp.zeros_like(token)

    hbm = lambda a: pltpu.with_memory_space_constraint(a, pltpu.HBM)
    lands = [lax.empty(shp, dt) for shp, dt in land_shapes]
    thru = [pltpu.HBM(a.shape, a.dtype) for a in list(srcs) + lands]
    outs = pl.pallas_call(
        body, name=name,
        in_specs=[HBM_SPEC] * (ns + nl) + [ANY_SPEC] * nd,
        out_specs=(SEM_SPEC, SEM_SPEC, *[HBM_SPEC] * (ns + nl), pl.BlockSpec(memory_space=pltpu.VMEM)),
        out_shape=(pltpu.SemaphoreType.DMA((ncopy,)), pltpu.SemaphoreType.DMA((ncopy,)), *thru,
                   jax.ShapeDtypeStruct((8, LANE), F32)),
        input_output_aliases={i: 2 + i for i in range(ns + nl)},
        compiler_params=pltpu.CompilerParams(has_side_effects=EFFECT),
    )(*[hbm(a) for a in srcs], *[hbm(a) for a in lands], *deps)
    return outs[0], outs[1], list(outs[2:2 + ns]), list(outs[2 + ns:2 + ns + nl]), outs[-1]


def exchange_wait(name, plan, started, after, place_own=False):
    send_sems, recv_sems, srcs, lands, _ = started
    ns, nl = len(srcs), len(lands)

    def body(*refs):
        s, l = refs[:ns], refs[ns:ns + nl]
        send_sems, recv_sems = refs[ns + nl], refs[ns + nl + 1]
        l_out = refs[2 * ns + nl + 3:2 * ns + 2 * nl + 3]
        scratch = refs[2 * ns + 2 * nl + 3:]
        copies = [_remote(cp, i, send_sems, recv_sems) for i, cp in enumerate(plan(s, l, _coords()))]
        if place_own:
            me = _linear(*_coords())
            local_sems, bufs = scratch[0], scratch[1:]
            loads = [pltpu.make_async_copy(s[q], bufs[q], local_sems.at[q]) for q in range(ns)]
            for cp in loads:
                cp.start()
            for cp in loads:
                cp.wait()
            stores = [pltpu.make_async_copy(bufs[q], l_out[q].at[me], local_sems.at[q]) for q in range(ns)]
            for cp in stores:
                cp.start()
        for cp in copies:
            cp.wait_recv()
        for cp in copies:
            cp.wait_send()
        if place_own:
            for cp in stores:
                cp.wait()

    scratch_shapes = []
    if place_own:
        scratch_shapes = [pltpu.SemaphoreType.DMA((ns,))] + [pltpu.VMEM(a.shape, a.dtype) for a in srcs]
    outs = pl.pallas_call(
        body, name=name,
        in_specs=[HBM_SPEC] * (ns + nl) + [SEM_SPEC, SEM_SPEC, ANY_SPEC],
        out_specs=[HBM_SPEC] * (ns + nl),
        out_shape=[pltpu.HBM(a.shape, a.dtype) for a in srcs + lands],
        input_output_aliases={i: i for i in range(ns + nl)},
        scratch_shapes=scratch_shapes,
        compiler_params=pltpu.CompilerParams(has_side_effects=EFFECT),
    )(*srcs, *lands, send_sems, recv_sems, after)
    return list(outs[:ns]), list(outs[ns:])


def pair_sum(name, g, recv):
    _, r, c = g.shape
    tr = _rows_tile(r, 512)

    def body(g_ref, r_ref, o_ref):
        own = jnp.where(lax.axis_index("c") == 0, g_ref[0, 0], g_ref[0, 1])
        o_ref[0] = (own.astype(F32) + r_ref[0].astype(F32)).astype(o_ref.dtype)

    return pl.pallas_call(
        body, name=name, grid=(NCHIP, r // tr),
        in_specs=[pl.BlockSpec((1, 2, tr, c), lambda k, i: (k, 0, i, 0)), pl.BlockSpec((1, tr, c), lambda k, i: (k, i, 0))],
        out_specs=pl.BlockSpec((1, tr, c), lambda k, i: (k, i, 0)),
        out_shape=jax.ShapeDtypeStruct((NCHIP, r, c), g.dtype), compiler_params=_params(("parallel", "parallel")),
    )(g.reshape(NCHIP, 2, r, c), recv)


def chip_sum_adam(name, partial, recv, w, m, v):
    _, r, c = partial.shape
    tr = _rows_tile(r, 512)

    def body(p_ref, r_ref, w_ref, m_ref, v_ref, g_o, d_o, m_o, v_o):
        chip = 2 * lax.axis_index("x") + lax.axis_index("y")
        own = p_ref[0]
        for k in range(1, NCHIP):
            own = jnp.where(chip == k, p_ref[k], own)
        g = own.astype(F32)
        for k in range(NCHIP - 1):
            g = g + r_ref[k].astype(F32)
        d, m2, v2 = _adam_math(w_ref[...], g, m_ref[...], v_ref[...])
        g_o[...] = g
        d_o[...] = d
        m_o[...] = m2
        v_o[...] = v2

    spec = pl.BlockSpec((tr, c), lambda i: (i, 0))
    out = jax.ShapeDtypeStruct((r, c), F32)
    return pl.pallas_call(
        body, name=name, grid=(r // tr,),
        in_specs=[pl.BlockSpec((NCHIP, tr, c), lambda i: (0, i, 0)), pl.BlockSpec((NCHIP - 1, tr, c), lambda i: (0, i, 0)),
                  spec, spec, spec],
        out_specs=[spec] * 4, out_shape=[out] * 4, compiler_params=_params(("parallel",)),
    )(partial, recv, w, m, v)


def _block_diag(w, rows_per, cols_per):
    w = w.reshape(NBLK, 8, rows_per, cols_per)
    eye = jnp.eye(8, dtype=w.dtype)
    out = w[:, :, :, None, :] * eye[None, :, None, :, None]
    return out.reshape(NBLK, 8 * rows_per, 8 * cols_per)


def _diag_blocks(wd, rows_per, cols_per):
    wd = wd.reshape(NBLK, 8, rows_per, 8, cols_per)
    idx = jnp.arange(8)
    return wd[:, idx, :, idx, :].transpose(1, 0, 2, 3).reshape(NG, rows_per, cols_per)


def _pad_rows(v, mult):
    n = v.shape[0]
    return jnp.pad(v, (0, (-n) % mult))


def kernel(x, c, w_ada, b_ada, norm1_g, w_in, conv_w, conv_b, conv_ln_g, conv_ln_b, conv_proj, ssm_a_re, ssm_a_im, ssm_b_re, ssm_b_im, ssm_c_re, ssm_c_im, ssm_d, ssm_log_dt, ssm_glu, w_out, norm2_g, w_ffn_in, w_ffn_out, final_g, loss_target, m_w_ada, m_b_ada, m_norm1_g, m_w_in, m_conv_w, m_conv_b, m_conv_ln_g, m_conv_ln_b, m_conv_proj, m_ssm_a_re, m_ssm_a_im, m_ssm_b_re, m_ssm_b_im, m_ssm_c_re, m_ssm_c_im, m_ssm_d, m_ssm_log_dt, m_ssm_glu, m_w_out, m_norm2_g, m_w_ffn_in, m_w_ffn_out, m_final_g, v_w_ada, v_b_ada, v_norm1_g, v_w_in, v_conv_w, v_conv_b, v_conv_ln_g, v_conv_ln_b, v_conv_proj, v_ssm_a_re, v_ssm_a_im, v_ssm_b_re, v_ssm_b_im, v_ssm_c_re, v_ssm_c_im, v_ssm_d, v_ssm_log_dt, v_ssm_glu, v_w_out, v_norm2_g, v_w_ffn_in, v_w_ffn_out, v_final_g):
    me = _linear(*_coords())
    xs = x[0]
    tgt = loss_target[0]

    flat = lambda g: g.reshape(NDEV * g.shape[1], g.shape[2])
    w_in_s = w_in[0].T.astype(BF16)
    mids = [p.astype(BF16) for p in (conv_proj[0].T, ssm_glu[0].T, w_out[0])]
    ffns = [p.astype(BF16) for p in (w_ffn_in[0].T, w_ffn_out[0])]
    zone = lambda p: ((NDEV,) + p.shape, p.dtype)
    c_all, cw_g = all_gather("gather_c_conv_w", [c, conv_w[0]])
    in_go = exchange_start("gather_in_start", near_plan, NCHIP, [w_in_s], [zone(w_in_s)], deps=[c_all])
    mids_go = exchange_start("gather_mid_start", gather_plan, 7 * len(mids), mids, [zone(p) for p in mids],
                             deps=[in_go[4]])
    ffns_go = exchange_start("gather_ffn_start", gather_plan, 7 * len(ffns), ffns, [zone(p) for p in ffns],
                             deps=[mids_go[4]])

    ncol = w_ada.shape[2]
    c_all = c_all.reshape(NDEV, D)
    b_cols = lax.dynamic_slice_in_dim(b_ada, me * ncol, ncol, axis=1)
    act_all, mod_cols = mod_fwd(c_all, w_ada[0], b_cols)
    (mod_all,) = all_gather("gather_mod", [mod_cols])
    mod = lax.dynamic_index_in_dim(mod_all, me, axis=1, keepdims=False).reshape(NMOD, D)
    sh1, sc1, g1, sh2, sc2, g2 = [mod[q:q + 1] for q in range(NMOD)]

    expand = jnp.repeat(jnp.eye(NG, dtype=F32), NP, axis=0)
    a_re_c, a_im_c = ssm_a_re.reshape(NST, 1), ssm_a_im.reshape(NST, 1)
    ldt_c = ssm_log_dt.reshape(NG, 1)
    b_re_r, b_im_r = ssm_b_re.reshape(NST, GH), ssm_b_im.reshape(NST, GH)
    e_re, e_im, bb_re, bb_im = disc_fwd(a_re_c, a_im_c, ldt_c, b_re_r, b_im_r, expand)
    e_re_b, e_im_b = e_re.reshape(NBLK, 1, SB), e_im.reshape(NBLK, 1, SB)
    bb_re_g, bb_im_g = bb_re.reshape(NG, NP, GH), bb_im.reshape(NG, NP, GH)
    wbt_re = _block_diag(bb_re_g, NP, GH)
    wbt_im = _block_diag(bb_im_g, NP, GH)
    wb_re, wb_im = wbt_re.transpose(0, 2, 1), wbt_im.transpose(0, 2, 1)
    wct = jnp.concatenate([_block_diag(ssm_c_re[0], GH, NP), -_block_diag(ssm_c_im[0], GH, NP)], axis=2)
    wc = wct.transpose(0, 2, 1)
    to_b = lambda a: a.astype(BF16)
    dvec = ssm_d.reshape(NBLK, 1, CB)

    n1g = norm1_g

    def f_norm1(xv, g, sc, sh):
        _, xh = _rms_stats(xv)
        return [xh * g * (1.0 + sc) + sh], []

    (h1,) = rowwise("norm1", f_norm1, [xs], [n1g, sc1, sh1], [(D, BF16)], [], 512, deps=[ffns_go[4]])
    _, (w_in_land,) = exchange_wait("gather_in_wait", near_plan, in_go, h1, place_own=True)
    pass_go = exchange_start("gather_in_pass_start", pass_on_plan, NCHIP - 1, [w_in_land], [])
    (w_in_g,), _ = exchange_wait("gather_in_pass_wait", pass_on_plan, pass_go, pass_go[4])
    w_in_t = flat(w_in_g)
    z = mm("mm_in", h1, w_in_t, "nt", tiles=(2048, CW, 1024), b_rot=Z_ROT)

    conv_w_full = cw_g.transpose(1, 0, 2).reshape(KC, CW)
    w32 = jnp.pad(conv_w_full, ((0, HALO - KC), (0, 0)))
    yc, s_act = conv_fwd(z, w32, conv_b, conv_ln_g, conv_ln_b)
    conv_proj_t, ssm_glu_t, w_out_f = [
        flat(g) for g in exchange_wait("gather_mid_wait", gather_plan, mids_go, s_act, place_own=True)[1]]
    y_conv = mm("mm_conv_proj", s_act, conv_proj_t, "nt")

    xs_re, xs_im, ypre, gl = ssm_fwd(z, to_b(wb_re), to_b(wb_im), to_b(wc), e_re_b, e_im_b, dvec)
    n_mrg = D // MRG_BLK

    pair_of = lambda t, n: t // 2 + (t % 2) * n

    def ep_merge(accs, yc_v, gates):
        za, zb = accs
        glc, gls = gates[:, 0:MRG_BLK], gates[:, MRG_BLK:2 * MRG_BLK]
        return [_sig(glc) * yc_v + _sig(gls) * (za * _sig(zb)), jnp.concatenate([za, zb], axis=1)]

    merged, z2_pair = mm_ep("mm_ssm_glu", gl, ssm_glu_t, 2, lambda j, q: j + q * n_mrg, ep_merge,
                            [(y_conv, 1, 0), (z, 2, 0)], [(D, BF16, 1), (2 * D, BF16, 2)], (512, MRG_BLK, SW))
    row_tiles = lambda bk: (512, D, bk)
    whole = lambda j, q: j

    def ep_norm2(accs, xv, g1v, g, sc, sh):
        (o1v,) = accs
        x1v = xv + g1v * o1v
        _, xh = _rms_stats(x1v)
        return [x1v, xh * g * (1.0 + sc) + sh, o1v]

    x1, h2, o1 = mm_ep("mm_out", merged, w_out_f, 1, whole, ep_norm2, [(xs, 1, 0)],
                       [(D, F32, 1), (D, BF16, 1), (D, BF16, 1)], row_tiles(D), b_kn=True,
                       consts=[g1, norm2_g, sc2, sh2])
    w_ffn_in_t, w_ffn_out_f = [
        flat(g) for g in exchange_wait("gather_ffn_wait", gather_plan, ffns_go, h2, place_own=True)[1]]
    ffn_tiles = (512, FFN_BLK, 1024)
    n_ffn_blk = FH // FFN_BLK
    pair_map = lambda t: t // 2 + (t % 2) * n_ffn_blk

    def ep_swiglu(accs):
        fg, fu = accs
        return [fg * _sig(fg) * fu, jnp.concatenate([fg, fu], axis=1)]

    act, f_pair = mm_ep("mm_ffn_in", h2, w_ffn_in_t, 2, lambda j, q: j + q * n_ffn_blk, ep_swiglu, [],
                        [(FH, BF16, 1), (2 * FH, BF16, 2)], ffn_tiles)
    fg_row = final_g.reshape(1, D)

    def ep_final(accs, x1v, tv, g2v, fg):
        (o2v,) = accs
        x2v = x1v + g2v * o2v
        r, xh = _rms_stats(x2v)
        yv = xh * fg
        err = yv - tv
        loss = jnp.sum(_colsum(err * err), axis=1, keepdims=True) * (0.5 / D)
        dy = err * (1.0 / D)
        dx2 = _rms_bwd(dy * fg, xh, r)
        return ([dx2, g2v * dx2],
                [jnp.broadcast_to(loss, (1, LANE)), _colsum(dy * xh), _colsum(dx2 * o2v)])

    dx2, do2, loss_l, d_final_g, d_g2 = mm_ep_pipe(
        "mm_ffn_out", act, w_ffn_out_f, 1, whole, ep_final, [(x1, 1, 0), (tgt, 1, 0)],
        [(D, F32, 1), (D, BF16, 1)], row_tiles(FH), b_kn=True, consts=[g2, fg_row], sums=[LANE, D, D])

    g_ffn_out = mm("mm_g_ffn_out", act, do2, "tn", BF16, tiles=(FFN_BLK, 1024, 2048))

    def ep_dswiglu(accs, fp):
        (da,) = accs
        fg, fu = fp[:, 0:FFN_BLK].astype(F32), fp[:, FFN_BLK:2 * FFN_BLK].astype(F32)
        sg = _sig(fg)
        return [jnp.concatenate([da * fu * (sg * (1.0 + fg * (1.0 - sg))), da * (fg * sg)], axis=1)]

    (df,) = mm_ep("mm_dact", do2, w_ffn_out_f, 1, lambda j, q: j, ep_dswiglu, [(f_pair, 2, 0)],
                  [(2 * FH, BF16, 2)], ffn_tiles)
    g_ffn_in_t = mm("mm_g_ffn_in", df, h2, "tn", BF16, tiles=(FFN_BLK, 1024, 2048), o_rot=pair_map)

    def pair_go(tag, grads_t, deps=()):
        srcs = [g.reshape(NDEV, -1, D) for g in grads_t]
        return exchange_start("pair_" + tag + "_start", pair_plan, NCHIP * len(srcs), srcs,
                              [((NCHIP,) + s.shape[1:], s.dtype) for s in srcs], deps)

    def chip_go(tag, names, pair_started, after):
        own, from_sibling = exchange_wait("pair_" + tag + "_wait", pair_plan, pair_started, after)
        partials = [pair_sum("pair_sum_" + n, g, r) for n, g, r in zip(names, own, from_sibling)]
        return exchange_start("chip_" + tag + "_start", chip_plan, (NCHIP - 1) * len(partials), partials,
                              [((NCHIP - 1,) + p.shape[1:], p.dtype) for p in partials])

    def chip_done(tag, chip_started, after):
        partials, from_chips = exchange_wait("chip_" + tag + "_wait", chip_plan, chip_started, after)
        return list(zip(partials, from_chips))

    pair_ffn = pair_go("ffn", [g_ffn_out, g_ffn_in_t])

    dh2 = mm("mm_dh2", df, w_ffn_in_t, "nn", BF16, tiles=(1024, 1024, FFN_BLK), b_rot=pair_map,
             deps=[pair_ffn[4]])

    def f_dnorm2(dh, x1v, dx2v, o1v, g, sc, g1v):
        dh, o1v = dh.astype(F32), o1v.astype(F32)
        r, xh = _rms_stats(x1v)
        dxh = dh * (1.0 + sc) * g
        dx1 = dx2v + _rms_bwd(dxh, xh, r)
        return ([dx1, g1v * dx1],
                [_colsum(dh * xh * g), _colsum(dh), _colsum(dh * (1.0 + sc) * xh), _colsum(dx1 * o1v)])

    dx1, do1, d_sc2, d_sh2, d_n2g, d_g1 = rowwise(
        "dnorm2", f_dnorm2, [dh2, x1, dx2, o1], [norm2_g, sc2, g1], [(D, F32), (D, BF16)], [D, D, D, D], 512)

    g_out = mm("mm_g_out", merged, do1, "tn", BF16)
    chip_ffn = chip_go("ffn", ("w_ffn_out", "w_ffn_in"), pair_ffn, g_out)

    def ep_dmerge(accs, yc_v, z2p, gates):
        (dm,) = accs
        za, zb = z2p[:, 0:MRG_BLK].astype(F32), z2p[:, MRG_BLK:2 * MRG_BLK].astype(F32)
        sc_, ss_, sb_ = _sig(gates[:, 0:MRG_BLK]), _sig(gates[:, MRG_BLK:2 * MRG_BLK]), _sig(zb)
        dys = dm * ss_
        dz2 = jnp.concatenate([dys * sb_, dys * za * sb_ * (1.0 - sb_)], axis=1)
        dgates = jnp.concatenate([dm * yc_v * sc_ * (1.0 - sc_), dm * (za * sb_) * ss_ * (1.0 - ss_)], axis=1)
        return [dm * sc_, dz2, dgates]

    dyconv, dz2, dz = mm_ep("mm_dmerged", do1, w_out_f, 1, lambda j, q: j, ep_dmerge,
                            [(y_conv, 1, 0), (z2_pair, 2, 0), (z, 2, 0)],
                            [(D, BF16, 1), (2 * D, BF16, 2), (ZW, BF16, 2)], (512, MRG_BLK, 1024), deps=[chip_ffn[4]])

    g_conv_proj_t = mm("mm_g_conv_proj", dyconv, s_act, "tn", BF16)
    mrg_map = lambda t: pair_of(t, n_mrg)
    dgl = mm("mm_dgl", dz2, ssm_glu_t, "nn", tiles=(2048, SW, MRG_BLK), b_rot=mrg_map)
    g_ssm_glu_t = mm("mm_g_ssm_glu", dz2, gl, "tn", BF16, tiles=(MRG_BLK, SW, 1024), o_rot=mrg_map)
    pair_mid = pair_go("mid", [g_out, g_conv_proj_t, g_ssm_glu_t])
    ds = mm("mm_ds", dyconv, conv_proj_t, "nn", tiles=(2048, CW, 1024), deps=[pair_mid[4]])
    dz, d_lng, d_lnb, d_cb, d_cw32 = conv_bwd(ds, yc, z, w32, conv_ln_g, conv_ln_b, dz)
    dz, d_d, d_ar, d_ai, d_wb_re, d_wb_im, d_wc = ssm_bwd(
        dgl, ypre, z, xs_re, xs_im, to_b(wbt_re), to_b(wbt_im), to_b(wct), e_re_b, e_im_b, dvec, dz)
    chip_mid = chip_go("mid", ("w_out", "conv_proj", "ssm_glu"), pair_mid, dz)

    d_bb_re = _diag_blocks(d_wb_re.transpose(0, 2, 1), NP, GH).reshape(NST, GH)
    d_bb_im = _diag_blocks(d_wb_im.transpose(0, 2, 1), NP, GH).reshape(NST, GH)
    d_wct = d_wc.transpose(0, 2, 1)
    d_c_re = _diag_blocks(d_wct[:, :, 0:SB], GH, NP)
    d_c_im = -_diag_blocks(d_wct[:, :, SB:2 * SB], GH, NP)
    d_a_re, d_a_im, d_ldt, d_b_re, d_b_im = disc_bwd(
        a_re_c, a_im_c, ldt_c, b_re_r, b_im_r, expand, d_ar.reshape(NST, 1), d_ai.reshape(NST, 1), d_bb_re, d_bb_im)

    small_local = [jnp.concatenate([d_g1, d_sh2, d_sc2, d_g2], axis=1).reshape(-1), d_cw32[0:KC].reshape(-1),
                   d_cb.reshape(-1), d_lng.reshape(-1), d_lnb.reshape(-1), d_a_re.reshape(-1), d_a_im.reshape(-1),
                   d_b_re.reshape(-1), d_b_im.reshape(-1), d_c_re.reshape(-1), d_c_im.reshape(-1), d_d.reshape(-1),
                   d_ldt.reshape(-1), d_n2g.reshape(-1), d_final_g.reshape(-1), loss_l[0, 0:1]]
    small_sizes = [v.shape[0] for v in small_local]
    small_pack = _pad_rows(jnp.concatenate(small_local), 256 * LANE).reshape(-1, LANE)
    small_go = exchange_start("gather_small_start", gather_plan, NDEV - 1, [small_pack],
                              [((NDEV,) + small_pack.shape, F32)], deps=[chip_mid[4]])

    g_in_t = mm("mm_g_in", dz, h1, "tn", BF16, tiles=(CW, 1024, 2048), o_rot=Z_ROT, deps=[small_go[4]])
    pair_in = pair_go("in", [g_in_t])

    dh1 = mm("mm_dh1", dz, w_in_t, "nn", BF16, tiles=(2048, 1024, CW), b_rot=Z_ROT, deps=[pair_in[4]])

    def f_dnorm1(dh, xv, dx1v, g, sc):
        dh = dh.astype(F32)
        r, xh = _rms_stats(xv)
        dxh = dh * (1.0 + sc) * g
        return ([dx1v + _rms_bwd(dxh, xh, r)],
                [_colsum(dh * xh * g), _colsum(dh), _colsum(dh * (1.0 + sc) * xh)])

    grad_x, d_sc1, d_sh1, d_n1g = rowwise(
        "dnorm1", f_dnorm1, [dh1, xs, dx1], [n1g, sc1], [(D, F32)], [D, D, D], 512)
    late_local = [d_sh1.reshape(-1), d_sc1.reshape(-1), d_n1g.reshape(-1)]
    late_pack = _pad_rows(jnp.concatenate(late_local), 16 * LANE).reshape(-1, LANE)
    late_go = exchange_start("gather_late_start", gather_plan, NDEV - 1, [late_pack],
                             [((NDEV,) + late_pack.shape, F32)])
    chip_in = chip_go("in", ("w_in",), pair_in, late_go[4])

    weights = {
        "w_ada": (w_ada, m_w_ada, v_w_ada), "b_ada": (b_ada, m_b_ada, v_b_ada), "norm1_g": (norm1_g, m_norm1_g, v_norm1_g),
        "w_in": (w_in, m_w_in, v_w_in), "conv_w": (conv_w, m_conv_w, v_conv_w), "conv_b": (conv_b, m_conv_b, v_conv_b),
        "conv_ln_g": (conv_ln_g, m_conv_ln_g, v_conv_ln_g), "conv_ln_b": (conv_ln_b, m_conv_ln_b, v_conv_ln_b),
        "conv_proj": (conv_proj, m_conv_proj, v_conv_proj), "ssm_a_re": (ssm_a_re, m_ssm_a_re, v_ssm_a_re),
        "ssm_a_im": (ssm_a_im, m_ssm_a_im, v_ssm_a_im), "ssm_b_re": (ssm_b_re, m_ssm_b_re, v_ssm_b_re),
        "ssm_b_im": (ssm_b_im, m_ssm_b_im, v_ssm_b_im), "ssm_c_re": (ssm_c_re, m_ssm_c_re, v_ssm_c_re),
        "ssm_c_im": (ssm_c_im, m_ssm_c_im, v_ssm_c_im), "ssm_d": (ssm_d, m_ssm_d, v_ssm_d),
        "ssm_log_dt": (ssm_log_dt, m_ssm_log_dt, v_ssm_log_dt), "ssm_glu": (ssm_glu, m_ssm_glu, v_ssm_glu),
        "w_out": (w_out, m_w_out, v_w_out), "norm2_g": (norm2_g, m_norm2_g, v_norm2_g),
        "w_ffn_in": (w_ffn_in, m_w_ffn_in, v_w_ffn_in), "w_ffn_out": (w_ffn_out, m_w_ffn_out, v_w_ffn_out),
        "final_g": (final_g, m_final_g, v_final_g),
    }
    order = list(weights)
    big = ("w_ada", "w_in", "conv_proj", "ssm_glu", "w_out", "w_ffn_in", "w_ffn_out")
    grads, delta, new_m, new_v = {}, {}, {}, {}

    def adam_big(n, g2d, transposed=False):
        wv, mv, vv = weights[n]
        shp = wv.shape
        t_in = (lambda a: a.reshape(shp[-2:]).T) if transposed else (lambda a: a.reshape(shp[-2:]))
        t_out = (lambda a: a.T.reshape(shp)) if transposed else (lambda a: a.reshape(shp))
        w2 = t_in(wv)
        if isinstance(g2d, tuple):
            partial, recv = [p.reshape((p.shape[0],) + w2.shape) for p in g2d]
            g2d, d_, m_, v_ = chip_sum_adam("adam_" + n, partial, recv, w2, t_in(mv), t_in(vv))
        else:
            d_, m_, v_ = adam("adam_" + n, w2, g2d, t_in(mv), t_in(vv))
        grads[n], delta[n], new_m[n], new_v[n] = t_out(g2d), t_out(d_), t_out(m_), t_out(v_)
        return d_

    parts_ffn_out, parts_ffn_in = chip_done("ffn", chip_ffn, chip_in[4])
    adam_big("w_ffn_out", parts_ffn_out)
    last = adam_big("w_ffn_in", parts_ffn_in, transposed=True)
    parts_out, parts_conv_proj, parts_ssm_glu = chip_done("mid", chip_mid, last)
    adam_big("w_out", parts_out)
    adam_big("conv_proj", parts_conv_proj, transposed=True)
    last_mid = adam_big("ssm_glu", parts_ssm_glu, transposed=True)

    _, (late_all,) = exchange_wait("gather_late_wait", gather_plan, late_go, last_mid, place_own=True)
    _, (small_all,) = exchange_wait("gather_small_wait", gather_plan, small_go, late_all, place_own=True)

    def unpack(vec, sizes):
        out, pos = [], 0
        for n in sizes:
            out.append(vec[pos:pos + n])
            pos += n
        return out

    g_sh1, g_sc1, g_n1g = unpack(sum_slots("sum_small_late", late_all).reshape(-1), [D, D, D])
    (g_mod_rest, g_cw_full, g_cb, g_lng, g_lnb, g_a_re, g_a_im, g_b_re, g_b_im, g_c_re, g_c_im, g_d, g_ldt,
     g_n2g, g_fg, loss_sum) = unpack(sum_slots("sum_small", small_all).reshape(-1), small_sizes)
    g_b_ada = jnp.concatenate([g_sh1, g_sc1, g_mod_rest])
    loss = loss_sum[0]
    dmod_all = jnp.concatenate([late_all.reshape(NDEV, -1)[:, 0:2 * D], small_all.reshape(NDEV, -1)[:, 0:4 * D]],
                               axis=1)
    g_w_ada = ada_grad(act_all, lax.dynamic_slice_in_dim(dmod_all, me * ncol, ncol, axis=1))
    ccol = conv_w.shape[2]
    g_conv_w = lax.dynamic_slice_in_dim(g_cw_full.reshape(KC, CW), me * ccol, ccol, axis=1)

    adam_big("w_ada", g_w_ada)
    grads.update({
        "b_ada": g_b_ada.reshape(b_ada.shape), "norm1_g": g_n1g.reshape(norm1_g.shape),
        "conv_w": g_conv_w[None], "conv_b": g_cb.reshape(conv_b.shape),
        "conv_ln_g": g_lng.reshape(conv_ln_g.shape), "conv_ln_b": g_lnb.reshape(conv_ln_b.shape),
        "ssm_a_re": g_a_re.reshape(ssm_a_re.shape),
        "ssm_a_im": g_a_im.reshape(ssm_a_im.shape), "ssm_b_re": g_b_re.reshape(ssm_b_re.shape),
        "ssm_b_im": g_b_im.reshape(ssm_b_im.shape), "ssm_c_re": g_c_re.reshape(ssm_c_re.shape),
        "ssm_c_im": g_c_im.reshape(ssm_c_im.shape), "ssm_d": g_d.reshape(ssm_d.shape),
        "ssm_log_dt": g_ldt.reshape(ssm_log_dt.shape),
        "norm2_g": g_n2g.reshape(norm2_g.shape),
        "final_g": g_fg.reshape(final_g.shape),
    })
    small = [n for n in order if n not in big]
    def rows(a):
        if a.ndim == 4 and a.shape[-1] < a.shape[-2]:
            a = a.swapaxes(-1, -2)
        return a.reshape(1, -1) if a.ndim == 1 else a.reshape(-1, a.shape[-1])

    def unrows(a, shp):
        if len(shp) == 4 and shp[-1] < shp[-2]:
            return a.reshape(shp[:-2] + (shp[-1], shp[-2])).swapaxes(-1, -2)
        return a.reshape(shp)

    small_out = adam_many("adam_small", [rows(weights[n][0]) for n in small], [rows(grads[n]) for n in small],
                          [rows(weights[n][1]) for n in small], [rows(weights[n][2]) for n in small])
    for q, n in enumerate(small):
        shp = weights[n][0].shape
        delta[n], new_m[n], new_v[n] = [unrows(small_out[t * len(small) + q], shp) for t in range(3)]

    (parts_in,) = chip_done("in", chip_in, small_out[0])
    adam_big("w_in", parts_in, transposed=True)

    return (loss, grad_x[None], *[grads[n] for n in order], *[delta[n] for n in order],
            *[new_m[n] for n in order], *[new_v[n] for n in order])
```
